```python
import math
import jax, jax.numpy as jnp
from jax import lax
import numpy as np

D_MODEL = 1024
BATCH = 16
SEQ = 2048
DEPTH = 4

HEAD_DIM = 64
N_BRANCHES = 3
LRU_W = D_MODEL
LRU_BLOCKS = D_MODEL // HEAD_DIM
LRU_BLOCK_W = LRU_W // LRU_BLOCKS
CONV_WIDTH = 4
LRU_C = 8.0
SWA_HEADS = D_MODEL // HEAD_DIM
SWA_KV_HEADS = SWA_HEADS // 4
SWA_GROUP = SWA_HEADS // SWA_KV_HEADS
SWA_WINDOW = 128
DIL_HEADS = D_MODEL // HEAD_DIM
DIL_CONFIGS = ((128, 1), (512, 4), (2048, 16))
FF_HIDDEN = int(math.ceil(8 * D_MODEL / 3 / 256) * 256)
DEEPNORM_ALPHA = (2.0 * DEPTH) ** 0.25
DEEPNORM_BETA = (8.0 * DEPTH) ** -0.25
LN_EPS = 1e-5
NEG_INF = -1e30
IN_WIDTHS = (LRU_W, LRU_W,
             SWA_HEADS * HEAD_DIM, SWA_KV_HEADS * HEAD_DIM, SWA_KV_HEADS * HEAD_DIM,
             DIL_HEADS * HEAD_DIM, DIL_HEADS * HEAD_DIM, DIL_HEADS * HEAD_DIM,
             N_BRANCHES * D_MODEL)
IN_WIDTH = sum(IN_WIDTHS)
BRANCH_W = D_MODEL

kernel_name = "hybrid_rglru_swa_sink_dilated_deepnorm"


def layer_norm(x, g, b):
    xf = x.astype(jnp.float32)
    mu = jnp.mean(xf, axis=-1, keepdims=True)
    var = jnp.mean(jnp.square(xf - mu), axis=-1, keepdims=True)
    y = (xf - mu) * lax.rsqrt(var + LN_EPS)
    return (y * g.astype(jnp.float32) + b.astype(jnp.float32)).astype(x.dtype)


def banded_window_attention(q, k, v, window, sink=None):
    bsz, L, hk, g, dh = q.shape
    blk = window
    nb = -(-L // blk)
    pad = nb * blk - L
    if pad:
        q = jnp.pad(q, ((0, 0), (0, pad), (0, 0), (0, 0), (0, 0)))
        k = jnp.pad(k, ((0, 0), (0, pad), (0, 0), (0, 0)))
        v = jnp.pad(v, ((0, 0), (0, pad), (0, 0), (0, 0)))
    qb = q.reshape(bsz, nb, blk, hk, g, dh)
    kb = k.reshape(bsz, nb, blk, hk, dh)
    vb = v.reshape(bsz, nb, blk, hk, dh)

    def with_prev(t):
        prev = jnp.pad(t[:, :-1], ((0, 0), (1, 0), (0, 0), (0, 0), (0, 0)))
        return jnp.concatenate([prev, t], axis=2)

    kk = with_prev(kb)
    vv = with_prev(vb)
    s = jnp.einsum('bnqhgd,bnkhd->bhgnqk', qb, kk,
                   preferred_element_type=jnp.float32) * (dh ** -0.5)
    qi = jnp.arange(blk)[:, None]
    kj = jnp.arange(2 * blk)[None, :]
    rel = qi + blk - kj
    key_exists = (jnp.arange(nb)[:, None, None] > 0) | (kj[None] >= blk)
    mask = (rel >= 0)[None] & (rel <= window)[None] & key_exists
    s = jnp.where(mask, s, NEG_INF)
    m = jnp.max(s, axis=-1)
    if sink is not None:
        sink_b = sink.astype(jnp.float32).reshape(hk, g, 1, 1)
        m = jnp.maximum(m, sink_b)
    p = jnp.exp(s - m[..., None])
    denom = jnp.sum(p, axis=-1)
    if sink is not None:
        denom = denom + jnp.exp(sink_b - m)
    o = jnp.einsum('bhgnqk,bnkhd->bnqhgd', p.astype(vv.dtype), vv,
                   preferred_element_type=jnp.float32)
    den_t = jnp.transpose(denom, (0, 3, 4, 1, 2))
    lse_t = jnp.transpose(m + jnp.log(denom), (0, 3, 4, 1, 2))
    o = (o / den_t[..., None]).reshape(bsz, nb * blk, hk, g, dh)[:, :L]
    lse = lse_t.reshape(bsz, nb * blk, hk, g)[:, :L]
    return o.astype(q.dtype), lse


def dilated_attention(q, k, v):
    bsz, S, H, dh = q.shape
    outs, lses = [], []
    for window, dil in DIL_CONFIGS:
        Ls = S // dil

        def to_sub(t):
            return t.reshape(bsz, Ls, dil, H, dh).transpose(0, 2, 1, 3, 4).reshape(bsz * dil, Ls, H, dh)

        o, lse = banded_window_attention(to_sub(q)[:, :, :, None], to_sub(k), to_sub(v), window // dil)
        o = o[:, :, :, 0].reshape(bsz, dil, Ls, H, dh).transpose(0, 2, 1, 3, 4).reshape(bsz, S, H, dh)
        lse = lse[:, :, :, 0].reshape(bsz, dil, Ls, H).transpose(0, 2, 1, 3).reshape(bsz, S, H)
        outs.append(o)
        lses.append(lse)
    w = jax.nn.softmax(jnp.stack(lses, axis=0), axis=0)
    o = jnp.sum(w[..., None] * jnp.stack(outs, axis=0).astype(jnp.float32), axis=0)
    return o.astype(q.dtype)


def rglru_branch(xr, gate_in, conv_w, conv_b, w_rg, b_rg, w_ig, b_ig, lru_lambda):
    bsz, S, W = xr.shape
    xp = jnp.pad(xr, ((0, 0), (CONV_WIDTH - 1, 0), (0, 0)))
    xc = sum(xp[:, j:j + S] * conv_w[j] for j in range(CONV_WIDTH)) + conv_b
    xh = xc.reshape(bsz, S, LRU_BLOCKS, LRU_BLOCK_W)
    r = jax.nn.sigmoid(jnp.einsum('bshi,hij->bshj', xh, w_rg).reshape(bsz, S, W) + b_rg)
    i = jax.nn.sigmoid(jnp.einsum('bshi,hij->bshj', xh, w_ig).reshape(bsz, S, W) + b_ig)
    log_a = -LRU_C * r.astype(jnp.float32) * jax.nn.softplus(-lru_lambda.astype(jnp.float32))
    a = jnp.exp(log_a)
    mult = jnp.sqrt(-jnp.expm1(2.0 * log_a))
    b = mult * (i * xc).astype(jnp.float32)

    def combine(e1, e2):
        a1, b1 = e1
        a2, b2 = e2
        return a1 * a2, a2 * b1 + b2

    _, h = lax.associative_scan(combine, (a, b), axis=1)
    return (h.astype(xr.dtype) * jax.nn.gelu(gate_in))


def hybrid_mixer(x, w_in, conv_w, conv_b, w_rg, b_rg, w_ig, b_ig, lru_lambda, sinks, w_branch, w_out):
    bsz, S, D = x.shape
    proj = x @ w_in
    split_at = list(np.cumsum(IN_WIDTHS)[:-1])
    lru_x, lru_gate, qb, kb, vb, qc, kc, vc, gates = jnp.split(proj, split_at, axis=-1)
    y_a = rglru_branch(lru_x, lru_gate, conv_w, conv_b, w_rg, b_rg, w_ig, b_ig, lru_lambda)
    o_b, _ = banded_window_attention(qb.reshape(bsz, S, SWA_KV_HEADS, SWA_GROUP, HEAD_DIM),
                                     kb.reshape(bsz, S, SWA_KV_HEADS, HEAD_DIM),
                                     vb.reshape(bsz, S, SWA_KV_HEADS, HEAD_DIM),
                                     SWA_WINDOW, sinks)
    y_b = o_b.reshape(bsz, S, SWA_HEADS * HEAD_DIM)
    y_c = dilated_attention(qc.reshape(bsz, S, DIL_HEADS, HEAD_DIM),
                            kc.reshape(bsz, S, DIL_HEADS, HEAD_DIM),
                            vc.reshape(bsz, S, DIL_HEADS, HEAD_DIM)).reshape(bsz, S, DIL_HEADS * HEAD_DIM)
    ys = jnp.stack([y_a, y_b, y_c], axis=2)
    branch = jnp.einsum('bsnc,ncd->bsnd', ys, w_branch)
    merged = jnp.sum(jax.nn.sigmoid(gates.reshape(bsz, S, N_BRANCHES, D)) * branch, axis=2)
    return merged @ w_out


def swiglu(x, w_ffn_in, w_ffn_out):
    h1, h3 = jnp.split(x @ w_ffn_in, 2, axis=-1)
    return (jax.nn.silu(h1) * h3) @ w_ffn_out


def _fwd_setup_inputs(seed: int = 0) -> dict:
    key = jax.random.key(seed)
    ks = jax.random.split(key, 20)
    f32 = jnp.float32

    def nrm(k, shape, scale):
        return jax.random.normal(k, shape, f32) * scale

    u = jax.random.uniform(ks[7], (DEPTH, LRU_W), f32, 0.9, 0.999)
    return {
        "x": jax.random.normal(ks[0], (BATCH, SEQ, D_MODEL), f32),
        "w_in": nrm(ks[1], (DEPTH, D_MODEL, IN_WIDTH), D_MODEL ** -0.5),
        "conv_w": nrm(ks[2], (DEPTH, CONV_WIDTH, LRU_W), CONV_WIDTH ** -0.5),
        "conv_b": nrm(ks[3], (DEPTH, LRU_W), 0.02),
        "w_rg": nrm(ks[4], (DEPTH, LRU_BLOCKS, LRU_BLOCK_W, LRU_BLOCK_W), LRU_BLOCK_W ** -0.5),
        "b_rg": nrm(ks[5], (DEPTH, LRU_W), 0.02),
        "w_ig": nrm(ks[6], (DEPTH, LRU_BLOCKS, LRU_BLOCK_W, LRU_BLOCK_W), LRU_BLOCK_W ** -0.5),
        "b_ig": nrm(ks[8], (DEPTH, LRU_W), 0.02),
        "lru_lambda": jnp.log(u / (1.0 - u)),
        "sinks": nrm(ks[9], (DEPTH, SWA_HEADS), 0.5),
        "w_branch": nrm(ks[10], (DEPTH, N_BRANCHES, BRANCH_W, D_MODEL), DEEPNORM_BETA * BRANCH_W ** -0.5),
        "w_out": nrm(ks[11], (DEPTH, D_MODEL, D_MODEL), DEEPNORM_BETA * D_MODEL ** -0.5),
        "ln1_g": 1.0 + nrm(ks[12], (DEPTH, D_MODEL), 0.02),
        "ln1_b": nrm(ks[13], (DEPTH, D_MODEL), 0.02),
        "w_ffn_in": nrm(ks[14], (DEPTH, D_MODEL, 2 * FF_HIDDEN), DEEPNORM_BETA * D_MODEL ** -0.5),
        "w_ffn_out": nrm(ks[15], (DEPTH, FF_HIDDEN, D_MODEL), DEEPNORM_BETA * FF_HIDDEN ** -0.5),
        "ln2_g": 1.0 + nrm(ks[16], (DEPTH, D_MODEL), 0.02),
        "ln2_b": nrm(ks[17], (DEPTH, D_MODEL), 0.02),
    }


def _fwd_reference(x, w_in, conv_w, conv_b, w_rg, b_rg, w_ig, b_ig, lru_lambda, sinks, w_branch, w_out,
              ln1_g, ln1_b, w_ffn_in, w_ffn_out, ln2_g, ln2_b):
    for l in range(DEPTH):
        mix = hybrid_mixer(x, w_in[l], conv_w[l], conv_b[l], w_rg[l], b_rg[l], w_ig[l], b_ig[l],
                           lru_lambda[l], sinks[l], w_branch[l], w_out[l])
        x = layer_norm(DEEPNORM_ALPHA * x + mix, ln1_g[l], ln1_b[l])
        ffn = swiglu(x, w_ffn_in[l], w_ffn_out[l])
        x = layer_norm(DEEPNORM_ALPHA * x + ffn, ln2_g[l], ln2_b[l])
    return x


import jax as _jax
import jax.numpy as _jnp

TWIN_FORMAT = 'train_step'
FWD_PARAMS = ['x', 'w_in', 'conv_w', 'conv_b', 'w_rg', 'b_rg', 'w_ig', 'b_ig', 'lru_lambda', 'sinks', 'w_branch', 'w_out', 'ln1_g', 'ln1_b', 'w_ffn_in', 'w_ffn_out', 'ln2_g', 'ln2_b']
TWIN_WEIGHTS = ['w_in', 'conv_w', 'conv_b', 'w_rg', 'b_rg', 'w_ig', 'b_ig', 'lru_lambda', 'sinks', 'w_branch', 'w_out', 'ln1_g', 'ln1_b', 'w_ffn_in', 'w_ffn_out', 'ln2_g', 'ln2_b']
TWIN_DIFF_INPUT = 'x'
TWIN_INPUTS = ['x', 'w_in', 'conv_w', 'conv_b', 'w_rg', 'b_rg', 'w_ig', 'b_ig', 'lru_lambda', 'sinks', 'w_branch', 'w_out', 'ln1_g', 'ln1_b', 'w_ffn_in', 'w_ffn_out', 'ln2_g', 'ln2_b', 'loss_target', 'm_w_in', 'm_conv_w', 'm_conv_b', 'm_w_rg', 'm_b_rg', 'm_w_ig', 'm_b_ig', 'm_lru_lambda', 'm_sinks', 'm_w_branch', 'm_w_out', 'm_ln1_g', 'm_ln1_b', 'm_w_ffn_in', 'm_w_ffn_out', 'm_ln2_g', 'm_ln2_b', 'v_w_in', 'v_conv_w', 'v_conv_b', 'v_w_rg', 'v_b_rg', 'v_w_ig', 'v_b_ig', 'v_lru_lambda', 'v_sinks', 'v_w_branch', 'v_w_out', 'v_ln1_g', 'v_ln1_b', 'v_w_ffn_in', 'v_w_ffn_out', 'v_ln2_g', 'v_ln2_b']
TWIN_OUTPUTS = ['loss', 'grad_x', 'grad_w_in', 'grad_conv_w', 'grad_conv_b', 'grad_w_rg', 'grad_b_rg', 'grad_w_ig', 'grad_b_ig', 'grad_lru_lambda', 'grad_sinks', 'grad_w_branch', 'grad_w_out', 'grad_ln1_g', 'grad_ln1_b', 'grad_w_ffn_in', 'grad_w_ffn_out', 'grad_ln2_g', 'grad_ln2_b', 'delta_w_in', 'delta_conv_w', 'delta_conv_b', 'delta_w_rg', 'delta_b_rg', 'delta_w_ig', 'delta_b_ig', 'delta_lru_lambda', 'delta_sinks', 'delta_w_branch', 'delta_w_out', 'delta_ln1_g', 'delta_ln1_b', 'delta_w_ffn_in', 'delta_w_ffn_out', 'delta_ln2_g', 'delta_ln2_b', 'new_m_w_in', 'new_m_conv_w', 'new_m_conv_b', 'new_m_w_rg', 'new_m_b_rg', 'new_m_w_ig', 'new_m_b_ig', 'new_m_lru_lambda', 'new_m_sinks', 'new_m_w_branch', 'new_m_w_out', 'new_m_ln1_g', 'new_m_ln1_b', 'new_m_w_ffn_in', 'new_m_w_ffn_out', 'new_m_ln2_g', 'new_m_ln2_b', 'new_v_w_in', 'new_v_conv_w', 'new_v_conv_b', 'new_v_w_rg', 'new_v_b_rg', 'new_v_w_ig', 'new_v_b_ig', 'new_v_lru_lambda', 'new_v_sinks', 'new_v_w_branch', 'new_v_w_out', 'new_v_ln1_g', 'new_v_ln1_b', 'new_v_w_ffn_in', 'new_v_w_ffn_out', 'new_v_ln2_g', 'new_v_ln2_b']
TWIN_LEAF_KINDS = {'loss': 'loss', 'grad_x': 'grad_x', 'grad_w_in': 'grad_w', 'grad_conv_w': 'grad_w', 'grad_conv_b': 'grad_w', 'grad_w_rg': 'grad_w', 'grad_b_rg': 'grad_w', 'grad_w_ig': 'grad_w', 'grad_b_ig': 'grad_w', 'grad_lru_lambda': 'grad_w', 'grad_sinks': 'grad_w', 'grad_w_branch': 'grad_w', 'grad_w_out': 'grad_w', 'grad_ln1_g': 'grad_w', 'grad_ln1_b': 'grad_w', 'grad_w_ffn_in': 'grad_w', 'grad_w_ffn_out': 'grad_w', 'grad_ln2_g': 'grad_w', 'grad_ln2_b': 'grad_w', 'delta_w_in': 'delta_w', 'delta_conv_w': 'delta_w', 'delta_conv_b': 'delta_w', 'delta_w_rg': 'delta_w', 'delta_b_rg': 'delta_w', 'delta_w_ig': 'delta_w', 'delta_b_ig': 'delta_w', 'delta_lru_lambda': 'delta_w', 'delta_sinks': 'delta_w', 'delta_w_branch': 'delta_w', 'delta_w_out': 'delta_w', 'delta_ln1_g': 'delta_w', 'delta_ln1_b': 'delta_w', 'delta_w_ffn_in': 'delta_w', 'delta_w_ffn_out': 'delta_w', 'delta_ln2_g': 'delta_w', 'delta_ln2_b': 'delta_w', 'new_m_w_in': 'new_m', 'new_m_conv_w': 'new_m', 'new_m_conv_b': 'new_m', 'new_m_w_rg': 'new_m', 'new_m_b_rg': 'new_m', 'new_m_w_ig': 'new_m', 'new_m_b_ig': 'new_m', 'new_m_lru_lambda': 'new_m', 'new_m_sinks': 'new_m', 'new_m_w_branch': 'new_m', 'new_m_w_out': 'new_m', 'new_m_ln1_g': 'new_m', 'new_m_ln1_b': 'new_m', 'new_m_w_ffn_in': 'new_m', 'new_m_w_ffn_out': 'new_m', 'new_m_ln2_g': 'new_m', 'new_m_ln2_b': 'new_m', 'new_v_w_in': 'new_v', 'new_v_conv_w': 'new_v', 'new_v_conv_b': 'new_v', 'new_v_w_rg': 'new_v', 'new_v_b_rg': 'new_v', 'new_v_w_ig': 'new_v', 'new_v_b_ig': 'new_v', 'new_v_lru_lambda': 'new_v', 'new_v_sinks': 'new_v', 'new_v_w_branch': 'new_v', 'new_v_w_out': 'new_v', 'new_v_ln1_g': 'new_v', 'new_v_ln1_b': 'new_v', 'new_v_w_ffn_in': 'new_v', 'new_v_w_ffn_out': 'new_v', 'new_v_ln2_g': 'new_v', 'new_v_ln2_b': 'new_v'}


def _forward(args):
    return _fwd_reference(*[args[k] for k in FWD_PARAMS])


def _output_shape():
    out = _jax.eval_shape(lambda: _forward(_fwd_setup_inputs(0)))
    return out.shape, out.dtype

N_MICROBATCH = 1
ADAM_LR = 0.001
ADAM_B1 = 0.9
ADAM_B2 = 0.999
ADAM_EPS = 1e-08
ADAM_WD = 0.01
ADAM_STEP = 10
PER_EXAMPLE_BATCH_AXIS = {'x': 0, 'loss_target': 0}
SHARED_INPUTS = []
_WEIGHT_DTYPES = {'w_in': _jnp.float32, 'conv_w': _jnp.float32, 'conv_b': _jnp.float32, 'w_rg': _jnp.float32, 'b_rg': _jnp.float32, 'w_ig': _jnp.float32, 'b_ig': _jnp.float32, 'lru_lambda': _jnp.float32, 'sinks': _jnp.float32, 'w_branch': _jnp.float32, 'w_out': _jnp.float32, 'ln1_g': _jnp.float32, 'ln1_b': _jnp.float32, 'w_ffn_in': _jnp.float32, 'w_ffn_out': _jnp.float32, 'ln2_g': _jnp.float32, 'ln2_b': _jnp.float32}
MOMENT_SCALE = {'w_in': 2.467131e-03, 'conv_w': 4.094725e-03, 'conv_b': 2.209250e-02, 'w_rg': 1.075002e-03, 'b_rg': 9.842157e-04, 'w_ig': 1.902388e-03, 'b_ig': 1.439342e-03, 'lru_lambda': 1.973978e-03, 'sinks': 1.575415e-03, 'w_branch': 6.737515e-03, 'w_out': 1.165080e-02, 'ln1_g': 1.115419e+00, 'ln1_b': 4.696654e-01, 'w_ffn_in': 5.977357e-03, 'w_ffn_out': 9.833446e-03, 'ln2_g': 1.607714e+01, 'ln2_b': 8.135037e-01}


def _to_microbatches(a, axis):
    t = _jnp.moveaxis(a, axis, 0)
    t = t.reshape((N_MICROBATCH, t.shape[0] // N_MICROBATCH) + t.shape[1:])
    return _jnp.moveaxis(t, 1, axis + 1)


def setup_inputs(seed: int = 0) -> dict:
    inp = _fwd_setup_inputs(seed)
    key = _jax.random.fold_in(_jax.random.key(seed), 7919)
    shape, _ = _output_shape()
    out = dict(inp)
    out["loss_target"] = _jax.random.normal(_jax.random.fold_in(key, 0), shape, _jnp.float32)
    for i, name in enumerate(TWIN_WEIGHTS):
        w = inp[name].astype(_jnp.float32)
        if MOMENT_SCALE is None:
            s = _jnp.sqrt(_jnp.mean(_jnp.square(w)) + 1e-30)
        else:
            s = MOMENT_SCALE[name]
        km, kv = _jax.random.split(_jax.random.fold_in(key, i + 1))
        out[name] = w
        out["m_" + name] = s * _jax.random.normal(km, w.shape, _jnp.float32)
        out["v_" + name] = (s * s) * _jax.random.uniform(kv, w.shape, _jnp.float32, 0.5, 1.5)
    if N_MICROBATCH > 1:
        for name, axis in PER_EXAMPLE_BATCH_AXIS.items():
            out[name] = _to_microbatches(out[name], axis)
    return {'x': out['x'], 'w_in': out['w_in'], 'conv_w': out['conv_w'], 'conv_b': out['conv_b'], 'w_rg': out['w_rg'], 'b_rg': out['b_rg'], 'w_ig': out['w_ig'], 'b_ig': out['b_ig'], 'lru_lambda': out['lru_lambda'], 'sinks': out['sinks'], 'w_branch': out['w_branch'], 'w_out': out['w_out'], 'ln1_g': out['ln1_g'], 'ln1_b': out['ln1_b'], 'w_ffn_in': out['w_ffn_in'], 'w_ffn_out': out['w_ffn_out'], 'ln2_g': out['ln2_g'], 'ln2_b': out['ln2_b'], 'loss_target': out['loss_target'], 'm_w_in': out['m_w_in'], 'm_conv_w': out['m_conv_w'], 'm_conv_b': out['m_conv_b'], 'm_w_rg': out['m_w_rg'], 'm_b_rg': out['m_b_rg'], 'm_w_ig': out['m_w_ig'], 'm_b_ig': out['m_b_ig'], 'm_lru_lambda': out['m_lru_lambda'], 'm_sinks': out['m_sinks'], 'm_w_branch': out['m_w_branch'], 'm_w_out': out['m_w_out'], 'm_ln1_g': out['m_ln1_g'], 'm_ln1_b': out['m_ln1_b'], 'm_w_ffn_in': out['m_w_ffn_in'], 'm_w_ffn_out': out['m_w_ffn_out'], 'm_ln2_g': out['m_ln2_g'], 'm_ln2_b': out['m_ln2_b'], 'v_w_in': out['v_w_in'], 'v_conv_w': out['v_conv_w'], 'v_conv_b': out['v_conv_b'], 'v_w_rg': out['v_w_rg'], 'v_b_rg': out['v_b_rg'], 'v_w_ig': out['v_w_ig'], 'v_b_ig': out['v_b_ig'], 'v_lru_lambda': out['v_lru_lambda'], 'v_sinks': out['v_sinks'], 'v_w_branch': out['v_w_branch'], 'v_w_out': out['v_w_out'], 'v_ln1_g': out['v_ln1_g'], 'v_ln1_b': out['v_ln1_b'], 'v_w_ffn_in': out['v_w_ffn_in'], 'v_w_ffn_out': out['v_w_ffn_out'], 'v_ln2_g': out['v_ln2_g'], 'v_ln2_b': out['v_ln2_b']}


def _loss(weights, diff, rest, loss_target):
    with _jax.named_scope("forward"):
        args = {**rest, TWIN_DIFF_INPUT: diff, **{k: w.astype(_WEIGHT_DTYPES[k]) for k, w in weights.items()}}
        y = _forward(args)
    with _jax.named_scope("loss_head"):
        err = _jnp.square(y.astype(_jnp.float32) - loss_target)
        return 0.5 * _jnp.sum(_jnp.mean(err, axis=-1)) if err.ndim else 0.5 * err


def _adamw(w, g, m, v):
    m = ADAM_B1 * m + (1.0 - ADAM_B1) * g
    v = ADAM_B2 * v + (1.0 - ADAM_B2) * _jnp.square(g)
    m_hat = m / (1.0 - ADAM_B1 ** ADAM_STEP)
    v_hat = v / (1.0 - ADAM_B2 ** ADAM_STEP)
    delta = -ADAM_LR * (m_hat / (_jnp.sqrt(v_hat) + ADAM_EPS) + ADAM_WD * w)
    return delta, m, v


def reference(x, w_in, conv_w, conv_b, w_rg, b_rg, w_ig, b_ig, lru_lambda, sinks, w_branch, w_out, ln1_g, ln1_b, w_ffn_in, w_ffn_out, ln2_g, ln2_b, loss_target, m_w_in, m_conv_w, m_conv_b, m_w_rg, m_b_rg, m_w_ig, m_b_ig, m_lru_lambda, m_sinks, m_w_branch, m_w_out, m_ln1_g, m_ln1_b, m_w_ffn_in, m_w_ffn_out, m_ln2_g, m_ln2_b, v_w_in, v_conv_w, v_conv_b, v_w_rg, v_b_rg, v_w_ig, v_b_ig, v_lru_lambda, v_sinks, v_w_branch, v_w_out, v_ln1_g, v_ln1_b, v_w_ffn_in, v_w_ffn_out, v_ln2_g, v_ln2_b):
    given = dict(x=x, w_in=w_in, conv_w=conv_w, conv_b=conv_b, w_rg=w_rg, b_rg=b_rg, w_ig=w_ig, b_ig=b_ig, lru_lambda=lru_lambda, sinks=sinks, w_branch=w_branch, w_out=w_out, ln1_g=ln1_g, ln1_b=ln1_b, w_ffn_in=w_ffn_in, w_ffn_out=w_ffn_out, ln2_g=ln2_g, ln2_b=ln2_b, loss_target=loss_target, m_w_in=m_w_in, m_conv_w=m_conv_w, m_conv_b=m_conv_b, m_w_rg=m_w_rg, m_b_rg=m_b_rg, m_w_ig=m_w_ig, m_b_ig=m_b_ig, m_lru_lambda=m_lru_lambda, m_sinks=m_sinks, m_w_branch=m_w_branch, m_w_out=m_w_out, m_ln1_g=m_ln1_g, m_ln1_b=m_ln1_b, m_w_ffn_in=m_w_ffn_in, m_w_ffn_out=m_w_ffn_out, m_ln2_g=m_ln2_g, m_ln2_b=m_ln2_b, v_w_in=v_w_in, v_conv_w=v_conv_w, v_conv_b=v_conv_b, v_w_rg=v_w_rg, v_b_rg=v_b_rg, v_w_ig=v_w_ig, v_b_ig=v_b_ig, v_lru_lambda=v_lru_lambda, v_sinks=v_sinks, v_w_branch=v_w_branch, v_w_out=v_w_out, v_ln1_g=v_ln1_g, v_ln1_b=v_ln1_b, v_w_ffn_in=v_w_ffn_in, v_w_ffn_out=v_w_ffn_out, v_ln2_g=v_ln2_g, v_ln2_b=v_ln2_b)
    weights = {n: given[n] for n in TWIN_WEIGHTS}
    shared = {n: given[n] for n in SHARED_INPUTS}
    per_example = {n: given[n] for n in ['x']}
    grad_fn = _jax.value_and_grad(_loss, argnums=(0, 1))

    def one_microbatch(ex, loss_target):
        ex = dict(ex)
        diff = ex.pop(TWIN_DIFF_INPUT)
        return grad_fn(weights, diff, {**shared, **ex}, loss_target)

    if N_MICROBATCH == 1:
        loss, (grad_w, grad_x) = one_microbatch(per_example, given["loss_target"])
    else:
        def body(carry, xs):
            loss_sum, grad_sum = carry
            l_k, (gw_k, gx_k) = one_microbatch(xs[0], xs[1])
            with _jax.named_scope("update"):
                return (loss_sum + l_k, _jax.tree.map(_jnp.add, grad_sum, gw_k)), gx_k

        init = (_jnp.zeros((), _jnp.float32), _jax.tree.map(_jnp.zeros_like, weights))
        (loss, grad_w), grad_x = _jax.lax.scan(body, init, (per_example, given["loss_target"]))
    with _jax.named_scope("update"):
        delta_w, new_m, new_v = {}, {}, {}
        for n in TWIN_WEIGHTS:
            delta_w[n], new_m[n], new_v[n] = _adamw(weights[n], grad_w[n], given["m_" + n], given["v_" + n])
    return (loss, grad_x, *[grad_w[n] for n in TWIN_WEIGHTS], *[delta_w[n] for n in TWIN_WEIGHTS],
            *[new_m[n] for n in TWIN_WEIGHTS], *[new_v[n] for n in TWIN_WEIGHTS])
```

```python
import functools
import math

import jax
import jax.numpy as jnp
from jax import lax
from jax.experimental import pallas as pl
from jax.experimental.pallas import tpu as pltpu

HEAD_DIM = 64
WIN = 128
DILS = (1, 4, 16)
SWA_GROUP = 4
CONV_WIDTH = 4
LRU_C = 8.0
LN_EPS = 1e-5
NEG_INF = -1e30
N_CHIPS = 4
ADAM_LR, ADAM_B1, ADAM_B2, ADAM_EPS, ADAM_WD, ADAM_STEP = 0.001, 0.9, 0.999, 1e-08, 0.01, 10

LANES = 128
SUBLANES = 8
VMEM_LIMIT = 48 * 1024 * 1024

F32 = jnp.float32
BF16 = jnp.bfloat16
MESH = pl.DeviceIdType.MESH
ANY = pl.BlockSpec(memory_space=pl.ANY)


def _pcall(body, **kw):
    return pl.pallas_call(body, **kw)


def _pcall_comm(body, **kw):
    return pl.pallas_call(body, **kw)


def _params(*sem):
    return pltpu.CompilerParams(dimension_semantics=tuple(sem), vmem_limit_bytes=VMEM_LIMIT)


def _tile(dim, target):
    if dim <= target:
        return dim
    best = None
    for t in range(LANES, target + 1, LANES):
        if dim % t == 0:
            best = t
    assert best is not None, (dim, target)
    return best


def _sigmoid(x):
    return 1.0 / (1.0 + jnp.exp(-x))


def _dot(a, b, dims):
    return lax.dot_general(a, b, (dims, ((), ())), preferred_element_type=F32)


def _dot_nn(a, b):
    return _dot(a, b, ((1,), (0,)))


def _dot_nt(a, b):
    return _dot(a, b, ((1,), (1,)))


def _dot_tn(a, b):
    return _dot(a, b, ((0,), (0,)))


def _matmul(a, b, *, mode, name, out_dtype=F32, tm=512, tn=512, tk=2048, resid=None, rs=1.0, out_shards=0):
    b_sh = b.ndim == 3
    if mode == "nn":
        M, K = a.shape
        N = b.shape[-1] * (b.shape[0] if b_sh else 1)
    elif mode == "nt":
        M, K = a.shape
        N = b.shape[-2]
    else:
        K, M = a.shape
        N = b.shape[-1]
    tm = _tile(M, tm)
    if mode == "nn" and b_sh:
        tn = b.shape[-1]
    elif out_shards:
        tn = N // out_shards
    else:
        tn = _tile(N, tn)
    if mode == "nt" and b_sh:
        tk = b.shape[-1]
    else:
        tk = _tile(K, tk)
    nk = K // tk
    grid = (M // tm, N // tn, nk)

    if mode == "nn":
        a_spec = pl.BlockSpec((tm, tk), lambda i, j, k: (i, k))
        b_spec = (pl.BlockSpec((None, tk, tn), lambda i, j, k: (j, k, 0)) if b_sh
                  else pl.BlockSpec((tk, tn), lambda i, j, k: (k, j)))
        contract = _dot_nn
    elif mode == "nt":
        a_spec = pl.BlockSpec((tm, tk), lambda i, j, k: (i, k))
        b_spec = (pl.BlockSpec((None, tn, tk), lambda i, j, k: (k, j, 0)) if b_sh
                  else pl.BlockSpec((tn, tk), lambda i, j, k: (j, k)))
        contract = _dot_nt
    else:
        a_spec = pl.BlockSpec((tk, tm), lambda i, j, k: (k, i))
        b_spec = pl.BlockSpec((tk, tn), lambda i, j, k: (k, j))
        contract = _dot_tn
    if out_shards:
        out_shape = jax.ShapeDtypeStruct((out_shards, M, tn), out_dtype)
        o_spec = pl.BlockSpec((None, tm, tn), lambda i, j, k: (j, i, 0))
    else:
        out_shape = jax.ShapeDtypeStruct((M, N), out_dtype)
        o_spec = pl.BlockSpec((tm, tn), lambda i, j, k: (i, j))
    in_specs = [a_spec, b_spec]
    args = [a, b]
    if resid is not None:
        in_specs.append(pl.BlockSpec((tm, tn), lambda i, j, k: (i, j)))
        args.append(resid)

    def body(*refs):
        if resid is not None:
            a_ref, b_ref, r_ref, o_ref, acc_ref = refs
        else:
            a_ref, b_ref, o_ref, acc_ref = refs
        k = pl.program_id(2)
        part = contract(a_ref[...].astype(BF16), b_ref[...].astype(BF16))

        @pl.when(k == 0)
        def _():
            acc_ref[...] = part

        @pl.when(k > 0)
        def _():
            acc_ref[...] += part

        @pl.when(k == nk - 1)
        def _():
            res = acc_ref[...]
            if resid is not None:
                res = res + rs * r_ref[...]
            o_ref[...] = res.astype(out_dtype)

    return _pcall(
        body, name=name, grid=grid, in_specs=in_specs, out_specs=o_spec, out_shape=out_shape,
        scratch_shapes=[pltpu.VMEM((tm, tn), F32)],
        compiler_params=_params("parallel", "parallel", "arbitrary"),
    )(*args)


def _shift_down(x, d, row):
    return jnp.where(row >= d, pltpu.roll(x, d, 0), 0.0)


def _shift_up(x, d, row, n):
    return jnp.where(row < n - d, pltpu.roll(x, n - d, 0), 0.0)


def _log1p(u):
    w = 1.0 + u
    return jnp.where(w == 1.0, u, jnp.log(w) * u / (w - 1.0))


def _gelu_parts(g):
    k = math.sqrt(2.0 / math.pi)
    c = 0.044715
    t = jnp.tanh(k * (g + c * g * g * g))
    val = 0.5 * g * (1.0 + t)
    der = 0.5 * (1.0 + t) + 0.5 * g * (1.0 - t * t) * k * (1.0 + 3.0 * c * g * g)
    return val, der


def _lru_gates(xr, cw_ref, cb_ref, wrg_ref, brg_ref, wig_ref, big_ref, lam_ref, row):
    xc = cw_ref[3:4, :] * xr + cb_ref[...]
    for d in range(1, CONV_WIDTH):
        xc = xc + cw_ref[3 - d:4 - d, :] * _shift_down(xr, d, row)
    xcb = xc.astype(BF16)
    r = _sigmoid(_dot_nn(xcb, wrg_ref[...]) + brg_ref[...])
    ig = _sigmoid(_dot_nn(xcb, wig_ref[...]) + big_ref[...])
    lam = lam_ref[...]
    sp = jnp.maximum(-lam, 0.0) + _log1p(jnp.exp(-jnp.abs(lam)))
    log_a = (-LRU_C) * r * sp
    a = jnp.exp(log_a)
    y2 = 2.0 * log_a
    one_m_a2 = jnp.where(y2 > -0.01, -(y2 + 0.5 * y2 * y2 + (1.0 / 6.0) * y2 * y2 * y2), 1.0 - jnp.exp(y2))
    mult = jnp.sqrt(one_m_a2)
    return xc, r, ig, sp, a, mult


def _scan_local(a, b, row, n, reverse):
    sub = row % SUBLANES
    d = 1
    while d < SUBLANES:
        if reverse:
            keep = sub < SUBLANES - d
            a_s = jnp.where(keep, pltpu.roll(a, n - d, 0), 1.0)
            b_s = jnp.where(keep, pltpu.roll(b, n - d, 0), 0.0)
        else:
            keep = sub >= d
            a_s = jnp.where(keep, pltpu.roll(a, d, 0), 1.0)
            b_s = jnp.where(keep, pltpu.roll(b, d, 0), 0.0)
        b = a * b_s + b
        a = a * a_s
        d *= 2
    return a, b


def _scan_carry(a_ref, b_ref, out_ref, n, reverse):
    ng = n // SUBLANES

    def step(gidx, carry):
        g = (ng - 1 - gidx) if reverse else gidx
        rows = pl.ds(pl.multiple_of(g * SUBLANES, SUBLANES), SUBLANES)
        h = a_ref[rows, :] * carry + b_ref[rows, :]
        out_ref[rows, :] = h
        return h[0:1, :] if reverse else h[SUBLANES - 1:SUBLANES, :]

    lax.fori_loop(0, ng, step, jnp.zeros((1, LANES), F32), unroll=8)


def _lru_specs(B, S, D, C, x_off, g_off):
    nct = D // LANES
    seq = lambda off: pl.BlockSpec((None, S, LANES), lambda ct, b: (b, 0, off // LANES + ct))
    row = lambda r: pl.BlockSpec((r, LANES), lambda ct, b: (0, ct))
    wbd = pl.BlockSpec((None, LANES, LANES), lambda ct, b: (ct, 0, 0))
    return nct, seq, row, wbd


def _lru_fwd(proj3, lp, *, D, x_off, g_off):
    B, S, C = proj3.shape
    nct, seq, row, wbd = _lru_specs(B, S, D, C, x_off, g_off)

    def body(xr_ref, g_ref, cw_ref, cb_ref, wrg_ref, brg_ref, wig_ref, big_ref, lam_ref, h_ref, ya_ref, a_s, b_s):
        rowi = lax.broadcasted_iota(jnp.int32, (S, LANES), 0)
        xr = xr_ref[...]
        xc, r, ig, sp, a, mult = _lru_gates(xr, cw_ref, cb_ref, wrg_ref, brg_ref, wig_ref, big_ref, lam_ref, rowi)
        al, bl = _scan_local(a, mult * (ig * xc), rowi, S, False)
        a_s[...] = al
        b_s[...] = bl
        _scan_carry(a_s, b_s, h_ref, S, False)
        gel, _ = _gelu_parts(g_ref[...])
        ya_ref[...] = (h_ref[...] * gel).astype(BF16)

    out_seq = pl.BlockSpec((None, S, LANES), lambda ct, b: (b, 0, ct))
    return _pcall(
        body, name="lru_fwd", grid=(nct, B),
        in_specs=[seq(x_off), seq(g_off), row(CONV_WIDTH), row(1), wbd, row(1), wbd, row(1), row(1)],
        out_specs=[out_seq, out_seq],
        out_shape=[jax.ShapeDtypeStruct((B, S, D), F32), jax.ShapeDtypeStruct((B, S, D), BF16)],
        scratch_shapes=[pltpu.VMEM((S, LANES), F32), pltpu.VMEM((S, LANES), F32)],
        compiler_params=_params("parallel", "parallel"),
    )(proj3, proj3, lp["conv_w"], lp["conv_b"], lp["w_rg_bd"], lp["b_rg"], lp["w_ig_bd"], lp["b_ig"], lp["lam"])


def _lru_bwd(proj3, h3, dya3, lp, *, D, x_off, g_off):
    B, S, C = proj3.shape
    nct, seq, row, wbd = _lru_specs(B, S, D, C, x_off, g_off)

    def body(xr_ref, g_ref, h_ref, dy_ref, cw_ref, cb_ref, wrg_ref, brg_ref, wig_ref, big_ref, lam_ref,
             dxr_ref, dg_ref, dcw_ref, dcb_ref, dwrg_ref, dbrg_ref, dwig_ref, dbig_ref, dlam_ref, a_s, b_s, l_s):
        first = pl.program_id(1) == 0
        rowi = lax.broadcasted_iota(jnp.int32, (S, LANES), 0)
        xr = xr_ref[...]
        xc, r, ig, sp, a, mult = _lru_gates(xr, cw_ref, cb_ref, wrg_ref, brg_ref, wig_ref, big_ref, lam_ref, rowi)
        h = h_ref[...]
        dy = dy_ref[...]
        gel, dgel = _gelu_parts(g_ref[...])
        dg_ref[...] = (dy * h * dgel).astype(BF16)
        al, bl = _scan_local(_shift_up(a, 1, rowi, S), dy * gel, rowi, S, True)
        a_s[...] = al
        b_s[...] = bl
        _scan_carry(a_s, b_s, l_s, S, True)
        lamb = l_s[...]
        u = ig * xc
        da = lamb * _shift_down(h, 1, rowi)
        dlog_a = da * a - (lamb * u) * (a * a) / mult
        du = lamb * mult
        dpre_r = (dlog_a * ((-LRU_C) * sp)) * r * (1.0 - r)
        dpre_i = (du * xc) * ig * (1.0 - ig)
        dsp = jnp.sum(dlog_a * ((-LRU_C) * r), axis=0, keepdims=True)
        dlam = dsp * (-1.0 / (1.0 + jnp.exp(lam_ref[...])))
        dpr = dpre_r.astype(BF16)
        dpi = dpre_i.astype(BF16)
        dxc = du * ig + _dot_nt(dpr, wrg_ref[...]) + _dot_nt(dpi, wig_ref[...])
        xcb = xc.astype(BF16)
        dwrg = _dot_tn(xcb, dpr)
        dwig = _dot_tn(xcb, dpi)
        dxr = cw_ref[3:4, :] * dxc
        dcw = [jnp.sum(xr * dxc, axis=0, keepdims=True)]
        for d in range(1, CONV_WIDTH):
            dxr = dxr + cw_ref[3 - d:4 - d, :] * _shift_up(dxc, d, rowi, S)
            dcw.append(jnp.sum(_shift_down(xr, d, rowi) * dxc, axis=0, keepdims=True))
        dxr_ref[...] = dxr.astype(BF16)
        dcw_rows = jnp.concatenate(dcw[::-1], axis=0)
        sums = ((dcw_ref, dcw_rows), (dcb_ref, jnp.sum(dxc, axis=0, keepdims=True)), (dwrg_ref, dwrg),
                (dbrg_ref, jnp.sum(dpre_r, axis=0, keepdims=True)), (dwig_ref, dwig),
                (dbig_ref, jnp.sum(dpre_i, axis=0, keepdims=True)), (dlam_ref, dlam))

        @pl.when(first)
        def _():
            for ref, val in sums:
                ref[...] = val

        @pl.when(jnp.logical_not(first))
        def _():
            for ref, val in sums:
                ref[...] += val

    out_seq = pl.BlockSpec((None, S, LANES), lambda ct, b: (b, 0, ct))
    f = lambda shape: jax.ShapeDtypeStruct(shape, F32)
    nb = D // LANES
    return _pcall(
        body, name="lru_bwd", grid=(nct, B),
        in_specs=[seq(x_off), seq(g_off), out_seq, out_seq, row(CONV_WIDTH), row(1), wbd, row(1), wbd, row(1), row(1)],
        out_specs=[out_seq, out_seq, row(CONV_WIDTH), row(1), wbd, row(1), wbd, row(1), row(1)],
        out_shape=[jax.ShapeDtypeStruct((B, S, D), BF16), jax.ShapeDtypeStruct((B, S, D), BF16),
                   f((CONV_WIDTH, D)), f((1, D)), f((nb, LANES, LANES)), f((1, D)), f((nb, LANES, LANES)), f((1, D)), f((1, D))],
        scratch_shapes=[pltpu.VMEM((S, LANES), F32)] * 3,
        compiler_params=_params("parallel", "arbitrary"),
    )(proj3, proj3, h3, dya3, lp["conv_w"], lp["conv_b"], lp["w_rg_bd"], lp["b_rg"], lp["w_ig_bd"], lp["b_ig"], lp["lam"])


def _heads_per_block(C, offs_q, offs_kv, group, n_heads):
    for hb in (8, 4, 2):
        qw, kvw = hb * HEAD_DIM, hb * HEAD_DIM // group
        if hb <= n_heads and n_heads % hb == 0 and kvw % LANES == 0 and C % qw == 0 and C % kvw == 0 \
                and all(o % qw == 0 for o in offs_q) and all(o % kvw == 0 for o in offs_kv):
            return hb
    raise ValueError("no head blocking fits these offsets")


def _attn_masks(i):
    qi = lax.broadcasted_iota(jnp.int32, (WIN, WIN), 0)
    kj = lax.broadcasted_iota(jnp.int32, (WIN, WIN), 1)
    return jnp.logical_and(kj >= qi, i > 0), kj <= qi


def _attn_fwd(src3, *, D, q_off, k_off, v_off, group, dil, sinks=None, name):
    B, S, C = src3.shape
    H = D // HEAD_DIM
    hb = _heads_per_block(C, (q_off,), (k_off, v_off), group, H)
    nhb, qw, kvw = H // hb, hb * HEAD_DIM, hb * HEAD_DIM // group
    Ls = S // dil
    nblk = Ls // WIN
    scale = HEAD_DIM ** -0.5
    src = src3.reshape(B, Ls, dil * C)
    has_sink = sinks is not None

    def body(*refs):
        if has_sink:
            q_ref, kp_ref, kc_ref, vp_ref, vc_ref, sk_ref, o_ref, lse_ref = refs
        else:
            q_ref, kp_ref, kc_ref, vp_ref, vc_ref, o_ref, lse_ref = refs
        valid_p, valid_c = _attn_masks(pl.program_id(3))
        lane = lax.broadcasted_iota(jnp.int32, (WIN, LANES), 1)
        lse_acc = jnp.zeros((WIN, LANES), F32)
        for h in range(hb):
            g = h // group
            qs, ks = slice(h * HEAD_DIM, (h + 1) * HEAD_DIM), slice(g * HEAD_DIM, (g + 1) * HEAD_DIM)
            qh = q_ref[:, qs].astype(BF16)
            sp_ = jnp.where(valid_p, _dot_nt(qh, kp_ref[:, ks].astype(BF16)) * scale, NEG_INF)
            sc = jnp.where(valid_c, _dot_nt(qh, kc_ref[:, ks].astype(BF16)) * scale, NEG_INF)
            m = jnp.maximum(jnp.max(sp_, axis=1, keepdims=True), jnp.max(sc, axis=1, keepdims=True))
            if has_sink:
                sk = sk_ref[:, h:h + 1]
                m = jnp.maximum(m, sk)
            pp = jnp.exp(sp_ - m)
            pc = jnp.exp(sc - m)
            den = jnp.sum(pp, axis=1, keepdims=True) + jnp.sum(pc, axis=1, keepdims=True)
            if has_sink:
                den = den + jnp.exp(sk - m)
            oh = _dot_nn(pp.astype(BF16), vp_ref[:, ks].astype(BF16)) + _dot_nn(pc.astype(BF16), vc_ref[:, ks].astype(BF16))
            o_ref[:, qs] = oh / den
            lse_acc = jnp.where(lane == h, m + jnp.log(den), lse_acc)
        lse_ref[...] = lse_acc

    def blk(width, off, prev):
        def imap(b, r, hh, i):
            return (b, jnp.maximum(i - 1, 0) if prev else i, (r * C + off) // width + hh)
        return pl.BlockSpec((None, WIN, width), imap)

    in_specs = [blk(qw, q_off, False), blk(kvw, k_off, True), blk(kvw, k_off, False), blk(kvw, v_off, True), blk(kvw, v_off, False)]
    args = [src] * 5
    if has_sink:
        in_specs.append(pl.BlockSpec((None, 1, LANES), lambda b, r, hh, i: (hh, 0, 0)))
        args.append(sinks)
    o, lse = _pcall(
        body, name=name, grid=(B, dil, nhb, nblk), in_specs=in_specs,
        out_specs=[pl.BlockSpec((None, WIN, qw), lambda b, r, hh, i: (b, i, r * (D // qw) + hh)),
                   pl.BlockSpec((None, WIN, LANES), lambda b, r, hh, i: (b, i, r * nhb + hh))],
        out_shape=[jax.ShapeDtypeStruct((B, Ls, dil * D), F32), jax.ShapeDtypeStruct((B, Ls, dil * nhb * LANES), F32)],
        compiler_params=_params("parallel", "parallel", "parallel", "parallel"),
    )(*args)
    return o.reshape(B, S, D), lse.reshape(B, S, nhb * LANES)


def _attn_bwd(src3, o3, lse3, do3, *, D, q_off, k_off, v_off, group, dil, sinks=None, acc=None, out_dtype=F32, name):
    B, S, C = src3.shape
    H = D // HEAD_DIM
    hb = _heads_per_block(C, (q_off,), (k_off, v_off), group, H)
    nhb, qw, kvw = H // hb, hb * HEAD_DIM, hb * HEAD_DIM // group
    KV = D // group
    Ls = S // dil
    nblk = Ls // WIN
    scale = HEAD_DIM ** -0.5
    has_sink = sinks is not None
    has_acc = acc is not None

    def body(*refs):
        q_ref, kp_ref, kc_ref, vp_ref, vc_ref, o_ref, l_ref, do_ref = refs[:8]
        pos = 8
        if has_sink:
            sk_ref = refs[pos]
            pos += 1
        if has_acc:
            aq_ref, ak_ref, av_ref = refs[pos:pos + 3]
            pos += 3
        dq_ref, dk_ref, dv_ref = refs[pos:pos + 3]
        pos += 3
        if has_sink:
            dsk_ref = refs[pos]
            pos += 1
        ck_ref, cv_ref = refs[pos:pos + 2]
        i = pl.program_id(3)

        @pl.when(i == 0)
        def _():
            ck_ref[...] = jnp.zeros_like(ck_ref)
            cv_ref[...] = jnp.zeros_like(cv_ref)
            if has_sink:
                dsk_ref[...] = jnp.zeros_like(dsk_ref)

        @pl.when(i < nblk)
        def _():
            valid_p, valid_c = _attn_masks(i)
            lane = lax.broadcasted_iota(jnp.int32, (1, LANES), 1)
            dsk_acc = jnp.zeros((1, LANES), F32)
            for g in range(hb // group):
                ks = slice(g * HEAD_DIM, (g + 1) * HEAD_DIM)
                kp, kc = kp_ref[:, ks].astype(BF16), kc_ref[:, ks].astype(BF16)
                vp, vc = vp_ref[:, ks].astype(BF16), vc_ref[:, ks].astype(BF16)
                dkp = dkc = dvp = dvc = jnp.zeros((WIN, HEAD_DIM), F32)
                for h in range(g * group, (g + 1) * group):
                    qs = slice(h * HEAD_DIM, (h + 1) * HEAD_DIM)
                    qh = q_ref[:, qs].astype(BF16)
                    doh = do_ref[:, qs]
                    dd = jnp.sum(doh * o_ref[:, qs], axis=1, keepdims=True)
                    lh = l_ref[:, h:h + 1]
                    dob = doh.astype(BF16)
                    pp = jnp.exp(jnp.where(valid_p, _dot_nt(qh, kp) * scale, NEG_INF) - lh)
                    pc = jnp.exp(jnp.where(valid_c, _dot_nt(qh, kc) * scale, NEG_INF) - lh)
                    dsp = (pp * (_dot_nt(dob, vp) - dd) * scale).astype(BF16)
                    dsc = (pc * (_dot_nt(dob, vc) - dd) * scale).astype(BF16)
                    dqh = _dot_nn(dsp, kp) + _dot_nn(dsc, kc)
                    if has_acc:
                        dqh = dqh + aq_ref[:, qs]
                    dq_ref[:, qs] = dqh.astype(out_dtype)
                    dkp = dkp + _dot_tn(dsp, qh)
                    dkc = dkc + _dot_tn(dsc, qh)
                    dvp = dvp + _dot_tn(pp.astype(BF16), dob)
                    dvc = dvc + _dot_tn(pc.astype(BF16), dob)
                    if has_sink:
                        sk = sk_ref[:, h:h + 1]
                        dsk_h = -jnp.sum(jnp.exp(sk - lh) * dd, axis=0, keepdims=True)
                        dsk_acc = jnp.where(lane == h, dsk_h, dsk_acc)
                dkp = dkp + ck_ref[:, ks]
                dvp = dvp + cv_ref[:, ks]
                if has_acc:
                    dkp = dkp + ak_ref[:, ks]
                    dvp = dvp + av_ref[:, ks]
                dk_ref[:, ks] = dkp.astype(out_dtype)
                dv_ref[:, ks] = dvp.astype(out_dtype)
                ck_ref[:, ks] = dkc
                cv_ref[:, ks] = dvc
            if has_sink:
                dsk_ref[...] += dsk_acc

        @pl.when(i == nblk)
        def _():
            dk = ck_ref[...]
            dv = cv_ref[...]
            if has_acc:
                dk = dk + ak_ref[...]
                dv = dv + av_ref[...]
            dk_ref[...] = dk.astype(out_dtype)
            dv_ref[...] = dv.astype(out_dtype)

    last = nblk - 1

    def blk(width, off, back, row_c):
        def imap(b, r, hh, i):
            ii = jnp.minimum(i, last)
            if back == 1:
                ii = jnp.maximum(ii - 1, 0)
            elif back == 2:
                ii = jnp.maximum(i - 1, 0)
            return (b, ii, (r * row_c + off) // width + hh)
        return pl.BlockSpec((None, WIN, width), imap)

    src = src3.reshape(B, Ls, dil * C)
    view = lambda t: t.reshape(B, Ls, dil * t.shape[-1])
    in_specs = [blk(qw, q_off, 0, C), blk(kvw, k_off, 1, C), blk(kvw, k_off, 0, C), blk(kvw, v_off, 1, C), blk(kvw, v_off, 0, C),
                blk(qw, 0, 0, D), blk(LANES, 0, 0, nhb * LANES), blk(qw, 0, 0, D)]
    args = [src] * 5 + [view(o3), view(lse3), view(do3)]
    if has_sink:
        in_specs.append(pl.BlockSpec((None, 1, LANES), lambda b, r, hh, i: (hh, 0, 0)))
        args.append(sinks)
    if has_acc:
        in_specs += [blk(qw, 0, 0, D), blk(kvw, 0, 2, KV), blk(kvw, 0, 2, KV)]
        args += [view(t) for t in acc]
    out_specs = [blk(qw, 0, 0, D), blk(kvw, 0, 2, KV), blk(kvw, 0, 2, KV)]
    out_shape = [jax.ShapeDtypeStruct((B, Ls, dil * D), out_dtype), jax.ShapeDtypeStruct((B, Ls, dil * KV), out_dtype),
                 jax.ShapeDtypeStruct((B, Ls, dil * KV), out_dtype)]
    if has_sink:
        out_specs.append(pl.BlockSpec((None, None, 1, LANES), lambda b, r, hh, i: (b, hh, 0, 0)))
        out_shape.append(jax.ShapeDtypeStruct((B, nhb, 1, LANES), F32))
    res = _pcall(
        body, name=name, grid=(B, dil, nhb, nblk + 1), in_specs=in_specs, out_specs=out_specs, out_shape=out_shape,
        scratch_shapes=[pltpu.VMEM((WIN, kvw), F32), pltpu.VMEM((WIN, kvw), F32)],
        compiler_params=_params("parallel", "parallel", "parallel", "arbitrary"),
    )(*args)
    dq, dk, dv = res[0].reshape(B, S, D), res[1].reshape(B, S, KV), res[2].reshape(B, S, KV)
    return (dq, dk, dv, res[3]) if has_sink else (dq, dk, dv)


def _combine(o_list, lse_list, *, D, hb):
    T = o_list[0].shape[0]
    n = len(o_list)
    nhb = D // HEAD_DIM // hb
    tm = _tile(T, 256)

    def body(*refs):
        o_refs, l_refs, out_ref, lo_ref = refs[:n], refs[n:2 * n], refs[2 * n], refs[2 * n + 1]
        lane = lax.broadcasted_iota(jnp.int32, (tm, LANES), 1)
        for hh in range(nhb):
            l_acc = jnp.zeros((tm, LANES), F32)
            for h in range(hb):
                col = hh * LANES + h
                ls = [r[:, col:col + 1] for r in l_refs]
                m = functools.reduce(jnp.maximum, ls)
                es = [jnp.exp(l - m) for l in ls]
                tot = functools.reduce(lambda u, v: u + v, es)
                cs = slice((hh * hb + h) * HEAD_DIM, (hh * hb + h + 1) * HEAD_DIM)
                acc = (es[0] / tot) * o_refs[0][:, cs]
                for e, r in zip(es[1:], o_refs[1:]):
                    acc = acc + (e / tot) * r[:, cs]
                out_ref[:, cs] = acc
                l_acc = jnp.where(lane == h, m + jnp.log(tot), l_acc)
            lo_ref[:, hh * LANES:(hh + 1) * LANES] = l_acc

    so = pl.BlockSpec((tm, D), lambda i: (i, 0))
    sl = pl.BlockSpec((tm, nhb * LANES), lambda i: (i, 0))
    return _pcall(
        body, name="dil_combine", grid=(T // tm,), in_specs=[so] * n + [sl] * n, out_specs=[so, sl],
        out_shape=[jax.ShapeDtypeStruct((T, D), F32), jax.ShapeDtypeStruct((T, nhb * LANES), F32)],
        compiler_params=_params("parallel"),
    )(*o_list, *lse_list)


def _branch_fwd(ys, wb, proj, *, D, g_off):
    T = proj.shape[0]
    tm, tn = _tile(T, 256), _tile(D, 512)
    n = len(ys)

    def body(*refs):
        y_refs, w_ref, g_refs, br_ref, mg_ref = refs[:n], refs[n], refs[n + 1:2 * n + 1], refs[2 * n + 1], refs[2 * n + 2]
        acc = None
        for k in range(n):
            br = _dot_nn(y_refs[k][...].astype(BF16), w_ref[k])
            br_ref[k] = br
            term = _sigmoid(g_refs[k][...]) * br
            acc = term if acc is None else acc + term
        mg_ref[...] = acc.astype(BF16)

    gate = lambda k: pl.BlockSpec((tm, tn), lambda i, j: (i, (g_off + k * D) // tn + j))
    return _pcall(
        body, name="branch_fwd", grid=(T // tm, D // tn),
        in_specs=[pl.BlockSpec((tm, D), lambda i, j: (i, 0))] * n + [pl.BlockSpec((n, D, tn), lambda i, j: (0, 0, j))]
        + [gate(k) for k in range(n)],
        out_specs=[pl.BlockSpec((n, tm, tn), lambda i, j: (0, i, j)), pl.BlockSpec((tm, tn), lambda i, j: (i, j))],
        out_shape=[jax.ShapeDtypeStruct((n, T, D), F32), jax.ShapeDtypeStruct((T, D), BF16)],
        compiler_params=_params("parallel", "parallel"),
    )(*ys, wb, *([proj] * n))


def _branch_bwd(dmerged, branch, proj, *, D, g_off):
    n, T, _ = branch.shape
    tm, tn = _tile(T, 512), _tile(D, 512)

    def body(dm_ref, br_ref, *rest):
        g_refs, db_ref, dg_refs = rest[:n], rest[n], rest[n + 1:]
        dm = dm_ref[...]
        for k in range(n):
            sg = _sigmoid(g_refs[k][...])
            db_ref[k] = (sg * dm).astype(BF16)
            dg_refs[k][...] = (dm * br_ref[k] * sg * (1.0 - sg)).astype(BF16)

    gate = lambda k: pl.BlockSpec((tm, tn), lambda i, j: (i, (g_off + k * D) // tn + j))
    blk = pl.BlockSpec((tm, tn), lambda i, j: (i, j))
    res = _pcall(
        body, name="branch_bwd", grid=(T // tm, D // tn),
        in_specs=[blk, pl.BlockSpec((n, tm, tn), lambda i, j: (0, i, j))] + [gate(k) for k in range(n)],
        out_specs=[pl.BlockSpec((n, tm, tn), lambda i, j: (0, i, j))] + [blk] * n,
        out_shape=[jax.ShapeDtypeStruct((n, T, D), BF16)] + [jax.ShapeDtypeStruct((T, D), BF16)] * n,
        compiler_params=_params("parallel", "parallel"),
    )(dmerged, branch, *([proj] * n))
    return res[0], list(res[1:])


def _ln_fwd(xres, y, g, b, *, alpha):
    T, D = xres.shape
    tm = _tile(T, 512)

    def body(x_ref, y_ref, g_ref, b_ref, z_ref, o_ref):
        z = alpha * x_ref[...] + y_ref[...]
        mu = jnp.mean(z, axis=1, keepdims=True)
        zc = z - mu
        var = jnp.mean(zc * zc, axis=1, keepdims=True)
        z_ref[...] = z
        o_ref[...] = zc * lax.rsqrt(var + LN_EPS) * g_ref[...] + b_ref[...]

    blk = pl.BlockSpec((tm, D), lambda i: (i, 0))
    vec = pl.BlockSpec((1, D), lambda i: (0, 0))
    return _pcall(
        body, name="ln_fwd", grid=(T // tm,), in_specs=[blk, blk, vec, vec], out_specs=[blk, blk],
        out_shape=[jax.ShapeDtypeStruct((T, D), F32)] * 2, compiler_params=_params("parallel"),
    )(xres, y, g, b)


def _ln_bwd(dout, z, g):
    T, D = z.shape
    tm = _tile(T, 512)

    def body(do_ref, z_ref, g_ref, dz_ref, dg_ref, db_ref):
        z = z_ref[...]
        do = do_ref[...]
        mu = jnp.mean(z, axis=1, keepdims=True)
        zc = z - mu
        rstd = lax.rsqrt(jnp.mean(zc * zc, axis=1, keepdims=True) + LN_EPS)
        xhat = zc * rstd
        dxh = do * g_ref[...]
        dz_ref[...] = rstd * (dxh - jnp.mean(dxh, axis=1, keepdims=True) - xhat * jnp.mean(dxh * xhat, axis=1, keepdims=True))
        dg = jnp.sum(do * xhat, axis=0, keepdims=True)
        db = jnp.sum(do, axis=0, keepdims=True)
        first = pl.program_id(0) == 0

        @pl.when(first)
        def _():
            dg_ref[...] = dg
            db_ref[...] = db

        @pl.when(jnp.logical_not(first))
        def _():
            dg_ref[...] += dg
            db_ref[...] += db

    blk = pl.BlockSpec((tm, D), lambda i: (i, 0))
    vec = pl.BlockSpec((1, D), lambda i: (0, 0))
    return _pcall(
        body, name="ln_bwd", grid=(T // tm,), in_specs=[blk, blk, vec], out_specs=[blk, vec, vec],
        out_shape=[jax.ShapeDtypeStruct((T, D), F32), jax.ShapeDtypeStruct((1, D), F32), jax.ShapeDtypeStruct((1, D), F32)],
        compiler_params=_params("arbitrary"),
    )(dout, z, g)


def _swiglu_fwd(hh):
    T, F2 = hh.shape
    Fh = F2 // 2
    tm, tn = _tile(T, 512), _tile(Fh, 256)
    nj = Fh // tn

    def body(h1_ref, h3_ref, f_ref):
        h1 = h1_ref[...]
        f_ref[...] = (h1 * _sigmoid(h1) * h3_ref[...]).astype(BF16)

    return _pcall(
        body, name="swiglu_fwd", grid=(T // tm, nj),
        in_specs=[pl.BlockSpec((tm, tn), lambda i, j: (i, j)), pl.BlockSpec((tm, tn), lambda i, j: (i, nj + j))],
        out_specs=pl.BlockSpec((tm, tn), lambda i, j: (i, j)),
        out_shape=jax.ShapeDtypeStruct((T, Fh), BF16), compiler_params=_params("parallel", "parallel"),
    )(hh, hh)


def _swiglu_bwd(hh, df):
    T, F2 = hh.shape
    Fh = F2 // 2
    tm, tn = _tile(T, 512), _tile(Fh, 256)
    nj = Fh // tn

    def body(h1_ref, h3_ref, df_ref, d1_ref, d3_ref):
        h1 = h1_ref[...]
        sg = _sigmoid(h1)
        d = df_ref[...]
        d1_ref[...] = (d * h3_ref[...] * sg * (1.0 + h1 * (1.0 - sg))).astype(BF16)
        d3_ref[...] = (d * h1 * sg).astype(BF16)

    lo = pl.BlockSpec((tm, tn), lambda i, j: (i, j))
    hi = pl.BlockSpec((tm, tn), lambda i, j: (i, nj + j))
    d1, d3 = _pcall(
        body, name="swiglu_bwd", grid=(T // tm, nj), in_specs=[lo, hi, lo], out_specs=[lo, lo],
        out_shape=[jax.ShapeDtypeStruct((T, Fh), BF16)] * 2, compiler_params=_params("parallel", "parallel"),
    )(hh, hh, df)
    return jnp.concatenate([d1, d3], axis=1)


def _loss_head(y, target):
    T, D = y.shape
    tm = _tile(T, 512)

    def body(y_ref, t_ref, dy_ref, l_ref):
        e = y_ref[...] - t_ref[...]
        dy_ref[...] = e * (1.0 / D)
        sq = e * e
        part = sq[:, 0:LANES]
        for c in range(1, D // LANES):
            part = part + sq[:, c * LANES:(c + 1) * LANES]
        part = jnp.sum(part, axis=0, keepdims=True) * (0.5 / D)
        first = pl.program_id(0) == 0

        @pl.when(first)
        def _():
            l_ref[...] = part

        @pl.when(jnp.logical_not(first))
        def _():
            l_ref[...] += part

    blk = pl.BlockSpec((tm, D), lambda i: (i, 0))
    return _pcall(
        body, name="loss_head", grid=(T // tm,), in_specs=[blk, blk],
        out_specs=[blk, pl.BlockSpec((1, LANES), lambda i: (0, 0))],
        out_shape=[jax.ShapeDtypeStruct((T, D), F32), jax.ShapeDtypeStruct((1, LANES), F32)],
        compiler_params=_params("arbitrary"),
    )(y, target)


def _as_rows(a):
    n = a.size
    for cols in (2048, 1024, 512, 256, LANES):
        if n % cols == 0:
            return a.reshape(n // cols, cols)
    return a.reshape(1, n)


def _adamw(w, g, m, v):
    w2, g2, m2, v2 = (_as_rows(t) for t in (w, g, m, v))
    R, Cc = w2.shape
    tm = R if (R <= 128 or R % SUBLANES) else max(t for t in range(SUBLANES, 129, SUBLANES) if R % t == 0)
    c1 = 1.0 - ADAM_B1 ** ADAM_STEP
    c2 = 1.0 - ADAM_B2 ** ADAM_STEP

    def body(w_ref, g_ref, m_ref, v_ref, d_ref, nm_ref, nv_ref):
        gg = g_ref[...]
        nm = ADAM_B1 * m_ref[...] + (1.0 - ADAM_B1) * gg
        nv = ADAM_B2 * v_ref[...] + (1.0 - ADAM_B2) * (gg * gg)
        d_ref[...] = (-ADAM_LR) * ((nm / c1) / (jnp.sqrt(nv / c2) + ADAM_EPS) + ADAM_WD * w_ref[...])
        nm_ref[...] = nm
        nv_ref[...] = nv

    blk = pl.BlockSpec((tm, Cc), lambda i: (i, 0))
    res = _pcall(
        body, name="adamw", grid=(R // tm,), in_specs=[blk] * 4, out_specs=[blk] * 3,
        out_shape=[jax.ShapeDtypeStruct((R, Cc), F32)] * 3, compiler_params=_params("parallel"),
    )(w2, g2, m2, v2)
    return tuple(t.reshape(w.shape) for t in res)


def _where_am_i():
    x, y, c = lax.axis_index("x"), lax.axis_index("y"), lax.axis_index("c")
    chips = [(1 - x, y), (x, 1 - y), (1 - x, 1 - y)]
    return x, y, c, chips


def _remote(src, dst, send_sems, recv_sems, k, to):
    return pltpu.make_async_remote_copy(src_ref=src, dst_ref=dst, send_sem=send_sems.at[k], recv_sem=recv_sems.at[k],
                                        device_id=to, device_id_type=MESH)


def _comm_call(body, name, ins, out_shapes, n_remote, n_local):
    return _pcall_comm(
        body, name=name, in_specs=[ANY] * len(ins), out_specs=[ANY] * len(out_shapes), out_shape=out_shapes,
        scratch_shapes=[pltpu.SemaphoreType.DMA((n_remote,)), pltpu.SemaphoreType.DMA((n_remote,)),
                        pltpu.SemaphoreType.DMA((max(n_local, 1),))],
    )(*ins)


def _gather_weights(shards):
    n = len(shards)

    def body(*refs):
        ins, outs = refs[:n], refs[n:2 * n]
        send_sems, recv_sems, local_sems = refs[2 * n:]
        x, y, c, chips = _where_am_i()
        s = 2 * x + y
        sib = (x, y, 1 - c)
        mine = [pltpu.make_async_copy(ins[t], outs[t].at[:, s], local_sems.at[t]) for t in range(n)]
        for cp in mine:
            cp.start()
        first = []
        for t in range(n):
            for j, (cx, cy) in enumerate(chips):
                first.append(_remote(ins[t].at[:, c], outs[t].at[:, s, c], send_sems, recv_sems, 6 * t + j, (cx, cy, c)))
        for cp in first:
            cp.start()
        passed = []
        for j, (cx, cy) in enumerate(chips):
            sj = 2 * cx + cy
            for t in range(n):
                land = outs[t].at[:, sj, c]
                _remote(land, land, send_sems, recv_sems, 6 * t + j, (cx, cy, c)).wait_recv()
                fw = _remote(land, land, send_sems, recv_sems, 6 * t + 3 + j, sib)
                fw.start()
                passed.append(fw)
        for j, (cx, cy) in enumerate(chips):
            sj = 2 * cx + cy
            for t in range(n):
                land = outs[t].at[:, sj, 1 - c]
                _remote(land, land, send_sems, recv_sems, 6 * t + 3 + j, sib).wait_recv()
        for cp in first + passed:
            cp.wait_send()
        for cp in mine:
            cp.wait()

    out_shapes = [jax.ShapeDtypeStruct((t.shape[0], N_CHIPS) + t.shape[1:], t.dtype) for t in shards]
    return _comm_call(body, "gather_weights", shards, out_shapes, 6 * n, n)


def _gather_small(v):
    def body(v_ref, out_ref, send_sems, recv_sems, local_sems):
        x, y, c, chips = _where_am_i()
        s = 2 * x + y
        mine = pltpu.make_async_copy(v_ref, out_ref.at[s], local_sems.at[0])
        mine.start()
        sends = [_remote(v_ref, out_ref.at[s], send_sems, recv_sems, j, (cx, cy, c)) for j, (cx, cy) in enumerate(chips)]
        for cp in sends:
            cp.start()
        for j, (cx, cy) in enumerate(chips):
            land = out_ref.at[2 * cx + cy]
            _remote(land, land, send_sems, recv_sems, j, (cx, cy, c)).wait_recv()
        for cp in sends:
            cp.wait_send()
        mine.wait()

    return _comm_call(body, "gather_small", [v], [jax.ShapeDtypeStruct((N_CHIPS,) + v.shape, v.dtype)], 3, 1)[0]


def _swap_sibling_halves(grads):
    n = len(grads)

    def body(*refs):
        ins, outs = refs[:n], refs[n:2 * n]
        send_sems, recv_sems, _ = refs[2 * n:]
        x, y, c, _chips = _where_am_i()
        sib = (x, y, 1 - c)
        cps = [_remote(ins[t].at[:, :, 1 - c], outs[t], send_sems, recv_sems, t, sib) for t in range(n)]
        for cp in cps:
            cp.start()
        for cp in cps:
            cp.wait()

    out_shapes = [jax.ShapeDtypeStruct(g.shape[:2] + g.shape[3:], g.dtype) for g in grads]
    return _comm_call(body, "grad_swap_halves", grads, out_shapes, n, 0)


def _exchange_chips(parts):
    n = len(parts)

    def body(*refs):
        ins, outs = refs[:n], refs[n:2 * n]
        send_sems, recv_sems, _ = refs[2 * n:]
        x, y, c, chips = _where_am_i()
        cps = []
        for t in range(n):
            for j, (cx, cy) in enumerate(chips):
                cps.append(_remote(ins[t].at[:, 2 * cx + cy], outs[t].at[j], send_sems, recv_sems, 3 * t + j, (cx, cy, c)))
        for cp in cps:
            cp.start()
        for cp in cps:
            cp.wait()

    out_shapes = [jax.ShapeDtypeStruct((3, p.shape[0]) + p.shape[2:], p.dtype) for p in parts]
    return _comm_call(body, "grad_exchange_chips", parts, out_shapes, 3 * n, 0)


def _join_sibling_halves(halves):
    n = len(halves)

    def body(*refs):
        ins, outs = refs[:n], refs[n:2 * n]
        send_sems, recv_sems, local_sems = refs[2 * n:]
        x, y, c, _chips = _where_am_i()
        sib = (x, y, 1 - c)
        mine = [pltpu.make_async_copy(ins[t], outs[t].at[:, c], local_sems.at[t]) for t in range(n)]
        for cp in mine:
            cp.start()
        cps = [_remote(ins[t], outs[t].at[:, c], send_sems, recv_sems, t, sib) for t in range(n)]
        for cp in cps:
            cp.start()
        for t in range(n):
            land = outs[t].at[:, 1 - c]
            _remote(land, land, send_sems, recv_sems, t, sib).wait_recv()
        for cp in cps:
            cp.wait_send()
        for cp in mine:
            cp.wait()

    out_shapes = [jax.ShapeDtypeStruct((h.shape[0], 2) + h.shape[1:], h.dtype) for h in halves]
    return _comm_call(body, "grad_join_halves", halves, out_shapes, n, n)


def _swap_small(v):
    def body(v_ref, out_ref, send_sems, recv_sems, _):
        x, y, c, _chips = _where_am_i()
        cp = _remote(v_ref, out_ref, send_sems, recv_sems, 0, (x, y, 1 - c))
        cp.start()
        cp.wait()

    return _comm_call(body, "small_swap", [v], [jax.ShapeDtypeStruct(v.shape, v.dtype)], 1, 0)[0]


def _exchange_small(v):
    def body(v_ref, out_ref, send_sems, recv_sems, _):
        x, y, c, chips = _where_am_i()
        cps = [_remote(v_ref, out_ref.at[j], send_sems, recv_sems, j, (cx, cy, c)) for j, (cx, cy) in enumerate(chips)]
        for cp in cps:
            cp.start()
        for cp in cps:
            cp.wait()

    return _comm_call(body, "small_exchange", [v], [jax.ShapeDtypeStruct((3,) + v.shape, v.dtype)], 3, 0)[0]


def _sum_rows(name, terms, out_dtypes):
    R, Cc = terms[0].shape
    tm = R if R <= 256 else max(t for t in range(16, 257, 16) if R % t == 0)
    n = len(terms)

    def body(*refs):
        acc = refs[0][...].astype(F32)
        for r in refs[1:n]:
            acc = acc + r[...].astype(F32)
        for o in refs[n:]:
            o[...] = acc.astype(o.dtype)

    blk = pl.BlockSpec((tm, Cc), lambda i: (i, 0))
    return _pcall(
        body, name=name, grid=(R // tm,), in_specs=[blk] * n, out_specs=[blk] * len(out_dtypes),
        out_shape=[jax.ShapeDtypeStruct((R, Cc), d) for d in out_dtypes], compiler_params=_params("parallel"),
    )(*terms)


def _pair_sum(g5, r1, core, shard):
    A4, _, Rh, Cc = g5.shape
    A = A4 // N_CHIPS
    tr = Rh if Rh <= 256 else max(t for t in range(16, 257, 16) if Rh % t == 0)

    def body(core_ref, shard_ref, g_ref, r_ref, qb_ref, qf_ref):
        q = g_ref[...] + r_ref[...]
        qb_ref[...] = q.astype(BF16)

        @pl.when(pl.program_id(2) == shard_ref[0])
        def _():
            qf_ref[...] = q

    grid_spec = pltpu.PrefetchScalarGridSpec(
        num_scalar_prefetch=2, grid=(A, Rh // tr, N_CHIPS),
        in_specs=[pl.BlockSpec((None, None, tr, Cc), lambda a, r, sh, core, shard: (a * N_CHIPS + sh, core[0], r, 0)),
                  pl.BlockSpec((None, tr, Cc), lambda a, r, sh, core, shard: (a * N_CHIPS + sh, r, 0))],
        out_specs=[pl.BlockSpec((None, tr, Cc), lambda a, r, sh, core, shard: (a * N_CHIPS + sh, r, 0)),
                   pl.BlockSpec((None, tr, Cc), lambda a, r, sh, core, shard: (a, r, 0))],
    )
    return _pcall(
        body, name="grad_pair_sum", grid_spec=grid_spec,
        out_shape=[jax.ShapeDtypeStruct((A4, Rh, Cc), BF16), jax.ShapeDtypeStruct((A, Rh, Cc), F32)],
        compiler_params=_params("parallel", "parallel", "arbitrary"),
    )(core, shard, g5, r1)


def _reduce_big_grads(grads, core, shard):
    r1 = _swap_sibling_halves(grads)
    qb, qf = [], []
    for g, r in zip(grads, r1):
        A, _, _, Rh, Cc = g.shape
        b, f = _pair_sum(g.reshape(A * N_CHIPS, 2, Rh, Cc), r.reshape(A * N_CHIPS, Rh, Cc), core, shard)
        qb.append(b.reshape(A, N_CHIPS, Rh, Cc))
        qf.append(f)
    r2 = _exchange_chips(qb)
    halves = []
    for f, r in zip(qf, r2):
        A, Rh, Cc = f.shape
        terms = [f.reshape(A * Rh, Cc)] + [r[j].reshape(A * Rh, Cc) for j in range(3)]
        halves.append(_sum_rows("grad_chip_sum", terms, [F32])[0].reshape(A, Rh, Cc))
    full = _join_sibling_halves(halves)
    return [t.reshape(t.shape[0], 2 * t.shape[2], t.shape[3]) for t in full]


def _allreduce_small(v):
    pair = _sum_rows("small_pair_sum", [v, _swap_small(v)], [F32])[0]
    others = _exchange_small(pair)
    x, y = lax.axis_index("x"), lax.axis_index("y")
    s = 2 * x + y
    stack = jnp.concatenate([pair[None], others], axis=0)
    src = jnp.stack([s, s ^ 2, s ^ 1, s ^ 3])
    order = jnp.argsort(src)
    terms = [lax.dynamic_index_in_dim(stack, order[k], 0, keepdims=False) for k in range(N_CHIPS)]
    return _sum_rows("small_chip_sum", terms, [F32])[0]


def _block_diag(w):
    nb, bw, _ = w.shape
    per = LANES // bw
    w = w.reshape(nb // per, per, bw, bw)
    eye = jnp.eye(per, dtype=w.dtype)
    bd = jnp.einsum("tpij,pq->tpiqj", w, eye).reshape(nb // per, LANES, LANES)
    return bd.astype(BF16)


def _block_diag_grad(g, bw):
    nt = g.shape[0]
    per = LANES // bw
    g = g.reshape(nt, per, bw, per, bw)
    return jnp.stack([g[:, p, :, p, :] for p in range(per)], axis=1).reshape(nt * per, bw, bw)


def _split5(w):
    R, Cc = w.shape[-2:]
    return w.reshape(-1, 2, R // 2, Cc)


def kernel(x, w_in, conv_w, conv_b, w_rg, b_rg, w_ig, b_ig, lru_lambda, sinks, w_branch, w_out, ln1_g, ln1_b, w_ffn_in, w_ffn_out, ln2_g, ln2_b, loss_target, m_w_in, m_conv_w, m_conv_b, m_w_rg, m_b_rg, m_w_ig, m_b_ig, m_lru_lambda, m_sinks, m_w_branch, m_w_out, m_ln1_g, m_ln1_b, m_w_ffn_in, m_w_ffn_out, m_ln2_g, m_ln2_b, v_w_in, v_conv_w, v_conv_b, v_w_rg, v_b_rg, v_w_ig, v_b_ig, v_lru_lambda, v_sinks, v_w_branch, v_w_out, v_ln1_g, v_ln1_b, v_w_ffn_in, v_w_ffn_out, v_ln2_g, v_ln2_b):
    B, S, D = x.shape
    T = B * S
    L = w_in.shape[0]
    H = D // HEAD_DIM
    KVB = D // SWA_GROUP
    FH = w_ffn_out.shape[1] * N_CHIPS
    C = w_in.shape[2] * N_CHIPS
    alpha = (2.0 * L) ** 0.25
    off = {}
    pos = 0
    for nm, wd in (("lx", D), ("lg", D), ("qb", D), ("kb", KVB), ("vb", KVB), ("qc", D), ("kc", D), ("vc", D), ("gt", 3 * D)):
        off[nm] = pos
        pos += wd
    assert pos == C
    cx, cy, cc = lax.axis_index("x"), lax.axis_index("y"), lax.axis_index("c")
    shard = (2 * cx + cy).astype(jnp.int32)
    core_a = cc.astype(jnp.int32).reshape(1)
    shard_a = shard.reshape(1)

    full = []
    for l in range(L):
        sh = [_split5(w_in[l].astype(BF16)), _split5(w_branch[l].astype(BF16)), _split5(w_out[l].astype(BF16)),
              _split5(w_ffn_in[l].astype(BF16)), _split5(w_ffn_out[l].astype(BF16))]
        g = _gather_weights(sh)
        full.append(dict(
            w_in=g[0].reshape(N_CHIPS, D, C // N_CHIPS),
            w_branch=g[1].reshape(3, D, D),
            w_out=g[2].reshape(D, D),
            w_ffn_in=g[3].reshape(N_CHIPS, D, 2 * FH // N_CHIPS),
            w_ffn_out=g[4].reshape(FH, D),
        ))
    cw_all = _gather_small(conv_w.reshape(L * CONV_WIDTH, D // N_CHIPS))
    conv_w_full = jnp.transpose(cw_all, (1, 0, 2)).reshape(L, CONV_WIDTH, D)

    def layer_params(l):
        return dict(conv_w=conv_w_full[l], conv_b=conv_b[l][None], w_rg_bd=_block_diag(w_rg[l]), b_rg=b_rg[l][None],
                    w_ig_bd=_block_diag(w_ig[l]), b_ig=b_ig[l][None], lam=lru_lambda[l][None])

    def sink_rows(l, hb):
        sk = sinks[l].reshape(H // hb, 1, hb)
        return jnp.pad(sk, ((0, 0), (0, 0), (0, LANES - hb)))

    hb_b = _heads_per_block(C, (off["qb"],), (off["kb"], off["vb"]), SWA_GROUP, H)
    hb_c = _heads_per_block(C, (off["qc"],), (off["kc"], off["vc"]), 1, H)

    saved = []
    xin = x.reshape(T, D)
    for l in range(L):
        fw, lp = full[l], layer_params(l)
        proj = _matmul(xin, fw["w_in"], mode="nn", name="mm_proj", tm=512)
        proj3 = proj.reshape(B, S, C)
        h3, ya3 = _lru_fwd(proj3, lp, D=D, x_off=off["lx"], g_off=off["lg"])
        skr = sink_rows(l, hb_b)
        yb3, lse_b = _attn_fwd(proj3, D=D, q_off=off["qb"], k_off=off["kb"], v_off=off["vb"], group=SWA_GROUP, dil=1,
                               sinks=skr, name="swa_fwd")
        os_, ls_ = [], []
        for dil in DILS:
            o_, l_ = _attn_fwd(proj3, D=D, q_off=off["qc"], k_off=off["kc"], v_off=off["vc"], group=1, dil=dil,
                               name="dil%d_fwd" % dil)
            os_.append(o_.reshape(T, D))
            ls_.append(l_.reshape(T, -1))
        yc, lse_c = _combine(os_, ls_, D=D, hb=hb_c)
        ya, yb = ya3.reshape(T, D), yb3.reshape(T, D)
        branch, merged = _branch_fwd([ya, yb, yc], fw["w_branch"], proj, D=D, g_off=off["gt"])
        mix = _matmul(merged, fw["w_out"], mode="nn", name="mm_out")
        z1, x1 = _ln_fwd(xin, mix, ln1_g[l][None], ln1_b[l][None], alpha=alpha)
        hh = _matmul(x1, fw["w_ffn_in"], mode="nn", name="mm_ffn_in")
        f = _swiglu_fwd(hh)
        ffn = _matmul(f, fw["w_ffn_out"], mode="nn", name="mm_ffn_out", tk=4096)
        z2, x2 = _ln_fwd(x1, ffn, ln2_g[l][None], ln2_b[l][None], alpha=alpha)
        saved.append(dict(x=xin, proj=proj, h3=h3, ya=ya, yb=yb, lse_b=lse_b, yc=yc, lse_c=lse_c, branch=branch,
                          merged=merged, z1=z1, x1=x1, hh=hh, f=f, z2=z2, skr=skr))
        xin = x2

    dx, loss_rows = _loss_head(xin, loss_target.reshape(T, D))
    loss = lax.psum(jnp.sum(loss_rows), ("x", "y", "c"))

    big = {k: [None] * L for k in ("w_in", "w_branch", "w_out", "w_ffn_in", "w_ffn_out")}
    small = [None] * L
    for l in reversed(range(L)):
        fw, lp, sv = full[l], layer_params(l), saved[l]
        dz2, dg2, db2 = _ln_bwd(dx, sv["z2"], ln2_g[l][None])
        df = _matmul(dz2, fw["w_ffn_out"], mode="nt", name="mm_dffn_out_x", tk=1024)
        g_ffn_out = _matmul(sv["f"], dz2, mode="tn", name="mm_dffn_out_w", tm=256, tn=1024, tk=1024)
        dhh = _swiglu_bwd(sv["hh"], df)
        dx1 = _matmul(dhh, fw["w_ffn_in"], mode="nt", name="mm_dffn_in_x", resid=dz2, rs=alpha)
        g_ffn_in = _matmul(sv["x1"], dhh, mode="tn", name="mm_dffn_in_w", tm=512, tk=1024, out_shards=N_CHIPS)
        dz1, dg1, db1 = _ln_bwd(dx1, sv["z1"], ln1_g[l][None])
        dmerged = _matmul(dz1, fw["w_out"], mode="nt", name="mm_dout_x", tk=1024)
        g_out = _matmul(sv["merged"], dz1, mode="tn", name="mm_dout_w", tm=512, tn=1024, tk=1024)
        dbranch, dgates = _branch_bwd(dmerged, sv["branch"], sv["proj"], D=D, g_off=off["gt"])
        ys = [sv["ya"], sv["yb"], sv["yc"]]
        dys, g_branch = [], []
        for n in range(3):
            dys.append(_matmul(dbranch[n], fw["w_branch"][n], mode="nt", name="mm_dbranch_x", tk=1024))
            g_branch.append(_matmul(ys[n], dbranch[n], mode="tn", name="mm_dbranch_w", tm=512, tn=1024, tk=1024))
        proj3 = sv["proj"].reshape(B, S, C)
        r3 = lambda t: t.reshape(B, S, t.shape[-1])
        lru = _lru_bwd(proj3, sv["h3"], r3(dys[0]), lp, D=D, x_off=off["lx"], g_off=off["lg"])
        dxr, dgate = lru[0], lru[1]
        dqb, dkb, dvb, dsk = _attn_bwd(proj3, r3(sv["yb"]), sv["lse_b"], r3(dys[1]), D=D, q_off=off["qb"], k_off=off["kb"],
                                       v_off=off["vb"], group=SWA_GROUP, dil=1, sinks=sv["skr"], out_dtype=BF16, name="swa_bwd")
        acc = None
        for dil in DILS:
            acc = _attn_bwd(proj3, r3(sv["yc"]), r3(sv["lse_c"]), r3(dys[2]), D=D, q_off=off["qc"], k_off=off["kc"],
                            v_off=off["vc"], group=1, dil=dil, acc=acc, name="dil%d_bwd" % dil)
        f2 = lambda t: t.reshape(T, t.shape[-1]).astype(BF16)
        dproj = jnp.concatenate([f2(dxr), f2(dgate), f2(dqb), f2(dkb), f2(dvb), f2(acc[0]), f2(acc[1]), f2(acc[2])] + dgates, axis=1)
        dx = _matmul(dproj, fw["w_in"], mode="nt", name="mm_dproj_x", resid=dz1, rs=alpha)
        g_in = _matmul(sv["x"], dproj, mode="tn", name="mm_dproj_w", tm=512, tk=1024, out_shards=N_CHIPS)

        g5 = [g_in.reshape(1, N_CHIPS, 2, D // 2, C // N_CHIPS),
              jnp.stack(g_branch).reshape(3, N_CHIPS, 2, D // N_CHIPS // 2, D),
              g_out.reshape(1, N_CHIPS, 2, D // N_CHIPS // 2, D),
              g_ffn_in.reshape(1, N_CHIPS, 2, D // 2, 2 * FH // N_CHIPS),
              g_ffn_out.reshape(1, N_CHIPS, 2, FH // N_CHIPS // 2, D)]
        red = _reduce_big_grads(g5, core_a, shard_a)
        big["w_in"][l] = red[0].reshape(D, C // N_CHIPS)
        big["w_branch"][l] = red[1].reshape(3, D // N_CHIPS, D)
        big["w_out"][l] = red[2].reshape(D // N_CHIPS, D)
        big["w_ffn_in"][l] = red[3].reshape(D, 2 * FH // N_CHIPS)
        big["w_ffn_out"][l] = red[4].reshape(FH // N_CHIPS, D)

        dsinks = jnp.sum(dsk, axis=0)[:, 0, :hb_b].reshape(H)
        bw = w_rg.shape[-1]
        small[l] = [lru[2].reshape(-1), lru[3].reshape(-1), _block_diag_grad(lru[4], bw).reshape(-1), lru[5].reshape(-1),
                    _block_diag_grad(lru[6], bw).reshape(-1), lru[7].reshape(-1), lru[8].reshape(-1),
                    jnp.pad(dsinks, (0, LANES - H)), dg1.reshape(-1), db1.reshape(-1), dg2.reshape(-1), db2.reshape(-1)]

    sizes = [t.size for t in small[0]]
    flat = jnp.concatenate([t for l in range(L) for t in small[l]])
    n_flat = flat.size
    rows = -(-n_flat // (LANES * 256)) * 256
    flat = jnp.pad(flat, (0, rows * LANES - n_flat)).reshape(rows, LANES)
    red_small = _allreduce_small(flat).reshape(-1)
    per_layer = sum(sizes)
    names = ["conv_w", "conv_b", "w_rg", "b_rg", "w_ig", "b_ig", "lru_lambda", "sinks", "ln1_g", "ln1_b", "ln2_g", "ln2_b"]
    sg = {nm: [] for nm in names}
    for l in range(L):
        p = l * per_layer
        for nm, sz in zip(names, sizes):
            sg[nm].append(red_small[p:p + sz])
            p += sz
    grads = dict(
        w_in=jnp.stack(big["w_in"]), w_branch=jnp.stack(big["w_branch"]), w_out=jnp.stack(big["w_out"]),
        w_ffn_in=jnp.stack(big["w_ffn_in"]), w_ffn_out=jnp.stack(big["w_ffn_out"]),
        conv_w=lax.dynamic_slice_in_dim(jnp.stack(sg["conv_w"]).reshape(L, CONV_WIDTH, D), shard * (D // N_CHIPS), D // N_CHIPS, axis=2),
        conv_b=jnp.stack(sg["conv_b"]), w_rg=jnp.stack(sg["w_rg"]).reshape(w_rg.shape), b_rg=jnp.stack(sg["b_rg"]),
        w_ig=jnp.stack(sg["w_ig"]).reshape(w_ig.shape), b_ig=jnp.stack(sg["b_ig"]), lru_lambda=jnp.stack(sg["lru_lambda"]),
        sinks=jnp.stack(sg["sinks"])[:, :H], ln1_g=jnp.stack(sg["ln1_g"]), ln1_b=jnp.stack(sg["ln1_b"]),
        ln2_g=jnp.stack(sg["ln2_g"]), ln2_b=jnp.stack(sg["ln2_b"]),
    )

    order = ["w_in", "conv_w", "conv_b", "w_rg", "b_rg", "w_ig", "b_ig", "lru_lambda", "sinks", "w_branch", "w_out",
             "ln1_g", "ln1_b", "w_ffn_in", "w_ffn_out", "ln2_g", "ln2_b"]
    weights = dict(w_in=w_in, conv_w=conv_w, conv_b=conv_b, w_rg=w_rg, b_rg=b_rg, w_ig=w_ig, b_ig=b_ig, lru_lambda=lru_lambda,
                   sinks=sinks, w_branch=w_branch, w_out=w_out, ln1_g=ln1_g, ln1_b=ln1_b, w_ffn_in=w_ffn_in,
                   w_ffn_out=w_ffn_out, ln2_g=ln2_g, ln2_b=ln2_b)
    ms = dict(w_in=m_w_in, conv_w=m_conv_w, conv_b=m_conv_b, w_rg=m_w_rg, b_rg=m_b_rg, w_ig=m_w_ig, b_ig=m_b_ig,
              lru_lambda=m_lru_lambda, sinks=m_sinks, w_branch=m_w_branch, w_out=m_w_out, ln1_g=m_ln1_g, ln1_b=m_ln1_b,
              w_ffn_in=m_w_ffn_in, w_ffn_out=m_w_ffn_out, ln2_g=m_ln2_g, ln2_b=m_ln2_b)
    vs = dict(w_in=v_w_in, conv_w=v_conv_w, conv_b=v_conv_b, w_rg=v_w_rg, b_rg=v_b_rg, w_ig=v_w_ig, b_ig=v_b_ig,
              lru_lambda=v_lru_lambda, sinks=v_sinks, w_branch=v_w_branch, w_out=v_w_out, ln1_g=v_ln1_g, ln1_b=v_ln1_b,
              w_ffn_in=v_w_ffn_in, w_ffn_out=v_w_ffn_out, ln2_g=v_ln2_g, ln2_b=v_ln2_b)
    deltas, new_m, new_v = {}, {}, {}
    for nm in order:
        deltas[nm], new_m[nm], new_v[nm] = _adamw(weights[nm], grads[nm], ms[nm], vs[nm])
    return (loss, dx.reshape(B, S, D), *[grads[nm] for nm in order], *[deltas[nm] for nm in order],
            *[new_m[nm] for nm in order], *[new_v[nm] for nm in order])
```

```python
import functools
import math

import jax
import jax.numpy as jnp
from jax import lax
from jax.experimental import pallas as pl
from jax.experimental.pallas import tpu as pltpu

HEAD_DIM = 64
WIN = 128
DILS = (1, 4, 16)
SWA_GROUP = 4
CONV_WIDTH = 4
LRU_C = 8.0
LN_EPS = 1e-5
NEG_INF = -1e30
N_CHIPS = 4
ADAM_LR, ADAM_B1, ADAM_B2, ADAM_EPS, ADAM_WD, ADAM_STEP = 0.001, 0.9, 0.999, 1e-08, 0.01, 10

LANES = 128
SUBLANES = 8
VMEM_LIMIT = 48 * 1024 * 1024

F32 = jnp.float32
BF16 = jnp.bfloat16
MESH = pl.DeviceIdType.MESH
ANY = pl.BlockSpec(memory_space=pl.ANY)


def _pcall(body, **kw):
    return pl.pallas_call(body, **kw)


def _pcall_comm(body, **kw):
    return pl.pallas_call(body, **kw)


def _params(*sem):
    return pltpu.CompilerParams(dimension_semantics=tuple(sem), vmem_limit_bytes=VMEM_LIMIT)


def _tile(dim, target):
    if dim <= target:
        return dim
    best = None
    for t in range(LANES, target + 1, LANES):
        if dim % t == 0:
            best = t
    assert best is not None, (dim, target)
    return best


def _sigmoid(x):
    return 1.0 / (1.0 + jnp.exp(-x))


def _dot(a, b, dims):
    return lax.dot_general(a, b, (dims, ((), ())), preferred_element_type=F32)


def _dot_nn(a, b):
    return _dot(a, b, ((1,), (0,)))


def _dot_nt(a, b):
    return _dot(a, b, ((1,), (1,)))


def _dot_tn(a, b):
    return _dot(a, b, ((0,), (0,)))


def _matmul(a, b, *, mode, name, out_dtype=F32, tm=512, tn=512, tk=2048, resid=None, rs=1.0, out_shards=0):
    b_sh = b.ndim == 3
    if mode == "nn":
        M, K = a.shape
        N = b.shape[-1] * (b.shape[0] if b_sh else 1)
    elif mode == "nt":
        M, K = a.shape
        N = b.shape[-2]
    else:
        K, M = a.shape
        N = b.shape[-1]
    tm = _tile(M, tm)
    if mode == "nn" and b_sh:
        tn = b.shape[-1]
    elif out_shards:
        tn = N // out_shards
    else:
        tn = _tile(N, tn)
    if mode == "nt" and b_sh:
        tk = b.shape[-1]
    else:
        tk = _tile(K, tk)
    nk = K // tk
    grid = (M // tm, N // tn, nk)

    if mode == "nn":
        a_spec = pl.BlockSpec((tm, tk), lambda i, j, k: (i, k))
        b_spec = (pl.BlockSpec((None, tk, tn), lambda i, j, k: (j, k, 0)) if b_sh
                  else pl.BlockSpec((tk, tn), lambda i, j, k: (k, j)))
        contract = _dot_nn
    elif mode == "nt":
        a_spec = pl.BlockSpec((tm, tk), lambda i, j, k: (i, k))
        b_spec = (pl.BlockSpec((None, tn, tk), lambda i, j, k: (k, j, 0)) if b_sh
                  else pl.BlockSpec((tn, tk), lambda i, j, k: (j, k)))
        contract = _dot_nt
    else:
        a_spec = pl.BlockSpec((tk, tm), lambda i, j, k: (k, i))
        b_spec = pl.BlockSpec((tk, tn), lambda i, j, k: (k, j))
        contract = _dot_tn
    if out_shards:
        out_shape = jax.ShapeDtypeStruct((out_shards, M, tn), out_dtype)
        o_spec = pl.BlockSpec((None, tm, tn), lambda i, j, k: (j, i, 0))
    else:
        out_shape = jax.ShapeDtypeStruct((M, N), out_dtype)
        o_spec = pl.BlockSpec((tm, tn), lambda i, j, k: (i, j))
    in_specs = [a_spec, b_spec]
    args = [a, b]
    if resid is not None:
        in_specs.append(pl.BlockSpec((tm, tn), lambda i, j, k: (i, j)))
        args.append(resid)

    def body(*refs):
        if resid is not None:
            a_ref, b_ref, r_ref, o_ref, acc_ref = refs
        else:
            a_ref, b_ref, o_ref, acc_ref = refs
        k = pl.program_id(2)
        part = contract(a_ref[...].astype(BF16), b_ref[...].astype(BF16))

        @pl.when(k == 0)
        def _():
            acc_ref[...] = part

        @pl.when(k > 0)
        def _():
            acc_ref[...] += part

        @pl.when(k == nk - 1)
        def _():
            res = acc_ref[...]
            if resid is not None:
                res = res + rs * r_ref[...]
            o_ref[...] = res.astype(out_dtype)

    return _pcall(
        body, name=name, grid=grid, in_specs=in_specs, out_specs=o_spec, out_shape=out_shape,
        scratch_shapes=[pltpu.VMEM((tm, tn), F32)],
        compiler_params=_params("parallel", "parallel", "arbitrary"),
    )(*args)


def _shift_down(x, d, row):
    return jnp.where(row >= d, pltpu.roll(x, d, 0), 0.0)


def _shift_up(x, d, row, n):
    return jnp.where(row < n - d, pltpu.roll(x, n - d, 0), 0.0)


def _log1p(u):
    w = 1.0 + u
    return jnp.where(w == 1.0, u, jnp.log(w) * u / (w - 1.0))


def _gelu_parts(g):
    k = math.sqrt(2.0 / math.pi)
    c = 0.044715
    t = jnp.tanh(k * (g + c * g * g * g))
    val = 0.5 * g * (1.0 + t)
    der = 0.5 * (1.0 + t) + 0.5 * g * (1.0 - t * t) * k * (1.0 + 3.0 * c * g * g)
    return val, der


def _lru_gates(xr, cw_ref, cb_ref, wrg_ref, brg_ref, wig_ref, big_ref, lam_ref, row):
    xc = cw_ref[3:4, :] * xr + cb_ref[...]
    for d in range(1, CONV_WIDTH):
        xc = xc + cw_ref[3 - d:4 - d, :] * _shift_down(xr, d, row)
    xcb = xc.astype(BF16)
    r = _sigmoid(_dot_nn(xcb, wrg_ref[...]) + brg_ref[...])
    ig = _sigmoid(_dot_nn(xcb, wig_ref[...]) + big_ref[...])
    lam = lam_ref[...]
    sp = jnp.maximum(-lam, 0.0) + _log1p(jnp.exp(-jnp.abs(lam)))
    log_a = (-LRU_C) * r * sp
    a = jnp.exp(log_a)
    y2 = 2.0 * log_a
    one_m_a2 = jnp.where(y2 > -0.01, -(y2 + 0.5 * y2 * y2 + (1.0 / 6.0) * y2 * y2 * y2), 1.0 - jnp.exp(y2))
    mult = jnp.sqrt(one_m_a2)
    return xc, r, ig, sp, a, mult


def _scan_local(a, b, row, n, reverse):
    sub = row % SUBLANES
    d = 1
    while d < SUBLANES:
        if reverse:
            keep = sub < SUBLANES - d
            a_s = jnp.where(keep, pltpu.roll(a, n - d, 0), 1.0)
            b_s = jnp.where(keep, pltpu.roll(b, n - d, 0), 0.0)
        else:
            keep = sub >= d
            a_s = jnp.where(keep, pltpu.roll(a, d, 0), 1.0)
            b_s = jnp.where(keep, pltpu.roll(b, d, 0), 0.0)
        b = a * b_s + b
        a = a * a_s
        d *= 2
    return a, b


def _scan_carry(a_ref, b_ref, out_ref, n, reverse):
    ng = n // SUBLANES

    def step(gidx, carry):
        g = (ng - 1 - gidx) if reverse else gidx
        rows = pl.ds(pl.multiple_of(g * SUBLANES, SUBLANES), SUBLANES)
        h = a_ref[rows, :] * carry + b_ref[rows, :]
        out_ref[rows, :] = h
        return h[0:1, :] if reverse else h[SUBLANES - 1:SUBLANES, :]

    lax.fori_loop(0, ng, step, jnp.zeros((1, LANES), F32), unroll=8)


def _lru_specs(B, S, D, C, x_off, g_off):
    nct = D // LANES
    seq = lambda off: pl.BlockSpec((None, S, LANES), lambda ct, b: (b, 0, off // LANES + ct))
    row = lambda r: pl.BlockSpec((r, LANES), lambda ct, b: (0, ct))
    wbd = pl.BlockSpec((None, LANES, LANES), lambda ct, b: (ct, 0, 0))
    return nct, seq, row, wbd


def _lru_fwd(proj3, lp, *, D, x_off, g_off):
    B, S, C = proj3.shape
    nct, seq, row, wbd = _lru_specs(B, S, D, C, x_off, g_off)

    def body(xr_ref, g_ref, cw_ref, cb_ref, wrg_ref, brg_ref, wig_ref, big_ref, lam_ref, h_ref, ya_ref, a_s, b_s):
        rowi = lax.broadcasted_iota(jnp.int32, (S, LANES), 0)
        xr = xr_ref[...]
        xc, r, ig, sp, a, mult = _lru_gates(xr, cw_ref, cb_ref, wrg_ref, brg_ref, wig_ref, big_ref, lam_ref, rowi)
        al, bl = _scan_local(a, mult * (ig * xc), rowi, S, False)
        a_s[...] = al
        b_s[...] = bl
        _scan_carry(a_s, b_s, h_ref, S, False)
        gel, _ = _gelu_parts(g_ref[...])
        ya_ref[...] = (h_ref[...] * gel).astype(BF16)

    out_seq = pl.BlockSpec((None, S, LANES), lambda ct, b: (b, 0, ct))
    return _pcall(
        body, name="lru_fwd", grid=(nct, B),
        in_specs=[seq(x_off), seq(g_off), row(CONV_WIDTH), row(1), wbd, row(1), wbd, row(1), row(1)],
        out_specs=[out_seq, out_seq],
        out_shape=[jax.ShapeDtypeStruct((B, S, D), F32), jax.ShapeDtypeStruct((B, S, D), BF16)],
        scratch_shapes=[pltpu.VMEM((S, LANES), F32), pltpu.VMEM((S, LANES), F32)],
        compiler_params=_params("parallel", "parallel"),
    )(proj3, proj3, lp["conv_w"], lp["conv_b"], lp["w_rg_bd"], lp["b_rg"], lp["w_ig_bd"], lp["b_ig"], lp["lam"])


def _lru_bwd(proj3, h3, dya3, lp, *, D, x_off, g_off):
    B, S, C = proj3.shape
    nct, seq, row, wbd = _lru_specs(B, S, D, C, x_off, g_off)

    def body(xr_ref, g_ref, h_ref, dy_ref, cw_ref, cb_ref, wrg_ref, brg_ref, wig_ref, big_ref, lam_ref,
             dxr_ref, dg_ref, dcw_ref, dcb_ref, dwrg_ref, dbrg_ref, dwig_ref, dbig_ref, dlam_ref, a_s, b_s, l_s):
        first = pl.program_id(1) == 0
        rowi = lax.broadcasted_iota(jnp.int32, (S, LANES), 0)
        xr = xr_ref[...]
        xc, r, ig, sp, a, mult = _lru_gates(xr, cw_ref, cb_ref, wrg_ref, brg_ref, wig_ref, big_ref, lam_ref, rowi)
        h = h_ref[...]
        dy = dy_ref[...]
        gel, dgel = _gelu_parts(g_ref[...])
        dg_ref[...] = (dy * h * dgel).astype(BF16)
        al, bl = _scan_local(_shift_up(a, 1, rowi, S), dy * gel, rowi, S, True)
        a_s[...] = al
        b_s[...] = bl
        _scan_carry(a_s, b_s, l_s, S, True)
        lamb = l_s[...]
        u = ig * xc
        da = lamb * _shift_down(h, 1, rowi)
        dlog_a = da * a - (lamb * u) * (a * a) / mult
        du = lamb * mult
        dpre_r = (dlog_a * ((-LRU_C) * sp)) * r * (1.0 - r)
        dpre_i = (du * xc) * ig * (1.0 - ig)
        dsp = jnp.sum(dlog_a * ((-LRU_C) * r), axis=0, keepdims=True)
        dlam = dsp * (-1.0 / (1.0 + jnp.exp(lam_ref[...])))
        dpr = dpre_r.astype(BF16)
        dpi = dpre_i.astype(BF16)
        dxc = du * ig + _dot_nt(dpr, wrg_ref[...]) + _dot_nt(dpi, wig_ref[...])
        xcb = xc.astype(BF16)
        dwrg = _dot_tn(xcb, dpr)
        dwig = _dot_tn(xcb, dpi)
        dxr = cw_ref[3:4, :] * dxc
        dcw = [jnp.sum(xr * dxc, axis=0, keepdims=True)]
        for d in range(1, CONV_WIDTH):
            dxr = dxr + cw_ref[3 - d:4 - d, :] * _shift_up(dxc, d, rowi, S)
            dcw.append(jnp.sum(_shift_down(xr, d, rowi) * dxc, axis=0, keepdims=True))
        dxr_ref[...] = dxr.astype(BF16)
        dcw_rows = jnp.concatenate(dcw[::-1], axis=0)
        sums = ((dcw_ref, dcw_rows), (dcb_ref, jnp.sum(dxc, axis=0, keepdims=True)), (dwrg_ref, dwrg),
                (dbrg_ref, jnp.sum(dpre_r, axis=0, keepdims=True)), (dwig_ref, dwig),
                (dbig_ref, jnp.sum(dpre_i, axis=0, keepdims=True)), (dlam_ref, dlam))

        @pl.when(first)
        def _():
            for ref, val in sums:
                ref[...] = val

        @pl.when(jnp.logical_not(first))
        def _():
            for ref, val in sums:
                ref[...] += val

    out_seq = pl.BlockSpec((None, S, LANES), lambda ct, b: (b, 0, ct))
    f = lambda shape: jax.ShapeDtypeStruct(shape, F32)
    nb = D // LANES
    return _pcall(
        body, name="lru_bwd", grid=(nct, B),
        in_specs=[seq(x_off), seq(g_off), out_seq, out_seq, row(CONV_WIDTH), row(1), wbd, row(1), wbd, row(1), row(1)],
        out_specs=[out_seq, out_seq, row(CONV_WIDTH), row(1), wbd, row(1), wbd, row(1), row(1)],
        out_shape=[jax.ShapeDtypeStruct((B, S, D), BF16), jax.ShapeDtypeStruct((B, S, D), BF16),
                   f((CONV_WIDTH, D)), f((1, D)), f((nb, LANES, LANES)), f((1, D)), f((nb, LANES, LANES)), f((1, D)), f((1, D))],
        scratch_shapes=[pltpu.VMEM((S, LANES), F32)] * 3,
        compiler_params=_params("parallel", "arbitrary"),
    )(proj3, proj3, h3, dya3, lp["conv_w"], lp["conv_b"], lp["w_rg_bd"], lp["b_rg"], lp["w_ig_bd"], lp["b_ig"], lp["lam"])


def _heads_per_block(C, offs_q, offs_kv, group, n_heads):
    for hb in (8, 4, 2):
        qw, kvw = hb * HEAD_DIM, hb * HEAD_DIM // group
        if hb <= n_heads and n_heads % hb == 0 and kvw % LANES == 0 and C % qw == 0 and C % kvw == 0 \
                and all(o % qw == 0 for o in offs_q) and all(o % kvw == 0 for o in offs_kv):
            return hb
    raise ValueError("no head blocking fits these offsets")


def _attn_masks(i):
    qi = lax.broadcasted_iota(jnp.int32, (WIN, WIN), 0)
    kj = lax.broadcasted_iota(jnp.int32, (WIN, WIN), 1)
    return jnp.logical_and(kj >= qi, i > 0), kj <= qi


def _attn_fwd(src3, *, D, q_off, k_off, v_off, group, dil, sinks=None, name):
    B, S, C = src3.shape
    H = D // HEAD_DIM
    hb = _heads_per_block(C, (q_off,), (k_off, v_off), group, H)
    nhb, qw, kvw = H // hb, hb * HEAD_DIM, hb * HEAD_DIM // group
    Ls = S // dil
    nblk = Ls // WIN
    scale = HEAD_DIM ** -0.5
    src = src3.reshape(B, Ls, dil * C)
    has_sink = sinks is not None

    def body(*refs):
        if has_sink:
            q_ref, kp_ref, kc_ref, vp_ref, vc_ref, sk_ref, o_ref, lse_ref = refs
        else:
            q_ref, kp_ref, kc_ref, vp_ref, vc_ref, o_ref, lse_ref = refs
        valid_p, valid_c = _attn_masks(pl.program_id(3))
        lane = lax.broadcasted_iota(jnp.int32, (WIN, LANES), 1)
        lse_acc = jnp.zeros((WIN, LANES), F32)
        for h in range(hb):
            g = h // group
            qs, ks = slice(h * HEAD_DIM, (h + 1) * HEAD_DIM), slice(g * HEAD_DIM, (g + 1) * HEAD_DIM)
            qh = q_ref[:, qs].astype(BF16)
            sp_ = jnp.where(valid_p, _dot_nt(qh, kp_ref[:, ks].astype(BF16)) * scale, NEG_INF)
            sc = jnp.where(valid_c, _dot_nt(qh, kc_ref[:, ks].astype(BF16)) * scale, NEG_INF)
            m = jnp.maximum(jnp.max(sp_, axis=1, keepdims=True), jnp.max(sc, axis=1, keepdims=True))
            if has_sink:
                sk = sk_ref[:, h:h + 1]
                m = jnp.maximum(m, sk)
            pp = jnp.exp(sp_ - m)
            pc = jnp.exp(sc - m)
            den = jnp.sum(pp, axis=1, keepdims=True) + jnp.sum(pc, axis=1, keepdims=True)
            if has_sink:
                den = den + jnp.exp(sk - m)
            oh = _dot_nn(pp.astype(BF16), vp_ref[:, ks].astype(BF16)) + _dot_nn(pc.astype(BF16), vc_ref[:, ks].astype(BF16))
            o_ref[:, qs] = oh / den
            lse_acc = jnp.where(lane == h, m + jnp.log(den), lse_acc)
        lse_ref[...] = lse_acc

    def blk(width, off, prev):
        def imap(b, r, hh, i):
            return (b, jnp.maximum(i - 1, 0) if prev else i, (r * C + off) // width + hh)
        return pl.BlockSpec((None, WIN, width), imap)

    in_specs = [blk(qw, q_off, False), blk(kvw, k_off, True), blk(kvw, k_off, False), blk(kvw, v_off, True), blk(kvw, v_off, False)]
    args = [src] * 5
    if has_sink:
        in_specs.append(pl.BlockSpec((None, 1, LANES), lambda b, r, hh, i: (hh, 0, 0)))
        args.append(sinks)
    o, lse = _pcall(
        body, name=name, grid=(B, dil, nhb, nblk), in_specs=in_specs,
        out_specs=[pl.BlockSpec((None, WIN, qw), lambda b, r, hh, i: (b, i, r * (D // qw) + hh)),
                   pl.BlockSpec((None, WIN, LANES), lambda b, r, hh, i: (b, i, r * nhb + hh))],
        out_shape=[jax.ShapeDtypeStruct((B, Ls, dil * D), F32), jax.ShapeDtypeStruct((B, Ls, dil * nhb * LANES), F32)],
        compiler_params=_params("parallel", "parallel", "parallel", "parallel"),
    )(*args)
    return o.reshape(B, S, D), lse.reshape(B, S, nhb * LANES)


def _attn_bwd(src3, o3, lse3, do3, *, D, q_off, k_off, v_off, group, dil, sinks=None, acc=None, out_dtype=F32, name):
    B, S, C = src3.shape
    H = D // HEAD_DIM
    hb = _heads_per_block(C, (q_off,), (k_off, v_off), group, H)
    nhb, qw, kvw = H // hb, hb * HEAD_DIM, hb * HEAD_DIM // group
    KV = D // group
    Ls = S // dil
    nblk = Ls // WIN
    scale = HEAD_DIM ** -0.5
    has_sink = sinks is not None
    has_acc = acc is not None

    def body(*refs):
        q_ref, kp_ref, kc_ref, vp_ref, vc_ref, o_ref, l_ref, do_ref = refs[:8]
        pos = 8
        if has_sink:
            sk_ref = refs[pos]
            pos += 1
        if has_acc:
            aq_ref, ak_ref, av_ref = refs[pos:pos + 3]
            pos += 3
        dq_ref, dk_ref, dv_ref = refs[pos:pos + 3]
        pos += 3
        if has_sink:
            dsk_ref = refs[pos]
            pos += 1
        ck_ref, cv_ref = refs[pos:pos + 2]
        i = pl.program_id(3)

        @pl.when(i == 0)
        def _():
            ck_ref[...] = jnp.zeros_like(ck_ref)
            cv_ref[...] = jnp.zeros_like(cv_ref)
            if has_sink:
                dsk_ref[...] = jnp.zeros_like(dsk_ref)

        @pl.when(i < nblk)
        def _():
            valid_p, valid_c = _attn_masks(i)
            lane = lax.broadcasted_iota(jnp.int32, (1, LANES), 1)
            dsk_acc = jnp.zeros((1, LANES), F32)
            for g in range(hb // group):
                ks = slice(g * HEAD_DIM, (g + 1) * HEAD_DIM)
                kp, kc = kp_ref[:, ks].astype(BF16), kc_ref[:, ks].astype(BF16)
                vp, vc = vp_ref[:, ks].astype(BF16), vc_ref[:, ks].astype(BF16)
                dkp = dkc = dvp = dvc = jnp.zeros((WIN, HEAD_DIM), F32)
                for h in range(g * group, (g + 1) * group):
                    qs = slice(h * HEAD_DIM, (h + 1) * HEAD_DIM)
                    qh = q_ref[:, qs].astype(BF16)
                    doh = do_ref[:, qs]
                    dd = jnp.sum(doh * o_ref[:, qs], axis=1, keepdims=True)
                    lh = l_ref[:, h:h + 1]
                    dob = doh.astype(BF16)
                    pp = jnp.exp(jnp.where(valid_p, _dot_nt(qh, kp) * scale, NEG_INF) - lh)
                    pc = jnp.exp(jnp.where(valid_c, _dot_nt(qh, kc) * scale, NEG_INF) - lh)
                    dsp = (pp * (_dot_nt(dob, vp) - dd) * scale).astype(BF16)
                    dsc = (pc * (_dot_nt(dob, vc) - dd) * scale).astype(BF16)
                    dqh = _dot_nn(dsp, kp) + _dot_nn(dsc, kc)
                    if has_acc:
                        dqh = dqh + aq_ref[:, qs]
                    dq_ref[:, qs] = dqh.astype(out_dtype)
                    dkp = dkp + _dot_tn(dsp, qh)
                    dkc = dkc + _dot_tn(dsc, qh)
                    dvp = dvp + _dot_tn(pp.astype(BF16), dob)
                    dvc = dvc + _dot_tn(pc.astype(BF16), dob)
                    if has_sink:
                        sk = sk_ref[:, h:h + 1]
                        dsk_h = -jnp.sum(jnp.exp(sk - lh) * dd, axis=0, keepdims=True)
                        dsk_acc = jnp.where(lane == h, dsk_h, dsk_acc)
                dkp = dkp + ck_ref[:, ks]
                dvp = dvp + cv_ref[:, ks]
                if has_acc:
                    dkp = dkp + ak_ref[:, ks]
                    dvp = dvp + av_ref[:, ks]
                dk_ref[:, ks] = dkp.astype(out_dtype)
                dv_ref[:, ks] = dvp.astype(out_dtype)
                ck_ref[:, ks] = dkc
                cv_ref[:, ks] = dvc
            if has_sink:
                dsk_ref[...] += dsk_acc

        @pl.when(i == nblk)
        def _():
            dk = ck_ref[...]
            dv = cv_ref[...]
            if has_acc:
                dk = dk + ak_ref[...]
                dv = dv + av_ref[...]
            dk_ref[...] = dk.astype(out_dtype)
            dv_ref[...] = dv.astype(out_dtype)

    last = nblk - 1

    def blk(width, off, back, row_c):
        def imap(b, r, hh, i):
            ii = jnp.minimum(i, last)
            if back == 1:
                ii = jnp.maximum(ii - 1, 0)
            elif back == 2:
                ii = jnp.maximum(i - 1, 0)
            return (b, ii, (r * row_c + off) // width + hh)
        return pl.BlockSpec((None, WIN, width), imap)

    src = src3.reshape(B, Ls, dil * C)
    view = lambda t: t.reshape(B, Ls, dil * t.shape[-1])
    in_specs = [blk(qw, q_off, 0, C), blk(kvw, k_off, 1, C), blk(kvw, k_off, 0, C), blk(kvw, v_off, 1, C), blk(kvw, v_off, 0, C),
                blk(qw, 0, 0, D), blk(LANES, 0, 0, nhb * LANES), blk(qw, 0, 0, D)]
    args = [src] * 5 + [view(o3), view(lse3), view(do3)]
    if has_sink:
        in_specs.append(pl.BlockSpec((None, 1, LANES), lambda b, r, hh, i: (hh, 0, 0)))
        args.append(sinks)
    if has_acc:
        in_specs += [blk(qw, 0, 0, D), blk(kvw, 0, 2, KV), blk(kvw, 0, 2, KV)]
        args += [view(t) for t in acc]
    out_specs = [blk(qw, 0, 0, D), blk(kvw, 0, 2, KV), blk(kvw, 0, 2, KV)]
    out_shape = [jax.ShapeDtypeStruct((B, Ls, dil * D), out_dtype), jax.ShapeDtypeStruct((B, Ls, dil * KV), out_dtype),
                 jax.ShapeDtypeStruct((B, Ls, dil * KV), out_dtype)]
    if has_sink:
        out_specs.append(pl.BlockSpec((None, None, 1, LANES), lambda b, r, hh, i: (b, hh, 0, 0)))
        out_shape.append(jax.ShapeDtypeStruct((B, nhb, 1, LANES), F32))
    res = _pcall(
        body, name=name, grid=(B, dil, nhb, nblk + 1), in_specs=in_specs, out_specs=out_specs, out_shape=out_shape,
        scratch_shapes=[pltpu.VMEM((WIN, kvw), F32), pltpu.VMEM((WIN, kvw), F32)],
        compiler_params=_params("parallel", "parallel", "parallel", "arbitrary"),
    )(*args)
    dq, dk, dv = res[0].reshape(B, S, D), res[1].reshape(B, S, KV), res[2].reshape(B, S, KV)
    return (dq, dk, dv, res[3]) if has_sink else (dq, dk, dv)


def _pair_stack(x, lo):
    z = jnp.zeros_like(x)
    return jnp.concatenate([jnp.where(lo, x, z), jnp.where(lo, z, x)], axis=0).astype(BF16)


def _pair_join(y2, lo):
    return jnp.where(lo, y2[:WIN], y2[WIN:])


def _pair_col(xb):
    return jnp.concatenate([xb[:, 0:1], xb[:, HEAD_DIM:HEAD_DIM + 1]], axis=0)


def _pair_bcast(col, lo):
    return jnp.where(lo, jnp.broadcast_to(col[:WIN], (WIN, LANES)), jnp.broadcast_to(col[WIN:], (WIN, LANES)))


def _dil_rows(it, d, S):
    if d == 1:
        cur = pl.multiple_of(it * WIN, WIN)
        prev = pl.multiple_of(jnp.maximum(it - 1, 0) * WIN, WIN)
        return pl.ds(cur, WIN), pl.ds(prev, WIN), it > 0
    r, i = it % d, it // d
    cur = i * (WIN * d) + r
    prev = jnp.maximum(i - 1, 0) * (WIN * d) + r
    return pl.ds(cur, WIN, stride=d), pl.ds(prev, WIN, stride=d), i > 0


def _dil_bias(two_blocks):
    nk = 2 * WIN if two_blocks else WIN
    qi = lax.broadcasted_iota(jnp.int32, (2 * WIN, nk), 0) & (WIN - 1)
    kj = lax.broadcasted_iota(jnp.int32, (2 * WIN, nk), 1)
    if not two_blocks:
        return jnp.where(kj <= qi, 0.0, NEG_INF), None
    cur = jnp.logical_and(kj >= WIN, kj - WIN <= qi)
    prev = jnp.logical_and(kj < WIN, kj >= qi)
    return jnp.where(jnp.logical_or(cur, prev), 0.0, NEG_INF), jnp.where(cur, 0.0, NEG_INF)


def _dil_specs(B, S, D, C, offs):
    grid = (B, D // LANES)
    seq = lambda off: pl.BlockSpec((None, S, LANES), lambda b, p: (b, 0, off // LANES + p))
    return grid, [seq(o) for o in offs], seq(0)


def _dil_fwd(proj3, *, D, q_off, k_off, v_off):
    B, S, C = proj3.shape
    n_it = S // WIN
    scale = HEAD_DIM ** -0.5
    grid, in_specs, out_spec = _dil_specs(B, S, D, C, (q_off, k_off, v_off))

    def body(q_ref, k_ref, v_ref, o_ref, l_ref):
        lo = lax.broadcasted_iota(jnp.int32, (WIN, LANES), 1) < HEAD_DIM
        for c, d in enumerate(DILS):
            two = S // d > WIN
            bias_all, bias_first = _dil_bias(two)

            def step(it, _, c=c, d=d, two=two, bias_all=bias_all, bias_first=bias_first):
                cur, prev, later = _dil_rows(it, d, S)
                q2 = _pair_stack(q_ref[cur, :], lo)
                if two:
                    k2 = jnp.concatenate([k_ref[prev, :], k_ref[cur, :]], axis=0).astype(BF16)
                    v2 = jnp.concatenate([v_ref[prev, :], v_ref[cur, :]], axis=0).astype(BF16)
                    bias = jnp.where(later, bias_all, bias_first)
                else:
                    k2, v2, bias = k_ref[cur, :].astype(BF16), v_ref[cur, :].astype(BF16), bias_all
                s2 = _dot_nt(q2, k2) * scale + bias
                m2 = jnp.max(s2, axis=1, keepdims=True)
                p2 = jnp.exp(s2 - m2)
                den = jnp.sum(p2, axis=1, keepdims=True)
                oc = _pair_join(_dot_nn(p2.astype(BF16), v2) / den, lo)
                lc = _pair_bcast(m2 + jnp.log(den), lo)
                if c == 0:
                    o_ref[cur, :] = oc
                    l_ref[cur, :] = lc
                else:
                    l_old = l_ref[cur, :]
                    mx = jnp.maximum(l_old, lc)
                    e_old, e_new = jnp.exp(l_old - mx), jnp.exp(lc - mx)
                    tot = e_old + e_new
                    o_ref[cur, :] = (e_old * o_ref[cur, :] + e_new * oc) / tot
                    l_ref[cur, :] = mx + jnp.log(tot)
                return 0

            lax.fori_loop(0, n_it, step, 0, unroll=2)

    return _pcall(
        body, name="dil_fwd", grid=grid, in_specs=in_specs, out_specs=[out_spec, out_spec],
        out_shape=[jax.ShapeDtypeStruct((B, S, D), F32)] * 2,
        compiler_params=_params("parallel", "parallel"),
    )(proj3, proj3, proj3)


def _dil_bwd(proj3, o3, l3, do3, *, D, q_off, k_off, v_off):
    B, S, C = proj3.shape
    n_it = S // WIN
    scale = HEAD_DIM ** -0.5
    grid, in_specs, out_spec = _dil_specs(B, S, D, C, (q_off, k_off, v_off))

    def body(q_ref, k_ref, v_ref, o_ref, l_ref, do_ref, dq_ref, dk_ref, dv_ref, dd_s, dq_s, dk_s, dv_s):
        lo = lax.broadcasted_iota(jnp.int32, (WIN, LANES), 1) < HEAD_DIM
        lo_s = lax.broadcasted_iota(jnp.int32, (S, LANES), 1) < HEAD_DIM
        prod = do_ref[...] * o_ref[...]
        d_lo = jnp.sum(jnp.where(lo_s, prod, 0.0), axis=1, keepdims=True)
        d_hi = jnp.sum(jnp.where(lo_s, 0.0, prod), axis=1, keepdims=True)
        dd_s[...] = jnp.where(lo_s, jnp.broadcast_to(d_lo, (S, LANES)), jnp.broadcast_to(d_hi, (S, LANES)))
        dq_s[...] = jnp.zeros_like(dq_s)
        dk_s[...] = jnp.zeros_like(dk_s)
        dv_s[...] = jnp.zeros_like(dv_s)
        for d in DILS:
            two = S // d > WIN
            bias_all, bias_first = _dil_bias(two)

            def step(it, _, d=d, two=two, bias_all=bias_all, bias_first=bias_first):
                cur, prev, later = _dil_rows(it, d, S)
                q2 = _pair_stack(q_ref[cur, :], lo)
                do2 = _pair_stack(do_ref[cur, :], lo)
                l2 = _pair_col(l_ref[cur, :])
                dd2 = _pair_col(dd_s[cur, :])
                if two:
                    k2 = jnp.concatenate([k_ref[prev, :], k_ref[cur, :]], axis=0).astype(BF16)
                    v2 = jnp.concatenate([v_ref[prev, :], v_ref[cur, :]], axis=0).astype(BF16)
                    bias = jnp.where(later, bias_all, bias_first)
                else:
                    k2, v2, bias = k_ref[cur, :].astype(BF16), v_ref[cur, :].astype(BF16), bias_all
                p2 = jnp.exp(_dot_nt(q2, k2) * scale + bias - l2)
                ds2 = (p2 * (_dot_nt(do2, v2) - dd2) * scale).astype(BF16)
                dq_s[cur, :] += _pair_join(_dot_nn(ds2, k2), lo)
                dk2 = _dot_tn(ds2, q2)
                dv2 = _dot_tn(p2.astype(BF16), do2)
                if two:
                    dk_s[prev, :] += dk2[:WIN]
                    dv_s[prev, :] += dv2[:WIN]
                    dk_s[cur, :] += dk2[WIN:]
                    dv_s[cur, :] += dv2[WIN:]
                else:
                    dk_s[cur, :] += dk2
                    dv_s[cur, :] += dv2
                return 0

            lax.fori_loop(0, n_it, step, 0, unroll=2)
        dq_ref[...] = dq_s[...].astype(BF16)
        dk_ref[...] = dk_s[...].astype(BF16)
        dv_ref[...] = dv_s[...].astype(BF16)

    return _pcall(
        body, name="dil_bwd", grid=grid, in_specs=in_specs + [out_spec] * 3, out_specs=[out_spec] * 3,
        out_shape=[jax.ShapeDtypeStruct((B, S, D), BF16)] * 3,
        scratch_shapes=[pltpu.VMEM((S, LANES), F32)] * 4,
        compiler_params=_params("parallel", "parallel"),
    )(proj3, proj3, proj3, o3, l3, do3)


def _branch_fwd(ys, wb, proj, *, D, g_off):
    T = proj.shape[0]
    tm, tn = _tile(T, 256), _tile(D, 512)
    n = len(ys)

    def body(*refs):
        y_refs, w_ref, g_refs, br_ref, mg_ref = refs[:n], refs[n], refs[n + 1:2 * n + 1], refs[2 * n + 1], refs[2 * n + 2]
        acc = None
        for k in range(n):
            br = _dot_nn(y_refs[k][...].astype(BF16), w_ref[k])
            br_ref[k] = br
            term = _sigmoid(g_refs[k][...]) * br
            acc = term if acc is None else acc + term
        mg_ref[...] = acc.astype(BF16)

    gate = lambda k: pl.BlockSpec((tm, tn), lambda i, j: (i, (g_off + k * D) // tn + j))
    return _pcall(
        body, name="branch_fwd", grid=(T // tm, D // tn),
        in_specs=[pl.BlockSpec((tm, D), lambda i, j: (i, 0))] * n + [pl.BlockSpec((n, D, tn), lambda i, j: (0, 0, j))]
        + [gate(k) for k in range(n)],
        out_specs=[pl.BlockSpec((n, tm, tn), lambda i, j: (0, i, j)), pl.BlockSpec((tm, tn), lambda i, j: (i, j))],
        out_shape=[jax.ShapeDtypeStruct((n, T, D), F32), jax.ShapeDtypeStruct((T, D), BF16)],
        compiler_params=_params("parallel", "parallel"),
    )(*ys, wb, *([proj] * n))


def _branch_bwd(dmerged, branch, proj, *, D, g_off):
    n, T, _ = branch.shape
    tm, tn = _tile(T, 512), _tile(D, 512)

    def body(dm_ref, br_ref, *rest):
        g_refs, db_ref, dg_refs = rest[:n], rest[n], rest[n + 1:]
        dm = dm_ref[...]
        for k in range(n):
            sg = _sigmoid(g_refs[k][...])
            db_ref[k] = (sg * dm).astype(BF16)
            dg_refs[k][...] = (dm * br_ref[k] * sg * (1.0 - sg)).astype(BF16)

    gate = lambda k: pl.BlockSpec((tm, tn), lambda i, j: (i, (g_off + k * D) // tn + j))
    blk = pl.BlockSpec((tm, tn), lambda i, j: (i, j))
    res = _pcall(
        body, name="branch_bwd", grid=(T // tm, D // tn),
        in_specs=[blk, pl.BlockSpec((n, tm, tn), lambda i, j: (0, i, j))] + [gate(k) for k in range(n)],
        out_specs=[pl.BlockSpec((n, tm, tn), lambda i, j: (0, i, j))] + [blk] * n,
        out_shape=[jax.ShapeDtypeStruct((n, T, D), BF16)] + [jax.ShapeDtypeStruct((T, D), BF16)] * n,
        compiler_params=_params("parallel", "parallel"),
    )(dmerged, branch, *([proj] * n))
    return res[0], list(res[1:])


def _ln_fwd(xres, y, g, b, *, alpha):
    T, D = xres.shape
    tm = _tile(T, 512)

    def body(x_ref, y_ref, g_ref, b_ref, z_ref, o_ref):
        z = alpha * x_ref[...] + y_ref[...]
        mu = jnp.mean(z, axis=1, keepdims=True)
        zc = z - mu
        var = jnp.mean(zc * zc, axis=1, keepdims=True)
        z_ref[...] = z
        o_ref[...] = zc * lax.rsqrt(var + LN_EPS) * g_ref[...] + b_ref[...]

    blk = pl.BlockSpec((tm, D), lambda i: (i, 0))
    vec = pl.BlockSpec((1, D), lambda i: (0, 0))
    return _pcall(
        body, name="ln_fwd", grid=(T // tm,), in_specs=[blk, blk, vec, vec], out_specs=[blk, blk],
        out_shape=[jax.ShapeDtypeStruct((T, D), F32)] * 2, compiler_params=_params("parallel"),
    )(xres, y, g, b)


def _ln_bwd(dout, z, g):
    T, D = z.shape
    tm = _tile(T, 512)

    def body(do_ref, z_ref, g_ref, dz_ref, dg_ref, db_ref):
        z = z_ref[...]
        do = do_ref[...]
        mu = jnp.mean(z, axis=1, keepdims=True)
        zc = z - mu
        rstd = lax.rsqrt(jnp.mean(zc * zc, axis=1, keepdims=True) + LN_EPS)
        xhat = zc * rstd
        dxh = do * g_ref[...]
        dz_ref[...] = rstd * (dxh - jnp.mean(dxh, axis=1, keepdims=True) - xhat * jnp.mean(dxh * xhat, axis=1, keepdims=True))
        dg = jnp.sum(do * xhat, axis=0, keepdims=True)
        db = jnp.sum(do, axis=0, keepdims=True)
        first = pl.program_id(0) == 0

        @pl.when(first)
        def _():
            dg_ref[...] = dg
            db_ref[...] = db

        @pl.when(jnp.logical_not(first))
        def _():
            dg_ref[...] += dg
            db_ref[...] += db

    blk = pl.BlockSpec((tm, D), lambda i: (i, 0))
    vec = pl.BlockSpec((1, D), lambda i: (0, 0))
    return _pcall(
        body, name="ln_bwd", grid=(T // tm,), in_specs=[blk, blk, vec], out_specs=[blk, vec, vec],
        out_shape=[jax.ShapeDtypeStruct((T, D), F32), jax.ShapeDtypeStruct((1, D), F32), jax.ShapeDtypeStruct((1, D), F32)],
        compiler_params=_params("arbitrary"),
    )(dout, z, g)


def _swiglu_fwd(hh):
    T, F2 = hh.shape
    Fh = F2 // 2
    tm, tn = _tile(T, 512), _tile(Fh, 256)
    nj = Fh // tn

    def body(h1_ref, h3_ref, f_ref):
        h1 = h1_ref[...]
        f_ref[...] = (h1 * _sigmoid(h1) * h3_ref[...]).astype(BF16)

    return _pcall(
        body, name="swiglu_fwd", grid=(T // tm, nj),
        in_specs=[pl.BlockSpec((tm, tn), lambda i, j: (i, j)), pl.BlockSpec((tm, tn), lambda i, j: (i, nj + j))],
        out_specs=pl.BlockSpec((tm, tn), lambda i, j: (i, j)),
        out_shape=jax.ShapeDtypeStruct((T, Fh), BF16), compiler_params=_params("parallel", "parallel"),
    )(hh, hh)


def _swiglu_bwd(hh, df):
    T, F2 = hh.shape
    Fh = F2 // 2
    tm, tn = _tile(T, 512), _tile(Fh, 256)
    nj = Fh // tn

    def body(h1_ref, h3_ref, df_ref, d1_ref, d3_ref):
        h1 = h1_ref[...]
        sg = _sigmoid(h1)
        d = df_ref[...]
        d1_ref[...] = (d * h3_ref[...] * sg * (1.0 + h1 * (1.0 - sg))).astype(BF16)
        d3_ref[...] = (d * h1 * sg).astype(BF16)

    lo = pl.BlockSpec((tm, tn), lambda i, j: (i, j))
    hi = pl.BlockSpec((tm, tn), lambda i, j: (i, nj + j))
    d1, d3 = _pcall(
        body, name="swiglu_bwd", grid=(T // tm, nj), in_specs=[lo, hi, lo], out_specs=[lo, lo],
        out_shape=[jax.ShapeDtypeStruct((T, Fh), BF16)] * 2, compiler_params=_params("parallel", "parallel"),
    )(hh, hh, df)
    return jnp.concatenate([d1, d3], axis=1)


def _loss_head(y, target):
    T, D = y.shape
    tm = _tile(T, 512)

    def body(y_ref, t_ref, dy_ref, l_ref):
        e = y_ref[...] - t_ref[...]
        dy_ref[...] = e * (1.0 / D)
        sq = e * e
        part = sq[:, 0:LANES]
        for c in range(1, D // LANES):
            part = part + sq[:, c * LANES:(c + 1) * LANES]
        part = jnp.sum(part, axis=0, keepdims=True) * (0.5 / D)
        first = pl.program_id(0) == 0

        @pl.when(first)
        def _():
            l_ref[...] = part

        @pl.when(jnp.logical_not(first))
        def _():
            l_ref[...] += part

    blk = pl.BlockSpec((tm, D), lambda i: (i, 0))
    return _pcall(
        body, name="loss_head", grid=(T // tm,), in_specs=[blk, blk],
        out_specs=[blk, pl.BlockSpec((1, LANES), lambda i: (0, 0))],
        out_shape=[jax.ShapeDtypeStruct((T, D), F32), jax.ShapeDtypeStruct((1, LANES), F32)],
        compiler_params=_params("arbitrary"),
    )(y, target)


def _as_rows(a):
    return a.reshape(-1, a.shape[-1])


def _adamw(w, g, m, v):
    w2, g2, m2, v2 = (_as_rows(t) for t in (w, g, m, v))
    R, Cc = w2.shape
    cap = max(SUBLANES, min(512, (256 * 1024) // Cc))
    tm = R if (R <= cap or R % SUBLANES) else max(t for t in range(SUBLANES, cap + 1, SUBLANES) if R % t == 0)
    c1 = 1.0 - ADAM_B1 ** ADAM_STEP
    c2 = 1.0 - ADAM_B2 ** ADAM_STEP

    def body(w_ref, g_ref, m_ref, v_ref, d_ref, nm_ref, nv_ref):
        gg = g_ref[...]
        nm = ADAM_B1 * m_ref[...] + (1.0 - ADAM_B1) * gg
        nv = ADAM_B2 * v_ref[...] + (1.0 - ADAM_B2) * (gg * gg)
        d_ref[...] = (-ADAM_LR) * ((nm / c1) / (jnp.sqrt(nv / c2) + ADAM_EPS) + ADAM_WD * w_ref[...])
        nm_ref[...] = nm
        nv_ref[...] = nv

    blk = pl.BlockSpec((tm, Cc), lambda i: (i, 0))
    res = _pcall(
        body, name="adamw", grid=(R // tm,), in_specs=[blk] * 4, out_specs=[blk] * 3,
        out_shape=[jax.ShapeDtypeStruct((R, Cc), F32)] * 3, compiler_params=_params("parallel"),
    )(w2, g2, m2, v2)
    return tuple(t.reshape(w.shape) for t in res)


def _where_am_i():
    x, y, c = lax.axis_index("x"), lax.axis_index("y"), lax.axis_index("c")
    chips = [(1 - x, y), (x, 1 - y), (1 - x, 1 - y)]
    return x, y, c, chips


def _remote(src, dst, send_sems, recv_sems, k, to):
    return pltpu.make_async_remote_copy(src_ref=src, dst_ref=dst, send_sem=send_sems.at[k], recv_sem=recv_sems.at[k],
                                        device_id=to, device_id_type=MESH)


def _comm_call(body, name, ins, out_shapes, n_remote, n_local):
    return _pcall_comm(
        body, name=name, in_specs=[ANY] * len(ins), out_specs=[ANY] * len(out_shapes), out_shape=out_shapes,
        scratch_shapes=[pltpu.SemaphoreType.DMA((n_remote,)), pltpu.SemaphoreType.DMA((n_remote,)),
                        pltpu.SemaphoreType.DMA((max(n_local, 1),))],
    )(*ins)


def _gather_weights(shards):
    n = len(shards)

    def body(*refs):
        ins, outs = refs[:n], refs[n:2 * n]
        send_sems, recv_sems, local_sems = refs[2 * n:]
        x, y, c, chips = _where_am_i()
        s = 2 * x + y
        sib = (x, y, 1 - c)
        first = []
        for t in range(n):
            for j, (cx, cy) in enumerate(chips):
                first.append(_remote(ins[t].at[:, c], outs[t].at[:, s, c], send_sems, recv_sems, 6 * t + j, (cx, cy, c)))
        for cp in first:
            cp.start()
        passed = []
        for j, (cx, cy) in enumerate(chips):
            sj = 2 * cx + cy
            for t in range(n):
                land = outs[t].at[:, sj, c]
                _remote(land, land, send_sems, recv_sems, 6 * t + j, (cx, cy, c)).wait_recv()
                fw = _remote(land, land, send_sems, recv_sems, 6 * t + 3 + j, sib)
                fw.start()
                passed.append(fw)
        for j, (cx, cy) in enumerate(chips):
            sj = 2 * cx + cy
            for t in range(n):
                land = outs[t].at[:, sj, 1 - c]
                _remote(land, land, send_sems, recv_sems, 6 * t + 3 + j, sib).wait_recv()
        for cp in first + passed:
            cp.wait_send()

    out_shapes = [jax.ShapeDtypeStruct((t.shape[0], N_CHIPS) + t.shape[1:], t.dtype) for t in shards]
    got = _comm_call(body, "gather_weights", shards, out_shapes, 6 * n, 0)
    s = 2 * lax.axis_index("x") + lax.axis_index("y")
    return [lax.dynamic_update_slice(g, t[:, None], (0, s, 0, 0, 0)) for g, t in zip(got, shards)]


def _gather_small(v):
    def body(v_ref, out_ref, send_sems, recv_sems, local_sems):
        x, y, c, chips = _where_am_i()
        s = 2 * x + y
        mine = pltpu.make_async_copy(v_ref, out_ref.at[s], local_sems.at[0])
        mine.start()
        sends = [_remote(v_ref, out_ref.at[s], send_sems, recv_sems, j, (cx, cy, c)) for j, (cx, cy) in enumerate(chips)]
        for cp in sends:
            cp.start()
        for j, (cx, cy) in enumerate(chips):
            land = out_ref.at[2 * cx + cy]
            _remote(land, land, send_sems, recv_sems, j, (cx, cy, c)).wait_recv()
        for cp in sends:
            cp.wait_send()
        mine.wait()

    return _comm_call(body, "gather_small", [v], [jax.ShapeDtypeStruct((N_CHIPS,) + v.shape, v.dtype)], 3, 1)[0]


def _swap_sibling_halves(grads):
    n = len(grads)

    def body(*refs):
        ins, outs = refs[:n], refs[n:2 * n]
        send_sems, recv_sems, _ = refs[2 * n:]
        x, y, c, _chips = _where_am_i()
        sib = (x, y, 1 - c)
        cps = [_remote(ins[t].at[:, :, 1 - c], outs[t], send_sems, recv_sems, t, sib) for t in range(n)]
        for cp in cps:
            cp.start()
        for cp in cps:
            cp.wait()

    out_shapes = [jax.ShapeDtypeStruct(g.shape[:2] + g.shape[3:], g.dtype) for g in grads]
    return _comm_call(body, "grad_swap_halves", grads, out_shapes, n, 0)


def _exchange_chips(parts):
    n = len(parts)

    def body(*refs):
        ins, outs = refs[:n], refs[n:2 * n]
        send_sems, recv_sems, _ = refs[2 * n:]
        x, y, c, chips = _where_am_i()
        cps = []
        for t in range(n):
            for j, (cx, cy) in enumerate(chips):
                cps.append(_remote(ins[t].at[:, 2 * cx + cy], outs[t].at[j], send_sems, recv_sems, 3 * t + j, (cx, cy, c)))
        for cp in cps:
            cp.start()
        for cp in cps:
            cp.wait()

    out_shapes = [jax.ShapeDtypeStruct((3, p.shape[0]) + p.shape[2:], p.dtype) for p in parts]
    return _comm_call(body, "grad_exchange_chips", parts, out_shapes, 3 * n, 0)


def _join_sibling_halves(halves):
    n = len(halves)

    def body(*refs):
        ins, outs = refs[:n], refs[n:2 * n]
        send_sems, recv_sems, local_sems = refs[2 * n:]
        x, y, c, _chips = _where_am_i()
        sib = (x, y, 1 - c)
        cps = [_remote(ins[t], outs[t].at[:, c], send_sems, recv_sems, t, sib) for t in range(n)]
        for cp in cps:
            cp.start()
        for t in range(n):
            land = outs[t].at[:, 1 - c]
            _remote(land, land, send_sems, recv_sems, t, sib).wait_recv()
        for cp in cps:
            cp.wait_send()

    out_shapes = [jax.ShapeDtypeStruct((h.shape[0], 2) + h.shape[1:], h.dtype) for h in halves]
    got = _comm_call(body, "grad_join_halves", halves, out_shapes, n, 0)
    c = lax.axis_index("c")
    return [lax.dynamic_update_slice(g, h[:, None], (0, c, 0, 0)) for g, h in zip(got, halves)]


def _swap_small(v):
    def body(v_ref, out_ref, send_sems, recv_sems, _):
        x, y, c, _chips = _where_am_i()
        cp = _remote(v_ref, out_ref, send_sems, recv_sems, 0, (x, y, 1 - c))
        cp.start()
        cp.wait()

    return _comm_call(body, "small_swap", [v], [jax.ShapeDtypeStruct(v.shape, v.dtype)], 1, 0)[0]


def _exchange_small(v):
    def body(v_ref, out_ref, send_sems, recv_sems, _):
        x, y, c, chips = _where_am_i()
        cps = [_remote(v_ref, out_ref.at[j], send_sems, recv_sems, j, (cx, cy, c)) for j, (cx, cy) in enumerate(chips)]
        for cp in cps:
            cp.start()
        for cp in cps:
            cp.wait()

    return _comm_call(body, "small_exchange", [v], [jax.ShapeDtypeStruct((3,) + v.shape, v.dtype)], 3, 0)[0]


def _sum_rows(name, terms, out_dtypes):
    R, Cc = terms[0].shape
    tm = R if R <= 256 else max(t for t in range(16, 257, 16) if R % t == 0)
    n = len(terms)

    def body(*refs):
        acc = refs[0][...].astype(F32)
        for r in refs[1:n]:
            acc = acc + r[...].astype(F32)
        for o in refs[n:]:
            o[...] = acc.astype(o.dtype)

    blk = pl.BlockSpec((tm, Cc), lambda i: (i, 0))
    return _pcall(
        body, name=name, grid=(R // tm,), in_specs=[blk] * n, out_specs=[blk] * len(out_dtypes),
        out_shape=[jax.ShapeDtypeStruct((R, Cc), d) for d in out_dtypes], compiler_params=_params("parallel"),
    )(*terms)


def _pair_sum(g5, r1, core, shard):
    A4, _, Rh, Cc = g5.shape
    A = A4 // N_CHIPS
    tr = Rh if Rh <= 256 else max(t for t in range(16, 257, 16) if Rh % t == 0)

    def body(core_ref, shard_ref, g_ref, r_ref, qb_ref, qf_ref):
        q = g_ref[...] + r_ref[...]
        qb_ref[...] = q.astype(BF16)

        @pl.when(pl.program_id(2) == shard_ref[0])
        def _():
            qf_ref[...] = q

    grid_spec = pltpu.PrefetchScalarGridSpec(
        num_scalar_prefetch=2, grid=(A, Rh // tr, N_CHIPS),
        in_specs=[pl.BlockSpec((None, None, tr, Cc), lambda a, r, sh, core, shard: (a * N_CHIPS + sh, core[0], r, 0)),
                  pl.BlockSpec((None, tr, Cc), lambda a, r, sh, core, shard: (a * N_CHIPS + sh, r, 0))],
        out_specs=[pl.BlockSpec((None, tr, Cc), lambda a, r, sh, core, shard: (a * N_CHIPS + sh, r, 0)),
                   pl.BlockSpec((None, tr, Cc), lambda a, r, sh, core, shard: (a, r, 0))],
    )
    return _pcall(
        body, name="grad_pair_sum", grid_spec=grid_spec,
        out_shape=[jax.ShapeDtypeStruct((A4, Rh, Cc), BF16), jax.ShapeDtypeStruct((A, Rh, Cc), F32)],
        compiler_params=_params("parallel", "parallel", "arbitrary"),
    )(core, shard, g5, r1)


def _reduce_big_grads(grads, core, shard):
    r1 = _swap_sibling_halves(grads)
    qb, qf = [], []
    for g, r in zip(grads, r1):
        A, _, _, Rh, Cc = g.shape
        b, f = _pair_sum(g.reshape(A * N_CHIPS, 2, Rh, Cc), r.reshape(A * N_CHIPS, Rh, Cc), core, shard)
        qb.append(b.reshape(A, N_CHIPS, Rh, Cc))
        qf.append(f)
    r2 = _exchange_chips(qb)
    halves = []
    for f, r in zip(qf, r2):
        A, Rh, Cc = f.shape
        terms = [f.reshape(A * Rh, Cc)] + [r[j].reshape(A * Rh, Cc) for j in range(3)]
        halves.append(_sum_rows("grad_chip_sum", terms, [F32])[0].reshape(A, Rh, Cc))
    full = _join_sibling_halves(halves)
    return [t.reshape(t.shape[0], 2 * t.shape[2], t.shape[3]) for t in full]


def _allreduce_small(v):
    pair = _sum_rows("small_pair_sum", [v, _swap_small(v)], [F32])[0]
    others = _exchange_small(pair)
    x, y = lax.axis_index("x"), lax.axis_index("y")
    s = 2 * x + y
    stack = jnp.concatenate([pair[None], others], axis=0)
    src = jnp.stack([s, s ^ 2, s ^ 1, s ^ 3])
    order = jnp.argsort(src)
    terms = [lax.dynamic_index_in_dim(stack, order[k], 0, keepdims=False) for k in range(N_CHIPS)]
    return _sum_rows("small_chip_sum", terms, [F32])[0]


def _block_diag(w):
    nb, bw, _ = w.shape
    per = LANES // bw
    w = w.reshape(nb // per, per, bw, bw)
    eye = jnp.eye(per, dtype=w.dtype)
    bd = jnp.einsum("tpij,pq->tpiqj", w, eye).reshape(nb // per, LANES, LANES)
    return bd.astype(BF16)


def _block_diag_grad(g, bw):
    nt = g.shape[0]
    per = LANES // bw
    g = g.reshape(nt, per, bw, per, bw)
    return jnp.stack([g[:, p, :, p, :] for p in range(per)], axis=1).reshape(nt * per, bw, bw)


def _split5(w):
    R, Cc = w.shape[-2:]
    return w.reshape(-1, 2, R // 2, Cc)


def kernel(x, w_in, conv_w, conv_b, w_rg, b_rg, w_ig, b_ig, lru_lambda, sinks, w_branch, w_out, ln1_g, ln1_b, w_ffn_in, w_ffn_out, ln2_g, ln2_b, loss_target, m_w_in, m_conv_w, m_conv_b, m_w_rg, m_b_rg, m_w_ig, m_b_ig, m_lru_lambda, m_sinks, m_w_branch, m_w_out, m_ln1_g, m_ln1_b, m_w_ffn_in, m_w_ffn_out, m_ln2_g, m_ln2_b, v_w_in, v_conv_w, v_conv_b, v_w_rg, v_b_rg, v_w_ig, v_b_ig, v_lru_lambda, v_sinks, v_w_branch, v_w_out, v_ln1_g, v_ln1_b, v_w_ffn_in, v_w_ffn_out, v_ln2_g, v_ln2_b):
    B, S, D = x.shape
    T = B * S
    L = w_in.shape[0]
    H = D // HEAD_DIM
    KVB = D // SWA_GROUP
    FH = w_ffn_out.shape[1] * N_CHIPS
    C = w_in.shape[2] * N_CHIPS
    alpha = (2.0 * L) ** 0.25
    off = {}
    pos = 0
    for nm, wd in (("lx", D), ("lg", D), ("qb", D), ("kb", KVB), ("vb", KVB), ("qc", D), ("kc", D), ("vc", D), ("gt", 3 * D)):
        off[nm] = pos
        pos += wd
    assert pos == C
    cx, cy, cc = lax.axis_index("x"), lax.axis_index("y"), lax.axis_index("c")
    shard = (2 * cx + cy).astype(jnp.int32)
    core_a = cc.astype(jnp.int32).reshape(1)
    shard_a = shard.reshape(1)

    full = []
    for l in range(L):
        sh = [_split5(w_in[l].astype(BF16)), _split5(w_branch[l].astype(BF16)), _split5(w_out[l].astype(BF16)),
              _split5(w_ffn_in[l].astype(BF16)), _split5(w_ffn_out[l].astype(BF16))]
        g = _gather_weights(sh)
        full.append(dict(
            w_in=g[0].reshape(N_CHIPS, D, C // N_CHIPS),
            w_branch=g[1].reshape(3, D, D),
            w_out=g[2].reshape(D, D),
            w_ffn_in=g[3].reshape(N_CHIPS, D, 2 * FH // N_CHIPS),
            w_ffn_out=g[4].reshape(FH, D),
        ))
    cw_all = _gather_small(conv_w.reshape(L * CONV_WIDTH, D // N_CHIPS))
    conv_w_full = jnp.transpose(cw_all, (1, 0, 2)).reshape(L, CONV_WIDTH, D)

    def layer_params(l):
        return dict(conv_w=conv_w_full[l], conv_b=conv_b[l][None], w_rg_bd=_block_diag(w_rg[l]), b_rg=b_rg[l][None],
                    w_ig_bd=_block_diag(w_ig[l]), b_ig=b_ig[l][None], lam=lru_lambda[l][None])

    def sink_rows(l, hb):
        sk = sinks[l].reshape(H // hb, 1, hb)
        return jnp.pad(sk, ((0, 0), (0, 0), (0, LANES - hb)))

    hb_b = _heads_per_block(C, (off["qb"],), (off["kb"], off["vb"]), SWA_GROUP, H)

    saved = []
    xin = x.reshape(T, D)
    for l in range(L):
        fw, lp = full[l], layer_params(l)
        proj = _matmul(xin, fw["w_in"], mode="nn", name="mm_proj", tm=512)
        proj3 = proj.reshape(B, S, C)
        h3, ya3 = _lru_fwd(proj3, lp, D=D, x_off=off["lx"], g_off=off["lg"])
        skr = sink_rows(l, hb_b)
        yb3, lse_b = _attn_fwd(proj3, D=D, q_off=off["qb"], k_off=off["kb"], v_off=off["vb"], group=SWA_GROUP, dil=1,
                               sinks=skr, name="swa_fwd")
        yc3, lse_c = _dil_fwd(proj3, D=D, q_off=off["qc"], k_off=off["kc"], v_off=off["vc"])
        ya, yb, yc = ya3.reshape(T, D), yb3.reshape(T, D), yc3.reshape(T, D)
        branch, merged = _branch_fwd([ya, yb, yc], fw["w_branch"], proj, D=D, g_off=off["gt"])
        mix = _matmul(merged, fw["w_out"], mode="nn", name="mm_out")
        z1, x1 = _ln_fwd(xin, mix, ln1_g[l][None], ln1_b[l][None], alpha=alpha)
        hh = _matmul(x1, fw["w_ffn_in"], mode="nn", name="mm_ffn_in")
        f = _swiglu_fwd(hh)
        ffn = _matmul(f, fw["w_ffn_out"], mode="nn", name="mm_ffn_out", tk=4096)
        z2, x2 = _ln_fwd(x1, ffn, ln2_g[l][None], ln2_b[l][None], alpha=alpha)
        saved.append(dict(x=xin, proj=proj, h3=h3, ya=ya, yb=yb, lse_b=lse_b, yc=yc, lse_c=lse_c, branch=branch,
                          merged=merged, z1=z1, x1=x1, hh=hh, f=f, z2=z2, skr=skr))
        xin = x2

    dx, loss_rows = _loss_head(xin, loss_target.reshape(T, D))
    loss = lax.psum(jnp.sum(loss_rows), ("x", "y", "c"))

    big = {k: [None] * L for k in ("w_in", "w_branch", "w_out", "w_ffn_in", "w_ffn_out")}
    small = [None] * L
    for l in reversed(range(L)):
        fw, lp, sv = full[l], layer_params(l), saved[l]
        dz2, dg2, db2 = _ln_bwd(dx, sv["z2"], ln2_g[l][None])
        df = _matmul(dz2, fw["w_ffn_out"], mode="nt", name="mm_dffn_out_x", tk=1024)
        g_ffn_out = _matmul(sv["f"], dz2, mode="tn", name="mm_dffn_out_w", tm=256, tn=1024, tk=1024)
        dhh = _swiglu_bwd(sv["hh"], df)
        dx1 = _matmul(dhh, fw["w_ffn_in"], mode="nt", name="mm_dffn_in_x", resid=dz2, rs=alpha)
        g_ffn_in = _matmul(sv["x1"], dhh, mode="tn", name="mm_dffn_in_w", tm=512, tk=1024, out_shards=N_CHIPS)
        dz1, dg1, db1 = _ln_bwd(dx1, sv["z1"], ln1_g[l][None])
        dmerged = _matmul(dz1, fw["w_out"], mode="nt", name="mm_dout_x", tk=1024)
        g_out = _matmul(sv["merged"], dz1, mode="tn", name="mm_dout_w", tm=512, tn=1024, tk=1024)
        dbranch, dgates = _branch_bwd(dmerged, sv["branch"], sv["proj"], D=D, g_off=off["gt"])
        ys = [sv["ya"], sv["yb"], sv["yc"]]
        dys, g_branch = [], []
        for n in range(3):
            dys.append(_matmul(dbranch[n], fw["w_branch"][n], mode="nt", name="mm_dbranch_x", tk=1024))
            g_branch.append(_matmul(ys[n], dbranch[n], mode="tn", name="mm_dbranch_w", tm=512, tn=1024, tk=1024))
        proj3 = sv["proj"].reshape(B, S, C)
        r3 = lambda t: t.reshape(B, S, t.shape[-1])
        lru = _lru_bwd(proj3, sv["h3"], r3(dys[0]), lp, D=D, x_off=off["lx"], g_off=off["lg"])
        dxr, dgate = lru[0], lru[1]
        dqb, dkb, dvb, dsk = _attn_bwd(proj3, r3(sv["yb"]), sv["lse_b"], r3(dys[1]), D=D, q_off=off["qb"], k_off=off["kb"],
                                       v_off=off["vb"], group=SWA_GROUP, dil=1, sinks=sv["skr"], out_dtype=BF16, name="swa_bwd")
        acc = _dil_bwd(proj3, r3(sv["yc"]), sv["lse_c"], r3(dys[2]), D=D, q_off=off["qc"], k_off=off["kc"], v_off=off["vc"])
        f2 = lambda t: t.reshape(T, t.shape[-1]).astype(BF16)
        dproj = jnp.concatenate([f2(dxr), f2(dgate), f2(dqb), f2(dkb), f2(dvb), f2(acc[0]), f2(acc[1]), f2(acc[2])] + dgates, axis=1)
        dx = _matmul(dproj, fw["w_in"], mode="nt", name="mm_dproj_x", resid=dz1, rs=alpha)
        g_in = _matmul(sv["x"], dproj, mode="tn", name="mm_dproj_w", tm=512, tk=1024, out_shards=N_CHIPS)

        g5 = [g_in.reshape(1, N_CHIPS, 2, D // 2, C // N_CHIPS),
              jnp.stack(g_branch).reshape(3, N_CHIPS, 2, D // N_CHIPS // 2, D),
              g_out.reshape(1, N_CHIPS, 2, D // N_CHIPS // 2, D),
              g_ffn_in.reshape(1, N_CHIPS, 2, D // 2, 2 * FH // N_CHIPS),
              g_ffn_out.reshape(1, N_CHIPS, 2, FH // N_CHIPS // 2, D)]
        red = _reduce_big_grads(g5, core_a, shard_a)
        big["w_in"][l] = red[0].reshape(D, C // N_CHIPS)
        big["w_branch"][l] = red[1].reshape(3, D // N_CHIPS, D)
        big["w_out"][l] = red[2].reshape(D // N_CHIPS, D)
        big["w_ffn_in"][l] = red[3].reshape(D, 2 * FH // N_CHIPS)
        big["w_ffn_out"][l] = red[4].reshape(FH // N_CHIPS, D)

        dsinks = jnp.sum(dsk, axis=0)[:, 0, :hb_b].reshape(H)
        bw = w_rg.shape[-1]
        small[l] = [lru[2].reshape(-1), lru[3].reshape(-1), _block_diag_grad(lru[4], bw).reshape(-1), lru[5].reshape(-1),
                    _block_diag_grad(lru[6], bw).reshape(-1), lru[7].reshape(-1), lru[8].reshape(-1),
                    jnp.pad(dsinks, (0, LANES - H)), dg1.reshape(-1), db1.reshape(-1), dg2.reshape(-1), db2.reshape(-1)]

    sizes = [t.size for t in small[0]]
    flat = jnp.concatenate([t for l in range(L) for t in small[l]])
    n_flat = flat.size
    rows = -(-n_flat // (LANES * 256)) * 256
    flat = jnp.pad(flat, (0, rows * LANES - n_flat)).reshape(rows, LANES)
    red_small = _allreduce_small(flat).reshape(-1)
    per_layer = sum(sizes)
    names = ["conv_w", "conv_b", "w_rg", "b_rg", "w_ig", "b_ig", "lru_lambda", "sinks", "ln1_g", "ln1_b", "ln2_g", "ln2_b"]
    sg = {nm: [] for nm in names}
    for l in range(L):
        p = l * per_layer
        for nm, sz in zip(names, sizes):
            sg[nm].append(red_small[p:p + sz])
            p += sz
    grads = dict(
        w_in=jnp.stack(big["w_in"]), w_branch=jnp.stack(big["w_branch"]), w_out=jnp.stack(big["w_out"]),
        w_ffn_in=jnp.stack(big["w_ffn_in"]), w_ffn_out=jnp.stack(big["w_ffn_out"]),
        conv_w=lax.dynamic_slice_in_dim(jnp.stack(sg["conv_w"]).reshape(L, CONV_WIDTH, D), shard * (D // N_CHIPS), D // N_CHIPS, axis=2),
        conv_b=jnp.stack(sg["conv_b"]), w_rg=jnp.stack(sg["w_rg"]).reshape(w_rg.shape), b_rg=jnp.stack(sg["b_rg"]),
        w_ig=jnp.stack(sg["w_ig"]).reshape(w_ig.shape), b_ig=jnp.stack(sg["b_ig"]), lru_lambda=jnp.stack(sg["lru_lambda"]),
        sinks=jnp.stack(sg["sinks"])[:, :H], ln1_g=jnp.stack(sg["ln1_g"]), ln1_b=jnp.stack(sg["ln1_b"]),
        ln2_g=jnp.stack(sg["ln2_g"]), ln2_b=jnp.stack(sg["ln2_b"]),
    )

    order = ["w_in", "conv_w", "conv_b", "w_rg", "b_rg", "w_ig", "b_ig", "lru_lambda", "sinks", "w_branch", "w_out",
             "ln1_g", "ln1_b", "w_ffn_in", "w_ffn_out", "ln2_g", "ln2_b"]
    weights = dict(w_in=w_in, conv_w=conv_w, conv_b=conv_b, w_rg=w_rg, b_rg=b_rg, w_ig=w_ig, b_ig=b_ig, lru_lambda=lru_lambda,
                   sinks=sinks, w_branch=w_branch, w_out=w_out, ln1_g=ln1_g, ln1_b=ln1_b, w_ffn_in=w_ffn_in,
                   w_ffn_out=w_ffn_out, ln2_g=ln2_g, ln2_b=ln2_b)
    ms = dict(w_in=m_w_in, conv_w=m_conv_w, conv_b=m_conv_b, w_rg=m_w_rg, b_rg=m_b_rg, w_ig=m_w_ig, b_ig=m_b_ig,
              lru_lambda=m_lru_lambda, sinks=m_sinks, w_branch=m_w_branch, w_out=m_w_out, ln1_g=m_ln1_g, ln1_b=m_ln1_b,
              w_ffn_in=m_w_ffn_in, w_ffn_out=m_w_ffn_out, ln2_g=m_ln2_g, ln2_b=m_ln2_b)
    vs = dict(w_in=v_w_in, conv_w=v_conv_w, conv_b=v_conv_b, w_rg=v_w_rg, b_rg=v_b_rg, w_ig=v_w_ig, b_ig=v_b_ig,
              lru_lambda=v_lru_lambda, sinks=v_sinks, w_branch=v_w_branch, w_out=v_w_out, ln1_g=v_ln1_g, ln1_b=v_ln1_b,
              w_ffn_in=v_w_ffn_in, w_ffn_out=v_w_ffn_out, ln2_g=v_ln2_g, ln2_b=v_ln2_b)
    deltas, new_m, new_v = {}, {}, {}
    for nm in order:
        deltas[nm], new_m[nm], new_v[nm] = _adamw(weights[nm], grads[nm], ms[nm], vs[nm])
    return (loss, dx.reshape(B, S, D), *[grads[nm] for nm in order], *[deltas[nm] for nm in order],
            *[new_m[nm] for nm in order], *[new_v[nm] for nm in order])
```

```python
import functools
import math

import jax
import jax.numpy as jnp
from jax import lax
from jax.experimental import pallas as pl
from jax.experimental.pallas import tpu as pltpu

HEAD_DIM = 64
WIN = 128
DILS = (1, 4, 16)
SWA_GROUP = 4
CONV_WIDTH = 4
LRU_C = 8.0
LN_EPS = 1e-5
NEG_INF = -1e30
N_CHIPS = 4
ADAM_LR, ADAM_B1, ADAM_B2, ADAM_EPS, ADAM_WD, ADAM_STEP = 0.001, 0.9, 0.999, 1e-08, 0.01, 10

LANES = 128
SUBLANES = 8
VMEM_LIMIT = 48 * 1024 * 1024

F32 = jnp.float32
BF16 = jnp.bfloat16
MESH = pl.DeviceIdType.MESH
ANY = pl.BlockSpec(memory_space=pl.ANY)


def _pcall(body, **kw):
    return pl.pallas_call(body, **kw)


def _pcall_comm(body, **kw):
    return pl.pallas_call(body, **kw)


def _params(*sem):
    return pltpu.CompilerParams(dimension_semantics=tuple(sem), vmem_limit_bytes=VMEM_LIMIT)


def _tile(dim, target):
    if dim <= target:
        return dim
    best = None
    for t in range(LANES, target + 1, LANES):
        if dim % t == 0:
            best = t
    assert best is not None, (dim, target)
    return best


def _sigmoid(x):
    return 1.0 / (1.0 + jnp.exp(-x))


def _dot(a, b, dims):
    return lax.dot_general(a, b, (dims, ((), ())), preferred_element_type=F32)


def _dot_nn(a, b):
    return _dot(a, b, ((1,), (0,)))


def _dot_nt(a, b):
    return _dot(a, b, ((1,), (1,)))


def _dot_tn(a, b):
    return _dot(a, b, ((0,), (0,)))


def _matmul(a, b, *, mode, name, out_dtype=F32, tm=512, tn=512, tk=2048, resid=None, rs=1.0, out_shards=0, n_outer=False):
    b_sh = b.ndim == 3
    if mode == "nn":
        M, K = a.shape
        N = b.shape[-1] * (b.shape[0] if b_sh else 1)
    elif mode == "nt":
        M, K = a.shape
        N = b.shape[-2]
    else:
        K, M = a.shape
        N = b.shape[-1]
    tm = _tile(M, tm)
    if mode == "nn" and b_sh:
        tn = b.shape[-1]
    elif out_shards:
        tn = N // out_shards
    else:
        tn = _tile(N, tn)
    if mode == "nt" and b_sh:
        tk = b.shape[-1]
    else:
        tk = _tile(K, tk)
    nk = K // tk
    grid = (N // tn, M // tm, nk) if n_outer else (M // tm, N // tn, nk)

    def spec(shape, f):
        return pl.BlockSpec(shape, (lambda g0, g1, k: f(g1, g0, k)) if n_outer else f)

    if mode == "nn":
        a_spec = spec((tm, tk), lambda i, j, k: (i, k))
        b_spec = spec((None, tk, tn), lambda i, j, k: (j, k, 0)) if b_sh else spec((tk, tn), lambda i, j, k: (k, j))
        contract = _dot_nn
    elif mode == "nt":
        a_spec = spec((tm, tk), lambda i, j, k: (i, k))
        b_spec = spec((None, tn, tk), lambda i, j, k: (k, j, 0)) if b_sh else spec((tn, tk), lambda i, j, k: (j, k))
        contract = _dot_nt
    else:
        a_spec = spec((tk, tm), lambda i, j, k: (k, i))
        b_spec = spec((tk, tn), lambda i, j, k: (k, j))
        contract = _dot_tn
    if out_shards:
        out_shape = jax.ShapeDtypeStruct((out_shards, M, tn), out_dtype)
        o_spec = spec((None, tm, tn), lambda i, j, k: (j, i, 0))
    else:
        out_shape = jax.ShapeDtypeStruct((M, N), out_dtype)
        o_spec = spec((tm, tn), lambda i, j, k: (i, j))
    in_specs = [a_spec, b_spec]
    args = [a, b]
    if resid is not None:
        in_specs.append(spec((tm, tn), lambda i, j, k: (i, j)))
        args.append(resid)

    def body(*refs):
        a_ref, b_ref = refs[:2]
        r_ref = refs[2] if resid is not None else None
        o_ref = refs[3] if resid is not None else refs[2]
        part = contract(a_ref[...].astype(BF16), b_ref[...].astype(BF16))

        def finish(res):
            if resid is not None:
                res = res + rs * r_ref[...]
            o_ref[...] = res.astype(out_dtype)

        if nk == 1:
            finish(part)
            return
        acc_ref = refs[-1]
        k = pl.program_id(2)

        @pl.when(k == 0)
        def _():
            acc_ref[...] = part

        @pl.when(jnp.logical_and(k > 0, k < nk - 1))
        def _():
            acc_ref[...] += part

        @pl.when(k == nk - 1)
        def _():
            finish(acc_ref[...] + part)

    return _pcall(
        body, name=name, grid=grid, in_specs=in_specs, out_specs=o_spec, out_shape=out_shape,
        scratch_shapes=[pltpu.VMEM((tm, tn), F32)] if nk > 1 else [],
        compiler_params=_params("parallel", "parallel", "arbitrary"),
    )(*args)


def _shift_down(x, d, row):
    return jnp.where(row >= d, pltpu.roll(x, d, 0), 0.0)


def _shift_up(x, d, row, n):
    return jnp.where(row < n - d, pltpu.roll(x, n - d, 0), 0.0)


def _log1p(u):
    w = 1.0 + u
    return jnp.where(w == 1.0, u, jnp.log(w) * u / (w - 1.0))


def _gelu_parts(g):
    k = math.sqrt(2.0 / math.pi)
    c = 0.044715
    t = jnp.tanh(k * (g + c * g * g * g))
    val = 0.5 * g * (1.0 + t)
    der = 0.5 * (1.0 + t) + 0.5 * g * (1.0 - t * t) * k * (1.0 + 3.0 * c * g * g)
    return val, der


def _lru_gates(xr, cw_ref, cb_ref, wrg_ref, brg_ref, wig_ref, big_ref, lam_ref, row):
    xc = cw_ref[3:4, :] * xr + cb_ref[...]
    for d in range(1, CONV_WIDTH):
        xc = xc + cw_ref[3 - d:4 - d, :] * _shift_down(xr, d, row)
    xcb = xc.astype(BF16)
    r = _sigmoid(_dot_nn(xcb, wrg_ref[...]) + brg_ref[...])
    ig = _sigmoid(_dot_nn(xcb, wig_ref[...]) + big_ref[...])
    lam = lam_ref[...]
    sp = jnp.maximum(-lam, 0.0) + _log1p(jnp.exp(-jnp.abs(lam)))
    log_a = (-LRU_C) * r * sp
    a = jnp.exp(log_a)
    y2 = 2.0 * log_a
    one_m_a2 = jnp.where(y2 > -0.01, -(y2 + 0.5 * y2 * y2 + (1.0 / 6.0) * y2 * y2 * y2), 1.0 - jnp.exp(y2))
    mult = jnp.sqrt(one_m_a2)
    return xc, r, ig, sp, a, mult


def _scan_local(a, b, row, n, reverse):
    sub = row % SUBLANES
    d = 1
    while d < SUBLANES:
        if reverse:
            keep = sub < SUBLANES - d
            a_s = jnp.where(keep, pltpu.roll(a, n - d, 0), 1.0)
            b_s = jnp.where(keep, pltpu.roll(b, n - d, 0), 0.0)
        else:
            keep = sub >= d
            a_s = jnp.where(keep, pltpu.roll(a, d, 0), 1.0)
            b_s = jnp.where(keep, pltpu.roll(b, d, 0), 0.0)
        b = a * b_s + b
        a = a * a_s
        d *= 2
    return a, b


def _scan_carry(a_ref, b_ref, out_ref, n, reverse):
    ng = n // SUBLANES

    def step(gidx, carry):
        g = (ng - 1 - gidx) if reverse else gidx
        rows = pl.ds(pl.multiple_of(g * SUBLANES, SUBLANES), SUBLANES)
        h = a_ref[rows, :] * carry + b_ref[rows, :]
        out_ref[rows, :] = h
        return h[0:1, :] if reverse else h[SUBLANES - 1:SUBLANES, :]

    lax.fori_loop(0, ng, step, jnp.zeros((1, LANES), F32), unroll=8)


def _lru_specs(B, S, D, C, x_off, g_off):
    nct = D // LANES
    seq = lambda off: pl.BlockSpec((None, S, LANES), lambda ct, b: (b, 0, off // LANES + ct))
    row = lambda r: pl.BlockSpec((r, LANES), lambda ct, b: (0, ct))
    wbd = pl.BlockSpec((None, LANES, LANES), lambda ct, b: (ct, 0, 0))
    return nct, seq, row, wbd


def _lru_fwd(proj3, lp, *, D, x_off, g_off):
    B, S, C = proj3.shape
    nct, seq, row, wbd = _lru_specs(B, S, D, C, x_off, g_off)

    def body(xr_ref, g_ref, cw_ref, cb_ref, wrg_ref, brg_ref, wig_ref, big_ref, lam_ref, h_ref, ya_ref, a_s, b_s):
        rowi = lax.broadcasted_iota(jnp.int32, (S, LANES), 0)
        xr = xr_ref[...]
        xc, r, ig, sp, a, mult = _lru_gates(xr, cw_ref, cb_ref, wrg_ref, brg_ref, wig_ref, big_ref, lam_ref, rowi)
        al, bl = _scan_local(a, mult * (ig * xc), rowi, S, False)
        a_s[...] = al
        b_s[...] = bl
        _scan_carry(a_s, b_s, h_ref, S, False)
        gel, _ = _gelu_parts(g_ref[...])
        ya_ref[...] = (h_ref[...] * gel).astype(BF16)

    out_seq = pl.BlockSpec((None, S, LANES), lambda ct, b: (b, 0, ct))
    return _pcall(
        body, name="lru_fwd", grid=(nct, B),
        in_specs=[seq(x_off), seq(g_off), row(CONV_WIDTH), row(1), wbd, row(1), wbd, row(1), row(1)],
        out_specs=[out_seq, out_seq],
        out_shape=[jax.ShapeDtypeStruct((B, S, D), F32), jax.ShapeDtypeStruct((B, S, D), BF16)],
        scratch_shapes=[pltpu.VMEM((S, LANES), F32), pltpu.VMEM((S, LANES), F32)],
        compiler_params=_params("parallel", "parallel"),
    )(proj3, proj3, lp["conv_w"], lp["conv_b"], lp["w_rg_bd"], lp["b_rg"], lp["w_ig_bd"], lp["b_ig"], lp["lam"])


def _lru_bwd(proj3, h3, dya3, lp, *, D, x_off, g_off):
    B, S, C = proj3.shape
    nct, seq, row, wbd = _lru_specs(B, S, D, C, x_off, g_off)

    def body(xr_ref, g_ref, h_ref, dy_ref, cw_ref, cb_ref, wrg_ref, brg_ref, wig_ref, big_ref, lam_ref,
             dxr_ref, dg_ref, dcw_ref, dcb_ref, dwrg_ref, dbrg_ref, dwig_ref, dbig_ref, dlam_ref, a_s, b_s, l_s):
        first = pl.program_id(1) == 0
        rowi = lax.broadcasted_iota(jnp.int32, (S, LANES), 0)
        xr = xr_ref[...]
        xc, r, ig, sp, a, mult = _lru_gates(xr, cw_ref, cb_ref, wrg_ref, brg_ref, wig_ref, big_ref, lam_ref, rowi)
        h = h_ref[...]
        dy = dy_ref[...]
        gel, dgel = _gelu_parts(g_ref[...])
        dg_ref[...] = (dy * h * dgel).astype(BF16)
        al, bl = _scan_local(_shift_up(a, 1, rowi, S), dy * gel, rowi, S, True)
        a_s[...] = al
        b_s[...] = bl
        _scan_carry(a_s, b_s, l_s, S, True)
        lamb = l_s[...]
        u = ig * xc
        da = lamb * _shift_down(h, 1, rowi)
        dlog_a = da * a - (lamb * u) * (a * a) / mult
        du = lamb * mult
        dpre_r = (dlog_a * ((-LRU_C) * sp)) * r * (1.0 - r)
        dpre_i = (du * xc) * ig * (1.0 - ig)
        dsp = jnp.sum(dlog_a * ((-LRU_C) * r), axis=0, keepdims=True)
        dlam = dsp * (-1.0 / (1.0 + jnp.exp(lam_ref[...])))
        dpr = dpre_r.astype(BF16)
        dpi = dpre_i.astype(BF16)
        dxc = du * ig + _dot_nt(dpr, wrg_ref[...]) + _dot_nt(dpi, wig_ref[...])
        xcb = xc.astype(BF16)
        dwrg = _dot_tn(xcb, dpr)
        dwig = _dot_tn(xcb, dpi)
        dxr = cw_ref[3:4, :] * dxc
        dcw = [jnp.sum(xr * dxc, axis=0, keepdims=True)]
        for d in range(1, CONV_WIDTH):
            dxr = dxr + cw_ref[3 - d:4 - d, :] * _shift_up(dxc, d, rowi, S)
            dcw.append(jnp.sum(_shift_down(xr, d, rowi) * dxc, axis=0, keepdims=True))
        dxr_ref[...] = dxr.astype(BF16)
        dcw_rows = jnp.concatenate(dcw[::-1], axis=0)
        sums = ((dcw_ref, dcw_rows), (dcb_ref, jnp.sum(dxc, axis=0, keepdims=True)), (dwrg_ref, dwrg),
                (dbrg_ref, jnp.sum(dpre_r, axis=0, keepdims=True)), (dwig_ref, dwig),
                (dbig_ref, jnp.sum(dpre_i, axis=0, keepdims=True)), (dlam_ref, dlam))

        @pl.when(first)
        def _():
            for ref, val in sums:
                ref[...] = val

        @pl.when(jnp.logical_not(first))
        def _():
            for ref, val in sums:
                ref[...] += val

    out_seq = pl.BlockSpec((None, S, LANES), lambda ct, b: (b, 0, ct))
    f = lambda shape: jax.ShapeDtypeStruct(shape, F32)
    nb = D // LANES
    return _pcall(
        body, name="lru_bwd", grid=(nct, B),
        in_specs=[seq(x_off), seq(g_off), out_seq, out_seq, row(CONV_WIDTH), row(1), wbd, row(1), wbd, row(1), row(1)],
        out_specs=[out_seq, out_seq, row(CONV_WIDTH), row(1), wbd, row(1), wbd, row(1), row(1)],
        out_shape=[jax.ShapeDtypeStruct((B, S, D), BF16), jax.ShapeDtypeStruct((B, S, D), BF16),
                   f((CONV_WIDTH, D)), f((1, D)), f((nb, LANES, LANES)), f((1, D)), f((nb, LANES, LANES)), f((1, D)), f((1, D))],
        scratch_shapes=[pltpu.VMEM((S, LANES), F32)] * 3,
        compiler_params=_params("parallel", "arbitrary"),
    )(proj3, proj3, h3, dya3, lp["conv_w"], lp["conv_b"], lp["w_rg_bd"], lp["b_rg"], lp["w_ig_bd"], lp["b_ig"], lp["lam"])


def _pair_stack(x, lo):
    z = jnp.zeros_like(x)
    return jnp.concatenate([jnp.where(lo, x, z), jnp.where(lo, z, x)], axis=0).astype(BF16)


def _pair_join(y2, lo):
    return jnp.where(lo, y2[:WIN], y2[WIN:])


def _pair_col(xb):
    return jnp.concatenate([xb[:, 0:1], xb[:, HEAD_DIM:HEAD_DIM + 1]], axis=0)


def _pair_bcast(col, lo):
    return jnp.where(lo, jnp.broadcast_to(col[:WIN], (WIN, LANES)), jnp.broadcast_to(col[WIN:], (WIN, LANES)))


def _dil_rows(it, d, S):
    if d == 1:
        cur = pl.multiple_of(it * WIN, WIN)
        prev = pl.multiple_of(jnp.maximum(it - 1, 0) * WIN, WIN)
        return pl.ds(cur, WIN), pl.ds(prev, WIN), it > 0
    r, i = it % d, it // d
    cur = i * (WIN * d) + r
    prev = jnp.maximum(i - 1, 0) * (WIN * d) + r
    return pl.ds(cur, WIN, stride=d), pl.ds(prev, WIN, stride=d), i > 0


def _dil_bias(two_blocks, stack=2):
    nk = 2 * WIN if two_blocks else WIN
    qi = lax.broadcasted_iota(jnp.int32, (stack * WIN, nk), 0) & (WIN - 1)
    kj = lax.broadcasted_iota(jnp.int32, (stack * WIN, nk), 1)
    if not two_blocks:
        return jnp.where(kj <= qi, 0.0, NEG_INF), None
    cur = jnp.logical_and(kj >= WIN, kj - WIN <= qi)
    prev = jnp.logical_and(kj < WIN, kj >= qi)
    return jnp.where(jnp.logical_or(cur, prev), 0.0, NEG_INF), jnp.where(cur, 0.0, NEG_INF)


def _dil_specs(B, S, D, C, offs):
    grid = (B, D // LANES)
    seq = lambda off: pl.BlockSpec((None, S, LANES), lambda b, p: (b, 0, off // LANES + p))
    return grid, [seq(o) for o in offs], seq(0)


def _dil_fwd(proj3, *, D, q_off, k_off, v_off):
    B, S, C = proj3.shape
    n_it = S // WIN
    scale = HEAD_DIM ** -0.5
    grid, in_specs, out_spec = _dil_specs(B, S, D, C, (q_off, k_off, v_off))

    def body(q_ref, k_ref, v_ref, o_ref, l_ref):
        lo = lax.broadcasted_iota(jnp.int32, (WIN, LANES), 1) < HEAD_DIM
        for c, d in enumerate(DILS):
            two = S // d > WIN
            bias_all, bias_first = _dil_bias(two)

            def step(it, _, c=c, d=d, two=two, bias_all=bias_all, bias_first=bias_first):
                cur, prev, later = _dil_rows(it, d, S)
                q2 = _pair_stack(q_ref[cur, :], lo)
                if two:
                    k2 = jnp.concatenate([k_ref[prev, :], k_ref[cur, :]], axis=0).astype(BF16)
                    v2 = jnp.concatenate([v_ref[prev, :], v_ref[cur, :]], axis=0).astype(BF16)
                    bias = jnp.where(later, bias_all, bias_first)
                else:
                    k2, v2, bias = k_ref[cur, :].astype(BF16), v_ref[cur, :].astype(BF16), bias_all
                s2 = _dot_nt(q2, k2) * scale + bias
                m2 = jnp.max(s2, axis=1, keepdims=True)
                p2 = jnp.exp(s2 - m2)
                den = jnp.sum(p2, axis=1, keepdims=True)
                oc = _pair_join(_dot_nn(p2.astype(BF16), v2) / den, lo)
                lc = _pair_bcast(m2 + jnp.log(den), lo)
                if c == 0:
                    o_ref[cur, :] = oc
                    l_ref[cur, :] = lc
                else:
                    l_old = l_ref[cur, :]
                    mx = jnp.maximum(l_old, lc)
                    e_old, e_new = jnp.exp(l_old - mx), jnp.exp(lc - mx)
                    tot = e_old + e_new
                    o_ref[cur, :] = (e_old * o_ref[cur, :] + e_new * oc) / tot
                    l_ref[cur, :] = mx + jnp.log(tot)
                return 0

            lax.fori_loop(0, n_it, step, 0, unroll=2)

    return _pcall(
        body, name="dil_fwd", grid=grid, in_specs=in_specs, out_specs=[out_spec, out_spec],
        out_shape=[jax.ShapeDtypeStruct((B, S, D), F32)] * 2,
        compiler_params=_params("parallel", "parallel"),
    )(proj3, proj3, proj3)


def _dil_bwd(proj3, o3, l3, do3, *, D, q_off, k_off, v_off):
    B, S, C = proj3.shape
    n_it = S // WIN
    scale = HEAD_DIM ** -0.5
    grid, in_specs, out_spec = _dil_specs(B, S, D, C, (q_off, k_off, v_off))

    def body(q_ref, k_ref, v_ref, o_ref, l_ref, do_ref, dq_ref, dk_ref, dv_ref, dd_s, dq_s, dk_s, dv_s):
        lo = lax.broadcasted_iota(jnp.int32, (WIN, LANES), 1) < HEAD_DIM
        lo_s = lax.broadcasted_iota(jnp.int32, (S, LANES), 1) < HEAD_DIM
        prod = do_ref[...] * o_ref[...]
        d_lo = jnp.sum(jnp.where(lo_s, prod, 0.0), axis=1, keepdims=True)
        d_hi = jnp.sum(jnp.where(lo_s, 0.0, prod), axis=1, keepdims=True)
        dd_s[...] = jnp.where(lo_s, jnp.broadcast_to(d_lo, (S, LANES)), jnp.broadcast_to(d_hi, (S, LANES)))
        dq_s[...] = jnp.zeros_like(dq_s)
        dk_s[...] = jnp.zeros_like(dk_s)
        dv_s[...] = jnp.zeros_like(dv_s)
        for d in DILS:
            two = S // d > WIN
            bias_all, bias_first = _dil_bias(two)

            def step(it, _, d=d, two=two, bias_all=bias_all, bias_first=bias_first):
                cur, prev, later = _dil_rows(it, d, S)
                q2 = _pair_stack(q_ref[cur, :], lo)
                do2 = _pair_stack(do_ref[cur, :], lo)
                l2 = _pair_col(l_ref[cur, :])
                dd2 = _pair_col(dd_s[cur, :])
                if two:
                    k2 = jnp.concatenate([k_ref[prev, :], k_ref[cur, :]], axis=0).astype(BF16)
                    v2 = jnp.concatenate([v_ref[prev, :], v_ref[cur, :]], axis=0).astype(BF16)
                    bias = jnp.where(later, bias_all, bias_first)
                else:
                    k2, v2, bias = k_ref[cur, :].astype(BF16), v_ref[cur, :].astype(BF16), bias_all
                p2 = jnp.exp(_dot_nt(q2, k2) * scale + bias - l2)
                ds2 = (p2 * (_dot_nt(do2, v2) - dd2) * scale).astype(BF16)
                dq_s[cur, :] += _pair_join(_dot_nn(ds2, k2), lo)
                dk2 = _dot_tn(ds2, q2)
                dv2 = _dot_tn(p2.astype(BF16), do2)
                if two:
                    dk_s[prev, :] += dk2[:WIN]
                    dv_s[prev, :] += dv2[:WIN]
                    dk_s[cur, :] += dk2[WIN:]
                    dv_s[cur, :] += dv2[WIN:]
                else:
                    dk_s[cur, :] += dk2
                    dv_s[cur, :] += dv2
                return 0

            lax.fori_loop(0, n_it, step, 0, unroll=2)
        dq_ref[...] = dq_s[...].astype(BF16)
        dk_ref[...] = dk_s[...].astype(BF16)
        dv_ref[...] = dv_s[...].astype(BF16)

    return _pcall(
        body, name="dil_bwd", grid=grid, in_specs=in_specs + [out_spec] * 3, out_specs=[out_spec] * 3,
        out_shape=[jax.ShapeDtypeStruct((B, S, D), BF16)] * 3,
        scratch_shapes=[pltpu.VMEM((S, LANES), F32)] * 4,
        compiler_params=_params("parallel", "parallel"),
    )(proj3, proj3, proj3, o3, l3, do3)


SWA_HB = 2 * SWA_GROUP


def _to_half(x, src, dst, lo):
    if src != dst:
        x = pltpu.roll(x, HEAD_DIM, 1)
    return jnp.where(lo if dst == 0 else jnp.logical_not(lo), x, 0.0)


def _swa_specs(B, S, D, C, q_off, k_off, v_off, n_steps):
    qw = SWA_HB * HEAD_DIM
    last = S // WIN - 1

    def blk(width, off, back):
        def imap(b, hh, i):
            ii = jnp.minimum(i, last)
            if back == 1:
                ii = jnp.maximum(ii - 1, 0)
            elif back == 2:
                ii = jnp.maximum(i - 1, 0)
            return (b, ii, off // width + hh)
        return pl.BlockSpec((None, WIN, width), imap)

    assert q_off % qw == 0 and k_off % LANES == 0 and v_off % LANES == 0 and D % qw == 0
    grid = (B, D // qw, n_steps)
    sink = pl.BlockSpec((None, 1, LANES), lambda b, hh, i: (hh, 0, 0))
    return grid, blk, sink, qw


def _swa_stack(ref, j, lo, dtype):
    parts = []
    for h in range(j * SWA_GROUP, (j + 1) * SWA_GROUP):
        g = h // 2
        parts.append(_to_half(ref[:, g * LANES:(g + 1) * LANES], h % 2, j, lo))
    return jnp.concatenate(parts, axis=0).astype(dtype)


def _swa_unstack(y4, j, lo):
    out = []
    for t in range(0, SWA_GROUP, 2):
        even = _to_half(y4[t * WIN:(t + 1) * WIN], j, 0, lo)
        odd = _to_half(y4[(t + 1) * WIN:(t + 2) * WIN], j, 1, lo)
        out.append(even + odd)
    return out


def _swa_cols(ref, j):
    cols = [jnp.broadcast_to(ref[:, h:h + 1], (WIN, 1)) for h in range(j * SWA_GROUP, (j + 1) * SWA_GROUP)]
    return jnp.concatenate(cols, axis=0)


def _swa_fwd(proj3, sinks, *, D, q_off, k_off, v_off):
    B, S, C = proj3.shape
    scale = HEAD_DIM ** -0.5
    grid, blk, sink, qw = _swa_specs(B, S, D, C, q_off, k_off, v_off, S // WIN)
    nhb = D // qw

    def body(q_ref, kp_ref, kc_ref, vp_ref, vc_ref, sk_ref, o_ref, lse_ref):
        lo = lax.broadcasted_iota(jnp.int32, (WIN, LANES), 1) < HEAD_DIM
        lane = lax.broadcasted_iota(jnp.int32, (WIN, LANES), 1)
        bias_all, bias_first = _dil_bias(True, SWA_GROUP)
        bias = jnp.where(pl.program_id(2) > 0, bias_all, bias_first)
        k2 = jnp.concatenate([kp_ref[...], kc_ref[...]], axis=0).astype(BF16)
        v2 = jnp.concatenate([vp_ref[...], vc_ref[...]], axis=0).astype(BF16)
        lse_acc = jnp.zeros((WIN, LANES), F32)
        for j in range(2):
            q4 = _swa_stack(q_ref, j, lo, BF16)
            sk4 = _swa_cols(sk_ref, j)
            s4 = _dot_nt(q4, k2) * scale + bias
            m = jnp.maximum(jnp.max(s4, axis=1, keepdims=True), sk4)
            p = jnp.exp(s4 - m)
            den = jnp.sum(p, axis=1, keepdims=True) + jnp.exp(sk4 - m)
            o4 = _dot_nn(p.astype(BF16), v2) / den
            l4 = m + jnp.log(den)
            for t, grp in enumerate(_swa_unstack(o4, j, lo)):
                g = j * (SWA_GROUP // 2) + t
                o_ref[:, g * LANES:(g + 1) * LANES] = grp
            for t in range(SWA_GROUP):
                lse_acc = jnp.where(lane == j * SWA_GROUP + t, l4[t * WIN:(t + 1) * WIN], lse_acc)
        lse_ref[...] = lse_acc

    o, lse = _pcall(
        body, name="swa_fwd", grid=grid,
        in_specs=[blk(qw, q_off, 0), blk(LANES, k_off, 1), blk(LANES, k_off, 0), blk(LANES, v_off, 1), blk(LANES, v_off, 0), sink],
        out_specs=[blk(qw, 0, 0), blk(LANES, 0, 0)],
        out_shape=[jax.ShapeDtypeStruct((B, S, D), F32), jax.ShapeDtypeStruct((B, S, nhb * LANES), F32)],
        compiler_params=_params("parallel", "parallel", "parallel"),
    )(proj3, proj3, proj3, proj3, proj3, sinks)
    return o, lse


def _swa_bwd(proj3, o3, lse3, do3, sinks, *, D, q_off, k_off, v_off):
    B, S, C = proj3.shape
    nblk = S // WIN
    scale = HEAD_DIM ** -0.5
    grid, blk, sink, qw = _swa_specs(B, S, D, C, q_off, k_off, v_off, nblk + 1)
    nhb = D // qw
    KV = D // SWA_GROUP

    def body(q_ref, kp_ref, kc_ref, vp_ref, vc_ref, o_ref, l_ref, do_ref, sk_ref, dq_ref, dk_ref, dv_ref, dsk_ref, ck_ref, cv_ref):
        i = pl.program_id(2)

        @pl.when(i == 0)
        def _():
            ck_ref[...] = jnp.zeros_like(ck_ref)
            cv_ref[...] = jnp.zeros_like(cv_ref)
            dsk_ref[...] = jnp.zeros_like(dsk_ref)

        @pl.when(i < nblk)
        def _():
            lo = lax.broadcasted_iota(jnp.int32, (WIN, LANES), 1) < HEAD_DIM
            lane = lax.broadcasted_iota(jnp.int32, (1, LANES), 1)
            bias_all, bias_first = _dil_bias(True, SWA_GROUP)
            bias = jnp.where(i > 0, bias_all, bias_first)
            k2 = jnp.concatenate([kp_ref[...], kc_ref[...]], axis=0).astype(BF16)
            v2 = jnp.concatenate([vp_ref[...], vc_ref[...]], axis=0).astype(BF16)
            lane_w = lax.broadcasted_iota(jnp.int32, (WIN, LANES), 1)
            dd = jnp.zeros((WIN, LANES), F32)
            for g in range(qw // LANES):
                prod = do_ref[:, g * LANES:(g + 1) * LANES] * o_ref[:, g * LANES:(g + 1) * LANES]
                dd = jnp.where(lane_w == 2 * g, jnp.sum(jnp.where(lo, prod, 0.0), axis=1, keepdims=True), dd)
                dd = jnp.where(lane_w == 2 * g + 1, jnp.sum(jnp.where(lo, 0.0, prod), axis=1, keepdims=True), dd)
            dk2 = jnp.zeros((2 * WIN, LANES), F32)
            dv2 = jnp.zeros((2 * WIN, LANES), F32)
            dsk_acc = jnp.zeros((1, LANES), F32)
            for j in range(2):
                q4 = _swa_stack(q_ref, j, lo, BF16)
                do4 = _swa_stack(do_ref, j, lo, BF16)
                heads = range(j * SWA_GROUP, (j + 1) * SWA_GROUP)
                l4 = jnp.concatenate([l_ref[:, h:h + 1] for h in heads], axis=0)
                dd4 = jnp.concatenate([dd[:, h:h + 1] for h in heads], axis=0)
                p4 = jnp.exp(_dot_nt(q4, k2) * scale + bias - l4)
                ds4 = (p4 * (_dot_nt(do4, v2) - dd4) * scale).astype(BF16)
                for t, grp in enumerate(_swa_unstack(_dot_nn(ds4, k2), j, lo)):
                    g = j * (SWA_GROUP // 2) + t
                    dq_ref[:, g * LANES:(g + 1) * LANES] = grp.astype(BF16)
                dk2 = dk2 + _dot_tn(ds4, q4)
                dv2 = dv2 + _dot_tn(p4.astype(BF16), do4)
                for h in heads:
                    dsk_h = -jnp.sum(jnp.exp(sk_ref[:, h:h + 1] - l_ref[:, h:h + 1]) * dd[:, h:h + 1], axis=0, keepdims=True)
                    dsk_acc = jnp.where(lane == h, dsk_h, dsk_acc)
            dk_ref[...] = (dk2[:WIN] + ck_ref[...]).astype(BF16)
            dv_ref[...] = (dv2[:WIN] + cv_ref[...]).astype(BF16)
            ck_ref[...] = dk2[WIN:]
            cv_ref[...] = dv2[WIN:]
            dsk_ref[...] += dsk_acc

        @pl.when(i == nblk)
        def _():
            dk_ref[...] = ck_ref[...].astype(BF16)
            dv_ref[...] = cv_ref[...].astype(BF16)

    res = _pcall(
        body, name="swa_bwd", grid=grid,
        in_specs=[blk(qw, q_off, 0), blk(LANES, k_off, 1), blk(LANES, k_off, 0), blk(LANES, v_off, 1), blk(LANES, v_off, 0),
                  blk(qw, 0, 0), blk(LANES, 0, 0), blk(qw, 0, 0), sink],
        out_specs=[blk(qw, 0, 0), blk(LANES, 0, 2), blk(LANES, 0, 2),
                   pl.BlockSpec((None, None, 1, LANES), lambda b, hh, i: (b, hh, 0, 0))],
        out_shape=[jax.ShapeDtypeStruct((B, S, D), BF16), jax.ShapeDtypeStruct((B, S, KV), BF16),
                   jax.ShapeDtypeStruct((B, S, KV), BF16), jax.ShapeDtypeStruct((B, nhb, 1, LANES), F32)],
        scratch_shapes=[pltpu.VMEM((WIN, LANES), F32), pltpu.VMEM((WIN, LANES), F32)],
        compiler_params=_params("parallel", "parallel", "arbitrary"),
    )(proj3, proj3, proj3, proj3, proj3, o3, lse3, do3, sinks)
    return res


def _branch_fwd(ys, wb, proj, *, D, g_off):
    T = proj.shape[0]
    tm, tn = _tile(T, 256), _tile(D, 512)
    n = len(ys)

    def body(*refs):
        y_refs, w_ref, g_refs, br_ref, mg_ref = refs[:n], refs[n], refs[n + 1:2 * n + 1], refs[2 * n + 1], refs[2 * n + 2]
        acc = None
        for k in range(n):
            br = _dot_nn(y_refs[k][...].astype(BF16), w_ref[k])
            br_ref[k] = br
            term = _sigmoid(g_refs[k][...]) * br
            acc = term if acc is None else acc + term
        mg_ref[...] = acc.astype(BF16)

    gate = lambda k: pl.BlockSpec((tm, tn), lambda i, j: (i, (g_off + k * D) // tn + j))
    return _pcall(
        body, name="branch_fwd", grid=(T // tm, D // tn),
        in_specs=[pl.BlockSpec((tm, D), lambda i, j: (i, 0))] * n + [pl.BlockSpec((n, D, tn), lambda i, j: (0, 0, j))]
        + [gate(k) for k in range(n)],
        out_specs=[pl.BlockSpec((n, tm, tn), lambda i, j: (0, i, j)), pl.BlockSpec((tm, tn), lambda i, j: (i, j))],
        out_shape=[jax.ShapeDtypeStruct((n, T, D), F32), jax.ShapeDtypeStruct((T, D), BF16)],
        compiler_params=_params("parallel", "parallel"),
    )(*ys, wb, *([proj] * n))


def _branch_bwd(dmerged, branch, proj, *, D, g_off):
    n, T, _ = branch.shape
    tm, tn = _tile(T, 512), _tile(D, 512)

    def body(dm_ref, br_ref, *rest):
        g_refs, db_ref, dg_refs = rest[:n], rest[n], rest[n + 1:]
        dm = dm_ref[...]
        for k in range(n):
            sg = _sigmoid(g_refs[k][...])
            db_ref[k] = (sg * dm).astype(BF16)
            dg_refs[k][...] = (dm * br_ref[k] * sg * (1.0 - sg)).astype(BF16)

    gate = lambda k: pl.BlockSpec((tm, tn), lambda i, j: (i, (g_off + k * D) // tn + j))
    blk = pl.BlockSpec((tm, tn), lambda i, j: (i, j))
    res = _pcall(
        body, name="branch_bwd", grid=(T // tm, D // tn),
        in_specs=[blk, pl.BlockSpec((n, tm, tn), lambda i, j: (0, i, j))] + [gate(k) for k in range(n)],
        out_specs=[pl.BlockSpec((n, tm, tn), lambda i, j: (0, i, j))] + [blk] * n,
        out_shape=[jax.ShapeDtypeStruct((n, T, D), BF16)] + [jax.ShapeDtypeStruct((T, D), BF16)] * n,
        compiler_params=_params("parallel", "parallel"),
    )(dmerged, branch, *([proj] * n))
    return res[0], list(res[1:])


def _ln_fwd(xres, y, g, b, *, alpha):
    T, D = xres.shape
    tm = _tile(T, 512)

    def body(x_ref, y_ref, g_ref, b_ref, z_ref, o_ref):
        z = alpha * x_ref[...] + y_ref[...]
        mu = jnp.mean(z, axis=1, keepdims=True)
        zc = z - mu
        var = jnp.mean(zc * zc, axis=1, keepdims=True)
        z_ref[...] = z
        o_ref[...] = zc * lax.rsqrt(var + LN_EPS) * g_ref[...] + b_ref[...]

    blk = pl.BlockSpec((tm, D), lambda i: (i, 0))
    vec = pl.BlockSpec((1, D), lambda i: (0, 0))
    return _pcall(
        body, name="ln_fwd", grid=(T // tm,), in_specs=[blk, blk, vec, vec], out_specs=[blk, blk],
        out_shape=[jax.ShapeDtypeStruct((T, D), F32)] * 2, compiler_params=_params("parallel"),
    )(xres, y, g, b)


def _ln_bwd(dout, z, g):
    T, D = z.shape
    tm = _tile(T, 512)

    def body(do_ref, z_ref, g_ref, dz_ref, dg_ref, db_ref):
        z = z_ref[...]
        do = do_ref[...]
        mu = jnp.mean(z, axis=1, keepdims=True)
        zc = z - mu
        rstd = lax.rsqrt(jnp.mean(zc * zc, axis=1, keepdims=True) + LN_EPS)
        xhat = zc * rstd
        dxh = do * g_ref[...]
        dz_ref[...] = rstd * (dxh - jnp.mean(dxh, axis=1, keepdims=True) - xhat * jnp.mean(dxh * xhat, axis=1, keepdims=True))
        dg = jnp.sum(do * xhat, axis=0, keepdims=True)
        db = jnp.sum(do, axis=0, keepdims=True)
        first = pl.program_id(0) == 0

        @pl.when(first)
        def _():
            dg_ref[...] = dg
            db_ref[...] = db

        @pl.when(jnp.logical_not(first))
        def _():
            dg_ref[...] += dg
            db_ref[...] += db

    blk = pl.BlockSpec((tm, D), lambda i: (i, 0))
    vec = pl.BlockSpec((1, D), lambda i: (0, 0))
    return _pcall(
        body, name="ln_bwd", grid=(T // tm,), in_specs=[blk, blk, vec], out_specs=[blk, vec, vec],
        out_shape=[jax.ShapeDtypeStruct((T, D), F32), jax.ShapeDtypeStruct((1, D), F32), jax.ShapeDtypeStruct((1, D), F32)],
        compiler_params=_params("arbitrary"),
    )(dout, z, g)


def _swiglu_fwd(hh):
    T, F2 = hh.shape
    Fh = F2 // 2
    tm, tn = _tile(T, 256), _tile(Fh, 1408)
    nj = Fh // tn

    def body(h1_ref, h3_ref, f_ref):
        h1 = h1_ref[...]
        f_ref[...] = (h1 * _sigmoid(h1) * h3_ref[...]).astype(BF16)

    return _pcall(
        body, name="swiglu_fwd", grid=(T // tm, nj),
        in_specs=[pl.BlockSpec((tm, tn), lambda i, j: (i, j)), pl.BlockSpec((tm, tn), lambda i, j: (i, nj + j))],
        out_specs=pl.BlockSpec((tm, tn), lambda i, j: (i, j)),
        out_shape=jax.ShapeDtypeStruct((T, Fh), BF16), compiler_params=_params("parallel", "parallel"),
    )(hh, hh)


def _swiglu_bwd(hh, df):
    T, F2 = hh.shape
    Fh = F2 // 2
    tm, tn = _tile(T, 256), _tile(Fh, 1408)
    nj = Fh // tn

    def body(h1_ref, h3_ref, df_ref, d1_ref, d3_ref):
        h1 = h1_ref[...]
        sg = _sigmoid(h1)
        d = df_ref[...]
        d1_ref[...] = (d * h3_ref[...] * sg * (1.0 + h1 * (1.0 - sg))).astype(BF16)
        d3_ref[...] = (d * h1 * sg).astype(BF16)

    lo = pl.BlockSpec((tm, tn), lambda i, j: (i, j))
    hi = pl.BlockSpec((tm, tn), lambda i, j: (i, nj + j))
    d1, d3 = _pcall(
        body, name="swiglu_bwd", grid=(T // tm, nj), in_specs=[lo, hi, lo], out_specs=[lo, lo],
        out_shape=[jax.ShapeDtypeStruct((T, Fh), BF16)] * 2, compiler_params=_params("parallel", "parallel"),
    )(hh, hh, df)
    return jnp.concatenate([d1, d3], axis=1)


def _loss_head(y, target):
    T, D = y.shape
    tm = _tile(T, 512)

    def body(y_ref, t_ref, dy_ref, l_ref):
        e = y_ref[...] - t_ref[...]
        dy_ref[...] = e * (1.0 / D)
        sq = e * e
        part = sq[:, 0:LANES]
        for c in range(1, D // LANES):
            part = part + sq[:, c * LANES:(c + 1) * LANES]
        part = jnp.sum(part, axis=0, keepdims=True) * (0.5 / D)
        first = pl.program_id(0) == 0

        @pl.when(first)
        def _():
            l_ref[...] = part

        @pl.when(jnp.logical_not(first))
        def _():
            l_ref[...] += part

    blk = pl.BlockSpec((tm, D), lambda i: (i, 0))
    return _pcall(
        body, name="loss_head", grid=(T // tm,), in_specs=[blk, blk],
        out_specs=[blk, pl.BlockSpec((1, LANES), lambda i: (0, 0))],
        out_shape=[jax.ShapeDtypeStruct((T, D), F32), jax.ShapeDtypeStruct((1, LANES), F32)],
        compiler_params=_params("arbitrary"),
    )(y, target)


def _as_rows(a):
    return a.reshape(-1, a.shape[-1])


def _adamw(w, g, m, v):
    w2, g2, m2, v2 = (_as_rows(t) for t in (w, g, m, v))
    R, Cc = w2.shape
    cap = max(SUBLANES, min(512, (256 * 1024) // Cc))
    tm = R if (R <= cap or R % SUBLANES) else max(t for t in range(SUBLANES, cap + 1, SUBLANES) if R % t == 0)
    c1 = 1.0 - ADAM_B1 ** ADAM_STEP
    c2 = 1.0 - ADAM_B2 ** ADAM_STEP

    def body(w_ref, g_ref, m_ref, v_ref, d_ref, nm_ref, nv_ref):
        gg = g_ref[...]
        nm = ADAM_B1 * m_ref[...] + (1.0 - ADAM_B1) * gg
        nv = ADAM_B2 * v_ref[...] + (1.0 - ADAM_B2) * (gg * gg)
        d_ref[...] = (-ADAM_LR) * ((nm / c1) / (jnp.sqrt(nv / c2) + ADAM_EPS) + ADAM_WD * w_ref[...])
        nm_ref[...] = nm
        nv_ref[...] = nv

    blk = pl.BlockSpec((tm, Cc), lambda i: (i, 0))
    res = _pcall(
        body, name="adamw", grid=(R // tm,), in_specs=[blk] * 4, out_specs=[blk] * 3,
        out_shape=[jax.ShapeDtypeStruct((R, Cc), F32)] * 3, compiler_params=_params("parallel"),
    )(w2, g2, m2, v2)
    return tuple(t.reshape(w.shape) for t in res)


def _where_am_i():
    x, y, c = lax.axis_index("x"), lax.axis_index("y"), lax.axis_index("c")
    chips = [(1 - x, y), (x, 1 - y), (1 - x, 1 - y)]
    return x, y, c, chips


def _remote(src, dst, send_sems, recv_sems, k, to):
    return pltpu.make_async_remote_copy(src_ref=src, dst_ref=dst, send_sem=send_sems.at[k], recv_sem=recv_sems.at[k],
                                        device_id=to, device_id_type=MESH)


def _comm_call(body, name, ins, out_shapes, n_remote, n_local):
    return _pcall_comm(
        body, name=name, in_specs=[ANY] * len(ins), out_specs=[ANY] * len(out_shapes), out_shape=out_shapes,
        scratch_shapes=[pltpu.SemaphoreType.DMA((n_remote,)), pltpu.SemaphoreType.DMA((n_remote,)),
                        pltpu.SemaphoreType.DMA((max(n_local, 1),))],
    )(*ins)


def _gather_weights(shards):
    n = len(shards)

    def body(*refs):
        ins, outs = refs[:n], refs[n:2 * n]
        send_sems, recv_sems, local_sems = refs[2 * n:]
        x, y, c, chips = _where_am_i()
        s = 2 * x + y
        sib = (x, y, 1 - c)
        first = []
        for t in range(n):
            for j, (cx, cy) in enumerate(chips):
                first.append(_remote(ins[t].at[:, c], outs[t].at[:, s, c], send_sems, recv_sems, 6 * t + j, (cx, cy, c)))
        for cp in first:
            cp.start()
        passed = []
        for j, (cx, cy) in enumerate(chips):
            sj = 2 * cx + cy
            for t in range(n):
                land = outs[t].at[:, sj, c]
                _remote(land, land, send_sems, recv_sems, 6 * t + j, (cx, cy, c)).wait_recv()
                fw = _remote(land, land, send_sems, recv_sems, 6 * t + 3 + j, sib)
                fw.start()
                passed.append(fw)
        for j, (cx, cy) in enumerate(chips):
            sj = 2 * cx + cy
            for t in range(n):
                land = outs[t].at[:, sj, 1 - c]
                _remote(land, land, send_sems, recv_sems, 6 * t + 3 + j, sib).wait_recv()
        for cp in first + passed:
            cp.wait_send()

    out_shapes = [jax.ShapeDtypeStruct((t.shape[0], N_CHIPS) + t.shape[1:], t.dtype) for t in shards]
    got = _comm_call(body, "gather_weights", shards, out_shapes, 6 * n, 0)
    s = 2 * lax.axis_index("x") + lax.axis_index("y")
    return [lax.dynamic_update_slice(g, t[:, None], (0, s, 0, 0, 0)) for g, t in zip(got, shards)]


def _gather_small(v):
    def body(v_ref, out_ref, send_sems, recv_sems, local_sems):
        x, y, c, chips = _where_am_i()
        s = 2 * x + y
        mine = pltpu.make_async_copy(v_ref, out_ref.at[s], local_sems.at[0])
        mine.start()
        sends = [_remote(v_ref, out_ref.at[s], send_sems, recv_sems, j, (cx, cy, c)) for j, (cx, cy) in enumerate(chips)]
        for cp in sends:
            cp.start()
        for j, (cx, cy) in enumerate(chips):
            land = out_ref.at[2 * cx + cy]
            _remote(land, land, send_sems, recv_sems, j, (cx, cy, c)).wait_recv()
        for cp in sends:
            cp.wait_send()
        mine.wait()

    return _comm_call(body, "gather_small", [v], [jax.ShapeDtypeStruct((N_CHIPS,) + v.shape, v.dtype)], 3, 1)[0]


def _swap_sibling_halves(grads):
    n = len(grads)

    def body(*refs):
        ins, outs = refs[:n], refs[n:2 * n]
        send_sems, recv_sems, _ = refs[2 * n:]
        x, y, c, _chips = _where_am_i()
        sib = (x, y, 1 - c)
        cps = [_remote(ins[t].at[:, :, 1 - c], outs[t], send_sems, recv_sems, t, sib) for t in range(n)]
        for cp in cps:
            cp.start()
        for cp in cps:
            cp.wait()

    out_shapes = [jax.ShapeDtypeStruct(g.shape[:2] + g.shape[3:], g.dtype) for g in grads]
    return _comm_call(body, "grad_swap_halves", grads, out_shapes, n, 0)


def _exchange_chips(parts):
    n = len(parts)

    def body(*refs):
        ins, outs = refs[:n], refs[n:2 * n]
        send_sems, recv_sems, _ = refs[2 * n:]
        x, y, c, chips = _where_am_i()
        cps = []
        for t in range(n):
            for j, (cx, cy) in enumerate(chips):
                cps.append(_remote(ins[t].at[:, 2 * cx + cy], outs[t].at[j], send_sems, recv_sems, 3 * t + j, (cx, cy, c)))
        for cp in cps:
            cp.start()
        for cp in cps:
            cp.wait()

    out_shapes = [jax.ShapeDtypeStruct((3, p.shape[0]) + p.shape[2:], p.dtype) for p in parts]
    return _comm_call(body, "grad_exchange_chips", parts, out_shapes, 3 * n, 0)


def _join_sibling_halves(halves):
    n = len(halves)

    def body(*refs):
        ins, outs = refs[:n], refs[n:2 * n]
        send_sems, recv_sems, local_sems = refs[2 * n:]
        x, y, c, _chips = _where_am_i()
        sib = (x, y, 1 - c)
        cps = [_remote(ins[t], outs[t].at[:, c], send_sems, recv_sems, t, sib) for t in range(n)]
        for cp in cps:
            cp.start()
        for t in range(n):
            land = outs[t].at[:, 1 - c]
            _remote(land, land, send_sems, recv_sems, t, sib).wait_recv()
        for cp in cps:
            cp.wait_send()

    out_shapes = [jax.ShapeDtypeStruct((h.shape[0], 2) + h.shape[1:], h.dtype) for h in halves]
    got = _comm_call(body, "grad_join_halves", halves, out_shapes, n, 0)
    c = lax.axis_index("c")
    return [lax.dynamic_update_slice(g, h[:, None], (0, c, 0, 0)) for g, h in zip(got, halves)]


def _swap_small(v):
    def body(v_ref, out_ref, send_sems, recv_sems, _):
        x, y, c, _chips = _where_am_i()
        cp = _remote(v_ref, out_ref, send_sems, recv_sems, 0, (x, y, 1 - c))
        cp.start()
        cp.wait()

    return _comm_call(body, "small_swap", [v], [jax.ShapeDtypeStruct(v.shape, v.dtype)], 1, 0)[0]


def _exchange_small(v):
    def body(v_ref, out_ref, send_sems, recv_sems, _):
        x, y, c, chips = _where_am_i()
        cps = [_remote(v_ref, out_ref.at[j], send_sems, recv_sems, j, (cx, cy, c)) for j, (cx, cy) in enumerate(chips)]
        for cp in cps:
            cp.start()
        for cp in cps:
            cp.wait()

    return _comm_call(body, "small_exchange", [v], [jax.ShapeDtypeStruct((3,) + v.shape, v.dtype)], 3, 0)[0]


def _sum_rows(name, terms, out_dtypes):
    R, Cc = terms[0].shape
    tm = R if R <= 256 else max(t for t in range(16, 257, 16) if R % t == 0)
    n = len(terms)

    def body(*refs):
        acc = refs[0][...].astype(F32)
        for r in refs[1:n]:
            acc = acc + r[...].astype(F32)
        for o in refs[n:]:
            o[...] = acc.astype(o.dtype)

    blk = pl.BlockSpec((tm, Cc), lambda i: (i, 0))
    return _pcall(
        body, name=name, grid=(R // tm,), in_specs=[blk] * n, out_specs=[blk] * len(out_dtypes),
        out_shape=[jax.ShapeDtypeStruct((R, Cc), d) for d in out_dtypes], compiler_params=_params("parallel"),
    )(*terms)


def _pair_sum(g5, r1, core, shard):
    A4, _, Rh, Cc = g5.shape
    A = A4 // N_CHIPS
    tr = Rh if Rh <= 256 else max(t for t in range(16, 257, 16) if Rh % t == 0)

    def body(core_ref, shard_ref, g_ref, r_ref, qb_ref, qf_ref):
        q = g_ref[...] + r_ref[...]
        qb_ref[...] = q.astype(BF16)

        @pl.when(pl.program_id(2) == shard_ref[0])
        def _():
            qf_ref[...] = q

    grid_spec = pltpu.PrefetchScalarGridSpec(
        num_scalar_prefetch=2, grid=(A, Rh // tr, N_CHIPS),
        in_specs=[pl.BlockSpec((None, None, tr, Cc), lambda a, r, sh, core, shard: (a * N_CHIPS + sh, core[0], r, 0)),
                  pl.BlockSpec((None, tr, Cc), lambda a, r, sh, core, shard: (a * N_CHIPS + sh, r, 0))],
        out_specs=[pl.BlockSpec((None, tr, Cc), lambda a, r, sh, core, shard: (a * N_CHIPS + sh, r, 0)),
                   pl.BlockSpec((None, tr, Cc), lambda a, r, sh, core, shard: (a, r, 0))],
    )
    return _pcall(
        body, name="grad_pair_sum", grid_spec=grid_spec,
        out_shape=[jax.ShapeDtypeStruct((A4, Rh, Cc), BF16), jax.ShapeDtypeStruct((A, Rh, Cc), F32)],
        compiler_params=_params("parallel", "parallel", "arbitrary"),
    )(core, shard, g5, r1)


def _reduce_big_grads(grads, core, shard):
    r1 = _swap_sibling_halves(grads)
    qb, qf = [], []
    for g, r in zip(grads, r1):
        A, _, _, Rh, Cc = g.shape
        b, f = _pair_sum(g.reshape(A * N_CHIPS, 2, Rh, Cc), r.reshape(A * N_CHIPS, Rh, Cc), core, shard)
        qb.append(b.reshape(A, N_CHIPS, Rh, Cc))
        qf.append(f)
    r2 = _exchange_chips(qb)
    halves = []
    for f, r in zip(qf, r2):
        A, Rh, Cc = f.shape
        terms = [f.reshape(A * Rh, Cc)] + [r[j].reshape(A * Rh, Cc) for j in range(3)]
        halves.append(_sum_rows("grad_chip_sum", terms, [F32])[0].reshape(A, Rh, Cc))
    full = _join_sibling_halves(halves)
    return [t.reshape(t.shape[0], 2 * t.shape[2], t.shape[3]) for t in full]


def _allreduce_small(v):
    pair = _sum_rows("small_pair_sum", [v, _swap_small(v)], [F32])[0]
    others = _exchange_small(pair)
    x, y = lax.axis_index("x"), lax.axis_index("y")
    s = 2 * x + y
    stack = jnp.concatenate([pair[None], others], axis=0)
    src = jnp.stack([s, s ^ 2, s ^ 1, s ^ 3])
    order = jnp.argsort(src)
    terms = [lax.dynamic_index_in_dim(stack, order[k], 0, keepdims=False) for k in range(N_CHIPS)]
    return _sum_rows("small_chip_sum", terms, [F32])[0]


def _block_diag(w):
    nb, bw, _ = w.shape
    per = LANES // bw
    w = w.reshape(nb // per, per, bw, bw)
    eye = jnp.eye(per, dtype=w.dtype)
    bd = jnp.einsum("tpij,pq->tpiqj", w, eye).reshape(nb // per, LANES, LANES)
    return bd.astype(BF16)


def _block_diag_grad(g, bw):
    nt = g.shape[0]
    per = LANES // bw
    g = g.reshape(nt, per, bw, per, bw)
    return jnp.stack([g[:, p, :, p, :] for p in range(per)], axis=1).reshape(nt * per, bw, bw)


def _split5(w):
    R, Cc = w.shape[-2:]
    return w.reshape(-1, 2, R // 2, Cc)


def kernel(x, w_in, conv_w, conv_b, w_rg, b_rg, w_ig, b_ig, lru_lambda, sinks, w_branch, w_out, ln1_g, ln1_b, w_ffn_in, w_ffn_out, ln2_g, ln2_b, loss_target, m_w_in, m_conv_w, m_conv_b, m_w_rg, m_b_rg, m_w_ig, m_b_ig, m_lru_lambda, m_sinks, m_w_branch, m_w_out, m_ln1_g, m_ln1_b, m_w_ffn_in, m_w_ffn_out, m_ln2_g, m_ln2_b, v_w_in, v_conv_w, v_conv_b, v_w_rg, v_b_rg, v_w_ig, v_b_ig, v_lru_lambda, v_sinks, v_w_branch, v_w_out, v_ln1_g, v_ln1_b, v_w_ffn_in, v_w_ffn_out, v_ln2_g, v_ln2_b):
    B, S, D = x.shape
    T = B * S
    L = w_in.shape[0]
    H = D // HEAD_DIM
    KVB = D // SWA_GROUP
    FH = w_ffn_out.shape[1] * N_CHIPS
    C = w_in.shape[2] * N_CHIPS
    alpha = (2.0 * L) ** 0.25
    off = {}
    pos = 0
    for nm, wd in (("lx", D), ("lg", D), ("qb", D), ("kb", KVB), ("vb", KVB), ("qc", D), ("kc", D), ("vc", D), ("gt", 3 * D)):
        off[nm] = pos
        pos += wd
    assert pos == C
    cx, cy, cc = lax.axis_index("x"), lax.axis_index("y"), lax.axis_index("c")
    shard = (2 * cx + cy).astype(jnp.int32)
    core_a = cc.astype(jnp.int32).reshape(1)
    shard_a = shard.reshape(1)

    full = []
    for l in range(L):
        sh = [_split5(w_in[l].astype(BF16)), _split5(w_branch[l].astype(BF16)), _split5(w_out[l].astype(BF16)),
              _split5(w_ffn_in[l].astype(BF16)), _split5(w_ffn_out[l].astype(BF16))]
        g = _gather_weights(sh)
        full.append(dict(
            w_in=g[0].reshape(N_CHIPS, D, C // N_CHIPS),
            w_branch=g[1].reshape(3, D, D),
            w_out=g[2].reshape(D, D),
            w_ffn_in=g[3].reshape(N_CHIPS, D, 2 * FH // N_CHIPS),
            w_ffn_out=g[4].reshape(FH, D),
        ))
    cw_all = _gather_small(conv_w.reshape(L * CONV_WIDTH, D // N_CHIPS))
    conv_w_full = jnp.transpose(cw_all, (1, 0, 2)).reshape(L, CONV_WIDTH, D)

    def layer_params(l):
        return dict(conv_w=conv_w_full[l], conv_b=conv_b[l][None], w_rg_bd=_block_diag(w_rg[l]), b_rg=b_rg[l][None],
                    w_ig_bd=_block_diag(w_ig[l]), b_ig=b_ig[l][None], lam=lru_lambda[l][None])

    def sink_rows(l, hb):
        sk = sinks[l].reshape(H // hb, 1, hb)
        return jnp.pad(sk, ((0, 0), (0, 0), (0, LANES - hb)))

    hb_b = SWA_HB

    saved = []
    xin = x.reshape(T, D)
    for l in range(L):
        fw, lp = full[l], layer_params(l)
        proj = _matmul(xin, fw["w_in"], mode="nn", name="mm_proj", tm=512, n_outer=True)
        proj3 = proj.reshape(B, S, C)
        h3, ya3 = _lru_fwd(proj3, lp, D=D, x_off=off["lx"], g_off=off["lg"])
        skr = sink_rows(l, hb_b)
        yb3, lse_b = _swa_fwd(proj3, skr, D=D, q_off=off["qb"], k_off=off["kb"], v_off=off["vb"])
        yc3, lse_c = _dil_fwd(proj3, D=D, q_off=off["qc"], k_off=off["kc"], v_off=off["vc"])
        ya, yb, yc = ya3.reshape(T, D), yb3.reshape(T, D), yc3.reshape(T, D)
        branch, merged = _branch_fwd([ya, yb, yc], fw["w_branch"], proj, D=D, g_off=off["gt"])
        mix = _matmul(merged, fw["w_out"], mode="nn", name="mm_out", tn=1024)
        z1, x1 = _ln_fwd(xin, mix, ln1_g[l][None], ln1_b[l][None], alpha=alpha)
        hh = _matmul(x1, fw["w_ffn_in"], mode="nn", name="mm_ffn_in", n_outer=True)
        f = _swiglu_fwd(hh)
        ffn = _matmul(f, fw["w_ffn_out"], mode="nn", name="mm_ffn_out", tn=1024, tk=4096)
        z2, x2 = _ln_fwd(x1, ffn, ln2_g[l][None], ln2_b[l][None], alpha=alpha)
        saved.append(dict(x=xin, proj=proj, h3=h3, ya=ya, yb=yb, lse_b=lse_b, yc=yc, lse_c=lse_c, branch=branch,
                          merged=merged, z1=z1, x1=x1, hh=hh, f=f, z2=z2, skr=skr))
        xin = x2

    dx, loss_rows = _loss_head(xin, loss_target.reshape(T, D))
    loss = lax.psum(jnp.sum(loss_rows), ("x", "y", "c"))

    big = {k: [None] * L for k in ("w_in", "w_branch", "w_out", "w_ffn_in", "w_ffn_out")}
    small = [None] * L
    for l in reversed(range(L)):
        fw, lp, sv = full[l], layer_params(l), saved[l]
        dz2, dg2, db2 = _ln_bwd(dx, sv["z2"], ln2_g[l][None])
        df = _matmul(dz2, fw["w_ffn_out"], mode="nt", name="mm_dffn_out_x", tn=4096, tk=1024)
        g_ffn_out = _matmul(sv["f"], dz2, mode="tn", name="mm_dffn_out_w", tm=1408, tn=1024, tk=1024)
        dhh = _swiglu_bwd(sv["hh"], df)
        dx1 = _matmul(dhh, fw["w_ffn_in"], mode="nt", name="mm_dffn_in_x", tm=1024, tn=1024, resid=dz2, rs=alpha)
        g_ffn_in = _matmul(sv["x1"], dhh, mode="tn", name="mm_dffn_in_w", tm=1024, tk=1024, out_shards=N_CHIPS)
        dz1, dg1, db1 = _ln_bwd(dx1, sv["z1"], ln1_g[l][None])
        dmerged = _matmul(dz1, fw["w_out"], mode="nt", name="mm_dout_x", tn=1024, tk=1024)
        g_out = _matmul(sv["merged"], dz1, mode="tn", name="mm_dout_w", tm=1024, tn=1024, tk=1024)
        dbranch, dgates = _branch_bwd(dmerged, sv["branch"], sv["proj"], D=D, g_off=off["gt"])
        ys = [sv["ya"], sv["yb"], sv["yc"]]
        dys, g_branch = [], []
        for n in range(3):
            dys.append(_matmul(dbranch[n], fw["w_branch"][n], mode="nt", name="mm_dbranch_x", tn=1024, tk=1024))
            g_branch.append(_matmul(ys[n], dbranch[n], mode="tn", name="mm_dbranch_w", tm=1024, tn=1024, tk=1024))
        proj3 = sv["proj"].reshape(B, S, C)
        r3 = lambda t: t.reshape(B, S, t.shape[-1])
        lru = _lru_bwd(proj3, sv["h3"], r3(dys[0]), lp, D=D, x_off=off["lx"], g_off=off["lg"])
        dxr, dgate = lru[0], lru[1]
        dqb, dkb, dvb, dsk = _swa_bwd(proj3, r3(sv["yb"]), sv["lse_b"], r3(dys[1]), sv["skr"], D=D, q_off=off["qb"],
                                      k_off=off["kb"], v_off=off["vb"])
        acc = _dil_bwd(proj3, r3(sv["yc"]), sv["lse_c"], r3(dys[2]), D=D, q_off=off["qc"], k_off=off["kc"], v_off=off["vc"])
        f2 = lambda t: t.reshape(T, t.shape[-1]).astype(BF16)
        dproj = jnp.concatenate([f2(dxr), f2(dgate), f2(dqb), f2(dkb), f2(dvb), f2(acc[0]), f2(acc[1]), f2(acc[2])] + dgates, axis=1)
        dx = _matmul(dproj, fw["w_in"], mode="nt", name="mm_dproj_x", tm=1024, tn=1024, resid=dz1, rs=alpha)
        g_in = _matmul(sv["x"], dproj, mode="tn", name="mm_dproj_w", tm=512, tk=1024, out_shards=N_CHIPS)

        g5 = [g_in.reshape(1, N_CHIPS, 2, D // 2, C // N_CHIPS),
              jnp.stack(g_branch).reshape(3, N_CHIPS, 2, D // N_CHIPS // 2, D),
              g_out.reshape(1, N_CHIPS, 2, D // N_CHIPS // 2, D),
              g_ffn_in.reshape(1, N_CHIPS, 2, D // 2, 2 * FH // N_CHIPS),
              g_ffn_out.reshape(1, N_CHIPS, 2, FH // N_CHIPS // 2, D)]
        red = _reduce_big_grads(g5, core_a, shard_a)
        big["w_in"][l] = red[0].reshape(D, C // N_CHIPS)
        big["w_branch"][l] = red[1].reshape(3, D // N_CHIPS, D)
        big["w_out"][l] = red[2].reshape(D // N_CHIPS, D)
        big["w_ffn_in"][l] = red[3].reshape(D, 2 * FH // N_CHIPS)
        big["w_ffn_out"][l] = red[4].reshape(FH // N_CHIPS, D)

        dsinks = jnp.sum(dsk, axis=0)[:, 0, :hb_b].reshape(H)
        bw = w_rg.shape[-1]
        small[l] = [lru[2].reshape(-1), lru[3].reshape(-1), _block_diag_grad(lru[4], bw).reshape(-1), lru[5].reshape(-1),
                    _block_diag_grad(lru[6], bw).reshape(-1), lru[7].reshape(-1), lru[8].reshape(-1),
                    jnp.pad(dsinks, (0, LANES - H)), dg1.reshape(-1), db1.reshape(-1), dg2.reshape(-1), db2.reshape(-1)]

    sizes = [t.size for t in small[0]]
    flat = jnp.concatenate([t for l in range(L) for t in small[l]])
    n_flat = flat.size
    rows = -(-n_flat // (LANES * 256)) * 256
    flat = jnp.pad(flat, (0, rows * LANES - n_flat)).reshape(rows, LANES)
    red_small = _allreduce_small(flat).reshape(-1)
    per_layer = sum(sizes)
    names = ["conv_w", "conv_b", "w_rg", "b_rg", "w_ig", "b_ig", "lru_lambda", "sinks", "ln1_g", "ln1_b", "ln2_g", "ln2_b"]
    sg = {nm: [] for nm in names}
    for l in range(L):
        p = l * per_layer
        for nm, sz in zip(names, sizes):
            sg[nm].append(red_small[p:p + sz])
            p += sz
    grads = dict(
        w_in=jnp.stack(big["w_in"]), w_branch=jnp.stack(big["w_branch"]), w_out=jnp.stack(big["w_out"]),
        w_ffn_in=jnp.stack(big["w_ffn_in"]), w_ffn_out=jnp.stack(big["w_ffn_out"]),
        conv_w=lax.dynamic_slice_in_dim(jnp.stack(sg["conv_w"]).reshape(L, CONV_WIDTH, D), shard * (D // N_CHIPS), D // N_CHIPS, axis=2),
        conv_b=jnp.stack(sg["conv_b"]), w_rg=jnp.stack(sg["w_rg"]).reshape(w_rg.shape), b_rg=jnp.stack(sg["b_rg"]),
        w_ig=jnp.stack(sg["w_ig"]).reshape(w_ig.shape), b_ig=jnp.stack(sg["b_ig"]), lru_lambda=jnp.stack(sg["lru_lambda"]),
        sinks=jnp.stack(sg["sinks"])[:, :H], ln1_g=jnp.stack(sg["ln1_g"]), ln1_b=jnp.stack(sg["ln1_b"]),
        ln2_g=jnp.stack(sg["ln2_g"]), ln2_b=jnp.stack(sg["ln2_b"]),
    )

    order = ["w_in", "conv_w", "conv_b", "w_rg", "b_rg", "w_ig", "b_ig", "lru_lambda", "sinks", "w_branch", "w_out",
             "ln1_g", "ln1_b", "w_ffn_in", "w_ffn_out", "ln2_g", "ln2_b"]
    weights = dict(w_in=w_in, conv_w=conv_w, conv_b=conv_b, w_rg=w_rg, b_rg=b_rg, w_ig=w_ig, b_ig=b_ig, lru_lambda=lru_lambda,
                   sinks=sinks, w_branch=w_branch, w_out=w_out, ln1_g=ln1_g, ln1_b=ln1_b, w_ffn_in=w_ffn_in,
                   w_ffn_out=w_ffn_out, ln2_g=ln2_g, ln2_b=ln2_b)
    ms = dict(w_in=m_w_in, conv_w=m_conv_w, conv_b=m_conv_b, w_rg=m_w_rg, b_rg=m_b_rg, w_ig=m_w_ig, b_ig=m_b_ig,
              lru_lambda=m_lru_lambda, sinks=m_sinks, w_branch=m_w_branch, w_out=m_w_out, ln1_g=m_ln1_g, ln1_b=m_ln1_b,
              w_ffn_in=m_w_ffn_in, w_ffn_out=m_w_ffn_out, ln2_g=m_ln2_g, ln2_b=m_ln2_b)
    vs = dict(w_in=v_w_in, conv_w=v_conv_w, conv_b=v_conv_b, w_rg=v_w_rg, b_rg=v_b_rg, w_ig=v_w_ig, b_ig=v_b_ig,
              lru_lambda=v_lru_lambda, sinks=v_sinks, w_branch=v_w_branch, w_out=v_w_out, ln1_g=v_ln1_g, ln1_b=v_ln1_b,
              w_ffn_in=v_w_ffn_in, w_ffn_out=v_w_ffn_out, ln2_g=v_ln2_g, ln2_b=v_ln2_b)
    deltas, new_m, new_v = {}, {}, {}
    for nm in order:
        deltas[nm], new_m[nm], new_v[nm] = _adamw(weights[nm], grads[nm], ms[nm], vs[nm])
    return (loss, dx.reshape(B, S, D), *[grads[nm] for nm in order], *[deltas[nm] for nm in order],
            *[new_m[nm] for nm in order], *[new_v[nm] for nm in order])
```

```python
import math

import jax
import jax.numpy as jnp
from jax import lax
from jax.experimental import pallas as pl
from jax.experimental.pallas import tpu as pltpu

HEAD_DIM = 64
WIN = 128
DILS = (1, 4, 16)
SWA_GROUP = 4
CONV_WIDTH = 4
LRU_C = 8.0
LN_EPS = 1e-5
NEG_INF = -1e30
N_CHIPS = 4
ADAM_LR, ADAM_B1, ADAM_B2, ADAM_EPS, ADAM_WD, ADAM_STEP = 0.001, 0.9, 0.999, 1e-08, 0.01, 10

LANES = 128
SUBLANES = 8
VMEM_LIMIT = 48 * 1024 * 1024

F32 = jnp.float32
BF16 = jnp.bfloat16
MESH = pl.DeviceIdType.MESH
ANY = pl.BlockSpec(memory_space=pl.ANY)


def _pcall(body, **kw):
    return pl.pallas_call(body, **kw)


def _pcall_comm(body, **kw):
    return pl.pallas_call(body, **kw)


def _params(*sem):
    return pltpu.CompilerParams(dimension_semantics=tuple(sem), vmem_limit_bytes=VMEM_LIMIT)


def _tile(dim, target):
    if dim <= target:
        return dim
    best = None
    for t in range(LANES, target + 1, LANES):
        if dim % t == 0:
            best = t
    assert best is not None, (dim, target)
    return best


def _sigmoid(x):
    return 1.0 / (1.0 + jnp.exp(-x))


def _dot(a, b, dims):
    return lax.dot_general(a, b, (dims, ((), ())), preferred_element_type=F32)


def _dot_nn(a, b):
    return _dot(a, b, ((1,), (0,)))


def _dot_nt(a, b):
    return _dot(a, b, ((1,), (1,)))


def _dot_tn(a, b):
    return _dot(a, b, ((0,), (0,)))


def _matmul(a, b, *, mode, name, out_dtype=F32, tm=512, tn=512, tk=2048, resid=None, rs=1.0, out_shards=0, n_outer=False):
    b_sh = b.ndim == 3
    if mode == "nn":
        M, K = a.shape
        N = b.shape[-1] * (b.shape[0] if b_sh else 1)
    elif mode == "nt":
        M, K = a.shape
        N = b.shape[-2]
    else:
        K, M = a.shape
        N = b.shape[-1]
    tm = _tile(M, tm)
    if mode == "nn" and b_sh:
        tn = b.shape[-1]
    elif out_shards:
        tn = N // out_shards
    else:
        tn = _tile(N, tn)
    if mode == "nt" and b_sh:
        tk = b.shape[-1]
    else:
        tk = _tile(K, tk)
    nk = K // tk
    grid = (N // tn, M // tm, nk) if n_outer else (M // tm, N // tn, nk)

    def spec(shape, f):
        return pl.BlockSpec(shape, (lambda g0, g1, k: f(g1, g0, k)) if n_outer else f)

    if mode == "nn":
        a_spec = spec((tm, tk), lambda i, j, k: (i, k))
        b_spec = spec((None, tk, tn), lambda i, j, k: (j, k, 0)) if b_sh else spec((tk, tn), lambda i, j, k: (k, j))
        contract = _dot_nn
    elif mode == "nt":
        a_spec = spec((tm, tk), lambda i, j, k: (i, k))
        b_spec = spec((None, tn, tk), lambda i, j, k: (k, j, 0)) if b_sh else spec((tn, tk), lambda i, j, k: (j, k))
        contract = _dot_nt
    else:
        a_spec = spec((tk, tm), lambda i, j, k: (k, i))
        b_spec = spec((tk, tn), lambda i, j, k: (k, j))
        contract = _dot_tn
    if out_shards:
        out_shape = jax.ShapeDtypeStruct((out_shards, M, tn), out_dtype)
        o_spec = spec((None, tm, tn), lambda i, j, k: (j, i, 0))
    else:
        out_shape = jax.ShapeDtypeStruct((M, N), out_dtype)
        o_spec = spec((tm, tn), lambda i, j, k: (i, j))
    in_specs = [a_spec, b_spec]
    args = [a, b]
    if resid is not None:
        in_specs.append(spec((tm, tn), lambda i, j, k: (i, j)))
        args.append(resid)

    def body(*refs):
        a_ref, b_ref = refs[:2]
        r_ref = refs[2] if resid is not None else None
        o_ref = refs[3] if resid is not None else refs[2]
        part = contract(a_ref[...].astype(BF16), b_ref[...].astype(BF16))

        def finish(res):
            if resid is not None:
                res = res + rs * r_ref[...]
            o_ref[...] = res.astype(out_dtype)

        if nk == 1:
            finish(part)
            return
        acc_ref = refs[-1]
        k = pl.program_id(2)

        @pl.when(k == 0)
        def _():
            acc_ref[...] = part

        @pl.when(jnp.logical_and(k > 0, k < nk - 1))
        def _():
            acc_ref[...] += part

        @pl.when(k == nk - 1)
        def _():
            finish(acc_ref[...] + part)

    return _pcall(
        body, name=name, grid=grid, in_specs=in_specs, out_specs=o_spec, out_shape=out_shape,
        scratch_shapes=[pltpu.VMEM((tm, tn), F32)] if nk > 1 else [],
        compiler_params=_params("parallel", "parallel", "arbitrary"),
    )(*args)


def _shift_down(x, d, row):
    return jnp.where(row >= d, pltpu.roll(x, d, 0), 0.0)


def _shift_up(x, d, row, n):
    return jnp.where(row < n - d, pltpu.roll(x, n - d, 0), 0.0)


def _log1p(u):
    w = 1.0 + u
    return jnp.where(w == 1.0, u, jnp.log(w) * u / (w - 1.0))


def _gelu_parts(g):
    k = math.sqrt(2.0 / math.pi)
    c = 0.044715
    t = jnp.tanh(k * (g + c * g * g * g))
    val = 0.5 * g * (1.0 + t)
    der = 0.5 * (1.0 + t) + 0.5 * g * (1.0 - t * t) * k * (1.0 + 3.0 * c * g * g)
    return val, der


def _lru_gates(xr, cw_ref, cb_ref, wrg_ref, brg_ref, wig_ref, big_ref, lam_ref, row):
    xc = cw_ref[3:4, :] * xr + cb_ref[...]
    for d in range(1, CONV_WIDTH):
        xc = xc + cw_ref[3 - d:4 - d, :] * _shift_down(xr, d, row)
    xcb = xc.astype(BF16)
    r = _sigmoid(_dot_nn(xcb, wrg_ref[...]) + brg_ref[...])
    ig = _sigmoid(_dot_nn(xcb, wig_ref[...]) + big_ref[...])
    lam = lam_ref[...]
    sp = jnp.maximum(-lam, 0.0) + _log1p(jnp.exp(-jnp.abs(lam)))
    log_a = (-LRU_C) * r * sp
    a = jnp.exp(log_a)
    y2 = 2.0 * log_a
    one_m_a2 = jnp.where(y2 > -0.01, -(y2 + 0.5 * y2 * y2 + (1.0 / 6.0) * y2 * y2 * y2), 1.0 - jnp.exp(y2))
    mult = jnp.sqrt(one_m_a2)
    return xc, r, ig, sp, a, mult


def _scan_local(a, b, row, n, reverse):
    sub = row % SUBLANES
    d = 1
    while d < SUBLANES:
        if reverse:
            keep = sub < SUBLANES - d
            a_s = jnp.where(keep, pltpu.roll(a, n - d, 0), 1.0)
            b_s = jnp.where(keep, pltpu.roll(b, n - d, 0), 0.0)
        else:
            keep = sub >= d
            a_s = jnp.where(keep, pltpu.roll(a, d, 0), 1.0)
            b_s = jnp.where(keep, pltpu.roll(b, d, 0), 0.0)
        b = a * b_s + b
        a = a * a_s
        d *= 2
    return a, b


def _scan_carry(a_ref, b_ref, out_ref, n, reverse):
    ng = n // SUBLANES

    def step(gidx, carry):
        g = (ng - 1 - gidx) if reverse else gidx
        rows = pl.ds(pl.multiple_of(g * SUBLANES, SUBLANES), SUBLANES)
        h = a_ref[rows, :] * carry + b_ref[rows, :]
        out_ref[rows, :] = h
        return h[0:1, :] if reverse else h[SUBLANES - 1:SUBLANES, :]

    lax.fori_loop(0, ng, step, jnp.zeros((1, LANES), F32), unroll=8)


def _lru_specs(B, S, D, C, x_off, g_off):
    nct = D // LANES
    seq = lambda off: pl.BlockSpec((None, S, LANES), lambda ct, b: (b, 0, off // LANES + ct))
    row = lambda r: pl.BlockSpec((r, LANES), lambda ct, b: (0, ct))
    wbd = pl.BlockSpec((None, LANES, LANES), lambda ct, b: (ct, 0, 0))
    return nct, seq, row, wbd


def _lru_fwd(proj3, lp, *, D, x_off, g_off):
    B, S, C = proj3.shape
    nct, seq, row, wbd = _lru_specs(B, S, D, C, x_off, g_off)

    def body(xr_ref, g_ref, cw_ref, cb_ref, wrg_ref, brg_ref, wig_ref, big_ref, lam_ref, h_ref, ya_ref, a_s, b_s):
        rowi = lax.broadcasted_iota(jnp.int32, (S, LANES), 0)
        xr = xr_ref[...]
        xc, r, ig, sp, a, mult = _lru_gates(xr, cw_ref, cb_ref, wrg_ref, brg_ref, wig_ref, big_ref, lam_ref, rowi)
        al, bl = _scan_local(a, mult * (ig * xc), rowi, S, False)
        a_s[...] = al
        b_s[...] = bl
        _scan_carry(a_s, b_s, h_ref, S, False)
        gel, _ = _gelu_parts(g_ref[...])
        ya_ref[...] = (h_ref[...] * gel).astype(BF16)

    out_seq = pl.BlockSpec((None, S, LANES), lambda ct, b: (b, 0, ct))
    return _pcall(
        body, name="lru_fwd", grid=(nct, B),
        in_specs=[seq(x_off), seq(g_off), row(CONV_WIDTH), row(1), wbd, row(1), wbd, row(1), row(1)],
        out_specs=[out_seq, out_seq],
        out_shape=[jax.ShapeDtypeStruct((B, S, D), F32), jax.ShapeDtypeStruct((B, S, D), BF16)],
        scratch_shapes=[pltpu.VMEM((S, LANES), F32), pltpu.VMEM((S, LANES), F32)],
        compiler_params=_params("parallel", "parallel"),
    )(proj3, proj3, lp["conv_w"], lp["conv_b"], lp["w_rg_bd"], lp["b_rg"], lp["w_ig_bd"], lp["b_ig"], lp["lam"])


def _lru_bwd(proj3, h3, dya3, lp, *, D, x_off, g_off):
    B, S, C = proj3.shape
    nct, seq, row, wbd = _lru_specs(B, S, D, C, x_off, g_off)

    def body(xr_ref, g_ref, h_ref, dy_ref, cw_ref, cb_ref, wrg_ref, brg_ref, wig_ref, big_ref, lam_ref,
             dxr_ref, dg_ref, dcw_ref, dcb_ref, dwrg_ref, dbrg_ref, dwig_ref, dbig_ref, dlam_ref, a_s, b_s, l_s):
        first = pl.program_id(1) == 0
        rowi = lax.broadcasted_iota(jnp.int32, (S, LANES), 0)
        xr = xr_ref[...]
        xc, r, ig, sp, a, mult = _lru_gates(xr, cw_ref, cb_ref, wrg_ref, brg_ref, wig_ref, big_ref, lam_ref, rowi)
        h = h_ref[...]
        dy = dy_ref[...]
        gel, dgel = _gelu_parts(g_ref[...])
        dg_ref[...] = (dy * h * dgel).astype(BF16)
        al, bl = _scan_local(_shift_up(a, 1, rowi, S), dy * gel, rowi, S, True)
        a_s[...] = al
        b_s[...] = bl
        _scan_carry(a_s, b_s, l_s, S, True)
        lamb = l_s[...]
        u = ig * xc
        da = lamb * _shift_down(h, 1, rowi)
        dlog_a = da * a - (lamb * u) * (a * a) / mult
        du = lamb * mult
        dpre_r = (dlog_a * ((-LRU_C) * sp)) * r * (1.0 - r)
        dpre_i = (du * xc) * ig * (1.0 - ig)
        dsp = jnp.sum(dlog_a * ((-LRU_C) * r), axis=0, keepdims=True)
        dlam = dsp * (-1.0 / (1.0 + jnp.exp(lam_ref[...])))
        dpr = dpre_r.astype(BF16)
        dpi = dpre_i.astype(BF16)
        dxc = du * ig + _dot_nt(dpr, wrg_ref[...]) + _dot_nt(dpi, wig_ref[...])
        xcb = xc.astype(BF16)
        dwrg = _dot_tn(xcb, dpr)
        dwig = _dot_tn(xcb, dpi)
        dxr = cw_ref[3:4, :] * dxc
        dcw = [jnp.sum(xr * dxc, axis=0, keepdims=True)]
        for d in range(1, CONV_WIDTH):
            dxr = dxr + cw_ref[3 - d:4 - d, :] * _shift_up(dxc, d, rowi, S)
            dcw.append(jnp.sum(_shift_down(xr, d, rowi) * dxc, axis=0, keepdims=True))
        dxr_ref[...] = dxr.astype(BF16)
        dcw_rows = jnp.concatenate(dcw[::-1], axis=0)
        sums = ((dcw_ref, dcw_rows), (dcb_ref, jnp.sum(dxc, axis=0, keepdims=True)), (dwrg_ref, dwrg),
                (dbrg_ref, jnp.sum(dpre_r, axis=0, keepdims=True)), (dwig_ref, dwig),
                (dbig_ref, jnp.sum(dpre_i, axis=0, keepdims=True)), (dlam_ref, dlam))

        @pl.when(first)
        def _():
            for ref, val in sums:
                ref[...] = val

        @pl.when(jnp.logical_not(first))
        def _():
            for ref, val in sums:
                ref[...] += val

    out_seq = pl.BlockSpec((None, S, LANES), lambda ct, b: (b, 0, ct))
    f = lambda shape: jax.ShapeDtypeStruct(shape, F32)
    nb = D // LANES
    return _pcall(
        body, name="lru_bwd", grid=(nct, B),
        in_specs=[seq(x_off), seq(g_off), out_seq, out_seq, row(CONV_WIDTH), row(1), wbd, row(1), wbd, row(1), row(1)],
        out_specs=[out_seq, out_seq, row(CONV_WIDTH), row(1), wbd, row(1), wbd, row(1), row(1)],
        out_shape=[jax.ShapeDtypeStruct((B, S, D), BF16), jax.ShapeDtypeStruct((B, S, D), BF16),
                   f((CONV_WIDTH, D)), f((1, D)), f((nb, LANES, LANES)), f((1, D)), f((nb, LANES, LANES)), f((1, D)), f((1, D))],
        scratch_shapes=[pltpu.VMEM((S, LANES), F32)] * 3,
        compiler_params=_params("parallel", "arbitrary"),
    )(proj3, proj3, h3, dya3, lp["conv_w"], lp["conv_b"], lp["w_rg_bd"], lp["b_rg"], lp["w_ig_bd"], lp["b_ig"], lp["lam"])


def _pair_stack(x, lo):
    z = jnp.zeros_like(x)
    return jnp.concatenate([jnp.where(lo, x, z), jnp.where(lo, z, x)], axis=0).astype(BF16)


def _pair_join(y2, lo):
    return jnp.where(lo, y2[:WIN], y2[WIN:])


def _pair_col(xb):
    return jnp.concatenate([xb[:, 0:1], xb[:, HEAD_DIM:HEAD_DIM + 1]], axis=0)


def _pair_bcast(col, lo):
    return jnp.where(lo, jnp.broadcast_to(col[:WIN], (WIN, LANES)), jnp.broadcast_to(col[WIN:], (WIN, LANES)))


def _dil_rows(it, d, S):
    if d == 1:
        cur = pl.multiple_of(it * WIN, WIN)
        prev = pl.multiple_of(jnp.maximum(it - 1, 0) * WIN, WIN)
        return pl.ds(cur, WIN), pl.ds(prev, WIN), it > 0
    r, i = it % d, it // d
    cur = i * (WIN * d) + r
    prev = jnp.maximum(i - 1, 0) * (WIN * d) + r
    return pl.ds(cur, WIN, stride=d), pl.ds(prev, WIN, stride=d), i > 0


def _dil_bias(two_blocks, stack=2):
    nk = 2 * WIN if two_blocks else WIN
    qi = lax.broadcasted_iota(jnp.int32, (stack * WIN, nk), 0) & (WIN - 1)
    kj = lax.broadcasted_iota(jnp.int32, (stack * WIN, nk), 1)
    if not two_blocks:
        return jnp.where(kj <= qi, 0.0, NEG_INF), None
    cur = jnp.logical_and(kj >= WIN, kj - WIN <= qi)
    prev = jnp.logical_and(kj < WIN, kj >= qi)
    return jnp.where(jnp.logical_or(cur, prev), 0.0, NEG_INF), jnp.where(cur, 0.0, NEG_INF)


def _dil_specs(B, S, D, C, offs):
    grid = (B, D // LANES)
    seq = lambda off: pl.BlockSpec((None, S, LANES), lambda b, p: (b, 0, off // LANES + p))
    return grid, [seq(o) for o in offs], seq(0)


def _call_with_rider(rider, body, *, name, grid, in_specs, out_specs, out_shape, scratch_shapes, args):
    if rider is None:
        return _pcall(body, name=name, grid=grid, in_specs=in_specs, out_specs=out_specs, out_shape=out_shape,
                      scratch_shapes=scratch_shapes, compiler_params=_params("parallel", "parallel"))(*args)
    n_in, n_out, n_sc = len(in_specs), len(out_specs), len(scratch_shapes)
    r_in, r_out = len(rider["ins"]), len(rider["out_shapes"])

    def wrapped(*refs):
        p = 0
        own_in = refs[p:p + n_in]; p += n_in
        rid_in = refs[p:p + r_in]; p += r_in
        own_out = refs[p:p + n_out]; p += n_out
        rid_out = refs[p:p + r_out]; p += r_out
        own_sc = refs[p:p + n_sc]; p += n_sc
        send_sems, recv_sems = refs[p:p + 2]
        i, j = pl.program_id(0), pl.program_id(1)

        @pl.when(jnp.logical_and(i == 0, j == 0))
        def _():
            rider["start"](rid_in, rid_out, send_sems, recv_sems)

        body(*own_in, *own_out, *own_sc)

        @pl.when(jnp.logical_and(i == grid[0] - 1, j == grid[1] - 1))
        def _():
            rider["finish"](rid_in, rid_out, send_sems, recv_sems)

    res = _pcall_comm(
        wrapped, name=name + "_" + rider["name"], grid=grid, in_specs=list(in_specs) + [ANY] * r_in,
        out_specs=list(out_specs) + [ANY] * r_out, out_shape=list(out_shape) + list(rider["out_shapes"]),
        scratch_shapes=list(scratch_shapes) + [pltpu.SemaphoreType.DMA((rider["n"],)), pltpu.SemaphoreType.DMA((rider["n"],))],
        compiler_params=_params("arbitrary", "arbitrary"),
    )(*args, *rider["ins"])
    return res


def _dil_fwd(proj3, *, D, q_off, k_off, v_off, rider=None):
    B, S, C = proj3.shape
    n_it = S // WIN
    scale = HEAD_DIM ** -0.5
    grid, in_specs, out_spec = _dil_specs(B, S, D, C, (q_off, k_off, v_off))

    def body(q_ref, k_ref, v_ref, o_ref, l_ref):
        lo = lax.broadcasted_iota(jnp.int32, (WIN, LANES), 1) < HEAD_DIM
        for c, d in enumerate(DILS):
            two = S // d > WIN
            bias_all, bias_first = _dil_bias(two)

            def step(it, _, c=c, d=d, two=two, bias_all=bias_all, bias_first=bias_first):
                cur, prev, later = _dil_rows(it, d, S)
                q2 = _pair_stack(q_ref[cur, :], lo)
                if two:
                    k2 = jnp.concatenate([k_ref[prev, :], k_ref[cur, :]], axis=0).astype(BF16)
                    v2 = jnp.concatenate([v_ref[prev, :], v_ref[cur, :]], axis=0).astype(BF16)
                    bias = jnp.where(later, bias_all, bias_first)
                else:
                    k2, v2, bias = k_ref[cur, :].astype(BF16), v_ref[cur, :].astype(BF16), bias_all
                s2 = _dot_nt(q2, k2) * scale + bias
                m2 = jnp.max(s2, axis=1, keepdims=True)
                p2 = jnp.exp(s2 - m2)
                den = jnp.sum(p2, axis=1, keepdims=True)
                oc = _pair_join(_dot_nn(p2.astype(BF16), v2) / den, lo)
                lc = _pair_bcast(m2 + jnp.log(den), lo)
                if c == 0:
                    o_ref[cur, :] = oc
                    l_ref[cur, :] = lc
                else:
                    l_old = l_ref[cur, :]
                    mx = jnp.maximum(l_old, lc)
                    e_old, e_new = jnp.exp(l_old - mx), jnp.exp(lc - mx)
                    tot = e_old + e_new
                    o_ref[cur, :] = (e_old * o_ref[cur, :] + e_new * oc) / tot
                    l_ref[cur, :] = mx + jnp.log(tot)
                return 0

            lax.fori_loop(0, n_it, step, 0, unroll=2)

    return _call_with_rider(
        rider, body, name="dil_fwd", grid=grid, in_specs=in_specs, out_specs=[out_spec, out_spec],
        out_shape=[jax.ShapeDtypeStruct((B, S, D), F32)] * 2, scratch_shapes=[], args=[proj3, proj3, proj3])


def _dil_bwd(proj3, o3, l3, do3, *, D, q_off, k_off, v_off, rider=None):
    B, S, C = proj3.shape
    n_it = S // WIN
    scale = HEAD_DIM ** -0.5
    grid, in_specs, out_spec = _dil_specs(B, S, D, C, (q_off, k_off, v_off))

    def body(q_ref, k_ref, v_ref, o_ref, l_ref, do_ref, dq_ref, dk_ref, dv_ref, dd_s, dq_s, dk_s, dv_s):
        lo = lax.broadcasted_iota(jnp.int32, (WIN, LANES), 1) < HEAD_DIM
        lo_s = lax.broadcasted_iota(jnp.int32, (S, LANES), 1) < HEAD_DIM
        prod = do_ref[...] * o_ref[...]
        d_lo = jnp.sum(jnp.where(lo_s, prod, 0.0), axis=1, keepdims=True)
        d_hi = jnp.sum(jnp.where(lo_s, 0.0, prod), axis=1, keepdims=True)
        dd_s[...] = jnp.where(lo_s, jnp.broadcast_to(d_lo, (S, LANES)), jnp.broadcast_to(d_hi, (S, LANES)))
        dq_s[...] = jnp.zeros_like(dq_s)
        dk_s[...] = jnp.zeros_like(dk_s)
        dv_s[...] = jnp.zeros_like(dv_s)
        for d in DILS:
            two = S // d > WIN
            bias_all, bias_first = _dil_bias(two)

            def step(it, _, d=d, two=two, bias_all=bias_all, bias_first=bias_first):
                cur, prev, later = _dil_rows(it, d, S)
                q2 = _pair_stack(q_ref[cur, :], lo)
                do2 = _pair_stack(do_ref[cur, :], lo)
                l2 = _pair_col(l_ref[cur, :])
                dd2 = _pair_col(dd_s[cur, :])
                if two:
                    k2 = jnp.concatenate([k_ref[prev, :], k_ref[cur, :]], axis=0).astype(BF16)
                    v2 = jnp.concatenate([v_ref[prev, :], v_ref[cur, :]], axis=0).astype(BF16)
                    bias = jnp.where(later, bias_all, bias_first)
                else:
                    k2, v2, bias = k_ref[cur, :].astype(BF16), v_ref[cur, :].astype(BF16), bias_all
                p2 = jnp.exp(_dot_nt(q2, k2) * scale + bias - l2)
                ds2 = (p2 * (_dot_nt(do2, v2) - dd2) * scale).astype(BF16)
                dq_s[cur, :] += _pair_join(_dot_nn(ds2, k2), lo)
                dk2 = _dot_tn(ds2, q2)
                dv2 = _dot_tn(p2.astype(BF16), do2)
                if two:
                    dk_s[prev, :] += dk2[:WIN]
                    dv_s[prev, :] += dv2[:WIN]
                    dk_s[cur, :] += dk2[WIN:]
                    dv_s[cur, :] += dv2[WIN:]
                else:
                    dk_s[cur, :] += dk2
                    dv_s[cur, :] += dv2
                return 0

            lax.fori_loop(0, n_it, step, 0, unroll=2)
        dq_ref[...] = dq_s[...].astype(BF16)
        dk_ref[...] = dk_s[...].astype(BF16)
        dv_ref[...] = dv_s[...].astype(BF16)

    return _call_with_rider(
        rider, body, name="dil_bwd", grid=grid, in_specs=in_specs + [out_spec] * 3, out_specs=[out_spec] * 3,
        out_shape=[jax.ShapeDtypeStruct((B, S, D), BF16)] * 3, scratch_shapes=[pltpu.VMEM((S, LANES), F32)] * 4,
        args=[proj3, proj3, proj3, o3, l3, do3])


SWA_HB = 2 * SWA_GROUP


def _to_half(x, src, dst, lo):
    if src != dst:
        x = pltpu.roll(x, HEAD_DIM, 1)
    return jnp.where(lo if dst == 0 else jnp.logical_not(lo), x, 0.0)


def _swa_specs(B, S, D, C, q_off, k_off, v_off, n_steps):
    qw = SWA_HB * HEAD_DIM
    last = S // WIN - 1

    def blk(width, off, back):
        def imap(b, hh, i):
            ii = jnp.minimum(i, last)
            if back == 1:
                ii = jnp.maximum(ii - 1, 0)
            elif back == 2:
                ii = jnp.maximum(i - 1, 0)
            return (b, ii, off // width + hh)
        return pl.BlockSpec((None, WIN, width), imap)

    assert q_off % qw == 0 and k_off % LANES == 0 and v_off % LANES == 0 and D % qw == 0
    grid = (B, D // qw, n_steps)
    sink = pl.BlockSpec((None, 1, LANES), lambda b, hh, i: (hh, 0, 0))
    return grid, blk, sink, qw


def _swa_stack(ref, j, lo, dtype):
    parts = []
    for h in range(j * SWA_GROUP, (j + 1) * SWA_GROUP):
        g = h // 2
        parts.append(_to_half(ref[:, g * LANES:(g + 1) * LANES], h % 2, j, lo))
    return jnp.concatenate(parts, axis=0).astype(dtype)


def _swa_unstack(y4, j, lo):
    out = []
    for t in range(0, SWA_GROUP, 2):
        even = _to_half(y4[t * WIN:(t + 1) * WIN], j, 0, lo)
        odd = _to_half(y4[(t + 1) * WIN:(t + 2) * WIN], j, 1, lo)
        out.append(even + odd)
    return out


def _swa_cols(ref, j):
    cols = [jnp.broadcast_to(ref[:, h:h + 1], (WIN, 1)) for h in range(j * SWA_GROUP, (j + 1) * SWA_GROUP)]
    return jnp.concatenate(cols, axis=0)


def _swa_fwd(proj3, sinks, *, D, q_off, k_off, v_off):
    B, S, C = proj3.shape
    scale = HEAD_DIM ** -0.5
    grid, blk, sink, qw = _swa_specs(B, S, D, C, q_off, k_off, v_off, S // WIN)
    nhb = D // qw

    def body(q_ref, kp_ref, kc_ref, vp_ref, vc_ref, sk_ref, o_ref, lse_ref):
        lo = lax.broadcasted_iota(jnp.int32, (WIN, LANES), 1) < HEAD_DIM
        lane = lax.broadcasted_iota(jnp.int32, (WIN, LANES), 1)
        bias_all, bias_first = _dil_bias(True, SWA_GROUP)
        bias = jnp.where(pl.program_id(2) > 0, bias_all, bias_first)
        k2 = jnp.concatenate([kp_ref[...], kc_ref[...]], axis=0).astype(BF16)
        v2 = jnp.concatenate([vp_ref[...], vc_ref[...]], axis=0).astype(BF16)
        lse_acc = jnp.zeros((WIN, LANES), F32)
        for j in range(2):
            q4 = _swa_stack(q_ref, j, lo, BF16)
            sk4 = _swa_cols(sk_ref, j)
            s4 = _dot_nt(q4, k2) * scale + bias
            m = jnp.maximum(jnp.max(s4, axis=1, keepdims=True), sk4)
            p = jnp.exp(s4 - m)
            den = jnp.sum(p, axis=1, keepdims=True) + jnp.exp(sk4 - m)
            o4 = _dot_nn(p.astype(BF16), v2) / den
            l4 = m + jnp.log(den)
            for t, grp in enumerate(_swa_unstack(o4, j, lo)):
                g = j * (SWA_GROUP // 2) + t
                o_ref[:, g * LANES:(g + 1) * LANES] = grp
            for t in range(SWA_GROUP):
                lse_acc = jnp.where(lane == j * SWA_GROUP + t, l4[t * WIN:(t + 1) * WIN], lse_acc)
        lse_ref[...] = lse_acc

    o, lse = _pcall(
        body, name="swa_fwd", grid=grid,
        in_specs=[blk(qw, q_off, 0), blk(LANES, k_off, 1), blk(LANES, k_off, 0), blk(LANES, v_off, 1), blk(LANES, v_off, 0), sink],
        out_specs=[blk(qw, 0, 0), blk(LANES, 0, 0)],
        out_shape=[jax.ShapeDtypeStruct((B, S, D), F32), jax.ShapeDtypeStruct((B, S, nhb * LANES), F32)],
        compiler_params=_params("parallel", "parallel", "parallel"),
    )(proj3, proj3, proj3, proj3, proj3, sinks)
    return o, lse


def _swa_bwd(proj3, o3, lse3, do3, sinks, *, D, q_off, k_off, v_off):
    B, S, C = proj3.shape
    nblk = S // WIN
    scale = HEAD_DIM ** -0.5
    grid, blk, sink, qw = _swa_specs(B, S, D, C, q_off, k_off, v_off, nblk + 1)
    nhb = D // qw
    KV = D // SWA_GROUP

    def body(q_ref, kp_ref, kc_ref, vp_ref, vc_ref, o_ref, l_ref, do_ref, sk_ref, dq_ref, dk_ref, dv_ref, dsk_ref, ck_ref, cv_ref):
        i = pl.program_id(2)

        @pl.when(i == 0)
        def _():
            ck_ref[...] = jnp.zeros_like(ck_ref)
            cv_ref[...] = jnp.zeros_like(cv_ref)
            dsk_ref[...] = jnp.zeros_like(dsk_ref)

        @pl.when(i < nblk)
        def _():
            lo = lax.broadcasted_iota(jnp.int32, (WIN, LANES), 1) < HEAD_DIM
            lane = lax.broadcasted_iota(jnp.int32, (1, LANES), 1)
            bias_all, bias_first = _dil_bias(True, SWA_GROUP)
            bias = jnp.where(i > 0, bias_all, bias_first)
            k2 = jnp.concatenate([kp_ref[...], kc_ref[...]], axis=0).astype(BF16)
            v2 = jnp.concatenate([vp_ref[...], vc_ref[...]], axis=0).astype(BF16)
            lane_w = lax.broadcasted_iota(jnp.int32, (WIN, LANES), 1)
            dd = jnp.zeros((WIN, LANES), F32)
            for g in range(qw // LANES):
                prod = do_ref[:, g * LANES:(g + 1) * LANES] * o_ref[:, g * LANES:(g + 1) * LANES]
                dd = jnp.where(lane_w == 2 * g, jnp.sum(jnp.where(lo, prod, 0.0), axis=1, keepdims=True), dd)
                dd = jnp.where(lane_w == 2 * g + 1, jnp.sum(jnp.where(lo, 0.0, prod), axis=1, keepdims=True), dd)
            dk2 = jnp.zeros((2 * WIN, LANES), F32)
            dv2 = jnp.zeros((2 * WIN, LANES), F32)
            dsk_acc = jnp.zeros((1, LANES), F32)
            for j in range(2):
                q4 = _swa_stack(q_ref, j, lo, BF16)
                do4 = _swa_stack(do_ref, j, lo, BF16)
                heads = range(j * SWA_GROUP, (j + 1) * SWA_GROUP)
                l4 = jnp.concatenate([l_ref[:, h:h + 1] for h in heads], axis=0)
                dd4 = jnp.concatenate([dd[:, h:h + 1] for h in heads], axis=0)
                p4 = jnp.exp(_dot_nt(q4, k2) * scale + bias - l4)
                ds4 = (p4 * (_dot_nt(do4, v2) - dd4) * scale).astype(BF16)
                for t, grp in enumerate(_swa_unstack(_dot_nn(ds4, k2), j, lo)):
                    g = j * (SWA_GROUP // 2) + t
                    dq_ref[:, g * LANES:(g + 1) * LANES] = grp.astype(BF16)
                dk2 = dk2 + _dot_tn(ds4, q4)
                dv2 = dv2 + _dot_tn(p4.astype(BF16), do4)
                for h in heads:
                    dsk_h = -jnp.sum(jnp.exp(sk_ref[:, h:h + 1] - l_ref[:, h:h + 1]) * dd[:, h:h + 1], axis=0, keepdims=True)
                    dsk_acc = jnp.where(lane == h, dsk_h, dsk_acc)
            dk_ref[...] = (dk2[:WIN] + ck_ref[...]).astype(BF16)
            dv_ref[...] = (dv2[:WIN] + cv_ref[...]).astype(BF16)
            ck_ref[...] = dk2[WIN:]
            cv_ref[...] = dv2[WIN:]
            dsk_ref[...] += dsk_acc

        @pl.when(i == nblk)
        def _():
            dk_ref[...] = ck_ref[...].astype(BF16)
            dv_ref[...] = cv_ref[...].astype(BF16)

    res = _pcall(
        body, name="swa_bwd", grid=grid,
        in_specs=[blk(qw, q_off, 0), blk(LANES, k_off, 1), blk(LANES, k_off, 0), blk(LANES, v_off, 1), blk(LANES, v_off, 0),
                  blk(qw, 0, 0), blk(LANES, 0, 0), blk(qw, 0, 0), sink],
        out_specs=[blk(qw, 0, 0), blk(LANES, 0, 2), blk(LANES, 0, 2),
                   pl.BlockSpec((None, None, 1, LANES), lambda b, hh, i: (b, hh, 0, 0))],
        out_shape=[jax.ShapeDtypeStruct((B, S, D), BF16), jax.ShapeDtypeStruct((B, S, KV), BF16),
                   jax.ShapeDtypeStruct((B, S, KV), BF16), jax.ShapeDtypeStruct((B, nhb, 1, LANES), F32)],
        scratch_shapes=[pltpu.VMEM((WIN, LANES), F32), pltpu.VMEM((WIN, LANES), F32)],
        compiler_params=_params("parallel", "parallel", "arbitrary"),
    )(proj3, proj3, proj3, proj3, proj3, o3, lse3, do3, sinks)
    return res


def _branch_fwd(ys, wb, proj, *, D, g_off):
    T = proj.shape[0]
    tm, tn = _tile(T, 256), _tile(D, 512)
    n = len(ys)

    def body(*refs):
        y_refs, w_ref, g_refs, br_ref, mg_ref = refs[:n], refs[n], refs[n + 1:2 * n + 1], refs[2 * n + 1], refs[2 * n + 2]
        acc = None
        for k in range(n):
            br = _dot_nn(y_refs[k][...].astype(BF16), w_ref[k])
            br_ref[k] = br
            term = _sigmoid(g_refs[k][...]) * br
            acc = term if acc is None else acc + term
        mg_ref[...] = acc.astype(BF16)

    gate = lambda k: pl.BlockSpec((tm, tn), lambda i, j: (i, (g_off + k * D) // tn + j))
    return _pcall(
        body, name="branch_fwd", grid=(T // tm, D // tn),
        in_specs=[pl.BlockSpec((tm, D), lambda i, j: (i, 0))] * n + [pl.BlockSpec((n, D, tn), lambda i, j: (0, 0, j))]
        + [gate(k) for k in range(n)],
        out_specs=[pl.BlockSpec((n, tm, tn), lambda i, j: (0, i, j)), pl.BlockSpec((tm, tn), lambda i, j: (i, j))],
        out_shape=[jax.ShapeDtypeStruct((n, T, D), F32), jax.ShapeDtypeStruct((T, D), BF16)],
        compiler_params=_params("parallel", "parallel"),
    )(*ys, wb, *([proj] * n))


def _branch_bwd(dmerged, branch, proj, *, D, g_off):
    n, T, _ = branch.shape
    tm, tn = _tile(T, 512), _tile(D, 512)

    def body(dm_ref, br_ref, *rest):
        g_refs, db_ref, dg_refs = rest[:n], rest[n], rest[n + 1:]
        dm = dm_ref[...]
        for k in range(n):
            sg = _sigmoid(g_refs[k][...])
            db_ref[k] = (sg * dm).astype(BF16)
            dg_refs[k][...] = (dm * br_ref[k] * sg * (1.0 - sg)).astype(BF16)

    gate = lambda k: pl.BlockSpec((tm, tn), lambda i, j: (i, (g_off + k * D) // tn + j))
    blk = pl.BlockSpec((tm, tn), lambda i, j: (i, j))
    res = _pcall(
        body, name="branch_bwd", grid=(T // tm, D // tn),
        in_specs=[blk, pl.BlockSpec((n, tm, tn), lambda i, j: (0, i, j))] + [gate(k) for k in range(n)],
        out_specs=[pl.BlockSpec((n, tm, tn), lambda i, j: (0, i, j))] + [blk] * n,
        out_shape=[jax.ShapeDtypeStruct((n, T, D), BF16)] + [jax.ShapeDtypeStruct((T, D), BF16)] * n,
        compiler_params=_params("parallel", "parallel"),
    )(dmerged, branch, *([proj] * n))
    return res[0], list(res[1:])


def _ln_fwd(xres, y, g, b, *, alpha):
    T, D = xres.shape
    tm = _tile(T, 512)

    def body(x_ref, y_ref, g_ref, b_ref, z_ref, o_ref):
        z = alpha * x_ref[...] + y_ref[...]
        mu = jnp.mean(z, axis=1, keepdims=True)
        zc = z - mu
        var = jnp.mean(zc * zc, axis=1, keepdims=True)
        z_ref[...] = z
        o_ref[...] = zc * lax.rsqrt(var + LN_EPS) * g_ref[...] + b_ref[...]

    blk = pl.BlockSpec((tm, D), lambda i: (i, 0))
    vec = pl.BlockSpec((1, D), lambda i: (0, 0))
    return _pcall(
        body, name="ln_fwd", grid=(T // tm,), in_specs=[blk, blk, vec, vec], out_specs=[blk, blk],
        out_shape=[jax.ShapeDtypeStruct((T, D), F32)] * 2, compiler_params=_params("parallel"),
    )(xres, y, g, b)


def _ln_bwd(dout, z, g):
    T, D = z.shape
    tm = _tile(T, 512)

    def body(do_ref, z_ref, g_ref, dz_ref, dg_ref, db_ref):
        z = z_ref[...]
        do = do_ref[...]
        mu = jnp.mean(z, axis=1, keepdims=True)
        zc = z - mu
        rstd = lax.rsqrt(jnp.mean(zc * zc, axis=1, keepdims=True) + LN_EPS)
        xhat = zc * rstd
        dxh = do * g_ref[...]
        dz_ref[...] = rstd * (dxh - jnp.mean(dxh, axis=1, keepdims=True) - xhat * jnp.mean(dxh * xhat, axis=1, keepdims=True))
        dg = jnp.sum(do * xhat, axis=0, keepdims=True)
        db = jnp.sum(do, axis=0, keepdims=True)
        first = pl.program_id(0) == 0

        @pl.when(first)
        def _():
            dg_ref[...] = dg
            db_ref[...] = db

        @pl.when(jnp.logical_not(first))
        def _():
            dg_ref[...] += dg
            db_ref[...] += db

    blk = pl.BlockSpec((tm, D), lambda i: (i, 0))
    vec = pl.BlockSpec((1, D), lambda i: (0, 0))
    return _pcall(
        body, name="ln_bwd", grid=(T // tm,), in_specs=[blk, blk, vec], out_specs=[blk, vec, vec],
        out_shape=[jax.ShapeDtypeStruct((T, D), F32), jax.ShapeDtypeStruct((1, D), F32), jax.ShapeDtypeStruct((1, D), F32)],
        compiler_params=_params("arbitrary"),
    )(dout, z, g)


def _swiglu_fwd(hh):
    T, F2 = hh.shape
    Fh = F2 // 2
    tm, tn = _tile(T, 256), _tile(Fh, 1408)
    nj = Fh // tn

    def body(h1_ref, h3_ref, f_ref):
        h1 = h1_ref[...]
        f_ref[...] = (h1 * _sigmoid(h1) * h3_ref[...]).astype(BF16)

    return _pcall(
        body, name="swiglu_fwd", grid=(T // tm, nj),
        in_specs=[pl.BlockSpec((tm, tn), lambda i, j: (i, j)), pl.BlockSpec((tm, tn), lambda i, j: (i, nj + j))],
        out_specs=pl.BlockSpec((tm, tn), lambda i, j: (i, j)),
        out_shape=jax.ShapeDtypeStruct((T, Fh), BF16), compiler_params=_params("parallel", "parallel"),
    )(hh, hh)


def _swiglu_bwd(hh, df):
    T, F2 = hh.shape
    Fh = F2 // 2
    tm, tn = _tile(T, 256), _tile(Fh, 1408)
    nj = Fh // tn

    def body(h1_ref, h3_ref, df_ref, d1_ref, d3_ref):
        h1 = h1_ref[...]
        sg = _sigmoid(h1)
        d = df_ref[...]
        d1_ref[...] = (d * h3_ref[...] * sg * (1.0 + h1 * (1.0 - sg))).astype(BF16)
        d3_ref[...] = (d * h1 * sg).astype(BF16)

    lo = pl.BlockSpec((tm, tn), lambda i, j: (i, j))
    hi = pl.BlockSpec((tm, tn), lambda i, j: (i, nj + j))
    d1, d3 = _pcall(
        body, name="swiglu_bwd", grid=(T // tm, nj), in_specs=[lo, hi, lo], out_specs=[lo, lo],
        out_shape=[jax.ShapeDtypeStruct((T, Fh), BF16)] * 2, compiler_params=_params("parallel", "parallel"),
    )(hh, hh, df)
    return jnp.concatenate([d1, d3], axis=1)


def _loss_head(y, target):
    T, D = y.shape
    tm = _tile(T, 512)

    def body(y_ref, t_ref, dy_ref, l_ref):
        e = y_ref[...] - t_ref[...]
        dy_ref[...] = e * (1.0 / D)
        sq = e * e
        part = sq[:, 0:LANES]
        for c in range(1, D // LANES):
            part = part + sq[:, c * LANES:(c + 1) * LANES]
        part = jnp.sum(part, axis=0, keepdims=True) * (0.5 / D)
        first = pl.program_id(0) == 0

        @pl.when(first)
        def _():
            l_ref[...] = part

        @pl.when(jnp.logical_not(first))
        def _():
            l_ref[...] += part

    blk = pl.BlockSpec((tm, D), lambda i: (i, 0))
    return _pcall(
        body, name="loss_head", grid=(T // tm,), in_specs=[blk, blk],
        out_specs=[blk, pl.BlockSpec((1, LANES), lambda i: (0, 0))],
        out_shape=[jax.ShapeDtypeStruct((T, D), F32), jax.ShapeDtypeStruct((1, LANES), F32)],
        compiler_params=_params("arbitrary"),
    )(y, target)


def _as_rows(a):
    return a.reshape(-1, a.shape[-1])


def _adamw(w, g, m, v):
    w2, g2, m2, v2 = (_as_rows(t) for t in (w, g, m, v))
    R, Cc = w2.shape
    cap = max(SUBLANES, min(512, (256 * 1024) // Cc))
    tm = R if (R <= cap or R % SUBLANES) else max(t for t in range(SUBLANES, cap + 1, SUBLANES) if R % t == 0)
    c1 = 1.0 - ADAM_B1 ** ADAM_STEP
    c2 = 1.0 - ADAM_B2 ** ADAM_STEP

    def body(w_ref, g_ref, m_ref, v_ref, d_ref, nm_ref, nv_ref):
        gg = g_ref[...]
        nm = ADAM_B1 * m_ref[...] + (1.0 - ADAM_B1) * gg
        nv = ADAM_B2 * v_ref[...] + (1.0 - ADAM_B2) * (gg * gg)
        d_ref[...] = (-ADAM_LR) * ((nm / c1) / (jnp.sqrt(nv / c2) + ADAM_EPS) + ADAM_WD * w_ref[...])
        nm_ref[...] = nm
        nv_ref[...] = nv

    blk = pl.BlockSpec((tm, Cc), lambda i: (i, 0))
    res = _pcall(
        body, name="adamw", grid=(R // tm,), in_specs=[blk] * 4, out_specs=[blk] * 3,
        out_shape=[jax.ShapeDtypeStruct((R, Cc), F32)] * 3, compiler_params=_params("parallel"),
    )(w2, g2, m2, v2)
    return tuple(t.reshape(w.shape) for t in res)


def _where_am_i():
    x, y, c = lax.axis_index("x"), lax.axis_index("y"), lax.axis_index("c")
    chips = [(1 - x, y), (x, 1 - y), (1 - x, 1 - y)]
    return x, y, c, chips


def _remote(src, dst, send_sems, recv_sems, k, to):
    return pltpu.make_async_remote_copy(src_ref=src, dst_ref=dst, send_sem=send_sems.at[k], recv_sem=recv_sems.at[k],
                                        device_id=to, device_id_type=MESH)


def _comm_call(body, name, ins, out_shapes, n_remote, n_local):
    return _pcall_comm(
        body, name=name, in_specs=[ANY] * len(ins), out_specs=[ANY] * len(out_shapes), out_shape=out_shapes,
        scratch_shapes=[pltpu.SemaphoreType.DMA((n_remote,)), pltpu.SemaphoreType.DMA((n_remote,)),
                        pltpu.SemaphoreType.DMA((max(n_local, 1),))],
    )(*ins)


def _gather_weights(shards):
    n = len(shards)

    def body(*refs):
        ins, outs = refs[:n], refs[n:2 * n]
        send_sems, recv_sems, local_sems = refs[2 * n:]
        x, y, c, chips = _where_am_i()
        s = 2 * x + y
        sib = (x, y, 1 - c)
        first = []
        for t in range(n):
            for j, (cx, cy) in enumerate(chips):
                first.append(_remote(ins[t].at[:, c], outs[t].at[:, s, c], send_sems, recv_sems, 6 * t + j, (cx, cy, c)))
        for cp in first:
            cp.start()
        passed = []
        for j, (cx, cy) in enumerate(chips):
            sj = 2 * cx + cy
            for t in range(n):
                land = outs[t].at[:, sj, c]
                _remote(land, land, send_sems, recv_sems, 6 * t + j, (cx, cy, c)).wait_recv()
                fw = _remote(land, land, send_sems, recv_sems, 6 * t + 3 + j, sib)
                fw.start()
                passed.append(fw)
        for j, (cx, cy) in enumerate(chips):
            sj = 2 * cx + cy
            for t in range(n):
                land = outs[t].at[:, sj, 1 - c]
                _remote(land, land, send_sems, recv_sems, 6 * t + 3 + j, sib).wait_recv()
        for cp in first + passed:
            cp.wait_send()

    out_shapes = [jax.ShapeDtypeStruct((t.shape[0], N_CHIPS) + t.shape[1:], t.dtype) for t in shards]
    got = _comm_call(body, "gather_weights", shards, out_shapes, 6 * n, 0)
    s = 2 * lax.axis_index("x") + lax.axis_index("y")
    return [lax.dynamic_update_slice(g, t[:, None], (0, s, 0, 0, 0)) for g, t in zip(got, shards)]


def _gather_rider(shards):
    n = len(shards)

    def copies(ins, outs, send_sems, recv_sems):
        x, y, c, chips = _where_am_i()
        s = 2 * x + y
        return [_remote(ins[t].at[:, c], outs[t].at[:, s, c], send_sems, recv_sems, 3 * t + j, (cx, cy, c))
                for t in range(n) for j, (cx, cy) in enumerate(chips)]

    def start(ins, outs, send_sems, recv_sems):
        for cp in copies(ins, outs, send_sems, recv_sems):
            cp.start()

    def finish(ins, outs, send_sems, recv_sems):
        x, y, c, chips = _where_am_i()
        for t in range(n):
            for j, (cx, cy) in enumerate(chips):
                land = outs[t].at[:, 2 * cx + cy, c]
                _remote(land, land, send_sems, recv_sems, 3 * t + j, (cx, cy, c)).wait_recv()
        for cp in copies(ins, outs, send_sems, recv_sems):
            cp.wait_send()

    out_shapes = [jax.ShapeDtypeStruct((t.shape[0], N_CHIPS) + t.shape[1:], t.dtype) for t in shards]
    return dict(name="gather", ins=list(shards), out_shapes=out_shapes, n=3 * n, start=start, finish=finish)


def _gather_forward(landed, shards):
    n = len(landed)

    def body(*refs):
        outs = refs[n:2 * n]
        send_sems, recv_sems, _ = refs[2 * n:]
        x, y, c, chips = _where_am_i()
        sib = (x, y, 1 - c)
        cps = []
        for t in range(n):
            for j, (cx, cy) in enumerate(chips):
                land = outs[t].at[:, 2 * cx + cy, c]
                cps.append(_remote(land, land, send_sems, recv_sems, 3 * t + j, sib))
        for cp in cps:
            cp.start()
        for t in range(n):
            for j, (cx, cy) in enumerate(chips):
                land = outs[t].at[:, 2 * cx + cy, 1 - c]
                _remote(land, land, send_sems, recv_sems, 3 * t + j, sib).wait_recv()
        for cp in cps:
            cp.wait_send()

    got = _pcall_comm(
        body, name="gather_forward", in_specs=[ANY] * n, out_specs=[ANY] * n,
        out_shape=[jax.ShapeDtypeStruct(t.shape, t.dtype) for t in landed], input_output_aliases={t: t for t in range(n)},
        scratch_shapes=[pltpu.SemaphoreType.DMA((3 * n,)), pltpu.SemaphoreType.DMA((3 * n,)), pltpu.SemaphoreType.DMA((1,))],
    )(*landed)
    s = 2 * lax.axis_index("x") + lax.axis_index("y")
    return [lax.dynamic_update_slice(g, t[:, None], (0, s, 0, 0, 0)) for g, t in zip(got, shards)]


def _gather_small(v):
    def body(v_ref, out_ref, send_sems, recv_sems, local_sems):
        x, y, c, chips = _where_am_i()
        s = 2 * x + y
        mine = pltpu.make_async_copy(v_ref, out_ref.at[s], local_sems.at[0])
        mine.start()
        sends = [_remote(v_ref, out_ref.at[s], send_sems, recv_sems, j, (cx, cy, c)) for j, (cx, cy) in enumerate(chips)]
        for cp in sends:
            cp.start()
        for j, (cx, cy) in enumerate(chips):
            land = out_ref.at[2 * cx + cy]
            _remote(land, land, send_sems, recv_sems, j, (cx, cy, c)).wait_recv()
        for cp in sends:
            cp.wait_send()
        mine.wait()

    return _comm_call(body, "gather_small", [v], [jax.ShapeDtypeStruct((N_CHIPS,) + v.shape, v.dtype)], 3, 1)[0]


def _swap_sibling_halves(grads):
    n = len(grads)

    def body(*refs):
        ins, outs = refs[:n], refs[n:2 * n]
        send_sems, recv_sems, _ = refs[2 * n:]
        x, y, c, _chips = _where_am_i()
        sib = (x, y, 1 - c)
        cps = [_remote(ins[t].at[:, :, 1 - c], outs[t], send_sems, recv_sems, t, sib) for t in range(n)]
        for cp in cps:
            cp.start()
        for cp in cps:
            cp.wait()

    out_shapes = [jax.ShapeDtypeStruct(g.shape[:2] + g.shape[3:], g.dtype) for g in grads]
    return _comm_call(body, "grad_swap_halves", grads, out_shapes, n, 0)


def _exchange_chips(parts):
    n = len(parts)

    def body(*refs):
        ins, outs = refs[:n], refs[n:2 * n]
        send_sems, recv_sems, _ = refs[2 * n:]
        x, y, c, chips = _where_am_i()
        cps = []
        for t in range(n):
            for j, (cx, cy) in enumerate(chips):
                cps.append(_remote(ins[t].at[:, 2 * cx + cy], outs[t].at[j], send_sems, recv_sems, 3 * t + j, (cx, cy, c)))
        for cp in cps:
            cp.start()
        for cp in cps:
            cp.wait()

    out_shapes = [jax.ShapeDtypeStruct((3, p.shape[0]) + p.shape[2:], p.dtype) for p in parts]
    return _comm_call(body, "grad_exchange_chips", parts, out_shapes, 3 * n, 0)


def _exchange_rider(parts):
    n = len(parts)

    def copies(ins, outs, send_sems, recv_sems):
        x, y, c, chips = _where_am_i()
        return [_remote(ins[t].at[:, 2 * cx + cy], outs[t].at[j], send_sems, recv_sems, 3 * t + j, (cx, cy, c))
                for t in range(n) for j, (cx, cy) in enumerate(chips)]

    def start(ins, outs, send_sems, recv_sems):
        for cp in copies(ins, outs, send_sems, recv_sems):
            cp.start()

    def finish(ins, outs, send_sems, recv_sems):
        for cp in copies(ins, outs, send_sems, recv_sems):
            cp.wait()

    out_shapes = [jax.ShapeDtypeStruct((3, p.shape[0]) + p.shape[2:], p.dtype) for p in parts]
    return dict(name="exchange", ins=list(parts), out_shapes=out_shapes, n=3 * n, start=start, finish=finish)


def _join_sibling_halves(halves):
    n = len(halves)

    def body(*refs):
        ins, outs = refs[:n], refs[n:2 * n]
        send_sems, recv_sems, local_sems = refs[2 * n:]
        x, y, c, _chips = _where_am_i()
        sib = (x, y, 1 - c)
        cps = [_remote(ins[t], outs[t].at[:, c], send_sems, recv_sems, t, sib) for t in range(n)]
        for cp in cps:
            cp.start()
        for t in range(n):
            land = outs[t].at[:, 1 - c]
            _remote(land, land, send_sems, recv_sems, t, sib).wait_recv()
        for cp in cps:
            cp.wait_send()

    out_shapes = [jax.ShapeDtypeStruct((h.shape[0], 2) + h.shape[1:], h.dtype) for h in halves]
    got = _comm_call(body, "grad_join_halves", halves, out_shapes, n, 0)
    c = lax.axis_index("c")
    return [lax.dynamic_update_slice(g, h[:, None], (0, c, 0, 0)) for g, h in zip(got, halves)]


def _swap_small(v):
    def body(v_ref, out_ref, send_sems, recv_sems, _):
        x, y, c, _chips = _where_am_i()
        cp = _remote(v_ref, out_ref, send_sems, recv_sems, 0, (x, y, 1 - c))
        cp.start()
        cp.wait()

    return _comm_call(body, "small_swap", [v], [jax.ShapeDtypeStruct(v.shape, v.dtype)], 1, 0)[0]


def _exchange_small(v):
    def body(v_ref, out_ref, send_sems, recv_sems, _):
        x, y, c, chips = _where_am_i()
        cps = [_remote(v_ref, out_ref.at[j], send_sems, recv_sems, j, (cx, cy, c)) for j, (cx, cy) in enumerate(chips)]
        for cp in cps:
            cp.start()
        for cp in cps:
            cp.wait()

    return _comm_call(body, "small_exchange", [v], [jax.ShapeDtypeStruct((3,) + v.shape, v.dtype)], 3, 0)[0]


def _sum_rows(name, terms, out_dtypes):
    R, Cc = terms[0].shape
    tm = R if R <= 256 else max(t for t in range(16, 257, 16) if R % t == 0)
    n = len(terms)

    def body(*refs):
        acc = refs[0][...].astype(F32)
        for r in refs[1:n]:
            acc = acc + r[...].astype(F32)
        for o in refs[n:]:
            o[...] = acc.astype(o.dtype)

    blk = pl.BlockSpec((tm, Cc), lambda i: (i, 0))
    return _pcall(
        body, name=name, grid=(R // tm,), in_specs=[blk] * n, out_specs=[blk] * len(out_dtypes),
        out_shape=[jax.ShapeDtypeStruct((R, Cc), d) for d in out_dtypes], compiler_params=_params("parallel"),
    )(*terms)


def _pair_sum(g5, r1, core, shard):
    A4, _, Rh, Cc = g5.shape
    A = A4 // N_CHIPS
    tr = Rh if Rh <= 256 else max(t for t in range(16, 257, 16) if Rh % t == 0)

    def body(core_ref, shard_ref, g_ref, r_ref, qb_ref, qf_ref):
        q = g_ref[...] + r_ref[...]
        qb_ref[...] = q.astype(BF16)

        @pl.when(pl.program_id(2) == shard_ref[0])
        def _():
            qf_ref[...] = q

    grid_spec = pltpu.PrefetchScalarGridSpec(
        num_scalar_prefetch=2, grid=(A, Rh // tr, N_CHIPS),
        in_specs=[pl.BlockSpec((None, None, tr, Cc), lambda a, r, sh, core, shard: (a * N_CHIPS + sh, core[0], r, 0)),
                  pl.BlockSpec((None, tr, Cc), lambda a, r, sh, core, shard: (a * N_CHIPS + sh, r, 0))],
        out_specs=[pl.BlockSpec((None, tr, Cc), lambda a, r, sh, core, shard: (a * N_CHIPS + sh, r, 0)),
                   pl.BlockSpec((None, tr, Cc), lambda a, r, sh, core, shard: (a, r, 0))],
    )
    return _pcall(
        body, name="grad_pair_sum", grid_spec=grid_spec,
        out_shape=[jax.ShapeDtypeStruct((A4, Rh, Cc), BF16), jax.ShapeDtypeStruct((A, Rh, Cc), F32)],
        compiler_params=_params("parallel", "parallel", "arbitrary"),
    )(core, shard, g5, r1)


def _reduce_chip(grads, core, shard):
    r1 = _swap_sibling_halves(grads)
    qb, qf = [], []
    for g, r in zip(grads, r1):
        A, _, _, Rh, Cc = g.shape
        b, f = _pair_sum(g.reshape(A * N_CHIPS, 2, Rh, Cc), r.reshape(A * N_CHIPS, Rh, Cc), core, shard)
        qb.append(b.reshape(A, N_CHIPS, Rh, Cc))
        qf.append(f)
    return qb, qf


def _reduce_finish(qf, r2):
    halves = []
    for f, r in zip(qf, r2):
        A, Rh, Cc = f.shape
        terms = [f.reshape(A * Rh, Cc)] + [r[j].reshape(A * Rh, Cc) for j in range(3)]
        halves.append(_sum_rows("grad_chip_sum", terms, [F32])[0].reshape(A, Rh, Cc))
    full = _join_sibling_halves(halves)
    return [t.reshape(t.shape[0], 2 * t.shape[2], t.shape[3]) for t in full]


def _allreduce_small(v):
    pair = _sum_rows("small_pair_sum", [v, _swap_small(v)], [F32])[0]
    others = _exchange_small(pair)
    x, y = lax.axis_index("x"), lax.axis_index("y")
    s = 2 * x + y
    stack = jnp.concatenate([pair[None], others], axis=0)
    src = jnp.stack([s, s ^ 2, s ^ 1, s ^ 3])
    order = jnp.argsort(src)
    terms = [lax.dynamic_index_in_dim(stack, order[k], 0, keepdims=False) for k in range(N_CHIPS)]
    return _sum_rows("small_chip_sum", terms, [F32])[0]


def _block_diag(w):
    nb, bw, _ = w.shape
    per = LANES // bw
    w = w.reshape(nb // per, per, bw, bw)
    eye = jnp.eye(per, dtype=w.dtype)
    bd = jnp.einsum("tpij,pq->tpiqj", w, eye).reshape(nb // per, LANES, LANES)
    return bd.astype(BF16)


def _block_diag_grad(g, bw):
    nt = g.shape[0]
    per = LANES // bw
    g = g.reshape(nt, per, bw, per, bw)
    return jnp.stack([g[:, p, :, p, :] for p in range(per)], axis=1).reshape(nt * per, bw, bw)


def _split5(w):
    R, Cc = w.shape[-2:]
    return w.reshape(-1, 2, R // 2, Cc)


def kernel(x, w_in, conv_w, conv_b, w_rg, b_rg, w_ig, b_ig, lru_lambda, sinks, w_branch, w_out, ln1_g, ln1_b, w_ffn_in, w_ffn_out, ln2_g, ln2_b, loss_target, m_w_in, m_conv_w, m_conv_b, m_w_rg, m_b_rg, m_w_ig, m_b_ig, m_lru_lambda, m_sinks, m_w_branch, m_w_out, m_ln1_g, m_ln1_b, m_w_ffn_in, m_w_ffn_out, m_ln2_g, m_ln2_b, v_w_in, v_conv_w, v_conv_b, v_w_rg, v_b_rg, v_w_ig, v_b_ig, v_lru_lambda, v_sinks, v_w_branch, v_w_out, v_ln1_g, v_ln1_b, v_w_ffn_in, v_w_ffn_out, v_ln2_g, v_ln2_b):
    B, S, D = x.shape
    T = B * S
    L = w_in.shape[0]
    H = D // HEAD_DIM
    KVB = D // SWA_GROUP
    FH = w_ffn_out.shape[1] * N_CHIPS
    C = w_in.shape[2] * N_CHIPS
    alpha = (2.0 * L) ** 0.25
    off = {}
    pos = 0
    for nm, wd in (("lx", D), ("lg", D), ("qb", D), ("kb", KVB), ("vb", KVB), ("qc", D), ("kc", D), ("vc", D), ("gt", 3 * D)):
        off[nm] = pos
        pos += wd
    assert pos == C
    cx, cy, cc = lax.axis_index("x"), lax.axis_index("y"), lax.axis_index("c")
    shard = (2 * cx + cy).astype(jnp.int32)
    core_a = cc.astype(jnp.int32).reshape(1)
    shard_a = shard.reshape(1)

    def shard_views(l):
        return [_split5(w_in[l].astype(BF16)), _split5(w_branch[l].astype(BF16)), _split5(w_out[l].astype(BF16)),
                _split5(w_ffn_in[l].astype(BF16)), _split5(w_ffn_out[l].astype(BF16))]

    def as_weights(g):
        return dict(
            w_in=g[0].reshape(N_CHIPS, D, C // N_CHIPS),
            w_branch=g[1].reshape(3, D, D),
            w_out=g[2].reshape(D, D),
            w_ffn_in=g[3].reshape(N_CHIPS, D, 2 * FH // N_CHIPS),
            w_ffn_out=g[4].reshape(FH, D),
        )

    full = [as_weights(_gather_weights(shard_views(0)))]
    cw_all = _gather_small(conv_w.reshape(L * CONV_WIDTH, D // N_CHIPS))
    conv_w_full = jnp.transpose(cw_all, (1, 0, 2)).reshape(L, CONV_WIDTH, D)

    def layer_params(l):
        return dict(conv_w=conv_w_full[l], conv_b=conv_b[l][None], w_rg_bd=_block_diag(w_rg[l]), b_rg=b_rg[l][None],
                    w_ig_bd=_block_diag(w_ig[l]), b_ig=b_ig[l][None], lam=lru_lambda[l][None])

    def sink_rows(l, hb):
        sk = sinks[l].reshape(H // hb, 1, hb)
        return jnp.pad(sk, ((0, 0), (0, 0), (0, LANES - hb)))

    hb_b = SWA_HB

    saved = []
    xin = x.reshape(T, D)
    for l in range(L):
        fw, lp = full[l], layer_params(l)
        proj = _matmul(xin, fw["w_in"], mode="nn", name="mm_proj", tm=512, n_outer=True)
        proj3 = proj.reshape(B, S, C)
        h3, ya3 = _lru_fwd(proj3, lp, D=D, x_off=off["lx"], g_off=off["lg"])
        skr = sink_rows(l, hb_b)
        yb3, lse_b = _swa_fwd(proj3, skr, D=D, q_off=off["qb"], k_off=off["kb"], v_off=off["vb"])
        dil_kw = dict(D=D, q_off=off["qc"], k_off=off["kc"], v_off=off["vc"])
        if l + 1 < L:
            nxt = shard_views(l + 1)
            res = _dil_fwd(proj3, rider=_gather_rider(nxt), **dil_kw)
            yc3, lse_c = res[0], res[1]
            full.append(as_weights(_gather_forward(res[2:], nxt)))
        else:
            yc3, lse_c = _dil_fwd(proj3, **dil_kw)
        ya, yb, yc = ya3.reshape(T, D), yb3.reshape(T, D), yc3.reshape(T, D)
        branch, merged = _branch_fwd([ya, yb, yc], fw["w_branch"], proj, D=D, g_off=off["gt"])
        mix = _matmul(merged, fw["w_out"], mode="nn", name="mm_out", tn=1024)
        z1, x1 = _ln_fwd(xin, mix, ln1_g[l][None], ln1_b[l][None], alpha=alpha)
        hh = _matmul(x1, fw["w_ffn_in"], mode="nn", name="mm_ffn_in", n_outer=True)
        f = _swiglu_fwd(hh)
        ffn = _matmul(f, fw["w_ffn_out"], mode="nn", name="mm_ffn_out", tn=1024, tk=4096)
        z2, x2 = _ln_fwd(x1, ffn, ln2_g[l][None], ln2_b[l][None], alpha=alpha)
        saved.append(dict(x=xin, proj=proj, h3=h3, ya=ya, yb=yb, lse_b=lse_b, yc=yc, lse_c=lse_c, branch=branch,
                          merged=merged, z1=z1, x1=x1, hh=hh, f=f, z2=z2, skr=skr))
        xin = x2

    dx, loss_rows = _loss_head(xin, loss_target.reshape(T, D))
    loss = lax.psum(jnp.sum(loss_rows), ("x", "y", "c"))

    big = {k: [None] * L for k in ("w_in", "w_branch", "w_out", "w_ffn_in", "w_ffn_out")}
    small = [None] * L

    def store_reduced(l, red):
        big["w_in"][l] = red[0].reshape(D, C // N_CHIPS)
        big["w_branch"][l] = red[1].reshape(3, D // N_CHIPS, D)
        big["w_out"][l] = red[2].reshape(D // N_CHIPS, D)
        big["w_ffn_in"][l] = red[3].reshape(D, 2 * FH // N_CHIPS)
        big["w_ffn_out"][l] = red[4].reshape(FH // N_CHIPS, D)

    pending = None
    for l in reversed(range(L)):
        fw, lp, sv = full[l], layer_params(l), saved[l]
        dz2, dg2, db2 = _ln_bwd(dx, sv["z2"], ln2_g[l][None])
        df = _matmul(dz2, fw["w_ffn_out"], mode="nt", name="mm_dffn_out_x", tn=4096, tk=1024)
        g_ffn_out = _matmul(sv["f"], dz2, mode="tn", name="mm_dffn_out_w", tm=1408, tn=1024, tk=1024)
        dhh = _swiglu_bwd(sv["hh"], df)
        dx1 = _matmul(dhh, fw["w_ffn_in"], mode="nt", name="mm_dffn_in_x", tm=1024, tn=1024, resid=dz2, rs=alpha)
        g_ffn_in = _matmul(sv["x1"], dhh, mode="tn", name="mm_dffn_in_w", tm=1024, tk=1024, out_shards=N_CHIPS)
        dz1, dg1, db1 = _ln_bwd(dx1, sv["z1"], ln1_g[l][None])
        dmerged = _matmul(dz1, fw["w_out"], mode="nt", name="mm_dout_x", tn=1024, tk=1024)
        g_out = _matmul(sv["merged"], dz1, mode="tn", name="mm_dout_w", tm=1024, tn=1024, tk=1024)
        dbranch, dgates = _branch_bwd(dmerged, sv["branch"], sv["proj"], D=D, g_off=off["gt"])
        ys = [sv["ya"], sv["yb"], sv["yc"]]
        dys, g_branch = [], []
        for n in range(3):
            dys.append(_matmul(dbranch[n], fw["w_branch"][n], mode="nt", name="mm_dbranch_x", tn=1024, tk=1024))
            g_branch.append(_matmul(ys[n], dbranch[n], mode="tn", name="mm_dbranch_w", tm=1024, tn=1024, tk=1024))
        proj3 = sv["proj"].reshape(B, S, C)
        r3 = lambda t: t.reshape(B, S, t.shape[-1])
        lru = _lru_bwd(proj3, sv["h3"], r3(dys[0]), lp, D=D, x_off=off["lx"], g_off=off["lg"])
        dxr, dgate = lru[0], lru[1]
        dqb, dkb, dvb, dsk = _swa_bwd(proj3, r3(sv["yb"]), sv["lse_b"], r3(dys[1]), sv["skr"], D=D, q_off=off["qb"],
                                      k_off=off["kb"], v_off=off["vb"])
        dil_kw = dict(D=D, q_off=off["qc"], k_off=off["kc"], v_off=off["vc"])
        if pending is None:
            acc = _dil_bwd(proj3, r3(sv["yc"]), sv["lse_c"], r3(dys[2]), **dil_kw)
        else:
            res = _dil_bwd(proj3, r3(sv["yc"]), sv["lse_c"], r3(dys[2]), rider=_exchange_rider(pending[1]), **dil_kw)
            acc = res[:3]
            store_reduced(pending[0], _reduce_finish(pending[2], res[3:]))
        f2 = lambda t: t.reshape(T, t.shape[-1]).astype(BF16)
        dproj = jnp.concatenate([f2(dxr), f2(dgate), f2(dqb), f2(dkb), f2(dvb), f2(acc[0]), f2(acc[1]), f2(acc[2])] + dgates, axis=1)
        dx = _matmul(dproj, fw["w_in"], mode="nt", name="mm_dproj_x", tm=1024, tn=1024, resid=dz1, rs=alpha)
        g_in = _matmul(sv["x"], dproj, mode="tn", name="mm_dproj_w", tm=512, tk=1024, out_shards=N_CHIPS)

        g5 = [g_in.reshape(1, N_CHIPS, 2, D // 2, C // N_CHIPS),
              jnp.stack(g_branch).reshape(3, N_CHIPS, 2, D // N_CHIPS // 2, D),
              g_out.reshape(1, N_CHIPS, 2, D // N_CHIPS // 2, D),
              g_ffn_in.reshape(1, N_CHIPS, 2, D // 2, 2 * FH // N_CHIPS),
              g_ffn_out.reshape(1, N_CHIPS, 2, FH // N_CHIPS // 2, D)]
        pending = (l,) + _reduce_chip(g5, core_a, shard_a)

        dsinks = jnp.sum(dsk, axis=0)[:, 0, :hb_b].reshape(H)
        bw = w_rg.shape[-1]
        small[l] = [lru[2].reshape(-1), lru[3].reshape(-1), _block_diag_grad(lru[4], bw).reshape(-1), lru[5].reshape(-1),
                    _block_diag_grad(lru[6], bw).reshape(-1), lru[7].reshape(-1), lru[8].reshape(-1),
                    jnp.pad(dsinks, (0, LANES - H)), dg1.reshape(-1), db1.reshape(-1), dg2.reshape(-1), db2.reshape(-1)]

    store_reduced(pending[0], _reduce_finish(pending[2], _exchange_chips(pending[1])))

    sizes = [t.size for t in small[0]]
    flat = jnp.concatenate([t for l in range(L) for t in small[l]])
    n_flat = flat.size
    rows = -(-n_flat // (LANES * 256)) * 256
    flat = jnp.pad(flat, (0, rows * LANES - n_flat)).reshape(rows, LANES)
    red_small = _allreduce_small(flat).reshape(-1)
    per_layer = sum(sizes)
    names = ["conv_w", "conv_b", "w_rg", "b_rg", "w_ig", "b_ig", "lru_lambda", "sinks", "ln1_g", "ln1_b", "ln2_g", "ln2_b"]
    sg = {nm: [] for nm in names}
    for l in range(L):
        p = l * per_layer
        for nm, sz in zip(names, sizes):
            sg[nm].append(red_small[p:p + sz])
            p += sz
    grads = dict(
        w_in=jnp.stack(big["w_in"]), w_branch=jnp.stack(big["w_branch"]), w_out=jnp.stack(big["w_out"]),
        w_ffn_in=jnp.stack(big["w_ffn_in"]), w_ffn_out=jnp.stack(big["w_ffn_out"]),
        conv_w=lax.dynamic_slice_in_dim(jnp.stack(sg["conv_w"]).reshape(L, CONV_WIDTH, D), shard * (D // N_CHIPS), D // N_CHIPS, axis=2),
        conv_b=jnp.stack(sg["conv_b"]), w_rg=jnp.stack(sg["w_rg"]).reshape(w_rg.shape), b_rg=jnp.stack(sg["b_rg"]),
        w_ig=jnp.stack(sg["w_ig"]).reshape(w_ig.shape), b_ig=jnp.stack(sg["b_ig"]), lru_lambda=jnp.stack(sg["lru_lambda"]),
        sinks=jnp.stack(sg["sinks"])[:, :H], ln1_g=jnp.stack(sg["ln1_g"]), ln1_b=jnp.stack(sg["ln1_b"]),
        ln2_g=jnp.stack(sg["ln2_g"]), ln2_b=jnp.stack(sg["ln2_b"]),
    )

    order = ["w_in", "conv_w", "conv_b", "w_rg", "b_rg", "w_ig", "b_ig", "lru_lambda", "sinks", "w_branch", "w_out",
             "ln1_g", "ln1_b", "w_ffn_in", "w_ffn_out", "ln2_g", "ln2_b"]
    weights = dict(w_in=w_in, conv_w=conv_w, conv_b=conv_b, w_rg=w_rg, b_rg=b_rg, w_ig=w_ig, b_ig=b_ig, lru_lambda=lru_lambda,
                   sinks=sinks, w_branch=w_branch, w_out=w_out, ln1_g=ln1_g, ln1_b=ln1_b, w_ffn_in=w_ffn_in,
                   w_ffn_out=w_ffn_out, ln2_g=ln2_g, ln2_b=ln2_b)
    ms = dict(w_in=m_w_in, conv_w=m_conv_w, conv_b=m_conv_b, w_rg=m_w_rg, b_rg=m_b_rg, w_ig=m_w_ig, b_ig=m_b_ig,
              lru_lambda=m_lru_lambda, sinks=m_sinks, w_branch=m_w_branch, w_out=m_w_out, ln1_g=m_ln1_g, ln1_b=m_ln1_b,
              w_ffn_in=m_w_ffn_in, w_ffn_out=m_w_ffn_out, ln2_g=m_ln2_g, ln2_b=m_ln2_b)
    vs = dict(w_in=v_w_in, conv_w=v_conv_w, conv_b=v_conv_b, w_rg=v_w_rg, b_rg=v_b_rg, w_ig=v_w_ig, b_ig=v_b_ig,
              lru_lambda=v_lru_lambda, sinks=v_sinks, w_branch=v_w_branch, w_out=v_w_out, ln1_g=v_ln1_g, ln1_b=v_ln1_b,
              w_ffn_in=v_w_ffn_in, w_ffn_out=v_w_ffn_out, ln2_g=v_ln2_g, ln2_b=v_ln2_b)
    deltas, new_m, new_v = {}, {}, {}
    for nm in order:
        deltas[nm], new_m[nm], new_v[nm] = _adamw(weights[nm], grads[nm], ms[nm], vs[nm])
    return (loss, dx.reshape(B, S, D), *[grads[nm] for nm in order], *[deltas[nm] for nm in order],
            *[new_m[nm] for nm in order], *[new_v[nm] for nm in order])
```

```python
import math

import jax
import jax.numpy as jnp
from jax import lax
from jax.experimental import pallas as pl
from jax.experimental.pallas import tpu as pltpu

HEAD_DIM = 64
WIN = 128
DILS = (1, 4, 16)
SWA_GROUP = 4
CONV_WIDTH = 4
LRU_C = 8.0
LN_EPS = 1e-5
NEG_INF = -1e30
N_CHIPS = 4
ADAM_LR, ADAM_B1, ADAM_B2, ADAM_EPS, ADAM_WD, ADAM_STEP = 0.001, 0.9, 0.999, 1e-08, 0.01, 10

LANES = 128
SUBLANES = 8
VMEM_LIMIT = 48 * 1024 * 1024

F32 = jnp.float32
BF16 = jnp.bfloat16
MESH = pl.DeviceIdType.MESH
ANY = pl.BlockSpec(memory_space=pl.ANY)


def _pcall(body, **kw):
    return pl.pallas_call(body, **kw)


def _pcall_comm(body, **kw):
    return pl.pallas_call(body, **kw)


def _params(*sem):
    return pltpu.CompilerParams(dimension_semantics=tuple(sem), vmem_limit_bytes=VMEM_LIMIT)


def _tile(dim, target):
    if dim <= target:
        return dim
    best = None
    for t in range(LANES, target + 1, LANES):
        if dim % t == 0:
            best = t
    assert best is not None, (dim, target)
    return best


def _sigmoid(x):
    return 1.0 / (1.0 + jnp.exp(-x))


def _dot(a, b, dims):
    return lax.dot_general(a, b, (dims, ((), ())), preferred_element_type=F32)


def _dot_nn(a, b):
    return _dot(a, b, ((1,), (0,)))


def _dot_nt(a, b):
    return _dot(a, b, ((1,), (1,)))


def _dot_tn(a, b):
    return _dot(a, b, ((0,), (0,)))


def _matmul(a, b, *, mode, name, out_dtype=F32, tm=512, tn=512, tk=2048, resid=None, rs=1.0, out_shards=0, n_outer=False,
            ln=None):
    b_sh = b.ndim == 3
    if mode == "nn":
        M, K = a.shape
        N = b.shape[-1] * (b.shape[0] if b_sh else 1)
    elif mode == "nt":
        M, K = a.shape
        N = b.shape[-2]
    else:
        K, M = a.shape
        N = b.shape[-1]
    tm = _tile(M, tm)
    if mode == "nn" and b_sh:
        tn = b.shape[-1]
    elif out_shards:
        tn = N // out_shards
    else:
        tn = _tile(N, tn)
    if mode == "nt" and b_sh:
        tk = b.shape[-1]
    else:
        tk = _tile(K, tk)
    nk = K // tk
    grid = (N // tn, M // tm, nk) if n_outer else (M // tm, N // tn, nk)

    def spec(shape, f):
        return pl.BlockSpec(shape, (lambda g0, g1, k: f(g1, g0, k)) if n_outer else f)

    if mode == "nn":
        a_spec = spec((tm, tk), lambda i, j, k: (i, k))
        b_spec = spec((None, tk, tn), lambda i, j, k: (j, k, 0)) if b_sh else spec((tk, tn), lambda i, j, k: (k, j))
        contract = _dot_nn
    elif mode == "nt":
        a_spec = spec((tm, tk), lambda i, j, k: (i, k))
        b_spec = spec((None, tn, tk), lambda i, j, k: (k, j, 0)) if b_sh else spec((tn, tk), lambda i, j, k: (j, k))
        contract = _dot_nt
    else:
        a_spec = spec((tk, tm), lambda i, j, k: (k, i))
        b_spec = spec((tk, tn), lambda i, j, k: (k, j))
        contract = _dot_tn
    if out_shards:
        out_shape = jax.ShapeDtypeStruct((out_shards, M, tn), out_dtype)
        o_spec = spec((None, tm, tn), lambda i, j, k: (j, i, 0))
    else:
        out_shape = jax.ShapeDtypeStruct((M, N), out_dtype)
        o_spec = spec((tm, tn), lambda i, j, k: (i, j))
    in_specs = [a_spec, b_spec]
    args = [a, b]
    if resid is not None:
        in_specs.append(spec((tm, tn), lambda i, j, k: (i, j)))
        args.append(resid)
    if ln is not None:
        assert tn == N and resid is not None and not out_shards
        in_specs += [spec((1, N), lambda i, j, k: (0, 0))] * 2
        args += list(ln)
        out_shape = [out_shape, out_shape]
        o_spec = [o_spec, o_spec]
    n_in = len(args)

    def body(*refs):
        a_ref, b_ref = refs[:2]
        r_ref = refs[2] if resid is not None else None
        o_ref = refs[n_in]
        part = contract(a_ref[...].astype(BF16), b_ref[...].astype(BF16))

        def finish(res):
            if resid is not None:
                res = res + rs * r_ref[...]
            o_ref[...] = res.astype(out_dtype)
            if ln is not None:
                g_ref, bb_ref, y_ref = refs[n_in - 2], refs[n_in - 1], refs[n_in + 1]
                zc = res - jnp.mean(res, axis=1, keepdims=True)
                var = jnp.mean(zc * zc, axis=1, keepdims=True)
                y_ref[...] = zc * lax.rsqrt(var + LN_EPS) * g_ref[...] + bb_ref[...]

        if nk == 1:
            finish(part)
            return
        acc_ref = refs[-1]
        k = pl.program_id(2)

        @pl.when(k == 0)
        def _():
            acc_ref[...] = part

        @pl.when(jnp.logical_and(k > 0, k < nk - 1))
        def _():
            acc_ref[...] += part

        @pl.when(k == nk - 1)
        def _():
            finish(acc_ref[...] + part)

    return _pcall(
        body, name=name, grid=grid, in_specs=in_specs, out_specs=o_spec, out_shape=out_shape,
        scratch_shapes=[pltpu.VMEM((tm, tn), F32)] if nk > 1 else [],
        compiler_params=_params("parallel", "parallel", "arbitrary"),
    )(*args)


def _shift_down(x, d, row):
    return jnp.where(row >= d, pltpu.roll(x, d, 0), 0.0)


def _shift_up(x, d, row, n):
    return jnp.where(row < n - d, pltpu.roll(x, n - d, 0), 0.0)


def _log1p(u):
    w = 1.0 + u
    return jnp.where(w == 1.0, u, jnp.log(w) * u / (w - 1.0))


def _gelu_parts(g):
    k = math.sqrt(2.0 / math.pi)
    c = 0.044715
    t = jnp.tanh(k * (g + c * g * g * g))
    val = 0.5 * g * (1.0 + t)
    der = 0.5 * (1.0 + t) + 0.5 * g * (1.0 - t * t) * k * (1.0 + 3.0 * c * g * g)
    return val, der


def _lru_gates(xr, cw_ref, cb_ref, wrg_ref, brg_ref, wig_ref, big_ref, lam_ref, row):
    xc = cw_ref[3:4, :] * xr + cb_ref[...]
    for d in range(1, CONV_WIDTH):
        xc = xc + cw_ref[3 - d:4 - d, :] * _shift_down(xr, d, row)
    xcb = xc.astype(BF16)
    r = _sigmoid(_dot_nn(xcb, wrg_ref[...]) + brg_ref[...])
    ig = _sigmoid(_dot_nn(xcb, wig_ref[...]) + big_ref[...])
    lam = lam_ref[...]
    sp = jnp.maximum(-lam, 0.0) + _log1p(jnp.exp(-jnp.abs(lam)))
    log_a = (-LRU_C) * r * sp
    a = jnp.exp(log_a)
    y2 = 2.0 * log_a
    one_m_a2 = jnp.where(y2 > -0.01, -(y2 + 0.5 * y2 * y2 + (1.0 / 6.0) * y2 * y2 * y2), 1.0 - jnp.exp(y2))
    mult = jnp.sqrt(one_m_a2)
    return xc, r, ig, sp, a, mult


def _scan_local(a, b, row, n, reverse):
    sub = row % SUBLANES
    d = 1
    while d < SUBLANES:
        if reverse:
            keep = sub < SUBLANES - d
            a_s = jnp.where(keep, pltpu.roll(a, n - d, 0), 1.0)
            b_s = jnp.where(keep, pltpu.roll(b, n - d, 0), 0.0)
        else:
            keep = sub >= d
            a_s = jnp.where(keep, pltpu.roll(a, d, 0), 1.0)
            b_s = jnp.where(keep, pltpu.roll(b, d, 0), 0.0)
        b = a * b_s + b
        a = a * a_s
        d *= 2
    return a, b


def _scan_carry(a_ref, b_ref, out_ref, n, reverse):
    ng = n // SUBLANES

    def step(gidx, carry):
        g = (ng - 1 - gidx) if reverse else gidx
        rows = pl.ds(pl.multiple_of(g * SUBLANES, SUBLANES), SUBLANES)
        h = a_ref[rows, :] * carry + b_ref[rows, :]
        out_ref[rows, :] = h
        return h[0:1, :] if reverse else h[SUBLANES - 1:SUBLANES, :]

    lax.fori_loop(0, ng, step, jnp.zeros((1, LANES), F32), unroll=8)


def _lru_specs(B, S, D, C, x_off, g_off):
    nct = D // LANES
    seq = lambda off: pl.BlockSpec((None, S, LANES), lambda ct, b: (b, 0, off // LANES + ct))
    row = lambda r: pl.BlockSpec((r, LANES), lambda ct, b: (0, ct))
    wbd = pl.BlockSpec((None, LANES, LANES), lambda ct, b: (ct, 0, 0))
    return nct, seq, row, wbd


def _lru_fwd(proj3, lp, *, D, x_off, g_off):
    B, S, C = proj3.shape
    nct, seq, row, wbd = _lru_specs(B, S, D, C, x_off, g_off)

    def body(xr_ref, g_ref, cw_ref, cb_ref, wrg_ref, brg_ref, wig_ref, big_ref, lam_ref, h_ref, ya_ref, a_s, b_s):
        rowi = lax.broadcasted_iota(jnp.int32, (S, LANES), 0)
        xr = xr_ref[...]
        xc, r, ig, sp, a, mult = _lru_gates(xr, cw_ref, cb_ref, wrg_ref, brg_ref, wig_ref, big_ref, lam_ref, rowi)
        al, bl = _scan_local(a, mult * (ig * xc), rowi, S, False)
        a_s[...] = al
        b_s[...] = bl
        _scan_carry(a_s, b_s, h_ref, S, False)
        gel, _ = _gelu_parts(g_ref[...])
        ya_ref[...] = (h_ref[...] * gel).astype(BF16)

    out_seq = pl.BlockSpec((None, S, LANES), lambda ct, b: (b, 0, ct))
    return _pcall(
        body, name="lru_fwd", grid=(nct, B),
        in_specs=[seq(x_off), seq(g_off), row(CONV_WIDTH), row(1), wbd, row(1), wbd, row(1), row(1)],
        out_specs=[out_seq, out_seq],
        out_shape=[jax.ShapeDtypeStruct((B, S, D), F32), jax.ShapeDtypeStruct((B, S, D), BF16)],
        scratch_shapes=[pltpu.VMEM((S, LANES), F32), pltpu.VMEM((S, LANES), F32)],
        compiler_params=_params("parallel", "parallel"),
    )(proj3, proj3, lp["conv_w"], lp["conv_b"], lp["w_rg_bd"], lp["b_rg"], lp["w_ig_bd"], lp["b_ig"], lp["lam"])


def _lru_bwd(proj3, h3, dya3, lp, *, D, x_off, g_off):
    B, S, C = proj3.shape
    nct, seq, row, wbd = _lru_specs(B, S, D, C, x_off, g_off)

    def body(xr_ref, g_ref, h_ref, dy_ref, cw_ref, cb_ref, wrg_ref, brg_ref, wig_ref, big_ref, lam_ref,
             dxr_ref, dg_ref, dcw_ref, dcb_ref, dwrg_ref, dbrg_ref, dwig_ref, dbig_ref, dlam_ref, a_s, b_s, l_s):
        first = pl.program_id(1) == 0
        rowi = lax.broadcasted_iota(jnp.int32, (S, LANES), 0)
        xr = xr_ref[...]
        xc, r, ig, sp, a, mult = _lru_gates(xr, cw_ref, cb_ref, wrg_ref, brg_ref, wig_ref, big_ref, lam_ref, rowi)
        h = h_ref[...]
        dy = dy_ref[...]
        gel, dgel = _gelu_parts(g_ref[...])
        dg_ref[...] = (dy * h * dgel).astype(BF16)
        al, bl = _scan_local(_shift_up(a, 1, rowi, S), dy * gel, rowi, S, True)
        a_s[...] = al
        b_s[...] = bl
        _scan_carry(a_s, b_s, l_s, S, True)
        lamb = l_s[...]
        u = ig * xc
        da = lamb * _shift_down(h, 1, rowi)
        dlog_a = da * a - (lamb * u) * (a * a) / mult
        du = lamb * mult
        dpre_r = (dlog_a * ((-LRU_C) * sp)) * r * (1.0 - r)
        dpre_i = (du * xc) * ig * (1.0 - ig)
        dsp = jnp.sum(dlog_a * ((-LRU_C) * r), axis=0, keepdims=True)
        dlam = dsp * (-1.0 / (1.0 + jnp.exp(lam_ref[...])))
        dpr = dpre_r.astype(BF16)
        dpi = dpre_i.astype(BF16)
        dxc = du * ig + _dot_nt(dpr, wrg_ref[...]) + _dot_nt(dpi, wig_ref[...])
        xcb = xc.astype(BF16)
        dwrg = _dot_tn(xcb, dpr)
        dwig = _dot_tn(xcb, dpi)
        dxr = cw_ref[3:4, :] * dxc
        dcw = [jnp.sum(xr * dxc, axis=0, keepdims=True)]
        for d in range(1, CONV_WIDTH):
            dxr = dxr + cw_ref[3 - d:4 - d, :] * _shift_up(dxc, d, rowi, S)
            dcw.append(jnp.sum(_shift_down(xr, d, rowi) * dxc, axis=0, keepdims=True))
        dxr_ref[...] = dxr.astype(BF16)
        dcw_rows = jnp.concatenate(dcw[::-1], axis=0)
        sums = ((dcw_ref, dcw_rows), (dcb_ref, jnp.sum(dxc, axis=0, keepdims=True)), (dwrg_ref, dwrg),
                (dbrg_ref, jnp.sum(dpre_r, axis=0, keepdims=True)), (dwig_ref, dwig),
                (dbig_ref, jnp.sum(dpre_i, axis=0, keepdims=True)), (dlam_ref, dlam))

        @pl.when(first)
        def _():
            for ref, val in sums:
                ref[...] = val

        @pl.when(jnp.logical_not(first))
        def _():
            for ref, val in sums:
                ref[...] += val

    out_seq = pl.BlockSpec((None, S, LANES), lambda ct, b: (b, 0, ct))
    f = lambda shape: jax.ShapeDtypeStruct(shape, F32)
    nb = D // LANES
    return _pcall(
        body, name="lru_bwd", grid=(nct, B),
        in_specs=[seq(x_off), seq(g_off), out_seq, out_seq, row(CONV_WIDTH), row(1), wbd, row(1), wbd, row(1), row(1)],
        out_specs=[out_seq, out_seq, row(CONV_WIDTH), row(1), wbd, row(1), wbd, row(1), row(1)],
        out_shape=[jax.ShapeDtypeStruct((B, S, D), BF16), jax.ShapeDtypeStruct((B, S, D), BF16),
                   f((CONV_WIDTH, D)), f((1, D)), f((nb, LANES, LANES)), f((1, D)), f((nb, LANES, LANES)), f((1, D)), f((1, D))],
        scratch_shapes=[pltpu.VMEM((S, LANES), F32)] * 3,
        compiler_params=_params("parallel", "arbitrary"),
    )(proj3, proj3, h3, dya3, lp["conv_w"], lp["conv_b"], lp["w_rg_bd"], lp["b_rg"], lp["w_ig_bd"], lp["b_ig"], lp["lam"])


def _pair_stack(x, lo):
    z = jnp.zeros_like(x)
    return jnp.concatenate([jnp.where(lo, x, z), jnp.where(lo, z, x)], axis=0).astype(BF16)


def _pair_join(y2, lo):
    return jnp.where(lo, y2[:WIN], y2[WIN:])


def _pair_col(xb):
    return jnp.concatenate([xb[:, 0:1], xb[:, HEAD_DIM:HEAD_DIM + 1]], axis=0)


def _pair_bcast(col, lo):
    return jnp.where(lo, jnp.broadcast_to(col[:WIN], (WIN, LANES)), jnp.broadcast_to(col[WIN:], (WIN, LANES)))


def _dil_rows(it, d, S):
    if d == 1:
        cur = pl.multiple_of(it * WIN, WIN)
        prev = pl.multiple_of(jnp.maximum(it - 1, 0) * WIN, WIN)
        return pl.ds(cur, WIN), pl.ds(prev, WIN), it > 0
    r, i = it % d, it // d
    cur = i * (WIN * d) + r
    prev = jnp.maximum(i - 1, 0) * (WIN * d) + r
    return pl.ds(cur, WIN, stride=d), pl.ds(prev, WIN, stride=d), i > 0


def _dil_bias(two_blocks, stack=2):
    nk = 2 * WIN if two_blocks else WIN
    qi = lax.broadcasted_iota(jnp.int32, (stack * WIN, nk), 0) & (WIN - 1)
    kj = lax.broadcasted_iota(jnp.int32, (stack * WIN, nk), 1)
    if not two_blocks:
        return jnp.where(kj <= qi, 0.0, NEG_INF), None
    cur = jnp.logical_and(kj >= WIN, kj - WIN <= qi)
    prev = jnp.logical_and(kj < WIN, kj >= qi)
    return jnp.where(jnp.logical_or(cur, prev), 0.0, NEG_INF), jnp.where(cur, 0.0, NEG_INF)


def _dil_specs(B, S, D, C, offs):
    grid = (B, D // LANES)
    seq = lambda off: pl.BlockSpec((None, S, LANES), lambda b, p: (b, 0, off // LANES + p))
    return grid, [seq(o) for o in offs], seq(0)


def _call_with_rider(rider, body, *, name, grid, in_specs, out_specs, out_shape, scratch_shapes, args, semantics=None):
    if rider is None:
        return _pcall(body, name=name, grid=grid, in_specs=in_specs, out_specs=out_specs, out_shape=out_shape,
                      scratch_shapes=scratch_shapes, compiler_params=_params(*(semantics or ("parallel",) * len(grid))))(*args)
    n_in, n_out, n_sc = len(in_specs), len(out_specs), len(scratch_shapes)
    r_in, r_out = len(rider["ins"]), len(rider["out_shapes"])

    def wrapped(*refs):
        p = 0
        own_in = refs[p:p + n_in]; p += n_in
        rid_in = refs[p:p + r_in]; p += r_in
        own_out = refs[p:p + n_out]; p += n_out
        rid_out = refs[p:p + r_out]; p += r_out
        own_sc = refs[p:p + n_sc]; p += n_sc
        send_sems, recv_sems = refs[p:p + 2]
        ids = [pl.program_id(a) for a in range(len(grid))]
        first = ids[0] == 0
        last = ids[0] == grid[0] - 1
        for a in range(1, len(grid)):
            first = jnp.logical_and(first, ids[a] == 0)
            last = jnp.logical_and(last, ids[a] == grid[a] - 1)

        @pl.when(first)
        def _():
            rider["start"](rid_in, rid_out, send_sems, recv_sems)

        body(*own_in, *own_out, *own_sc)

        @pl.when(last)
        def _():
            rider["finish"](rid_in, rid_out, send_sems, recv_sems)

    res = _pcall_comm(
        wrapped, name=name + "_" + rider["name"], grid=grid, in_specs=list(in_specs) + [ANY] * r_in,
        out_specs=list(out_specs) + [ANY] * r_out, out_shape=list(out_shape) + list(rider["out_shapes"]),
        scratch_shapes=list(scratch_shapes) + [pltpu.SemaphoreType.DMA((rider["n"],)), pltpu.SemaphoreType.DMA((rider["n"],))],
        compiler_params=_params(*(("arbitrary",) * len(grid))),
    )(*args, *rider["ins"])
    return res


def _dil_fwd(proj3, *, D, q_off, k_off, v_off, rider=None):
    B, S, C = proj3.shape
    n_it = S // WIN
    scale = HEAD_DIM ** -0.5
    grid, in_specs, out_spec = _dil_specs(B, S, D, C, (q_off, k_off, v_off))

    def body(q_ref, k_ref, v_ref, o_ref, l_ref):
        lo = lax.broadcasted_iota(jnp.int32, (WIN, LANES), 1) < HEAD_DIM
        for c, d in enumerate(DILS):
            two = S // d > WIN
            bias_all, bias_first = _dil_bias(two)

            def step(it, _, c=c, d=d, two=two, bias_all=bias_all, bias_first=bias_first):
                cur, prev, later = _dil_rows(it, d, S)
                q2 = _pair_stack(q_ref[cur, :], lo)
                if two:
                    k2 = jnp.concatenate([k_ref[prev, :], k_ref[cur, :]], axis=0).astype(BF16)
                    v2 = jnp.concatenate([v_ref[prev, :], v_ref[cur, :]], axis=0).astype(BF16)
                    bias = jnp.where(later, bias_all, bias_first)
                else:
                    k2, v2, bias = k_ref[cur, :].astype(BF16), v_ref[cur, :].astype(BF16), bias_all
                s2 = _dot_nt(q2, k2) * scale + bias
                m2 = jnp.max(s2, axis=1, keepdims=True)
                p2 = jnp.exp(s2 - m2)
                den = jnp.sum(p2, axis=1, keepdims=True)
                oc = _pair_join(_dot_nn(p2.astype(BF16), v2) / den, lo)
                lc = _pair_bcast(m2 + jnp.log(den), lo)
                if c == 0:
                    o_ref[cur, :] = oc
                    l_ref[cur, :] = lc
                else:
                    l_old = l_ref[cur, :]
                    mx = jnp.maximum(l_old, lc)
                    e_old, e_new = jnp.exp(l_old - mx), jnp.exp(lc - mx)
                    tot = e_old + e_new
                    o_ref[cur, :] = (e_old * o_ref[cur, :] + e_new * oc) / tot
                    l_ref[cur, :] = mx + jnp.log(tot)
                return 0

            lax.fori_loop(0, n_it, step, 0, unroll=4)

    return _call_with_rider(
        rider, body, name="dil_fwd", grid=grid, in_specs=in_specs, out_specs=[out_spec, out_spec],
        out_shape=[jax.ShapeDtypeStruct((B, S, D), F32)] * 2, scratch_shapes=[], args=[proj3, proj3, proj3])


def _dil_bwd(proj3, o3, l3, do3, *, D, q_off, k_off, v_off, rider=None):
    B, S, C = proj3.shape
    n_it = S // WIN
    scale = HEAD_DIM ** -0.5
    grid, in_specs, out_spec = _dil_specs(B, S, D, C, (q_off, k_off, v_off))

    def body(q_ref, k_ref, v_ref, o_ref, l_ref, do_ref, dq_ref, dk_ref, dv_ref, dd_s, dq_s, dk_s, dv_s):
        lo = lax.broadcasted_iota(jnp.int32, (WIN, LANES), 1) < HEAD_DIM
        lo_s = lax.broadcasted_iota(jnp.int32, (S, LANES), 1) < HEAD_DIM
        prod = do_ref[...] * o_ref[...]
        d_lo = jnp.sum(jnp.where(lo_s, prod, 0.0), axis=1, keepdims=True)
        d_hi = jnp.sum(jnp.where(lo_s, 0.0, prod), axis=1, keepdims=True)
        dd_s[...] = jnp.where(lo_s, jnp.broadcast_to(d_lo, (S, LANES)), jnp.broadcast_to(d_hi, (S, LANES)))
        dq_s[...] = jnp.zeros_like(dq_s)
        dk_s[...] = jnp.zeros_like(dk_s)
        dv_s[...] = jnp.zeros_like(dv_s)
        for d in DILS:
            two = S // d > WIN
            bias_all, bias_first = _dil_bias(two)

            def step(it, _, d=d, two=two, bias_all=bias_all, bias_first=bias_first):
                cur, prev, later = _dil_rows(it, d, S)
                q2 = _pair_stack(q_ref[cur, :], lo)
                do2 = _pair_stack(do_ref[cur, :], lo)
                l2 = _pair_col(l_ref[cur, :])
                dd2 = _pair_col(dd_s[cur, :])
                if two:
                    k2 = jnp.concatenate([k_ref[prev, :], k_ref[cur, :]], axis=0).astype(BF16)
                    v2 = jnp.concatenate([v_ref[prev, :], v_ref[cur, :]], axis=0).astype(BF16)
                    bias = jnp.where(later, bias_all, bias_first)
                else:
                    k2, v2, bias = k_ref[cur, :].astype(BF16), v_ref[cur, :].astype(BF16), bias_all
                p2 = jnp.exp(_dot_nt(q2, k2) * scale + bias - l2)
                ds2 = (p2 * (_dot_nt(do2, v2) - dd2) * scale).astype(BF16)
                dq_s[cur, :] += _pair_join(_dot_nn(ds2, k2), lo)
                dk2 = _dot_tn(ds2, q2)
                dv2 = _dot_tn(p2.astype(BF16), do2)
                if two:
                    dk_s[prev, :] += dk2[:WIN]
                    dv_s[prev, :] += dv2[:WIN]
                    dk_s[cur, :] += dk2[WIN:]
                    dv_s[cur, :] += dv2[WIN:]
                else:
                    dk_s[cur, :] += dk2
                    dv_s[cur, :] += dv2
                return 0

            lax.fori_loop(0, n_it, step, 0, unroll=4)
        dq_ref[...] = dq_s[...].astype(BF16)
        dk_ref[...] = dk_s[...].astype(BF16)
        dv_ref[...] = dv_s[...].astype(BF16)

    return _call_with_rider(
        rider, body, name="dil_bwd", grid=grid, in_specs=in_specs + [out_spec] * 3, out_specs=[out_spec] * 3,
        out_shape=[jax.ShapeDtypeStruct((B, S, D), BF16)] * 3, scratch_shapes=[pltpu.VMEM((S, LANES), F32)] * 4,
        args=[proj3, proj3, proj3, o3, l3, do3])


SWA_HB = 2 * SWA_GROUP


def _to_half(x, src, dst, lo):
    if src != dst:
        x = pltpu.roll(x, HEAD_DIM, 1)
    return jnp.where(lo if dst == 0 else jnp.logical_not(lo), x, 0.0)


def _swa_specs(B, S, D, C, q_off, k_off, v_off, n_steps):
    qw = SWA_HB * HEAD_DIM
    last = S // WIN - 1

    def blk(width, off, back):
        def imap(b, hh, i):
            ii = jnp.minimum(i, last)
            if back == 1:
                ii = jnp.maximum(ii - 1, 0)
            elif back == 2:
                ii = jnp.maximum(i - 1, 0)
            return (b, ii, off // width + hh)
        return pl.BlockSpec((None, WIN, width), imap)

    assert q_off % qw == 0 and k_off % LANES == 0 and v_off % LANES == 0 and D % qw == 0
    grid = (B, D // qw, n_steps)
    sink = pl.BlockSpec((None, 1, LANES), lambda b, hh, i: (hh, 0, 0))
    return grid, blk, sink, qw


def _swa_stack(ref, j, lo, dtype):
    parts = []
    for h in range(j * SWA_GROUP, (j + 1) * SWA_GROUP):
        g = h // 2
        parts.append(_to_half(ref[:, g * LANES:(g + 1) * LANES], h % 2, j, lo))
    return jnp.concatenate(parts, axis=0).astype(dtype)


def _swa_unstack(y4, j, lo):
    out = []
    for t in range(0, SWA_GROUP, 2):
        even = _to_half(y4[t * WIN:(t + 1) * WIN], j, 0, lo)
        odd = _to_half(y4[(t + 1) * WIN:(t + 2) * WIN], j, 1, lo)
        out.append(even + odd)
    return out


def _swa_cols(ref, j):
    cols = [jnp.broadcast_to(ref[:, h:h + 1], (WIN, 1)) for h in range(j * SWA_GROUP, (j + 1) * SWA_GROUP)]
    return jnp.concatenate(cols, axis=0)


def _swa_fwd(proj3, sinks, *, D, q_off, k_off, v_off, rider=None):
    B, S, C = proj3.shape
    scale = HEAD_DIM ** -0.5
    grid, blk, sink, qw = _swa_specs(B, S, D, C, q_off, k_off, v_off, S // WIN)
    nhb = D // qw

    def body(q_ref, kp_ref, kc_ref, vp_ref, vc_ref, sk_ref, o_ref, lse_ref):
        lo = lax.broadcasted_iota(jnp.int32, (WIN, LANES), 1) < HEAD_DIM
        lane = lax.broadcasted_iota(jnp.int32, (WIN, LANES), 1)
        bias_all, bias_first = _dil_bias(True, SWA_GROUP)
        bias = jnp.where(pl.program_id(2) > 0, bias_all, bias_first)
        k2 = jnp.concatenate([kp_ref[...], kc_ref[...]], axis=0).astype(BF16)
        v2 = jnp.concatenate([vp_ref[...], vc_ref[...]], axis=0).astype(BF16)
        lse_acc = jnp.zeros((WIN, LANES), F32)
        for j in range(2):
            q4 = _swa_stack(q_ref, j, lo, BF16)
            sk4 = _swa_cols(sk_ref, j)
            s4 = _dot_nt(q4, k2) * scale + bias
            m = jnp.maximum(jnp.max(s4, axis=1, keepdims=True), sk4)
            p = jnp.exp(s4 - m)
            den = jnp.sum(p, axis=1, keepdims=True) + jnp.exp(sk4 - m)
            o4 = _dot_nn(p.astype(BF16), v2) / den
            l4 = m + jnp.log(den)
            for t, grp in enumerate(_swa_unstack(o4, j, lo)):
                g = j * (SWA_GROUP // 2) + t
                o_ref[:, g * LANES:(g + 1) * LANES] = grp
            for t in range(SWA_GROUP):
                lse_acc = jnp.where(lane == j * SWA_GROUP + t, l4[t * WIN:(t + 1) * WIN], lse_acc)
        lse_ref[...] = lse_acc

    return _call_with_rider(
        rider, body, name="swa_fwd", grid=grid,
        in_specs=[blk(qw, q_off, 0), blk(LANES, k_off, 1), blk(LANES, k_off, 0), blk(LANES, v_off, 1), blk(LANES, v_off, 0), sink],
        out_specs=[blk(qw, 0, 0), blk(LANES, 0, 0)],
        out_shape=[jax.ShapeDtypeStruct((B, S, D), F32), jax.ShapeDtypeStruct((B, S, nhb * LANES), F32)],
        scratch_shapes=[], args=[proj3, proj3, proj3, proj3, proj3, sinks])


def _swa_bwd(proj3, o3, lse3, do3, sinks, *, D, q_off, k_off, v_off):
    B, S, C = proj3.shape
    nblk = S // WIN
    scale = HEAD_DIM ** -0.5
    grid, blk, sink, qw = _swa_specs(B, S, D, C, q_off, k_off, v_off, nblk + 1)
    nhb = D // qw
    KV = D // SWA_GROUP

    def body(q_ref, kp_ref, kc_ref, vp_ref, vc_ref, o_ref, l_ref, do_ref, sk_ref, dq_ref, dk_ref, dv_ref, dsk_ref, ck_ref, cv_ref):
        i = pl.program_id(2)

        @pl.when(i == 0)
        def _():
            ck_ref[...] = jnp.zeros_like(ck_ref)
            cv_ref[...] = jnp.zeros_like(cv_ref)
            dsk_ref[...] = jnp.zeros_like(dsk_ref)

        @pl.when(i < nblk)
        def _():
            lo = lax.broadcasted_iota(jnp.int32, (WIN, LANES), 1) < HEAD_DIM
            lane = lax.broadcasted_iota(jnp.int32, (1, LANES), 1)
            bias_all, bias_first = _dil_bias(True, SWA_GROUP)
            bias = jnp.where(i > 0, bias_all, bias_first)
            k2 = jnp.concatenate([kp_ref[...], kc_ref[...]], axis=0).astype(BF16)
            v2 = jnp.concatenate([vp_ref[...], vc_ref[...]], axis=0).astype(BF16)
            lane_w = lax.broadcasted_iota(jnp.int32, (WIN, LANES), 1)
            dd = jnp.zeros((WIN, LANES), F32)
            for g in range(qw // LANES):
                prod = do_ref[:, g * LANES:(g + 1) * LANES] * o_ref[:, g * LANES:(g + 1) * LANES]
                dd = jnp.where(lane_w == 2 * g, jnp.sum(jnp.where(lo, prod, 0.0), axis=1, keepdims=True), dd)
                dd = jnp.where(lane_w == 2 * g + 1, jnp.sum(jnp.where(lo, 0.0, prod), axis=1, keepdims=True), dd)
            dk2 = jnp.zeros((2 * WIN, LANES), F32)
            dv2 = jnp.zeros((2 * WIN, LANES), F32)
            dsk_acc = jnp.zeros((1, LANES), F32)
            for j in range(2):
                q4 = _swa_stack(q_ref, j, lo, BF16)
                do4 = _swa_stack(do_ref, j, lo, BF16)
                heads = range(j * SWA_GROUP, (j + 1) * SWA_GROUP)
                l4 = jnp.concatenate([l_ref[:, h:h + 1] for h in heads], axis=0)
                dd4 = jnp.concatenate([dd[:, h:h + 1] for h in heads], axis=0)
                p4 = jnp.exp(_dot_nt(q4, k2) * scale + bias - l4)
                ds4 = (p4 * (_dot_nt(do4, v2) - dd4) * scale).astype(BF16)
                for t, grp in enumerate(_swa_unstack(_dot_nn(ds4, k2), j, lo)):
                    g = j * (SWA_GROUP // 2) + t
                    dq_ref[:, g * LANES:(g + 1) * LANES] = grp.astype(BF16)
                dk2 = dk2 + _dot_tn(ds4, q4)
                dv2 = dv2 + _dot_tn(p4.astype(BF16), do4)
                for h in heads:
                    dsk_h = -jnp.sum(jnp.exp(sk_ref[:, h:h + 1] - l_ref[:, h:h + 1]) * dd[:, h:h + 1], axis=0, keepdims=True)
                    dsk_acc = jnp.where(lane == h, dsk_h, dsk_acc)
            dk_ref[...] = (dk2[:WIN] + ck_ref[...]).astype(BF16)
            dv_ref[...] = (dv2[:WIN] + cv_ref[...]).astype(BF16)
            ck_ref[...] = dk2[WIN:]
            cv_ref[...] = dv2[WIN:]
            dsk_ref[...] += dsk_acc

        @pl.when(i == nblk)
        def _():
            dk_ref[...] = ck_ref[...].astype(BF16)
            dv_ref[...] = cv_ref[...].astype(BF16)

    res = _pcall(
        body, name="swa_bwd", grid=grid,
        in_specs=[blk(qw, q_off, 0), blk(LANES, k_off, 1), blk(LANES, k_off, 0), blk(LANES, v_off, 1), blk(LANES, v_off, 0),
                  blk(qw, 0, 0), blk(LANES, 0, 0), blk(qw, 0, 0), sink],
        out_specs=[blk(qw, 0, 0), blk(LANES, 0, 2), blk(LANES, 0, 2),
                   pl.BlockSpec((None, None, 1, LANES), lambda b, hh, i: (b, hh, 0, 0))],
        out_shape=[jax.ShapeDtypeStruct((B, S, D), BF16), jax.ShapeDtypeStruct((B, S, KV), BF16),
                   jax.ShapeDtypeStruct((B, S, KV), BF16), jax.ShapeDtypeStruct((B, nhb, 1, LANES), F32)],
        scratch_shapes=[pltpu.VMEM((WIN, LANES), F32), pltpu.VMEM((WIN, LANES), F32)],
        compiler_params=_params("parallel", "parallel", "arbitrary"),
    )(proj3, proj3, proj3, proj3, proj3, o3, lse3, do3, sinks)
    return res


def _branch_fwd(ys, wb, proj, *, D, g_off):
    T = proj.shape[0]
    tm, tn = _tile(T, 256), _tile(D, 512)
    n = len(ys)

    def body(*refs):
        y_refs, w_ref, g_refs, br_ref, mg_ref = refs[:n], refs[n], refs[n + 1:2 * n + 1], refs[2 * n + 1], refs[2 * n + 2]
        acc = None
        for k in range(n):
            br = _dot_nn(y_refs[k][...].astype(BF16), w_ref[k])
            br_ref[k] = br
            term = _sigmoid(g_refs[k][...]) * br
            acc = term if acc is None else acc + term
        mg_ref[...] = acc.astype(BF16)

    gate = lambda k: pl.BlockSpec((tm, tn), lambda i, j: (i, (g_off + k * D) // tn + j))
    return _pcall(
        body, name="branch_fwd", grid=(T // tm, D // tn),
        in_specs=[pl.BlockSpec((tm, D), lambda i, j: (i, 0))] * n + [pl.BlockSpec((n, D, tn), lambda i, j: (0, 0, j))]
        + [gate(k) for k in range(n)],
        out_specs=[pl.BlockSpec((n, tm, tn), lambda i, j: (0, i, j)), pl.BlockSpec((tm, tn), lambda i, j: (i, j))],
        out_shape=[jax.ShapeDtypeStruct((n, T, D), F32), jax.ShapeDtypeStruct((T, D), BF16)],
        compiler_params=_params("parallel", "parallel"),
    )(*ys, wb, *([proj] * n))


def _branch_bwd(dmerged, branch, proj, *, D, g_off):
    n, T, _ = branch.shape
    tm, tn = _tile(T, 512), _tile(D, 512)

    def body(dm_ref, br_ref, *rest):
        g_refs, db_ref, dg_refs = rest[:n], rest[n], rest[n + 1:]
        dm = dm_ref[...]
        for k in range(n):
            sg = _sigmoid(g_refs[k][...])
            db_ref[k] = (sg * dm).astype(BF16)
            dg_refs[k][...] = (dm * br_ref[k] * sg * (1.0 - sg)).astype(BF16)

    gate = lambda k: pl.BlockSpec((tm, tn), lambda i, j: (i, (g_off + k * D) // tn + j))
    blk = pl.BlockSpec((tm, tn), lambda i, j: (i, j))
    res = _pcall(
        body, name="branch_bwd", grid=(T // tm, D // tn),
        in_specs=[blk, pl.BlockSpec((n, tm, tn), lambda i, j: (0, i, j))] + [gate(k) for k in range(n)],
        out_specs=[pl.BlockSpec((n, tm, tn), lambda i, j: (0, i, j))] + [blk] * n,
        out_shape=[jax.ShapeDtypeStruct((n, T, D), BF16)] + [jax.ShapeDtypeStruct((T, D), BF16)] * n,
        compiler_params=_params("parallel", "parallel"),
    )(dmerged, branch, *([proj] * n))
    return res[0], list(res[1:])


def _ln_bwd(dout, z, g):
    T, D = z.shape
    tm = _tile(T, 512)

    def body(do_ref, z_ref, g_ref, dz_ref, dg_ref, db_ref):
        z = z_ref[...]
        do = do_ref[...]
        mu = jnp.mean(z, axis=1, keepdims=True)
        zc = z - mu
        rstd = lax.rsqrt(jnp.mean(zc * zc, axis=1, keepdims=True) + LN_EPS)
        xhat = zc * rstd
        dxh = do * g_ref[...]
        dz_ref[...] = rstd * (dxh - jnp.mean(dxh, axis=1, keepdims=True) - xhat * jnp.mean(dxh * xhat, axis=1, keepdims=True))
        dg = jnp.sum(do * xhat, axis=0, keepdims=True)
        db = jnp.sum(do, axis=0, keepdims=True)
        first = pl.program_id(0) == 0

        @pl.when(first)
        def _():
            dg_ref[...] = dg
            db_ref[...] = db

        @pl.when(jnp.logical_not(first))
        def _():
            dg_ref[...] += dg
            db_ref[...] += db

    blk = pl.BlockSpec((tm, D), lambda i: (i, 0))
    vec = pl.BlockSpec((1, D), lambda i: (0, 0))
    return _pcall(
        body, name="ln_bwd", grid=(T // tm,), in_specs=[blk, blk, vec], out_specs=[blk, vec, vec],
        out_shape=[jax.ShapeDtypeStruct((T, D), F32), jax.ShapeDtypeStruct((1, D), F32), jax.ShapeDtypeStruct((1, D), F32)],
        compiler_params=_params("arbitrary"),
    )(dout, z, g)


def _swiglu_fwd(hh):
    T, F2 = hh.shape
    Fh = F2 // 2
    tm, tn = _tile(T, 256), _tile(Fh, 1408)
    nj = Fh // tn

    def body(h1_ref, h3_ref, f_ref):
        h1 = h1_ref[...]
        f_ref[...] = (h1 * _sigmoid(h1) * h3_ref[...]).astype(BF16)

    return _pcall(
        body, name="swiglu_fwd", grid=(T // tm, nj),
        in_specs=[pl.BlockSpec((tm, tn), lambda i, j: (i, j)), pl.BlockSpec((tm, tn), lambda i, j: (i, nj + j))],
        out_specs=pl.BlockSpec((tm, tn), lambda i, j: (i, j)),
        out_shape=jax.ShapeDtypeStruct((T, Fh), BF16), compiler_params=_params("parallel", "parallel"),
    )(hh, hh)


def _swiglu_bwd(hh, df, rider=None):
    T, F2 = hh.shape
    Fh = F2 // 2
    tm, tn = _tile(T, 256), _tile(Fh, 1408)
    nj = Fh // tn

    def body(h1_ref, h3_ref, df_ref, d1_ref, d3_ref):
        h1 = h1_ref[...]
        sg = _sigmoid(h1)
        d = df_ref[...]
        d1_ref[...] = (d * h3_ref[...] * sg * (1.0 + h1 * (1.0 - sg))).astype(BF16)
        d3_ref[...] = (d * h1 * sg).astype(BF16)

    lo = pl.BlockSpec((tm, tn), lambda i, j: (i, j))
    hi = pl.BlockSpec((tm, tn), lambda i, j: (i, nj + j))
    res = _call_with_rider(
        rider, body, name="swiglu_bwd", grid=(T // tm, nj), in_specs=[lo, hi, lo], out_specs=[lo, lo],
        out_shape=[jax.ShapeDtypeStruct((T, Fh), BF16)] * 2, scratch_shapes=[], args=[hh, hh, df])
    return jnp.concatenate([res[0], res[1]], axis=1), list(res[2:])


def _loss_head(y, target):
    T, D = y.shape
    tm = _tile(T, 512)

    def body(y_ref, t_ref, dy_ref, l_ref):
        e = y_ref[...] - t_ref[...]
        dy_ref[...] = e * (1.0 / D)
        sq = e * e
        part = sq[:, 0:LANES]
        for c in range(1, D // LANES):
            part = part + sq[:, c * LANES:(c + 1) * LANES]
        part = jnp.sum(part, axis=0, keepdims=True) * (0.5 / D)
        first = pl.program_id(0) == 0

        @pl.when(first)
        def _():
            l_ref[...] = part

        @pl.when(jnp.logical_not(first))
        def _():
            l_ref[...] += part

    blk = pl.BlockSpec((tm, D), lambda i: (i, 0))
    return _pcall(
        body, name="loss_head", grid=(T // tm,), in_specs=[blk, blk],
        out_specs=[blk, pl.BlockSpec((1, LANES), lambda i: (0, 0))],
        out_shape=[jax.ShapeDtypeStruct((T, D), F32), jax.ShapeDtypeStruct((1, LANES), F32)],
        compiler_params=_params("arbitrary"),
    )(y, target)


def _as_rows(a):
    return a.reshape(-1, a.shape[-1])


def _adamw(w, g, m, v):
    w2, g2, m2, v2 = (_as_rows(t) for t in (w, g, m, v))
    R, Cc = w2.shape
    cap = max(SUBLANES, min(512, (256 * 1024) // Cc))
    tm = R if (R <= cap or R % SUBLANES) else max(t for t in range(SUBLANES, cap + 1, SUBLANES) if R % t == 0)
    c1 = 1.0 - ADAM_B1 ** ADAM_STEP
    c2 = 1.0 - ADAM_B2 ** ADAM_STEP

    def body(w_ref, g_ref, m_ref, v_ref, d_ref, nm_ref, nv_ref):
        gg = g_ref[...]
        nm = ADAM_B1 * m_ref[...] + (1.0 - ADAM_B1) * gg
        nv = ADAM_B2 * v_ref[...] + (1.0 - ADAM_B2) * (gg * gg)
        d_ref[...] = (-ADAM_LR) * ((nm / c1) / (jnp.sqrt(nv / c2) + ADAM_EPS) + ADAM_WD * w_ref[...])
        nm_ref[...] = nm
        nv_ref[...] = nv

    blk = pl.BlockSpec((tm, Cc), lambda i: (i, 0))
    res = _pcall(
        body, name="adamw", grid=(R // tm,), in_specs=[blk] * 4, out_specs=[blk] * 3,
        out_shape=[jax.ShapeDtypeStruct((R, Cc), F32)] * 3, compiler_params=_params("parallel"),
    )(w2, g2, m2, v2)
    return tuple(t.reshape(w.shape) for t in res)


def _where_am_i():
    x, y, c = lax.axis_index("x"), lax.axis_index("y"), lax.axis_index("c")
    chips = [(1 - x, y), (x, 1 - y), (1 - x, 1 - y)]
    return x, y, c, chips


def _remote(src, dst, send_sems, recv_sems, k, to):
    return pltpu.make_async_remote_copy(src_ref=src, dst_ref=dst, send_sem=send_sems.at[k], recv_sem=recv_sems.at[k],
                                        device_id=to, device_id_type=MESH)


def _comm_call(body, name, ins, out_shapes, n_remote, n_local):
    return _pcall_comm(
        body, name=name, in_specs=[ANY] * len(ins), out_specs=[ANY] * len(out_shapes), out_shape=out_shapes,
        scratch_shapes=[pltpu.SemaphoreType.DMA((n_remote,)), pltpu.SemaphoreType.DMA((n_remote,)),
                        pltpu.SemaphoreType.DMA((max(n_local, 1),))],
    )(*ins)


def _gather_weights(shards):
    n = len(shards)

    def body(*refs):
        ins, outs = refs[:n], refs[n:2 * n]
        send_sems, recv_sems, local_sems = refs[2 * n:]
        x, y, c, chips = _where_am_i()
        s = 2 * x + y
        sib = (x, y, 1 - c)
        first = []
        for t in range(n):
            for j, (cx, cy) in enumerate(chips):
                first.append(_remote(ins[t].at[:, c], outs[t].at[:, s, c], send_sems, recv_sems, 6 * t + j, (cx, cy, c)))
        for cp in first:
            cp.start()
        passed = []
        for j, (cx, cy) in enumerate(chips):
            sj = 2 * cx + cy
            for t in range(n):
                land = outs[t].at[:, sj, c]
                _remote(land, land, send_sems, recv_sems, 6 * t + j, (cx, cy, c)).wait_recv()
                fw = _remote(land, land, send_sems, recv_sems, 6 * t + 3 + j, sib)
                fw.start()
                passed.append(fw)
        for j, (cx, cy) in enumerate(chips):
            sj = 2 * cx + cy
            for t in range(n):
                land = outs[t].at[:, sj, 1 - c]
                _remote(land, land, send_sems, recv_sems, 6 * t + 3 + j, sib).wait_recv()
        for cp in first + passed:
            cp.wait_send()

    out_shapes = [jax.ShapeDtypeStruct((t.shape[0], N_CHIPS) + t.shape[1:], t.dtype) for t in shards]
    got = _comm_call(body, "gather_weights", shards, out_shapes, 6 * n, 0)
    s = 2 * lax.axis_index("x") + lax.axis_index("y")
    return [lax.dynamic_update_slice(g, t[:, None], (0, s, 0, 0, 0)) for g, t in zip(got, shards)]


def _gather_rider(shards):
    n = len(shards)

    def copies(ins, outs, send_sems, recv_sems):
        x, y, c, chips = _where_am_i()
        s = 2 * x + y
        return [_remote(ins[t].at[:, c], outs[t].at[:, s, c], send_sems, recv_sems, 3 * t + j, (cx, cy, c))
                for t in range(n) for j, (cx, cy) in enumerate(chips)]

    def start(ins, outs, send_sems, recv_sems):
        for cp in copies(ins, outs, send_sems, recv_sems):
            cp.start()

    def finish(ins, outs, send_sems, recv_sems):
        x, y, c, chips = _where_am_i()
        for t in range(n):
            for j, (cx, cy) in enumerate(chips):
                land = outs[t].at[:, 2 * cx + cy, c]
                _remote(land, land, send_sems, recv_sems, 3 * t + j, (cx, cy, c)).wait_recv()
        for cp in copies(ins, outs, send_sems, recv_sems):
            cp.wait_send()

    out_shapes = [jax.ShapeDtypeStruct((t.shape[0], N_CHIPS) + t.shape[1:], t.dtype) for t in shards]
    return dict(name="gather", ins=list(shards), out_shapes=out_shapes, n=3 * n, start=start, finish=finish)


def _gather_forward(landed, shards):
    n = len(landed)

    def body(*refs):
        outs = refs[n:2 * n]
        send_sems, recv_sems, _ = refs[2 * n:]
        x, y, c, chips = _where_am_i()
        sib = (x, y, 1 - c)
        cps = []
        for t in range(n):
            for j, (cx, cy) in enumerate(chips):
                land = outs[t].at[:, 2 * cx + cy, c]
                cps.append(_remote(land, land, send_sems, recv_sems, 3 * t + j, sib))
        for cp in cps:
            cp.start()
        for t in range(n):
            for j, (cx, cy) in enumerate(chips):
                land = outs[t].at[:, 2 * cx + cy, 1 - c]
                _remote(land, land, send_sems, recv_sems, 3 * t + j, sib).wait_recv()
        for cp in cps:
            cp.wait_send()

    got = _pcall_comm(
        body, name="gather_forward", in_specs=[ANY] * n, out_specs=[ANY] * n,
        out_shape=[jax.ShapeDtypeStruct(t.shape, t.dtype) for t in landed], input_output_aliases={t: t for t in range(n)},
        scratch_shapes=[pltpu.SemaphoreType.DMA((3 * n,)), pltpu.SemaphoreType.DMA((3 * n,)), pltpu.SemaphoreType.DMA((1,))],
    )(*landed)
    s = 2 * lax.axis_index("x") + lax.axis_index("y")
    return [lax.dynamic_update_slice(g, t[:, None], (0, s, 0, 0, 0)) for g, t in zip(got, shards)]


def _gather_small(v):
    def body(v_ref, out_ref, send_sems, recv_sems, local_sems):
        x, y, c, chips = _where_am_i()
        s = 2 * x + y
        mine = pltpu.make_async_copy(v_ref, out_ref.at[s], local_sems.at[0])
        mine.start()
        sends = [_remote(v_ref, out_ref.at[s], send_sems, recv_sems, j, (cx, cy, c)) for j, (cx, cy) in enumerate(chips)]
        for cp in sends:
            cp.start()
        for j, (cx, cy) in enumerate(chips):
            land = out_ref.at[2 * cx + cy]
            _remote(land, land, send_sems, recv_sems, j, (cx, cy, c)).wait_recv()
        for cp in sends:
            cp.wait_send()
        mine.wait()

    return _comm_call(body, "gather_small", [v], [jax.ShapeDtypeStruct((N_CHIPS,) + v.shape, v.dtype)], 3, 1)[0]


def _swap_sibling_halves(grads):
    n = len(grads)

    def body(*refs):
        ins, outs = refs[:n], refs[n:2 * n]
        send_sems, recv_sems, _ = refs[2 * n:]
        x, y, c, _chips = _where_am_i()
        sib = (x, y, 1 - c)
        cps = [_remote(ins[t].at[:, :, 1 - c], outs[t], send_sems, recv_sems, t, sib) for t in range(n)]
        for cp in cps:
            cp.start()
        for cp in cps:
            cp.wait()

    out_shapes = [jax.ShapeDtypeStruct(g.shape[:2] + g.shape[3:], g.dtype) for g in grads]
    return _comm_call(body, "grad_swap_halves", grads, out_shapes, n, 0)


def _exchange_chips(parts):
    n = len(parts)

    def body(*refs):
        ins, outs = refs[:n], refs[n:2 * n]
        send_sems, recv_sems, _ = refs[2 * n:]
        x, y, c, chips = _where_am_i()
        cps = []
        for t in range(n):
            for j, (cx, cy) in enumerate(chips):
                cps.append(_remote(ins[t].at[:, 2 * cx + cy], outs[t].at[j], send_sems, recv_sems, 3 * t + j, (cx, cy, c)))
        for cp in cps:
            cp.start()
        for cp in cps:
            cp.wait()

    out_shapes = [jax.ShapeDtypeStruct((3, p.shape[0]) + p.shape[2:], p.dtype) for p in parts]
    return _comm_call(body, "grad_exchange_chips", parts, out_shapes, 3 * n, 0)


def _exchange_rider(parts):
    n = len(parts)

    def copies(ins, outs, send_sems, recv_sems):
        x, y, c, chips = _where_am_i()
        return [_remote(ins[t].at[:, 2 * cx + cy], outs[t].at[j], send_sems, recv_sems, 3 * t + j, (cx, cy, c))
                for t in range(n) for j, (cx, cy) in enumerate(chips)]

    def start(ins, outs, send_sems, recv_sems):
        for cp in copies(ins, outs, send_sems, recv_sems):
            cp.start()

    def finish(ins, outs, send_sems, recv_sems):
        for cp in copies(ins, outs, send_sems, recv_sems):
            cp.wait()

    out_shapes = [jax.ShapeDtypeStruct((3, p.shape[0]) + p.shape[2:], p.dtype) for p in parts]
    return dict(name="exchange", ins=list(parts), out_shapes=out_shapes, n=3 * n, start=start, finish=finish)


def _join_sibling_halves(halves):
    n = len(halves)

    def body(*refs):
        ins, outs = refs[:n], refs[n:2 * n]
        send_sems, recv_sems, local_sems = refs[2 * n:]
        x, y, c, _chips = _where_am_i()
        sib = (x, y, 1 - c)
        cps = [_remote(ins[t], outs[t].at[:, c], send_sems, recv_sems, t, sib) for t in range(n)]
        for cp in cps:
            cp.start()
        for t in range(n):
            land = outs[t].at[:, 1 - c]
            _remote(land, land, send_sems, recv_sems, t, sib).wait_recv()
        for cp in cps:
            cp.wait_send()

    out_shapes = [jax.ShapeDtypeStruct((h.shape[0], 2) + h.shape[1:], h.dtype) for h in halves]
    got = _comm_call(body, "grad_join_halves", halves, out_shapes, n, 0)
    c = lax.axis_index("c")
    return [lax.dynamic_update_slice(g, h[:, None], (0, c, 0, 0)) for g, h in zip(got, halves)]


def _swap_small(v):
    def body(v_ref, out_ref, send_sems, recv_sems, _):
        x, y, c, _chips = _where_am_i()
        cp = _remote(v_ref, out_ref, send_sems, recv_sems, 0, (x, y, 1 - c))
        cp.start()
        cp.wait()

    return _comm_call(body, "small_swap", [v], [jax.ShapeDtypeStruct(v.shape, v.dtype)], 1, 0)[0]


def _exchange_small(v):
    def body(v_ref, out_ref, send_sems, recv_sems, _):
        x, y, c, chips = _where_am_i()
        cps = [_remote(v_ref, out_ref.at[j], send_sems, recv_sems, j, (cx, cy, c)) for j, (cx, cy) in enumerate(chips)]
        for cp in cps:
            cp.start()
        for cp in cps:
            cp.wait()

    return _comm_call(body, "small_exchange", [v], [jax.ShapeDtypeStruct((3,) + v.shape, v.dtype)], 3, 0)[0]


def _sum_rows(name, terms, out_dtypes):
    R, Cc = terms[0].shape
    tm = R if R <= 256 else max(t for t in range(16, 257, 16) if R % t == 0)
    n = len(terms)

    def body(*refs):
        acc = refs[0][...].astype(F32)
        for r in refs[1:n]:
            acc = acc + r[...].astype(F32)
        for o in refs[n:]:
            o[...] = acc.astype(o.dtype)

    blk = pl.BlockSpec((tm, Cc), lambda i: (i, 0))
    return _pcall(
        body, name=name, grid=(R // tm,), in_specs=[blk] * n, out_specs=[blk] * len(out_dtypes),
        out_shape=[jax.ShapeDtypeStruct((R, Cc), d) for d in out_dtypes], compiler_params=_params("parallel"),
    )(*terms)


def _pair_sum(g5, r1, core, shard):
    A4, _, Rh, Cc = g5.shape
    A = A4 // N_CHIPS
    tr = Rh if Rh <= 256 else max(t for t in range(16, 257, 16) if Rh % t == 0)

    def body(core_ref, shard_ref, g_ref, r_ref, qb_ref, qf_ref):
        q = g_ref[...] + r_ref[...]
        qb_ref[...] = q.astype(BF16)

        @pl.when(pl.program_id(2) == shard_ref[0])
        def _():
            qf_ref[...] = q

    grid_spec = pltpu.PrefetchScalarGridSpec(
        num_scalar_prefetch=2, grid=(A, Rh // tr, N_CHIPS),
        in_specs=[pl.BlockSpec((None, None, tr, Cc), lambda a, r, sh, core, shard: (a * N_CHIPS + sh, core[0], r, 0)),
                  pl.BlockSpec((None, tr, Cc), lambda a, r, sh, core, shard: (a * N_CHIPS + sh, r, 0))],
        out_specs=[pl.BlockSpec((None, tr, Cc), lambda a, r, sh, core, shard: (a * N_CHIPS + sh, r, 0)),
                   pl.BlockSpec((None, tr, Cc), lambda a, r, sh, core, shard: (a, r, 0))],
    )
    return _pcall(
        body, name="grad_pair_sum", grid_spec=grid_spec,
        out_shape=[jax.ShapeDtypeStruct((A4, Rh, Cc), BF16), jax.ShapeDtypeStruct((A, Rh, Cc), F32)],
        compiler_params=_params("parallel", "parallel", "arbitrary"),
    )(core, shard, g5, r1)


def _swap_rider(grads):
    n = len(grads)

    def copies(ins, outs, send_sems, recv_sems):
        x, y, c, _chips = _where_am_i()
        return [_remote(ins[t].at[:, :, 1 - c], outs[t], send_sems, recv_sems, t, (x, y, 1 - c)) for t in range(n)]

    def start(ins, outs, send_sems, recv_sems):
        for cp in copies(ins, outs, send_sems, recv_sems):
            cp.start()

    def finish(ins, outs, send_sems, recv_sems):
        for cp in copies(ins, outs, send_sems, recv_sems):
            cp.wait()

    out_shapes = [jax.ShapeDtypeStruct(g.shape[:2] + g.shape[3:], g.dtype) for g in grads]
    return dict(name="swap", ins=list(grads), out_shapes=out_shapes, n=n, start=start, finish=finish)


def _reduce_chip(grads, r1, core, shard):
    qb, qf = [], []
    for g, r in zip(grads, r1):
        A, _, _, Rh, Cc = g.shape
        b, f = _pair_sum(g.reshape(A * N_CHIPS, 2, Rh, Cc), r.reshape(A * N_CHIPS, Rh, Cc), core, shard)
        qb.append(b.reshape(A, N_CHIPS, Rh, Cc))
        qf.append(f)
    return qb, qf


def _reduce_finish(qf, r2):
    halves = []
    for f, r in zip(qf, r2):
        A, Rh, Cc = f.shape
        terms = [f.reshape(A * Rh, Cc)] + [r[j].reshape(A * Rh, Cc) for j in range(3)]
        halves.append(_sum_rows("grad_chip_sum", terms, [F32])[0].reshape(A, Rh, Cc))
    full = _join_sibling_halves(halves)
    return [t.reshape(t.shape[0], 2 * t.shape[2], t.shape[3]) for t in full]


def _allreduce_small(v):
    pair = _sum_rows("small_pair_sum", [v, _swap_small(v)], [F32])[0]
    others = _exchange_small(pair)
    x, y = lax.axis_index("x"), lax.axis_index("y")
    s = 2 * x + y
    stack = jnp.concatenate([pair[None], others], axis=0)
    src = jnp.stack([s, s ^ 2, s ^ 1, s ^ 3])
    order = jnp.argsort(src)
    terms = [lax.dynamic_index_in_dim(stack, order[k], 0, keepdims=False) for k in range(N_CHIPS)]
    return _sum_rows("small_chip_sum", terms, [F32])[0]


def _block_diag(w):
    nb, bw, _ = w.shape
    per = LANES // bw
    w = w.reshape(nb // per, per, bw, bw)
    eye = jnp.eye(per, dtype=w.dtype)
    bd = jnp.einsum("tpij,pq->tpiqj", w, eye).reshape(nb // per, LANES, LANES)
    return bd.astype(BF16)


def _block_diag_grad(g, bw):
    nt = g.shape[0]
    per = LANES // bw
    g = g.reshape(nt, per, bw, per, bw)
    return jnp.stack([g[:, p, :, p, :] for p in range(per)], axis=1).reshape(nt * per, bw, bw)


def _split5(w):
    R, Cc = w.shape[-2:]
    return w.reshape(-1, 2, R // 2, Cc)


def kernel(x, w_in, conv_w, conv_b, w_rg, b_rg, w_ig, b_ig, lru_lambda, sinks, w_branch, w_out, ln1_g, ln1_b, w_ffn_in, w_ffn_out, ln2_g, ln2_b, loss_target, m_w_in, m_conv_w, m_conv_b, m_w_rg, m_b_rg, m_w_ig, m_b_ig, m_lru_lambda, m_sinks, m_w_branch, m_w_out, m_ln1_g, m_ln1_b, m_w_ffn_in, m_w_ffn_out, m_ln2_g, m_ln2_b, v_w_in, v_conv_w, v_conv_b, v_w_rg, v_b_rg, v_w_ig, v_b_ig, v_lru_lambda, v_sinks, v_w_branch, v_w_out, v_ln1_g, v_ln1_b, v_w_ffn_in, v_w_ffn_out, v_ln2_g, v_ln2_b):
    B, S, D = x.shape
    T = B * S
    L = w_in.shape[0]
    H = D // HEAD_DIM
    KVB = D // SWA_GROUP
    FH = w_ffn_out.shape[1] * N_CHIPS
    C = w_in.shape[2] * N_CHIPS
    alpha = (2.0 * L) ** 0.25
    off = {}
    pos = 0
    for nm, wd in (("lx", D), ("lg", D), ("qb", D), ("kb", KVB), ("vb", KVB), ("qc", D), ("kc", D), ("vc", D), ("gt", 3 * D)):
        off[nm] = pos
        pos += wd
    assert pos == C
    cx, cy, cc = lax.axis_index("x"), lax.axis_index("y"), lax.axis_index("c")
    shard = (2 * cx + cy).astype(jnp.int32)
    core_a = cc.astype(jnp.int32).reshape(1)
    shard_a = shard.reshape(1)

    def shard_views(l):
        return [_split5(w_in[l].astype(BF16)), _split5(w_branch[l].astype(BF16)), _split5(w_out[l].astype(BF16)),
                _split5(w_ffn_in[l].astype(BF16)), _split5(w_ffn_out[l].astype(BF16))]

    def as_weights(g):
        return dict(
            w_in=g[0].reshape(N_CHIPS, D, C // N_CHIPS),
            w_branch=g[1].reshape(3, D, D),
            w_out=g[2].reshape(D, D),
            w_ffn_in=g[3].reshape(N_CHIPS, D, 2 * FH // N_CHIPS),
            w_ffn_out=g[4].reshape(FH, D),
        )

    first_views = shard_views(0)
    w_in0 = _gather_weights(first_views[:1])
    full = [dict(w_in=w_in0[0].reshape(N_CHIPS, D, C // N_CHIPS))]
    cw_all = _gather_small(conv_w.reshape(L * CONV_WIDTH, D // N_CHIPS))
    conv_w_full = jnp.transpose(cw_all, (1, 0, 2)).reshape(L, CONV_WIDTH, D)

    def layer_params(l):
        return dict(conv_w=conv_w_full[l], conv_b=conv_b[l][None], w_rg_bd=_block_diag(w_rg[l]), b_rg=b_rg[l][None],
                    w_ig_bd=_block_diag(w_ig[l]), b_ig=b_ig[l][None], lam=lru_lambda[l][None])

    def sink_rows(l, hb):
        sk = sinks[l].reshape(H // hb, 1, hb)
        return jnp.pad(sk, ((0, 0), (0, 0), (0, LANES - hb)))

    hb_b = SWA_HB

    saved = []
    xin = x.reshape(T, D)
    for l in range(L):
        fw, lp = full[l], layer_params(l)
        proj = _matmul(xin, fw["w_in"], mode="nn", name="mm_proj", tm=512, n_outer=True)
        proj3 = proj.reshape(B, S, C)
        h3, ya3 = _lru_fwd(proj3, lp, D=D, x_off=off["lx"], g_off=off["lg"])
        skr = sink_rows(l, hb_b)
        swa_kw = dict(D=D, q_off=off["qb"], k_off=off["kb"], v_off=off["vb"])
        if l == 0:
            res = _swa_fwd(proj3, skr, rider=_gather_rider(first_views[1:]), **swa_kw)
            yb3, lse_b = res[0], res[1]
            fw = as_weights(w_in0 + _gather_forward(res[2:], first_views[1:]))
            full[0] = fw
        else:
            yb3, lse_b = _swa_fwd(proj3, skr, **swa_kw)
        dil_kw = dict(D=D, q_off=off["qc"], k_off=off["kc"], v_off=off["vc"])
        if l + 1 < L:
            nxt = shard_views(l + 1)
            res = _dil_fwd(proj3, rider=_gather_rider(nxt), **dil_kw)
            yc3, lse_c = res[0], res[1]
            full.append(as_weights(_gather_forward(res[2:], nxt)))
        else:
            yc3, lse_c = _dil_fwd(proj3, **dil_kw)
        ya, yb, yc = ya3.reshape(T, D), yb3.reshape(T, D), yc3.reshape(T, D)
        branch, merged = _branch_fwd([ya, yb, yc], fw["w_branch"], proj, D=D, g_off=off["gt"])
        z1, x1 = _matmul(merged, fw["w_out"], mode="nn", name="mm_out_ln", tn=1024, resid=xin, rs=alpha,
                         ln=(ln1_g[l][None], ln1_b[l][None]))
        hh = _matmul(x1, fw["w_ffn_in"], mode="nn", name="mm_ffn_in", n_outer=True)
        f = _swiglu_fwd(hh)
        z2, x2 = _matmul(f, fw["w_ffn_out"], mode="nn", name="mm_ffn_out_ln", tn=1024, tk=4096, resid=x1, rs=alpha,
                         ln=(ln2_g[l][None], ln2_b[l][None]))
        saved.append(dict(x=xin, proj=proj, h3=h3, ya=ya, yb=yb, lse_b=lse_b, yc=yc, lse_c=lse_c, branch=branch,
                          merged=merged, z1=z1, x1=x1, hh=hh, f=f, z2=z2, skr=skr))
        xin = x2

    dx, loss_rows = _loss_head(xin, loss_target.reshape(T, D))
    loss = lax.psum(jnp.sum(loss_rows), ("x", "y", "c"))

    big = {k: [None] * L for k in ("w_in", "w_branch", "w_out", "w_ffn_in", "w_ffn_out")}
    small = [None] * L

    def store_reduced(l, red):
        big["w_in"][l] = red[0].reshape(D, C // N_CHIPS)
        big["w_branch"][l] = red[1].reshape(3, D // N_CHIPS, D)
        big["w_out"][l] = red[2].reshape(D // N_CHIPS, D)
        big["w_ffn_in"][l] = red[3].reshape(D, 2 * FH // N_CHIPS)
        big["w_ffn_out"][l] = red[4].reshape(FH // N_CHIPS, D)

    above = None
    pending = None
    for l in reversed(range(L)):
        fw, lp, sv = full[l], layer_params(l), saved[l]
        dz2, dg2, db2 = _ln_bwd(dx, sv["z2"], ln2_g[l][None])
        df = _matmul(dz2, fw["w_ffn_out"], mode="nt", name="mm_dffn_out_x", tn=4096, tk=1024)
        g_ffn_out = _matmul(sv["f"], dz2, mode="tn", name="mm_dffn_out_w", tm=1408, tn=1024, tk=1024)
        if above is None:
            dhh, _unused = _swiglu_bwd(sv["hh"], df)
        else:
            dhh, r1 = _swiglu_bwd(sv["hh"], df, rider=_swap_rider(above[1]))
            pending = (above[0],) + _reduce_chip(above[1], r1, core_a, shard_a)
        dx1 = _matmul(dhh, fw["w_ffn_in"], mode="nt", name="mm_dffn_in_x", tm=1024, tn=1024, resid=dz2, rs=alpha)
        g_ffn_in = _matmul(sv["x1"], dhh, mode="tn", name="mm_dffn_in_w", tm=1024, tk=1024, out_shards=N_CHIPS)
        dz1, dg1, db1 = _ln_bwd(dx1, sv["z1"], ln1_g[l][None])
        dmerged = _matmul(dz1, fw["w_out"], mode="nt", name="mm_dout_x", tn=1024, tk=1024)
        g_out = _matmul(sv["merged"], dz1, mode="tn", name="mm_dout_w", tm=1024, tn=1024, tk=1024)
        dbranch, dgates = _branch_bwd(dmerged, sv["branch"], sv["proj"], D=D, g_off=off["gt"])
        ys = [sv["ya"], sv["yb"], sv["yc"]]
        dys, g_branch = [], []
        for n in range(3):
            dys.append(_matmul(dbranch[n], fw["w_branch"][n], mode="nt", name="mm_dbranch_x", tn=1024, tk=1024))
            g_branch.append(_matmul(ys[n], dbranch[n], mode="tn", name="mm_dbranch_w", tm=1024, tn=1024, tk=1024))
        proj3 = sv["proj"].reshape(B, S, C)
        r3 = lambda t: t.reshape(B, S, t.shape[-1])
        lru = _lru_bwd(proj3, sv["h3"], r3(dys[0]), lp, D=D, x_off=off["lx"], g_off=off["lg"])
        dxr, dgate = lru[0], lru[1]
        dqb, dkb, dvb, dsk = _swa_bwd(proj3, r3(sv["yb"]), sv["lse_b"], r3(dys[1]), sv["skr"], D=D, q_off=off["qb"],
                                      k_off=off["kb"], v_off=off["vb"])
        dil_kw = dict(D=D, q_off=off["qc"], k_off=off["kc"], v_off=off["vc"])
        if pending is None:
            acc = _dil_bwd(proj3, r3(sv["yc"]), sv["lse_c"], r3(dys[2]), **dil_kw)
        else:
            res = _dil_bwd(proj3, r3(sv["yc"]), sv["lse_c"], r3(dys[2]), rider=_exchange_rider(pending[1]), **dil_kw)
            acc = res[:3]
            store_reduced(pending[0], _reduce_finish(pending[2], res[3:]))
        f2 = lambda t: t.reshape(T, t.shape[-1]).astype(BF16)
        dproj = jnp.concatenate([f2(dxr), f2(dgate), f2(dqb), f2(dkb), f2(dvb), f2(acc[0]), f2(acc[1]), f2(acc[2])] + dgates, axis=1)
        dx = _matmul(dproj, fw["w_in"], mode="nt", name="mm_dproj_x", tm=1024, tn=1024, resid=dz1, rs=alpha)
        g_in = _matmul(sv["x"], dproj, mode="tn", name="mm_dproj_w", tm=512, tk=1024, out_shards=N_CHIPS)

        g5 = [g_in.reshape(1, N_CHIPS, 2, D // 2, C // N_CHIPS),
              jnp.stack(g_branch).reshape(3, N_CHIPS, 2, D // N_CHIPS // 2, D),
              g_out.reshape(1, N_CHIPS, 2, D // N_CHIPS // 2, D),
              g_ffn_in.reshape(1, N_CHIPS, 2, D // 2, 2 * FH // N_CHIPS),
              g_ffn_out.reshape(1, N_CHIPS, 2, FH // N_CHIPS // 2, D)]
        above = (l, g5)

        dsinks = jnp.sum(dsk, axis=0)[:, 0, :hb_b].reshape(H)
        bw = w_rg.shape[-1]
        small[l] = [lru[2].reshape(-1), lru[3].reshape(-1), _block_diag_grad(lru[4], bw).reshape(-1), lru[5].reshape(-1),
                    _block_diag_grad(lru[6], bw).reshape(-1), lru[7].reshape(-1), lru[8].reshape(-1),
                    jnp.pad(dsinks, (0, LANES - H)), dg1.reshape(-1), db1.reshape(-1), dg2.reshape(-1), db2.reshape(-1)]

    qb, qf = _reduce_chip(above[1], _swap_sibling_halves(above[1]), core_a, shard_a)
    store_reduced(above[0], _reduce_finish(qf, _exchange_chips(qb)))

    sizes = [t.size for t in small[0]]
    flat = jnp.concatenate([t for l in range(L) for t in small[l]])
    n_flat = flat.size
    rows = -(-n_flat // (LANES * 256)) * 256
    flat = jnp.pad(flat, (0, rows * LANES - n_flat)).reshape(rows, LANES)
    red_small = _allreduce_small(flat).reshape(-1)
    per_layer = sum(sizes)
    names = ["conv_w", "conv_b", "w_rg", "b_rg", "w_ig", "b_ig", "lru_lambda", "sinks", "ln1_g", "ln1_b", "ln2_g", "ln2_b"]
    sg = {nm: [] for nm in names}
    for l in range(L):
        p = l * per_layer
        for nm, sz in zip(names, sizes):
            sg[nm].append(red_small[p:p + sz])
            p += sz
    grads = dict(
        w_in=jnp.stack(big["w_in"]), w_branch=jnp.stack(big["w_branch"]), w_out=jnp.stack(big["w_out"]),
        w_ffn_in=jnp.stack(big["w_ffn_in"]), w_ffn_out=jnp.stack(big["w_ffn_out"]),
        conv_w=lax.dynamic_slice_in_dim(jnp.stack(sg["conv_w"]).reshape(L, CONV_WIDTH, D), shard * (D // N_CHIPS), D // N_CHIPS, axis=2),
        conv_b=jnp.stack(sg["conv_b"]), w_rg=jnp.stack(sg["w_rg"]).reshape(w_rg.shape), b_rg=jnp.stack(sg["b_rg"]),
        w_ig=jnp.stack(sg["w_ig"]).reshape(w_ig.shape), b_ig=jnp.stack(sg["b_ig"]), lru_lambda=jnp.stack(sg["lru_lambda"]),
        sinks=jnp.stack(sg["sinks"])[:, :H], ln1_g=jnp.stack(sg["ln1_g"]), ln1_b=jnp.stack(sg["ln1_b"]),
        ln2_g=jnp.stack(sg["ln2_g"]), ln2_b=jnp.stack(sg["ln2_b"]),
    )

    order = ["w_in", "conv_w", "conv_b", "w_rg", "b_rg", "w_ig", "b_ig", "lru_lambda", "sinks", "w_branch", "w_out",
             "ln1_g", "ln1_b", "w_ffn_in", "w_ffn_out", "ln2_g", "ln2_b"]
    weights = dict(w_in=w_in, conv_w=conv_w, conv_b=conv_b, w_rg=w_rg, b_rg=b_rg, w_ig=w_ig, b_ig=b_ig, lru_lambda=lru_lambda,
                   sinks=sinks, w_branch=w_branch, w_out=w_out, ln1_g=ln1_g, ln1_b=ln1_b, w_ffn_in=w_ffn_in,
                   w_ffn_out=w_ffn_out, ln2_g=ln2_g, ln2_b=ln2_b)
    ms = dict(w_in=m_w_in, conv_w=m_conv_w, conv_b=m_conv_b, w_rg=m_w_rg, b_rg=m_b_rg, w_ig=m_w_ig, b_ig=m_b_ig,
              lru_lambda=m_lru_lambda, sinks=m_sinks, w_branch=m_w_branch, w_out=m_w_out, ln1_g=m_ln1_g, ln1_b=m_ln1_b,
              w_ffn_in=m_w_ffn_in, w_ffn_out=m_w_ffn_out, ln2_g=m_ln2_g, ln2_b=m_ln2_b)
    vs = dict(w_in=v_w_in, conv_w=v_conv_w, conv_b=v_conv_b, w_rg=v_w_rg, b_rg=v_b_rg, w_ig=v_w_ig, b_ig=v_b_ig,
              lru_lambda=v_lru_lambda, sinks=v_sinks, w_branch=v_w_branch, w_out=v_w_out, ln1_g=v_ln1_g, ln1_b=v_ln1_b,
              w_ffn_in=v_w_ffn_in, w_ffn_out=v_w_ffn_out, ln2_g=v_ln2_g, ln2_b=v_ln2_b)
    deltas, new_m, new_v = {}, {}, {}
    for nm in order:
        deltas[nm], new_m[nm], new_v[nm] = _adamw(weights[nm], grads[nm], ms[nm], vs[nm])
    return (loss, dx.reshape(B, S, D), *[grads[nm] for nm in order], *[deltas[nm] for nm in order],
            *[new_m[nm] for nm in order], *[new_v[nm] for nm in order])
```

```python
import math

import jax
import jax.numpy as jnp
from jax import lax
from jax.experimental import pallas as pl
from jax.experimental.pallas import tpu as pltpu

HEAD_DIM = 64
WIN = 128
DILS = (1, 4, 16)
SWA_GROUP = 4
CONV_WIDTH = 4
LRU_C = 8.0
LN_EPS = 1e-5
NEG_INF = -1e30
N_CHIPS = 4
ADAM_LR, ADAM_B1, ADAM_B2, ADAM_EPS, ADAM_WD, ADAM_STEP = 0.001, 0.9, 0.999, 1e-08, 0.01, 10

LANES = 128
SUBLANES = 8
VMEM_LIMIT = 48 * 1024 * 1024

F32 = jnp.float32
BF16 = jnp.bfloat16
MESH = pl.DeviceIdType.MESH
ANY = pl.BlockSpec(memory_space=pl.ANY)


def _pcall(body, **kw):
    return pl.pallas_call(body, **kw)


def _pcall_comm(body, **kw):
    return pl.pallas_call(body, **kw)


def _params(*sem):
    return pltpu.CompilerParams(dimension_semantics=tuple(sem), vmem_limit_bytes=VMEM_LIMIT)


def _tile(dim, target):
    if dim <= target:
        return dim
    best = None
    for t in range(LANES, target + 1, LANES):
        if dim % t == 0:
            best = t
    assert best is not None, (dim, target)
    return best


def _sigmoid(x):
    return 1.0 / (1.0 + jnp.exp(-x))


def _dot(a, b, dims):
    return lax.dot_general(a, b, (dims, ((), ())), preferred_element_type=F32)


def _dot_nn(a, b):
    return _dot(a, b, ((1,), (0,)))


def _dot_nt(a, b):
    return _dot(a, b, ((1,), (1,)))


def _dot_tn(a, b):
    return _dot(a, b, ((0,), (0,)))


def _matmul(a, b, *, mode, name, out_dtype=F32, tm=512, tn=512, tk=2048, resid=None, rs=1.0, out_shards=0, n_outer=False,
            ln=None, a_pick=0, b_pick=0, rider=None):
    b_sh = b.ndim == 3
    a_st = a.ndim == 3
    if mode == "nn":
        M, K = a.shape[-2:]
        N = b.shape[-1] * (b.shape[0] if b_sh else 1)
    elif mode == "nt":
        M, K = a.shape[-2:]
        N = b.shape[-2]
    else:
        K, M = a.shape
        N = b.shape[-1]
    tm = _tile(M, tm)
    if mode == "nn" and b_sh:
        tn = b.shape[-1]
    elif out_shards:
        tn = N // out_shards
    else:
        tn = _tile(N, tn)
    if mode == "nt" and b_sh:
        tk = b.shape[-1]
    else:
        tk = _tile(K, tk)
    nk = K // tk
    grid = (N // tn, M // tm, nk) if n_outer else (M // tm, N // tn, nk)

    def spec(shape, f):
        return pl.BlockSpec(shape, (lambda g0, g1, k: f(g1, g0, k)) if n_outer else f)

    a_rows = spec((None, tm, tk), lambda i, j, k: (a_pick, i, k)) if a_st else spec((tm, tk), lambda i, j, k: (i, k))
    if mode == "nn":
        a_spec = a_rows
        b_spec = spec((None, tk, tn), lambda i, j, k: (j, k, 0)) if b_sh else spec((tk, tn), lambda i, j, k: (k, j))
        contract = _dot_nn
    elif mode == "nt":
        a_spec = a_rows
        b_spec = spec((None, tn, tk), lambda i, j, k: (k, j, 0)) if b_sh else spec((tn, tk), lambda i, j, k: (j, k))
        contract = _dot_nt
    else:
        a_spec = spec((tk, tm), lambda i, j, k: (k, i))
        b_spec = spec((None, tk, tn), lambda i, j, k: (b_pick, k, j)) if b_sh else spec((tk, tn), lambda i, j, k: (k, j))
        contract = _dot_tn
    if out_shards:
        out_shape = jax.ShapeDtypeStruct((out_shards, M, tn), out_dtype)
        o_spec = spec((None, tm, tn), lambda i, j, k: (j, i, 0))
    else:
        out_shape = jax.ShapeDtypeStruct((M, N), out_dtype)
        o_spec = spec((tm, tn), lambda i, j, k: (i, j))
    in_specs = [a_spec, b_spec]
    args = [a, b]
    if resid is not None:
        in_specs.append(spec((tm, tn), lambda i, j, k: (i, j)))
        args.append(resid)
    if ln is not None:
        assert tn == N and resid is not None and not out_shards
        in_specs += [spec((1, N), lambda i, j, k: (0, 0))] * 2
        args += list(ln)
        out_shape = [out_shape, out_shape]
        o_spec = [o_spec, o_spec]
    n_in = len(args)

    def body(*refs):
        a_ref, b_ref = refs[:2]
        r_ref = refs[2] if resid is not None else None
        o_ref = refs[n_in]
        part = contract(a_ref[...].astype(BF16), b_ref[...].astype(BF16))

        def finish(res):
            if resid is not None:
                res = res + rs * r_ref[...]
            o_ref[...] = res.astype(out_dtype)
            if ln is not None:
                g_ref, bb_ref, y_ref = refs[n_in - 2], refs[n_in - 1], refs[n_in + 1]
                zc = res - jnp.mean(res, axis=1, keepdims=True)
                var = jnp.mean(zc * zc, axis=1, keepdims=True)
                y_ref[...] = zc * lax.rsqrt(var + LN_EPS) * g_ref[...] + bb_ref[...]

        if nk == 1:
            finish(part)
            return
        acc_ref = refs[-1]
        k = pl.program_id(2)

        @pl.when(k == 0)
        def _():
            acc_ref[...] = part

        @pl.when(jnp.logical_and(k > 0, k < nk - 1))
        def _():
            acc_ref[...] += part

        @pl.when(k == nk - 1)
        def _():
            finish(acc_ref[...] + part)

    if rider is None:
        return _pcall(
            body, name=name, grid=grid, in_specs=in_specs, out_specs=o_spec, out_shape=out_shape,
            scratch_shapes=[pltpu.VMEM((tm, tn), F32)] if nk > 1 else [],
            compiler_params=_params("parallel", "parallel", "arbitrary"),
        )(*args)
    assert ln is None
    res = _call_with_rider(
        rider, body, name=name, grid=grid, in_specs=in_specs, out_specs=[o_spec], out_shape=[out_shape],
        scratch_shapes=[pltpu.VMEM((tm, tn), F32)] if nk > 1 else [], args=args)
    return res[0], list(res[1:])


def _shift_down(x, d, row):
    return jnp.where(row >= d, pltpu.roll(x, d, 0), 0.0)


def _shift_up(x, d, row, n):
    return jnp.where(row < n - d, pltpu.roll(x, n - d, 0), 0.0)


def _log1p(u):
    w = 1.0 + u
    return jnp.where(w == 1.0, u, jnp.log(w) * u / (w - 1.0))


def _gelu_parts(g):
    k = math.sqrt(2.0 / math.pi)
    c = 0.044715
    t = jnp.tanh(k * (g + c * g * g * g))
    val = 0.5 * g * (1.0 + t)
    der = 0.5 * (1.0 + t) + 0.5 * g * (1.0 - t * t) * k * (1.0 + 3.0 * c * g * g)
    return val, der


def _lru_gates(xr, cw_ref, cb_ref, wrg_ref, brg_ref, wig_ref, big_ref, lam_ref, row):
    xc = cw_ref[3:4, :] * xr + cb_ref[...]
    for d in range(1, CONV_WIDTH):
        xc = xc + cw_ref[3 - d:4 - d, :] * _shift_down(xr, d, row)
    xcb = xc.astype(BF16)
    r = _sigmoid(_dot_nn(xcb, wrg_ref[...]) + brg_ref[...])
    ig = _sigmoid(_dot_nn(xcb, wig_ref[...]) + big_ref[...])
    lam = lam_ref[...]
    sp = jnp.maximum(-lam, 0.0) + _log1p(jnp.exp(-jnp.abs(lam)))
    log_a = (-LRU_C) * r * sp
    a = jnp.exp(log_a)
    y2 = 2.0 * log_a
    one_m_a2 = jnp.where(y2 > -0.01, -(y2 + 0.5 * y2 * y2 + (1.0 / 6.0) * y2 * y2 * y2), 1.0 - jnp.exp(y2))
    mult = jnp.sqrt(one_m_a2)
    return xc, r, ig, sp, a, mult


def _scan_local(a, b, row, n, reverse):
    sub = row % SUBLANES
    d = 1
    while d < SUBLANES:
        if reverse:
            keep = sub < SUBLANES - d
            a_s = jnp.where(keep, pltpu.roll(a, n - d, 0), 1.0)
            b_s = jnp.where(keep, pltpu.roll(b, n - d, 0), 0.0)
        else:
            keep = sub >= d
            a_s = jnp.where(keep, pltpu.roll(a, d, 0), 1.0)
            b_s = jnp.where(keep, pltpu.roll(b, d, 0), 0.0)
        b = a * b_s + b
        a = a * a_s
        d *= 2
    return a, b


def _scan_carry(a_ref, b_ref, out_ref, n, reverse):
    ng = n // SUBLANES

    def step(gidx, carry):
        g = (ng - 1 - gidx) if reverse else gidx
        rows = pl.ds(pl.multiple_of(g * SUBLANES, SUBLANES), SUBLANES)
        h = a_ref[rows, :] * carry + b_ref[rows, :]
        out_ref[rows, :] = h
        return h[0:1, :] if reverse else h[SUBLANES - 1:SUBLANES, :]

    lax.fori_loop(0, ng, step, jnp.zeros((1, LANES), F32), unroll=8)


def _lru_specs(B, S, D, C, x_off, g_off):
    nct = D // LANES
    seq = lambda off: pl.BlockSpec((None, S, LANES), lambda ct, b: (b, 0, off // LANES + ct))
    row = lambda r: pl.BlockSpec((r, LANES), lambda ct, b: (0, ct))
    wbd = pl.BlockSpec((None, LANES, LANES), lambda ct, b: (ct, 0, 0))
    return nct, seq, row, wbd


def _lru_fwd(proj3, lp, *, D, x_off, g_off):
    B, S, C = proj3.shape
    nct, seq, row, wbd = _lru_specs(B, S, D, C, x_off, g_off)

    def body(xr_ref, g_ref, cw_ref, cb_ref, wrg_ref, brg_ref, wig_ref, big_ref, lam_ref, h_ref, ya_ref, a_s, b_s):
        rowi = lax.broadcasted_iota(jnp.int32, (S, LANES), 0)
        xr = xr_ref[...]
        xc, r, ig, sp, a, mult = _lru_gates(xr, cw_ref, cb_ref, wrg_ref, brg_ref, wig_ref, big_ref, lam_ref, rowi)
        al, bl = _scan_local(a, mult * (ig * xc), rowi, S, False)
        a_s[...] = al
        b_s[...] = bl
        _scan_carry(a_s, b_s, h_ref, S, False)
        gel, _ = _gelu_parts(g_ref[...])
        ya_ref[...] = (h_ref[...] * gel).astype(BF16)

    out_seq = pl.BlockSpec((None, S, LANES), lambda ct, b: (b, 0, ct))
    return _pcall(
        body, name="lru_fwd", grid=(nct, B),
        in_specs=[seq(x_off), seq(g_off), row(CONV_WIDTH), row(1), wbd, row(1), wbd, row(1), row(1)],
        out_specs=[out_seq, out_seq],
        out_shape=[jax.ShapeDtypeStruct((B, S, D), F32), jax.ShapeDtypeStruct((B, S, D), BF16)],
        scratch_shapes=[pltpu.VMEM((S, LANES), F32), pltpu.VMEM((S, LANES), F32)],
        compiler_params=_params("parallel", "parallel"),
    )(proj3, proj3, lp["conv_w"], lp["conv_b"], lp["w_rg_bd"], lp["b_rg"], lp["w_ig_bd"], lp["b_ig"], lp["lam"])


def _lru_bwd(proj3, h3, dya3, lp, *, D, x_off, g_off):
    B, S, C = proj3.shape
    nct, seq, row, wbd = _lru_specs(B, S, D, C, x_off, g_off)

    def body(xr_ref, g_ref, h_ref, dy_ref, cw_ref, cb_ref, wrg_ref, brg_ref, wig_ref, big_ref, lam_ref,
             dxr_ref, dg_ref, dcw_ref, dcb_ref, dwrg_ref, dbrg_ref, dwig_ref, dbig_ref, dlam_ref, a_s, b_s, l_s):
        first = pl.program_id(1) == 0
        rowi = lax.broadcasted_iota(jnp.int32, (S, LANES), 0)
        xr = xr_ref[...]
        xc, r, ig, sp, a, mult = _lru_gates(xr, cw_ref, cb_ref, wrg_ref, brg_ref, wig_ref, big_ref, lam_ref, rowi)
        h = h_ref[...]
        dy = dy_ref[...]
        gel, dgel = _gelu_parts(g_ref[...])
        dg_ref[...] = (dy * h * dgel).astype(BF16)
        al, bl = _scan_local(_shift_up(a, 1, rowi, S), dy * gel, rowi, S, True)
        a_s[...] = al
        b_s[...] = bl
        _scan_carry(a_s, b_s, l_s, S, True)
        lamb = l_s[...]
        u = ig * xc
        da = lamb * _shift_down(h, 1, rowi)
        dlog_a = da * a - (lamb * u) * (a * a) / mult
        du = lamb * mult
        dpre_r = (dlog_a * ((-LRU_C) * sp)) * r * (1.0 - r)
        dpre_i = (du * xc) * ig * (1.0 - ig)
        dsp = jnp.sum(dlog_a * ((-LRU_C) * r), axis=0, keepdims=True)
        dlam = dsp * (-1.0 / (1.0 + jnp.exp(lam_ref[...])))
        dpr = dpre_r.astype(BF16)
        dpi = dpre_i.astype(BF16)
        dxc = du * ig + _dot_nt(dpr, wrg_ref[...]) + _dot_nt(dpi, wig_ref[...])
        xcb = xc.astype(BF16)
        dwrg = _dot_tn(xcb, dpr)
        dwig = _dot_tn(xcb, dpi)
        dxr = cw_ref[3:4, :] * dxc
        dcw = [jnp.sum(xr * dxc, axis=0, keepdims=True)]
        for d in range(1, CONV_WIDTH):
            dxr = dxr + cw_ref[3 - d:4 - d, :] * _shift_up(dxc, d, rowi, S)
            dcw.append(jnp.sum(_shift_down(xr, d, rowi) * dxc, axis=0, keepdims=True))
        dxr_ref[...] = dxr.astype(BF16)
        dcw_rows = jnp.concatenate(dcw[::-1], axis=0)
        sums = ((dcw_ref, dcw_rows), (dcb_ref, jnp.sum(dxc, axis=0, keepdims=True)), (dwrg_ref, dwrg),
                (dbrg_ref, jnp.sum(dpre_r, axis=0, keepdims=True)), (dwig_ref, dwig),
                (dbig_ref, jnp.sum(dpre_i, axis=0, keepdims=True)), (dlam_ref, dlam))

        @pl.when(first)
        def _():
            for ref, val in sums:
                ref[...] = val

        @pl.when(jnp.logical_not(first))
        def _():
            for ref, val in sums:
                ref[...] += val

    out_seq = pl.BlockSpec((None, S, LANES), lambda ct, b: (b, 0, ct))
    f = lambda shape: jax.ShapeDtypeStruct(shape, F32)
    nb = D // LANES
    return _pcall(
        body, name="lru_bwd", grid=(nct, B),
        in_specs=[seq(x_off), seq(g_off), out_seq, out_seq, row(CONV_WIDTH), row(1), wbd, row(1), wbd, row(1), row(1)],
        out_specs=[out_seq, out_seq, row(CONV_WIDTH), row(1), wbd, row(1), wbd, row(1), row(1)],
        out_shape=[jax.ShapeDtypeStruct((B, S, D), BF16), jax.ShapeDtypeStruct((B, S, D), BF16),
                   f((CONV_WIDTH, D)), f((1, D)), f((nb, LANES, LANES)), f((1, D)), f((nb, LANES, LANES)), f((1, D)), f((1, D))],
        scratch_shapes=[pltpu.VMEM((S, LANES), F32)] * 3,
        compiler_params=_params("parallel", "arbitrary"),
    )(proj3, proj3, h3, dya3, lp["conv_w"], lp["conv_b"], lp["w_rg_bd"], lp["b_rg"], lp["w_ig_bd"], lp["b_ig"], lp["lam"])


def _pair_stack(x, lo):
    z = jnp.zeros_like(x)
    return jnp.concatenate([jnp.where(lo, x, z), jnp.where(lo, z, x)], axis=0).astype(BF16)


def _pair_join(y2, lo):
    return jnp.where(lo, y2[:WIN], y2[WIN:])


def _pair_col(xb):
    return jnp.concatenate([xb[:, 0:1], xb[:, HEAD_DIM:HEAD_DIM + 1]], axis=0)


def _pair_bcast(col, lo):
    return jnp.where(lo, jnp.broadcast_to(col[:WIN], (WIN, LANES)), jnp.broadcast_to(col[WIN:], (WIN, LANES)))


def _dil_rows(it, d, S):
    if d == 1:
        cur = pl.multiple_of(it * WIN, WIN)
        prev = pl.multiple_of(jnp.maximum(it - 1, 0) * WIN, WIN)
        return pl.ds(cur, WIN), pl.ds(prev, WIN), it > 0
    r, i = it % d, it // d
    cur = i * (WIN * d) + r
    prev = jnp.maximum(i - 1, 0) * (WIN * d) + r
    return pl.ds(cur, WIN, stride=d), pl.ds(prev, WIN, stride=d), i > 0


def _dil_bias(two_blocks, stack=2):
    nk = 2 * WIN if two_blocks else WIN
    qi = lax.broadcasted_iota(jnp.int32, (stack * WIN, nk), 0) & (WIN - 1)
    kj = lax.broadcasted_iota(jnp.int32, (stack * WIN, nk), 1)
    if not two_blocks:
        return jnp.where(kj <= qi, 0.0, NEG_INF), None
    cur = jnp.logical_and(kj >= WIN, kj - WIN <= qi)
    prev = jnp.logical_and(kj < WIN, kj >= qi)
    return jnp.where(jnp.logical_or(cur, prev), 0.0, NEG_INF), jnp.where(cur, 0.0, NEG_INF)


def _dil_specs(B, S, D, C, offs):
    grid = (B, D // LANES)
    seq = lambda off: pl.BlockSpec((None, S, LANES), lambda b, p: (b, 0, off // LANES + p))
    return grid, [seq(o) for o in offs], seq(0)


def _call_with_rider(rider, body, *, name, grid, in_specs, out_specs, out_shape, scratch_shapes, args, semantics=None):
    if rider is None:
        return _pcall(body, name=name, grid=grid, in_specs=in_specs, out_specs=out_specs, out_shape=out_shape,
                      scratch_shapes=scratch_shapes, compiler_params=_params(*(semantics or ("parallel",) * len(grid))))(*args)
    n_in, n_out, n_sc = len(in_specs), len(out_specs), len(scratch_shapes)
    r_in, r_out = len(rider["ins"]), len(rider["out_shapes"])

    def wrapped(*refs):
        p = 0
        own_in = refs[p:p + n_in]; p += n_in
        rid_in = refs[p:p + r_in]; p += r_in
        own_out = refs[p:p + n_out]; p += n_out
        rid_out = refs[p:p + r_out]; p += r_out
        own_sc = refs[p:p + n_sc]; p += n_sc
        send_sems, recv_sems = refs[p:p + 2]
        ids = [pl.program_id(a) for a in range(len(grid))]
        first = ids[0] == 0
        last = ids[0] == grid[0] - 1
        for a in range(1, len(grid)):
            first = jnp.logical_and(first, ids[a] == 0)
            last = jnp.logical_and(last, ids[a] == grid[a] - 1)

        @pl.when(first)
        def _():
            rider["start"](rid_in, rid_out, send_sems, recv_sems)

        body(*own_in, *own_out, *own_sc)

        @pl.when(last)
        def _():
            rider["finish"](rid_in, rid_out, send_sems, recv_sems)

    aliases = {n_in + t: n_out + t for t in range(r_in)} if rider.get("in_place") else {}
    res = _pcall_comm(
        wrapped, name=name + "_" + rider["name"], grid=grid, in_specs=list(in_specs) + [ANY] * r_in,
        out_specs=list(out_specs) + [ANY] * r_out, out_shape=list(out_shape) + list(rider["out_shapes"]),
        scratch_shapes=list(scratch_shapes) + [pltpu.SemaphoreType.DMA((rider["n"],)), pltpu.SemaphoreType.DMA((rider["n"],))],
        input_output_aliases=aliases, compiler_params=_params(*(("arbitrary",) * len(grid))),
    )(*args, *rider["ins"])
    return res


def _dil_fwd(proj3, *, D, q_off, k_off, v_off, rider=None):
    B, S, C = proj3.shape
    n_it = S // WIN
    scale = HEAD_DIM ** -0.5
    grid, in_specs, out_spec = _dil_specs(B, S, D, C, (q_off, k_off, v_off))

    def body(q_ref, k_ref, v_ref, o_ref, l_ref):
        lo = lax.broadcasted_iota(jnp.int32, (WIN, LANES), 1) < HEAD_DIM
        for c, d in enumerate(DILS):
            two = S // d > WIN
            bias_all, bias_first = _dil_bias(two)

            def step(it, _, c=c, d=d, two=two, bias_all=bias_all, bias_first=bias_first):
                cur, prev, later = _dil_rows(it, d, S)
                q2 = _pair_stack(q_ref[cur, :], lo)
                if two:
                    k2 = jnp.concatenate([k_ref[prev, :], k_ref[cur, :]], axis=0).astype(BF16)
                    v2 = jnp.concatenate([v_ref[prev, :], v_ref[cur, :]], axis=0).astype(BF16)
                    bias = jnp.where(later, bias_all, bias_first)
                else:
                    k2, v2, bias = k_ref[cur, :].astype(BF16), v_ref[cur, :].astype(BF16), bias_all
                s2 = _dot_nt(q2, k2) * scale + bias
                m2 = jnp.max(s2, axis=1, keepdims=True)
                p2 = jnp.exp(s2 - m2)
                den = jnp.sum(p2, axis=1, keepdims=True)
                oc = _pair_join(_dot_nn(p2.astype(BF16), v2) / den, lo)
                lc = _pair_bcast(m2 + jnp.log(den), lo)
                if c == 0:
                    o_ref[cur, :] = oc
                    l_ref[cur, :] = lc
                else:
                    l_old = l_ref[cur, :]
                    mx = jnp.maximum(l_old, lc)
                    e_old, e_new = jnp.exp(l_old - mx), jnp.exp(lc - mx)
                    tot = e_old + e_new
                    o_ref[cur, :] = (e_old * o_ref[cur, :] + e_new * oc) / tot
                    l_ref[cur, :] = mx + jnp.log(tot)
                return 0

            lax.fori_loop(0, n_it, step, 0, unroll=4)

    return _call_with_rider(
        rider, body, name="dil_fwd", grid=grid, in_specs=in_specs, out_specs=[out_spec, out_spec],
        out_shape=[jax.ShapeDtypeStruct((B, S, D), F32)] * 2, scratch_shapes=[], args=[proj3, proj3, proj3])


def _dil_bwd(proj3, o3, l3, do3, *, D, q_off, k_off, v_off, rider=None):
    B, S, C = proj3.shape
    n_it = S // WIN
    scale = HEAD_DIM ** -0.5
    grid, in_specs, out_spec = _dil_specs(B, S, D, C, (q_off, k_off, v_off))

    def body(q_ref, k_ref, v_ref, o_ref, l_ref, do_ref, dq_ref, dk_ref, dv_ref, dd_s, dq_s, dk_s, dv_s):
        lo = lax.broadcasted_iota(jnp.int32, (WIN, LANES), 1) < HEAD_DIM
        lo_s = lax.broadcasted_iota(jnp.int32, (S, LANES), 1) < HEAD_DIM
        prod = do_ref[...] * o_ref[...]
        d_lo = jnp.sum(jnp.where(lo_s, prod, 0.0), axis=1, keepdims=True)
        d_hi = jnp.sum(jnp.where(lo_s, 0.0, prod), axis=1, keepdims=True)
        dd_s[...] = jnp.where(lo_s, jnp.broadcast_to(d_lo, (S, LANES)), jnp.broadcast_to(d_hi, (S, LANES)))
        dq_s[...] = jnp.zeros_like(dq_s)
        dk_s[...] = jnp.zeros_like(dk_s)
        dv_s[...] = jnp.zeros_like(dv_s)
        for d in DILS:
            two = S // d > WIN
            bias_all, bias_first = _dil_bias(two)

            def step(it, _, d=d, two=two, bias_all=bias_all, bias_first=bias_first):
                cur, prev, later = _dil_rows(it, d, S)
                q2 = _pair_stack(q_ref[cur, :], lo)
                do2 = _pair_stack(do_ref[cur, :], lo)
                l2 = _pair_col(l_ref[cur, :])
                dd2 = _pair_col(dd_s[cur, :])
                if two:
                    k2 = jnp.concatenate([k_ref[prev, :], k_ref[cur, :]], axis=0).astype(BF16)
                    v2 = jnp.concatenate([v_ref[prev, :], v_ref[cur, :]], axis=0).astype(BF16)
                    bias = jnp.where(later, bias_all, bias_first)
                else:
                    k2, v2, bias = k_ref[cur, :].astype(BF16), v_ref[cur, :].astype(BF16), bias_all
                p2 = jnp.exp(_dot_nt(q2, k2) * scale + bias - l2)
                ds2 = (p2 * (_dot_nt(do2, v2) - dd2) * scale).astype(BF16)
                dq_s[cur, :] += _pair_join(_dot_nn(ds2, k2), lo)
                dk2 = _dot_tn(ds2, q2)
                dv2 = _dot_tn(p2.astype(BF16), do2)
                if two:
                    dk_s[prev, :] += dk2[:WIN]
                    dv_s[prev, :] += dv2[:WIN]
                    dk_s[cur, :] += dk2[WIN:]
                    dv_s[cur, :] += dv2[WIN:]
                else:
                    dk_s[cur, :] += dk2
                    dv_s[cur, :] += dv2
                return 0

            lax.fori_loop(0, n_it, step, 0, unroll=4)
        dq_ref[...] = dq_s[...].astype(BF16)
        dk_ref[...] = dk_s[...].astype(BF16)
        dv_ref[...] = dv_s[...].astype(BF16)

    return _call_with_rider(
        rider, body, name="dil_bwd", grid=grid, in_specs=in_specs + [out_spec] * 3, out_specs=[out_spec] * 3,
        out_shape=[jax.ShapeDtypeStruct((B, S, D), BF16)] * 3, scratch_shapes=[pltpu.VMEM((S, LANES), F32)] * 4,
        args=[proj3, proj3, proj3, o3, l3, do3])


SWA_HB = 2 * SWA_GROUP


def _to_half(x, src, dst, lo):
    if src != dst:
        x = pltpu.roll(x, HEAD_DIM, 1)
    return jnp.where(lo if dst == 0 else jnp.logical_not(lo), x, 0.0)


def _swa_specs(B, S, D, C, q_off, k_off, v_off, n_steps):
    qw = SWA_HB * HEAD_DIM
    last = S // WIN - 1

    def blk(width, off, back):
        def imap(b, hh, i):
            ii = jnp.minimum(i, last)
            if back == 1:
                ii = jnp.maximum(ii - 1, 0)
            elif back == 2:
                ii = jnp.maximum(i - 1, 0)
            return (b, ii, off // width + hh)
        return pl.BlockSpec((None, WIN, width), imap)

    assert q_off % qw == 0 and k_off % LANES == 0 and v_off % LANES == 0 and D % qw == 0
    grid = (B, D // qw, n_steps)
    sink = pl.BlockSpec((None, 1, LANES), lambda b, hh, i: (hh, 0, 0))
    return grid, blk, sink, qw


def _swa_kv(g):
    return 2 * g // SWA_GROUP


SWA_STACKS = ((0, 1), (2, 3))


def _swa_stack(ref, gs, lo, dtype):
    parts = []
    for g in gs:
        x = ref[:, g * LANES:(g + 1) * LANES]
        parts += [_to_half(x, 0, _swa_kv(g), lo), _to_half(x, 1, _swa_kv(g), lo)]
    return jnp.concatenate(parts, axis=0).astype(dtype)


def _swa_unstack(y, gs, lo):
    out = []
    for t, g in enumerate(gs):
        even, odd = y[2 * t * WIN:(2 * t + 1) * WIN], y[(2 * t + 1) * WIN:(2 * t + 2) * WIN]
        out.append(_to_half(even, _swa_kv(g), 0, lo) + _to_half(odd, _swa_kv(g), 1, lo))
    return out


def _swa_cols(x, gs):
    cols = []
    for g in gs:
        cols += [jnp.broadcast_to(x[:, 2 * g:2 * g + 1], (WIN, 1)), jnp.broadcast_to(x[:, 2 * g + 1:2 * g + 2], (WIN, 1))]
    return jnp.concatenate(cols, axis=0)


def _swa_fwd(proj3, sinks, *, D, q_off, k_off, v_off, rider=None):
    B, S, C = proj3.shape
    scale = HEAD_DIM ** -0.5
    grid, blk, sink, qw = _swa_specs(B, S, D, C, q_off, k_off, v_off, S // WIN)
    nhb = D // qw

    def body(q_ref, kp_ref, kc_ref, vp_ref, vc_ref, sk_ref, o_ref, lse_ref):
        lo = lax.broadcasted_iota(jnp.int32, (WIN, LANES), 1) < HEAD_DIM
        lane = lax.broadcasted_iota(jnp.int32, (WIN, LANES), 1)
        k2 = jnp.concatenate([kp_ref[...], kc_ref[...]], axis=0).astype(BF16)
        v2 = jnp.concatenate([vp_ref[...], vc_ref[...]], axis=0).astype(BF16)
        sk = sk_ref[...]
        lse_acc = jnp.zeros((WIN, LANES), F32)
        for gs in SWA_STACKS:
            bias_all, bias_first = _dil_bias(True, 2 * len(gs))
            bias = jnp.where(pl.program_id(2) > 0, bias_all, bias_first)
            qs = _swa_stack(q_ref, gs, lo, BF16)
            sks = _swa_cols(sk, gs)
            s = _dot_nt(qs, k2) * scale + bias
            m = jnp.maximum(jnp.max(s, axis=1, keepdims=True), sks)
            p = jnp.exp(s - m)
            den = jnp.sum(p, axis=1, keepdims=True) + jnp.exp(sks - m)
            for g, grp in zip(gs, _swa_unstack(_dot_nn(p.astype(BF16), v2) / den, gs, lo)):
                o_ref[:, g * LANES:(g + 1) * LANES] = grp
            ls = m + jnp.log(den)
            for t, g in enumerate(gs):
                lse_acc = jnp.where(lane == 2 * g, ls[2 * t * WIN:(2 * t + 1) * WIN], lse_acc)
                lse_acc = jnp.where(lane == 2 * g + 1, ls[(2 * t + 1) * WIN:(2 * t + 2) * WIN], lse_acc)
        lse_ref[...] = lse_acc

    return _call_with_rider(
        rider, body, name="swa_fwd", grid=grid,
        in_specs=[blk(qw, q_off, 0), blk(LANES, k_off, 1), blk(LANES, k_off, 0), blk(LANES, v_off, 1), blk(LANES, v_off, 0), sink],
        out_specs=[blk(qw, 0, 0), blk(LANES, 0, 0)],
        out_shape=[jax.ShapeDtypeStruct((B, S, D), F32), jax.ShapeDtypeStruct((B, S, nhb * LANES), F32)],
        scratch_shapes=[], args=[proj3, proj3, proj3, proj3, proj3, sinks])


def _swa_bwd(proj3, o3, lse3, do3, sinks, *, D, q_off, k_off, v_off):
    B, S, C = proj3.shape
    nblk = S // WIN
    scale = HEAD_DIM ** -0.5
    grid, blk, sink, qw = _swa_specs(B, S, D, C, q_off, k_off, v_off, nblk + 1)
    nhb = D // qw
    KV = D // SWA_GROUP

    def body(q_ref, kp_ref, kc_ref, vp_ref, vc_ref, o_ref, l_ref, do_ref, sk_ref, dq_ref, dk_ref, dv_ref, dsk_ref, ck_ref, cv_ref):
        i = pl.program_id(2)

        @pl.when(i == 0)
        def _():
            ck_ref[...] = jnp.zeros_like(ck_ref)
            cv_ref[...] = jnp.zeros_like(cv_ref)
            dsk_ref[...] = jnp.zeros_like(dsk_ref)

        @pl.when(i < nblk)
        def _():
            lo = lax.broadcasted_iota(jnp.int32, (WIN, LANES), 1) < HEAD_DIM
            lane = lax.broadcasted_iota(jnp.int32, (1, LANES), 1)
            k2 = jnp.concatenate([kp_ref[...], kc_ref[...]], axis=0).astype(BF16)
            v2 = jnp.concatenate([vp_ref[...], vc_ref[...]], axis=0).astype(BF16)
            lse = l_ref[...]
            dk2 = jnp.zeros((2 * WIN, LANES), F32)
            dv2 = jnp.zeros((2 * WIN, LANES), F32)
            dsk_acc = jnp.zeros((1, LANES), F32)
            for gs in SWA_STACKS:
                bias_all, bias_first = _dil_bias(True, 2 * len(gs))
                bias = jnp.where(i > 0, bias_all, bias_first)
                qs = _swa_stack(q_ref, gs, lo, BF16)
                dos = _swa_stack(do_ref, gs, lo, BF16)
                dds = []
                for g in gs:
                    prod = do_ref[:, g * LANES:(g + 1) * LANES] * o_ref[:, g * LANES:(g + 1) * LANES]
                    dds += [jnp.sum(jnp.where(lo, prod, 0.0), axis=1, keepdims=True),
                            jnp.sum(jnp.where(lo, 0.0, prod), axis=1, keepdims=True)]
                dds = jnp.concatenate(dds, axis=0)
                ls = _swa_cols(lse, gs)
                ps = jnp.exp(_dot_nt(qs, k2) * scale + bias - ls)
                dss = (ps * (_dot_nt(dos, v2) - dds) * scale).astype(BF16)
                for g, grp in zip(gs, _swa_unstack(_dot_nn(dss, k2), gs, lo)):
                    dq_ref[:, g * LANES:(g + 1) * LANES] = grp.astype(BF16)
                dk2 = dk2 + _dot_tn(dss, qs)
                dv2 = dv2 + _dot_tn(ps.astype(BF16), dos)
                dsks = jnp.exp(_swa_cols(sk_ref[...], gs) - ls) * dds
                for t, g in enumerate(gs):
                    for u in range(2):
                        rows = slice((2 * t + u) * WIN, (2 * t + u + 1) * WIN)
                        dsk_acc = jnp.where(lane == 2 * g + u, -jnp.sum(dsks[rows], axis=0, keepdims=True), dsk_acc)
            dk_ref[...] = (dk2[:WIN] + ck_ref[...]).astype(BF16)
            dv_ref[...] = (dv2[:WIN] + cv_ref[...]).astype(BF16)
            ck_ref[...] = dk2[WIN:]
            cv_ref[...] = dv2[WIN:]
            dsk_ref[...] += dsk_acc

        @pl.when(i == nblk)
        def _():
            dk_ref[...] = ck_ref[...].astype(BF16)
            dv_ref[...] = cv_ref[...].astype(BF16)

    res = _pcall(
        body, name="swa_bwd", grid=grid,
        in_specs=[blk(qw, q_off, 0), blk(LANES, k_off, 1), blk(LANES, k_off, 0), blk(LANES, v_off, 1), blk(LANES, v_off, 0),
                  blk(qw, 0, 0), blk(LANES, 0, 0), blk(qw, 0, 0), sink],
        out_specs=[blk(qw, 0, 0), blk(LANES, 0, 2), blk(LANES, 0, 2),
                   pl.BlockSpec((None, None, 1, LANES), lambda b, hh, i: (b, hh, 0, 0))],
        out_shape=[jax.ShapeDtypeStruct((B, S, D), BF16), jax.ShapeDtypeStruct((B, S, KV), BF16),
                   jax.ShapeDtypeStruct((B, S, KV), BF16), jax.ShapeDtypeStruct((B, nhb, 1, LANES), F32)],
        scratch_shapes=[pltpu.VMEM((WIN, LANES), F32), pltpu.VMEM((WIN, LANES), F32)],
        compiler_params=_params("parallel", "parallel", "arbitrary"),
    )(proj3, proj3, proj3, proj3, proj3, o3, lse3, do3, sinks)
    return res


def _branch_fwd(ys, wb, proj, *, D, g_off, rider=None):
    T = proj.shape[0]
    tm, tn = _tile(T, 256), _tile(D, 512)
    n = len(ys)

    def body(*refs):
        y_refs, w_ref, g_refs, br_ref, mg_ref = refs[:n], refs[n], refs[n + 1:2 * n + 1], refs[2 * n + 1], refs[2 * n + 2]
        acc = None
        for k in range(n):
            br = _dot_nn(y_refs[k][...].astype(BF16), w_ref[k])
            br_ref[k] = br
            term = _sigmoid(g_refs[k][...]) * br
            acc = term if acc is None else acc + term
        mg_ref[...] = acc.astype(BF16)

    gate = lambda k: pl.BlockSpec((tm, tn), lambda i, j: (i, (g_off + k * D) // tn + j))
    return _call_with_rider(
        rider, body, name="branch_fwd", grid=(T // tm, D // tn),
        in_specs=[pl.BlockSpec((tm, D), lambda i, j: (i, 0))] * n + [pl.BlockSpec((n, D, tn), lambda i, j: (0, 0, j))]
        + [gate(k) for k in range(n)],
        out_specs=[pl.BlockSpec((n, tm, tn), lambda i, j: (0, i, j)), pl.BlockSpec((tm, tn), lambda i, j: (i, j))],
        out_shape=[jax.ShapeDtypeStruct((n, T, D), F32), jax.ShapeDtypeStruct((T, D), BF16)],
        scratch_shapes=[], args=[*ys, wb, *([proj] * n)])


def _branch_bwd(dmerged, branch, proj, *, D, g_off):
    n, T, _ = branch.shape
    tm, tn = _tile(T, 512), _tile(D, 512)

    def body(dm_ref, br_ref, *rest):
        g_refs, db_ref, dg_refs = rest[:n], rest[n], rest[n + 1:]
        dm = dm_ref[...]
        for k in range(n):
            sg = _sigmoid(g_refs[k][...])
            db_ref[k] = (sg * dm).astype(BF16)
            dg_refs[k][...] = (dm * br_ref[k] * sg * (1.0 - sg)).astype(BF16)

    gate = lambda k: pl.BlockSpec((tm, tn), lambda i, j: (i, (g_off + k * D) // tn + j))
    blk = pl.BlockSpec((tm, tn), lambda i, j: (i, j))
    res = _pcall(
        body, name="branch_bwd", grid=(T // tm, D // tn),
        in_specs=[blk, pl.BlockSpec((n, tm, tn), lambda i, j: (0, i, j))] + [gate(k) for k in range(n)],
        out_specs=[pl.BlockSpec((n, tm, tn), lambda i, j: (0, i, j))] + [blk] * n,
        out_shape=[jax.ShapeDtypeStruct((n, T, D), BF16)] + [jax.ShapeDtypeStruct((T, D), BF16)] * n,
        compiler_params=_params("parallel", "parallel"),
    )(dmerged, branch, *([proj] * n))
    return res[0], list(res[1:])


def _ln_bwd(dout, z, g):
    T, D = z.shape
    tm = _tile(T, 512)

    def body(do_ref, z_ref, g_ref, dz_ref, dg_ref, db_ref):
        z = z_ref[...]
        do = do_ref[...]
        mu = jnp.mean(z, axis=1, keepdims=True)
        zc = z - mu
        rstd = lax.rsqrt(jnp.mean(zc * zc, axis=1, keepdims=True) + LN_EPS)
        xhat = zc * rstd
        dxh = do * g_ref[...]
        dz_ref[...] = rstd * (dxh - jnp.mean(dxh, axis=1, keepdims=True) - xhat * jnp.mean(dxh * xhat, axis=1, keepdims=True))
        dg = jnp.sum(do * xhat, axis=0, keepdims=True)
        db = jnp.sum(do, axis=0, keepdims=True)
        first = pl.program_id(0) == 0

        @pl.when(first)
        def _():
            dg_ref[...] = dg
            db_ref[...] = db

        @pl.when(jnp.logical_not(first))
        def _():
            dg_ref[...] += dg
            db_ref[...] += db

    blk = pl.BlockSpec((tm, D), lambda i: (i, 0))
    vec = pl.BlockSpec((1, D), lambda i: (0, 0))
    return _pcall(
        body, name="ln_bwd", grid=(T // tm,), in_specs=[blk, blk, vec], out_specs=[blk, vec, vec],
        out_shape=[jax.ShapeDtypeStruct((T, D), F32), jax.ShapeDtypeStruct((1, D), F32), jax.ShapeDtypeStruct((1, D), F32)],
        compiler_params=_params("arbitrary"),
    )(dout, z, g)


def _swiglu_fwd(hh):
    T, F2 = hh.shape
    Fh = F2 // 2
    tm, tn = _tile(T, 256), _tile(Fh, 1408)
    nj = Fh // tn

    def body(h1_ref, h3_ref, f_ref):
        h1 = h1_ref[...]
        f_ref[...] = (h1 * _sigmoid(h1) * h3_ref[...]).astype(BF16)

    return _pcall(
        body, name="swiglu_fwd", grid=(T // tm, nj),
        in_specs=[pl.BlockSpec((tm, tn), lambda i, j: (i, j)), pl.BlockSpec((tm, tn), lambda i, j: (i, nj + j))],
        out_specs=pl.BlockSpec((tm, tn), lambda i, j: (i, j)),
        out_shape=jax.ShapeDtypeStruct((T, Fh), BF16), compiler_params=_params("parallel", "parallel"),
    )(hh, hh)


def _swiglu_bwd(hh, df, rider=None):
    T, F2 = hh.shape
    Fh = F2 // 2
    tm, tn = _tile(T, 256), _tile(Fh, 1408)
    nj = Fh // tn

    def body(h1_ref, h3_ref, df_ref, d1_ref, d3_ref):
        h1 = h1_ref[...]
        sg = _sigmoid(h1)
        d = df_ref[...]
        d1_ref[...] = (d * h3_ref[...] * sg * (1.0 + h1 * (1.0 - sg))).astype(BF16)
        d3_ref[...] = (d * h1 * sg).astype(BF16)

    lo = pl.BlockSpec((tm, tn), lambda i, j: (i, j))
    hi = pl.BlockSpec((tm, tn), lambda i, j: (i, nj + j))
    res = _call_with_rider(
        rider, body, name="swiglu_bwd", grid=(T // tm, nj), in_specs=[lo, hi, lo], out_specs=[lo, lo],
        out_shape=[jax.ShapeDtypeStruct((T, Fh), BF16)] * 2, scratch_shapes=[], args=[hh, hh, df])
    return jnp.concatenate([res[0], res[1]], axis=1), list(res[2:])


def _loss_head(y, target):
    T, D = y.shape
    tm = _tile(T, 512)

    def body(y_ref, t_ref, dy_ref, l_ref):
        e = y_ref[...] - t_ref[...]
        dy_ref[...] = e * (1.0 / D)
        sq = e * e
        part = sq[:, 0:LANES]
        for c in range(1, D // LANES):
            part = part + sq[:, c * LANES:(c + 1) * LANES]
        part = jnp.sum(part, axis=0, keepdims=True) * (0.5 / D)
        first = pl.program_id(0) == 0

        @pl.when(first)
        def _():
            l_ref[...] = part

        @pl.when(jnp.logical_not(first))
        def _():
            l_ref[...] += part

    blk = pl.BlockSpec((tm, D), lambda i: (i, 0))
    return _pcall(
        body, name="loss_head", grid=(T // tm,), in_specs=[blk, blk],
        out_specs=[blk, pl.BlockSpec((1, LANES), lambda i: (0, 0))],
        out_shape=[jax.ShapeDtypeStruct((T, D), F32), jax.ShapeDtypeStruct((1, LANES), F32)],
        compiler_params=_params("arbitrary"),
    )(y, target)


def _as_rows(a):
    return a.reshape(-1, a.shape[-1])


def _adamw(w, g, m, v, rider=None):
    w2, g2, m2, v2 = (_as_rows(t) for t in (w, g, m, v))
    R, Cc = w2.shape
    cap = max(SUBLANES, min(512, (256 * 1024) // Cc))
    tm = R if (R <= cap or R % SUBLANES) else max(t for t in range(SUBLANES, cap + 1, SUBLANES) if R % t == 0)
    c1 = 1.0 - ADAM_B1 ** ADAM_STEP
    c2 = 1.0 - ADAM_B2 ** ADAM_STEP

    def body(w_ref, g_ref, m_ref, v_ref, d_ref, nm_ref, nv_ref):
        gg = g_ref[...]
        nm = ADAM_B1 * m_ref[...] + (1.0 - ADAM_B1) * gg
        nv = ADAM_B2 * v_ref[...] + (1.0 - ADAM_B2) * (gg * gg)
        d_ref[...] = (-ADAM_LR) * ((nm / c1) / (jnp.sqrt(nv / c2) + ADAM_EPS) + ADAM_WD * w_ref[...])
        nm_ref[...] = nm
        nv_ref[...] = nv

    blk = pl.BlockSpec((tm, Cc), lambda i: (i, 0))
    res = _call_with_rider(
        rider, body, name="adamw", grid=(R // tm,), in_specs=[blk] * 4, out_specs=[blk] * 3,
        out_shape=[jax.ShapeDtypeStruct((R, Cc), F32)] * 3, scratch_shapes=[], args=[w2, g2, m2, v2])
    return tuple(t.reshape(w.shape) for t in res[:3]) + tuple(res[3:])


def _where_am_i():
    x, y, c = lax.axis_index("x"), lax.axis_index("y"), lax.axis_index("c")
    chips = [(1 - x, y), (x, 1 - y), (1 - x, 1 - y)]
    return x, y, c, chips


def _remote(src, dst, send_sems, recv_sems, k, to):
    return pltpu.make_async_remote_copy(src_ref=src, dst_ref=dst, send_sem=send_sems.at[k], recv_sem=recv_sems.at[k],
                                        device_id=to, device_id_type=MESH)


def _comm_call(body, name, ins, out_shapes, n_remote, n_local):
    return _pcall_comm(
        body, name=name, in_specs=[ANY] * len(ins), out_specs=[ANY] * len(out_shapes), out_shape=out_shapes,
        scratch_shapes=[pltpu.SemaphoreType.DMA((n_remote,)), pltpu.SemaphoreType.DMA((n_remote,)),
                        pltpu.SemaphoreType.DMA((max(n_local, 1),))],
    )(*ins)


def _gather_weights(shards):
    n = len(shards)

    def body(*refs):
        ins, outs = refs[:n], refs[n:2 * n]
        send_sems, recv_sems, local_sems = refs[2 * n:]
        x, y, c, chips = _where_am_i()
        s = 2 * x + y
        sib = (x, y, 1 - c)
        first = []
        for t in range(n):
            for j, (cx, cy) in enumerate(chips):
                first.append(_remote(ins[t].at[:, c], outs[t].at[:, s, c], send_sems, recv_sems, 6 * t + j, (cx, cy, c)))
        for cp in first:
            cp.start()
        passed = []
        for j, (cx, cy) in enumerate(chips):
            sj = 2 * cx + cy
            for t in range(n):
                land = outs[t].at[:, sj, c]
                _remote(land, land, send_sems, recv_sems, 6 * t + j, (cx, cy, c)).wait_recv()
                fw = _remote(land, land, send_sems, recv_sems, 6 * t + 3 + j, sib)
                fw.start()
                passed.append(fw)
        for j, (cx, cy) in enumerate(chips):
            sj = 2 * cx + cy
            for t in range(n):
                land = outs[t].at[:, sj, 1 - c]
                _remote(land, land, send_sems, recv_sems, 6 * t + 3 + j, sib).wait_recv()
        for cp in first + passed:
            cp.wait_send()

    out_shapes = [jax.ShapeDtypeStruct((t.shape[0], N_CHIPS) + t.shape[1:], t.dtype) for t in shards]
    got = _comm_call(body, "gather_weights", shards, out_shapes, 6 * n, 0)
    s = 2 * lax.axis_index("x") + lax.axis_index("y")
    return [lax.dynamic_update_slice(g, t[:, None], (0, s, 0, 0, 0)) for g, t in zip(got, shards)]


def _gather_rider(shards):
    n = len(shards)

    def copies(ins, outs, send_sems, recv_sems):
        x, y, c, chips = _where_am_i()
        s = 2 * x + y
        return [_remote(ins[t].at[:, c], outs[t].at[:, s, c], send_sems, recv_sems, 3 * t + j, (cx, cy, c))
                for t in range(n) for j, (cx, cy) in enumerate(chips)]

    def start(ins, outs, send_sems, recv_sems):
        for cp in copies(ins, outs, send_sems, recv_sems):
            cp.start()

    def finish(ins, outs, send_sems, recv_sems):
        x, y, c, chips = _where_am_i()
        for t in range(n):
            for j, (cx, cy) in enumerate(chips):
                land = outs[t].at[:, 2 * cx + cy, c]
                _remote(land, land, send_sems, recv_sems, 3 * t + j, (cx, cy, c)).wait_recv()
        for cp in copies(ins, outs, send_sems, recv_sems):
            cp.wait_send()

    out_shapes = [jax.ShapeDtypeStruct((t.shape[0], N_CHIPS) + t.shape[1:], t.dtype) for t in shards]
    return dict(name="gather", ins=list(shards), out_shapes=out_shapes, n=3 * n, start=start, finish=finish)


def _gather_forward(landed, shards):
    n = len(landed)

    def body(*refs):
        outs = refs[n:2 * n]
        send_sems, recv_sems, _ = refs[2 * n:]
        x, y, c, chips = _where_am_i()
        sib = (x, y, 1 - c)
        cps = []
        for t in range(n):
            for j, (cx, cy) in enumerate(chips):
                land = outs[t].at[:, 2 * cx + cy, c]
                cps.append(_remote(land, land, send_sems, recv_sems, 3 * t + j, sib))
        for cp in cps:
            cp.start()
        for t in range(n):
            for j, (cx, cy) in enumerate(chips):
                land = outs[t].at[:, 2 * cx + cy, 1 - c]
                _remote(land, land, send_sems, recv_sems, 3 * t + j, sib).wait_recv()
        for cp in cps:
            cp.wait_send()

    got = _pcall_comm(
        body, name="gather_forward", in_specs=[ANY] * n, out_specs=[ANY] * n,
        out_shape=[jax.ShapeDtypeStruct(t.shape, t.dtype) for t in landed], input_output_aliases={t: t for t in range(n)},
        scratch_shapes=[pltpu.SemaphoreType.DMA((3 * n,)), pltpu.SemaphoreType.DMA((3 * n,)), pltpu.SemaphoreType.DMA((1,))],
    )(*landed)
    s = 2 * lax.axis_index("x") + lax.axis_index("y")
    return [lax.dynamic_update_slice(g, t[:, None], (0, s, 0, 0, 0)) for g, t in zip(got, shards)]


def _forward_rider(landed):
    n = len(landed)

    def copies(outs, send_sems, recv_sems):
        x, y, c, chips = _where_am_i()
        cps = []
        for t in range(n):
            for j, (cx, cy) in enumerate(chips):
                land = outs[t].at[:, 2 * cx + cy, c]
                cps.append(_remote(land, land, send_sems, recv_sems, 3 * t + j, (x, y, 1 - c)))
        return cps

    def start(ins, outs, send_sems, recv_sems):
        for cp in copies(outs, send_sems, recv_sems):
            cp.start()

    def finish(ins, outs, send_sems, recv_sems):
        x, y, c, chips = _where_am_i()
        for t in range(n):
            for j, (cx, cy) in enumerate(chips):
                land = outs[t].at[:, 2 * cx + cy, 1 - c]
                _remote(land, land, send_sems, recv_sems, 3 * t + j, (x, y, 1 - c)).wait_recv()
        for cp in copies(outs, send_sems, recv_sems):
            cp.wait_send()

    out_shapes = [jax.ShapeDtypeStruct(t.shape, t.dtype) for t in landed]
    return dict(name="forward", ins=list(landed), out_shapes=out_shapes, n=3 * n, start=start, finish=finish, in_place=True)


def _place_own_shard(got, shards):
    s = 2 * lax.axis_index("x") + lax.axis_index("y")
    return [lax.dynamic_update_slice(g, t[:, None], (0, s, 0, 0, 0)) for g, t in zip(got, shards)]


def _gather_small(v):
    def body(v_ref, out_ref, send_sems, recv_sems, local_sems):
        x, y, c, chips = _where_am_i()
        s = 2 * x + y
        mine = pltpu.make_async_copy(v_ref, out_ref.at[s], local_sems.at[0])
        mine.start()
        sends = [_remote(v_ref, out_ref.at[s], send_sems, recv_sems, j, (cx, cy, c)) for j, (cx, cy) in enumerate(chips)]
        for cp in sends:
            cp.start()
        for j, (cx, cy) in enumerate(chips):
            land = out_ref.at[2 * cx + cy]
            _remote(land, land, send_sems, recv_sems, j, (cx, cy, c)).wait_recv()
        for cp in sends:
            cp.wait_send()
        mine.wait()

    return _comm_call(body, "gather_small", [v], [jax.ShapeDtypeStruct((N_CHIPS,) + v.shape, v.dtype)], 3, 1)[0]


def _swap_sibling_halves(grads):
    n = len(grads)

    def body(*refs):
        ins, outs = refs[:n], refs[n:2 * n]
        send_sems, recv_sems, _ = refs[2 * n:]
        x, y, c, _chips = _where_am_i()
        sib = (x, y, 1 - c)
        cps = [_remote(ins[t].at[:, :, 1 - c], outs[t], send_sems, recv_sems, t, sib) for t in range(n)]
        for cp in cps:
            cp.start()
        for cp in cps:
            cp.wait()

    out_shapes = [jax.ShapeDtypeStruct(g.shape[:2] + g.shape[3:], g.dtype) for g in grads]
    return _comm_call(body, "grad_swap_halves", grads, out_shapes, n, 0)


def _exchange_chips(parts):
    n = len(parts)

    def body(*refs):
        ins, outs = refs[:n], refs[n:2 * n]
        send_sems, recv_sems, _ = refs[2 * n:]
        x, y, c, chips = _where_am_i()
        cps = []
        for t in range(n):
            for j, (cx, cy) in enumerate(chips):
                cps.append(_remote(ins[t].at[:, 2 * cx + cy], outs[t].at[j], send_sems, recv_sems, 3 * t + j, (cx, cy, c)))
        for cp in cps:
            cp.start()
        for cp in cps:
            cp.wait()

    out_shapes = [jax.ShapeDtypeStruct((3, p.shape[0]) + p.shape[2:], p.dtype) for p in parts]
    return _comm_call(body, "grad_exchange_chips", parts, out_shapes, 3 * n, 0)


def _exchange_rider(parts):
    n = len(parts)

    def copies(ins, outs, send_sems, recv_sems):
        x, y, c, chips = _where_am_i()
        return [_remote(ins[t].at[:, 2 * cx + cy], outs[t].at[j], send_sems, recv_sems, 3 * t + j, (cx, cy, c))
                for t in range(n) for j, (cx, cy) in enumerate(chips)]

    def start(ins, outs, send_sems, recv_sems):
        for cp in copies(ins, outs, send_sems, recv_sems):
            cp.start()

    def finish(ins, outs, send_sems, recv_sems):
        for cp in copies(ins, outs, send_sems, recv_sems):
            cp.wait()

    out_shapes = [jax.ShapeDtypeStruct((3, p.shape[0]) + p.shape[2:], p.dtype) for p in parts]
    return dict(name="exchange", ins=list(parts), out_shapes=out_shapes, n=3 * n, start=start, finish=finish)


def _join_sibling_halves(halves):
    n = len(halves)

    def body(*refs):
        ins, outs = refs[:n], refs[n:2 * n]
        send_sems, recv_sems, local_sems = refs[2 * n:]
        x, y, c, _chips = _where_am_i()
        sib = (x, y, 1 - c)
        cps = [_remote(ins[t], outs[t].at[:, c], send_sems, recv_sems, t, sib) for t in range(n)]
        for cp in cps:
            cp.start()
        for t in range(n):
            land = outs[t].at[:, 1 - c]
            _remote(land, land, send_sems, recv_sems, t, sib).wait_recv()
        for cp in cps:
            cp.wait_send()

    out_shapes = [jax.ShapeDtypeStruct((h.shape[0], 2) + h.shape[1:], h.dtype) for h in halves]
    got = _comm_call(body, "grad_join_halves", halves, out_shapes, n, 0)
    c = lax.axis_index("c")
    return [lax.dynamic_update_slice(g, h[:, None], (0, c, 0, 0)) for g, h in zip(got, halves)]


def _join_rider(halves):
    n = len(halves)

    def copies(ins, outs, send_sems, recv_sems):
        x, y, c, _chips = _where_am_i()
        return [_remote(ins[t], outs[t].at[:, c], send_sems, recv_sems, t, (x, y, 1 - c)) for t in range(n)]

    def start(ins, outs, send_sems, recv_sems):
        for cp in copies(ins, outs, send_sems, recv_sems):
            cp.start()

    def finish(ins, outs, send_sems, recv_sems):
        x, y, c, _chips = _where_am_i()
        for t in range(n):
            land = outs[t].at[:, 1 - c]
            _remote(land, land, send_sems, recv_sems, t, (x, y, 1 - c)).wait_recv()
        for cp in copies(ins, outs, send_sems, recv_sems):
            cp.wait_send()

    out_shapes = [jax.ShapeDtypeStruct((h.shape[0], 2) + h.shape[1:], h.dtype) for h in halves]
    return dict(name="join", ins=list(halves), out_shapes=out_shapes, n=n, start=start, finish=finish)


def _place_own_half(got, halves):
    c = lax.axis_index("c")
    return [lax.dynamic_update_slice(g, h[:, None], (0, c, 0, 0)) for g, h in zip(got, halves)]


def _small_exchange_rider(v):
    def copies(ins, outs, send_sems, recv_sems):
        x, y, c, chips = _where_am_i()
        return [_remote(ins[0], outs[0].at[j], send_sems, recv_sems, j, (cx, cy, c)) for j, (cx, cy) in enumerate(chips)]

    def start(ins, outs, send_sems, recv_sems):
        for cp in copies(ins, outs, send_sems, recv_sems):
            cp.start()

    def finish(ins, outs, send_sems, recv_sems):
        for cp in copies(ins, outs, send_sems, recv_sems):
            cp.wait()

    return dict(name="small_exchange", ins=[v], out_shapes=[jax.ShapeDtypeStruct((3,) + v.shape, v.dtype)], n=3,
                start=start, finish=finish)


def _swap_small(v):
    def body(v_ref, out_ref, send_sems, recv_sems, _):
        x, y, c, _chips = _where_am_i()
        cp = _remote(v_ref, out_ref, send_sems, recv_sems, 0, (x, y, 1 - c))
        cp.start()
        cp.wait()

    return _comm_call(body, "small_swap", [v], [jax.ShapeDtypeStruct(v.shape, v.dtype)], 1, 0)[0]


def _sum_rows(name, terms, out_dtypes):
    R, Cc = terms[0].shape
    tm = R if R <= 256 else max(t for t in range(16, 257, 16) if R % t == 0)
    n = len(terms)

    def body(*refs):
        acc = refs[0][...].astype(F32)
        for r in refs[1:n]:
            acc = acc + r[...].astype(F32)
        for o in refs[n:]:
            o[...] = acc.astype(o.dtype)

    blk = pl.BlockSpec((tm, Cc), lambda i: (i, 0))
    return _pcall(
        body, name=name, grid=(R // tm,), in_specs=[blk] * n, out_specs=[blk] * len(out_dtypes),
        out_shape=[jax.ShapeDtypeStruct((R, Cc), d) for d in out_dtypes], compiler_params=_params("parallel"),
    )(*terms)


def _pair_sum(g5, r1, core, shard):
    A4, _, Rh, Cc = g5.shape
    A = A4 // N_CHIPS
    tr = Rh if Rh <= 256 else max(t for t in range(16, 257, 16) if Rh % t == 0)

    def body(core_ref, shard_ref, g_ref, r_ref, qb_ref, qf_ref):
        q = g_ref[...] + r_ref[...]
        qb_ref[...] = q.astype(BF16)

        @pl.when(pl.program_id(2) == shard_ref[0])
        def _():
            qf_ref[...] = q

    grid_spec = pltpu.PrefetchScalarGridSpec(
        num_scalar_prefetch=2, grid=(A, Rh // tr, N_CHIPS),
        in_specs=[pl.BlockSpec((None, None, tr, Cc), lambda a, r, sh, core, shard: (a * N_CHIPS + sh, core[0], r, 0)),
                  pl.BlockSpec((None, tr, Cc), lambda a, r, sh, core, shard: (a * N_CHIPS + sh, r, 0))],
        out_specs=[pl.BlockSpec((None, tr, Cc), lambda a, r, sh, core, shard: (a * N_CHIPS + sh, r, 0)),
                   pl.BlockSpec((None, tr, Cc), lambda a, r, sh, core, shard: (a, r, 0))],
    )
    return _pcall(
        body, name="grad_pair_sum", grid_spec=grid_spec,
        out_shape=[jax.ShapeDtypeStruct((A4, Rh, Cc), BF16), jax.ShapeDtypeStruct((A, Rh, Cc), F32)],
        compiler_params=_params("parallel", "parallel", "arbitrary"),
    )(core, shard, g5, r1)


def _swap_rider(grads):
    n = len(grads)

    def copies(ins, outs, send_sems, recv_sems):
        x, y, c, _chips = _where_am_i()
        return [_remote(ins[t].at[:, :, 1 - c], outs[t], send_sems, recv_sems, t, (x, y, 1 - c)) for t in range(n)]

    def start(ins, outs, send_sems, recv_sems):
        for cp in copies(ins, outs, send_sems, recv_sems):
            cp.start()

    def finish(ins, outs, send_sems, recv_sems):
        for cp in copies(ins, outs, send_sems, recv_sems):
            cp.wait()

    out_shapes = [jax.ShapeDtypeStruct(g.shape[:2] + g.shape[3:], g.dtype) for g in grads]
    return dict(name="swap", ins=list(grads), out_shapes=out_shapes, n=n, start=start, finish=finish)


def _reduce_chip(grads, r1, core, shard):
    qb, qf = [], []
    for g, r in zip(grads, r1):
        A, _, _, Rh, Cc = g.shape
        b, f = _pair_sum(g.reshape(A * N_CHIPS, 2, Rh, Cc), r.reshape(A * N_CHIPS, Rh, Cc), core, shard)
        qb.append(b.reshape(A, N_CHIPS, Rh, Cc))
        qf.append(f)
    return qb, qf


def _reduce_finish(qf, r2):
    return _as_shards(_join_sibling_halves(_chip_sums(qf, r2)))


def _chip_sums(qf, r2):
    halves = []
    for f, r in zip(qf, r2):
        A, Rh, Cc = f.shape
        terms = [f.reshape(A * Rh, Cc)] + [r[j].reshape(A * Rh, Cc) for j in range(3)]
        halves.append(_sum_rows("grad_chip_sum", terms, [F32])[0].reshape(A, Rh, Cc))
    return halves


def _as_shards(full):
    return [t.reshape(t.shape[0], 2 * t.shape[2], t.shape[3]) for t in full]


def _small_pair(v):
    return _sum_rows("small_pair_sum", [v, _swap_small(v)], [F32])[0]


def _small_chip_sum(pair, others):
    x, y = lax.axis_index("x"), lax.axis_index("y")
    s = 2 * x + y
    stack = jnp.concatenate([pair[None], others], axis=0)
    src = jnp.stack([s, s ^ 2, s ^ 1, s ^ 3])
    order = jnp.argsort(src)
    terms = [lax.dynamic_index_in_dim(stack, order[k], 0, keepdims=False) for k in range(N_CHIPS)]
    return _sum_rows("small_chip_sum", terms, [F32])[0]


def _block_diag(w):
    nb, bw, _ = w.shape
    per = LANES // bw
    w = w.reshape(nb // per, per, bw, bw)
    eye = jnp.eye(per, dtype=w.dtype)
    bd = jnp.einsum("tpij,pq->tpiqj", w, eye).reshape(nb // per, LANES, LANES)
    return bd.astype(BF16)


def _block_diag_grad(g, bw):
    nt = g.shape[0]
    per = LANES // bw
    g = g.reshape(nt, per, bw, per, bw)
    return jnp.stack([g[:, p, :, p, :] for p in range(per)], axis=1).reshape(nt * per, bw, bw)


def _split5(w):
    R, Cc = w.shape[-2:]
    return w.reshape(-1, 2, R // 2, Cc)


def kernel(x, w_in, conv_w, conv_b, w_rg, b_rg, w_ig, b_ig, lru_lambda, sinks, w_branch, w_out, ln1_g, ln1_b, w_ffn_in, w_ffn_out, ln2_g, ln2_b, loss_target, m_w_in, m_conv_w, m_conv_b, m_w_rg, m_b_rg, m_w_ig, m_b_ig, m_lru_lambda, m_sinks, m_w_branch, m_w_out, m_ln1_g, m_ln1_b, m_w_ffn_in, m_w_ffn_out, m_ln2_g, m_ln2_b, v_w_in, v_conv_w, v_conv_b, v_w_rg, v_b_rg, v_w_ig, v_b_ig, v_lru_lambda, v_sinks, v_w_branch, v_w_out, v_ln1_g, v_ln1_b, v_w_ffn_in, v_w_ffn_out, v_ln2_g, v_ln2_b):
    B, S, D = x.shape
    T = B * S
    L = w_in.shape[0]
    H = D // HEAD_DIM
    KVB = D // SWA_GROUP
    FH = w_ffn_out.shape[1] * N_CHIPS
    C = w_in.shape[2] * N_CHIPS
    alpha = (2.0 * L) ** 0.25
    off = {}
    pos = 0
    for nm, wd in (("lx", D), ("lg", D), ("qb", D), ("kb", KVB), ("vb", KVB), ("qc", D), ("kc", D), ("vc", D), ("gt", 3 * D)):
        off[nm] = pos
        pos += wd
    assert pos == C
    cx, cy, cc = lax.axis_index("x"), lax.axis_index("y"), lax.axis_index("c")
    shard = (2 * cx + cy).astype(jnp.int32)
    core_a = cc.astype(jnp.int32).reshape(1)
    shard_a = shard.reshape(1)

    def shard_views(l):
        return [_split5(w_in[l].astype(BF16)), _split5(w_branch[l].astype(BF16)), _split5(w_out[l].astype(BF16)),
                _split5(w_ffn_in[l].astype(BF16)), _split5(w_ffn_out[l].astype(BF16))]

    def as_weights(g):
        return dict(
            w_in=g[0].reshape(N_CHIPS, D, C // N_CHIPS),
            w_branch=g[1].reshape(3, D, D),
            w_out=g[2].reshape(D, D),
            w_ffn_in=g[3].reshape(N_CHIPS, D, 2 * FH // N_CHIPS),
            w_ffn_out=g[4].reshape(FH, D),
        )

    first_views = shard_views(0)
    w_in0 = _gather_weights(first_views[:1])
    full = [dict(w_in=w_in0[0].reshape(N_CHIPS, D, C // N_CHIPS))]
    cw_all = _gather_small(conv_w.reshape(L * CONV_WIDTH, D // N_CHIPS))
    conv_w_full = jnp.transpose(cw_all, (1, 0, 2)).reshape(L, CONV_WIDTH, D)

    def layer_params(l):
        return dict(conv_w=conv_w_full[l], conv_b=conv_b[l][None], w_rg_bd=_block_diag(w_rg[l]), b_rg=b_rg[l][None],
                    w_ig_bd=_block_diag(w_ig[l]), b_ig=b_ig[l][None], lam=lru_lambda[l][None])

    def sink_rows(l, hb):
        sk = sinks[l].reshape(H // hb, 1, hb)
        return jnp.pad(sk, ((0, 0), (0, 0), (0, LANES - hb)))

    hb_b = SWA_HB

    saved = []
    xin = x.reshape(T, D)
    for l in range(L):
        fw, lp = full[l], layer_params(l)
        proj = _matmul(xin, fw["w_in"], mode="nn", name="mm_proj", tm=512, n_outer=True)
        proj3 = proj.reshape(B, S, C)
        h3, ya3 = _lru_fwd(proj3, lp, D=D, x_off=off["lx"], g_off=off["lg"])
        skr = sink_rows(l, hb_b)
        swa_kw = dict(D=D, q_off=off["qb"], k_off=off["kb"], v_off=off["vb"])
        nxt = shard_views(l + 1) if l + 1 < L else None
        landed = []
        if l == 0:
            res = _swa_fwd(proj3, skr, rider=_gather_rider(first_views[1:]), **swa_kw)
            yb3, lse_b = res[0], res[1]
            fw = as_weights(w_in0 + _gather_forward(res[2:], first_views[1:]))
            full[0] = fw
        elif nxt is not None:
            res = _swa_fwd(proj3, skr, rider=_gather_rider(nxt[:1]), **swa_kw)
            yb3, lse_b = res[0], res[1]
            landed = list(res[2:])
        else:
            yb3, lse_b = _swa_fwd(proj3, skr, **swa_kw)
        dil_kw = dict(D=D, q_off=off["qc"], k_off=off["kc"], v_off=off["vc"])
        if nxt is not None:
            rest = nxt[len(landed):]
            res = _dil_fwd(proj3, rider=_gather_rider(rest), **dil_kw)
            yc3, lse_c = res[0], res[1]
            landed = landed + list(res[2:])
        else:
            yc3, lse_c = _dil_fwd(proj3, **dil_kw)
        ya, yb, yc = ya3.reshape(T, D), yb3.reshape(T, D), yc3.reshape(T, D)
        if nxt is not None:
            res = _branch_fwd([ya, yb, yc], fw["w_branch"], proj, D=D, g_off=off["gt"], rider=_forward_rider(landed))
            branch, merged = res[0], res[1]
            full.append(as_weights(_place_own_shard(res[2:], nxt)))
        else:
            branch, merged = _branch_fwd([ya, yb, yc], fw["w_branch"], proj, D=D, g_off=off["gt"])
        z1, x1 = _matmul(merged, fw["w_out"], mode="nn", name="mm_out_ln", tn=1024, resid=xin, rs=alpha,
                         ln=(ln1_g[l][None], ln1_b[l][None]))
        hh = _matmul(x1, fw["w_ffn_in"], mode="nn", name="mm_ffn_in", n_outer=True)
        f = _swiglu_fwd(hh)
        z2, x2 = _matmul(f, fw["w_ffn_out"], mode="nn", name="mm_ffn_out_ln", tn=1024, tk=4096, resid=x1, rs=alpha,
                         ln=(ln2_g[l][None], ln2_b[l][None]))
        saved.append(dict(x=xin, proj=proj, h3=h3, ya=ya, yb=yb, lse_b=lse_b, yc=yc, lse_c=lse_c, branch=branch,
                          merged=merged, z1=z1, x1=x1, hh=hh, f=f, z2=z2, skr=skr))
        xin = x2

    dx, loss_rows = _loss_head(xin, loss_target.reshape(T, D))
    loss = lax.psum(jnp.sum(loss_rows), ("x", "y", "c"))

    big = {k: [None] * L for k in ("w_in", "w_branch", "w_out", "w_ffn_in", "w_ffn_out")}
    small = [None] * L

    def store_reduced(l, red):
        big["w_in"][l] = red[0].reshape(D, C // N_CHIPS)
        big["w_branch"][l] = red[1].reshape(3, D // N_CHIPS, D)
        big["w_out"][l] = red[2].reshape(D // N_CHIPS, D)
        big["w_ffn_in"][l] = red[3].reshape(D, 2 * FH // N_CHIPS)
        big["w_ffn_out"][l] = red[4].reshape(FH // N_CHIPS, D)

    above = None
    pending = None
    for l in reversed(range(L)):
        fw, lp, sv = full[l], layer_params(l), saved[l]
        dz2, dg2, db2 = _ln_bwd(dx, sv["z2"], ln2_g[l][None])
        df = _matmul(dz2, fw["w_ffn_out"], mode="nt", name="mm_dffn_out_x", tn=4096, tk=1024)
        g_ffn_out = _matmul(sv["f"], dz2, mode="tn", name="mm_dffn_out_w", tm=1408, tn=1024, tk=1024)
        if above is None:
            dhh, _unused = _swiglu_bwd(sv["hh"], df)
        else:
            dhh, r1 = _swiglu_bwd(sv["hh"], df, rider=_swap_rider(above[1]))
            pending = (above[0],) + _reduce_chip(above[1], r1, core_a, shard_a)
        dx1 = _matmul(dhh, fw["w_ffn_in"], mode="nt", name="mm_dffn_in_x", tm=1024, tn=1024, resid=dz2, rs=alpha)
        g_ffn_in = _matmul(sv["x1"], dhh, mode="tn", name="mm_dffn_in_w", tm=1024, tk=1024, out_shards=N_CHIPS)
        dz1, dg1, db1 = _ln_bwd(dx1, sv["z1"], ln1_g[l][None])
        dmerged = _matmul(dz1, fw["w_out"], mode="nt", name="mm_dout_x", tn=1024, tk=1024)
        g_out = _matmul(sv["merged"], dz1, mode="tn", name="mm_dout_w", tm=1024, tn=1024, tk=1024)
        dbranch, dgates = _branch_bwd(dmerged, sv["branch"], sv["proj"], D=D, g_off=off["gt"])
        ys = [sv["ya"], sv["yb"], sv["yc"]]
        dys, g_branch = [], []
        for n in range(3):
            dys.append(_matmul(dbranch, fw["w_branch"][n], mode="nt", name="mm_dbranch_x", tn=1024, tk=1024, a_pick=n))
            g_branch.append(_matmul(ys[n], dbranch, mode="tn", name="mm_dbranch_w", tm=1024, tn=1024, tk=1024, b_pick=n))
        proj3 = sv["proj"].reshape(B, S, C)
        r3 = lambda t: t.reshape(B, S, t.shape[-1])
        lru = _lru_bwd(proj3, sv["h3"], r3(dys[0]), lp, D=D, x_off=off["lx"], g_off=off["lg"])
        dxr, dgate = lru[0], lru[1]
        dqb, dkb, dvb, dsk = _swa_bwd(proj3, r3(sv["yb"]), sv["lse_b"], r3(dys[1]), sv["skr"], D=D, q_off=off["qb"],
                                      k_off=off["kb"], v_off=off["vb"])
        dil_kw = dict(D=D, q_off=off["qc"], k_off=off["kc"], v_off=off["vc"])
        if pending is None:
            acc = _dil_bwd(proj3, r3(sv["yc"]), sv["lse_c"], r3(dys[2]), **dil_kw)
        else:
            res = _dil_bwd(proj3, r3(sv["yc"]), sv["lse_c"], r3(dys[2]), rider=_exchange_rider(pending[1]), **dil_kw)
            acc = res[:3]
            halves = _chip_sums(pending[2], res[3:])
        f2 = lambda t: t.reshape(T, t.shape[-1]).astype(BF16)
        dproj = jnp.concatenate([f2(dxr), f2(dgate), f2(dqb), f2(dkb), f2(dvb), f2(acc[0]), f2(acc[1]), f2(acc[2])] + dgates, axis=1)
        dx_kw = dict(mode="nt", name="mm_dproj_x", tm=1024, tn=1024, resid=dz1, rs=alpha)
        if pending is None:
            dx = _matmul(dproj, fw["w_in"], **dx_kw)
        else:
            dx, got = _matmul(dproj, fw["w_in"], rider=_join_rider(halves), **dx_kw)
            store_reduced(pending[0], _as_shards(_place_own_half(got, halves)))
        g_in = _matmul(sv["x"], dproj, mode="tn", name="mm_dproj_w", tm=512, tk=1024, out_shards=N_CHIPS)

        g5 = [g_in.reshape(1, N_CHIPS, 2, D // 2, C // N_CHIPS),
              jnp.stack(g_branch).reshape(3, N_CHIPS, 2, D // N_CHIPS // 2, D),
              g_out.reshape(1, N_CHIPS, 2, D // N_CHIPS // 2, D),
              g_ffn_in.reshape(1, N_CHIPS, 2, D // 2, 2 * FH // N_CHIPS),
              g_ffn_out.reshape(1, N_CHIPS, 2, FH // N_CHIPS // 2, D)]
        above = (l, g5)

        dsinks = jnp.sum(dsk, axis=0)[:, 0, :hb_b].reshape(H)
        bw = w_rg.shape[-1]
        small[l] = [lru[2].reshape(-1), lru[3].reshape(-1), _block_diag_grad(lru[4], bw).reshape(-1), lru[5].reshape(-1),
                    _block_diag_grad(lru[6], bw).reshape(-1), lru[7].reshape(-1), lru[8].reshape(-1),
                    jnp.pad(dsinks, (0, LANES - H)), dg1.reshape(-1), db1.reshape(-1), dg2.reshape(-1), db2.reshape(-1)]

    qb, qf = _reduce_chip(above[1], _swap_sibling_halves(above[1]), core_a, shard_a)
    store_reduced(above[0], _reduce_finish(qf, _exchange_chips(qb)))

    sizes = [t.size for t in small[0]]
    flat = jnp.concatenate([t for l in range(L) for t in small[l]])
    n_flat = flat.size
    rows = -(-n_flat // (LANES * 256)) * 256
    flat = jnp.pad(flat, (0, rows * LANES - n_flat)).reshape(rows, LANES)
    pair = _small_pair(flat)

    order = ["w_in", "conv_w", "conv_b", "w_rg", "b_rg", "w_ig", "b_ig", "lru_lambda", "sinks", "w_branch", "w_out",
             "ln1_g", "ln1_b", "w_ffn_in", "w_ffn_out", "ln2_g", "ln2_b"]
    weights = dict(w_in=w_in, conv_w=conv_w, conv_b=conv_b, w_rg=w_rg, b_rg=b_rg, w_ig=w_ig, b_ig=b_ig, lru_lambda=lru_lambda,
                   sinks=sinks, w_branch=w_branch, w_out=w_out, ln1_g=ln1_g, ln1_b=ln1_b, w_ffn_in=w_ffn_in,
                   w_ffn_out=w_ffn_out, ln2_g=ln2_g, ln2_b=ln2_b)
    ms = dict(w_in=m_w_in, conv_w=m_conv_w, conv_b=m_conv_b, w_rg=m_w_rg, b_rg=m_b_rg, w_ig=m_w_ig, b_ig=m_b_ig,
              lru_lambda=m_lru_lambda, sinks=m_sinks, w_branch=m_w_branch, w_out=m_w_out, ln1_g=m_ln1_g, ln1_b=m_ln1_b,
              w_ffn_in=m_w_ffn_in, w_ffn_out=m_w_ffn_out, ln2_g=m_ln2_g, ln2_b=m_ln2_b)
    vs = dict(w_in=v_w_in, conv_w=v_conv_w, conv_b=v_conv_b, w_rg=v_w_rg, b_rg=v_b_rg, w_ig=v_w_ig, b_ig=v_b_ig,
              lru_lambda=v_lru_lambda, sinks=v_sinks, w_branch=v_w_branch, w_out=v_w_out, ln1_g=v_ln1_g, ln1_b=v_ln1_b,
              w_ffn_in=v_w_ffn_in, w_ffn_out=v_w_ffn_out, ln2_g=v_ln2_g, ln2_b=v_ln2_b)
    grads = dict(w_in=jnp.stack(big["w_in"]), w_branch=jnp.stack(big["w_branch"]), w_out=jnp.stack(big["w_out"]),
                 w_ffn_in=jnp.stack(big["w_ffn_in"]), w_ffn_out=jnp.stack(big["w_ffn_out"]))
    deltas, new_m, new_v = {}, {}, {}
    deltas["w_in"], new_m["w_in"], new_v["w_in"], others = _adamw(w_in, grads["w_in"], m_w_in, v_w_in,
                                                                  rider=_small_exchange_rider(pair))
    red_small = _small_chip_sum(pair, others).reshape(-1)
    per_layer = sum(sizes)
    names = ["conv_w", "conv_b", "w_rg", "b_rg", "w_ig", "b_ig", "lru_lambda", "sinks", "ln1_g", "ln1_b", "ln2_g", "ln2_b"]
    sg = {nm: [] for nm in names}
    for l in range(L):
        p = l * per_layer
        for nm, sz in zip(names, sizes):
            sg[nm].append(red_small[p:p + sz])
            p += sz
    grads.update(
        conv_w=lax.dynamic_slice_in_dim(jnp.stack(sg["conv_w"]).reshape(L, CONV_WIDTH, D), shard * (D // N_CHIPS), D // N_CHIPS, axis=2),
        conv_b=jnp.stack(sg["conv_b"]), w_rg=jnp.stack(sg["w_rg"]).reshape(w_rg.shape), b_rg=jnp.stack(sg["b_rg"]),
        w_ig=jnp.stack(sg["w_ig"]).reshape(w_ig.shape), b_ig=jnp.stack(sg["b_ig"]), lru_lambda=jnp.stack(sg["lru_lambda"]),
        sinks=jnp.stack(sg["sinks"])[:, :H], ln1_g=jnp.stack(sg["ln1_g"]), ln1_b=jnp.stack(sg["ln1_b"]),
        ln2_g=jnp.stack(sg["ln2_g"]), ln2_b=jnp.stack(sg["ln2_b"]),
    )

    for nm in order[1:]:
        deltas[nm], new_m[nm], new_v[nm] = _adamw(weights[nm], grads[nm], ms[nm], vs[nm])
    return (loss, dx.reshape(B, S, D), *[grads[nm] for nm in order], *[deltas[nm] for nm in order],
            *[new_m[nm] for nm in order], *[new_v[nm] for nm in order])
```

```python
import math

import jax
import jax.numpy as jnp
from jax import lax
from jax.experimental import pallas as pl
from jax.experimental.pallas import tpu as pltpu

HEAD_DIM = 64
WIN = 128
DILS = (1, 4, 16)
SWA_GROUP = 4
CONV_WIDTH = 4
LRU_C = 8.0
LN_EPS = 1e-5
NEG_INF = -1e30
N_CHIPS = 4
ADAM_LR, ADAM_B1, ADAM_B2, ADAM_EPS, ADAM_WD, ADAM_STEP = 0.001, 0.9, 0.999, 1e-08, 0.01, 10

LANES = 128
SUBLANES = 8
VMEM_LIMIT = 48 * 1024 * 1024

assert math.log2(HEAD_DIM) % 2 == 0

F32 = jnp.float32
BF16 = jnp.bfloat16
MESH = pl.DeviceIdType.MESH
ANY = pl.BlockSpec(memory_space=pl.ANY)


def _pcall(body, **kw):
    return pl.pallas_call(body, **kw)


def _pcall_comm(body, **kw):
    return pl.pallas_call(body, **kw)


def _params(*sem):
    return pltpu.CompilerParams(dimension_semantics=tuple(sem), vmem_limit_bytes=VMEM_LIMIT)


def _tile(dim, target):
    if dim <= target:
        return dim
    best = None
    for t in range(LANES, target + 1, LANES):
        if dim % t == 0:
            best = t
    assert best is not None, (dim, target)
    return best


def _sigmoid(x):
    return 1.0 / (1.0 + jnp.exp(-x))


def _dot(a, b, dims):
    return lax.dot_general(a, b, (dims, ((), ())), preferred_element_type=F32)


def _dot_nn(a, b):
    return _dot(a, b, ((1,), (0,)))


def _dot_nt(a, b):
    return _dot(a, b, ((1,), (1,)))


def _dot_tn(a, b):
    return _dot(a, b, ((0,), (0,)))


def _matmul(a, b, *, mode, name, out_dtype=F32, tm=512, tn=512, tk=2048, resid=None, rs=1.0, out_shards=0, n_outer=False,
            ln=None, a_pick=0, b_pick=0, rider=None):
    b_sh = b.ndim == 3
    a_st = a.ndim == 3
    if mode == "nn":
        M, K = a.shape[-2:]
        N = b.shape[-1] * (b.shape[0] if b_sh else 1)
    elif mode == "nt":
        M, K = a.shape[-2:]
        N = b.shape[-2]
    else:
        K, M = a.shape
        N = b.shape[-1]
    tm = _tile(M, tm)
    if mode == "nn" and b_sh:
        tn = b.shape[-1]
    elif out_shards:
        tn = N // out_shards
    else:
        tn = _tile(N, tn)
    if mode == "nt" and b_sh:
        tk = b.shape[-1]
    else:
        tk = _tile(K, tk)
    nk = K // tk
    grid = (N // tn, M // tm, nk) if n_outer else (M // tm, N // tn, nk)

    def spec(shape, f):
        return pl.BlockSpec(shape, (lambda g0, g1, k: f(g1, g0, k)) if n_outer else f)

    a_rows = spec((None, tm, tk), lambda i, j, k: (a_pick, i, k)) if a_st else spec((tm, tk), lambda i, j, k: (i, k))
    if mode == "nn":
        a_spec = a_rows
        b_spec = spec((None, tk, tn), lambda i, j, k: (j, k, 0)) if b_sh else spec((tk, tn), lambda i, j, k: (k, j))
        contract = _dot_nn
    elif mode == "nt":
        a_spec = a_rows
        b_spec = spec((None, tn, tk), lambda i, j, k: (k, j, 0)) if b_sh else spec((tn, tk), lambda i, j, k: (j, k))
        contract = _dot_nt
    else:
        a_spec = spec((tk, tm), lambda i, j, k: (k, i))
        b_spec = spec((None, tk, tn), lambda i, j, k: (b_pick, k, j)) if b_sh else spec((tk, tn), lambda i, j, k: (k, j))
        contract = _dot_tn
    if out_shards:
        out_shape = jax.ShapeDtypeStruct((out_shards, M, tn), out_dtype)
        o_spec = spec((None, tm, tn), lambda i, j, k: (j, i, 0))
    else:
        out_shape = jax.ShapeDtypeStruct((M, N), out_dtype)
        o_spec = spec((tm, tn), lambda i, j, k: (i, j))
    in_specs = [a_spec, b_spec]
    args = [a, b]
    if resid is not None:
        in_specs.append(spec((tm, tn), lambda i, j, k: (i, j)))
        args.append(resid)
    if ln is not None:
        assert tn == N and resid is not None and not out_shards
        in_specs += [spec((1, N), lambda i, j, k: (0, 0))] * 2
        args += list(ln)
        out_shape = [out_shape, out_shape]
        o_spec = [o_spec, o_spec]
    n_in = len(args)

    def body(*refs):
        a_ref, b_ref = refs[:2]
        r_ref = refs[2] if resid is not None else None
        o_ref = refs[n_in]
        part = contract(a_ref[...].astype(BF16), b_ref[...].astype(BF16))

        def finish(res):
            if resid is not None:
                res = res + rs * r_ref[...]
            o_ref[...] = res.astype(out_dtype)
            if ln is not None:
                g_ref, bb_ref, y_ref = refs[n_in - 2], refs[n_in - 1], refs[n_in + 1]
                zc = res - jnp.mean(res, axis=1, keepdims=True)
                var = jnp.mean(zc * zc, axis=1, keepdims=True)
                y_ref[...] = zc * lax.rsqrt(var + LN_EPS) * g_ref[...] + bb_ref[...]

        if nk == 1:
            finish(part)
            return
        acc_ref = refs[-1]
        k = pl.program_id(2)

        @pl.when(k == 0)
        def _():
            acc_ref[...] = part

        @pl.when(jnp.logical_and(k > 0, k < nk - 1))
        def _():
            acc_ref[...] += part

        @pl.when(k == nk - 1)
        def _():
            finish(acc_ref[...] + part)

    if rider is None:
        return _pcall(
            body, name=name, grid=grid, in_specs=in_specs, out_specs=o_spec, out_shape=out_shape,
            scratch_shapes=[pltpu.VMEM((tm, tn), F32)] if nk > 1 else [],
            compiler_params=_params("parallel", "parallel", "arbitrary"),
        )(*args)
    assert ln is None
    res = _call_with_rider(
        rider, body, name=name, grid=grid, in_specs=in_specs, out_specs=[o_spec], out_shape=[out_shape],
        scratch_shapes=[pltpu.VMEM((tm, tn), F32)] if nk > 1 else [], args=args)
    return res[0], list(res[1:])


def _shift_down(x, d, row):
    return jnp.where(row >= d, pltpu.roll(x, d, 0), 0.0)


def _shift_up(x, d, row, n):
    return jnp.where(row < n - d, pltpu.roll(x, n - d, 0), 0.0)


def _log1p(u):
    w = 1.0 + u
    return jnp.where(w == 1.0, u, jnp.log(w) * u / (w - 1.0))


def _gelu_parts(g):
    k = math.sqrt(2.0 / math.pi)
    c = 0.044715
    t = jnp.tanh(k * (g + c * g * g * g))
    val = 0.5 * g * (1.0 + t)
    der = 0.5 * (1.0 + t) + 0.5 * g * (1.0 - t * t) * k * (1.0 + 3.0 * c * g * g)
    return val, der


def _lru_gates(xr, cw_ref, cb_ref, wrg_ref, brg_ref, wig_ref, big_ref, lam_ref, row):
    xc = cw_ref[3:4, :] * xr + cb_ref[...]
    for d in range(1, CONV_WIDTH):
        xc = xc + cw_ref[3 - d:4 - d, :] * _shift_down(xr, d, row)
    xcb = xc.astype(BF16)
    r = _sigmoid(_dot_nn(xcb, wrg_ref[...]) + brg_ref[...])
    ig = _sigmoid(_dot_nn(xcb, wig_ref[...]) + big_ref[...])
    lam = lam_ref[...]
    sp = jnp.maximum(-lam, 0.0) + _log1p(jnp.exp(-jnp.abs(lam)))
    log_a = (-LRU_C) * r * sp
    a = jnp.exp(log_a)
    y2 = 2.0 * log_a
    one_m_a2 = jnp.where(y2 > -0.01, -(y2 + 0.5 * y2 * y2 + (1.0 / 6.0) * y2 * y2 * y2), 1.0 - jnp.exp(y2))
    mult = jnp.sqrt(one_m_a2)
    return xc, r, ig, sp, a, mult


def _scan_local(a, b, row, n, reverse):
    sub = row % SUBLANES
    d = 1
    while d < SUBLANES:
        if reverse:
            keep = sub < SUBLANES - d
            a_s = jnp.where(keep, pltpu.roll(a, n - d, 0), 1.0)
            b_s = jnp.where(keep, pltpu.roll(b, n - d, 0), 0.0)
        else:
            keep = sub >= d
            a_s = jnp.where(keep, pltpu.roll(a, d, 0), 1.0)
            b_s = jnp.where(keep, pltpu.roll(b, d, 0), 0.0)
        b = a * b_s + b
        a = a * a_s
        d *= 2
    return a, b


def _scan_carry(a_ref, b_ref, out_ref, n, reverse):
    ng = n // SUBLANES

    def step(gidx, carry):
        g = (ng - 1 - gidx) if reverse else gidx
        rows = pl.ds(pl.multiple_of(g * SUBLANES, SUBLANES), SUBLANES)
        h = a_ref[rows, :] * carry + b_ref[rows, :]
        out_ref[rows, :] = h
        return h[0:1, :] if reverse else h[SUBLANES - 1:SUBLANES, :]

    lax.fori_loop(0, ng, step, jnp.zeros((1, LANES), F32), unroll=8)


def _lru_specs(B, S, D, C, x_off, g_off):
    nct = D // LANES
    seq = lambda off: pl.BlockSpec((None, S, LANES), lambda ct, b: (b, 0, off // LANES + ct))
    row = lambda r: pl.BlockSpec((r, LANES), lambda ct, b: (0, ct))
    wbd = pl.BlockSpec((None, LANES, LANES), lambda ct, b: (ct, 0, 0))
    return nct, seq, row, wbd


def _lru_fwd(proj3, lp, *, D, x_off, g_off):
    B, S, C = proj3.shape
    nct, seq, row, wbd = _lru_specs(B, S, D, C, x_off, g_off)

    def body(xr_ref, g_ref, cw_ref, cb_ref, wrg_ref, brg_ref, wig_ref, big_ref, lam_ref, h_ref, ya_ref, a_s, b_s):
        rowi = lax.broadcasted_iota(jnp.int32, (S, LANES), 0)
        xr = xr_ref[...]
        xc, r, ig, sp, a, mult = _lru_gates(xr, cw_ref, cb_ref, wrg_ref, brg_ref, wig_ref, big_ref, lam_ref, rowi)
        al, bl = _scan_local(a, mult * (ig * xc), rowi, S, False)
        a_s[...] = al
        b_s[...] = bl
        _scan_carry(a_s, b_s, h_ref, S, False)
        gel, _ = _gelu_parts(g_ref[...])
        ya_ref[...] = (h_ref[...] * gel).astype(BF16)

    out_seq = pl.BlockSpec((None, S, LANES), lambda ct, b: (b, 0, ct))
    return _pcall(
        body, name="lru_fwd", grid=(nct, B),
        in_specs=[seq(x_off), seq(g_off), row(CONV_WIDTH), row(1), wbd, row(1), wbd, row(1), row(1)],
        out_specs=[out_seq, out_seq],
        out_shape=[jax.ShapeDtypeStruct((B, S, D), F32), jax.ShapeDtypeStruct((B, S, D), BF16)],
        scratch_shapes=[pltpu.VMEM((S, LANES), F32), pltpu.VMEM((S, LANES), F32)],
        compiler_params=_params("parallel", "parallel"),
    )(proj3, proj3, lp["conv_w"], lp["conv_b"], lp["w_rg_bd"], lp["b_rg"], lp["w_ig_bd"], lp["b_ig"], lp["lam"])


def _lru_bwd(proj3, h3, dya3, lp, *, D, x_off, g_off):
    B, S, C = proj3.shape
    nct, seq, row, wbd = _lru_specs(B, S, D, C, x_off, g_off)

    def body(xr_ref, g_ref, h_ref, dy_ref, cw_ref, cb_ref, wrg_ref, brg_ref, wig_ref, big_ref, lam_ref,
             dxr_ref, dg_ref, dcw_ref, dcb_ref, dwrg_ref, dbrg_ref, dwig_ref, dbig_ref, dlam_ref, a_s, b_s, l_s):
        first = pl.program_id(1) == 0
        rowi = lax.broadcasted_iota(jnp.int32, (S, LANES), 0)
        xr = xr_ref[...]
        xc, r, ig, sp, a, mult = _lru_gates(xr, cw_ref, cb_ref, wrg_ref, brg_ref, wig_ref, big_ref, lam_ref, rowi)
        h = h_ref[...]
        dy = dy_ref[...]
        gel, dgel = _gelu_parts(g_ref[...])
        dg_ref[...] = (dy * h * dgel).astype(BF16)
        al, bl = _scan_local(_shift_up(a, 1, rowi, S), dy * gel, rowi, S, True)
        a_s[...] = al
        b_s[...] = bl
        _scan_carry(a_s, b_s, l_s, S, True)
        lamb = l_s[...]
        u = ig * xc
        da = lamb * _shift_down(h, 1, rowi)
        dlog_a = da * a - (lamb * u) * (a * a) / mult
        du = lamb * mult
        dpre_r = (dlog_a * ((-LRU_C) * sp)) * r * (1.0 - r)
        dpre_i = (du * xc) * ig * (1.0 - ig)
        dsp = jnp.sum(dlog_a * ((-LRU_C) * r), axis=0, keepdims=True)
        dlam = dsp * (-1.0 / (1.0 + jnp.exp(lam_ref[...])))
        dpr = dpre_r.astype(BF16)
        dpi = dpre_i.astype(BF16)
        dxc = du * ig + _dot_nt(dpr, wrg_ref[...]) + _dot_nt(dpi, wig_ref[...])
        xcb = xc.astype(BF16)
        dwrg = _dot_tn(xcb, dpr)
        dwig = _dot_tn(xcb, dpi)
        dxr = cw_ref[3:4, :] * dxc
        dcw = [jnp.sum(xr * dxc, axis=0, keepdims=True)]
        for d in range(1, CONV_WIDTH):
            dxr = dxr + cw_ref[3 - d:4 - d, :] * _shift_up(dxc, d, rowi, S)
            dcw.append(jnp.sum(_shift_down(xr, d, rowi) * dxc, axis=0, keepdims=True))
        dxr_ref[...] = dxr.astype(BF16)
        dcw_rows = jnp.concatenate(dcw[::-1], axis=0)
        sums = ((dcw_ref, dcw_rows), (dcb_ref, jnp.sum(dxc, axis=0, keepdims=True)), (dwrg_ref, dwrg),
                (dbrg_ref, jnp.sum(dpre_r, axis=0, keepdims=True)), (dwig_ref, dwig),
                (dbig_ref, jnp.sum(dpre_i, axis=0, keepdims=True)), (dlam_ref, dlam))

        @pl.when(first)
        def _():
            for ref, val in sums:
                ref[...] = val

        @pl.when(jnp.logical_not(first))
        def _():
            for ref, val in sums:
                ref[...] += val

    out_seq = pl.BlockSpec((None, S, LANES), lambda ct, b: (b, 0, ct))
    f = lambda shape: jax.ShapeDtypeStruct(shape, F32)
    nb = D // LANES
    return _pcall(
        body, name="lru_bwd", grid=(nct, B),
        in_specs=[seq(x_off), seq(g_off), out_seq, out_seq, row(CONV_WIDTH), row(1), wbd, row(1), wbd, row(1), row(1)],
        out_specs=[out_seq, out_seq, row(CONV_WIDTH), row(1), wbd, row(1), wbd, row(1), row(1)],
        out_shape=[jax.ShapeDtypeStruct((B, S, D), BF16), jax.ShapeDtypeStruct((B, S, D), BF16),
                   f((CONV_WIDTH, D)), f((1, D)), f((nb, LANES, LANES)), f((1, D)), f((nb, LANES, LANES)), f((1, D)), f((1, D))],
        scratch_shapes=[pltpu.VMEM((S, LANES), F32)] * 3,
        compiler_params=_params("parallel", "arbitrary"),
    )(proj3, proj3, h3, dya3, lp["conv_w"], lp["conv_b"], lp["w_rg_bd"], lp["b_rg"], lp["w_ig_bd"], lp["b_ig"], lp["lam"])


def _pair_stack(x, lo):
    z = jnp.zeros_like(x)
    return jnp.concatenate([jnp.where(lo, x, z), jnp.where(lo, z, x)], axis=0).astype(BF16)


def _pair_join(y2, lo):
    return jnp.where(lo, y2[:WIN], y2[WIN:])


def _pair_col(xb):
    return jnp.concatenate([xb[:, 0:1], xb[:, HEAD_DIM:HEAD_DIM + 1]], axis=0)


def _pair_bcast(col, lo):
    return jnp.where(lo, jnp.broadcast_to(col[:WIN], (WIN, LANES)), jnp.broadcast_to(col[WIN:], (WIN, LANES)))


def _dil_rows(it, d, S):
    if d == 1:
        cur = pl.multiple_of(it * WIN, WIN)
        prev = pl.multiple_of(jnp.maximum(it - 1, 0) * WIN, WIN)
        return pl.ds(cur, WIN), pl.ds(prev, WIN), it > 0
    r, i = it % d, it // d
    cur = i * (WIN * d) + r
    prev = jnp.maximum(i - 1, 0) * (WIN * d) + r
    return pl.ds(cur, WIN, stride=d), pl.ds(prev, WIN, stride=d), i > 0


def _dil_bias(two_blocks, stack=2):
    nk = 2 * WIN if two_blocks else WIN
    qi = lax.broadcasted_iota(jnp.int32, (stack * WIN, nk), 0) & (WIN - 1)
    kj = lax.broadcasted_iota(jnp.int32, (stack * WIN, nk), 1)
    if not two_blocks:
        return jnp.where(kj <= qi, 0.0, NEG_INF), None
    cur = jnp.logical_and(kj >= WIN, kj - WIN <= qi)
    prev = jnp.logical_and(kj < WIN, kj >= qi)
    return jnp.where(jnp.logical_or(cur, prev), 0.0, NEG_INF), jnp.where(cur, 0.0, NEG_INF)


def _dil_specs(B, S, D, C, offs):
    grid = (B, D // LANES)
    seq = lambda off: pl.BlockSpec((None, S, LANES), lambda b, p: (b, 0, off // LANES + p))
    return grid, [seq(o) for o in offs], seq(0)


def _call_with_rider(rider, body, *, name, grid, in_specs, out_specs, out_shape, scratch_shapes, args, semantics=None):
    if rider is None:
        return _pcall(body, name=name, grid=grid, in_specs=in_specs, out_specs=out_specs, out_shape=out_shape,
                      scratch_shapes=scratch_shapes, compiler_params=_params(*(semantics or ("parallel",) * len(grid))))(*args)
    n_in, n_out, n_sc = len(in_specs), len(out_specs), len(scratch_shapes)
    r_in, r_out = len(rider["ins"]), len(rider["out_shapes"])

    def wrapped(*refs):
        p = 0
        own_in = refs[p:p + n_in]; p += n_in
        rid_in = refs[p:p + r_in]; p += r_in
        own_out = refs[p:p + n_out]; p += n_out
        rid_out = refs[p:p + r_out]; p += r_out
        own_sc = refs[p:p + n_sc]; p += n_sc
        send_sems, recv_sems = refs[p:p + 2]
        ids = [pl.program_id(a) for a in range(len(grid))]
        first = ids[0] == 0
        last = ids[0] == grid[0] - 1
        for a in range(1, len(grid)):
            first = jnp.logical_and(first, ids[a] == 0)
            last = jnp.logical_and(last, ids[a] == grid[a] - 1)

        @pl.when(first)
        def _():
            rider["start"](rid_in, rid_out, send_sems, recv_sems)

        body(*own_in, *own_out, *own_sc)

        @pl.when(last)
        def _():
            rider["finish"](rid_in, rid_out, send_sems, recv_sems)

    aliases = {n_in + t: n_out + t for t in range(r_in)} if rider.get("in_place") else {}
    res = _pcall_comm(
        wrapped, name=name + "_" + rider["name"], grid=grid, in_specs=list(in_specs) + [ANY] * r_in,
        out_specs=list(out_specs) + [ANY] * r_out, out_shape=list(out_shape) + list(rider["out_shapes"]),
        scratch_shapes=list(scratch_shapes) + [pltpu.SemaphoreType.DMA((rider["n"],)), pltpu.SemaphoreType.DMA((rider["n"],))],
        input_output_aliases=aliases, compiler_params=_params(*(("arbitrary",) * len(grid))),
    )(*args, *rider["ins"])
    return res


def _dil_fwd(proj3, *, D, q_off, k_off, v_off, rider=None):
    B, S, C = proj3.shape
    n_it = S // WIN
    scale = HEAD_DIM ** -0.5
    grid, in_specs, out_spec = _dil_specs(B, S, D, C, (q_off, k_off, v_off))

    def body(q_ref, k_ref, v_ref, o_ref, l_ref):
        lo = lax.broadcasted_iota(jnp.int32, (WIN, LANES), 1) < HEAD_DIM
        for c, d in enumerate(DILS):
            two = S // d > WIN
            bias_all, bias_first = _dil_bias(two)

            def step(it, _, c=c, d=d, two=two, bias_all=bias_all, bias_first=bias_first):
                cur, prev, later = _dil_rows(it, d, S)
                q2 = _pair_stack(q_ref[cur, :] * scale, lo)
                if two:
                    k2 = jnp.concatenate([k_ref[prev, :], k_ref[cur, :]], axis=0).astype(BF16)
                    v2 = jnp.concatenate([v_ref[prev, :], v_ref[cur, :]], axis=0).astype(BF16)
                    bias = jnp.where(later, bias_all, bias_first)
                else:
                    k2, v2, bias = k_ref[cur, :].astype(BF16), v_ref[cur, :].astype(BF16), bias_all
                s2 = _dot_nt(q2, k2) + bias
                m2 = jnp.max(s2, axis=1, keepdims=True)
                p2 = jnp.exp(s2 - m2)
                den = jnp.sum(p2, axis=1, keepdims=True)
                oc = _pair_join(_dot_nn(p2.astype(BF16), v2) / den, lo)
                lc = _pair_bcast(m2 + jnp.log(den), lo)
                if c == 0:
                    o_ref[cur, :] = oc
                    l_ref[cur, :] = lc
                else:
                    l_old = l_ref[cur, :]
                    mx = jnp.maximum(l_old, lc)
                    e_old, e_new = jnp.exp(l_old - mx), jnp.exp(lc - mx)
                    tot = e_old + e_new
                    o_ref[cur, :] = (e_old * o_ref[cur, :] + e_new * oc) / tot
                    l_ref[cur, :] = mx + jnp.log(tot)
                return 0

            lax.fori_loop(0, n_it, step, 0, unroll=16)

    return _call_with_rider(
        rider, body, name="dil_fwd", grid=grid, in_specs=in_specs, out_specs=[out_spec, out_spec],
        out_shape=[jax.ShapeDtypeStruct((B, S, D), F32)] * 2, scratch_shapes=[], args=[proj3, proj3, proj3])


def _dil_bwd(proj3, o3, l3, do3, *, D, q_off, k_off, v_off, rider=None):
    B, S, C = proj3.shape
    n_it = S // WIN
    scale = HEAD_DIM ** -0.5
    grid, in_specs, out_spec = _dil_specs(B, S, D, C, (q_off, k_off, v_off))

    def body(q_ref, k_ref, v_ref, o_ref, l_ref, do_ref, dq_ref, dk_ref, dv_ref, dd_s, dq_s, dk_s, dv_s):
        lo = lax.broadcasted_iota(jnp.int32, (WIN, LANES), 1) < HEAD_DIM
        lo_s = lax.broadcasted_iota(jnp.int32, (S, LANES), 1) < HEAD_DIM
        prod = do_ref[...] * o_ref[...]
        d_lo = jnp.sum(jnp.where(lo_s, prod, 0.0), axis=1, keepdims=True)
        d_hi = jnp.sum(jnp.where(lo_s, 0.0, prod), axis=1, keepdims=True)
        dd_s[...] = jnp.where(lo_s, jnp.broadcast_to(d_lo, (S, LANES)), jnp.broadcast_to(d_hi, (S, LANES)))
        dq_s[...] = jnp.zeros_like(dq_s)
        dk_s[...] = jnp.zeros_like(dk_s)
        dv_s[...] = jnp.zeros_like(dv_s)
        for d in DILS:
            two = S // d > WIN
            bias_all, bias_first = _dil_bias(two)

            def step(it, _, d=d, two=two, bias_all=bias_all, bias_first=bias_first):
                cur, prev, later = _dil_rows(it, d, S)
                q2 = _pair_stack(q_ref[cur, :] * scale, lo)
                do2 = _pair_stack(do_ref[cur, :], lo)
                l2 = _pair_col(l_ref[cur, :])
                dd2 = _pair_col(dd_s[cur, :])
                if two:
                    k2 = jnp.concatenate([k_ref[prev, :], k_ref[cur, :]], axis=0).astype(BF16)
                    v2 = jnp.concatenate([v_ref[prev, :], v_ref[cur, :]], axis=0).astype(BF16)
                    bias = jnp.where(later, bias_all, bias_first)
                else:
                    k2, v2, bias = k_ref[cur, :].astype(BF16), v_ref[cur, :].astype(BF16), bias_all
                p2 = jnp.exp(_dot_nt(q2, k2) + bias - l2)
                ds2 = (p2 * (_dot_nt(do2, v2) - dd2)).astype(BF16)
                dq_s[cur, :] += _pair_join(_dot_nn(ds2, k2), lo) * scale
                dk2 = _dot_tn(ds2, q2)
                dv2 = _dot_tn(p2.astype(BF16), do2)
                if two:
                    dk_s[prev, :] += dk2[:WIN]
                    dv_s[prev, :] += dv2[:WIN]
                    dk_s[cur, :] += dk2[WIN:]
                    dv_s[cur, :] += dv2[WIN:]
                else:
                    dk_s[cur, :] += dk2
                    dv_s[cur, :] += dv2
                return 0

            lax.fori_loop(0, n_it, step, 0, unroll=16)
        dq_ref[...] = dq_s[...].astype(BF16)
        dk_ref[...] = dk_s[...].astype(BF16)
        dv_ref[...] = dv_s[...].astype(BF16)

    return _call_with_rider(
        rider, body, name="dil_bwd", grid=grid, in_specs=in_specs + [out_spec] * 3, out_specs=[out_spec] * 3,
        out_shape=[jax.ShapeDtypeStruct((B, S, D), BF16)] * 3, scratch_shapes=[pltpu.VMEM((S, LANES), F32)] * 4,
        args=[proj3, proj3, proj3, o3, l3, do3])


SWA_HB = 2 * SWA_GROUP


def _to_half(x, src, dst, lo):
    if src != dst:
        x = pltpu.roll(x, HEAD_DIM, 1)
    return jnp.where(lo if dst == 0 else jnp.logical_not(lo), x, 0.0)


def _swa_kv(g):
    return 2 * g // SWA_GROUP


SWA_STACKS = ((0, 1), (2, 3))


def _swa_stack(ref, gs, lo, dtype, rows=slice(None)):
    parts = []
    for g in gs:
        x = ref[rows, g * LANES:(g + 1) * LANES]
        parts += [_to_half(x, 0, _swa_kv(g), lo), _to_half(x, 1, _swa_kv(g), lo)]
    return jnp.concatenate(parts, axis=0).astype(dtype)


def _swa_unstack(y, gs, lo):
    out = []
    for t, g in enumerate(gs):
        even, odd = y[2 * t * WIN:(2 * t + 1) * WIN], y[(2 * t + 1) * WIN:(2 * t + 2) * WIN]
        out.append(_to_half(even, _swa_kv(g), 0, lo) + _to_half(odd, _swa_kv(g), 1, lo))
    return out


def _swa_cols(x, gs):
    cols = []
    for g in gs:
        cols += [jnp.broadcast_to(x[:, 2 * g:2 * g + 1], (WIN, 1)), jnp.broadcast_to(x[:, 2 * g + 1:2 * g + 2], (WIN, 1))]
    return jnp.concatenate(cols, axis=0)


SWA_UNROLL = 8


def _swa_seq_specs(B, S, D, q_off, k_off, v_off):
    qw = SWA_HB * HEAD_DIM
    assert q_off % qw == 0 and k_off % LANES == 0 and v_off % LANES == 0 and D % qw == 0
    seq = lambda width, off: pl.BlockSpec((None, S, width), lambda b, hh: (b, 0, off // width + hh))
    sink = pl.BlockSpec((None, 1, LANES), lambda b, hh: (hh, 0, 0))
    return (B, D // qw), seq, sink, qw


def _swa_rows(it):
    cur = pl.ds(pl.multiple_of(it * WIN, WIN), WIN)
    prev = pl.ds(pl.multiple_of(jnp.maximum(it - 1, 0) * WIN, WIN), WIN)
    return cur, prev, it > 0


def _swa_seq_fwd(proj3, sinks, *, D, q_off, k_off, v_off, rider=None):
    B, S, C = proj3.shape
    scale = HEAD_DIM ** -0.5
    grid, seq, sink, qw = _swa_seq_specs(B, S, D, q_off, k_off, v_off)
    nhb = D // qw

    def body(q_ref, k_ref, v_ref, sk_ref, o_ref, lse_ref):
        lo = lax.broadcasted_iota(jnp.int32, (WIN, LANES), 1) < HEAD_DIM
        lane = lax.broadcasted_iota(jnp.int32, (WIN, LANES), 1)
        sk = sk_ref[...]
        biases = [_dil_bias(True, 2 * len(gs)) for gs in SWA_STACKS]

        def step(it, _):
            cur, prev, later = _swa_rows(it)
            k2 = jnp.concatenate([k_ref[prev, :], k_ref[cur, :]], axis=0).astype(BF16)
            v2 = jnp.concatenate([v_ref[prev, :], v_ref[cur, :]], axis=0).astype(BF16)
            lse_acc = jnp.zeros((WIN, LANES), F32)
            for gs, (bias_all, bias_first) in zip(SWA_STACKS, biases):
                bias = jnp.where(later, bias_all, bias_first)
                qs = _swa_stack(q_ref, gs, lo, BF16, cur)
                sks = _swa_cols(sk, gs)
                s = _dot_nt(qs, k2) * scale + bias
                m = jnp.maximum(jnp.max(s, axis=1, keepdims=True), sks)
                p = jnp.exp(s - m)
                den = jnp.sum(p, axis=1, keepdims=True) + jnp.exp(sks - m)
                for g, grp in zip(gs, _swa_unstack(_dot_nn(p.astype(BF16), v2) / den, gs, lo)):
                    o_ref[cur, g * LANES:(g + 1) * LANES] = grp
                ls = m + jnp.log(den)
                for t, g in enumerate(gs):
                    lse_acc = jnp.where(lane == 2 * g, ls[2 * t * WIN:(2 * t + 1) * WIN], lse_acc)
                    lse_acc = jnp.where(lane == 2 * g + 1, ls[(2 * t + 1) * WIN:(2 * t + 2) * WIN], lse_acc)
            lse_ref[cur, :] = lse_acc
            return 0

        lax.fori_loop(0, S // WIN, step, 0, unroll=SWA_UNROLL)

    return _call_with_rider(
        rider, body, name="swa_fwd", grid=grid,
        in_specs=[seq(qw, q_off), seq(LANES, k_off), seq(LANES, v_off), sink],
        out_specs=[seq(qw, 0), seq(LANES, 0)],
        out_shape=[jax.ShapeDtypeStruct((B, S, D), F32), jax.ShapeDtypeStruct((B, S, nhb * LANES), F32)],
        scratch_shapes=[], args=[proj3, proj3, proj3, sinks])


def _swa_seq_bwd(proj3, o3, lse3, do3, sinks, *, D, q_off, k_off, v_off):
    B, S, C = proj3.shape
    scale = HEAD_DIM ** -0.5
    grid, seq, sink, qw = _swa_seq_specs(B, S, D, q_off, k_off, v_off)
    nhb = D // qw
    KV = D // SWA_GROUP

    def body(q_ref, k_ref, v_ref, o_ref, l_ref, do_ref, sk_ref, dq_ref, dk_ref, dv_ref, dsk_ref, dk_s, dv_s):
        lo = lax.broadcasted_iota(jnp.int32, (WIN, LANES), 1) < HEAD_DIM
        lane = lax.broadcasted_iota(jnp.int32, (1, LANES), 1)
        sk = sk_ref[...]
        biases = [_dil_bias(True, 2 * len(gs)) for gs in SWA_STACKS]
        dk_s[...] = jnp.zeros_like(dk_s)
        dv_s[...] = jnp.zeros_like(dv_s)

        dsk_ref[...] = jnp.zeros_like(dsk_ref)

        def step(it, _):
            cur, prev, later = _swa_rows(it)
            k2 = jnp.concatenate([k_ref[prev, :], k_ref[cur, :]], axis=0).astype(BF16)
            v2 = jnp.concatenate([v_ref[prev, :], v_ref[cur, :]], axis=0).astype(BF16)
            lse = l_ref[cur, :]
            dk2 = jnp.zeros((2 * WIN, LANES), F32)
            dv2 = jnp.zeros((2 * WIN, LANES), F32)
            dsk_acc = jnp.zeros((1, LANES), F32)
            for gs, (bias_all, bias_first) in zip(SWA_STACKS, biases):
                bias = jnp.where(later, bias_all, bias_first)
                qs = _swa_stack(q_ref, gs, lo, BF16, cur)
                dos = _swa_stack(do_ref, gs, lo, BF16, cur)
                dds = []
                for g in gs:
                    prod = do_ref[cur, g * LANES:(g + 1) * LANES] * o_ref[cur, g * LANES:(g + 1) * LANES]
                    dds += [jnp.sum(jnp.where(lo, prod, 0.0), axis=1, keepdims=True),
                            jnp.sum(jnp.where(lo, 0.0, prod), axis=1, keepdims=True)]
                dds = jnp.concatenate(dds, axis=0)
                ls = _swa_cols(lse, gs)
                ps = jnp.exp(_dot_nt(qs, k2) * scale + bias - ls)
                dss = (ps * (_dot_nt(dos, v2) - dds) * scale).astype(BF16)
                for g, grp in zip(gs, _swa_unstack(_dot_nn(dss, k2), gs, lo)):
                    dq_ref[cur, g * LANES:(g + 1) * LANES] = grp.astype(BF16)
                dk2 = dk2 + _dot_tn(dss, qs)
                dv2 = dv2 + _dot_tn(ps.astype(BF16), dos)
                dsks = jnp.exp(_swa_cols(sk, gs) - ls) * dds
                for t, g in enumerate(gs):
                    for u in range(2):
                        rows = slice((2 * t + u) * WIN, (2 * t + u + 1) * WIN)
                        dsk_acc = dsk_acc + jnp.where(lane == 2 * g + u, -jnp.sum(dsks[rows], axis=0, keepdims=True), 0.0)
            dk_s[prev, :] += dk2[:WIN]
            dv_s[prev, :] += dv2[:WIN]
            dk_s[cur, :] += dk2[WIN:]
            dv_s[cur, :] += dv2[WIN:]
            dsk_ref[...] += dsk_acc
            return 0

        lax.fori_loop(0, S // WIN, step, 0, unroll=SWA_UNROLL)
        dk_ref[...] = dk_s[...].astype(BF16)
        dv_ref[...] = dv_s[...].astype(BF16)

    return _pcall(
        body, name="swa_bwd", grid=grid,
        in_specs=[seq(qw, q_off), seq(LANES, k_off), seq(LANES, v_off), seq(qw, 0), seq(LANES, 0), seq(qw, 0), sink],
        out_specs=[seq(qw, 0), seq(LANES, 0), seq(LANES, 0), pl.BlockSpec((None, None, 1, LANES), lambda b, hh: (b, hh, 0, 0))],
        out_shape=[jax.ShapeDtypeStruct((B, S, D), BF16), jax.ShapeDtypeStruct((B, S, KV), BF16),
                   jax.ShapeDtypeStruct((B, S, KV), BF16), jax.ShapeDtypeStruct((B, nhb, 1, LANES), F32)],
        scratch_shapes=[pltpu.VMEM((S, LANES), F32), pltpu.VMEM((S, LANES), F32)],
        compiler_params=_params("parallel", "parallel"),
    )(proj3, proj3, proj3, o3, lse3, do3, sinks)


def _branch_fwd(ys, wb, proj, *, D, g_off, rider=None):
    T = proj.shape[0]
    tm, tn = _tile(T, 256), _tile(D, 512)
    n = len(ys)

    def body(*refs):
        y_refs, w_ref, g_refs, br_ref, mg_ref = refs[:n], refs[n], refs[n + 1:2 * n + 1], refs[2 * n + 1], refs[2 * n + 2]
        acc = None
        for k in range(n):
            br = _dot_nn(y_refs[k][...].astype(BF16), w_ref[k])
            br_ref[k] = br
            term = _sigmoid(g_refs[k][...]) * br
            acc = term if acc is None else acc + term
        mg_ref[...] = acc.astype(BF16)

    gate = lambda k: pl.BlockSpec((tm, tn), lambda i, j: (i, (g_off + k * D) // tn + j))
    return _call_with_rider(
        rider, body, name="branch_fwd", grid=(T // tm, D // tn),
        in_specs=[pl.BlockSpec((tm, D), lambda i, j: (i, 0))] * n + [pl.BlockSpec((n, D, tn), lambda i, j: (0, 0, j))]
        + [gate(k) for k in range(n)],
        out_specs=[pl.BlockSpec((n, tm, tn), lambda i, j: (0, i, j)), pl.BlockSpec((tm, tn), lambda i, j: (i, j))],
        out_shape=[jax.ShapeDtypeStruct((n, T, D), F32), jax.ShapeDtypeStruct((T, D), BF16)],
        scratch_shapes=[], args=[*ys, wb, *([proj] * n)])


def _branch_bwd(dmerged, branch, proj, *, D, g_off):
    n, T, _ = branch.shape
    tm, tn = _tile(T, 512), _tile(D, 512)

    def body(dm_ref, br_ref, *rest):
        g_refs, db_ref, dg_refs = rest[:n], rest[n], rest[n + 1:]
        dm = dm_ref[...]
        for k in range(n):
            sg = _sigmoid(g_refs[k][...])
            db_ref[k] = (sg * dm).astype(BF16)
            dg_refs[k][...] = (dm * br_ref[k] * sg * (1.0 - sg)).astype(BF16)

    gate = lambda k: pl.BlockSpec((tm, tn), lambda i, j: (i, (g_off + k * D) // tn + j))
    blk = pl.BlockSpec((tm, tn), lambda i, j: (i, j))
    res = _pcall(
        body, name="branch_bwd", grid=(T // tm, D // tn),
        in_specs=[blk, pl.BlockSpec((n, tm, tn), lambda i, j: (0, i, j))] + [gate(k) for k in range(n)],
        out_specs=[pl.BlockSpec((n, tm, tn), lambda i, j: (0, i, j))] + [blk] * n,
        out_shape=[jax.ShapeDtypeStruct((n, T, D), BF16)] + [jax.ShapeDtypeStruct((T, D), BF16)] * n,
        compiler_params=_params("parallel", "parallel"),
    )(dmerged, branch, *([proj] * n))
    return res[0], list(res[1:])


def _ln_bwd(dout, z, g):
    T, D = z.shape
    tm = _tile(T, 512)

    def body(do_ref, z_ref, g_ref, dz_ref, dg_ref, db_ref):
        z = z_ref[...]
        do = do_ref[...]
        mu = jnp.mean(z, axis=1, keepdims=True)
        zc = z - mu
        rstd = lax.rsqrt(jnp.mean(zc * zc, axis=1, keepdims=True) + LN_EPS)
        xhat = zc * rstd
        dxh = do * g_ref[...]
        dz_ref[...] = rstd * (dxh - jnp.mean(dxh, axis=1, keepdims=True) - xhat * jnp.mean(dxh * xhat, axis=1, keepdims=True))
        dg = jnp.sum(do * xhat, axis=0, keepdims=True)
        db = jnp.sum(do, axis=0, keepdims=True)
        first = pl.program_id(0) == 0

        @pl.when(first)
        def _():
            dg_ref[...] = dg
            db_ref[...] = db

        @pl.when(jnp.logical_not(first))
        def _():
            dg_ref[...] += dg
            db_ref[...] += db

    blk = pl.BlockSpec((tm, D), lambda i: (i, 0))
    vec = pl.BlockSpec((1, D), lambda i: (0, 0))
    return _pcall(
        body, name="ln_bwd", grid=(T // tm,), in_specs=[blk, blk, vec], out_specs=[blk, vec, vec],
        out_shape=[jax.ShapeDtypeStruct((T, D), F32), jax.ShapeDtypeStruct((1, D), F32), jax.ShapeDtypeStruct((1, D), F32)],
        compiler_params=_params("arbitrary"),
    )(dout, z, g)


def _swiglu_fwd(hh):
    T, F2 = hh.shape
    Fh = F2 // 2
    tm, tn = _tile(T, 256), _tile(Fh, 1408)
    nj = Fh // tn

    def body(h1_ref, h3_ref, f_ref):
        h1 = h1_ref[...]
        f_ref[...] = (h1 * _sigmoid(h1) * h3_ref[...]).astype(BF16)

    return _pcall(
        body, name="swiglu_fwd", grid=(T // tm, nj),
        in_specs=[pl.BlockSpec((tm, tn), lambda i, j: (i, j)), pl.BlockSpec((tm, tn), lambda i, j: (i, nj + j))],
        out_specs=pl.BlockSpec((tm, tn), lambda i, j: (i, j)),
        out_shape=jax.ShapeDtypeStruct((T, Fh), BF16), compiler_params=_params("parallel", "parallel"),
    )(hh, hh)


def _swiglu_bwd(hh, df, rider=None):
    T, F2 = hh.shape
    Fh = F2 // 2
    tm, tn = _tile(T, 256), _tile(Fh, 1408)
    nj = Fh // tn

    def body(h1_ref, h3_ref, df_ref, d1_ref, d3_ref):
        h1 = h1_ref[...]
        sg = _sigmoid(h1)
        d = df_ref[...]
        d1_ref[...] = (d * h3_ref[...] * sg * (1.0 + h1 * (1.0 - sg))).astype(BF16)
        d3_ref[...] = (d * h1 * sg).astype(BF16)

    lo = pl.BlockSpec((tm, tn), lambda i, j: (i, j))
    hi = pl.BlockSpec((tm, tn), lambda i, j: (i, nj + j))
    res = _call_with_rider(
        rider, body, name="swiglu_bwd", grid=(T // tm, nj), in_specs=[lo, hi, lo], out_specs=[lo, lo],
        out_shape=[jax.ShapeDtypeStruct((T, Fh), BF16)] * 2, scratch_shapes=[], args=[hh, hh, df])
    return jnp.concatenate([res[0], res[1]], axis=1), list(res[2:])


def _loss_head(y, target):
    T, D = y.shape
    tm = _tile(T, 512)

    def body(y_ref, t_ref, dy_ref, l_ref):
        e = y_ref[...] - t_ref[...]
        dy_ref[...] = e * (1.0 / D)
        sq = e * e
        part = sq[:, 0:LANES]
        for c in range(1, D // LANES):
            part = part + sq[:, c * LANES:(c + 1) * LANES]
        part = jnp.sum(part, axis=0, keepdims=True) * (0.5 / D)
        first = pl.program_id(0) == 0

        @pl.when(first)
        def _():
            l_ref[...] = part

        @pl.when(jnp.logical_not(first))
        def _():
            l_ref[...] += part

    blk = pl.BlockSpec((tm, D), lambda i: (i, 0))
    return _pcall(
        body, name="loss_head", grid=(T // tm,), in_specs=[blk, blk],
        out_specs=[blk, pl.BlockSpec((1, LANES), lambda i: (0, 0))],
        out_shape=[jax.ShapeDtypeStruct((T, D), F32), jax.ShapeDtypeStruct((1, LANES), F32)],
        compiler_params=_params("arbitrary"),
    )(y, target)


def _as_rows(a):
    return a.reshape(-1, a.shape[-1])


def _adamw(w, g, m, v, rider=None):
    w2, g2, m2, v2 = (_as_rows(t) for t in (w, g, m, v))
    R, Cc = w2.shape
    cap = max(SUBLANES, min(512, (256 * 1024) // Cc))
    tm = R if (R <= cap or R % SUBLANES) else max(t for t in range(SUBLANES, cap + 1, SUBLANES) if R % t == 0)
    c1 = 1.0 - ADAM_B1 ** ADAM_STEP
    c2 = 1.0 - ADAM_B2 ** ADAM_STEP

    def body(w_ref, g_ref, m_ref, v_ref, d_ref, nm_ref, nv_ref):
        gg = g_ref[...]
        nm = ADAM_B1 * m_ref[...] + (1.0 - ADAM_B1) * gg
        nv = ADAM_B2 * v_ref[...] + (1.0 - ADAM_B2) * (gg * gg)
        d_ref[...] = (-ADAM_LR) * ((nm / c1) / (jnp.sqrt(nv / c2) + ADAM_EPS) + ADAM_WD * w_ref[...])
        nm_ref[...] = nm
        nv_ref[...] = nv

    blk = pl.BlockSpec((tm, Cc), lambda i: (i, 0))
    res = _call_with_rider(
        rider, body, name="adamw", grid=(R // tm,), in_specs=[blk] * 4, out_specs=[blk] * 3,
        out_shape=[jax.ShapeDtypeStruct((R, Cc), F32)] * 3, scratch_shapes=[], args=[w2, g2, m2, v2])
    return tuple(t.reshape(w.shape) for t in res[:3]) + tuple(res[3:])


def _where_am_i():
    x, y, c = lax.axis_index("x"), lax.axis_index("y"), lax.axis_index("c")
    chips = [(1 - x, y), (x, 1 - y), (1 - x, 1 - y)]
    return x, y, c, chips


def _remote(src, dst, send_sems, recv_sems, k, to):
    return pltpu.make_async_remote_copy(src_ref=src, dst_ref=dst, send_sem=send_sems.at[k], recv_sem=recv_sems.at[k],
                                        device_id=to, device_id_type=MESH)


def _comm_call(body, name, ins, out_shapes, n_remote, n_local):
    return _pcall_comm(
        body, name=name, in_specs=[ANY] * len(ins), out_specs=[ANY] * len(out_shapes), out_shape=out_shapes,
        scratch_shapes=[pltpu.SemaphoreType.DMA((n_remote,)), pltpu.SemaphoreType.DMA((n_remote,)),
                        pltpu.SemaphoreType.DMA((max(n_local, 1),))],
    )(*ins)


def _gather_weights(shards):
    n = len(shards)

    def body(*refs):
        ins, outs = refs[:n], refs[n:2 * n]
        send_sems, recv_sems, local_sems = refs[2 * n:]
        x, y, c, chips = _where_am_i()
        s = 2 * x + y
        sib = (x, y, 1 - c)
        first = []
        for t in range(n):
            for j, (cx, cy) in enumerate(chips):
                first.append(_remote(ins[t].at[:, c], outs[t].at[:, s, c], send_sems, recv_sems, 6 * t + j, (cx, cy, c)))
        for cp in first:
            cp.start()
        passed = []
        for j, (cx, cy) in enumerate(chips):
            sj = 2 * cx + cy
            for t in range(n):
                land = outs[t].at[:, sj, c]
                _remote(land, land, send_sems, recv_sems, 6 * t + j, (cx, cy, c)).wait_recv()
                fw = _remote(land, land, send_sems, recv_sems, 6 * t + 3 + j, sib)
                fw.start()
                passed.append(fw)
        for j, (cx, cy) in enumerate(chips):
            sj = 2 * cx + cy
            for t in range(n):
                land = outs[t].at[:, sj, 1 - c]
                _remote(land, land, send_sems, recv_sems, 6 * t + 3 + j, sib).wait_recv()
        for cp in first + passed:
            cp.wait_send()

    out_shapes = [jax.ShapeDtypeStruct((t.shape[0], N_CHIPS) + t.shape[1:], t.dtype) for t in shards]
    got = _comm_call(body, "gather_weights", shards, out_shapes, 6 * n, 0)
    s = 2 * lax.axis_index("x") + lax.axis_index("y")
    return [lax.dynamic_update_slice(g, t[:, None], (0, s, 0, 0, 0)) for g, t in zip(got, shards)]


def _gather_rider(shards):
    n = len(shards)

    def copies(ins, outs, send_sems, recv_sems):
        x, y, c, chips = _where_am_i()
        s = 2 * x + y
        return [_remote(ins[t].at[:, c], outs[t].at[:, s, c], send_sems, recv_sems, 3 * t + j, (cx, cy, c))
                for t in range(n) for j, (cx, cy) in enumerate(chips)]

    def start(ins, outs, send_sems, recv_sems):
        for cp in copies(ins, outs, send_sems, recv_sems):
            cp.start()

    def finish(ins, outs, send_sems, recv_sems):
        x, y, c, chips = _where_am_i()
        for t in range(n):
            for j, (cx, cy) in enumerate(chips):
                land = outs[t].at[:, 2 * cx + cy, c]
                _remote(land, land, send_sems, recv_sems, 3 * t + j, (cx, cy, c)).wait_recv()
        for cp in copies(ins, outs, send_sems, recv_sems):
            cp.wait_send()

    out_shapes = [jax.ShapeDtypeStruct((t.shape[0], N_CHIPS) + t.shape[1:], t.dtype) for t in shards]
    return dict(name="gather", ins=list(shards), out_shapes=out_shapes, n=3 * n, start=start, finish=finish)


def _gather_forward(landed, shards):
    n = len(landed)

    def body(*refs):
        outs = refs[n:2 * n]
        send_sems, recv_sems, _ = refs[2 * n:]
        x, y, c, chips = _where_am_i()
        sib = (x, y, 1 - c)
        cps = []
        for t in range(n):
            for j, (cx, cy) in enumerate(chips):
                land = outs[t].at[:, 2 * cx + cy, c]
                cps.append(_remote(land, land, send_sems, recv_sems, 3 * t + j, sib))
        for cp in cps:
            cp.start()
        for t in range(n):
            for j, (cx, cy) in enumerate(chips):
                land = outs[t].at[:, 2 * cx + cy, 1 - c]
                _remote(land, land, send_sems, recv_sems, 3 * t + j, sib).wait_recv()
        for cp in cps:
            cp.wait_send()

    got = _pcall_comm(
        body, name="gather_forward", in_specs=[ANY] * n, out_specs=[ANY] * n,
        out_shape=[jax.ShapeDtypeStruct(t.shape, t.dtype) for t in landed], input_output_aliases={t: t for t in range(n)},
        scratch_shapes=[pltpu.SemaphoreType.DMA((3 * n,)), pltpu.SemaphoreType.DMA((3 * n,)), pltpu.SemaphoreType.DMA((1,))],
    )(*landed)
    s = 2 * lax.axis_index("x") + lax.axis_index("y")
    return [lax.dynamic_update_slice(g, t[:, None], (0, s, 0, 0, 0)) for g, t in zip(got, shards)]


def _forward_rider(landed):
    n = len(landed)

    def copies(outs, send_sems, recv_sems):
        x, y, c, chips = _where_am_i()
        cps = []
        for t in range(n):
            for j, (cx, cy) in enumerate(chips):
                land = outs[t].at[:, 2 * cx + cy, c]
                cps.append(_remote(land, land, send_sems, recv_sems, 3 * t + j, (x, y, 1 - c)))
        return cps

    def start(ins, outs, send_sems, recv_sems):
        for cp in copies(outs, send_sems, recv_sems):
            cp.start()

    def finish(ins, outs, send_sems, recv_sems):
        x, y, c, chips = _where_am_i()
        for t in range(n):
            for j, (cx, cy) in enumerate(chips):
                land = outs[t].at[:, 2 * cx + cy, 1 - c]
                _remote(land, land, send_sems, recv_sems, 3 * t + j, (x, y, 1 - c)).wait_recv()
        for cp in copies(outs, send_sems, recv_sems):
            cp.wait_send()

    out_shapes = [jax.ShapeDtypeStruct(t.shape, t.dtype) for t in landed]
    return dict(name="forward", ins=list(landed), out_shapes=out_shapes, n=3 * n, start=start, finish=finish, in_place=True)


def _place_own_shard(got, shards):
    s = 2 * lax.axis_index("x") + lax.axis_index("y")
    return [lax.dynamic_update_slice(g, t[:, None], (0, s, 0, 0, 0)) for g, t in zip(got, shards)]


def _gather_small(v):
    def body(v_ref, out_ref, send_sems, recv_sems, local_sems):
        x, y, c, chips = _where_am_i()
        s = 2 * x + y
        mine = pltpu.make_async_copy(v_ref, out_ref.at[s], local_sems.at[0])
        mine.start()
        sends = [_remote(v_ref, out_ref.at[s], send_sems, recv_sems, j, (cx, cy, c)) for j, (cx, cy) in enumerate(chips)]
        for cp in sends:
            cp.start()
        for j, (cx, cy) in enumerate(chips):
            land = out_ref.at[2 * cx + cy]
            _remote(land, land, send_sems, recv_sems, j, (cx, cy, c)).wait_recv()
        for cp in sends:
            cp.wait_send()
        mine.wait()

    return _comm_call(body, "gather_small", [v], [jax.ShapeDtypeStruct((N_CHIPS,) + v.shape, v.dtype)], 3, 1)[0]


def _swap_sibling_halves(grads):
    n = len(grads)

    def body(*refs):
        ins, outs = refs[:n], refs[n:2 * n]
        send_sems, recv_sems, _ = refs[2 * n:]
        x, y, c, _chips = _where_am_i()
        sib = (x, y, 1 - c)
        cps = [_remote(ins[t].at[:, :, 1 - c], outs[t], send_sems, recv_sems, t, sib) for t in range(n)]
        for cp in cps:
            cp.start()
        for cp in cps:
            cp.wait()

    out_shapes = [jax.ShapeDtypeStruct(g.shape[:2] + g.shape[3:], g.dtype) for g in grads]
    return _comm_call(body, "grad_swap_halves", grads, out_shapes, n, 0)


def _exchange_chips(parts):
    n = len(parts)

    def body(*refs):
        ins, outs = refs[:n], refs[n:2 * n]
        send_sems, recv_sems, _ = refs[2 * n:]
        x, y, c, chips = _where_am_i()
        cps = []
        for t in range(n):
            for j, (cx, cy) in enumerate(chips):
                cps.append(_remote(ins[t].at[:, 2 * cx + cy], outs[t].at[j], send_sems, recv_sems, 3 * t + j, (cx, cy, c)))
        for cp in cps:
            cp.start()
        for cp in cps:
            cp.wait()

    out_shapes = [jax.ShapeDtypeStruct((3, p.shape[0]) + p.shape[2:], p.dtype) for p in parts]
    return _comm_call(body, "grad_exchange_chips", parts, out_shapes, 3 * n, 0)


def _exchange_rider(parts):
    n = len(parts)

    def copies(ins, outs, send_sems, recv_sems):
        x, y, c, chips = _where_am_i()
        return [_remote(ins[t].at[:, 2 * cx + cy], outs[t].at[j], send_sems, recv_sems, 3 * t + j, (cx, cy, c))
                for t in range(n) for j, (cx, cy) in enumerate(chips)]

    def start(ins, outs, send_sems, recv_sems):
        for cp in copies(ins, outs, send_sems, recv_sems):
            cp.start()

    def finish(ins, outs, send_sems, recv_sems):
        for cp in copies(ins, outs, send_sems, recv_sems):
            cp.wait()

    out_shapes = [jax.ShapeDtypeStruct((3, p.shape[0]) + p.shape[2:], p.dtype) for p in parts]
    return dict(name="exchange", ins=list(parts), out_shapes=out_shapes, n=3 * n, start=start, finish=finish)


def _join_sibling_halves(halves):
    n = len(halves)

    def body(*refs):
        ins, outs = refs[:n], refs[n:2 * n]
        send_sems, recv_sems, local_sems = refs[2 * n:]
        x, y, c, _chips = _where_am_i()
        sib = (x, y, 1 - c)
        cps = [_remote(ins[t], outs[t].at[:, c], send_sems, recv_sems, t, sib) for t in range(n)]
        for cp in cps:
            cp.start()
        for t in range(n):
            land = outs[t].at[:, 1 - c]
            _remote(land, land, send_sems, recv_sems, t, sib).wait_recv()
        for cp in cps:
            cp.wait_send()

    out_shapes = [jax.ShapeDtypeStruct((h.shape[0], 2) + h.shape[1:], h.dtype) for h in halves]
    got = _comm_call(body, "grad_join_halves", halves, out_shapes, n, 0)
    c = lax.axis_index("c")
    return [lax.dynamic_update_slice(g, h[:, None], (0, c, 0, 0)) for g, h in zip(got, halves)]


def _join_rider(halves):
    n = len(halves)

    def copies(ins, outs, send_sems, recv_sems):
        x, y, c, _chips = _where_am_i()
        return [_remote(ins[t], outs[t].at[:, c], send_sems, recv_sems, t, (x, y, 1 - c)) for t in range(n)]

    def start(ins, outs, send_sems, recv_sems):
        for cp in copies(ins, outs, send_sems, recv_sems):
            cp.start()

    def finish(ins, outs, send_sems, recv_sems):
        x, y, c, _chips = _where_am_i()
        for t in range(n):
            land = outs[t].at[:, 1 - c]
            _remote(land, land, send_sems, recv_sems, t, (x, y, 1 - c)).wait_recv()
        for cp in copies(ins, outs, send_sems, recv_sems):
            cp.wait_send()

    out_shapes = [jax.ShapeDtypeStruct((h.shape[0], 2) + h.shape[1:], h.dtype) for h in halves]
    return dict(name="join", ins=list(halves), out_shapes=out_shapes, n=n, start=start, finish=finish)


def _place_own_half(got, halves):
    c = lax.axis_index("c")
    return [lax.dynamic_update_slice(g, h[:, None], (0, c, 0, 0)) for g, h in zip(got, halves)]


def _small_exchange_rider(v):
    def copies(ins, outs, send_sems, recv_sems):
        x, y, c, chips = _where_am_i()
        return [_remote(ins[0], outs[0].at[j], send_sems, recv_sems, j, (cx, cy, c)) for j, (cx, cy) in enumerate(chips)]

    def start(ins, outs, send_sems, recv_sems):
        for cp in copies(ins, outs, send_sems, recv_sems):
            cp.start()

    def finish(ins, outs, send_sems, recv_sems):
        for cp in copies(ins, outs, send_sems, recv_sems):
            cp.wait()

    return dict(name="small_exchange", ins=[v], out_shapes=[jax.ShapeDtypeStruct((3,) + v.shape, v.dtype)], n=3,
                start=start, finish=finish)


def _swap_small(v):
    def body(v_ref, out_ref, send_sems, recv_sems, _):
        x, y, c, _chips = _where_am_i()
        cp = _remote(v_ref, out_ref, send_sems, recv_sems, 0, (x, y, 1 - c))
        cp.start()
        cp.wait()

    return _comm_call(body, "small_swap", [v], [jax.ShapeDtypeStruct(v.shape, v.dtype)], 1, 0)[0]


def _sum_rows(name, terms, out_dtypes):
    R, Cc = terms[0].shape
    tm = R if R <= 256 else max(t for t in range(16, 257, 16) if R % t == 0)
    n = len(terms)

    def body(*refs):
        acc = refs[0][...].astype(F32)
        for r in refs[1:n]:
            acc = acc + r[...].astype(F32)
        for o in refs[n:]:
            o[...] = acc.astype(o.dtype)

    blk = pl.BlockSpec((tm, Cc), lambda i: (i, 0))
    return _pcall(
        body, name=name, grid=(R // tm,), in_specs=[blk] * n, out_specs=[blk] * len(out_dtypes),
        out_shape=[jax.ShapeDtypeStruct((R, Cc), d) for d in out_dtypes], compiler_params=_params("parallel"),
    )(*terms)


def _pair_sum(g5, r1, core, shard):
    A4, _, Rh, Cc = g5.shape
    A = A4 // N_CHIPS
    tr = Rh if Rh <= 256 else max(t for t in range(16, 257, 16) if Rh % t == 0)

    def body(core_ref, shard_ref, g_ref, r_ref, qb_ref, qf_ref):
        q = g_ref[...] + r_ref[...]
        qb_ref[...] = q.astype(BF16)

        @pl.when(pl.program_id(2) == shard_ref[0])
        def _():
            qf_ref[...] = q

    grid_spec = pltpu.PrefetchScalarGridSpec(
        num_scalar_prefetch=2, grid=(A, Rh // tr, N_CHIPS),
        in_specs=[pl.BlockSpec((None, None, tr, Cc), lambda a, r, sh, core, shard: (a * N_CHIPS + sh, core[0], r, 0)),
                  pl.BlockSpec((None, tr, Cc), lambda a, r, sh, core, shard: (a * N_CHIPS + sh, r, 0))],
        out_specs=[pl.BlockSpec((None, tr, Cc), lambda a, r, sh, core, shard: (a * N_CHIPS + sh, r, 0)),
                   pl.BlockSpec((None, tr, Cc), lambda a, r, sh, core, shard: (a, r, 0))],
    )
    return _pcall(
        body, name="grad_pair_sum", grid_spec=grid_spec,
        out_shape=[jax.ShapeDtypeStruct((A4, Rh, Cc), BF16), jax.ShapeDtypeStruct((A, Rh, Cc), F32)],
        compiler_params=_params("parallel", "parallel", "arbitrary"),
    )(core, shard, g5, r1)


def _swap_rider(grads):
    n = len(grads)

    def copies(ins, outs, send_sems, recv_sems):
        x, y, c, _chips = _where_am_i()
        return [_remote(ins[t].at[:, :, 1 - c], outs[t], send_sems, recv_sems, t, (x, y, 1 - c)) for t in range(n)]

    def start(ins, outs, send_sems, recv_sems):
        for cp in copies(ins, outs, send_sems, recv_sems):
            cp.start()

    def finish(ins, outs, send_sems, recv_sems):
        for cp in copies(ins, outs, send_sems, recv_sems):
            cp.wait()

    out_shapes = [jax.ShapeDtypeStruct(g.shape[:2] + g.shape[3:], g.dtype) for g in grads]
    return dict(name="swap", ins=list(grads), out_shapes=out_shapes, n=n, start=start, finish=finish)


def _reduce_chip(grads, r1, core, shard):
    qb, qf = [], []
    for g, r in zip(grads, r1):
        A, _, _, Rh, Cc = g.shape
        b, f = _pair_sum(g.reshape(A * N_CHIPS, 2, Rh, Cc), r.reshape(A * N_CHIPS, Rh, Cc), core, shard)
        qb.append(b.reshape(A, N_CHIPS, Rh, Cc))
        qf.append(f)
    return qb, qf


def _reduce_finish(qf, r2):
    return _as_shards(_join_sibling_halves(_chip_sums(qf, r2)))


def _chip_sums(qf, r2):
    halves = []
    for f, r in zip(qf, r2):
        A, Rh, Cc = f.shape
        terms = [f.reshape(A * Rh, Cc)] + [r[j].reshape(A * Rh, Cc) for j in range(3)]
        halves.append(_sum_rows("grad_chip_sum", terms, [F32])[0].reshape(A, Rh, Cc))
    return halves


def _as_shards(full):
    return [t.reshape(t.shape[0], 2 * t.shape[2], t.shape[3]) for t in full]


def _small_pair(v):
    return _sum_rows("small_pair_sum", [v, _swap_small(v)], [F32])[0]


def _small_chip_sum(pair, others):
    x, y = lax.axis_index("x"), lax.axis_index("y")
    s = 2 * x + y
    stack = jnp.concatenate([pair[None], others], axis=0)
    src = jnp.stack([s, s ^ 2, s ^ 1, s ^ 3])
    order = jnp.argsort(src)
    terms = [lax.dynamic_index_in_dim(stack, order[k], 0, keepdims=False) for k in range(N_CHIPS)]
    return _sum_rows("small_chip_sum", terms, [F32])[0]


def _block_diag(w):
    nb, bw, _ = w.shape
    per = LANES // bw
    w = w.reshape(nb // per, per, bw, bw)
    eye = jnp.eye(per, dtype=w.dtype)
    bd = jnp.einsum("tpij,pq->tpiqj", w, eye).reshape(nb // per, LANES, LANES)
    return bd.astype(BF16)


def _block_diag_grad(g, bw):
    nt = g.shape[0]
    per = LANES // bw
    g = g.reshape(nt, per, bw, per, bw)
    return jnp.stack([g[:, p, :, p, :] for p in range(per)], axis=1).reshape(nt * per, bw, bw)


def _split5(w):
    R, Cc = w.shape[-2:]
    return w.reshape(-1, 2, R // 2, Cc)


def kernel(x, w_in, conv_w, conv_b, w_rg, b_rg, w_ig, b_ig, lru_lambda, sinks, w_branch, w_out, ln1_g, ln1_b, w_ffn_in, w_ffn_out, ln2_g, ln2_b, loss_target, m_w_in, m_conv_w, m_conv_b, m_w_rg, m_b_rg, m_w_ig, m_b_ig, m_lru_lambda, m_sinks, m_w_branch, m_w_out, m_ln1_g, m_ln1_b, m_w_ffn_in, m_w_ffn_out, m_ln2_g, m_ln2_b, v_w_in, v_conv_w, v_conv_b, v_w_rg, v_b_rg, v_w_ig, v_b_ig, v_lru_lambda, v_sinks, v_w_branch, v_w_out, v_ln1_g, v_ln1_b, v_w_ffn_in, v_w_ffn_out, v_ln2_g, v_ln2_b):
    B, S, D = x.shape
    T = B * S
    L = w_in.shape[0]
    H = D // HEAD_DIM
    KVB = D // SWA_GROUP
    FH = w_ffn_out.shape[1] * N_CHIPS
    C = w_in.shape[2] * N_CHIPS
    alpha = (2.0 * L) ** 0.25
    off = {}
    pos = 0
    for nm, wd in (("lx", D), ("lg", D), ("qb", D), ("kb", KVB), ("vb", KVB), ("qc", D), ("kc", D), ("vc", D), ("gt", 3 * D)):
        off[nm] = pos
        pos += wd
    assert pos == C
    cx, cy, cc = lax.axis_index("x"), lax.axis_index("y"), lax.axis_index("c")
    shard = (2 * cx + cy).astype(jnp.int32)
    core_a = cc.astype(jnp.int32).reshape(1)
    shard_a = shard.reshape(1)

    def shard_views(l):
        return [_split5(w_in[l].astype(BF16)), _split5(w_branch[l].astype(BF16)), _split5(w_out[l].astype(BF16)),
                _split5(w_ffn_in[l].astype(BF16)), _split5(w_ffn_out[l].astype(BF16))]

    def as_weights(g):
        return dict(
            w_in=g[0].reshape(N_CHIPS, D, C // N_CHIPS),
            w_branch=g[1].reshape(3, D, D),
            w_out=g[2].reshape(D, D),
            w_ffn_in=g[3].reshape(N_CHIPS, D, 2 * FH // N_CHIPS),
            w_ffn_out=g[4].reshape(FH, D),
        )

    first_views = shard_views(0)
    w_in0 = _gather_weights(first_views[:1])
    full = [dict(w_in=w_in0[0].reshape(N_CHIPS, D, C // N_CHIPS))]
    cw_all = _gather_small(conv_w.reshape(L * CONV_WIDTH, D // N_CHIPS))
    conv_w_full = jnp.transpose(cw_all, (1, 0, 2)).reshape(L, CONV_WIDTH, D)

    def layer_params(l):
        return dict(conv_w=conv_w_full[l], conv_b=conv_b[l][None], w_rg_bd=_block_diag(w_rg[l]), b_rg=b_rg[l][None],
                    w_ig_bd=_block_diag(w_ig[l]), b_ig=b_ig[l][None], lam=lru_lambda[l][None])

    def sink_rows(l, hb):
        sk = sinks[l].reshape(H // hb, 1, hb)
        return jnp.pad(sk, ((0, 0), (0, 0), (0, LANES - hb)))

    hb_b = SWA_HB

    saved = []
    xin = x.reshape(T, D)
    for l in range(L):
        fw, lp = full[l], layer_params(l)
        proj = _matmul(xin, fw["w_in"], mode="nn", name="mm_proj", tm=512, n_outer=True)
        proj3 = proj.reshape(B, S, C)
        h3, ya3 = _lru_fwd(proj3, lp, D=D, x_off=off["lx"], g_off=off["lg"])
        skr = sink_rows(l, hb_b)
        swa_kw = dict(D=D, q_off=off["qb"], k_off=off["kb"], v_off=off["vb"])
        nxt = shard_views(l + 1) if l + 1 < L else None
        landed = []
        if l == 0:
            res = _swa_seq_fwd(proj3, skr, rider=_gather_rider(first_views[1:]), **swa_kw)
            yb3, lse_b = res[0], res[1]
            fw = as_weights(w_in0 + _gather_forward(res[2:], first_views[1:]))
            full[0] = fw
        elif nxt is not None:
            res = _swa_seq_fwd(proj3, skr, rider=_gather_rider(nxt[:1]), **swa_kw)
            yb3, lse_b = res[0], res[1]
            landed = list(res[2:])
        else:
            yb3, lse_b = _swa_seq_fwd(proj3, skr, **swa_kw)
        dil_kw = dict(D=D, q_off=off["qc"], k_off=off["kc"], v_off=off["vc"])
        if nxt is not None:
            rest = nxt[len(landed):]
            res = _dil_fwd(proj3, rider=_gather_rider(rest), **dil_kw)
            yc3, lse_c = res[0], res[1]
            landed = landed + list(res[2:])
        else:
            yc3, lse_c = _dil_fwd(proj3, **dil_kw)
        ya, yb, yc = ya3.reshape(T, D), yb3.reshape(T, D), yc3.reshape(T, D)
        if nxt is not None:
            res = _branch_fwd([ya, yb, yc], fw["w_branch"], proj, D=D, g_off=off["gt"], rider=_forward_rider(landed))
            branch, merged = res[0], res[1]
            full.append(as_weights(_place_own_shard(res[2:], nxt)))
        else:
            branch, merged = _branch_fwd([ya, yb, yc], fw["w_branch"], proj, D=D, g_off=off["gt"])
        z1, x1 = _matmul(merged, fw["w_out"], mode="nn", name="mm_out_ln", tn=1024, resid=xin, rs=alpha,
                         ln=(ln1_g[l][None], ln1_b[l][None]))
        hh = _matmul(x1, fw["w_ffn_in"], mode="nn", name="mm_ffn_in", n_outer=True)
        f = _swiglu_fwd(hh)
        z2, x2 = _matmul(f, fw["w_ffn_out"], mode="nn", name="mm_ffn_out_ln", tn=1024, tk=4096, resid=x1, rs=alpha,
                         ln=(ln2_g[l][None], ln2_b[l][None]))
        saved.append(dict(x=xin, proj=proj, h3=h3, ya=ya, yb=yb, lse_b=lse_b, yc=yc, lse_c=lse_c, branch=branch,
                          merged=merged, z1=z1, x1=x1, hh=hh, f=f, z2=z2, skr=skr))
        xin = x2

    dx, loss_rows = _loss_head(xin, loss_target.reshape(T, D))
    loss = lax.psum(jnp.sum(loss_rows), ("x", "y", "c"))

    big = {k: [None] * L for k in ("w_in", "w_branch", "w_out", "w_ffn_in", "w_ffn_out")}
    small = [None] * L

    def store_reduced(l, red):
        big["w_in"][l] = red[0].reshape(D, C // N_CHIPS)
        big["w_branch"][l] = red[1].reshape(3, D // N_CHIPS, D)
        big["w_out"][l] = red[2].reshape(D // N_CHIPS, D)
        big["w_ffn_in"][l] = red[3].reshape(D, 2 * FH // N_CHIPS)
        big["w_ffn_out"][l] = red[4].reshape(FH // N_CHIPS, D)

    above = None
    pending = None
    for l in reversed(range(L)):
        fw, lp, sv = full[l], layer_params(l), saved[l]
        dz2, dg2, db2 = _ln_bwd(dx, sv["z2"], ln2_g[l][None])
        df = _matmul(dz2, fw["w_ffn_out"], mode="nt", name="mm_dffn_out_x", tn=4096, tk=1024)
        g_ffn_out = _matmul(sv["f"], dz2, mode="tn", name="mm_dffn_out_w", tm=1408, tn=1024, tk=1024)
        if above is None:
            dhh, _unused = _swiglu_bwd(sv["hh"], df)
        else:
            dhh, r1 = _swiglu_bwd(sv["hh"], df, rider=_swap_rider(above[1]))
            pending = (above[0],) + _reduce_chip(above[1], r1, core_a, shard_a)
        dx1 = _matmul(dhh, fw["w_ffn_in"], mode="nt", name="mm_dffn_in_x", tm=1024, tn=1024, resid=dz2, rs=alpha)
        g_ffn_in = _matmul(sv["x1"], dhh, mode="tn", name="mm_dffn_in_w", tm=1024, tk=1024, out_shards=N_CHIPS)
        dz1, dg1, db1 = _ln_bwd(dx1, sv["z1"], ln1_g[l][None])
        dmerged = _matmul(dz1, fw["w_out"], mode="nt", name="mm_dout_x", tn=1024, tk=1024)
        g_out = _matmul(sv["merged"], dz1, mode="tn", name="mm_dout_w", tm=1024, tn=1024, tk=1024)
        dbranch, dgates = _branch_bwd(dmerged, sv["branch"], sv["proj"], D=D, g_off=off["gt"])
        ys = [sv["ya"], sv["yb"], sv["yc"]]
        dys, g_branch = [], []
        for n in range(3):
            dys.append(_matmul(dbranch, fw["w_branch"][n], mode="nt", name="mm_dbranch_x", tn=1024, tk=1024, a_pick=n))
            g_branch.append(_matmul(ys[n], dbranch, mode="tn", name="mm_dbranch_w", tm=1024, tn=1024, tk=1024, b_pick=n))
        proj3 = sv["proj"].reshape(B, S, C)
        r3 = lambda t: t.reshape(B, S, t.shape[-1])
        lru = _lru_bwd(proj3, sv["h3"], r3(dys[0]), lp, D=D, x_off=off["lx"], g_off=off["lg"])
        dxr, dgate = lru[0], lru[1]
        dqb, dkb, dvb, dsk = _swa_seq_bwd(proj3, r3(sv["yb"]), sv["lse_b"], r3(dys[1]), sv["skr"], D=D, q_off=off["qb"],
                                      k_off=off["kb"], v_off=off["vb"])
        dil_kw = dict(D=D, q_off=off["qc"], k_off=off["kc"], v_off=off["vc"])
        if pending is None:
            acc = _dil_bwd(proj3, r3(sv["yc"]), sv["lse_c"], r3(dys[2]), **dil_kw)
        else:
            res = _dil_bwd(proj3, r3(sv["yc"]), sv["lse_c"], r3(dys[2]), rider=_exchange_rider(pending[1]), **dil_kw)
            acc = res[:3]
            halves = _chip_sums(pending[2], res[3:])
        f2 = lambda t: t.reshape(T, t.shape[-1]).astype(BF16)
        dproj = jnp.concatenate([f2(dxr), f2(dgate), f2(dqb), f2(dkb), f2(dvb), f2(acc[0]), f2(acc[1]), f2(acc[2])] + dgates, axis=1)
        dx_kw = dict(mode="nt", name="mm_dproj_x", tm=1024, tn=1024, resid=dz1, rs=alpha)
        if pending is None:
            dx = _matmul(dproj, fw["w_in"], **dx_kw)
        else:
            dx, got = _matmul(dproj, fw["w_in"], rider=_join_rider(halves), **dx_kw)
            store_reduced(pending[0], _as_shards(_place_own_half(got, halves)))
        g_in = _matmul(sv["x"], dproj, mode="tn", name="mm_dproj_w", tm=512, tk=1024, out_shards=N_CHIPS)

        g5 = [g_in.reshape(1, N_CHIPS, 2, D // 2, C // N_CHIPS),
              jnp.stack(g_branch).reshape(3, N_CHIPS, 2, D // N_CHIPS // 2, D),
              g_out.reshape(1, N_CHIPS, 2, D // N_CHIPS // 2, D),
              g_ffn_in.reshape(1, N_CHIPS, 2, D // 2, 2 * FH // N_CHIPS),
              g_ffn_out.reshape(1, N_CHIPS, 2, FH // N_CHIPS // 2, D)]
        above = (l, g5)

        dsinks = jnp.sum(dsk, axis=0)[:, 0, :hb_b].reshape(H)
        bw = w_rg.shape[-1]
        small[l] = [lru[2].reshape(-1), lru[3].reshape(-1), _block_diag_grad(lru[4], bw).reshape(-1), lru[5].reshape(-1),
                    _block_diag_grad(lru[6], bw).reshape(-1), lru[7].reshape(-1), lru[8].reshape(-1),
                    jnp.pad(dsinks, (0, LANES - H)), dg1.reshape(-1), db1.reshape(-1), dg2.reshape(-1), db2.reshape(-1)]

    qb, qf = _reduce_chip(above[1], _swap_sibling_halves(above[1]), core_a, shard_a)
    store_reduced(above[0], _reduce_finish(qf, _exchange_chips(qb)))

    sizes = [t.size for t in small[0]]
    flat = jnp.concatenate([t for l in range(L) for t in small[l]])
    n_flat = flat.size
    rows = -(-n_flat // (LANES * 256)) * 256
    flat = jnp.pad(flat, (0, rows * LANES - n_flat)).reshape(rows, LANES)
    pair = _small_pair(flat)

    order = ["w_in", "conv_w", "conv_b", "w_rg", "b_rg", "w_ig", "b_ig", "lru_lambda", "sinks", "w_branch", "w_out",
             "ln1_g", "ln1_b", "w_ffn_in", "w_ffn_out", "ln2_g", "ln2_b"]
    weights = dict(w_in=w_in, conv_w=conv_w, conv_b=conv_b, w_rg=w_rg, b_rg=b_rg, w_ig=w_ig, b_ig=b_ig, lru_lambda=lru_lambda,
                   sinks=sinks, w_branch=w_branch, w_out=w_out, ln1_g=ln1_g, ln1_b=ln1_b, w_ffn_in=w_ffn_in,
                   w_ffn_out=w_ffn_out, ln2_g=ln2_g, ln2_b=ln2_b)
    ms = dict(w_in=m_w_in, conv_w=m_conv_w, conv_b=m_conv_b, w_rg=m_w_rg, b_rg=m_b_rg, w_ig=m_w_ig, b_ig=m_b_ig,
              lru_lambda=m_lru_lambda, sinks=m_sinks, w_branch=m_w_branch, w_out=m_w_out, ln1_g=m_ln1_g, ln1_b=m_ln1_b,
              w_ffn_in=m_w_ffn_in, w_ffn_out=m_w_ffn_out, ln2_g=m_ln2_g, ln2_b=m_ln2_b)
    vs = dict(w_in=v_w_in, conv_w=v_conv_w, conv_b=v_conv_b, w_rg=v_w_rg, b_rg=v_b_rg, w_ig=v_w_ig, b_ig=v_b_ig,
              lru_lambda=v_lru_lambda, sinks=v_sinks, w_branch=v_w_branch, w_out=v_w_out, ln1_g=v_ln1_g, ln1_b=v_ln1_b,
              w_ffn_in=v_w_ffn_in, w_ffn_out=v_w_ffn_out, ln2_g=v_ln2_g, ln2_b=v_ln2_b)
    grads = dict(w_in=jnp.stack(big["w_in"]), w_branch=jnp.stack(big["w_branch"]), w_out=jnp.stack(big["w_out"]),
                 w_ffn_in=jnp.stack(big["w_ffn_in"]), w_ffn_out=jnp.stack(big["w_ffn_out"]))
    deltas, new_m, new_v = {}, {}, {}
    deltas["w_in"], new_m["w_in"], new_v["w_in"], others = _adamw(w_in, grads["w_in"], m_w_in, v_w_in,
                                                                  rider=_small_exchange_rider(pair))
    red_small = _small_chip_sum(pair, others).reshape(-1)
    per_layer = sum(sizes)
    names = ["conv_w", "conv_b", "w_rg", "b_rg", "w_ig", "b_ig", "lru_lambda", "sinks", "ln1_g", "ln1_b", "ln2_g", "ln2_b"]
    sg = {nm: [] for nm in names}
    for l in range(L):
        p = l * per_layer
        for nm, sz in zip(names, sizes):
            sg[nm].append(red_small[p:p + sz])
            p += sz
    grads.update(
        conv_w=lax.dynamic_slice_in_dim(jnp.stack(sg["conv_w"]).reshape(L, CONV_WIDTH, D), shard * (D // N_CHIPS), D // N_CHIPS, axis=2),
        conv_b=jnp.stack(sg["conv_b"]), w_rg=jnp.stack(sg["w_rg"]).reshape(w_rg.shape), b_rg=jnp.stack(sg["b_rg"]),
        w_ig=jnp.stack(sg["w_ig"]).reshape(w_ig.shape), b_ig=jnp.stack(sg["b_ig"]), lru_lambda=jnp.stack(sg["lru_lambda"]),
        sinks=jnp.stack(sg["sinks"])[:, :H], ln1_g=jnp.stack(sg["ln1_g"]), ln1_b=jnp.stack(sg["ln1_b"]),
        ln2_g=jnp.stack(sg["ln2_g"]), ln2_b=jnp.stack(sg["ln2_b"]),
    )

    for nm in order[1:]:
        deltas[nm], new_m[nm], new_v[nm] = _adamw(weights[nm], grads[nm], ms[nm], vs[nm])
    return (loss, dx.reshape(B, S, D), *[grads[nm] for nm in order], *[deltas[nm] for nm in order],
            *[new_m[nm] for nm in order], *[new_v[nm] for nm in order])
```

```python
import math

import jax
import jax.numpy as jnp
from jax import lax
from jax.experimental import pallas as pl
from jax.experimental.pallas import tpu as pltpu

HEAD_DIM = 64
WIN = 128
DILS = (1, 4, 16)
SWA_GROUP = 4
CONV_WIDTH = 4
LRU_C = 8.0
LN_EPS = 1e-5
NEG_INF = -1e30
N_CHIPS = 4
ADAM_LR, ADAM_B1, ADAM_B2, ADAM_EPS, ADAM_WD, ADAM_STEP = 0.001, 0.9, 0.999, 1e-08, 0.01, 10

LANES = 128
SUBLANES = 8
VMEM_LIMIT = 48 * 1024 * 1024

assert math.log2(HEAD_DIM) % 2 == 0

F32 = jnp.float32
BF16 = jnp.bfloat16
MESH = pl.DeviceIdType.MESH
ANY = pl.BlockSpec(memory_space=pl.ANY)


def _pcall(body, **kw):
    return pl.pallas_call(body, **kw)


def _pcall_comm(body, **kw):
    return pl.pallas_call(body, **kw)


def _params(*sem):
    return pltpu.CompilerParams(dimension_semantics=tuple(sem), vmem_limit_bytes=VMEM_LIMIT)


def _tile(dim, target):
    if dim <= target:
        return dim
    best = None
    for t in range(LANES, target + 1, LANES):
        if dim % t == 0:
            best = t
    assert best is not None, (dim, target)
    return best


def _sigmoid(x):
    return 1.0 / (1.0 + jnp.exp(-x))


def _dot(a, b, dims):
    return lax.dot_general(a, b, (dims, ((), ())), preferred_element_type=F32)


def _dot_nn(a, b):
    return _dot(a, b, ((1,), (0,)))


def _dot_nt(a, b):
    return _dot(a, b, ((1,), (1,)))


def _dot_tn(a, b):
    return _dot(a, b, ((0,), (0,)))


def _matmul(a, b, *, mode, name, out_dtype=F32, tm=512, tn=512, tk=2048, resid=None, rs=1.0, out_shards=0, n_outer=False,
            ln=None, a_pick=0, b_pick=0, rider=None):
    b_sh = b.ndim == 3
    a_st = a.ndim == 3
    if mode == "nn":
        M, K = a.shape[-2:]
        N = b.shape[-1] * (b.shape[0] if b_sh else 1)
    elif mode == "nt":
        M, K = a.shape[-2:]
        N = b.shape[-2]
    else:
        K, M = a.shape
        N = b.shape[-1]
    tm = _tile(M, tm)
    if mode == "nn" and b_sh:
        tn = b.shape[-1]
    elif out_shards:
        tn = N // out_shards
    else:
        tn = _tile(N, tn)
    if mode == "nt" and b_sh:
        tk = b.shape[-1]
    else:
        tk = _tile(K, tk)
    nk = K // tk
    grid = (N // tn, M // tm, nk) if n_outer else (M // tm, N // tn, nk)

    def spec(shape, f):
        return pl.BlockSpec(shape, (lambda g0, g1, k: f(g1, g0, k)) if n_outer else f)

    a_rows = spec((None, tm, tk), lambda i, j, k: (a_pick, i, k)) if a_st else spec((tm, tk), lambda i, j, k: (i, k))
    if mode == "nn":
        a_spec = a_rows
        b_spec = spec((None, tk, tn), lambda i, j, k: (j, k, 0)) if b_sh else spec((tk, tn), lambda i, j, k: (k, j))
        contract = _dot_nn
    elif mode == "nt":
        a_spec = a_rows
        b_spec = spec((None, tn, tk), lambda i, j, k: (k, j, 0)) if b_sh else spec((tn, tk), lambda i, j, k: (j, k))
        contract = _dot_nt
    else:
        a_spec = spec((tk, tm), lambda i, j, k: (k, i))
        b_spec = spec((None, tk, tn), lambda i, j, k: (b_pick, k, j)) if b_sh else spec((tk, tn), lambda i, j, k: (k, j))
        contract = _dot_tn
    if out_shards:
        out_shape = jax.ShapeDtypeStruct((out_shards, M, tn), out_dtype)
        o_spec = spec((None, tm, tn), lambda i, j, k: (j, i, 0))
    else:
        out_shape = jax.ShapeDtypeStruct((M, N), out_dtype)
        o_spec = spec((tm, tn), lambda i, j, k: (i, j))
    in_specs = [a_spec, b_spec]
    args = [a, b]
    if resid is not None:
        in_specs.append(spec((tm, tn), lambda i, j, k: (i, j)))
        args.append(resid)
    if ln is not None:
        assert tn == N and resid is not None and not out_shards
        in_specs += [spec((1, N), lambda i, j, k: (0, 0))] * 2
        args += list(ln)
        out_shape = [out_shape, out_shape]
        o_spec = [o_spec, o_spec]
    n_in = len(args)

    def body(*refs):
        a_ref, b_ref = refs[:2]
        r_ref = refs[2] if resid is not None else None
        o_ref = refs[n_in]
        part = contract(a_ref[...].astype(BF16), b_ref[...].astype(BF16))

        def finish(res):
            if resid is not None:
                res = res + rs * r_ref[...]
            o_ref[...] = res.astype(out_dtype)
            if ln is not None:
                g_ref, bb_ref, y_ref = refs[n_in - 2], refs[n_in - 1], refs[n_in + 1]
                zc = res - jnp.mean(res, axis=1, keepdims=True)
                var = jnp.mean(zc * zc, axis=1, keepdims=True)
                y_ref[...] = zc * lax.rsqrt(var + LN_EPS) * g_ref[...] + bb_ref[...]

        if nk == 1:
            finish(part)
            return
        acc_ref = refs[-1]
        k = pl.program_id(2)

        @pl.when(k == 0)
        def _():
            acc_ref[...] = part

        @pl.when(jnp.logical_and(k > 0, k < nk - 1))
        def _():
            acc_ref[...] += part

        @pl.when(k == nk - 1)
        def _():
            finish(acc_ref[...] + part)

    if rider is None:
        return _pcall(
            body, name=name, grid=grid, in_specs=in_specs, out_specs=o_spec, out_shape=out_shape,
            scratch_shapes=[pltpu.VMEM((tm, tn), F32)] if nk > 1 else [],
            compiler_params=_params("parallel", "parallel", "arbitrary"),
        )(*args)
    assert ln is None
    res = _call_with_rider(
        rider, body, name=name, grid=grid, in_specs=in_specs, out_specs=[o_spec], out_shape=[out_shape],
        scratch_shapes=[pltpu.VMEM((tm, tn), F32)] if nk > 1 else [], args=args)
    return res[0], list(res[1:])


def _shift_down(x, d, row):
    return jnp.where(row >= d, pltpu.roll(x, d, 0), 0.0)


def _shift_up(x, d, row, n):
    return jnp.where(row < n - d, pltpu.roll(x, n - d, 0), 0.0)


def _log1p(u):
    w = 1.0 + u
    return jnp.where(w == 1.0, u, jnp.log(w) * u / (w - 1.0))


def _gelu_parts(g):
    k = math.sqrt(2.0 / math.pi)
    c = 0.044715
    t = jnp.tanh(k * (g + c * g * g * g))
    val = 0.5 * g * (1.0 + t)
    der = 0.5 * (1.0 + t) + 0.5 * g * (1.0 - t * t) * k * (1.0 + 3.0 * c * g * g)
    return val, der


def _lru_gates(xr, cw_ref, cb_ref, wrg_ref, brg_ref, wig_ref, big_ref, lam_ref, row):
    xc = cw_ref[3:4, :] * xr + cb_ref[...]
    for d in range(1, CONV_WIDTH):
        xc = xc + cw_ref[3 - d:4 - d, :] * _shift_down(xr, d, row)
    xcb = xc.astype(BF16)
    r = _sigmoid(_dot_nn(xcb, wrg_ref[...]) + brg_ref[...])
    ig = _sigmoid(_dot_nn(xcb, wig_ref[...]) + big_ref[...])
    lam = lam_ref[...]
    sp = jnp.maximum(-lam, 0.0) + _log1p(jnp.exp(-jnp.abs(lam)))
    log_a = (-LRU_C) * r * sp
    a = jnp.exp(log_a)
    y2 = 2.0 * log_a
    one_m_a2 = jnp.where(y2 > -0.01, -(y2 + 0.5 * y2 * y2 + (1.0 / 6.0) * y2 * y2 * y2), 1.0 - jnp.exp(y2))
    mult = jnp.sqrt(one_m_a2)
    return xc, r, ig, sp, a, mult


def _scan_local(a, b, row, n, reverse):
    sub = row % SUBLANES
    d = 1
    while d < SUBLANES:
        if reverse:
            keep = sub < SUBLANES - d
            a_s = jnp.where(keep, pltpu.roll(a, n - d, 0), 1.0)
            b_s = jnp.where(keep, pltpu.roll(b, n - d, 0), 0.0)
        else:
            keep = sub >= d
            a_s = jnp.where(keep, pltpu.roll(a, d, 0), 1.0)
            b_s = jnp.where(keep, pltpu.roll(b, d, 0), 0.0)
        b = a * b_s + b
        a = a * a_s
        d *= 2
    return a, b


def _scan_carry(a_ref, b_ref, out_ref, n, reverse):
    ng = n // SUBLANES

    def step(gidx, carry):
        g = (ng - 1 - gidx) if reverse else gidx
        rows = pl.ds(pl.multiple_of(g * SUBLANES, SUBLANES), SUBLANES)
        h = a_ref[rows, :] * carry + b_ref[rows, :]
        out_ref[rows, :] = h
        return h[0:1, :] if reverse else h[SUBLANES - 1:SUBLANES, :]

    lax.fori_loop(0, ng, step, jnp.zeros((1, LANES), F32), unroll=8)


def _lru_specs(B, S, D, C, x_off, g_off):
    nct = D // LANES
    seq = lambda off: pl.BlockSpec((None, S, LANES), lambda ct, b: (b, 0, off // LANES + ct))
    row = lambda r: pl.BlockSpec((r, LANES), lambda ct, b: (0, ct))
    wbd = pl.BlockSpec((None, LANES, LANES), lambda ct, b: (ct, 0, 0))
    return nct, seq, row, wbd


def _lru_fwd(proj3, lp, *, D, x_off, g_off, rider=None):
    B, S, C = proj3.shape
    nct, seq, row, wbd = _lru_specs(B, S, D, C, x_off, g_off)

    def body(xr_ref, g_ref, cw_ref, cb_ref, wrg_ref, brg_ref, wig_ref, big_ref, lam_ref, h_ref, ya_ref, a_s, b_s):
        rowi = lax.broadcasted_iota(jnp.int32, (S, LANES), 0)
        xr = xr_ref[...]
        xc, r, ig, sp, a, mult = _lru_gates(xr, cw_ref, cb_ref, wrg_ref, brg_ref, wig_ref, big_ref, lam_ref, rowi)
        al, bl = _scan_local(a, mult * (ig * xc), rowi, S, False)
        a_s[...] = al
        b_s[...] = bl
        _scan_carry(a_s, b_s, h_ref, S, False)
        gel, _ = _gelu_parts(g_ref[...])
        ya_ref[...] = (h_ref[...] * gel).astype(BF16)

    out_seq = pl.BlockSpec((None, S, LANES), lambda ct, b: (b, 0, ct))
    return _call_with_rider(
        rider, body, name="lru_fwd", grid=(nct, B),
        in_specs=[seq(x_off), seq(g_off), row(CONV_WIDTH), row(1), wbd, row(1), wbd, row(1), row(1)],
        out_specs=[out_seq, out_seq],
        out_shape=[jax.ShapeDtypeStruct((B, S, D), F32), jax.ShapeDtypeStruct((B, S, D), BF16)],
        scratch_shapes=[pltpu.VMEM((S, LANES), F32), pltpu.VMEM((S, LANES), F32)],
        args=[proj3, proj3, lp["conv_w"], lp["conv_b"], lp["w_rg_bd"], lp["b_rg"], lp["w_ig_bd"], lp["b_ig"], lp["lam"]])


def _lru_bwd(proj3, h3, dya3, lp, *, D, x_off, g_off, rider=None):
    B, S, C = proj3.shape
    nct, seq, row, wbd = _lru_specs(B, S, D, C, x_off, g_off)

    def body(xr_ref, g_ref, h_ref, dy_ref, cw_ref, cb_ref, wrg_ref, brg_ref, wig_ref, big_ref, lam_ref,
             dxr_ref, dg_ref, dcw_ref, dcb_ref, dwrg_ref, dbrg_ref, dwig_ref, dbig_ref, dlam_ref, a_s, b_s, l_s):
        first = pl.program_id(1) == 0
        rowi = lax.broadcasted_iota(jnp.int32, (S, LANES), 0)
        xr = xr_ref[...]
        xc, r, ig, sp, a, mult = _lru_gates(xr, cw_ref, cb_ref, wrg_ref, brg_ref, wig_ref, big_ref, lam_ref, rowi)
        h = h_ref[...]
        dy = dy_ref[...]
        gel, dgel = _gelu_parts(g_ref[...])
        dg_ref[...] = (dy * h * dgel).astype(BF16)
        al, bl = _scan_local(_shift_up(a, 1, rowi, S), dy * gel, rowi, S, True)
        a_s[...] = al
        b_s[...] = bl
        _scan_carry(a_s, b_s, l_s, S, True)
        lamb = l_s[...]
        u = ig * xc
        da = lamb * _shift_down(h, 1, rowi)
        dlog_a = da * a - (lamb * u) * (a * a) / mult
        du = lamb * mult
        dpre_r = (dlog_a * ((-LRU_C) * sp)) * r * (1.0 - r)
        dpre_i = (du * xc) * ig * (1.0 - ig)
        dsp = jnp.sum(dlog_a * ((-LRU_C) * r), axis=0, keepdims=True)
        dlam = dsp * (-1.0 / (1.0 + jnp.exp(lam_ref[...])))
        dpr = dpre_r.astype(BF16)
        dpi = dpre_i.astype(BF16)
        dxc = du * ig + _dot_nt(dpr, wrg_ref[...]) + _dot_nt(dpi, wig_ref[...])
        xcb = xc.astype(BF16)
        dwrg = _dot_tn(xcb, dpr)
        dwig = _dot_tn(xcb, dpi)
        dxr = cw_ref[3:4, :] * dxc
        dcw = [jnp.sum(xr * dxc, axis=0, keepdims=True)]
        for d in range(1, CONV_WIDTH):
            dxr = dxr + cw_ref[3 - d:4 - d, :] * _shift_up(dxc, d, rowi, S)
            dcw.append(jnp.sum(_shift_down(xr, d, rowi) * dxc, axis=0, keepdims=True))
        dxr_ref[...] = dxr.astype(BF16)
        dcw_rows = jnp.concatenate(dcw[::-1], axis=0)
        sums = ((dcw_ref, dcw_rows), (dcb_ref, jnp.sum(dxc, axis=0, keepdims=True)), (dwrg_ref, dwrg),
                (dbrg_ref, jnp.sum(dpre_r, axis=0, keepdims=True)), (dwig_ref, dwig),
                (dbig_ref, jnp.sum(dpre_i, axis=0, keepdims=True)), (dlam_ref, dlam))

        @pl.when(first)
        def _():
            for ref, val in sums:
                ref[...] = val

        @pl.when(jnp.logical_not(first))
        def _():
            for ref, val in sums:
                ref[...] += val

    out_seq = pl.BlockSpec((None, S, LANES), lambda ct, b: (b, 0, ct))
    f = lambda shape: jax.ShapeDtypeStruct(shape, F32)
    nb = D // LANES
    return _call_with_rider(
        rider, body, name="lru_bwd", grid=(nct, B),
        in_specs=[seq(x_off), seq(g_off), out_seq, out_seq, row(CONV_WIDTH), row(1), wbd, row(1), wbd, row(1), row(1)],
        out_specs=[out_seq, out_seq, row(CONV_WIDTH), row(1), wbd, row(1), wbd, row(1), row(1)],
        out_shape=[jax.ShapeDtypeStruct((B, S, D), BF16), jax.ShapeDtypeStruct((B, S, D), BF16),
                   f((CONV_WIDTH, D)), f((1, D)), f((nb, LANES, LANES)), f((1, D)), f((nb, LANES, LANES)), f((1, D)), f((1, D))],
        scratch_shapes=[pltpu.VMEM((S, LANES), F32)] * 3, semantics=("parallel", "arbitrary"),
        args=[proj3, proj3, h3, dya3, lp["conv_w"], lp["conv_b"], lp["w_rg_bd"], lp["b_rg"], lp["w_ig_bd"], lp["b_ig"], lp["lam"]])


def _pair_stack(x, lo):
    z = jnp.zeros_like(x)
    return jnp.concatenate([jnp.where(lo, x, z), jnp.where(lo, z, x)], axis=0).astype(BF16)


def _pair_join(y2, lo):
    return jnp.where(lo, y2[:WIN], y2[WIN:])


def _pair_col(xb):
    return jnp.concatenate([xb[:, 0:1], xb[:, HEAD_DIM:HEAD_DIM + 1]], axis=0)


def _pair_bcast(col, lo):
    return jnp.where(lo, jnp.broadcast_to(col[:WIN], (WIN, LANES)), jnp.broadcast_to(col[WIN:], (WIN, LANES)))


def _dil_rows(it, d, S):
    if d == 1:
        cur = pl.multiple_of(it * WIN, WIN)
        prev = pl.multiple_of(jnp.maximum(it - 1, 0) * WIN, WIN)
        return pl.ds(cur, WIN), pl.ds(prev, WIN), it > 0
    r, i = it % d, it // d
    cur = i * (WIN * d) + r
    prev = jnp.maximum(i - 1, 0) * (WIN * d) + r
    return pl.ds(cur, WIN, stride=d), pl.ds(prev, WIN, stride=d), i > 0


def _dil_bias(two_blocks, stack=2):
    nk = 2 * WIN if two_blocks else WIN
    qi = lax.broadcasted_iota(jnp.int32, (stack * WIN, nk), 0) & (WIN - 1)
    kj = lax.broadcasted_iota(jnp.int32, (stack * WIN, nk), 1)
    if not two_blocks:
        return jnp.where(kj <= qi, 0.0, NEG_INF), None
    cur = jnp.logical_and(kj >= WIN, kj - WIN <= qi)
    prev = jnp.logical_and(kj < WIN, kj >= qi)
    return jnp.where(jnp.logical_or(cur, prev), 0.0, NEG_INF), jnp.where(cur, 0.0, NEG_INF)


def _dil_specs(B, S, D, C, offs):
    grid = (B, D // LANES)
    seq = lambda off: pl.BlockSpec((None, S, LANES), lambda b, p: (b, 0, off // LANES + p))
    return grid, [seq(o) for o in offs], seq(0)


def _call_with_rider(rider, body, *, name, grid, in_specs, out_specs, out_shape, scratch_shapes, args, semantics=None):
    if rider is None:
        return _pcall(body, name=name, grid=grid, in_specs=in_specs, out_specs=out_specs, out_shape=out_shape,
                      scratch_shapes=scratch_shapes, compiler_params=_params(*(semantics or ("parallel",) * len(grid))))(*args)
    n_in, n_out, n_sc = len(in_specs), len(out_specs), len(scratch_shapes)
    r_in, r_out = len(rider["ins"]), len(rider["out_shapes"])

    def wrapped(*refs):
        p = 0
        own_in = refs[p:p + n_in]; p += n_in
        rid_in = refs[p:p + r_in]; p += r_in
        own_out = refs[p:p + n_out]; p += n_out
        rid_out = refs[p:p + r_out]; p += r_out
        own_sc = refs[p:p + n_sc]; p += n_sc
        send_sems, recv_sems = refs[p:p + 2]
        ids = [pl.program_id(a) for a in range(len(grid))]
        first = ids[0] == 0
        last = ids[0] == grid[0] - 1
        for a in range(1, len(grid)):
            first = jnp.logical_and(first, ids[a] == 0)
            last = jnp.logical_and(last, ids[a] == grid[a] - 1)

        @pl.when(first)
        def _():
            rider["start"](rid_in, rid_out, send_sems, recv_sems)

        body(*own_in, *own_out, *own_sc)

        @pl.when(last)
        def _():
            rider["finish"](rid_in, rid_out, send_sems, recv_sems)

    aliases = {n_in + t: n_out + t for t in range(r_in)} if rider.get("in_place") else {}
    res = _pcall_comm(
        wrapped, name=name + "_" + rider["name"], grid=grid, in_specs=list(in_specs) + [ANY] * r_in,
        out_specs=list(out_specs) + [ANY] * r_out, out_shape=list(out_shape) + list(rider["out_shapes"]),
        scratch_shapes=list(scratch_shapes) + [pltpu.SemaphoreType.DMA((rider["n"],)), pltpu.SemaphoreType.DMA((rider["n"],))],
        input_output_aliases=aliases, compiler_params=_params(*(("arbitrary",) * len(grid))),
    )(*args, *rider["ins"])
    return res


def _dil_fwd(proj3, *, D, q_off, k_off, v_off, rider=None):
    B, S, C = proj3.shape
    n_it = S // WIN
    scale = HEAD_DIM ** -0.5
    grid, in_specs, out_spec = _dil_specs(B, S, D, C, (q_off, k_off, v_off))

    def body(q_ref, k_ref, v_ref, o_ref, l_ref):
        lo = lax.broadcasted_iota(jnp.int32, (WIN, LANES), 1) < HEAD_DIM
        for c, d in enumerate(DILS):
            two = S // d > WIN
            bias_all, bias_first = _dil_bias(two)

            def step(it, _, c=c, d=d, two=two, bias_all=bias_all, bias_first=bias_first):
                cur, prev, later = _dil_rows(it, d, S)
                q2 = _pair_stack(q_ref[cur, :] * scale, lo)
                if two:
                    k2 = jnp.concatenate([k_ref[prev, :], k_ref[cur, :]], axis=0).astype(BF16)
                    v2 = jnp.concatenate([v_ref[prev, :], v_ref[cur, :]], axis=0).astype(BF16)
                    bias = jnp.where(later, bias_all, bias_first)
                else:
                    k2, v2, bias = k_ref[cur, :].astype(BF16), v_ref[cur, :].astype(BF16), bias_all
                s2 = _dot_nt(q2, k2) + bias
                m2 = jnp.max(s2, axis=1, keepdims=True)
                p2 = jnp.exp(s2 - m2)
                den = jnp.sum(p2, axis=1, keepdims=True)
                oc = _pair_join(_dot_nn(p2.astype(BF16), v2) / den, lo)
                lc = _pair_bcast(m2 + jnp.log(den), lo)
                if c == 0:
                    o_ref[cur, :] = oc
                    l_ref[cur, :] = lc
                else:
                    l_old = l_ref[cur, :]
                    mx = jnp.maximum(l_old, lc)
                    e_old, e_new = jnp.exp(l_old - mx), jnp.exp(lc - mx)
                    tot = e_old + e_new
                    o_ref[cur, :] = (e_old * o_ref[cur, :] + e_new * oc) / tot
                    l_ref[cur, :] = mx + jnp.log(tot)
                return 0

            lax.fori_loop(0, n_it, step, 0, unroll=16)

    return _call_with_rider(
        rider, body, name="dil_fwd", grid=grid, in_specs=in_specs, out_specs=[out_spec, out_spec],
        out_shape=[jax.ShapeDtypeStruct((B, S, D), F32)] * 2, scratch_shapes=[], args=[proj3, proj3, proj3])


def _dil_bwd(proj3, o3, l3, do3, *, D, q_off, k_off, v_off, rider=None):
    B, S, C = proj3.shape
    n_it = S // WIN
    scale = HEAD_DIM ** -0.5
    grid, in_specs, out_spec = _dil_specs(B, S, D, C, (q_off, k_off, v_off))

    def body(q_ref, k_ref, v_ref, o_ref, l_ref, do_ref, dq_ref, dk_ref, dv_ref, dd_s, dq_s, dk_s, dv_s):
        lo = lax.broadcasted_iota(jnp.int32, (WIN, LANES), 1) < HEAD_DIM
        lo_s = lax.broadcasted_iota(jnp.int32, (S, LANES), 1) < HEAD_DIM
        prod = do_ref[...] * o_ref[...]
        d_lo = jnp.sum(jnp.where(lo_s, prod, 0.0), axis=1, keepdims=True)
        d_hi = jnp.sum(jnp.where(lo_s, 0.0, prod), axis=1, keepdims=True)
        dd_s[...] = jnp.where(lo_s, jnp.broadcast_to(d_lo, (S, LANES)), jnp.broadcast_to(d_hi, (S, LANES)))
        dq_s[...] = jnp.zeros_like(dq_s)
        dk_s[...] = jnp.zeros_like(dk_s)
        dv_s[...] = jnp.zeros_like(dv_s)
        for d in DILS:
            two = S // d > WIN
            bias_all, bias_first = _dil_bias(two)

            def step(it, _, d=d, two=two, bias_all=bias_all, bias_first=bias_first):
                cur, prev, later = _dil_rows(it, d, S)
                q2 = _pair_stack(q_ref[cur, :] * scale, lo)
                do2 = _pair_stack(do_ref[cur, :], lo)
                l2 = _pair_col(l_ref[cur, :])
                dd2 = _pair_col(dd_s[cur, :])
                if two:
                    k2 = jnp.concatenate([k_ref[prev, :], k_ref[cur, :]], axis=0).astype(BF16)
                    v2 = jnp.concatenate([v_ref[prev, :], v_ref[cur, :]], axis=0).astype(BF16)
                    bias = jnp.where(later, bias_all, bias_first)
                else:
                    k2, v2, bias = k_ref[cur, :].astype(BF16), v_ref[cur, :].astype(BF16), bias_all
                p2 = jnp.exp(_dot_nt(q2, k2) + bias - l2)
                ds2 = (p2 * (_dot_nt(do2, v2) - dd2)).astype(BF16)
                dq_s[cur, :] += _pair_join(_dot_nn(ds2, k2), lo) * scale
                dk2 = _dot_tn(ds2, q2)
                dv2 = _dot_tn(p2.astype(BF16), do2)
                if two:
                    dk_s[prev, :] += dk2[:WIN]
                    dv_s[prev, :] += dv2[:WIN]
                    dk_s[cur, :] += dk2[WIN:]
                    dv_s[cur, :] += dv2[WIN:]
                else:
                    dk_s[cur, :] += dk2
                    dv_s[cur, :] += dv2
                return 0

            lax.fori_loop(0, n_it, step, 0, unroll=16)
        dq_ref[...] = dq_s[...].astype(BF16)
        dk_ref[...] = dk_s[...].astype(BF16)
        dv_ref[...] = dv_s[...].astype(BF16)

    return _call_with_rider(
        rider, body, name="dil_bwd", grid=grid, in_specs=in_specs + [out_spec] * 3, out_specs=[out_spec] * 3,
        out_shape=[jax.ShapeDtypeStruct((B, S, D), BF16)] * 3, scratch_shapes=[pltpu.VMEM((S, LANES), F32)] * 4,
        args=[proj3, proj3, proj3, o3, l3, do3])


SWA_HB = 2 * SWA_GROUP


def _to_half(x, src, dst, lo):
    if src != dst:
        x = pltpu.roll(x, HEAD_DIM, 1)
    return jnp.where(lo if dst == 0 else jnp.logical_not(lo), x, 0.0)


def _swa_kv(g):
    return 2 * g // SWA_GROUP


SWA_STACKS = ((0, 1), (2, 3))


def _swa_stack(ref, gs, lo, dtype, rows=slice(None)):
    parts = []
    for g in gs:
        x = ref[rows, g * LANES:(g + 1) * LANES]
        parts += [_to_half(x, 0, _swa_kv(g), lo), _to_half(x, 1, _swa_kv(g), lo)]
    return jnp.concatenate(parts, axis=0).astype(dtype)


def _swa_unstack(y, gs, lo):
    out = []
    for t, g in enumerate(gs):
        even, odd = y[2 * t * WIN:(2 * t + 1) * WIN], y[(2 * t + 1) * WIN:(2 * t + 2) * WIN]
        out.append(_to_half(even, _swa_kv(g), 0, lo) + _to_half(odd, _swa_kv(g), 1, lo))
    return out


def _swa_cols(x, gs):
    cols = []
    for g in gs:
        cols += [jnp.broadcast_to(x[:, 2 * g:2 * g + 1], (WIN, 1)), jnp.broadcast_to(x[:, 2 * g + 1:2 * g + 2], (WIN, 1))]
    return jnp.concatenate(cols, axis=0)


SWA_UNROLL = 8


def _swa_seq_specs(B, S, D, q_off, k_off, v_off):
    qw = SWA_HB * HEAD_DIM
    assert q_off % qw == 0 and k_off % LANES == 0 and v_off % LANES == 0 and D % qw == 0
    seq = lambda width, off: pl.BlockSpec((None, S, width), lambda b, hh: (b, 0, off // width + hh))
    sink = pl.BlockSpec((None, 1, LANES), lambda b, hh: (hh, 0, 0))
    return (B, D // qw), seq, sink, qw


def _swa_rows(it):
    cur = pl.ds(pl.multiple_of(it * WIN, WIN), WIN)
    prev = pl.ds(pl.multiple_of(jnp.maximum(it - 1, 0) * WIN, WIN), WIN)
    return cur, prev, it > 0


def _swa_seq_fwd(proj3, sinks, *, D, q_off, k_off, v_off, rider=None):
    B, S, C = proj3.shape
    scale = HEAD_DIM ** -0.5
    grid, seq, sink, qw = _swa_seq_specs(B, S, D, q_off, k_off, v_off)
    nhb = D // qw

    def body(q_ref, k_ref, v_ref, sk_ref, o_ref, lse_ref):
        lo = lax.broadcasted_iota(jnp.int32, (WIN, LANES), 1) < HEAD_DIM
        lane = lax.broadcasted_iota(jnp.int32, (WIN, LANES), 1)
        sk = sk_ref[...]
        biases = [_dil_bias(True, 2 * len(gs)) for gs in SWA_STACKS]

        def step(it, _):
            cur, prev, later = _swa_rows(it)
            k2 = jnp.concatenate([k_ref[prev, :], k_ref[cur, :]], axis=0).astype(BF16)
            v2 = jnp.concatenate([v_ref[prev, :], v_ref[cur, :]], axis=0).astype(BF16)
            lse_acc = jnp.zeros((WIN, LANES), F32)
            for gs, (bias_all, bias_first) in zip(SWA_STACKS, biases):
                bias = jnp.where(later, bias_all, bias_first)
                qs = _swa_stack(q_ref, gs, lo, BF16, cur)
                sks = _swa_cols(sk, gs)
                s = _dot_nt(qs, k2) * scale + bias
                m = jnp.maximum(jnp.max(s, axis=1, keepdims=True), sks)
                p = jnp.exp(s - m)
                den = jnp.sum(p, axis=1, keepdims=True) + jnp.exp(sks - m)
                for g, grp in zip(gs, _swa_unstack(_dot_nn(p.astype(BF16), v2) / den, gs, lo)):
                    o_ref[cur, g * LANES:(g + 1) * LANES] = grp
                ls = m + jnp.log(den)
                for t, g in enumerate(gs):
                    lse_acc = jnp.where(lane == 2 * g, ls[2 * t * WIN:(2 * t + 1) * WIN], lse_acc)
                    lse_acc = jnp.where(lane == 2 * g + 1, ls[(2 * t + 1) * WIN:(2 * t + 2) * WIN], lse_acc)
            lse_ref[cur, :] = lse_acc
            return 0

        lax.fori_loop(0, S // WIN, step, 0, unroll=SWA_UNROLL)

    return _call_with_rider(
        rider, body, name="swa_fwd", grid=grid,
        in_specs=[seq(qw, q_off), seq(LANES, k_off), seq(LANES, v_off), sink],
        out_specs=[seq(qw, 0), seq(LANES, 0)],
        out_shape=[jax.ShapeDtypeStruct((B, S, D), F32), jax.ShapeDtypeStruct((B, S, nhb * LANES), F32)],
        scratch_shapes=[], args=[proj3, proj3, proj3, sinks])


def _swa_seq_bwd(proj3, o3, lse3, do3, sinks, *, D, q_off, k_off, v_off):
    B, S, C = proj3.shape
    scale = HEAD_DIM ** -0.5
    grid, seq, sink, qw = _swa_seq_specs(B, S, D, q_off, k_off, v_off)
    nhb = D // qw
    KV = D // SWA_GROUP

    def body(q_ref, k_ref, v_ref, o_ref, l_ref, do_ref, sk_ref, dq_ref, dk_ref, dv_ref, dsk_ref, dk_s, dv_s):
        lo = lax.broadcasted_iota(jnp.int32, (WIN, LANES), 1) < HEAD_DIM
        lane = lax.broadcasted_iota(jnp.int32, (1, LANES), 1)
        sk = sk_ref[...]
        biases = [_dil_bias(True, 2 * len(gs)) for gs in SWA_STACKS]
        dk_s[...] = jnp.zeros_like(dk_s)
        dv_s[...] = jnp.zeros_like(dv_s)

        dsk_ref[...] = jnp.zeros_like(dsk_ref)

        def step(it, _):
            cur, prev, later = _swa_rows(it)
            k2 = jnp.concatenate([k_ref[prev, :], k_ref[cur, :]], axis=0).astype(BF16)
            v2 = jnp.concatenate([v_ref[prev, :], v_ref[cur, :]], axis=0).astype(BF16)
            lse = l_ref[cur, :]
            dk2 = jnp.zeros((2 * WIN, LANES), F32)
            dv2 = jnp.zeros((2 * WIN, LANES), F32)
            dsk_acc = jnp.zeros((1, LANES), F32)
            for gs, (bias_all, bias_first) in zip(SWA_STACKS, biases):
                bias = jnp.where(later, bias_all, bias_first)
                qs = _swa_stack(q_ref, gs, lo, BF16, cur)
                dos = _swa_stack(do_ref, gs, lo, BF16, cur)
                dds = []
                for g in gs:
                    prod = do_ref[cur, g * LANES:(g + 1) * LANES] * o_ref[cur, g * LANES:(g + 1) * LANES]
                    dds += [jnp.sum(jnp.where(lo, prod, 0.0), axis=1, keepdims=True),
                            jnp.sum(jnp.where(lo, 0.0, prod), axis=1, keepdims=True)]
                dds = jnp.concatenate(dds, axis=0)
                ls = _swa_cols(lse, gs)
                ps = jnp.exp(_dot_nt(qs, k2) * scale + bias - ls)
                dss = (ps * (_dot_nt(dos, v2) - dds) * scale).astype(BF16)
                for g, grp in zip(gs, _swa_unstack(_dot_nn(dss, k2), gs, lo)):
                    dq_ref[cur, g * LANES:(g + 1) * LANES] = grp.astype(BF16)
                dk2 = dk2 + _dot_tn(dss, qs)
                dv2 = dv2 + _dot_tn(ps.astype(BF16), dos)
                dsks = jnp.exp(_swa_cols(sk, gs) - ls) * dds
                for t, g in enumerate(gs):
                    for u in range(2):
                        rows = slice((2 * t + u) * WIN, (2 * t + u + 1) * WIN)
                        dsk_acc = dsk_acc + jnp.where(lane == 2 * g + u, -jnp.sum(dsks[rows], axis=0, keepdims=True), 0.0)
            dk_s[prev, :] += dk2[:WIN]
            dv_s[prev, :] += dv2[:WIN]
            dk_s[cur, :] += dk2[WIN:]
            dv_s[cur, :] += dv2[WIN:]
            dsk_ref[...] += dsk_acc
            return 0

        lax.fori_loop(0, S // WIN, step, 0, unroll=SWA_UNROLL)
        dk_ref[...] = dk_s[...].astype(BF16)
        dv_ref[...] = dv_s[...].astype(BF16)

    return _pcall(
        body, name="swa_bwd", grid=grid,
        in_specs=[seq(qw, q_off), seq(LANES, k_off), seq(LANES, v_off), seq(qw, 0), seq(LANES, 0), seq(qw, 0), sink],
        out_specs=[seq(qw, 0), seq(LANES, 0), seq(LANES, 0), pl.BlockSpec((None, None, 1, LANES), lambda b, hh: (b, hh, 0, 0))],
        out_shape=[jax.ShapeDtypeStruct((B, S, D), BF16), jax.ShapeDtypeStruct((B, S, KV), BF16),
                   jax.ShapeDtypeStruct((B, S, KV), BF16), jax.ShapeDtypeStruct((B, nhb, 1, LANES), F32)],
        scratch_shapes=[pltpu.VMEM((S, LANES), F32), pltpu.VMEM((S, LANES), F32)],
        compiler_params=_params("parallel", "parallel"),
    )(proj3, proj3, proj3, o3, lse3, do3, sinks)


def _branch_fwd(ys, wb, proj, *, D, g_off, rider=None):
    T = proj.shape[0]
    tm, tn = _tile(T, 256), _tile(D, 512)
    n = len(ys)

    def body(*refs):
        y_refs, w_ref, g_refs, br_ref, mg_ref = refs[:n], refs[n], refs[n + 1:2 * n + 1], refs[2 * n + 1], refs[2 * n + 2]
        acc = None
        for k in range(n):
            br = _dot_nn(y_refs[k][...].astype(BF16), w_ref[k])
            br_ref[k] = br
            term = _sigmoid(g_refs[k][...]) * br
            acc = term if acc is None else acc + term
        mg_ref[...] = acc.astype(BF16)

    gate = lambda k: pl.BlockSpec((tm, tn), lambda i, j: (i, (g_off + k * D) // tn + j))
    return _call_with_rider(
        rider, body, name="branch_fwd", grid=(T // tm, D // tn),
        in_specs=[pl.BlockSpec((tm, D), lambda i, j: (i, 0))] * n + [pl.BlockSpec((n, D, tn), lambda i, j: (0, 0, j))]
        + [gate(k) for k in range(n)],
        out_specs=[pl.BlockSpec((n, tm, tn), lambda i, j: (0, i, j)), pl.BlockSpec((tm, tn), lambda i, j: (i, j))],
        out_shape=[jax.ShapeDtypeStruct((n, T, D), F32), jax.ShapeDtypeStruct((T, D), BF16)],
        scratch_shapes=[], args=[*ys, wb, *([proj] * n)])


def _branch_bwd(dmerged, branch, proj, *, D, g_off):
    n, T, _ = branch.shape
    tm, tn = _tile(T, 512), _tile(D, 512)

    def body(dm_ref, br_ref, *rest):
        g_refs, db_ref, dg_refs = rest[:n], rest[n], rest[n + 1:]
        dm = dm_ref[...]
        for k in range(n):
            sg = _sigmoid(g_refs[k][...])
            db_ref[k] = (sg * dm).astype(BF16)
            dg_refs[k][...] = (dm * br_ref[k] * sg * (1.0 - sg)).astype(BF16)

    gate = lambda k: pl.BlockSpec((tm, tn), lambda i, j: (i, (g_off + k * D) // tn + j))
    blk = pl.BlockSpec((tm, tn), lambda i, j: (i, j))
    res = _pcall(
        body, name="branch_bwd", grid=(T // tm, D // tn),
        in_specs=[blk, pl.BlockSpec((n, tm, tn), lambda i, j: (0, i, j))] + [gate(k) for k in range(n)],
        out_specs=[pl.BlockSpec((n, tm, tn), lambda i, j: (0, i, j))] + [blk] * n,
        out_shape=[jax.ShapeDtypeStruct((n, T, D), BF16)] + [jax.ShapeDtypeStruct((T, D), BF16)] * n,
        compiler_params=_params("parallel", "parallel"),
    )(dmerged, branch, *([proj] * n))
    return res[0], list(res[1:])


def _ln_bwd(dout, z, g):
    T, D = z.shape
    tm = _tile(T, 512)

    def body(do_ref, z_ref, g_ref, dz_ref, dg_ref, db_ref):
        z = z_ref[...]
        do = do_ref[...]
        mu = jnp.mean(z, axis=1, keepdims=True)
        zc = z - mu
        rstd = lax.rsqrt(jnp.mean(zc * zc, axis=1, keepdims=True) + LN_EPS)
        xhat = zc * rstd
        dxh = do * g_ref[...]
        dz_ref[...] = rstd * (dxh - jnp.mean(dxh, axis=1, keepdims=True) - xhat * jnp.mean(dxh * xhat, axis=1, keepdims=True))
        dg = jnp.sum(do * xhat, axis=0, keepdims=True)
        db = jnp.sum(do, axis=0, keepdims=True)
        first = pl.program_id(0) == 0

        @pl.when(first)
        def _():
            dg_ref[...] = dg
            db_ref[...] = db

        @pl.when(jnp.logical_not(first))
        def _():
            dg_ref[...] += dg
            db_ref[...] += db

    blk = pl.BlockSpec((tm, D), lambda i: (i, 0))
    vec = pl.BlockSpec((1, D), lambda i: (0, 0))
    return _pcall(
        body, name="ln_bwd", grid=(T // tm,), in_specs=[blk, blk, vec], out_specs=[blk, vec, vec],
        out_shape=[jax.ShapeDtypeStruct((T, D), F32), jax.ShapeDtypeStruct((1, D), F32), jax.ShapeDtypeStruct((1, D), F32)],
        compiler_params=_params("arbitrary"),
    )(dout, z, g)


def _swiglu_fwd(hh):
    T, F2 = hh.shape
    Fh = F2 // 2
    tm, tn = _tile(T, 256), _tile(Fh, 1408)
    nj = Fh // tn

    def body(h1_ref, h3_ref, f_ref):
        h1 = h1_ref[...]
        f_ref[...] = (h1 * _sigmoid(h1) * h3_ref[...]).astype(BF16)

    return _pcall(
        body, name="swiglu_fwd", grid=(T // tm, nj),
        in_specs=[pl.BlockSpec((tm, tn), lambda i, j: (i, j)), pl.BlockSpec((tm, tn), lambda i, j: (i, nj + j))],
        out_specs=pl.BlockSpec((tm, tn), lambda i, j: (i, j)),
        out_shape=jax.ShapeDtypeStruct((T, Fh), BF16), compiler_params=_params("parallel", "parallel"),
    )(hh, hh)


def _swiglu_bwd(hh, df, rider=None):
    T, F2 = hh.shape
    Fh = F2 // 2
    tm, tn = _tile(T, 256), _tile(Fh, 1408)
    nj = Fh // tn

    def body(h1_ref, h3_ref, df_ref, d1_ref, d3_ref):
        h1 = h1_ref[...]
        sg = _sigmoid(h1)
        d = df_ref[...]
        d1_ref[...] = (d * h3_ref[...] * sg * (1.0 + h1 * (1.0 - sg))).astype(BF16)
        d3_ref[...] = (d * h1 * sg).astype(BF16)

    lo = pl.BlockSpec((tm, tn), lambda i, j: (i, j))
    hi = pl.BlockSpec((tm, tn), lambda i, j: (i, nj + j))
    res = _call_with_rider(
        rider, body, name="swiglu_bwd", grid=(T // tm, nj), in_specs=[lo, hi, lo], out_specs=[lo, lo],
        out_shape=[jax.ShapeDtypeStruct((T, Fh), BF16)] * 2, scratch_shapes=[], args=[hh, hh, df])
    return jnp.concatenate([res[0], res[1]], axis=1), list(res[2:])


def _loss_head(y, target):
    T, D = y.shape
    tm = _tile(T, 512)

    def body(y_ref, t_ref, dy_ref, l_ref):
        e = y_ref[...] - t_ref[...]
        dy_ref[...] = e * (1.0 / D)
        sq = e * e
        part = sq[:, 0:LANES]
        for c in range(1, D // LANES):
            part = part + sq[:, c * LANES:(c + 1) * LANES]
        part = jnp.sum(part, axis=0, keepdims=True) * (0.5 / D)
        first = pl.program_id(0) == 0

        @pl.when(first)
        def _():
            l_ref[...] = part

        @pl.when(jnp.logical_not(first))
        def _():
            l_ref[...] += part

    blk = pl.BlockSpec((tm, D), lambda i: (i, 0))
    return _pcall(
        body, name="loss_head", grid=(T // tm,), in_specs=[blk, blk],
        out_specs=[blk, pl.BlockSpec((1, LANES), lambda i: (0, 0))],
        out_shape=[jax.ShapeDtypeStruct((T, D), F32), jax.ShapeDtypeStruct((1, LANES), F32)],
        compiler_params=_params("arbitrary"),
    )(y, target)


def _as_rows(a):
    return a.reshape(-1, a.shape[-1])


def _adamw(w, g, m, v, rider=None):
    w2, g2, m2, v2 = (_as_rows(t) for t in (w, g, m, v))
    R, Cc = w2.shape
    cap = max(SUBLANES, min(512, (256 * 1024) // Cc))
    tm = R if (R <= cap or R % SUBLANES) else max(t for t in range(SUBLANES, cap + 1, SUBLANES) if R % t == 0)
    c1 = 1.0 - ADAM_B1 ** ADAM_STEP
    c2 = 1.0 - ADAM_B2 ** ADAM_STEP

    def body(w_ref, g_ref, m_ref, v_ref, d_ref, nm_ref, nv_ref):
        gg = g_ref[...]
        nm = ADAM_B1 * m_ref[...] + (1.0 - ADAM_B1) * gg
        nv = ADAM_B2 * v_ref[...] + (1.0 - ADAM_B2) * (gg * gg)
        d_ref[...] = (-ADAM_LR) * ((nm / c1) / (jnp.sqrt(nv / c2) + ADAM_EPS) + ADAM_WD * w_ref[...])
        nm_ref[...] = nm
        nv_ref[...] = nv

    blk = pl.BlockSpec((tm, Cc), lambda i: (i, 0))
    res = _call_with_rider(
        rider, body, name="adamw", grid=(R // tm,), in_specs=[blk] * 4, out_specs=[blk] * 3,
        out_shape=[jax.ShapeDtypeStruct((R, Cc), F32)] * 3, scratch_shapes=[], args=[w2, g2, m2, v2])
    return tuple(t.reshape(w.shape) for t in res[:3]) + tuple(res[3:])


def _where_am_i():
    x, y, c = lax.axis_index("x"), lax.axis_index("y"), lax.axis_index("c")
    chips = [(1 - x, y), (x, 1 - y), (1 - x, 1 - y)]
    return x, y, c, chips


def _remote(src, dst, send_sems, recv_sems, k, to):
    return pltpu.make_async_remote_copy(src_ref=src, dst_ref=dst, send_sem=send_sems.at[k], recv_sem=recv_sems.at[k],
                                        device_id=to, device_id_type=MESH)


def _comm_call(body, name, ins, out_shapes, n_remote, n_local):
    return _pcall_comm(
        body, name=name, in_specs=[ANY] * len(ins), out_specs=[ANY] * len(out_shapes), out_shape=out_shapes,
        scratch_shapes=[pltpu.SemaphoreType.DMA((n_remote,)), pltpu.SemaphoreType.DMA((n_remote,)),
                        pltpu.SemaphoreType.DMA((max(n_local, 1),))],
    )(*ins)


def _gather_weights(shards):
    n = len(shards)

    def body(*refs):
        ins, outs = refs[:n], refs[n:2 * n]
        send_sems, recv_sems, local_sems = refs[2 * n:]
        x, y, c, chips = _where_am_i()
        s = 2 * x + y
        sib = (x, y, 1 - c)
        first = []
        for t in range(n):
            for j, (cx, cy) in enumerate(chips):
                first.append(_remote(ins[t].at[:, c], outs[t].at[:, s, c], send_sems, recv_sems, 6 * t + j, (cx, cy, c)))
        for cp in first:
            cp.start()
        passed = []
        for j, (cx, cy) in enumerate(chips):
            sj = 2 * cx + cy
            for t in range(n):
                land = outs[t].at[:, sj, c]
                _remote(land, land, send_sems, recv_sems, 6 * t + j, (cx, cy, c)).wait_recv()
                fw = _remote(land, land, send_sems, recv_sems, 6 * t + 3 + j, sib)
                fw.start()
                passed.append(fw)
        for j, (cx, cy) in enumerate(chips):
            sj = 2 * cx + cy
            for t in range(n):
                land = outs[t].at[:, sj, 1 - c]
                _remote(land, land, send_sems, recv_sems, 6 * t + 3 + j, sib).wait_recv()
        for cp in first + passed:
            cp.wait_send()

    out_shapes = [jax.ShapeDtypeStruct((t.shape[0], N_CHIPS) + t.shape[1:], t.dtype) for t in shards]
    got = _comm_call(body, "gather_weights", shards, out_shapes, 6 * n, 0)
    s = 2 * lax.axis_index("x") + lax.axis_index("y")
    return [lax.dynamic_update_slice(g, t[:, None], (0, s, 0, 0, 0)) for g, t in zip(got, shards)]


def _gather_rider(shards):
    n = len(shards)

    def copies(ins, outs, send_sems, recv_sems):
        x, y, c, chips = _where_am_i()
        s = 2 * x + y
        return [_remote(ins[t].at[:, c], outs[t].at[:, s, c], send_sems, recv_sems, 3 * t + j, (cx, cy, c))
                for t in range(n) for j, (cx, cy) in enumerate(chips)]

    def start(ins, outs, send_sems, recv_sems):
        for cp in copies(ins, outs, send_sems, recv_sems):
            cp.start()

    def finish(ins, outs, send_sems, recv_sems):
        x, y, c, chips = _where_am_i()
        for t in range(n):
            for j, (cx, cy) in enumerate(chips):
                land = outs[t].at[:, 2 * cx + cy, c]
                _remote(land, land, send_sems, recv_sems, 3 * t + j, (cx, cy, c)).wait_recv()
        for cp in copies(ins, outs, send_sems, recv_sems):
            cp.wait_send()

    out_shapes = [jax.ShapeDtypeStruct((t.shape[0], N_CHIPS) + t.shape[1:], t.dtype) for t in shards]
    return dict(name="gather", ins=list(shards), out_shapes=out_shapes, n=3 * n, start=start, finish=finish)


def _gather_forward(landed, shards):
    n = len(landed)

    def body(*refs):
        outs = refs[n:2 * n]
        send_sems, recv_sems, _ = refs[2 * n:]
        x, y, c, chips = _where_am_i()
        sib = (x, y, 1 - c)
        cps = []
        for t in range(n):
            for j, (cx, cy) in enumerate(chips):
                land = outs[t].at[:, 2 * cx + cy, c]
                cps.append(_remote(land, land, send_sems, recv_sems, 3 * t + j, sib))
        for cp in cps:
            cp.start()
        for t in range(n):
            for j, (cx, cy) in enumerate(chips):
                land = outs[t].at[:, 2 * cx + cy, 1 - c]
                _remote(land, land, send_sems, recv_sems, 3 * t + j, sib).wait_recv()
        for cp in cps:
            cp.wait_send()

    got = _pcall_comm(
        body, name="gather_forward", in_specs=[ANY] * n, out_specs=[ANY] * n,
        out_shape=[jax.ShapeDtypeStruct(t.shape, t.dtype) for t in landed], input_output_aliases={t: t for t in range(n)},
        scratch_shapes=[pltpu.SemaphoreType.DMA((3 * n,)), pltpu.SemaphoreType.DMA((3 * n,)), pltpu.SemaphoreType.DMA((1,))],
    )(*landed)
    s = 2 * lax.axis_index("x") + lax.axis_index("y")
    return [lax.dynamic_update_slice(g, t[:, None], (0, s, 0, 0, 0)) for g, t in zip(got, shards)]


def _forward_rider(landed):
    n = len(landed)

    def copies(outs, send_sems, recv_sems):
        x, y, c, chips = _where_am_i()
        cps = []
        for t in range(n):
            for j, (cx, cy) in enumerate(chips):
                land = outs[t].at[:, 2 * cx + cy, c]
                cps.append(_remote(land, land, send_sems, recv_sems, 3 * t + j, (x, y, 1 - c)))
        return cps

    def start(ins, outs, send_sems, recv_sems):
        for cp in copies(outs, send_sems, recv_sems):
            cp.start()

    def finish(ins, outs, send_sems, recv_sems):
        x, y, c, chips = _where_am_i()
        for t in range(n):
            for j, (cx, cy) in enumerate(chips):
                land = outs[t].at[:, 2 * cx + cy, 1 - c]
                _remote(land, land, send_sems, recv_sems, 3 * t + j, (x, y, 1 - c)).wait_recv()
        for cp in copies(outs, send_sems, recv_sems):
            cp.wait_send()

    out_shapes = [jax.ShapeDtypeStruct(t.shape, t.dtype) for t in landed]
    return dict(name="forward", ins=list(landed), out_shapes=out_shapes, n=3 * n, start=start, finish=finish, in_place=True)


def _place_own_shard(got, shards):
    s = 2 * lax.axis_index("x") + lax.axis_index("y")
    return [lax.dynamic_update_slice(g, t[:, None], (0, s, 0, 0, 0)) for g, t in zip(got, shards)]


def _gather_small(v):
    def body(v_ref, out_ref, send_sems, recv_sems, local_sems):
        x, y, c, chips = _where_am_i()
        s = 2 * x + y
        mine = pltpu.make_async_copy(v_ref, out_ref.at[s], local_sems.at[0])
        mine.start()
        sends = [_remote(v_ref, out_ref.at[s], send_sems, recv_sems, j, (cx, cy, c)) for j, (cx, cy) in enumerate(chips)]
        for cp in sends:
            cp.start()
        for j, (cx, cy) in enumerate(chips):
            land = out_ref.at[2 * cx + cy]
            _remote(land, land, send_sems, recv_sems, j, (cx, cy, c)).wait_recv()
        for cp in sends:
            cp.wait_send()
        mine.wait()

    return _comm_call(body, "gather_small", [v], [jax.ShapeDtypeStruct((N_CHIPS,) + v.shape, v.dtype)], 3, 1)[0]


def _swap_sibling_halves(grads):
    n = len(grads)

    def body(*refs):
        ins, outs = refs[:n], refs[n:2 * n]
        send_sems, recv_sems, _ = refs[2 * n:]
        x, y, c, _chips = _where_am_i()
        sib = (x, y, 1 - c)
        cps = [_remote(ins[t].at[:, :, 1 - c], outs[t], send_sems, recv_sems, t, sib) for t in range(n)]
        for cp in cps:
            cp.start()
        for cp in cps:
            cp.wait()

    out_shapes = [jax.ShapeDtypeStruct(g.shape[:2] + g.shape[3:], g.dtype) for g in grads]
    return _comm_call(body, "grad_swap_halves", grads, out_shapes, n, 0)


def _exchange_chips(parts):
    n = len(parts)

    def body(*refs):
        ins, outs = refs[:n], refs[n:2 * n]
        send_sems, recv_sems, _ = refs[2 * n:]
        x, y, c, chips = _where_am_i()
        cps = []
        for t in range(n):
            for j, (cx, cy) in enumerate(chips):
                cps.append(_remote(ins[t].at[:, 2 * cx + cy], outs[t].at[j], send_sems, recv_sems, 3 * t + j, (cx, cy, c)))
        for cp in cps:
            cp.start()
        for cp in cps:
            cp.wait()

    out_shapes = [jax.ShapeDtypeStruct((3, p.shape[0]) + p.shape[2:], p.dtype) for p in parts]
    return _comm_call(body, "grad_exchange_chips", parts, out_shapes, 3 * n, 0)


def _exchange_rider(parts):
    n = len(parts)

    def copies(ins, outs, send_sems, recv_sems):
        x, y, c, chips = _where_am_i()
        return [_remote(ins[t].at[:, 2 * cx + cy], outs[t].at[j], send_sems, recv_sems, 3 * t + j, (cx, cy, c))
                for t in range(n) for j, (cx, cy) in enumerate(chips)]

    def start(ins, outs, send_sems, recv_sems):
        for cp in copies(ins, outs, send_sems, recv_sems):
            cp.start()

    def finish(ins, outs, send_sems, recv_sems):
        for cp in copies(ins, outs, send_sems, recv_sems):
            cp.wait()

    out_shapes = [jax.ShapeDtypeStruct((3, p.shape[0]) + p.shape[2:], p.dtype) for p in parts]
    return dict(name="exchange", ins=list(parts), out_shapes=out_shapes, n=3 * n, start=start, finish=finish)


def _join_sibling_halves(halves):
    n = len(halves)

    def body(*refs):
        ins, outs = refs[:n], refs[n:2 * n]
        send_sems, recv_sems, local_sems = refs[2 * n:]
        x, y, c, _chips = _where_am_i()
        sib = (x, y, 1 - c)
        cps = [_remote(ins[t], outs[t].at[:, c], send_sems, recv_sems, t, sib) for t in range(n)]
        for cp in cps:
            cp.start()
        for t in range(n):
            land = outs[t].at[:, 1 - c]
            _remote(land, land, send_sems, recv_sems, t, sib).wait_recv()
        for cp in cps:
            cp.wait_send()

    out_shapes = [jax.ShapeDtypeStruct((h.shape[0], 2) + h.shape[1:], h.dtype) for h in halves]
    got = _comm_call(body, "grad_join_halves", halves, out_shapes, n, 0)
    c = lax.axis_index("c")
    return [lax.dynamic_update_slice(g, h[:, None], (0, c, 0, 0)) for g, h in zip(got, halves)]


def _join_rider(halves):
    n = len(halves)

    def copies(ins, outs, send_sems, recv_sems):
        x, y, c, _chips = _where_am_i()
        return [_remote(ins[t], outs[t].at[:, c], send_sems, recv_sems, t, (x, y, 1 - c)) for t in range(n)]

    def start(ins, outs, send_sems, recv_sems):
        for cp in copies(ins, outs, send_sems, recv_sems):
            cp.start()

    def finish(ins, outs, send_sems, recv_sems):
        x, y, c, _chips = _where_am_i()
        for t in range(n):
            land = outs[t].at[:, 1 - c]
            _remote(land, land, send_sems, recv_sems, t, (x, y, 1 - c)).wait_recv()
        for cp in copies(ins, outs, send_sems, recv_sems):
            cp.wait_send()

    out_shapes = [jax.ShapeDtypeStruct((h.shape[0], 2) + h.shape[1:], h.dtype) for h in halves]
    return dict(name="join", ins=list(halves), out_shapes=out_shapes, n=n, start=start, finish=finish)


def _place_own_half(got, halves):
    c = lax.axis_index("c")
    return [lax.dynamic_update_slice(g, h[:, None], (0, c, 0, 0)) for g, h in zip(got, halves)]


def _small_exchange_rider(v):
    def copies(ins, outs, send_sems, recv_sems):
        x, y, c, chips = _where_am_i()
        return [_remote(ins[0], outs[0].at[j], send_sems, recv_sems, j, (cx, cy, c)) for j, (cx, cy) in enumerate(chips)]

    def start(ins, outs, send_sems, recv_sems):
        for cp in copies(ins, outs, send_sems, recv_sems):
            cp.start()

    def finish(ins, outs, send_sems, recv_sems):
        for cp in copies(ins, outs, send_sems, recv_sems):
            cp.wait()

    return dict(name="small_exchange", ins=[v], out_shapes=[jax.ShapeDtypeStruct((3,) + v.shape, v.dtype)], n=3,
                start=start, finish=finish)


def _swap_small(v):
    def body(v_ref, out_ref, send_sems, recv_sems, _):
        x, y, c, _chips = _where_am_i()
        cp = _remote(v_ref, out_ref, send_sems, recv_sems, 0, (x, y, 1 - c))
        cp.start()
        cp.wait()

    return _comm_call(body, "small_swap", [v], [jax.ShapeDtypeStruct(v.shape, v.dtype)], 1, 0)[0]


def _sum_rows(name, terms, out_dtypes):
    R, Cc = terms[0].shape
    tm = R if R <= 256 else max(t for t in range(16, 257, 16) if R % t == 0)
    n = len(terms)

    def body(*refs):
        acc = refs[0][...].astype(F32)
        for r in refs[1:n]:
            acc = acc + r[...].astype(F32)
        for o in refs[n:]:
            o[...] = acc.astype(o.dtype)

    blk = pl.BlockSpec((tm, Cc), lambda i: (i, 0))
    return _pcall(
        body, name=name, grid=(R // tm,), in_specs=[blk] * n, out_specs=[blk] * len(out_dtypes),
        out_shape=[jax.ShapeDtypeStruct((R, Cc), d) for d in out_dtypes], compiler_params=_params("parallel"),
    )(*terms)


def _pair_sum(g5, r1, core, shard):
    A4, _, Rh, Cc = g5.shape
    A = A4 // N_CHIPS
    tr = Rh if Rh <= 256 else max(t for t in range(16, 257, 16) if Rh % t == 0)

    def body(core_ref, shard_ref, g_ref, r_ref, qb_ref, qf_ref):
        q = g_ref[...] + r_ref[...]
        qb_ref[...] = q.astype(BF16)

        @pl.when(pl.program_id(2) == shard_ref[0])
        def _():
            qf_ref[...] = q

    grid_spec = pltpu.PrefetchScalarGridSpec(
        num_scalar_prefetch=2, grid=(A, Rh // tr, N_CHIPS),
        in_specs=[pl.BlockSpec((None, None, tr, Cc), lambda a, r, sh, core, shard: (a * N_CHIPS + sh, core[0], r, 0)),
                  pl.BlockSpec((None, tr, Cc), lambda a, r, sh, core, shard: (a * N_CHIPS + sh, r, 0))],
        out_specs=[pl.BlockSpec((None, tr, Cc), lambda a, r, sh, core, shard: (a * N_CHIPS + sh, r, 0)),
                   pl.BlockSpec((None, tr, Cc), lambda a, r, sh, core, shard: (a, r, 0))],
    )
    return _pcall(
        body, name="grad_pair_sum", grid_spec=grid_spec,
        out_shape=[jax.ShapeDtypeStruct((A4, Rh, Cc), BF16), jax.ShapeDtypeStruct((A, Rh, Cc), F32)],
        compiler_params=_params("parallel", "parallel", "arbitrary"),
    )(core, shard, g5, r1)


def _swap_rider(grads):
    n = len(grads)

    def copies(ins, outs, send_sems, recv_sems):
        x, y, c, _chips = _where_am_i()
        return [_remote(ins[t].at[:, :, 1 - c], outs[t], send_sems, recv_sems, t, (x, y, 1 - c)) for t in range(n)]

    def start(ins, outs, send_sems, recv_sems):
        for cp in copies(ins, outs, send_sems, recv_sems):
            cp.start()

    def finish(ins, outs, send_sems, recv_sems):
        for cp in copies(ins, outs, send_sems, recv_sems):
            cp.wait()

    out_shapes = [jax.ShapeDtypeStruct(g.shape[:2] + g.shape[3:], g.dtype) for g in grads]
    return dict(name="swap", ins=list(grads), out_shapes=out_shapes, n=n, start=start, finish=finish)


def _reduce_chip(grads, r1, core, shard):
    qb, qf = [], []
    for g, r in zip(grads, r1):
        A, _, _, Rh, Cc = g.shape
        b, f = _pair_sum(g.reshape(A * N_CHIPS, 2, Rh, Cc), r.reshape(A * N_CHIPS, Rh, Cc), core, shard)
        qb.append(b.reshape(A, N_CHIPS, Rh, Cc))
        qf.append(f)
    return qb, qf


def _reduce_finish(qf, r2):
    return _as_shards(_join_sibling_halves(_chip_sums(qf, r2)))


def _chip_sums(qf, r2):
    halves = []
    for f, r in zip(qf, r2):
        A, Rh, Cc = f.shape
        terms = [f.reshape(A * Rh, Cc)] + [r[j].reshape(A * Rh, Cc) for j in range(3)]
        halves.append(_sum_rows("grad_chip_sum", terms, [F32])[0].reshape(A, Rh, Cc))
    return halves


def _as_shards(full):
    return [t.reshape(t.shape[0], 2 * t.shape[2], t.shape[3]) for t in full]


def _small_pair(v):
    return _sum_rows("small_pair_sum", [v, _swap_small(v)], [F32])[0]


def _small_chip_sum(pair, others):
    x, y = lax.axis_index("x"), lax.axis_index("y")
    s = 2 * x + y
    stack = jnp.concatenate([pair[None], others], axis=0)
    src = jnp.stack([s, s ^ 2, s ^ 1, s ^ 3])
    order = jnp.argsort(src)
    terms = [lax.dynamic_index_in_dim(stack, order[k], 0, keepdims=False) for k in range(N_CHIPS)]
    return _sum_rows("small_chip_sum", terms, [F32])[0]


def _block_diag(w):
    nb, bw, _ = w.shape
    per = LANES // bw
    w = w.reshape(nb // per, per, bw, bw)
    eye = jnp.eye(per, dtype=w.dtype)
    bd = jnp.einsum("tpij,pq->tpiqj", w, eye).reshape(nb // per, LANES, LANES)
    return bd.astype(BF16)


def _block_diag_grad(g, bw):
    nt = g.shape[0]
    per = LANES // bw
    g = g.reshape(nt, per, bw, per, bw)
    return jnp.stack([g[:, p, :, p, :] for p in range(per)], axis=1).reshape(nt * per, bw, bw)


def _split5(w):
    R, Cc = w.shape[-2:]
    return w.reshape(-1, 2, R // 2, Cc)


def kernel(x, w_in, conv_w, conv_b, w_rg, b_rg, w_ig, b_ig, lru_lambda, sinks, w_branch, w_out, ln1_g, ln1_b, w_ffn_in, w_ffn_out, ln2_g, ln2_b, loss_target, m_w_in, m_conv_w, m_conv_b, m_w_rg, m_b_rg, m_w_ig, m_b_ig, m_lru_lambda, m_sinks, m_w_branch, m_w_out, m_ln1_g, m_ln1_b, m_w_ffn_in, m_w_ffn_out, m_ln2_g, m_ln2_b, v_w_in, v_conv_w, v_conv_b, v_w_rg, v_b_rg, v_w_ig, v_b_ig, v_lru_lambda, v_sinks, v_w_branch, v_w_out, v_ln1_g, v_ln1_b, v_w_ffn_in, v_w_ffn_out, v_ln2_g, v_ln2_b):
    B, S, D = x.shape
    T = B * S
    L = w_in.shape[0]
    H = D // HEAD_DIM
    KVB = D // SWA_GROUP
    FH = w_ffn_out.shape[1] * N_CHIPS
    C = w_in.shape[2] * N_CHIPS
    alpha = (2.0 * L) ** 0.25
    off = {}
    pos = 0
    for nm, wd in (("lx", D), ("lg", D), ("qb", D), ("kb", KVB), ("vb", KVB), ("qc", D), ("kc", D), ("vc", D), ("gt", 3 * D)):
        off[nm] = pos
        pos += wd
    assert pos == C
    cx, cy, cc = lax.axis_index("x"), lax.axis_index("y"), lax.axis_index("c")
    shard = (2 * cx + cy).astype(jnp.int32)
    core_a = cc.astype(jnp.int32).reshape(1)
    shard_a = shard.reshape(1)

    def shard_views(l):
        return [_split5(w_in[l].astype(BF16)), _split5(w_branch[l].astype(BF16)), _split5(w_out[l].astype(BF16)),
                _split5(w_ffn_in[l].astype(BF16)), _split5(w_ffn_out[l].astype(BF16))]

    def as_weights(g):
        return dict(
            w_in=g[0].reshape(N_CHIPS, D, C // N_CHIPS),
            w_branch=g[1].reshape(3, D, D),
            w_out=g[2].reshape(D, D),
            w_ffn_in=g[3].reshape(N_CHIPS, D, 2 * FH // N_CHIPS),
            w_ffn_out=g[4].reshape(FH, D),
        )

    first_views = shard_views(0)
    w_in0 = _gather_weights(first_views[:1])
    full = [dict(w_in=w_in0[0].reshape(N_CHIPS, D, C // N_CHIPS))]
    cw_all = _gather_small(conv_w.reshape(L * CONV_WIDTH, D // N_CHIPS))
    conv_w_full = jnp.transpose(cw_all, (1, 0, 2)).reshape(L, CONV_WIDTH, D)

    def layer_params(l):
        return dict(conv_w=conv_w_full[l], conv_b=conv_b[l][None], w_rg_bd=_block_diag(w_rg[l]), b_rg=b_rg[l][None],
                    w_ig_bd=_block_diag(w_ig[l]), b_ig=b_ig[l][None], lam=lru_lambda[l][None])

    def sink_rows(l, hb):
        sk = sinks[l].reshape(H // hb, 1, hb)
        return jnp.pad(sk, ((0, 0), (0, 0), (0, LANES - hb)))

    hb_b = SWA_HB

    saved = []
    xin = x.reshape(T, D)
    for l in range(L):
        fw, lp = full[l], layer_params(l)
        nxt = shard_views(l + 1) if l + 1 < L else None
        own, ahead = {}, {}
        if l == 0:
            own = {"lru": (3,), "swa": (1, 2, 4)}
            ahead = {"proj": (0,), "dil": (1, 2, 3, 4)} if nxt is not None else {}
        elif nxt is not None:
            ahead = {"lru": (3,), "swa": (0,), "dil": (1, 2, 4)}
        landed_own, landed_next = {}, {}

        def carried(host):
            idx_own, idx_next = own.get(host, ()), ahead.get(host, ())
            views = [first_views[t] for t in idx_own] + [nxt[t] for t in idx_next]
            if not views:
                return None, lambda bufs: None

            def file(bufs):
                for t, buf in zip(idx_own, bufs[:len(idx_own)]):
                    landed_own[t] = buf
                for t, buf in zip(idx_next, bufs[len(idx_own):]):
                    landed_next[t] = buf
            return _gather_rider(views), file

        rider, file = carried("proj")
        proj_kw = dict(mode="nn", name="mm_proj", tm=512, n_outer=True)
        if rider is None:
            proj = _matmul(xin, fw["w_in"], **proj_kw)
        else:
            proj, bufs = _matmul(xin, fw["w_in"], rider=rider, **proj_kw)
            file(bufs)
        proj3 = proj.reshape(B, S, C)
        rider, file = carried("lru")
        res = _lru_fwd(proj3, lp, D=D, x_off=off["lx"], g_off=off["lg"], rider=rider)
        h3, ya3 = res[0], res[1]
        file(res[2:])
        skr = sink_rows(l, hb_b)
        rider, file = carried("swa")
        res = _swa_seq_fwd(proj3, skr, D=D, q_off=off["qb"], k_off=off["kb"], v_off=off["vb"], rider=rider)
        yb3, lse_b = res[0], res[1]
        file(res[2:])
        if l == 0:
            rest = sorted(landed_own)
            got = _gather_forward([landed_own[t] for t in rest], [first_views[t] for t in rest])
            fw = as_weights(w_in0 + got)
            full[0] = fw
        rider, file = carried("dil")
        res = _dil_fwd(proj3, D=D, q_off=off["qc"], k_off=off["kc"], v_off=off["vc"], rider=rider)
        yc3, lse_c = res[0], res[1]
        file(res[2:])
        ya, yb, yc = ya3.reshape(T, D), yb3.reshape(T, D), yc3.reshape(T, D)
        if nxt is not None:
            landed = [landed_next[t] for t in range(len(nxt))]
            res = _branch_fwd([ya, yb, yc], fw["w_branch"], proj, D=D, g_off=off["gt"], rider=_forward_rider(landed))
            branch, merged = res[0], res[1]
            full.append(as_weights(_place_own_shard(res[2:], nxt)))
        else:
            branch, merged = _branch_fwd([ya, yb, yc], fw["w_branch"], proj, D=D, g_off=off["gt"])
        z1, x1 = _matmul(merged, fw["w_out"], mode="nn", name="mm_out_ln", tn=1024, resid=xin, rs=alpha,
                         ln=(ln1_g[l][None], ln1_b[l][None]))
        hh = _matmul(x1, fw["w_ffn_in"], mode="nn", name="mm_ffn_in", n_outer=True)
        f = _swiglu_fwd(hh)
        z2, x2 = _matmul(f, fw["w_ffn_out"], mode="nn", name="mm_ffn_out_ln", tn=1024, tk=4096, resid=x1, rs=alpha,
                         ln=(ln2_g[l][None], ln2_b[l][None]))
        saved.append(dict(x=xin, proj=proj, h3=h3, ya=ya, yb=yb, lse_b=lse_b, yc=yc, lse_c=lse_c, branch=branch,
                          merged=merged, z1=z1, x1=x1, hh=hh, f=f, z2=z2, skr=skr))
        xin = x2

    dx, loss_rows = _loss_head(xin, loss_target.reshape(T, D))
    loss = lax.psum(jnp.sum(loss_rows), ("x", "y", "c"))

    big = {k: [None] * L for k in ("w_in", "w_branch", "w_out", "w_ffn_in", "w_ffn_out")}
    small = [None] * L

    def store_reduced(l, red):
        big["w_in"][l] = red[0].reshape(D, C // N_CHIPS)
        big["w_branch"][l] = red[1].reshape(3, D // N_CHIPS, D)
        big["w_out"][l] = red[2].reshape(D // N_CHIPS, D)
        big["w_ffn_in"][l] = red[3].reshape(D, 2 * FH // N_CHIPS)
        big["w_ffn_out"][l] = red[4].reshape(FH // N_CHIPS, D)

    above = None
    pending = None
    for l in reversed(range(L)):
        fw, lp, sv = full[l], layer_params(l), saved[l]
        dz2, dg2, db2 = _ln_bwd(dx, sv["z2"], ln2_g[l][None])
        df = _matmul(dz2, fw["w_ffn_out"], mode="nt", name="mm_dffn_out_x", tn=4096, tk=1024)
        g_ffn_out = _matmul(sv["f"], dz2, mode="tn", name="mm_dffn_out_w", tm=1408, tn=1024, tk=1024)
        dhh, _unused = _swiglu_bwd(sv["hh"], df)
        dx1 = _matmul(dhh, fw["w_ffn_in"], mode="nt", name="mm_dffn_in_x", tm=1024, tn=1024, resid=dz2, rs=alpha)
        g_ffn_in = _matmul(sv["x1"], dhh, mode="tn", name="mm_dffn_in_w", tm=1024, tk=1024, out_shards=N_CHIPS)
        dz1, dg1, db1 = _ln_bwd(dx1, sv["z1"], ln1_g[l][None])
        dmerged = _matmul(dz1, fw["w_out"], mode="nt", name="mm_dout_x", tn=1024, tk=1024)
        g_out = _matmul(sv["merged"], dz1, mode="tn", name="mm_dout_w", tm=1024, tn=1024, tk=1024)
        dbranch, dgates = _branch_bwd(dmerged, sv["branch"], sv["proj"], D=D, g_off=off["gt"])
        ys = [sv["ya"], sv["yb"], sv["yc"]]
        dys, g_branch = [], []
        for n in range(3):
            dys.append(_matmul(dbranch, fw["w_branch"][n], mode="nt", name="mm_dbranch_x", tn=1024, tk=1024, a_pick=n))
            g_branch.append(_matmul(ys[n], dbranch, mode="tn", name="mm_dbranch_w", tm=1024, tn=1024, tk=1024, b_pick=n))
        proj3 = sv["proj"].reshape(B, S, C)
        r3 = lambda t: t.reshape(B, S, t.shape[-1])
        lru = _lru_bwd(proj3, sv["h3"], r3(dys[0]), lp, D=D, x_off=off["lx"], g_off=off["lg"],
                       rider=None if above is None else _swap_rider(above[1]))
        if above is not None:
            pending = (above[0],) + _reduce_chip(above[1], lru[9:], core_a, shard_a)
        dxr, dgate = lru[0], lru[1]
        dqb, dkb, dvb, dsk = _swa_seq_bwd(proj3, r3(sv["yb"]), sv["lse_b"], r3(dys[1]), sv["skr"], D=D, q_off=off["qb"],
                                      k_off=off["kb"], v_off=off["vb"])
        dil_kw = dict(D=D, q_off=off["qc"], k_off=off["kc"], v_off=off["vc"])
        if pending is None:
            acc = _dil_bwd(proj3, r3(sv["yc"]), sv["lse_c"], r3(dys[2]), **dil_kw)
        else:
            res = _dil_bwd(proj3, r3(sv["yc"]), sv["lse_c"], r3(dys[2]), rider=_exchange_rider(pending[1]), **dil_kw)
            acc = res[:3]
            halves = _chip_sums(pending[2], res[3:])
        f2 = lambda t: t.reshape(T, t.shape[-1]).astype(BF16)
        dproj = jnp.concatenate([f2(dxr), f2(dgate), f2(dqb), f2(dkb), f2(dvb), f2(acc[0]), f2(acc[1]), f2(acc[2])] + dgates, axis=1)
        dx_kw = dict(mode="nt", name="mm_dproj_x", tm=1024, tn=1024, resid=dz1, rs=alpha)
        if pending is None:
            dx = _matmul(dproj, fw["w_in"], **dx_kw)
        else:
            dx, got = _matmul(dproj, fw["w_in"], rider=_join_rider(halves), **dx_kw)
            store_reduced(pending[0], _as_shards(_place_own_half(got, halves)))
        g_in = _matmul(sv["x"], dproj, mode="tn", name="mm_dproj_w", tm=512, tk=1024, out_shards=N_CHIPS)

        g5 = [g_in.reshape(1, N_CHIPS, 2, D // 2, C // N_CHIPS),
              jnp.stack(g_branch).reshape(3, N_CHIPS, 2, D // N_CHIPS // 2, D),
              g_out.reshape(1, N_CHIPS, 2, D // N_CHIPS // 2, D),
              g_ffn_in.reshape(1, N_CHIPS, 2, D // 2, 2 * FH // N_CHIPS),
              g_ffn_out.reshape(1, N_CHIPS, 2, FH // N_CHIPS // 2, D)]
        above = (l, g5)

        dsinks = jnp.sum(dsk, axis=0)[:, 0, :hb_b].reshape(H)
        bw = w_rg.shape[-1]
        small[l] = [lru[2].reshape(-1), lru[3].reshape(-1), _block_diag_grad(lru[4], bw).reshape(-1), lru[5].reshape(-1),
                    _block_diag_grad(lru[6], bw).reshape(-1), lru[7].reshape(-1), lru[8].reshape(-1),
                    jnp.pad(dsinks, (0, LANES - H)), dg1.reshape(-1), db1.reshape(-1), dg2.reshape(-1), db2.reshape(-1)]

    qb, qf = _reduce_chip(above[1], _swap_sibling_halves(above[1]), core_a, shard_a)
    store_reduced(above[0], _reduce_finish(qf, _exchange_chips(qb)))

    sizes = [t.size for t in small[0]]
    flat = jnp.concatenate([t for l in range(L) for t in small[l]])
    n_flat = flat.size
    rows = -(-n_flat // (LANES * 256)) * 256
    flat = jnp.pad(flat, (0, rows * LANES - n_flat)).reshape(rows, LANES)
    pair = _small_pair(flat)

    order = ["w_in", "conv_w", "conv_b", "w_rg", "b_rg", "w_ig", "b_ig", "lru_lambda", "sinks", "w_branch", "w_out",
             "ln1_g", "ln1_b", "w_ffn_in", "w_ffn_out", "ln2_g", "ln2_b"]
    weights = dict(w_in=w_in, conv_w=conv_w, conv_b=conv_b, w_rg=w_rg, b_rg=b_rg, w_ig=w_ig, b_ig=b_ig, lru_lambda=lru_lambda,
                   sinks=sinks, w_branch=w_branch, w_out=w_out, ln1_g=ln1_g, ln1_b=ln1_b, w_ffn_in=w_ffn_in,
                   w_ffn_out=w_ffn_out, ln2_g=ln2_g, ln2_b=ln2_b)
    ms = dict(w_in=m_w_in, conv_w=m_conv_w, conv_b=m_conv_b, w_rg=m_w_rg, b_rg=m_b_rg, w_ig=m_w_ig, b_ig=m_b_ig,
              lru_lambda=m_lru_lambda, sinks=m_sinks, w_branch=m_w_branch, w_out=m_w_out, ln1_g=m_ln1_g, ln1_b=m_ln1_b,
              w_ffn_in=m_w_ffn_in, w_ffn_out=m_w_ffn_out, ln2_g=m_ln2_g, ln2_b=m_ln2_b)
    vs = dict(w_in=v_w_in, conv_w=v_conv_w, conv_b=v_conv_b, w_rg=v_w_rg, b_rg=v_b_rg, w_ig=v_w_ig, b_ig=v_b_ig,
              lru_lambda=v_lru_lambda, sinks=v_sinks, w_branch=v_w_branch, w_out=v_w_out, ln1_g=v_ln1_g, ln1_b=v_ln1_b,
              w_ffn_in=v_w_ffn_in, w_ffn_out=v_w_ffn_out, ln2_g=v_ln2_g, ln2_b=v_ln2_b)
    grads = dict(w_in=jnp.stack(big["w_in"]), w_branch=jnp.stack(big["w_branch"]), w_out=jnp.stack(big["w_out"]),
                 w_ffn_in=jnp.stack(big["w_ffn_in"]), w_ffn_out=jnp.stack(big["w_ffn_out"]))
    deltas, new_m, new_v = {}, {}, {}
    deltas["w_in"], new_m["w_in"], new_v["w_in"], others = _adamw(w_in, grads["w_in"], m_w_in, v_w_in,
                                                                  rider=_small_exchange_rider(pair))
    red_small = _small_chip_sum(pair, others).reshape(-1)
    per_layer = sum(sizes)
    names = ["conv_w", "conv_b", "w_rg", "b_rg", "w_ig", "b_ig", "lru_lambda", "sinks", "ln1_g", "ln1_b", "ln2_g", "ln2_b"]
    sg = {nm: [] for nm in names}
    for l in range(L):
        p = l * per_layer
        for nm, sz in zip(names, sizes):
            sg[nm].append(red_small[p:p + sz])
            p += sz
    grads.update(
        conv_w=lax.dynamic_slice_in_dim(jnp.stack(sg["conv_w"]).reshape(L, CONV_WIDTH, D), shard * (D // N_CHIPS), D // N_CHIPS, axis=2),
        conv_b=jnp.stack(sg["conv_b"]), w_rg=jnp.stack(sg["w_rg"]).reshape(w_rg.shape), b_rg=jnp.stack(sg["b_rg"]),
        w_ig=jnp.stack(sg["w_ig"]).reshape(w_ig.shape), b_ig=jnp.stack(sg["b_ig"]), lru_lambda=jnp.stack(sg["lru_lambda"]),
        sinks=jnp.stack(sg["sinks"])[:, :H], ln1_g=jnp.stack(sg["ln1_g"]), ln1_b=jnp.stack(sg["ln1_b"]),
        ln2_g=jnp.stack(sg["ln2_g"]), ln2_b=jnp.stack(sg["ln2_b"]),
    )

    for nm in order[1:]:
        deltas[nm], new_m[nm], new_v[nm] = _adamw(weights[nm], grads[nm], ms[nm], vs[nm])
    return (loss, dx.reshape(B, S, D), *[grads[nm] for nm in order], *[deltas[nm] for nm in order],
            *[new_m[nm] for nm in order], *[new_v[nm] for nm in order])
```

```python
import math

import jax
import jax.numpy as jnp
from jax import lax
from jax.experimental import pallas as pl
from jax.experimental.pallas import tpu as pltpu

HEAD_DIM = 64
WIN = 128
DILS = (1, 4, 16)
SWA_GROUP = 4
CONV_WIDTH = 4
LRU_C = 8.0
LN_EPS = 1e-5
NEG_INF = -1e30
N_CHIPS = 4
ADAM_LR, ADAM_B1, ADAM_B2, ADAM_EPS, ADAM_WD, ADAM_STEP = 0.001, 0.9, 0.999, 1e-08, 0.01, 10

LANES = 128
SUBLANES = 8
VMEM_LIMIT = 48 * 1024 * 1024

assert math.log2(HEAD_DIM) % 2 == 0

F32 = jnp.float32
BF16 = jnp.bfloat16
MESH = pl.DeviceIdType.MESH
ANY = pl.BlockSpec(memory_space=pl.ANY)


def _pcall(body, **kw):
    return pl.pallas_call(body, **kw)


def _pcall_comm(body, **kw):
    return pl.pallas_call(body, **kw)


def _params(*sem):
    return pltpu.CompilerParams(dimension_semantics=tuple(sem), vmem_limit_bytes=VMEM_LIMIT)


def _tile(dim, target):
    if dim <= target:
        return dim
    best = None
    for t in range(LANES, target + 1, LANES):
        if dim % t == 0:
            best = t
    assert best is not None, (dim, target)
    return best


def _sigmoid(x):
    return 1.0 / (1.0 + jnp.exp(-x))


def _dot(a, b, dims):
    return lax.dot_general(a, b, (dims, ((), ())), preferred_element_type=F32)


def _dot_nn(a, b):
    return _dot(a, b, ((1,), (0,)))


def _dot_nt(a, b):
    return _dot(a, b, ((1,), (1,)))


def _dot_tn(a, b):
    return _dot(a, b, ((0,), (0,)))


def _matmul(a, b, *, mode, name, out_dtype=F32, tm=512, tn=512, tk=2048, resid=None, rs=1.0, out_shards=0, n_outer=False,
            ln=None, a_pick=0, b_pick=0, rider=None):
    b_sh = b.ndim == 3
    a_st = a.ndim == 3
    if mode == "nn":
        M, K = a.shape[-2:]
        N = b.shape[-1] * (b.shape[0] if b_sh else 1)
    elif mode == "nt":
        M, K = a.shape[-2:]
        N = b.shape[-2]
    else:
        K, M = a.shape
        N = b.shape[-1]
    tm = _tile(M, tm)
    if mode == "nn" and b_sh:
        tn = b.shape[-1]
    elif out_shards:
        tn = N // out_shards
    else:
        tn = _tile(N, tn)
    if mode == "nt" and b_sh:
        tk = b.shape[-1]
    else:
        tk = _tile(K, tk)
    nk = K // tk
    grid = (N // tn, M // tm, nk) if n_outer else (M // tm, N // tn, nk)

    def spec(shape, f):
        return pl.BlockSpec(shape, (lambda g0, g1, k: f(g1, g0, k)) if n_outer else f)

    a_rows = spec((None, tm, tk), lambda i, j, k: (a_pick, i, k)) if a_st else spec((tm, tk), lambda i, j, k: (i, k))
    if mode == "nn":
        a_spec = a_rows
        b_spec = spec((None, tk, tn), lambda i, j, k: (j, k, 0)) if b_sh else spec((tk, tn), lambda i, j, k: (k, j))
        contract = _dot_nn
    elif mode == "nt":
        a_spec = a_rows
        b_spec = spec((None, tn, tk), lambda i, j, k: (k, j, 0)) if b_sh else spec((tn, tk), lambda i, j, k: (j, k))
        contract = _dot_nt
    else:
        a_spec = spec((tk, tm), lambda i, j, k: (k, i))
        b_spec = spec((None, tk, tn), lambda i, j, k: (b_pick, k, j)) if b_sh else spec((tk, tn), lambda i, j, k: (k, j))
        contract = _dot_tn
    if out_shards:
        out_shape = jax.ShapeDtypeStruct((out_shards, M, tn), out_dtype)
        o_spec = spec((None, tm, tn), lambda i, j, k: (j, i, 0))
    else:
        out_shape = jax.ShapeDtypeStruct((M, N), out_dtype)
        o_spec = spec((tm, tn), lambda i, j, k: (i, j))
    in_specs = [a_spec, b_spec]
    args = [a, b]
    if resid is not None:
        in_specs.append(spec((tm, tn), lambda i, j, k: (i, j)))
        args.append(resid)
    if ln is not None:
        assert tn == N and resid is not None and not out_shards
        in_specs += [spec((1, N), lambda i, j, k: (0, 0))] * 2
        args += list(ln)
        out_shape = [out_shape, out_shape]
        o_spec = [o_spec, o_spec]
    n_in = len(args)

    def body(*refs):
        a_ref, b_ref = refs[:2]
        r_ref = refs[2] if resid is not None else None
        o_ref = refs[n_in]
        part = contract(a_ref[...].astype(BF16), b_ref[...].astype(BF16))

        def finish(res):
            if resid is not None:
                res = res + rs * r_ref[...]
            o_ref[...] = res.astype(out_dtype)
            if ln is not None:
                g_ref, bb_ref, y_ref = refs[n_in - 2], refs[n_in - 1], refs[n_in + 1]
                zc = res - jnp.mean(res, axis=1, keepdims=True)
                var = jnp.mean(zc * zc, axis=1, keepdims=True)
                y_ref[...] = zc * lax.rsqrt(var + LN_EPS) * g_ref[...] + bb_ref[...]

        if nk == 1:
            finish(part)
            return
        acc_ref = refs[-1]
        k = pl.program_id(2)

        @pl.when(k == 0)
        def _():
            acc_ref[...] = part

        @pl.when(jnp.logical_and(k > 0, k < nk - 1))
        def _():
            acc_ref[...] += part

        @pl.when(k == nk - 1)
        def _():
            finish(acc_ref[...] + part)

    if rider is None:
        return _pcall(
            body, name=name, grid=grid, in_specs=in_specs, out_specs=o_spec, out_shape=out_shape,
            scratch_shapes=[pltpu.VMEM((tm, tn), F32)] if nk > 1 else [],
            compiler_params=_params("parallel", "parallel", "arbitrary"),
        )(*args)
    assert ln is None
    res = _call_with_rider(
        rider, body, name=name, grid=grid, in_specs=in_specs, out_specs=[o_spec], out_shape=[out_shape],
        scratch_shapes=[pltpu.VMEM((tm, tn), F32)] if nk > 1 else [], args=args)
    return res[0], list(res[1:])


def _shift_down(x, d, row):
    return jnp.where(row >= d, pltpu.roll(x, d, 0), 0.0)


def _shift_up(x, d, row, n):
    return jnp.where(row < n - d, pltpu.roll(x, n - d, 0), 0.0)


def _log1p(u):
    w = 1.0 + u
    return jnp.where(w == 1.0, u, jnp.log(w) * u / (w - 1.0))


def _gelu_parts(g):
    k = math.sqrt(2.0 / math.pi)
    c = 0.044715
    t = jnp.tanh(k * (g + c * g * g * g))
    val = 0.5 * g * (1.0 + t)
    der = 0.5 * (1.0 + t) + 0.5 * g * (1.0 - t * t) * k * (1.0 + 3.0 * c * g * g)
    return val, der


def _lru_gates(xr, cw_ref, cb_ref, wrg_ref, brg_ref, wig_ref, big_ref, lam_ref, row):
    xc = cw_ref[3:4, :] * xr + cb_ref[...]
    for d in range(1, CONV_WIDTH):
        xc = xc + cw_ref[3 - d:4 - d, :] * _shift_down(xr, d, row)
    xcb = xc.astype(BF16)
    r = _sigmoid(_dot_nn(xcb, wrg_ref[...]) + brg_ref[...])
    ig = _sigmoid(_dot_nn(xcb, wig_ref[...]) + big_ref[...])
    lam = lam_ref[...]
    sp = jnp.maximum(-lam, 0.0) + _log1p(jnp.exp(-jnp.abs(lam)))
    log_a = (-LRU_C) * r * sp
    a = jnp.exp(log_a)
    y2 = 2.0 * log_a
    one_m_a2 = jnp.where(y2 > -0.01, -(y2 + 0.5 * y2 * y2 + (1.0 / 6.0) * y2 * y2 * y2), 1.0 - jnp.exp(y2))
    mult = jnp.sqrt(one_m_a2)
    return xc, r, ig, sp, a, mult


def _scan_local(a, b, row, n, reverse):
    sub = row % SUBLANES
    d = 1
    while d < SUBLANES:
        if reverse:
            keep = sub < SUBLANES - d
            a_s = jnp.where(keep, pltpu.roll(a, n - d, 0), 1.0)
            b_s = jnp.where(keep, pltpu.roll(b, n - d, 0), 0.0)
        else:
            keep = sub >= d
            a_s = jnp.where(keep, pltpu.roll(a, d, 0), 1.0)
            b_s = jnp.where(keep, pltpu.roll(b, d, 0), 0.0)
        b = a * b_s + b
        a = a * a_s
        d *= 2
    return a, b


def _scan_carry(a_ref, b_ref, out_ref, n, reverse):
    ng = n // SUBLANES

    def step(gidx, carry):
        g = (ng - 1 - gidx) if reverse else gidx
        rows = pl.ds(pl.multiple_of(g * SUBLANES, SUBLANES), SUBLANES)
        h = a_ref[rows, :] * carry + b_ref[rows, :]
        out_ref[rows, :] = h
        return h[0:1, :] if reverse else h[SUBLANES - 1:SUBLANES, :]

    lax.fori_loop(0, ng, step, jnp.zeros((1, LANES), F32), unroll=8)


def _lru_specs(B, S, D, C, x_off, g_off):
    nct = D // LANES
    seq = lambda off: pl.BlockSpec((None, S, LANES), lambda ct, b: (b, 0, off // LANES + ct))
    row = lambda r: pl.BlockSpec((r, LANES), lambda ct, b: (0, ct))
    wbd = pl.BlockSpec((None, LANES, LANES), lambda ct, b: (ct, 0, 0))
    return nct, seq, row, wbd


def _lru_fwd(proj3, lp, *, D, x_off, g_off, rider=None):
    B, S, C = proj3.shape
    nct, seq, row, wbd = _lru_specs(B, S, D, C, x_off, g_off)

    def body(xr_ref, g_ref, cw_ref, cb_ref, wrg_ref, brg_ref, wig_ref, big_ref, lam_ref, h_ref, ya_ref, a_s, b_s):
        rowi = lax.broadcasted_iota(jnp.int32, (S, LANES), 0)
        xr = xr_ref[...]
        xc, r, ig, sp, a, mult = _lru_gates(xr, cw_ref, cb_ref, wrg_ref, brg_ref, wig_ref, big_ref, lam_ref, rowi)
        al, bl = _scan_local(a, mult * (ig * xc), rowi, S, False)
        a_s[...] = al
        b_s[...] = bl
        _scan_carry(a_s, b_s, h_ref, S, False)
        gel, _ = _gelu_parts(g_ref[...])
        ya_ref[...] = (h_ref[...] * gel).astype(BF16)

    out_seq = pl.BlockSpec((None, S, LANES), lambda ct, b: (b, 0, ct))
    return _call_with_rider(
        rider, body, name="lru_fwd", grid=(nct, B),
        in_specs=[seq(x_off), seq(g_off), row(CONV_WIDTH), row(1), wbd, row(1), wbd, row(1), row(1)],
        out_specs=[out_seq, out_seq],
        out_shape=[jax.ShapeDtypeStruct((B, S, D), F32), jax.ShapeDtypeStruct((B, S, D), BF16)],
        scratch_shapes=[pltpu.VMEM((S, LANES), F32), pltpu.VMEM((S, LANES), F32)],
        args=[proj3, proj3, lp["conv_w"], lp["conv_b"], lp["w_rg_bd"], lp["b_rg"], lp["w_ig_bd"], lp["b_ig"], lp["lam"]])


def _lru_bwd(proj3, h3, dya3, lp, *, D, x_off, g_off, rider=None):
    B, S, C = proj3.shape
    nct, seq, row, wbd = _lru_specs(B, S, D, C, x_off, g_off)

    def body(xr_ref, g_ref, h_ref, dy_ref, cw_ref, cb_ref, wrg_ref, brg_ref, wig_ref, big_ref, lam_ref,
             dxr_ref, dg_ref, dcw_ref, dcb_ref, dwrg_ref, dbrg_ref, dwig_ref, dbig_ref, dlam_ref, a_s, b_s, l_s):
        first = pl.program_id(1) == 0
        rowi = lax.broadcasted_iota(jnp.int32, (S, LANES), 0)
        xr = xr_ref[...]
        xc, r, ig, sp, a, mult = _lru_gates(xr, cw_ref, cb_ref, wrg_ref, brg_ref, wig_ref, big_ref, lam_ref, rowi)
        h = h_ref[...]
        dy = dy_ref[...]
        gel, dgel = _gelu_parts(g_ref[...])
        dg_ref[...] = (dy * h * dgel).astype(BF16)
        al, bl = _scan_local(_shift_up(a, 1, rowi, S), dy * gel, rowi, S, True)
        a_s[...] = al
        b_s[...] = bl
        _scan_carry(a_s, b_s, l_s, S, True)
        lamb = l_s[...]
        u = ig * xc
        da = lamb * _shift_down(h, 1, rowi)
        dlog_a = da * a - (lamb * u) * (a * a) / mult
        du = lamb * mult
        dpre_r = (dlog_a * ((-LRU_C) * sp)) * r * (1.0 - r)
        dpre_i = (du * xc) * ig * (1.0 - ig)
        dsp = jnp.sum(dlog_a * ((-LRU_C) * r), axis=0, keepdims=True)
        dlam = dsp * (-1.0 / (1.0 + jnp.exp(lam_ref[...])))
        dpr = dpre_r.astype(BF16)
        dpi = dpre_i.astype(BF16)
        dxc = du * ig + _dot_nt(dpr, wrg_ref[...]) + _dot_nt(dpi, wig_ref[...])
        xcb = xc.astype(BF16)
        dwrg = _dot_tn(xcb, dpr)
        dwig = _dot_tn(xcb, dpi)
        dxr = cw_ref[3:4, :] * dxc
        dcw = [jnp.sum(xr * dxc, axis=0, keepdims=True)]
        for d in range(1, CONV_WIDTH):
            dxr = dxr + cw_ref[3 - d:4 - d, :] * _shift_up(dxc, d, rowi, S)
            dcw.append(jnp.sum(_shift_down(xr, d, rowi) * dxc, axis=0, keepdims=True))
        dxr_ref[...] = dxr.astype(BF16)
        dcw_rows = jnp.concatenate(dcw[::-1], axis=0)
        sums = ((dcw_ref, dcw_rows), (dcb_ref, jnp.sum(dxc, axis=0, keepdims=True)), (dwrg_ref, dwrg),
                (dbrg_ref, jnp.sum(dpre_r, axis=0, keepdims=True)), (dwig_ref, dwig),
                (dbig_ref, jnp.sum(dpre_i, axis=0, keepdims=True)), (dlam_ref, dlam))

        @pl.when(first)
        def _():
            for ref, val in sums:
                ref[...] = val

        @pl.when(jnp.logical_not(first))
        def _():
            for ref, val in sums:
                ref[...] += val

    out_seq = pl.BlockSpec((None, S, LANES), lambda ct, b: (b, 0, ct))
    f = lambda shape: jax.ShapeDtypeStruct(shape, F32)
    nb = D // LANES
    return _call_with_rider(
        rider, body, name="lru_bwd", grid=(nct, B),
        in_specs=[seq(x_off), seq(g_off), out_seq, out_seq, row(CONV_WIDTH), row(1), wbd, row(1), wbd, row(1), row(1)],
        out_specs=[out_seq, out_seq, row(CONV_WIDTH), row(1), wbd, row(1), wbd, row(1), row(1)],
        out_shape=[jax.ShapeDtypeStruct((B, S, D), BF16), jax.ShapeDtypeStruct((B, S, D), BF16),
                   f((CONV_WIDTH, D)), f((1, D)), f((nb, LANES, LANES)), f((1, D)), f((nb, LANES, LANES)), f((1, D)), f((1, D))],
        scratch_shapes=[pltpu.VMEM((S, LANES), F32)] * 3, semantics=("parallel", "arbitrary"),
        args=[proj3, proj3, h3, dya3, lp["conv_w"], lp["conv_b"], lp["w_rg_bd"], lp["b_rg"], lp["w_ig_bd"], lp["b_ig"], lp["lam"]])


def _pair_stack(x, lo):
    z = jnp.zeros_like(x)
    return jnp.concatenate([jnp.where(lo, x, z), jnp.where(lo, z, x)], axis=0).astype(BF16)


def _pair_join(y2, lo):
    return jnp.where(lo, y2[:WIN], y2[WIN:])


def _pair_col(xb):
    return jnp.concatenate([xb[:, 0:1], xb[:, HEAD_DIM:HEAD_DIM + 1]], axis=0)


def _pair_bcast(col, lo):
    return jnp.where(lo, jnp.broadcast_to(col[:WIN], (WIN, LANES)), jnp.broadcast_to(col[WIN:], (WIN, LANES)))


def _dil_rows(it, d, S):
    if d == 1:
        cur = pl.multiple_of(it * WIN, WIN)
        prev = pl.multiple_of(jnp.maximum(it - 1, 0) * WIN, WIN)
        return pl.ds(cur, WIN), pl.ds(prev, WIN), it > 0
    r, i = it % d, it // d
    cur = i * (WIN * d) + r
    prev = jnp.maximum(i - 1, 0) * (WIN * d) + r
    return pl.ds(cur, WIN, stride=d), pl.ds(prev, WIN, stride=d), i > 0


def _dil_bias(two_blocks, stack=2):
    nk = 2 * WIN if two_blocks else WIN
    qi = lax.broadcasted_iota(jnp.int32, (stack * WIN, nk), 0) & (WIN - 1)
    kj = lax.broadcasted_iota(jnp.int32, (stack * WIN, nk), 1)
    if not two_blocks:
        return jnp.where(kj <= qi, 0.0, NEG_INF), None
    cur = jnp.logical_and(kj >= WIN, kj - WIN <= qi)
    prev = jnp.logical_and(kj < WIN, kj >= qi)
    return jnp.where(jnp.logical_or(cur, prev), 0.0, NEG_INF), jnp.where(cur, 0.0, NEG_INF)


def _dil_specs(B, S, D, C, offs):
    grid = (B, D // LANES)
    seq = lambda off: pl.BlockSpec((None, S, LANES), lambda b, p: (b, 0, off // LANES + p))
    return grid, [seq(o) for o in offs], seq(0)


def _call_with_rider(rider, body, *, name, grid, in_specs, out_specs, out_shape, scratch_shapes, args, semantics=None):
    if rider is None:
        return _pcall(body, name=name, grid=grid, in_specs=in_specs, out_specs=out_specs, out_shape=out_shape,
                      scratch_shapes=scratch_shapes, compiler_params=_params(*(semantics or ("parallel",) * len(grid))))(*args)
    n_in, n_out, n_sc = len(in_specs), len(out_specs), len(scratch_shapes)
    r_in, r_out = len(rider["ins"]), len(rider["out_shapes"])

    def wrapped(*refs):
        p = 0
        own_in = refs[p:p + n_in]; p += n_in
        rid_in = refs[p:p + r_in]; p += r_in
        own_out = refs[p:p + n_out]; p += n_out
        rid_out = refs[p:p + r_out]; p += r_out
        own_sc = refs[p:p + n_sc]; p += n_sc
        send_sems, recv_sems = refs[p:p + 2]
        ids = [pl.program_id(a) for a in range(len(grid))]
        first = ids[0] == 0
        last = ids[0] == grid[0] - 1
        for a in range(1, len(grid)):
            first = jnp.logical_and(first, ids[a] == 0)
            last = jnp.logical_and(last, ids[a] == grid[a] - 1)

        @pl.when(first)
        def _():
            rider["start"](rid_in, rid_out, send_sems, recv_sems)

        body(*own_in, *own_out, *own_sc)

        @pl.when(last)
        def _():
            rider["finish"](rid_in, rid_out, send_sems, recv_sems)

    aliases = {n_in + t: n_out + t for t in range(r_in)} if rider.get("in_place") else {}
    res = _pcall_comm(
        wrapped, name=name + "_" + rider["name"], grid=grid, in_specs=list(in_specs) + [ANY] * r_in,
        out_specs=list(out_specs) + [ANY] * r_out, out_shape=list(out_shape) + list(rider["out_shapes"]),
        scratch_shapes=list(scratch_shapes) + [pltpu.SemaphoreType.DMA((rider["n"],)), pltpu.SemaphoreType.DMA((rider["n"],))],
        input_output_aliases=aliases, compiler_params=_params(*(("arbitrary",) * len(grid))),
    )(*args, *rider["ins"])
    return res


def _dil_fwd(proj3, *, D, q_off, k_off, v_off, rider=None):
    B, S, C = proj3.shape
    n_it = S // WIN
    scale = HEAD_DIM ** -0.5
    grid, in_specs, out_spec = _dil_specs(B, S, D, C, (q_off, k_off, v_off))

    def body(q_ref, k_ref, v_ref, o_ref, l_ref):
        lo = lax.broadcasted_iota(jnp.int32, (WIN, LANES), 1) < HEAD_DIM
        for c, d in enumerate(DILS):
            two = S // d > WIN
            bias_all, bias_first = _dil_bias(two)

            def step(it, _, c=c, d=d, two=two, bias_all=bias_all, bias_first=bias_first):
                cur, prev, later = _dil_rows(it, d, S)
                q2 = _pair_stack(q_ref[cur, :] * scale, lo)
                if two:
                    k2 = jnp.concatenate([k_ref[prev, :], k_ref[cur, :]], axis=0).astype(BF16)
                    v2 = jnp.concatenate([v_ref[prev, :], v_ref[cur, :]], axis=0).astype(BF16)
                    bias = jnp.where(later, bias_all, bias_first)
                else:
                    k2, v2, bias = k_ref[cur, :].astype(BF16), v_ref[cur, :].astype(BF16), bias_all
                s2 = _dot_nt(q2, k2) + bias
                m2 = jnp.max(s2, axis=1, keepdims=True)
                p2 = jnp.exp(s2 - m2)
                den = jnp.sum(p2, axis=1, keepdims=True)
                oc = _pair_join(_dot_nn(p2.astype(BF16), v2) / den, lo)
                lc = _pair_bcast(m2 + jnp.log(den), lo)
                if c == 0:
                    o_ref[cur, :] = oc
                    l_ref[cur, :] = lc
                else:
                    l_old = l_ref[cur, :]
                    mx = jnp.maximum(l_old, lc)
                    e_old, e_new = jnp.exp(l_old - mx), jnp.exp(lc - mx)
                    tot = e_old + e_new
                    o_ref[cur, :] = (e_old * o_ref[cur, :] + e_new * oc) / tot
                    l_ref[cur, :] = mx + jnp.log(tot)
                return 0

            lax.fori_loop(0, n_it, step, 0, unroll=16)

    return _call_with_rider(
        rider, body, name="dil_fwd", grid=grid, in_specs=in_specs, out_specs=[out_spec, out_spec],
        out_shape=[jax.ShapeDtypeStruct((B, S, D), F32)] * 2, scratch_shapes=[], args=[proj3, proj3, proj3])


def _dil_bwd(proj3, o3, l3, do3, *, D, q_off, k_off, v_off, rider=None):
    B, S, C = proj3.shape
    n_it = S // WIN
    scale = HEAD_DIM ** -0.5
    grid, in_specs, out_spec = _dil_specs(B, S, D, C, (q_off, k_off, v_off))

    def body(q_ref, k_ref, v_ref, o_ref, l_ref, do_ref, dq_ref, dk_ref, dv_ref, dd_s, dq_s, dk_s, dv_s):
        lo = lax.broadcasted_iota(jnp.int32, (WIN, LANES), 1) < HEAD_DIM
        lo_s = lax.broadcasted_iota(jnp.int32, (S, LANES), 1) < HEAD_DIM
        prod = do_ref[...] * o_ref[...]
        d_lo = jnp.sum(jnp.where(lo_s, prod, 0.0), axis=1, keepdims=True)
        d_hi = jnp.sum(jnp.where(lo_s, 0.0, prod), axis=1, keepdims=True)
        dd_s[...] = jnp.where(lo_s, jnp.broadcast_to(d_lo, (S, LANES)), jnp.broadcast_to(d_hi, (S, LANES)))
        dq_s[...] = jnp.zeros_like(dq_s)
        dk_s[...] = jnp.zeros_like(dk_s)
        dv_s[...] = jnp.zeros_like(dv_s)
        for d in DILS:
            two = S // d > WIN
            bias_all, bias_first = _dil_bias(two)

            def step(it, _, d=d, two=two, bias_all=bias_all, bias_first=bias_first):
                cur, prev, later = _dil_rows(it, d, S)
                q2 = _pair_stack(q_ref[cur, :] * scale, lo)
                do2 = _pair_stack(do_ref[cur, :], lo)
                l2 = _pair_col(l_ref[cur, :])
                dd2 = _pair_col(dd_s[cur, :])
                if two:
                    k2 = jnp.concatenate([k_ref[prev, :], k_ref[cur, :]], axis=0).astype(BF16)
                    v2 = jnp.concatenate([v_ref[prev, :], v_ref[cur, :]], axis=0).astype(BF16)
                    bias = jnp.where(later, bias_all, bias_first)
                else:
                    k2, v2, bias = k_ref[cur, :].astype(BF16), v_ref[cur, :].astype(BF16), bias_all
                p2 = jnp.exp(_dot_nt(q2, k2) + bias - l2)
                ds2 = (p2 * (_dot_nt(do2, v2) - dd2)).astype(BF16)
                dq_s[cur, :] += _pair_join(_dot_nn(ds2, k2), lo) * scale
                dk2 = _dot_tn(ds2, q2)
                dv2 = _dot_tn(p2.astype(BF16), do2)
                if two:
                    dk_s[prev, :] += dk2[:WIN]
                    dv_s[prev, :] += dv2[:WIN]
                    dk_s[cur, :] += dk2[WIN:]
                    dv_s[cur, :] += dv2[WIN:]
                else:
                    dk_s[cur, :] += dk2
                    dv_s[cur, :] += dv2
                return 0

            lax.fori_loop(0, n_it, step, 0, unroll=16)
        dq_ref[...] = dq_s[...].astype(BF16)
        dk_ref[...] = dk_s[...].astype(BF16)
        dv_ref[...] = dv_s[...].astype(BF16)

    return _call_with_rider(
        rider, body, name="dil_bwd", grid=grid, in_specs=in_specs + [out_spec] * 3, out_specs=[out_spec] * 3,
        out_shape=[jax.ShapeDtypeStruct((B, S, D), BF16)] * 3, scratch_shapes=[pltpu.VMEM((S, LANES), F32)] * 4,
        args=[proj3, proj3, proj3, o3, l3, do3])


SWA_HB = 2 * SWA_GROUP


def _to_half(x, src, dst, lo):
    if src != dst:
        x = pltpu.roll(x, HEAD_DIM, 1)
    return jnp.where(lo if dst == 0 else jnp.logical_not(lo), x, 0.0)


def _swa_kv(g):
    return 2 * g // SWA_GROUP


SWA_STACKS = ((0, 1), (2, 3))


def _swa_stack(ref, gs, lo, dtype, rows=slice(None)):
    parts = []
    for g in gs:
        x = ref[rows, g * LANES:(g + 1) * LANES]
        parts += [_to_half(x, 0, _swa_kv(g), lo), _to_half(x, 1, _swa_kv(g), lo)]
    return jnp.concatenate(parts, axis=0).astype(dtype)


def _swa_unstack(y, gs, lo):
    out = []
    for t, g in enumerate(gs):
        even, odd = y[2 * t * WIN:(2 * t + 1) * WIN], y[(2 * t + 1) * WIN:(2 * t + 2) * WIN]
        out.append(_to_half(even, _swa_kv(g), 0, lo) + _to_half(odd, _swa_kv(g), 1, lo))
    return out


def _swa_cols(x, gs):
    cols = []
    for g in gs:
        cols += [jnp.broadcast_to(x[:, 2 * g:2 * g + 1], (WIN, 1)), jnp.broadcast_to(x[:, 2 * g + 1:2 * g + 2], (WIN, 1))]
    return jnp.concatenate(cols, axis=0)


SWA_UNROLL = 8


def _swa_seq_specs(B, S, D, q_off, k_off, v_off):
    qw = SWA_HB * HEAD_DIM
    assert q_off % qw == 0 and k_off % LANES == 0 and v_off % LANES == 0 and D % qw == 0
    seq = lambda width, off: pl.BlockSpec((None, S, width), lambda b, hh: (b, 0, off // width + hh))
    sink = pl.BlockSpec((None, 1, LANES), lambda b, hh: (hh, 0, 0))
    return (B, D // qw), seq, sink, qw


def _swa_rows(it):
    cur = pl.ds(pl.multiple_of(it * WIN, WIN), WIN)
    prev = pl.ds(pl.multiple_of(jnp.maximum(it - 1, 0) * WIN, WIN), WIN)
    return cur, prev, it > 0


def _swa_seq_fwd(proj3, sinks, *, D, q_off, k_off, v_off, rider=None):
    B, S, C = proj3.shape
    scale = HEAD_DIM ** -0.5
    grid, seq, sink, qw = _swa_seq_specs(B, S, D, q_off, k_off, v_off)
    nhb = D // qw

    def body(q_ref, k_ref, v_ref, sk_ref, o_ref, lse_ref):
        lo = lax.broadcasted_iota(jnp.int32, (WIN, LANES), 1) < HEAD_DIM
        lane = lax.broadcasted_iota(jnp.int32, (WIN, LANES), 1)
        sk = sk_ref[...]
        biases = [_dil_bias(True, 2 * len(gs)) for gs in SWA_STACKS]

        def step(it, _):
            cur, prev, later = _swa_rows(it)
            k2 = jnp.concatenate([k_ref[prev, :], k_ref[cur, :]], axis=0).astype(BF16)
            v2 = jnp.concatenate([v_ref[prev, :], v_ref[cur, :]], axis=0).astype(BF16)
            lse_acc = jnp.zeros((WIN, LANES), F32)
            for gs, (bias_all, bias_first) in zip(SWA_STACKS, biases):
                bias = jnp.where(later, bias_all, bias_first)
                qs = _swa_stack(q_ref, gs, lo, BF16, cur)
                sks = _swa_cols(sk, gs)
                s = _dot_nt(qs, k2) * scale + bias
                m = jnp.maximum(jnp.max(s, axis=1, keepdims=True), sks)
                p = jnp.exp(s - m)
                den = jnp.sum(p, axis=1, keepdims=True) + jnp.exp(sks - m)
                for g, grp in zip(gs, _swa_unstack(_dot_nn(p.astype(BF16), v2) / den, gs, lo)):
                    o_ref[cur, g * LANES:(g + 1) * LANES] = grp
                ls = m + jnp.log(den)
                for t, g in enumerate(gs):
                    lse_acc = jnp.where(lane == 2 * g, ls[2 * t * WIN:(2 * t + 1) * WIN], lse_acc)
                    lse_acc = jnp.where(lane == 2 * g + 1, ls[(2 * t + 1) * WIN:(2 * t + 2) * WIN], lse_acc)
            lse_ref[cur, :] = lse_acc
            return 0

        lax.fori_loop(0, S // WIN, step, 0, unroll=SWA_UNROLL)

    return _call_with_rider(
        rider, body, name="swa_fwd", grid=grid,
        in_specs=[seq(qw, q_off), seq(LANES, k_off), seq(LANES, v_off), sink],
        out_specs=[seq(qw, 0), seq(LANES, 0)],
        out_shape=[jax.ShapeDtypeStruct((B, S, D), F32), jax.ShapeDtypeStruct((B, S, nhb * LANES), F32)],
        scratch_shapes=[], args=[proj3, proj3, proj3, sinks])


def _swa_seq_bwd(proj3, o3, lse3, do3, sinks, *, D, q_off, k_off, v_off):
    B, S, C = proj3.shape
    scale = HEAD_DIM ** -0.5
    grid, seq, sink, qw = _swa_seq_specs(B, S, D, q_off, k_off, v_off)
    nhb = D // qw
    KV = D // SWA_GROUP

    def body(q_ref, k_ref, v_ref, o_ref, l_ref, do_ref, sk_ref, dq_ref, dk_ref, dv_ref, dsk_ref, dk_s, dv_s):
        lo = lax.broadcasted_iota(jnp.int32, (WIN, LANES), 1) < HEAD_DIM
        lane = lax.broadcasted_iota(jnp.int32, (1, LANES), 1)
        sk = sk_ref[...]
        biases = [_dil_bias(True, 2 * len(gs)) for gs in SWA_STACKS]
        dk_s[...] = jnp.zeros_like(dk_s)
        dv_s[...] = jnp.zeros_like(dv_s)

        dsk_ref[...] = jnp.zeros_like(dsk_ref)

        def step(it, _):
            cur, prev, later = _swa_rows(it)
            k2 = jnp.concatenate([k_ref[prev, :], k_ref[cur, :]], axis=0).astype(BF16)
            v2 = jnp.concatenate([v_ref[prev, :], v_ref[cur, :]], axis=0).astype(BF16)
            lse = l_ref[cur, :]
            dk2 = jnp.zeros((2 * WIN, LANES), F32)
            dv2 = jnp.zeros((2 * WIN, LANES), F32)
            dsk_acc = jnp.zeros((1, LANES), F32)
            for gs, (bias_all, bias_first) in zip(SWA_STACKS, biases):
                bias = jnp.where(later, bias_all, bias_first)
                qs = _swa_stack(q_ref, gs, lo, BF16, cur)
                dos = _swa_stack(do_ref, gs, lo, BF16, cur)
                dds = []
                for g in gs:
                    prod = do_ref[cur, g * LANES:(g + 1) * LANES] * o_ref[cur, g * LANES:(g + 1) * LANES]
                    dds += [jnp.sum(jnp.where(lo, prod, 0.0), axis=1, keepdims=True),
                            jnp.sum(jnp.where(lo, 0.0, prod), axis=1, keepdims=True)]
                dds = jnp.concatenate(dds, axis=0)
                ls = _swa_cols(lse, gs)
                ps = jnp.exp(_dot_nt(qs, k2) * scale + bias - ls)
                dss = (ps * (_dot_nt(dos, v2) - dds) * scale).astype(BF16)
                for g, grp in zip(gs, _swa_unstack(_dot_nn(dss, k2), gs, lo)):
                    dq_ref[cur, g * LANES:(g + 1) * LANES] = grp.astype(BF16)
                dk2 = dk2 + _dot_tn(dss, qs)
                dv2 = dv2 + _dot_tn(ps.astype(BF16), dos)
                dsks = jnp.exp(_swa_cols(sk, gs) - ls) * dds
                for t, g in enumerate(gs):
                    for u in range(2):
                        rows = slice((2 * t + u) * WIN, (2 * t + u + 1) * WIN)
                        dsk_acc = dsk_acc + jnp.where(lane == 2 * g + u, -jnp.sum(dsks[rows], axis=0, keepdims=True), 0.0)
            dk_s[prev, :] += dk2[:WIN]
            dv_s[prev, :] += dv2[:WIN]
            dk_s[cur, :] += dk2[WIN:]
            dv_s[cur, :] += dv2[WIN:]
            dsk_ref[...] += dsk_acc
            return 0

        lax.fori_loop(0, S // WIN, step, 0, unroll=SWA_UNROLL)
        dk_ref[...] = dk_s[...].astype(BF16)
        dv_ref[...] = dv_s[...].astype(BF16)

    return _pcall(
        body, name="swa_bwd", grid=grid,
        in_specs=[seq(qw, q_off), seq(LANES, k_off), seq(LANES, v_off), seq(qw, 0), seq(LANES, 0), seq(qw, 0), sink],
        out_specs=[seq(qw, 0), seq(LANES, 0), seq(LANES, 0), pl.BlockSpec((None, None, 1, LANES), lambda b, hh: (b, hh, 0, 0))],
        out_shape=[jax.ShapeDtypeStruct((B, S, D), BF16), jax.ShapeDtypeStruct((B, S, KV), BF16),
                   jax.ShapeDtypeStruct((B, S, KV), BF16), jax.ShapeDtypeStruct((B, nhb, 1, LANES), F32)],
        scratch_shapes=[pltpu.VMEM((S, LANES), F32), pltpu.VMEM((S, LANES), F32)],
        compiler_params=_params("parallel", "parallel"),
    )(proj3, proj3, proj3, o3, lse3, do3, sinks)


def _branch_fwd(ys, wb, proj, *, D, g_off, rider=None):
    T = proj.shape[0]
    tm, tn = _tile(T, 256), _tile(D, 512)
    n = len(ys)

    def body(*refs):
        y_refs, w_ref, g_refs, br_ref, mg_ref = refs[:n], refs[n], refs[n + 1:2 * n + 1], refs[2 * n + 1], refs[2 * n + 2]
        acc = None
        for k in range(n):
            br = _dot_nn(y_refs[k][...].astype(BF16), w_ref[k])
            br_ref[k] = br
            term = _sigmoid(g_refs[k][...]) * br
            acc = term if acc is None else acc + term
        mg_ref[...] = acc.astype(BF16)

    gate = lambda k: pl.BlockSpec((tm, tn), lambda i, j: (i, (g_off + k * D) // tn + j))
    return _call_with_rider(
        rider, body, name="branch_fwd", grid=(T // tm, D // tn),
        in_specs=[pl.BlockSpec((tm, D), lambda i, j: (i, 0))] * n + [pl.BlockSpec((n, D, tn), lambda i, j: (0, 0, j))]
        + [gate(k) for k in range(n)],
        out_specs=[pl.BlockSpec((n, tm, tn), lambda i, j: (0, i, j)), pl.BlockSpec((tm, tn), lambda i, j: (i, j))],
        out_shape=[jax.ShapeDtypeStruct((n, T, D), F32), jax.ShapeDtypeStruct((T, D), BF16)],
        scratch_shapes=[], args=[*ys, wb, *([proj] * n)])


def _branch_bwd(dmerged, branch, proj, *, D, g_off):
    n, T, _ = branch.shape
    tm, tn = _tile(T, 512), _tile(D, 512)

    def body(dm_ref, br_ref, *rest):
        g_refs, db_ref, dg_refs = rest[:n], rest[n], rest[n + 1:]
        dm = dm_ref[...]
        for k in range(n):
            sg = _sigmoid(g_refs[k][...])
            db_ref[k] = (sg * dm).astype(BF16)
            dg_refs[k][...] = (dm * br_ref[k] * sg * (1.0 - sg)).astype(BF16)

    gate = lambda k: pl.BlockSpec((tm, tn), lambda i, j: (i, (g_off + k * D) // tn + j))
    blk = pl.BlockSpec((tm, tn), lambda i, j: (i, j))
    res = _pcall(
        body, name="branch_bwd", grid=(T // tm, D // tn),
        in_specs=[blk, pl.BlockSpec((n, tm, tn), lambda i, j: (0, i, j))] + [gate(k) for k in range(n)],
        out_specs=[pl.BlockSpec((n, tm, tn), lambda i, j: (0, i, j))] + [blk] * n,
        out_shape=[jax.ShapeDtypeStruct((n, T, D), BF16)] + [jax.ShapeDtypeStruct((T, D), BF16)] * n,
        compiler_params=_params("parallel", "parallel"),
    )(dmerged, branch, *([proj] * n))
    return res[0], list(res[1:])


def _ln_bwd(dout, z, g):
    T, D = z.shape
    tm = _tile(T, 512)

    def body(do_ref, z_ref, g_ref, dz_ref, dg_ref, db_ref):
        z = z_ref[...]
        do = do_ref[...]
        mu = jnp.mean(z, axis=1, keepdims=True)
        zc = z - mu
        rstd = lax.rsqrt(jnp.mean(zc * zc, axis=1, keepdims=True) + LN_EPS)
        xhat = zc * rstd
        dxh = do * g_ref[...]
        dz_ref[...] = rstd * (dxh - jnp.mean(dxh, axis=1, keepdims=True) - xhat * jnp.mean(dxh * xhat, axis=1, keepdims=True))
        dg = jnp.sum(do * xhat, axis=0, keepdims=True)
        db = jnp.sum(do, axis=0, keepdims=True)
        first = pl.program_id(0) == 0

        @pl.when(first)
        def _():
            dg_ref[...] = dg
            db_ref[...] = db

        @pl.when(jnp.logical_not(first))
        def _():
            dg_ref[...] += dg
            db_ref[...] += db

    blk = pl.BlockSpec((tm, D), lambda i: (i, 0))
    vec = pl.BlockSpec((1, D), lambda i: (0, 0))
    return _pcall(
        body, name="ln_bwd", grid=(T // tm,), in_specs=[blk, blk, vec], out_specs=[blk, vec, vec],
        out_shape=[jax.ShapeDtypeStruct((T, D), F32), jax.ShapeDtypeStruct((1, D), F32), jax.ShapeDtypeStruct((1, D), F32)],
        compiler_params=_params("arbitrary"),
    )(dout, z, g)


def _ffn_in_fwd(x1, w_sh):
    T, D = x1.shape
    ns, _, nsh = w_sh.shape
    half = ns // 2
    Fh = half * nsh
    tm = _tile(T, 512)

    def body(x_ref, wa_ref, wb_ref, h1_ref, h3_ref, f_ref):
        xb = x_ref[...].astype(BF16)
        h1 = _dot_nn(xb, wa_ref[...])
        h3 = _dot_nn(xb, wb_ref[...])
        h1_ref[...] = h1
        h3_ref[...] = h3
        f_ref[...] = (h1 * _sigmoid(h1) * h3).astype(BF16)

    cols = pl.BlockSpec((tm, nsh), lambda j, i: (i, j))
    return _pcall(
        body, name="ffn_in_fwd", grid=(half, T // tm),
        in_specs=[pl.BlockSpec((tm, D), lambda j, i: (i, 0)), pl.BlockSpec((None, D, nsh), lambda j, i: (j, 0, 0)),
                  pl.BlockSpec((None, D, nsh), lambda j, i: (j + half, 0, 0))],
        out_specs=[cols, cols, cols],
        out_shape=[jax.ShapeDtypeStruct((T, Fh), F32), jax.ShapeDtypeStruct((T, Fh), F32), jax.ShapeDtypeStruct((T, Fh), BF16)],
        compiler_params=_params("parallel", "parallel"),
    )(x1, w_sh, w_sh)


def _ffn_out_bwd(dy, w_ffn_out, h1, h3):
    T, Fh = h1.shape
    D = w_ffn_out.shape[1]
    tm = _tile(T, 256)

    def body(dy_ref, w_ref, h1_ref, h3_ref, o_ref):
        d = _dot_nt(dy_ref[...].astype(BF16), w_ref[...])
        h1v = h1_ref[...]
        sg = _sigmoid(h1v)
        o_ref[:, :Fh] = (d * h3_ref[...] * sg * (1.0 + h1v * (1.0 - sg))).astype(BF16)
        o_ref[:, Fh:] = (d * h1v * sg).astype(BF16)

    blk = pl.BlockSpec((tm, Fh), lambda i: (i, 0))
    return _pcall(
        body, name="ffn_out_bwd", grid=(T // tm,),
        in_specs=[pl.BlockSpec((tm, D), lambda i: (i, 0)), pl.BlockSpec((Fh, D), lambda i: (0, 0)), blk, blk],
        out_specs=pl.BlockSpec((tm, 2 * Fh), lambda i: (i, 0)),
        out_shape=jax.ShapeDtypeStruct((T, 2 * Fh), BF16), compiler_params=_params("parallel"),
    )(dy, w_ffn_out, h1, h3)


def _loss_head(y, target):
    T, D = y.shape
    tm = _tile(T, 512)

    def body(y_ref, t_ref, dy_ref, l_ref):
        e = y_ref[...] - t_ref[...]
        dy_ref[...] = e * (1.0 / D)
        sq = e * e
        part = sq[:, 0:LANES]
        for c in range(1, D // LANES):
            part = part + sq[:, c * LANES:(c + 1) * LANES]
        part = jnp.sum(part, axis=0, keepdims=True) * (0.5 / D)
        first = pl.program_id(0) == 0

        @pl.when(first)
        def _():
            l_ref[...] = part

        @pl.when(jnp.logical_not(first))
        def _():
            l_ref[...] += part

    blk = pl.BlockSpec((tm, D), lambda i: (i, 0))
    return _pcall(
        body, name="loss_head", grid=(T // tm,), in_specs=[blk, blk],
        out_specs=[blk, pl.BlockSpec((1, LANES), lambda i: (0, 0))],
        out_shape=[jax.ShapeDtypeStruct((T, D), F32), jax.ShapeDtypeStruct((1, LANES), F32)],
        compiler_params=_params("arbitrary"),
    )(y, target)


def _as_rows(a):
    return a.reshape(-1, a.shape[-1])


def _adamw(w, g, m, v, rider=None):
    w2, g2, m2, v2 = (_as_rows(t) for t in (w, g, m, v))
    R, Cc = w2.shape
    cap = max(SUBLANES, min(512, (256 * 1024) // Cc))
    tm = R if (R <= cap or R % SUBLANES) else max(t for t in range(SUBLANES, cap + 1, SUBLANES) if R % t == 0)
    c1 = 1.0 - ADAM_B1 ** ADAM_STEP
    c2 = 1.0 - ADAM_B2 ** ADAM_STEP

    def body(w_ref, g_ref, m_ref, v_ref, d_ref, nm_ref, nv_ref):
        gg = g_ref[...]
        nm = ADAM_B1 * m_ref[...] + (1.0 - ADAM_B1) * gg
        nv = ADAM_B2 * v_ref[...] + (1.0 - ADAM_B2) * (gg * gg)
        d_ref[...] = (-ADAM_LR) * ((nm / c1) / (jnp.sqrt(nv / c2) + ADAM_EPS) + ADAM_WD * w_ref[...])
        nm_ref[...] = nm
        nv_ref[...] = nv

    blk = pl.BlockSpec((tm, Cc), lambda i: (i, 0))
    res = _call_with_rider(
        rider, body, name="adamw", grid=(R // tm,), in_specs=[blk] * 4, out_specs=[blk] * 3,
        out_shape=[jax.ShapeDtypeStruct((R, Cc), F32)] * 3, scratch_shapes=[], args=[w2, g2, m2, v2])
    return tuple(t.reshape(w.shape) for t in res[:3]) + tuple(res[3:])


def _where_am_i():
    x, y, c = lax.axis_index("x"), lax.axis_index("y"), lax.axis_index("c")
    chips = [(1 - x, y), (x, 1 - y), (1 - x, 1 - y)]
    return x, y, c, chips


def _remote(src, dst, send_sems, recv_sems, k, to):
    return pltpu.make_async_remote_copy(src_ref=src, dst_ref=dst, send_sem=send_sems.at[k], recv_sem=recv_sems.at[k],
                                        device_id=to, device_id_type=MESH)


def _comm_call(body, name, ins, out_shapes, n_remote, n_local):
    return _pcall_comm(
        body, name=name, in_specs=[ANY] * len(ins), out_specs=[ANY] * len(out_shapes), out_shape=out_shapes,
        scratch_shapes=[pltpu.SemaphoreType.DMA((n_remote,)), pltpu.SemaphoreType.DMA((n_remote,)),
                        pltpu.SemaphoreType.DMA((max(n_local, 1),))],
    )(*ins)


def _gather_weights(shards):
    n = len(shards)

    def body(*refs):
        ins, outs = refs[:n], refs[n:2 * n]
        send_sems, recv_sems, local_sems = refs[2 * n:]
        x, y, c, chips = _where_am_i()
        s = 2 * x + y
        sib = (x, y, 1 - c)
        first = []
        for t in range(n):
            for j, (cx, cy) in enumerate(chips):
                first.append(_remote(ins[t].at[:, c], outs[t].at[:, s, c], send_sems, recv_sems, 6 * t + j, (cx, cy, c)))
        for cp in first:
            cp.start()
        passed = []
        for j, (cx, cy) in enumerate(chips):
            sj = 2 * cx + cy
            for t in range(n):
                land = outs[t].at[:, sj, c]
                _remote(land, land, send_sems, recv_sems, 6 * t + j, (cx, cy, c)).wait_recv()
                fw = _remote(land, land, send_sems, recv_sems, 6 * t + 3 + j, sib)
                fw.start()
                passed.append(fw)
        for j, (cx, cy) in enumerate(chips):
            sj = 2 * cx + cy
            for t in range(n):
                land = outs[t].at[:, sj, 1 - c]
                _remote(land, land, send_sems, recv_sems, 6 * t + 3 + j, sib).wait_recv()
        for cp in first + passed:
            cp.wait_send()

    out_shapes = [jax.ShapeDtypeStruct((t.shape[0], N_CHIPS) + t.shape[1:], t.dtype) for t in shards]
    got = _comm_call(body, "gather_weights", shards, out_shapes, 6 * n, 0)
    s = 2 * lax.axis_index("x") + lax.axis_index("y")
    return [lax.dynamic_update_slice(g, t[:, None], (0, s, 0, 0, 0)) for g, t in zip(got, shards)]


def _gather_rider(shards):
    n = len(shards)

    def copies(ins, outs, send_sems, recv_sems):
        x, y, c, chips = _where_am_i()
        s = 2 * x + y
        return [_remote(ins[t].at[:, c], outs[t].at[:, s, c], send_sems, recv_sems, 3 * t + j, (cx, cy, c))
                for t in range(n) for j, (cx, cy) in enumerate(chips)]

    def start(ins, outs, send_sems, recv_sems):
        for cp in copies(ins, outs, send_sems, recv_sems):
            cp.start()

    def finish(ins, outs, send_sems, recv_sems):
        x, y, c, chips = _where_am_i()
        for t in range(n):
            for j, (cx, cy) in enumerate(chips):
                land = outs[t].at[:, 2 * cx + cy, c]
                _remote(land, land, send_sems, recv_sems, 3 * t + j, (cx, cy, c)).wait_recv()
        for cp in copies(ins, outs, send_sems, recv_sems):
            cp.wait_send()

    out_shapes = [jax.ShapeDtypeStruct((t.shape[0], N_CHIPS) + t.shape[1:], t.dtype) for t in shards]
    return dict(name="gather", ins=list(shards), out_shapes=out_shapes, n=3 * n, start=start, finish=finish)


def _gather_forward(landed, shards):
    n = len(landed)

    def body(*refs):
        outs = refs[n:2 * n]
        send_sems, recv_sems, _ = refs[2 * n:]
        x, y, c, chips = _where_am_i()
        sib = (x, y, 1 - c)
        cps = []
        for t in range(n):
            for j, (cx, cy) in enumerate(chips):
                land = outs[t].at[:, 2 * cx + cy, c]
                cps.append(_remote(land, land, send_sems, recv_sems, 3 * t + j, sib))
        for cp in cps:
            cp.start()
        for t in range(n):
            for j, (cx, cy) in enumerate(chips):
                land = outs[t].at[:, 2 * cx + cy, 1 - c]
                _remote(land, land, send_sems, recv_sems, 3 * t + j, sib).wait_recv()
        for cp in cps:
            cp.wait_send()

    got = _pcall_comm(
        body, name="gather_forward", in_specs=[ANY] * n, out_specs=[ANY] * n,
        out_shape=[jax.ShapeDtypeStruct(t.shape, t.dtype) for t in landed], input_output_aliases={t: t for t in range(n)},
        scratch_shapes=[pltpu.SemaphoreType.DMA((3 * n,)), pltpu.SemaphoreType.DMA((3 * n,)), pltpu.SemaphoreType.DMA((1,))],
    )(*landed)
    s = 2 * lax.axis_index("x") + lax.axis_index("y")
    return [lax.dynamic_update_slice(g, t[:, None], (0, s, 0, 0, 0)) for g, t in zip(got, shards)]


def _forward_rider(landed):
    n = len(landed)

    def copies(outs, send_sems, recv_sems):
        x, y, c, chips = _where_am_i()
        cps = []
        for t in range(n):
            for j, (cx, cy) in enumerate(chips):
                land = outs[t].at[:, 2 * cx + cy, c]
                cps.append(_remote(land, land, send_sems, recv_sems, 3 * t + j, (x, y, 1 - c)))
        return cps

    def start(ins, outs, send_sems, recv_sems):
        for cp in copies(outs, send_sems, recv_sems):
            cp.start()

    def finish(ins, outs, send_sems, recv_sems):
        x, y, c, chips = _where_am_i()
        for t in range(n):
            for j, (cx, cy) in enumerate(chips):
                land = outs[t].at[:, 2 * cx + cy, 1 - c]
                _remote(land, land, send_sems, recv_sems, 3 * t + j, (x, y, 1 - c)).wait_recv()
        for cp in copies(outs, send_sems, recv_sems):
            cp.wait_send()

    out_shapes = [jax.ShapeDtypeStruct(t.shape, t.dtype) for t in landed]
    return dict(name="forward", ins=list(landed), out_shapes=out_shapes, n=3 * n, start=start, finish=finish, in_place=True)


def _place_own_shard(got, shards):
    s = 2 * lax.axis_index("x") + lax.axis_index("y")
    return [lax.dynamic_update_slice(g, t[:, None], (0, s, 0, 0, 0)) for g, t in zip(got, shards)]


def _gather_small(v):
    def body(v_ref, out_ref, send_sems, recv_sems, local_sems):
        x, y, c, chips = _where_am_i()
        s = 2 * x + y
        mine = pltpu.make_async_copy(v_ref, out_ref.at[s], local_sems.at[0])
        mine.start()
        sends = [_remote(v_ref, out_ref.at[s], send_sems, recv_sems, j, (cx, cy, c)) for j, (cx, cy) in enumerate(chips)]
        for cp in sends:
            cp.start()
        for j, (cx, cy) in enumerate(chips):
            land = out_ref.at[2 * cx + cy]
            _remote(land, land, send_sems, recv_sems, j, (cx, cy, c)).wait_recv()
        for cp in sends:
            cp.wait_send()
        mine.wait()

    return _comm_call(body, "gather_small", [v], [jax.ShapeDtypeStruct((N_CHIPS,) + v.shape, v.dtype)], 3, 1)[0]


def _swap_sibling_halves(grads):
    n = len(grads)

    def body(*refs):
        ins, outs = refs[:n], refs[n:2 * n]
        send_sems, recv_sems, _ = refs[2 * n:]
        x, y, c, _chips = _where_am_i()
        sib = (x, y, 1 - c)
        cps = [_remote(ins[t].at[:, :, 1 - c], outs[t], send_sems, recv_sems, t, sib) for t in range(n)]
        for cp in cps:
            cp.start()
        for cp in cps:
            cp.wait()

    out_shapes = [jax.ShapeDtypeStruct(g.shape[:2] + g.shape[3:], g.dtype) for g in grads]
    return _comm_call(body, "grad_swap_halves", grads, out_shapes, n, 0)


def _exchange_chips(parts):
    n = len(parts)

    def body(*refs):
        ins, outs = refs[:n], refs[n:2 * n]
        send_sems, recv_sems, _ = refs[2 * n:]
        x, y, c, chips = _where_am_i()
        cps = []
        for t in range(n):
            for j, (cx, cy) in enumerate(chips):
                cps.append(_remote(ins[t].at[:, 2 * cx + cy], outs[t].at[j], send_sems, recv_sems, 3 * t + j, (cx, cy, c)))
        for cp in cps:
            cp.start()
        for cp in cps:
            cp.wait()

    out_shapes = [jax.ShapeDtypeStruct((3, p.shape[0]) + p.shape[2:], p.dtype) for p in parts]
    return _comm_call(body, "grad_exchange_chips", parts, out_shapes, 3 * n, 0)


def _exchange_rider(parts):
    n = len(parts)

    def copies(ins, outs, send_sems, recv_sems):
        x, y, c, chips = _where_am_i()
        return [_remote(ins[t].at[:, 2 * cx + cy], outs[t].at[j], send_sems, recv_sems, 3 * t + j, (cx, cy, c))
                for t in range(n) for j, (cx, cy) in enumerate(chips)]

    def start(ins, outs, send_sems, recv_sems):
        for cp in copies(ins, outs, send_sems, recv_sems):
            cp.start()

    def finish(ins, outs, send_sems, recv_sems):
        for cp in copies(ins, outs, send_sems, recv_sems):
            cp.wait()

    out_shapes = [jax.ShapeDtypeStruct((3, p.shape[0]) + p.shape[2:], p.dtype) for p in parts]
    return dict(name="exchange", ins=list(parts), out_shapes=out_shapes, n=3 * n, start=start, finish=finish)


def _join_sibling_halves(halves):
    n = len(halves)

    def body(*refs):
        ins, outs = refs[:n], refs[n:2 * n]
        send_sems, recv_sems, local_sems = refs[2 * n:]
        x, y, c, _chips = _where_am_i()
        sib = (x, y, 1 - c)
        cps = [_remote(ins[t], outs[t].at[:, c], send_sems, recv_sems, t, sib) for t in range(n)]
        for cp in cps:
            cp.start()
        for t in range(n):
            land = outs[t].at[:, 1 - c]
            _remote(land, land, send_sems, recv_sems, t, sib).wait_recv()
        for cp in cps:
            cp.wait_send()

    out_shapes = [jax.ShapeDtypeStruct((h.shape[0], 2) + h.shape[1:], h.dtype) for h in halves]
    got = _comm_call(body, "grad_join_halves", halves, out_shapes, n, 0)
    c = lax.axis_index("c")
    return [lax.dynamic_update_slice(g, h[:, None], (0, c, 0, 0)) for g, h in zip(got, halves)]


def _join_rider(halves):
    n = len(halves)

    def copies(ins, outs, send_sems, recv_sems):
        x, y, c, _chips = _where_am_i()
        return [_remote(ins[t], outs[t].at[:, c], send_sems, recv_sems, t, (x, y, 1 - c)) for t in range(n)]

    def start(ins, outs, send_sems, recv_sems):
        for cp in copies(ins, outs, send_sems, recv_sems):
            cp.start()

    def finish(ins, outs, send_sems, recv_sems):
        x, y, c, _chips = _where_am_i()
        for t in range(n):
            land = outs[t].at[:, 1 - c]
            _remote(land, land, send_sems, recv_sems, t, (x, y, 1 - c)).wait_recv()
        for cp in copies(ins, outs, send_sems, recv_sems):
            cp.wait_send()

    out_shapes = [jax.ShapeDtypeStruct((h.shape[0], 2) + h.shape[1:], h.dtype) for h in halves]
    return dict(name="join", ins=list(halves), out_shapes=out_shapes, n=n, start=start, finish=finish)


def _place_own_half(got, halves):
    c = lax.axis_index("c")
    return [lax.dynamic_update_slice(g, h[:, None], (0, c, 0, 0)) for g, h in zip(got, halves)]


def _small_exchange_rider(v):
    def copies(ins, outs, send_sems, recv_sems):
        x, y, c, chips = _where_am_i()
        return [_remote(ins[0], outs[0].at[j], send_sems, recv_sems, j, (cx, cy, c)) for j, (cx, cy) in enumerate(chips)]

    def start(ins, outs, send_sems, recv_sems):
        for cp in copies(ins, outs, send_sems, recv_sems):
            cp.start()

    def finish(ins, outs, send_sems, recv_sems):
        for cp in copies(ins, outs, send_sems, recv_sems):
            cp.wait()

    return dict(name="small_exchange", ins=[v], out_shapes=[jax.ShapeDtypeStruct((3,) + v.shape, v.dtype)], n=3,
                start=start, finish=finish)


def _swap_small(v):
    def body(v_ref, out_ref, send_sems, recv_sems, _):
        x, y, c, _chips = _where_am_i()
        cp = _remote(v_ref, out_ref, send_sems, recv_sems, 0, (x, y, 1 - c))
        cp.start()
        cp.wait()

    return _comm_call(body, "small_swap", [v], [jax.ShapeDtypeStruct(v.shape, v.dtype)], 1, 0)[0]


def _sum_rows(name, terms, out_dtypes):
    R, Cc = terms[0].shape
    tm = R if R <= 256 else max(t for t in range(16, 257, 16) if R % t == 0)
    n = len(terms)

    def body(*refs):
        acc = refs[0][...].astype(F32)
        for r in refs[1:n]:
            acc = acc + r[...].astype(F32)
        for o in refs[n:]:
            o[...] = acc.astype(o.dtype)

    blk = pl.BlockSpec((tm, Cc), lambda i: (i, 0))
    return _pcall(
        body, name=name, grid=(R // tm,), in_specs=[blk] * n, out_specs=[blk] * len(out_dtypes),
        out_shape=[jax.ShapeDtypeStruct((R, Cc), d) for d in out_dtypes], compiler_params=_params("parallel"),
    )(*terms)


def _pair_sum(g5, r1, core, shard):
    A4, _, Rh, Cc = g5.shape
    A = A4 // N_CHIPS
    tr = Rh if Rh <= 256 else max(t for t in range(16, 257, 16) if Rh % t == 0)

    def body(core_ref, shard_ref, g_ref, r_ref, qb_ref, qf_ref):
        q = g_ref[...] + r_ref[...]
        qb_ref[...] = q.astype(BF16)

        @pl.when(pl.program_id(2) == shard_ref[0])
        def _():
            qf_ref[...] = q

    grid_spec = pltpu.PrefetchScalarGridSpec(
        num_scalar_prefetch=2, grid=(A, Rh // tr, N_CHIPS),
        in_specs=[pl.BlockSpec((None, None, tr, Cc), lambda a, r, sh, core, shard: (a * N_CHIPS + sh, core[0], r, 0)),
                  pl.BlockSpec((None, tr, Cc), lambda a, r, sh, core, shard: (a * N_CHIPS + sh, r, 0))],
        out_specs=[pl.BlockSpec((None, tr, Cc), lambda a, r, sh, core, shard: (a * N_CHIPS + sh, r, 0)),
                   pl.BlockSpec((None, tr, Cc), lambda a, r, sh, core, shard: (a, r, 0))],
    )
    return _pcall(
        body, name="grad_pair_sum", grid_spec=grid_spec,
        out_shape=[jax.ShapeDtypeStruct((A4, Rh, Cc), BF16), jax.ShapeDtypeStruct((A, Rh, Cc), F32)],
        compiler_params=_params("parallel", "parallel", "arbitrary"),
    )(core, shard, g5, r1)


def _swap_rider(grads):
    n = len(grads)

    def copies(ins, outs, send_sems, recv_sems):
        x, y, c, _chips = _where_am_i()
        return [_remote(ins[t].at[:, :, 1 - c], outs[t], send_sems, recv_sems, t, (x, y, 1 - c)) for t in range(n)]

    def start(ins, outs, send_sems, recv_sems):
        for cp in copies(ins, outs, send_sems, recv_sems):
            cp.start()

    def finish(ins, outs, send_sems, recv_sems):
        for cp in copies(ins, outs, send_sems, recv_sems):
            cp.wait()

    out_shapes = [jax.ShapeDtypeStruct(g.shape[:2] + g.shape[3:], g.dtype) for g in grads]
    return dict(name="swap", ins=list(grads), out_shapes=out_shapes, n=n, start=start, finish=finish)


def _reduce_chip(grads, r1, core, shard):
    qb, qf = [], []
    for g, r in zip(grads, r1):
        A, _, _, Rh, Cc = g.shape
        b, f = _pair_sum(g.reshape(A * N_CHIPS, 2, Rh, Cc), r.reshape(A * N_CHIPS, Rh, Cc), core, shard)
        qb.append(b.reshape(A, N_CHIPS, Rh, Cc))
        qf.append(f)
    return qb, qf


def _reduce_finish(qf, r2):
    return _as_shards(_join_sibling_halves(_chip_sums(qf, r2)))


def _chip_sums(qf, r2):
    halves = []
    for f, r in zip(qf, r2):
        A, Rh, Cc = f.shape
        terms = [f.reshape(A * Rh, Cc)] + [r[j].reshape(A * Rh, Cc) for j in range(3)]
        halves.append(_sum_rows("grad_chip_sum", terms, [F32])[0].reshape(A, Rh, Cc))
    return halves


def _as_shards(full):
    return [t.reshape(t.shape[0], 2 * t.shape[2], t.shape[3]) for t in full]


def _small_pair(v):
    return _sum_rows("small_pair_sum", [v, _swap_small(v)], [F32])[0]


def _small_chip_sum(pair, others):
    x, y = lax.axis_index("x"), lax.axis_index("y")
    s = 2 * x + y
    stack = jnp.concatenate([pair[None], others], axis=0)
    src = jnp.stack([s, s ^ 2, s ^ 1, s ^ 3])
    order = jnp.argsort(src)
    terms = [lax.dynamic_index_in_dim(stack, order[k], 0, keepdims=False) for k in range(N_CHIPS)]
    return _sum_rows("small_chip_sum", terms, [F32])[0]


def _block_diag(w):
    nb, bw, _ = w.shape
    per = LANES // bw
    w = w.reshape(nb // per, per, bw, bw)
    eye = jnp.eye(per, dtype=w.dtype)
    bd = jnp.einsum("tpij,pq->tpiqj", w, eye).reshape(nb // per, LANES, LANES)
    return bd.astype(BF16)


def _block_diag_grad(g, bw):
    nt = g.shape[0]
    per = LANES // bw
    g = g.reshape(nt, per, bw, per, bw)
    return jnp.stack([g[:, p, :, p, :] for p in range(per)], axis=1).reshape(nt * per, bw, bw)


def _split5(w):
    R, Cc = w.shape[-2:]
    return w.reshape(-1, 2, R // 2, Cc)


def kernel(x, w_in, conv_w, conv_b, w_rg, b_rg, w_ig, b_ig, lru_lambda, sinks, w_branch, w_out, ln1_g, ln1_b, w_ffn_in, w_ffn_out, ln2_g, ln2_b, loss_target, m_w_in, m_conv_w, m_conv_b, m_w_rg, m_b_rg, m_w_ig, m_b_ig, m_lru_lambda, m_sinks, m_w_branch, m_w_out, m_ln1_g, m_ln1_b, m_w_ffn_in, m_w_ffn_out, m_ln2_g, m_ln2_b, v_w_in, v_conv_w, v_conv_b, v_w_rg, v_b_rg, v_w_ig, v_b_ig, v_lru_lambda, v_sinks, v_w_branch, v_w_out, v_ln1_g, v_ln1_b, v_w_ffn_in, v_w_ffn_out, v_ln2_g, v_ln2_b):
    B, S, D = x.shape
    T = B * S
    L = w_in.shape[0]
    H = D // HEAD_DIM
    KVB = D // SWA_GROUP
    FH = w_ffn_out.shape[1] * N_CHIPS
    C = w_in.shape[2] * N_CHIPS
    alpha = (2.0 * L) ** 0.25
    off = {}
    pos = 0
    for nm, wd in (("lx", D), ("lg", D), ("qb", D), ("kb", KVB), ("vb", KVB), ("qc", D), ("kc", D), ("vc", D), ("gt", 3 * D)):
        off[nm] = pos
        pos += wd
    assert pos == C
    cx, cy, cc = lax.axis_index("x"), lax.axis_index("y"), lax.axis_index("c")
    shard = (2 * cx + cy).astype(jnp.int32)
    core_a = cc.astype(jnp.int32).reshape(1)
    shard_a = shard.reshape(1)

    def shard_views(l):
        return [_split5(w_in[l].astype(BF16)), _split5(w_branch[l].astype(BF16)), _split5(w_out[l].astype(BF16)),
                _split5(w_ffn_in[l].astype(BF16)), _split5(w_ffn_out[l].astype(BF16))]

    def as_weights(g):
        return dict(
            w_in=g[0].reshape(N_CHIPS, D, C // N_CHIPS),
            w_branch=g[1].reshape(3, D, D),
            w_out=g[2].reshape(D, D),
            w_ffn_in=g[3].reshape(N_CHIPS, D, 2 * FH // N_CHIPS),
            w_ffn_out=g[4].reshape(FH, D),
        )

    first_views = shard_views(0)
    w_in0 = _gather_weights(first_views[:1])
    full = [dict(w_in=w_in0[0].reshape(N_CHIPS, D, C // N_CHIPS))]
    cw_all = _gather_small(conv_w.reshape(L * CONV_WIDTH, D // N_CHIPS))
    conv_w_full = jnp.transpose(cw_all, (1, 0, 2)).reshape(L, CONV_WIDTH, D)

    def layer_params(l):
        return dict(conv_w=conv_w_full[l], conv_b=conv_b[l][None], w_rg_bd=_block_diag(w_rg[l]), b_rg=b_rg[l][None],
                    w_ig_bd=_block_diag(w_ig[l]), b_ig=b_ig[l][None], lam=lru_lambda[l][None])

    def sink_rows(l, hb):
        sk = sinks[l].reshape(H // hb, 1, hb)
        return jnp.pad(sk, ((0, 0), (0, 0), (0, LANES - hb)))

    hb_b = SWA_HB

    saved = []
    xin = x.reshape(T, D)
    for l in range(L):
        fw, lp = full[l], layer_params(l)
        nxt = shard_views(l + 1) if l + 1 < L else None
        own, ahead = {}, {}
        if l == 0:
            own = {"lru": (3,), "swa": (1, 2, 4)}
            ahead = {"proj": (0,), "dil": (1, 2, 3, 4)} if nxt is not None else {}
        elif nxt is not None:
            ahead = {"lru": (3,), "swa": (0,), "dil": (1, 2, 4)}
        landed_own, landed_next = {}, {}

        def carried(host):
            idx_own, idx_next = own.get(host, ()), ahead.get(host, ())
            views = [first_views[t] for t in idx_own] + [nxt[t] for t in idx_next]
            if not views:
                return None, lambda bufs: None

            def file(bufs):
                for t, buf in zip(idx_own, bufs[:len(idx_own)]):
                    landed_own[t] = buf
                for t, buf in zip(idx_next, bufs[len(idx_own):]):
                    landed_next[t] = buf
            return _gather_rider(views), file

        rider, file = carried("proj")
        proj_kw = dict(mode="nn", name="mm_proj", tm=512, n_outer=True)
        if rider is None:
            proj = _matmul(xin, fw["w_in"], **proj_kw)
        else:
            proj, bufs = _matmul(xin, fw["w_in"], rider=rider, **proj_kw)
            file(bufs)
        proj3 = proj.reshape(B, S, C)
        rider, file = carried("lru")
        res = _lru_fwd(proj3, lp, D=D, x_off=off["lx"], g_off=off["lg"], rider=rider)
        h3, ya3 = res[0], res[1]
        file(res[2:])
        skr = sink_rows(l, hb_b)
        rider, file = carried("swa")
        res = _swa_seq_fwd(proj3, skr, D=D, q_off=off["qb"], k_off=off["kb"], v_off=off["vb"], rider=rider)
        yb3, lse_b = res[0], res[1]
        file(res[2:])
        if l == 0:
            rest = sorted(landed_own)
            got = _gather_forward([landed_own[t] for t in rest], [first_views[t] for t in rest])
            fw = as_weights(w_in0 + got)
            full[0] = fw
        rider, file = carried("dil")
        res = _dil_fwd(proj3, D=D, q_off=off["qc"], k_off=off["kc"], v_off=off["vc"], rider=rider)
        yc3, lse_c = res[0], res[1]
        file(res[2:])
        ya, yb, yc = ya3.reshape(T, D), yb3.reshape(T, D), yc3.reshape(T, D)
        if nxt is not None:
            landed = [landed_next[t] for t in range(len(nxt))]
            res = _branch_fwd([ya, yb, yc], fw["w_branch"], proj, D=D, g_off=off["gt"], rider=_forward_rider(landed))
            branch, merged = res[0], res[1]
            full.append(as_weights(_place_own_shard(res[2:], nxt)))
        else:
            branch, merged = _branch_fwd([ya, yb, yc], fw["w_branch"], proj, D=D, g_off=off["gt"])
        z1, x1 = _matmul(merged, fw["w_out"], mode="nn", name="mm_out_ln", tn=1024, resid=xin, rs=alpha,
                         ln=(ln1_g[l][None], ln1_b[l][None]))
        ffn_h1, ffn_h3, f = _ffn_in_fwd(x1, fw["w_ffn_in"])
        z2, x2 = _matmul(f, fw["w_ffn_out"], mode="nn", name="mm_ffn_out_ln", tn=1024, tk=4096, resid=x1, rs=alpha,
                         ln=(ln2_g[l][None], ln2_b[l][None]))
        saved.append(dict(x=xin, proj=proj, h3=h3, ya=ya, yb=yb, lse_b=lse_b, yc=yc, lse_c=lse_c, branch=branch,
                          merged=merged, z1=z1, x1=x1, ffn_h1=ffn_h1, ffn_h3=ffn_h3, f=f, z2=z2, skr=skr))
        xin = x2

    dx, loss_rows = _loss_head(xin, loss_target.reshape(T, D))
    loss = lax.psum(jnp.sum(loss_rows), ("x", "y", "c"))

    big = {k: [None] * L for k in ("w_in", "w_branch", "w_out", "w_ffn_in", "w_ffn_out")}
    small = [None] * L

    def store_reduced(l, red):
        big["w_in"][l] = red[0].reshape(D, C // N_CHIPS)
        big["w_branch"][l] = red[1].reshape(3, D // N_CHIPS, D)
        big["w_out"][l] = red[2].reshape(D // N_CHIPS, D)
        big["w_ffn_in"][l] = red[3].reshape(D, 2 * FH // N_CHIPS)
        big["w_ffn_out"][l] = red[4].reshape(FH // N_CHIPS, D)

    above = None
    pending = None
    for l in reversed(range(L)):
        fw, lp, sv = full[l], layer_params(l), saved[l]
        dz2, dg2, db2 = _ln_bwd(dx, sv["z2"], ln2_g[l][None])
        g_ffn_out = _matmul(sv["f"], dz2, mode="tn", name="mm_dffn_out_w", tm=1408, tn=1024, tk=1024)
        dhh = _ffn_out_bwd(dz2, fw["w_ffn_out"], sv["ffn_h1"], sv["ffn_h3"])
        dx1 = _matmul(dhh, fw["w_ffn_in"], mode="nt", name="mm_dffn_in_x", tm=1024, tn=1024, resid=dz2, rs=alpha)
        g_ffn_in = _matmul(sv["x1"], dhh, mode="tn", name="mm_dffn_in_w", tm=1024, tk=1024, out_shards=N_CHIPS)
        dz1, dg1, db1 = _ln_bwd(dx1, sv["z1"], ln1_g[l][None])
        dmerged = _matmul(dz1, fw["w_out"], mode="nt", name="mm_dout_x", tn=1024, tk=1024)
        g_out = _matmul(sv["merged"], dz1, mode="tn", name="mm_dout_w", tm=1024, tn=1024, tk=1024)
        dbranch, dgates = _branch_bwd(dmerged, sv["branch"], sv["proj"], D=D, g_off=off["gt"])
        ys = [sv["ya"], sv["yb"], sv["yc"]]
        dys, g_branch = [], []
        for n in range(3):
            dys.append(_matmul(dbranch, fw["w_branch"][n], mode="nt", name="mm_dbranch_x", tn=1024, tk=1024, a_pick=n))
            g_branch.append(_matmul(ys[n], dbranch, mode="tn", name="mm_dbranch_w", tm=1024, tn=1024, tk=1024, b_pick=n))
        proj3 = sv["proj"].reshape(B, S, C)
        r3 = lambda t: t.reshape(B, S, t.shape[-1])
        lru = _lru_bwd(proj3, sv["h3"], r3(dys[0]), lp, D=D, x_off=off["lx"], g_off=off["lg"],
                       rider=None if above is None else _swap_rider(above[1]))
        if above is not None:
            pending = (above[0],) + _reduce_chip(above[1], lru[9:], core_a, shard_a)
        dxr, dgate = lru[0], lru[1]
        dqb, dkb, dvb, dsk = _swa_seq_bwd(proj3, r3(sv["yb"]), sv["lse_b"], r3(dys[1]), sv["skr"], D=D, q_off=off["qb"],
                                      k_off=off["kb"], v_off=off["vb"])
        dil_kw = dict(D=D, q_off=off["qc"], k_off=off["kc"], v_off=off["vc"])
        if pending is None:
            acc = _dil_bwd(proj3, r3(sv["yc"]), sv["lse_c"], r3(dys[2]), **dil_kw)
        else:
            res = _dil_bwd(proj3, r3(sv["yc"]), sv["lse_c"], r3(dys[2]), rider=_exchange_rider(pending[1]), **dil_kw)
            acc = res[:3]
            halves = _chip_sums(pending[2], res[3:])
        f2 = lambda t: t.reshape(T, t.shape[-1]).astype(BF16)
        dproj = jnp.concatenate([f2(dxr), f2(dgate), f2(dqb), f2(dkb), f2(dvb), f2(acc[0]), f2(acc[1]), f2(acc[2])] + dgates, axis=1)
        dx_kw = dict(mode="nt", name="mm_dproj_x", tm=1024, tn=1024, resid=dz1, rs=alpha)
        if pending is None:
            dx = _matmul(dproj, fw["w_in"], **dx_kw)
        else:
            dx, got = _matmul(dproj, fw["w_in"], rider=_join_rider(halves), **dx_kw)
            store_reduced(pending[0], _as_shards(_place_own_half(got, halves)))
        g_in = _matmul(sv["x"], dproj, mode="tn", name="mm_dproj_w", tm=512, tk=1024, out_shards=N_CHIPS)

        g5 = [g_in.reshape(1, N_CHIPS, 2, D // 2, C // N_CHIPS),
              jnp.stack(g_branch).reshape(3, N_CHIPS, 2, D // N_CHIPS // 2, D),
              g_out.reshape(1, N_CHIPS, 2, D // N_CHIPS // 2, D),
              g_ffn_in.reshape(1, N_CHIPS, 2, D // 2, 2 * FH // N_CHIPS),
              g_ffn_out.reshape(1, N_CHIPS, 2, FH // N_CHIPS // 2, D)]
        above = (l, g5)

        dsinks = jnp.sum(dsk, axis=0)[:, 0, :hb_b].reshape(H)
        bw = w_rg.shape[-1]
        small[l] = [lru[2].reshape(-1), lru[3].reshape(-1), _block_diag_grad(lru[4], bw).reshape(-1), lru[5].reshape(-1),
                    _block_diag_grad(lru[6], bw).reshape(-1), lru[7].reshape(-1), lru[8].reshape(-1),
                    jnp.pad(dsinks, (0, LANES - H)), dg1.reshape(-1), db1.reshape(-1), dg2.reshape(-1), db2.reshape(-1)]

    qb, qf = _reduce_chip(above[1], _swap_sibling_halves(above[1]), core_a, shard_a)
    store_reduced(above[0], _reduce_finish(qf, _exchange_chips(qb)))

    sizes = [t.size for t in small[0]]
    flat = jnp.concatenate([t for l in range(L) for t in small[l]])
    n_flat = flat.size
    rows = -(-n_flat // (LANES * 256)) * 256
    flat = jnp.pad(flat, (0, rows * LANES - n_flat)).reshape(rows, LANES)
    pair = _small_pair(flat)

    order = ["w_in", "conv_w", "conv_b", "w_rg", "b_rg", "w_ig", "b_ig", "lru_lambda", "sinks", "w_branch", "w_out",
             "ln1_g", "ln1_b", "w_ffn_in", "w_ffn_out", "ln2_g", "ln2_b"]
    weights = dict(w_in=w_in, conv_w=conv_w, conv_b=conv_b, w_rg=w_rg, b_rg=b_rg, w_ig=w_ig, b_ig=b_ig, lru_lambda=lru_lambda,
                   sinks=sinks, w_branch=w_branch, w_out=w_out, ln1_g=ln1_g, ln1_b=ln1_b, w_ffn_in=w_ffn_in,
                   w_ffn_out=w_ffn_out, ln2_g=ln2_g, ln2_b=ln2_b)
    ms = dict(w_in=m_w_in, conv_w=m_conv_w, conv_b=m_conv_b, w_rg=m_w_rg, b_rg=m_b_rg, w_ig=m_w_ig, b_ig=m_b_ig,
              lru_lambda=m_lru_lambda, sinks=m_sinks, w_branch=m_w_branch, w_out=m_w_out, ln1_g=m_ln1_g, ln1_b=m_ln1_b,
              w_ffn_in=m_w_ffn_in, w_ffn_out=m_w_ffn_out, ln2_g=m_ln2_g, ln2_b=m_ln2_b)
    vs = dict(w_in=v_w_in, conv_w=v_conv_w, conv_b=v_conv_b, w_rg=v_w_rg, b_rg=v_b_rg, w_ig=v_w_ig, b_ig=v_b_ig,
              lru_lambda=v_lru_lambda, sinks=v_sinks, w_branch=v_w_branch, w_out=v_w_out, ln1_g=v_ln1_g, ln1_b=v_ln1_b,
              w_ffn_in=v_w_ffn_in, w_ffn_out=v_w_ffn_out, ln2_g=v_ln2_g, ln2_b=v_ln2_b)
    grads = dict(w_in=jnp.stack(big["w_in"]), w_branch=jnp.stack(big["w_branch"]), w_out=jnp.stack(big["w_out"]),
                 w_ffn_in=jnp.stack(big["w_ffn_in"]), w_ffn_out=jnp.stack(big["w_ffn_out"]))
    deltas, new_m, new_v = {}, {}, {}
    deltas["w_in"], new_m["w_in"], new_v["w_in"], others = _adamw(w_in, grads["w_in"], m_w_in, v_w_in,
                                                                  rider=_small_exchange_rider(pair))
    red_small = _small_chip_sum(pair, others).reshape(-1)
    per_layer = sum(sizes)
    names = ["conv_w", "conv_b", "w_rg", "b_rg", "w_ig", "b_ig", "lru_lambda", "sinks", "ln1_g", "ln1_b", "ln2_g", "ln2_b"]
    sg = {nm: [] for nm in names}
    for l in range(L):
        p = l * per_layer
        for nm, sz in zip(names, sizes):
            sg[nm].append(red_small[p:p + sz])
            p += sz
    grads.update(
        conv_w=lax.dynamic_slice_in_dim(jnp.stack(sg["conv_w"]).reshape(L, CONV_WIDTH, D), shard * (D // N_CHIPS), D // N_CHIPS, axis=2),
        conv_b=jnp.stack(sg["conv_b"]), w_rg=jnp.stack(sg["w_rg"]).reshape(w_rg.shape), b_rg=jnp.stack(sg["b_rg"]),
        w_ig=jnp.stack(sg["w_ig"]).reshape(w_ig.shape), b_ig=jnp.stack(sg["b_ig"]), lru_lambda=jnp.stack(sg["lru_lambda"]),
        sinks=jnp.stack(sg["sinks"])[:, :H], ln1_g=jnp.stack(sg["ln1_g"]), ln1_b=jnp.stack(sg["ln1_b"]),
        ln2_g=jnp.stack(sg["ln2_g"]), ln2_b=jnp.stack(sg["ln2_b"]),
    )

    for nm in order[1:]:
        deltas[nm], new_m[nm], new_v[nm] = _adamw(weights[nm], grads[nm], ms[nm], vs[nm])
    return (loss, dx.reshape(B, S, D), *[grads[nm] for nm in order], *[deltas[nm] for nm in order],
            *[new_m[nm] for nm in order], *[new_v[nm] for nm in order])
```

```python
import math

import jax
import jax.numpy as jnp
from jax import lax
from jax.experimental import pallas as pl
from jax.experimental.pallas import tpu as pltpu

HEAD_DIM = 64
WIN = 128
DILS = (1, 4, 16)
SWA_GROUP = 4
CONV_WIDTH = 4
LRU_C = 8.0
LN_EPS = 1e-5
NEG_INF = -1e30
N_CHIPS = 4
ADAM_LR, ADAM_B1, ADAM_B2, ADAM_EPS, ADAM_WD, ADAM_STEP = 0.001, 0.9, 0.999, 1e-08, 0.01, 10

LANES = 128
SUBLANES = 8
VMEM_LIMIT = 48 * 1024 * 1024

assert math.log2(HEAD_DIM) % 2 == 0

F32 = jnp.float32
BF16 = jnp.bfloat16
MESH = pl.DeviceIdType.MESH
ANY = pl.BlockSpec(memory_space=pl.ANY)


def _pcall(body, **kw):
    return pl.pallas_call(body, **kw)


def _pcall_comm(body, **kw):
    return pl.pallas_call(body, **kw)


def _params(*sem):
    return pltpu.CompilerParams(dimension_semantics=tuple(sem), vmem_limit_bytes=VMEM_LIMIT)


def _tile(dim, target):
    if dim <= target:
        return dim
    best = None
    for t in range(LANES, target + 1, LANES):
        if dim % t == 0:
            best = t
    assert best is not None, (dim, target)
    return best


def _sigmoid(x):
    return 1.0 / (1.0 + jnp.exp(-x))


def _dot(a, b, dims):
    return lax.dot_general(a, b, (dims, ((), ())), preferred_element_type=F32)


def _dot_nn(a, b):
    return _dot(a, b, ((1,), (0,)))


def _dot_nt(a, b):
    return _dot(a, b, ((1,), (1,)))


def _dot_tn(a, b):
    return _dot(a, b, ((0,), (0,)))


def _matmul(a, b, *, mode, name, out_dtype=F32, tm=512, tn=512, tk=2048, resid=None, rs=1.0, out_shards=0, n_outer=False,
            ln=None, a_pick=0, b_pick=0, rider=None):
    b_sh = b.ndim == 3
    a_st = a.ndim == 3
    if mode == "nn":
        M, K = a.shape[-2:]
        N = b.shape[-1] * (b.shape[0] if b_sh else 1)
    elif mode == "nt":
        M, K = a.shape[-2:]
        N = b.shape[-2]
    else:
        K, M = a.shape
        N = b.shape[-1]
    tm = _tile(M, tm)
    if mode == "nn" and b_sh:
        tn = b.shape[-1]
    elif out_shards:
        tn = N // out_shards
    else:
        tn = _tile(N, tn)
    if mode == "nt" and b_sh:
        tk = b.shape[-1]
    else:
        tk = _tile(K, tk)
    nk = K // tk
    grid = (N // tn, M // tm, nk) if n_outer else (M // tm, N // tn, nk)

    def spec(shape, f):
        return pl.BlockSpec(shape, (lambda g0, g1, k: f(g1, g0, k)) if n_outer else f)

    a_rows = spec((None, tm, tk), lambda i, j, k: (a_pick, i, k)) if a_st else spec((tm, tk), lambda i, j, k: (i, k))
    if mode == "nn":
        a_spec = a_rows
        b_spec = spec((None, tk, tn), lambda i, j, k: (j, k, 0)) if b_sh else spec((tk, tn), lambda i, j, k: (k, j))
        contract = _dot_nn
    elif mode == "nt":
        a_spec = a_rows
        b_spec = spec((None, tn, tk), lambda i, j, k: (k, j, 0)) if b_sh else spec((tn, tk), lambda i, j, k: (j, k))
        contract = _dot_nt
    else:
        a_spec = spec((tk, tm), lambda i, j, k: (k, i))
        b_spec = spec((None, tk, tn), lambda i, j, k: (b_pick, k, j)) if b_sh else spec((tk, tn), lambda i, j, k: (k, j))
        contract = _dot_tn
    if out_shards:
        out_shape = jax.ShapeDtypeStruct((out_shards, M, tn), out_dtype)
        o_spec = spec((None, tm, tn), lambda i, j, k: (j, i, 0))
    else:
        out_shape = jax.ShapeDtypeStruct((M, N), out_dtype)
        o_spec = spec((tm, tn), lambda i, j, k: (i, j))
    in_specs = [a_spec, b_spec]
    args = [a, b]
    if resid is not None:
        in_specs.append(spec((tm, tn), lambda i, j, k: (i, j)))
        args.append(resid)
    if ln is not None:
        assert tn == N and resid is not None and not out_shards
        in_specs += [spec((1, N), lambda i, j, k: (0, 0))] * 2
        args += list(ln)
        out_shape = [out_shape, out_shape]
        o_spec = [o_spec, o_spec]
    n_in = len(args)

    def body(*refs):
        a_ref, b_ref = refs[:2]
        r_ref = refs[2] if resid is not None else None
        o_ref = refs[n_in]
        part = contract(a_ref[...].astype(BF16), b_ref[...].astype(BF16))

        def finish(res):
            if resid is not None:
                res = res + rs * r_ref[...]
            o_ref[...] = res.astype(out_dtype)
            if ln is not None:
                g_ref, bb_ref, y_ref = refs[n_in - 2], refs[n_in - 1], refs[n_in + 1]
                zc = res - jnp.mean(res, axis=1, keepdims=True)
                var = jnp.mean(zc * zc, axis=1, keepdims=True)
                y_ref[...] = zc * lax.rsqrt(var + LN_EPS) * g_ref[...] + bb_ref[...]

        if nk == 1:
            finish(part)
            return
        acc_ref = refs[-1]
        k = pl.program_id(2)

        @pl.when(k == 0)
        def _():
            acc_ref[...] = part

        @pl.when(jnp.logical_and(k > 0, k < nk - 1))
        def _():
            acc_ref[...] += part

        @pl.when(k == nk - 1)
        def _():
            finish(acc_ref[...] + part)

    if rider is None:
        return _pcall(
            body, name=name, grid=grid, in_specs=in_specs, out_specs=o_spec, out_shape=out_shape,
            scratch_shapes=[pltpu.VMEM((tm, tn), F32)] if nk > 1 else [],
            compiler_params=_params("parallel", "parallel", "arbitrary"),
        )(*args)
    assert ln is None
    res = _call_with_rider(
        rider, body, name=name, grid=grid, in_specs=in_specs, out_specs=[o_spec], out_shape=[out_shape],
        scratch_shapes=[pltpu.VMEM((tm, tn), F32)] if nk > 1 else [], args=args)
    return res[0], list(res[1:])


def _shift_down(x, d, row):
    return jnp.where(row >= d, pltpu.roll(x, d, 0), 0.0)


def _shift_up(x, d, row, n):
    return jnp.where(row < n - d, pltpu.roll(x, n - d, 0), 0.0)


def _log1p(u):
    w = 1.0 + u
    return jnp.where(w == 1.0, u, jnp.log(w) * u / (w - 1.0))


def _gelu_parts(g):
    k = math.sqrt(2.0 / math.pi)
    c = 0.044715
    t = jnp.tanh(k * (g + c * g * g * g))
    val = 0.5 * g * (1.0 + t)
    der = 0.5 * (1.0 + t) + 0.5 * g * (1.0 - t * t) * k * (1.0 + 3.0 * c * g * g)
    return val, der


def _lru_gates(xr, cw_ref, cb_ref, wrg_ref, brg_ref, wig_ref, big_ref, lam_ref, row):
    xc = cw_ref[3:4, :] * xr + cb_ref[...]
    for d in range(1, CONV_WIDTH):
        xc = xc + cw_ref[3 - d:4 - d, :] * _shift_down(xr, d, row)
    xcb = xc.astype(BF16)
    r = _sigmoid(_dot_nn(xcb, wrg_ref[...]) + brg_ref[...])
    ig = _sigmoid(_dot_nn(xcb, wig_ref[...]) + big_ref[...])
    lam = lam_ref[...]
    sp = jnp.maximum(-lam, 0.0) + _log1p(jnp.exp(-jnp.abs(lam)))
    log_a = (-LRU_C) * r * sp
    a = jnp.exp(log_a)
    y2 = 2.0 * log_a
    one_m_a2 = jnp.where(y2 > -0.01, -(y2 + 0.5 * y2 * y2 + (1.0 / 6.0) * y2 * y2 * y2), 1.0 - jnp.exp(y2))
    mult = jnp.sqrt(one_m_a2)
    return xc, r, ig, sp, a, mult


def _scan_local(a, b, row, n, reverse):
    sub = row % SUBLANES
    d = 1
    while d < SUBLANES:
        if reverse:
            keep = sub < SUBLANES - d
            a_s = jnp.where(keep, pltpu.roll(a, n - d, 0), 1.0)
            b_s = jnp.where(keep, pltpu.roll(b, n - d, 0), 0.0)
        else:
            keep = sub >= d
            a_s = jnp.where(keep, pltpu.roll(a, d, 0), 1.0)
            b_s = jnp.where(keep, pltpu.roll(b, d, 0), 0.0)
        b = a * b_s + b
        a = a * a_s
        d *= 2
    return a, b


def _scan_carry(a_ref, b_ref, out_ref, n, reverse):
    ng = n // SUBLANES

    def step(gidx, carry):
        g = (ng - 1 - gidx) if reverse else gidx
        rows = pl.ds(pl.multiple_of(g * SUBLANES, SUBLANES), SUBLANES)
        h = a_ref[rows, :] * carry + b_ref[rows, :]
        out_ref[rows, :] = h
        return h[0:1, :] if reverse else h[SUBLANES - 1:SUBLANES, :]

    lax.fori_loop(0, ng, step, jnp.zeros((1, LANES), F32), unroll=8)


def _lru_specs(B, S, D, C, x_off, g_off):
    nct = D // LANES
    seq = lambda off: pl.BlockSpec((None, S, LANES), lambda ct, b: (b, 0, off // LANES + ct))
    row = lambda r: pl.BlockSpec((r, LANES), lambda ct, b: (0, ct))
    wbd = pl.BlockSpec((None, LANES, LANES), lambda ct, b: (ct, 0, 0))
    return nct, seq, row, wbd


def _lru_fwd(proj3, lp, *, D, x_off, g_off, rider=None):
    B, S, C = proj3.shape
    nct, seq, row, wbd = _lru_specs(B, S, D, C, x_off, g_off)

    def body(xr_ref, g_ref, cw_ref, cb_ref, wrg_ref, brg_ref, wig_ref, big_ref, lam_ref, h_ref, ya_ref, a_s, b_s):
        rowi = lax.broadcasted_iota(jnp.int32, (S, LANES), 0)
        xr = xr_ref[...]
        xc, r, ig, sp, a, mult = _lru_gates(xr, cw_ref, cb_ref, wrg_ref, brg_ref, wig_ref, big_ref, lam_ref, rowi)
        al, bl = _scan_local(a, mult * (ig * xc), rowi, S, False)
        a_s[...] = al
        b_s[...] = bl
        _scan_carry(a_s, b_s, h_ref, S, False)
        gel, _ = _gelu_parts(g_ref[...])
        ya_ref[...] = (h_ref[...] * gel).astype(BF16)

    out_seq = pl.BlockSpec((None, S, LANES), lambda ct, b: (b, 0, ct))
    return _call_with_rider(
        rider, body, name="lru_fwd", grid=(nct, B),
        in_specs=[seq(x_off), seq(g_off), row(CONV_WIDTH), row(1), wbd, row(1), wbd, row(1), row(1)],
        out_specs=[out_seq, out_seq],
        out_shape=[jax.ShapeDtypeStruct((B, S, D), F32), jax.ShapeDtypeStruct((B, S, D), BF16)],
        scratch_shapes=[pltpu.VMEM((S, LANES), F32), pltpu.VMEM((S, LANES), F32)],
        args=[proj3, proj3, lp["conv_w"], lp["conv_b"], lp["w_rg_bd"], lp["b_rg"], lp["w_ig_bd"], lp["b_ig"], lp["lam"]])


def _lru_bwd(proj3, h3, dya3, lp, *, D, x_off, g_off, rider=None):
    B, S, C = proj3.shape
    nct, seq, row, wbd = _lru_specs(B, S, D, C, x_off, g_off)

    def body(xr_ref, g_ref, h_ref, dy_ref, cw_ref, cb_ref, wrg_ref, brg_ref, wig_ref, big_ref, lam_ref,
             dxr_ref, dg_ref, dcw_ref, dcb_ref, dwrg_ref, dbrg_ref, dwig_ref, dbig_ref, dlam_ref, a_s, b_s, l_s):
        first = pl.program_id(1) == 0
        rowi = lax.broadcasted_iota(jnp.int32, (S, LANES), 0)
        xr = xr_ref[...]
        xc, r, ig, sp, a, mult = _lru_gates(xr, cw_ref, cb_ref, wrg_ref, brg_ref, wig_ref, big_ref, lam_ref, rowi)
        h = h_ref[...]
        dy = dy_ref[...]
        gel, dgel = _gelu_parts(g_ref[...])
        dg_ref[...] = (dy * h * dgel).astype(BF16)
        al, bl = _scan_local(_shift_up(a, 1, rowi, S), dy * gel, rowi, S, True)
        a_s[...] = al
        b_s[...] = bl
        _scan_carry(a_s, b_s, l_s, S, True)
        lamb = l_s[...]
        u = ig * xc
        da = lamb * _shift_down(h, 1, rowi)
        dlog_a = da * a - (lamb * u) * (a * a) / mult
        du = lamb * mult
        dpre_r = (dlog_a * ((-LRU_C) * sp)) * r * (1.0 - r)
        dpre_i = (du * xc) * ig * (1.0 - ig)
        dsp = jnp.sum(dlog_a * ((-LRU_C) * r), axis=0, keepdims=True)
        dlam = dsp * (-1.0 / (1.0 + jnp.exp(lam_ref[...])))
        dpr = dpre_r.astype(BF16)
        dpi = dpre_i.astype(BF16)
        dxc = du * ig + _dot_nt(dpr, wrg_ref[...]) + _dot_nt(dpi, wig_ref[...])
        xcb = xc.astype(BF16)
        dwrg = _dot_tn(xcb, dpr)
        dwig = _dot_tn(xcb, dpi)
        dxr = cw_ref[3:4, :] * dxc
        dcw = [jnp.sum(xr * dxc, axis=0, keepdims=True)]
        for d in range(1, CONV_WIDTH):
            dxr = dxr + cw_ref[3 - d:4 - d, :] * _shift_up(dxc, d, rowi, S)
            dcw.append(jnp.sum(_shift_down(xr, d, rowi) * dxc, axis=0, keepdims=True))
        dxr_ref[...] = dxr.astype(BF16)
        dcw_rows = jnp.concatenate(dcw[::-1], axis=0)
        sums = ((dcw_ref, dcw_rows), (dcb_ref, jnp.sum(dxc, axis=0, keepdims=True)), (dwrg_ref, dwrg),
                (dbrg_ref, jnp.sum(dpre_r, axis=0, keepdims=True)), (dwig_ref, dwig),
                (dbig_ref, jnp.sum(dpre_i, axis=0, keepdims=True)), (dlam_ref, dlam))

        @pl.when(first)
        def _():
            for ref, val in sums:
                ref[...] = val

        @pl.when(jnp.logical_not(first))
        def _():
            for ref, val in sums:
                ref[...] += val

    out_seq = pl.BlockSpec((None, S, LANES), lambda ct, b: (b, 0, ct))
    f = lambda shape: jax.ShapeDtypeStruct(shape, F32)
    nb = D // LANES
    return _call_with_rider(
        rider, body, name="lru_bwd", grid=(nct, B),
        in_specs=[seq(x_off), seq(g_off), out_seq, out_seq, row(CONV_WIDTH), row(1), wbd, row(1), wbd, row(1), row(1)],
        out_specs=[out_seq, out_seq, row(CONV_WIDTH), row(1), wbd, row(1), wbd, row(1), row(1)],
        out_shape=[jax.ShapeDtypeStruct((B, S, D), BF16), jax.ShapeDtypeStruct((B, S, D), BF16),
                   f((CONV_WIDTH, D)), f((1, D)), f((nb, LANES, LANES)), f((1, D)), f((nb, LANES, LANES)), f((1, D)), f((1, D))],
        scratch_shapes=[pltpu.VMEM((S, LANES), F32)] * 3, semantics=("parallel", "arbitrary"),
        args=[proj3, proj3, h3, dya3, lp["conv_w"], lp["conv_b"], lp["w_rg_bd"], lp["b_rg"], lp["w_ig_bd"], lp["b_ig"], lp["lam"]])


def _pair_stack(x, lo):
    z = jnp.zeros_like(x)
    return jnp.concatenate([jnp.where(lo, x, z), jnp.where(lo, z, x)], axis=0).astype(BF16)


def _pair_join(y2, lo):
    return jnp.where(lo, y2[:WIN], y2[WIN:])


def _pair_col(xb):
    return jnp.concatenate([xb[:, 0:1], xb[:, HEAD_DIM:HEAD_DIM + 1]], axis=0)


def _pair_bcast(col, lo):
    return jnp.where(lo, jnp.broadcast_to(col[:WIN], (WIN, LANES)), jnp.broadcast_to(col[WIN:], (WIN, LANES)))


def _dil_rows(it, d, S):
    if d == 1:
        cur = pl.multiple_of(it * WIN, WIN)
        prev = pl.multiple_of(jnp.maximum(it - 1, 0) * WIN, WIN)
        return pl.ds(cur, WIN), pl.ds(prev, WIN), it > 0
    r, i = it % d, it // d
    cur = i * (WIN * d) + r
    prev = jnp.maximum(i - 1, 0) * (WIN * d) + r
    return pl.ds(cur, WIN, stride=d), pl.ds(prev, WIN, stride=d), i > 0


def _dil_bias(two_blocks, stack=2):
    nk = 2 * WIN if two_blocks else WIN
    qi = lax.broadcasted_iota(jnp.int32, (stack * WIN, nk), 0) & (WIN - 1)
    kj = lax.broadcasted_iota(jnp.int32, (stack * WIN, nk), 1)
    if not two_blocks:
        return jnp.where(kj <= qi, 0.0, NEG_INF), None
    cur = jnp.logical_and(kj >= WIN, kj - WIN <= qi)
    prev = jnp.logical_and(kj < WIN, kj >= qi)
    return jnp.where(jnp.logical_or(cur, prev), 0.0, NEG_INF), jnp.where(cur, 0.0, NEG_INF)


def _dil_specs(B, S, D, C, offs):
    grid = (B, D // LANES)
    seq = lambda off: pl.BlockSpec((None, S, LANES), lambda b, p: (b, 0, off // LANES + p))
    return grid, [seq(o) for o in offs], seq(0)


def _call_with_rider(rider, body, *, name, grid, in_specs, out_specs, out_shape, scratch_shapes, args, semantics=None):
    if rider is None:
        return _pcall(body, name=name, grid=grid, in_specs=in_specs, out_specs=out_specs, out_shape=out_shape,
                      scratch_shapes=scratch_shapes, compiler_params=_params(*(semantics or ("parallel",) * len(grid))))(*args)
    n_in, n_out, n_sc = len(in_specs), len(out_specs), len(scratch_shapes)
    r_in, r_out = len(rider["ins"]), len(rider["out_shapes"])

    def wrapped(*refs):
        p = 0
        own_in = refs[p:p + n_in]; p += n_in
        rid_in = refs[p:p + r_in]; p += r_in
        own_out = refs[p:p + n_out]; p += n_out
        rid_out = refs[p:p + r_out]; p += r_out
        own_sc = refs[p:p + n_sc]; p += n_sc
        send_sems, recv_sems = refs[p:p + 2]
        ids = [pl.program_id(a) for a in range(len(grid))]
        first = ids[0] == 0
        last = ids[0] == grid[0] - 1
        for a in range(1, len(grid)):
            first = jnp.logical_and(first, ids[a] == 0)
            last = jnp.logical_and(last, ids[a] == grid[a] - 1)

        @pl.when(first)
        def _():
            rider["start"](rid_in, rid_out, send_sems, recv_sems)

        body(*own_in, *own_out, *own_sc)

        @pl.when(last)
        def _():
            rider["finish"](rid_in, rid_out, send_sems, recv_sems)

    aliases = {n_in + t: n_out + t for t in range(r_in)} if rider.get("in_place") else {}
    res = _pcall_comm(
        wrapped, name=name + "_" + rider["name"], grid=grid, in_specs=list(in_specs) + [ANY] * r_in,
        out_specs=list(out_specs) + [ANY] * r_out, out_shape=list(out_shape) + list(rider["out_shapes"]),
        scratch_shapes=list(scratch_shapes) + [pltpu.SemaphoreType.DMA((rider["n"],)), pltpu.SemaphoreType.DMA((rider["n"],))],
        input_output_aliases=aliases, compiler_params=_params(*(("arbitrary",) * len(grid))),
    )(*args, *rider["ins"])
    return res


def _dil_fwd(proj3, *, D, q_off, k_off, v_off, rider=None):
    B, S, C = proj3.shape
    n_it = S // WIN
    scale = HEAD_DIM ** -0.5
    grid, in_specs, out_spec = _dil_specs(B, S, D, C, (q_off, k_off, v_off))

    def body(q_ref, k_ref, v_ref, o_ref, l_ref):
        lo = lax.broadcasted_iota(jnp.int32, (WIN, LANES), 1) < HEAD_DIM
        for c, d in enumerate(DILS):
            two = S // d > WIN
            bias_all, bias_first = _dil_bias(two)

            def step(it, _, c=c, d=d, two=two, bias_all=bias_all, bias_first=bias_first):
                cur, prev, later = _dil_rows(it, d, S)
                q2 = _pair_stack(q_ref[cur, :] * scale, lo)
                if two:
                    k2 = jnp.concatenate([k_ref[prev, :], k_ref[cur, :]], axis=0).astype(BF16)
                    v2 = jnp.concatenate([v_ref[prev, :], v_ref[cur, :]], axis=0).astype(BF16)
                    bias = jnp.where(later, bias_all, bias_first)
                else:
                    k2, v2, bias = k_ref[cur, :].astype(BF16), v_ref[cur, :].astype(BF16), bias_all
                s2 = _dot_nt(q2, k2) + bias
                m2 = jnp.max(s2, axis=1, keepdims=True)
                p2 = jnp.exp(s2 - m2)
                den = jnp.sum(p2, axis=1, keepdims=True)
                oc = _pair_join(_dot_nn(p2.astype(BF16), v2) / den, lo)
                lc = _pair_bcast(m2 + jnp.log(den), lo)
                if c == 0:
                    o_ref[cur, :] = oc
                    l_ref[cur, :] = lc
                else:
                    l_old = l_ref[cur, :]
                    mx = jnp.maximum(l_old, lc)
                    e_old, e_new = jnp.exp(l_old - mx), jnp.exp(lc - mx)
                    tot = e_old + e_new
                    o_ref[cur, :] = (e_old * o_ref[cur, :] + e_new * oc) / tot
                    l_ref[cur, :] = mx + jnp.log(tot)
                return 0

            lax.fori_loop(0, n_it, step, 0, unroll=16)

    return _call_with_rider(
        rider, body, name="dil_fwd", grid=grid, in_specs=in_specs, out_specs=[out_spec, out_spec],
        out_shape=[jax.ShapeDtypeStruct((B, S, D), F32)] * 2, scratch_shapes=[], args=[proj3, proj3, proj3])


def _dil_bwd(proj3, o3, l3, do3, *, D, q_off, k_off, v_off, rider=None):
    B, S, C = proj3.shape
    n_it = S // WIN
    scale = HEAD_DIM ** -0.5
    grid, in_specs, out_spec = _dil_specs(B, S, D, C, (q_off, k_off, v_off))

    def body(q_ref, k_ref, v_ref, o_ref, l_ref, do_ref, dq_ref, dk_ref, dv_ref, dd_s, dq_s, dk_s, dv_s):
        lo = lax.broadcasted_iota(jnp.int32, (WIN, LANES), 1) < HEAD_DIM
        lo_s = lax.broadcasted_iota(jnp.int32, (S, LANES), 1) < HEAD_DIM
        prod = do_ref[...] * o_ref[...]
        d_lo = jnp.sum(jnp.where(lo_s, prod, 0.0), axis=1, keepdims=True)
        d_hi = jnp.sum(jnp.where(lo_s, 0.0, prod), axis=1, keepdims=True)
        dd_s[...] = jnp.where(lo_s, jnp.broadcast_to(d_lo, (S, LANES)), jnp.broadcast_to(d_hi, (S, LANES)))
        dq_s[...] = jnp.zeros_like(dq_s)
        dk_s[...] = jnp.zeros_like(dk_s)
        dv_s[...] = jnp.zeros_like(dv_s)
        for d in DILS:
            two = S // d > WIN
            bias_all, bias_first = _dil_bias(two)

            def step(it, _, d=d, two=two, bias_all=bias_all, bias_first=bias_first):
                cur, prev, later = _dil_rows(it, d, S)
                q2 = _pair_stack(q_ref[cur, :] * scale, lo)
                do2 = _pair_stack(do_ref[cur, :], lo)
                l2 = _pair_col(l_ref[cur, :])
                dd2 = _pair_col(dd_s[cur, :])
                if two:
                    k2 = jnp.concatenate([k_ref[prev, :], k_ref[cur, :]], axis=0).astype(BF16)
                    v2 = jnp.concatenate([v_ref[prev, :], v_ref[cur, :]], axis=0).astype(BF16)
                    bias = jnp.where(later, bias_all, bias_first)
                else:
                    k2, v2, bias = k_ref[cur, :].astype(BF16), v_ref[cur, :].astype(BF16), bias_all
                p2 = jnp.exp(_dot_nt(q2, k2) + bias - l2)
                ds2 = (p2 * (_dot_nt(do2, v2) - dd2)).astype(BF16)
                dq_s[cur, :] += _pair_join(_dot_nn(ds2, k2), lo) * scale
                dk2 = _dot_tn(ds2, q2)
                dv2 = _dot_tn(p2.astype(BF16), do2)
                if two:
                    dk_s[prev, :] += dk2[:WIN]
                    dv_s[prev, :] += dv2[:WIN]
                    dk_s[cur, :] += dk2[WIN:]
                    dv_s[cur, :] += dv2[WIN:]
                else:
                    dk_s[cur, :] += dk2
                    dv_s[cur, :] += dv2
                return 0

            lax.fori_loop(0, n_it, step, 0, unroll=16)
        dq_ref[...] = dq_s[...].astype(BF16)
        dk_ref[...] = dk_s[...].astype(BF16)
        dv_ref[...] = dv_s[...].astype(BF16)

    return _call_with_rider(
        rider, body, name="dil_bwd", grid=grid, in_specs=in_specs + [out_spec] * 3, out_specs=[out_spec] * 3,
        out_shape=[jax.ShapeDtypeStruct((B, S, D), BF16)] * 3, scratch_shapes=[pltpu.VMEM((S, LANES), F32)] * 4,
        args=[proj3, proj3, proj3, o3, l3, do3])


SWA_HB = 2 * SWA_GROUP


def _to_half(x, src, dst, lo):
    if src != dst:
        x = pltpu.roll(x, HEAD_DIM, 1)
    return jnp.where(lo if dst == 0 else jnp.logical_not(lo), x, 0.0)


def _swa_kv(g):
    return 2 * g // SWA_GROUP


SWA_STACKS = ((0, 1), (2, 3))


def _swa_stack(ref, gs, lo, dtype, rows=slice(None)):
    parts = []
    for g in gs:
        x = ref[rows, g * LANES:(g + 1) * LANES]
        parts += [_to_half(x, 0, _swa_kv(g), lo), _to_half(x, 1, _swa_kv(g), lo)]
    return jnp.concatenate(parts, axis=0).astype(dtype)


def _swa_unstack(y, gs, lo):
    out = []
    for t, g in enumerate(gs):
        even, odd = y[2 * t * WIN:(2 * t + 1) * WIN], y[(2 * t + 1) * WIN:(2 * t + 2) * WIN]
        out.append(_to_half(even, _swa_kv(g), 0, lo) + _to_half(odd, _swa_kv(g), 1, lo))
    return out


def _swa_cols(x, gs):
    cols = []
    for g in gs:
        cols += [jnp.broadcast_to(x[:, 2 * g:2 * g + 1], (WIN, 1)), jnp.broadcast_to(x[:, 2 * g + 1:2 * g + 2], (WIN, 1))]
    return jnp.concatenate(cols, axis=0)


SWA_UNROLL = 8


def _swa_seq_specs(B, S, D, q_off, k_off, v_off):
    qw = SWA_HB * HEAD_DIM
    assert q_off % qw == 0 and k_off % LANES == 0 and v_off % LANES == 0 and D % qw == 0
    seq = lambda width, off: pl.BlockSpec((None, S, width), lambda b, hh: (b, 0, off // width + hh))
    sink = pl.BlockSpec((None, 1, LANES), lambda b, hh: (hh, 0, 0))
    return (B, D // qw), seq, sink, qw


def _swa_rows(it):
    cur = pl.ds(pl.multiple_of(it * WIN, WIN), WIN)
    prev = pl.ds(pl.multiple_of(jnp.maximum(it - 1, 0) * WIN, WIN), WIN)
    return cur, prev, it > 0


def _swa_seq_fwd(proj3, sinks, *, D, q_off, k_off, v_off, rider=None):
    B, S, C = proj3.shape
    scale = HEAD_DIM ** -0.5
    grid, seq, sink, qw = _swa_seq_specs(B, S, D, q_off, k_off, v_off)
    nhb = D // qw

    def body(q_ref, k_ref, v_ref, sk_ref, o_ref, lse_ref):
        lo = lax.broadcasted_iota(jnp.int32, (WIN, LANES), 1) < HEAD_DIM
        lane = lax.broadcasted_iota(jnp.int32, (WIN, LANES), 1)
        sk = sk_ref[...]
        biases = [_dil_bias(True, 2 * len(gs)) for gs in SWA_STACKS]

        def step(it, _):
            cur, prev, later = _swa_rows(it)
            k2 = jnp.concatenate([k_ref[prev, :], k_ref[cur, :]], axis=0).astype(BF16)
            v2 = jnp.concatenate([v_ref[prev, :], v_ref[cur, :]], axis=0).astype(BF16)
            lse_acc = jnp.zeros((WIN, LANES), F32)
            for gs, (bias_all, bias_first) in zip(SWA_STACKS, biases):
                bias = jnp.where(later, bias_all, bias_first)
                qs = _swa_stack(q_ref, gs, lo, BF16, cur)
                sks = _swa_cols(sk, gs)
                s = _dot_nt(qs, k2) * scale + bias
                m = jnp.maximum(jnp.max(s, axis=1, keepdims=True), sks)
                p = jnp.exp(s - m)
                den = jnp.sum(p, axis=1, keepdims=True) + jnp.exp(sks - m)
                for g, grp in zip(gs, _swa_unstack(_dot_nn(p.astype(BF16), v2) / den, gs, lo)):
                    o_ref[cur, g * LANES:(g + 1) * LANES] = grp
                ls = m + jnp.log(den)
                for t, g in enumerate(gs):
                    lse_acc = jnp.where(lane == 2 * g, ls[2 * t * WIN:(2 * t + 1) * WIN], lse_acc)
                    lse_acc = jnp.where(lane == 2 * g + 1, ls[(2 * t + 1) * WIN:(2 * t + 2) * WIN], lse_acc)
            lse_ref[cur, :] = lse_acc
            return 0

        lax.fori_loop(0, S // WIN, step, 0, unroll=SWA_UNROLL)

    return _call_with_rider(
        rider, body, name="swa_fwd", grid=grid,
        in_specs=[seq(qw, q_off), seq(LANES, k_off), seq(LANES, v_off), sink],
        out_specs=[seq(qw, 0), seq(LANES, 0)],
        out_shape=[jax.ShapeDtypeStruct((B, S, D), F32), jax.ShapeDtypeStruct((B, S, nhb * LANES), F32)],
        scratch_shapes=[], args=[proj3, proj3, proj3, sinks])


def _swa_seq_bwd(proj3, o3, lse3, do3, sinks, *, D, q_off, k_off, v_off):
    B, S, C = proj3.shape
    scale = HEAD_DIM ** -0.5
    grid, seq, sink, qw = _swa_seq_specs(B, S, D, q_off, k_off, v_off)
    nhb = D // qw
    KV = D // SWA_GROUP

    def body(q_ref, k_ref, v_ref, o_ref, l_ref, do_ref, sk_ref, dq_ref, dk_ref, dv_ref, dsk_ref, dk_s, dv_s):
        lo = lax.broadcasted_iota(jnp.int32, (WIN, LANES), 1) < HEAD_DIM
        lane = lax.broadcasted_iota(jnp.int32, (1, LANES), 1)
        sk = sk_ref[...]
        biases = [_dil_bias(True, 2 * len(gs)) for gs in SWA_STACKS]
        dk_s[...] = jnp.zeros_like(dk_s)
        dv_s[...] = jnp.zeros_like(dv_s)

        dsk_ref[...] = jnp.zeros_like(dsk_ref)

        def step(it, _):
            cur, prev, later = _swa_rows(it)
            k2 = jnp.concatenate([k_ref[prev, :], k_ref[cur, :]], axis=0).astype(BF16)
            v2 = jnp.concatenate([v_ref[prev, :], v_ref[cur, :]], axis=0).astype(BF16)
            lse = l_ref[cur, :]
            dk2 = jnp.zeros((2 * WIN, LANES), F32)
            dv2 = jnp.zeros((2 * WIN, LANES), F32)
            dsk_acc = jnp.zeros((1, LANES), F32)
            for gs, (bias_all, bias_first) in zip(SWA_STACKS, biases):
                bias = jnp.where(later, bias_all, bias_first)
                qs = _swa_stack(q_ref, gs, lo, BF16, cur)
                dos = _swa_stack(do_ref, gs, lo, BF16, cur)
                dds = []
                for g in gs:
                    prod = do_ref[cur, g * LANES:(g + 1) * LANES] * o_ref[cur, g * LANES:(g + 1) * LANES]
                    dds += [jnp.sum(jnp.where(lo, prod, 0.0), axis=1, keepdims=True),
                            jnp.sum(jnp.where(lo, 0.0, prod), axis=1, keepdims=True)]
                dds = jnp.concatenate(dds, axis=0)
                ls = _swa_cols(lse, gs)
                ps = jnp.exp(_dot_nt(qs, k2) * scale + bias - ls)
                dss = (ps * (_dot_nt(dos, v2) - dds) * scale).astype(BF16)
                for g, grp in zip(gs, _swa_unstack(_dot_nn(dss, k2), gs, lo)):
                    dq_ref[cur, g * LANES:(g + 1) * LANES] = grp.astype(BF16)
                dk2 = dk2 + _dot_tn(dss, qs)
                dv2 = dv2 + _dot_tn(ps.astype(BF16), dos)
                dsks = jnp.exp(_swa_cols(sk, gs) - ls) * dds
                for t, g in enumerate(gs):
                    for u in range(2):
                        rows = slice((2 * t + u) * WIN, (2 * t + u + 1) * WIN)
                        dsk_acc = dsk_acc + jnp.where(lane == 2 * g + u, -jnp.sum(dsks[rows], axis=0, keepdims=True), 0.0)
            dk_s[prev, :] += dk2[:WIN]
            dv_s[prev, :] += dv2[:WIN]
            dk_s[cur, :] += dk2[WIN:]
            dv_s[cur, :] += dv2[WIN:]
            dsk_ref[...] += dsk_acc
            return 0

        lax.fori_loop(0, S // WIN, step, 0, unroll=SWA_UNROLL)
        dk_ref[...] = dk_s[...].astype(BF16)
        dv_ref[...] = dv_s[...].astype(BF16)

    return _pcall(
        body, name="swa_bwd", grid=grid,
        in_specs=[seq(qw, q_off), seq(LANES, k_off), seq(LANES, v_off), seq(qw, 0), seq(LANES, 0), seq(qw, 0), sink],
        out_specs=[seq(qw, 0), seq(LANES, 0), seq(LANES, 0), pl.BlockSpec((None, None, 1, LANES), lambda b, hh: (b, hh, 0, 0))],
        out_shape=[jax.ShapeDtypeStruct((B, S, D), BF16), jax.ShapeDtypeStruct((B, S, KV), BF16),
                   jax.ShapeDtypeStruct((B, S, KV), BF16), jax.ShapeDtypeStruct((B, nhb, 1, LANES), F32)],
        scratch_shapes=[pltpu.VMEM((S, LANES), F32), pltpu.VMEM((S, LANES), F32)],
        compiler_params=_params("parallel", "parallel"),
    )(proj3, proj3, proj3, o3, lse3, do3, sinks)


def _branch_fwd(ys, wb, proj, *, D, g_off, rider=None):
    T = proj.shape[0]
    tm, tn = _tile(T, 256), _tile(D, 512)
    n = len(ys)

    def body(*refs):
        y_refs, w_ref, g_refs, br_ref, mg_ref = refs[:n], refs[n], refs[n + 1:2 * n + 1], refs[2 * n + 1], refs[2 * n + 2]
        acc = None
        for k in range(n):
            br = _dot_nn(y_refs[k][...].astype(BF16), w_ref[k])
            br_ref[k] = br
            term = _sigmoid(g_refs[k][...]) * br
            acc = term if acc is None else acc + term
        mg_ref[...] = acc.astype(BF16)

    gate = lambda k: pl.BlockSpec((tm, tn), lambda i, j: (i, (g_off + k * D) // tn + j))
    return _call_with_rider(
        rider, body, name="branch_fwd", grid=(T // tm, D // tn),
        in_specs=[pl.BlockSpec((tm, D), lambda i, j: (i, 0))] * n + [pl.BlockSpec((n, D, tn), lambda i, j: (0, 0, j))]
        + [gate(k) for k in range(n)],
        out_specs=[pl.BlockSpec((n, tm, tn), lambda i, j: (0, i, j)), pl.BlockSpec((tm, tn), lambda i, j: (i, j))],
        out_shape=[jax.ShapeDtypeStruct((n, T, D), F32), jax.ShapeDtypeStruct((T, D), BF16)],
        scratch_shapes=[], args=[*ys, wb, *([proj] * n)])


def _branch_bwd(dmix, w_out, branch, proj, *, D, g_off):
    n, T, _ = branch.shape
    tm, tn = _tile(T, 512), _tile(D, 512)

    def body(dy_ref, w_ref, br_ref, *rest):
        g_refs, db_ref, dg_refs = rest[:n], rest[n], rest[n + 1:]
        dm = _dot_nt(dy_ref[...].astype(BF16), w_ref[...])
        for k in range(n):
            sg = _sigmoid(g_refs[k][...])
            db_ref[k] = (sg * dm).astype(BF16)
            dg_refs[k][...] = (dm * br_ref[k] * sg * (1.0 - sg)).astype(BF16)

    gate = lambda k: pl.BlockSpec((tm, tn), lambda i, j: (i, (g_off + k * D) // tn + j))
    blk = pl.BlockSpec((tm, tn), lambda i, j: (i, j))
    res = _pcall(
        body, name="branch_bwd", grid=(T // tm, D // tn),
        in_specs=[pl.BlockSpec((tm, D), lambda i, j: (i, 0)), pl.BlockSpec((tn, D), lambda i, j: (j, 0)),
                  pl.BlockSpec((n, tm, tn), lambda i, j: (0, i, j))] + [gate(k) for k in range(n)],
        out_specs=[pl.BlockSpec((n, tm, tn), lambda i, j: (0, i, j))] + [blk] * n,
        out_shape=[jax.ShapeDtypeStruct((n, T, D), BF16)] + [jax.ShapeDtypeStruct((T, D), BF16)] * n,
        compiler_params=_params("parallel", "parallel"),
    )(dmix, w_out, branch, *([proj] * n))
    return res[0], list(res[1:])


def _ln_bwd(dout, z, g):
    T, D = z.shape
    tm = _tile(T, 512)

    def body(do_ref, z_ref, g_ref, dz_ref, dg_ref, db_ref):
        z = z_ref[...]
        do = do_ref[...]
        mu = jnp.mean(z, axis=1, keepdims=True)
        zc = z - mu
        rstd = lax.rsqrt(jnp.mean(zc * zc, axis=1, keepdims=True) + LN_EPS)
        xhat = zc * rstd
        dxh = do * g_ref[...]
        dz_ref[...] = rstd * (dxh - jnp.mean(dxh, axis=1, keepdims=True) - xhat * jnp.mean(dxh * xhat, axis=1, keepdims=True))
        dg = jnp.sum(do * xhat, axis=0, keepdims=True)
        db = jnp.sum(do, axis=0, keepdims=True)
        first = pl.program_id(0) == 0

        @pl.when(first)
        def _():
            dg_ref[...] = dg
            db_ref[...] = db

        @pl.when(jnp.logical_not(first))
        def _():
            dg_ref[...] += dg
            db_ref[...] += db

    blk = pl.BlockSpec((tm, D), lambda i: (i, 0))
    vec = pl.BlockSpec((1, D), lambda i: (0, 0))
    return _pcall(
        body, name="ln_bwd", grid=(T // tm,), in_specs=[blk, blk, vec], out_specs=[blk, vec, vec],
        out_shape=[jax.ShapeDtypeStruct((T, D), F32), jax.ShapeDtypeStruct((1, D), F32), jax.ShapeDtypeStruct((1, D), F32)],
        compiler_params=_params("arbitrary"),
    )(dout, z, g)


def _ffn_in_fwd(x1, w_sh):
    T, D = x1.shape
    ns, _, nsh = w_sh.shape
    half = ns // 2
    Fh = half * nsh
    tm = _tile(T, 512)

    def body(x_ref, wa_ref, wb_ref, h1_ref, h3_ref, f_ref):
        xb = x_ref[...].astype(BF16)
        h1 = _dot_nn(xb, wa_ref[...])
        h3 = _dot_nn(xb, wb_ref[...])
        h1_ref[...] = h1
        h3_ref[...] = h3
        f_ref[...] = (h1 * _sigmoid(h1) * h3).astype(BF16)

    cols = pl.BlockSpec((tm, nsh), lambda j, i: (i, j))
    return _pcall(
        body, name="ffn_in_fwd", grid=(half, T // tm),
        in_specs=[pl.BlockSpec((tm, D), lambda j, i: (i, 0)), pl.BlockSpec((None, D, nsh), lambda j, i: (j, 0, 0)),
                  pl.BlockSpec((None, D, nsh), lambda j, i: (j + half, 0, 0))],
        out_specs=[cols, cols, cols],
        out_shape=[jax.ShapeDtypeStruct((T, Fh), F32), jax.ShapeDtypeStruct((T, Fh), F32), jax.ShapeDtypeStruct((T, Fh), BF16)],
        compiler_params=_params("parallel", "parallel"),
    )(x1, w_sh, w_sh)


def _ffn_out_bwd(dy, w_ffn_out, h1, h3):
    T, Fh = h1.shape
    D = w_ffn_out.shape[1]
    tm = _tile(T, 256)

    def body(dy_ref, w_ref, h1_ref, h3_ref, o_ref):
        d = _dot_nt(dy_ref[...].astype(BF16), w_ref[...])
        h1v = h1_ref[...]
        sg = _sigmoid(h1v)
        o_ref[:, :Fh] = (d * h3_ref[...] * sg * (1.0 + h1v * (1.0 - sg))).astype(BF16)
        o_ref[:, Fh:] = (d * h1v * sg).astype(BF16)

    blk = pl.BlockSpec((tm, Fh), lambda i: (i, 0))
    return _pcall(
        body, name="ffn_out_bwd", grid=(T // tm,),
        in_specs=[pl.BlockSpec((tm, D), lambda i: (i, 0)), pl.BlockSpec((Fh, D), lambda i: (0, 0)), blk, blk],
        out_specs=pl.BlockSpec((tm, 2 * Fh), lambda i: (i, 0)),
        out_shape=jax.ShapeDtypeStruct((T, 2 * Fh), BF16), compiler_params=_params("parallel"),
    )(dy, w_ffn_out, h1, h3)


def _loss_head(y, target):
    T, D = y.shape
    tm = _tile(T, 512)

    def body(y_ref, t_ref, dy_ref, l_ref):
        e = y_ref[...] - t_ref[...]
        dy_ref[...] = e * (1.0 / D)
        sq = e * e
        part = sq[:, 0:LANES]
        for c in range(1, D // LANES):
            part = part + sq[:, c * LANES:(c + 1) * LANES]
        part = jnp.sum(part, axis=0, keepdims=True) * (0.5 / D)
        first = pl.program_id(0) == 0

        @pl.when(first)
        def _():
            l_ref[...] = part

        @pl.when(jnp.logical_not(first))
        def _():
            l_ref[...] += part

    blk = pl.BlockSpec((tm, D), lambda i: (i, 0))
    return _pcall(
        body, name="loss_head", grid=(T // tm,), in_specs=[blk, blk],
        out_specs=[blk, pl.BlockSpec((1, LANES), lambda i: (0, 0))],
        out_shape=[jax.ShapeDtypeStruct((T, D), F32), jax.ShapeDtypeStruct((1, LANES), F32)],
        compiler_params=_params("arbitrary"),
    )(y, target)


def _as_rows(a):
    return a.reshape(-1, a.shape[-1])


def _adamw(w, g, m, v, rider=None):
    w2, g2, m2, v2 = (_as_rows(t) for t in (w, g, m, v))
    R, Cc = w2.shape
    cap = max(SUBLANES, min(512, (256 * 1024) // Cc))
    tm = R if (R <= cap or R % SUBLANES) else max(t for t in range(SUBLANES, cap + 1, SUBLANES) if R % t == 0)
    c1 = 1.0 - ADAM_B1 ** ADAM_STEP
    c2 = 1.0 - ADAM_B2 ** ADAM_STEP

    def body(w_ref, g_ref, m_ref, v_ref, d_ref, nm_ref, nv_ref):
        gg = g_ref[...]
        nm = ADAM_B1 * m_ref[...] + (1.0 - ADAM_B1) * gg
        nv = ADAM_B2 * v_ref[...] + (1.0 - ADAM_B2) * (gg * gg)
        d_ref[...] = (-ADAM_LR) * ((nm / c1) / (jnp.sqrt(nv / c2) + ADAM_EPS) + ADAM_WD * w_ref[...])
        nm_ref[...] = nm
        nv_ref[...] = nv

    blk = pl.BlockSpec((tm, Cc), lambda i: (i, 0))
    res = _call_with_rider(
        rider, body, name="adamw", grid=(R // tm,), in_specs=[blk] * 4, out_specs=[blk] * 3,
        out_shape=[jax.ShapeDtypeStruct((R, Cc), F32)] * 3, scratch_shapes=[], args=[w2, g2, m2, v2])
    return tuple(t.reshape(w.shape) for t in res[:3]) + tuple(res[3:])


def _where_am_i():
    x, y, c = lax.axis_index("x"), lax.axis_index("y"), lax.axis_index("c")
    chips = [(1 - x, y), (x, 1 - y), (1 - x, 1 - y)]
    return x, y, c, chips


def _remote(src, dst, send_sems, recv_sems, k, to):
    return pltpu.make_async_remote_copy(src_ref=src, dst_ref=dst, send_sem=send_sems.at[k], recv_sem=recv_sems.at[k],
                                        device_id=to, device_id_type=MESH)


def _comm_call(body, name, ins, out_shapes, n_remote, n_local):
    return _pcall_comm(
        body, name=name, in_specs=[ANY] * len(ins), out_specs=[ANY] * len(out_shapes), out_shape=out_shapes,
        scratch_shapes=[pltpu.SemaphoreType.DMA((n_remote,)), pltpu.SemaphoreType.DMA((n_remote,)),
                        pltpu.SemaphoreType.DMA((max(n_local, 1),))],
    )(*ins)


def _gather_weights(shards):
    n = len(shards)

    def body(*refs):
        ins, outs = refs[:n], refs[n:2 * n]
        send_sems, recv_sems, local_sems = refs[2 * n:]
        x, y, c, chips = _where_am_i()
        s = 2 * x + y
        sib = (x, y, 1 - c)
        first = []
        for t in range(n):
            for j, (cx, cy) in enumerate(chips):
                first.append(_remote(ins[t].at[:, c], outs[t].at[:, s, c], send_sems, recv_sems, 6 * t + j, (cx, cy, c)))
        for cp in first:
            cp.start()
        passed = []
        for j, (cx, cy) in enumerate(chips):
            sj = 2 * cx + cy
            for t in range(n):
                land = outs[t].at[:, sj, c]
                _remote(land, land, send_sems, recv_sems, 6 * t + j, (cx, cy, c)).wait_recv()
                fw = _remote(land, land, send_sems, recv_sems, 6 * t + 3 + j, sib)
                fw.start()
                passed.append(fw)
        for j, (cx, cy) in enumerate(chips):
            sj = 2 * cx + cy
            for t in range(n):
                land = outs[t].at[:, sj, 1 - c]
                _remote(land, land, send_sems, recv_sems, 6 * t + 3 + j, sib).wait_recv()
        for cp in first + passed:
            cp.wait_send()

    out_shapes = [jax.ShapeDtypeStruct((t.shape[0], N_CHIPS) + t.shape[1:], t.dtype) for t in shards]
    got = _comm_call(body, "gather_weights", shards, out_shapes, 6 * n, 0)
    s = 2 * lax.axis_index("x") + lax.axis_index("y")
    return [lax.dynamic_update_slice(g, t[:, None], (0, s, 0, 0, 0)) for g, t in zip(got, shards)]


def _gather_rider(shards):
    n = len(shards)

    def copies(ins, outs, send_sems, recv_sems):
        x, y, c, chips = _where_am_i()
        s = 2 * x + y
        return [_remote(ins[t].at[:, c], outs[t].at[:, s, c], send_sems, recv_sems, 3 * t + j, (cx, cy, c))
                for t in range(n) for j, (cx, cy) in enumerate(chips)]

    def start(ins, outs, send_sems, recv_sems):
        for cp in copies(ins, outs, send_sems, recv_sems):
            cp.start()

    def finish(ins, outs, send_sems, recv_sems):
        x, y, c, chips = _where_am_i()
        for t in range(n):
            for j, (cx, cy) in enumerate(chips):
                land = outs[t].at[:, 2 * cx + cy, c]
                _remote(land, land, send_sems, recv_sems, 3 * t + j, (cx, cy, c)).wait_recv()
        for cp in copies(ins, outs, send_sems, recv_sems):
            cp.wait_send()

    out_shapes = [jax.ShapeDtypeStruct((t.shape[0], N_CHIPS) + t.shape[1:], t.dtype) for t in shards]
    return dict(name="gather", ins=list(shards), out_shapes=out_shapes, n=3 * n, start=start, finish=finish)


def _gather_forward(landed, shards):
    n = len(landed)

    def body(*refs):
        outs = refs[n:2 * n]
        send_sems, recv_sems, _ = refs[2 * n:]
        x, y, c, chips = _where_am_i()
        sib = (x, y, 1 - c)
        cps = []
        for t in range(n):
            for j, (cx, cy) in enumerate(chips):
                land = outs[t].at[:, 2 * cx + cy, c]
                cps.append(_remote(land, land, send_sems, recv_sems, 3 * t + j, sib))
        for cp in cps:
            cp.start()
        for t in range(n):
            for j, (cx, cy) in enumerate(chips):
                land = outs[t].at[:, 2 * cx + cy, 1 - c]
                _remote(land, land, send_sems, recv_sems, 3 * t + j, sib).wait_recv()
        for cp in cps:
            cp.wait_send()

    got = _pcall_comm(
        body, name="gather_forward", in_specs=[ANY] * n, out_specs=[ANY] * n,
        out_shape=[jax.ShapeDtypeStruct(t.shape, t.dtype) for t in landed], input_output_aliases={t: t for t in range(n)},
        scratch_shapes=[pltpu.SemaphoreType.DMA((3 * n,)), pltpu.SemaphoreType.DMA((3 * n,)), pltpu.SemaphoreType.DMA((1,))],
    )(*landed)
    s = 2 * lax.axis_index("x") + lax.axis_index("y")
    return [lax.dynamic_update_slice(g, t[:, None], (0, s, 0, 0, 0)) for g, t in zip(got, shards)]


def _forward_rider(landed):
    n = len(landed)

    def copies(outs, send_sems, recv_sems):
        x, y, c, chips = _where_am_i()
        cps = []
        for t in range(n):
            for j, (cx, cy) in enumerate(chips):
                land = outs[t].at[:, 2 * cx + cy, c]
                cps.append(_remote(land, land, send_sems, recv_sems, 3 * t + j, (x, y, 1 - c)))
        return cps

    def start(ins, outs, send_sems, recv_sems):
        for cp in copies(outs, send_sems, recv_sems):
            cp.start()

    def finish(ins, outs, send_sems, recv_sems):
        x, y, c, chips = _where_am_i()
        for t in range(n):
            for j, (cx, cy) in enumerate(chips):
                land = outs[t].at[:, 2 * cx + cy, 1 - c]
                _remote(land, land, send_sems, recv_sems, 3 * t + j, (x, y, 1 - c)).wait_recv()
        for cp in copies(outs, send_sems, recv_sems):
            cp.wait_send()

    out_shapes = [jax.ShapeDtypeStruct(t.shape, t.dtype) for t in landed]
    return dict(name="forward", ins=list(landed), out_shapes=out_shapes, n=3 * n, start=start, finish=finish, in_place=True)


def _place_own_shard(got, shards):
    s = 2 * lax.axis_index("x") + lax.axis_index("y")
    return [lax.dynamic_update_slice(g, t[:, None], (0, s, 0, 0, 0)) for g, t in zip(got, shards)]


def _gather_small(v):
    def body(v_ref, out_ref, send_sems, recv_sems, local_sems):
        x, y, c, chips = _where_am_i()
        s = 2 * x + y
        mine = pltpu.make_async_copy(v_ref, out_ref.at[s], local_sems.at[0])
        mine.start()
        sends = [_remote(v_ref, out_ref.at[s], send_sems, recv_sems, j, (cx, cy, c)) for j, (cx, cy) in enumerate(chips)]
        for cp in sends:
            cp.start()
        for j, (cx, cy) in enumerate(chips):
            land = out_ref.at[2 * cx + cy]
            _remote(land, land, send_sems, recv_sems, j, (cx, cy, c)).wait_recv()
        for cp in sends:
            cp.wait_send()
        mine.wait()

    return _comm_call(body, "gather_small", [v], [jax.ShapeDtypeStruct((N_CHIPS,) + v.shape, v.dtype)], 3, 1)[0]


def _swap_sibling_halves(grads):
    n = len(grads)

    def body(*refs):
        ins, outs = refs[:n], refs[n:2 * n]
        send_sems, recv_sems, _ = refs[2 * n:]
        x, y, c, _chips = _where_am_i()
        sib = (x, y, 1 - c)
        cps = [_remote(ins[t].at[:, :, 1 - c], outs[t], send_sems, recv_sems, t, sib) for t in range(n)]
        for cp in cps:
            cp.start()
        for cp in cps:
            cp.wait()

    out_shapes = [jax.ShapeDtypeStruct(g.shape[:2] + g.shape[3:], g.dtype) for g in grads]
    return _comm_call(body, "grad_swap_halves", grads, out_shapes, n, 0)


def _exchange_chips(parts):
    n = len(parts)

    def body(*refs):
        ins, outs = refs[:n], refs[n:2 * n]
        send_sems, recv_sems, _ = refs[2 * n:]
        x, y, c, chips = _where_am_i()
        cps = []
        for t in range(n):
            for j, (cx, cy) in enumerate(chips):
                cps.append(_remote(ins[t].at[:, 2 * cx + cy], outs[t].at[j], send_sems, recv_sems, 3 * t + j, (cx, cy, c)))
        for cp in cps:
            cp.start()
        for cp in cps:
            cp.wait()

    out_shapes = [jax.ShapeDtypeStruct((3, p.shape[0]) + p.shape[2:], p.dtype) for p in parts]
    return _comm_call(body, "grad_exchange_chips", parts, out_shapes, 3 * n, 0)


def _exchange_rider(parts):
    n = len(parts)

    def copies(ins, outs, send_sems, recv_sems):
        x, y, c, chips = _where_am_i()
        return [_remote(ins[t].at[:, 2 * cx + cy], outs[t].at[j], send_sems, recv_sems, 3 * t + j, (cx, cy, c))
                for t in range(n) for j, (cx, cy) in enumerate(chips)]

    def start(ins, outs, send_sems, recv_sems):
        for cp in copies(ins, outs, send_sems, recv_sems):
            cp.start()

    def finish(ins, outs, send_sems, recv_sems):
        for cp in copies(ins, outs, send_sems, recv_sems):
            cp.wait()

    out_shapes = [jax.ShapeDtypeStruct((3, p.shape[0]) + p.shape[2:], p.dtype) for p in parts]
    return dict(name="exchange", ins=list(parts), out_shapes=out_shapes, n=3 * n, start=start, finish=finish)


def _join_sibling_halves(halves):
    n = len(halves)

    def body(*refs):
        ins, outs = refs[:n], refs[n:2 * n]
        send_sems, recv_sems, local_sems = refs[2 * n:]
        x, y, c, _chips = _where_am_i()
        sib = (x, y, 1 - c)
        cps = [_remote(ins[t], outs[t].at[:, c], send_sems, recv_sems, t, sib) for t in range(n)]
        for cp in cps:
            cp.start()
        for t in range(n):
            land = outs[t].at[:, 1 - c]
            _remote(land, land, send_sems, recv_sems, t, sib).wait_recv()
        for cp in cps:
            cp.wait_send()

    out_shapes = [jax.ShapeDtypeStruct((h.shape[0], 2) + h.shape[1:], h.dtype) for h in halves]
    got = _comm_call(body, "grad_join_halves", halves, out_shapes, n, 0)
    c = lax.axis_index("c")
    return [lax.dynamic_update_slice(g, h[:, None], (0, c, 0, 0)) for g, h in zip(got, halves)]


def _join_rider(halves):
    n = len(halves)

    def copies(ins, outs, send_sems, recv_sems):
        x, y, c, _chips = _where_am_i()
        return [_remote(ins[t], outs[t].at[:, c], send_sems, recv_sems, t, (x, y, 1 - c)) for t in range(n)]

    def start(ins, outs, send_sems, recv_sems):
        for cp in copies(ins, outs, send_sems, recv_sems):
            cp.start()

    def finish(ins, outs, send_sems, recv_sems):
        x, y, c, _chips = _where_am_i()
        for t in range(n):
            land = outs[t].at[:, 1 - c]
            _remote(land, land, send_sems, recv_sems, t, (x, y, 1 - c)).wait_recv()
        for cp in copies(ins, outs, send_sems, recv_sems):
            cp.wait_send()

    out_shapes = [jax.ShapeDtypeStruct((h.shape[0], 2) + h.shape[1:], h.dtype) for h in halves]
    return dict(name="join", ins=list(halves), out_shapes=out_shapes, n=n, start=start, finish=finish)


def _place_own_half(got, halves):
    c = lax.axis_index("c")
    return [lax.dynamic_update_slice(g, h[:, None], (0, c, 0, 0)) for g, h in zip(got, halves)]


def _small_exchange_rider(v):
    def copies(ins, outs, send_sems, recv_sems):
        x, y, c, chips = _where_am_i()
        return [_remote(ins[0], outs[0].at[j], send_sems, recv_sems, j, (cx, cy, c)) for j, (cx, cy) in enumerate(chips)]

    def start(ins, outs, send_sems, recv_sems):
        for cp in copies(ins, outs, send_sems, recv_sems):
            cp.start()

    def finish(ins, outs, send_sems, recv_sems):
        for cp in copies(ins, outs, send_sems, recv_sems):
            cp.wait()

    return dict(name="small_exchange", ins=[v], out_shapes=[jax.ShapeDtypeStruct((3,) + v.shape, v.dtype)], n=3,
                start=start, finish=finish)


def _swap_small(v):
    def body(v_ref, out_ref, send_sems, recv_sems, _):
        x, y, c, _chips = _where_am_i()
        cp = _remote(v_ref, out_ref, send_sems, recv_sems, 0, (x, y, 1 - c))
        cp.start()
        cp.wait()

    return _comm_call(body, "small_swap", [v], [jax.ShapeDtypeStruct(v.shape, v.dtype)], 1, 0)[0]


def _sum_rows(name, terms, out_dtypes):
    R, Cc = terms[0].shape
    tm = R if R <= 256 else max(t for t in range(16, 257, 16) if R % t == 0)
    n = len(terms)

    def body(*refs):
        acc = refs[0][...].astype(F32)
        for r in refs[1:n]:
            acc = acc + r[...].astype(F32)
        for o in refs[n:]:
            o[...] = acc.astype(o.dtype)

    blk = pl.BlockSpec((tm, Cc), lambda i: (i, 0))
    return _pcall(
        body, name=name, grid=(R // tm,), in_specs=[blk] * n, out_specs=[blk] * len(out_dtypes),
        out_shape=[jax.ShapeDtypeStruct((R, Cc), d) for d in out_dtypes], compiler_params=_params("parallel"),
    )(*terms)


def _pair_sum(g5, r1, core, shard):
    A4, _, Rh, Cc = g5.shape
    A = A4 // N_CHIPS
    tr = Rh if Rh <= 256 else max(t for t in range(16, 257, 16) if Rh % t == 0)

    def body(core_ref, shard_ref, g_ref, r_ref, qb_ref, qf_ref):
        q = g_ref[...] + r_ref[...]
        qb_ref[...] = q.astype(BF16)

        @pl.when(pl.program_id(2) == shard_ref[0])
        def _():
            qf_ref[...] = q

    grid_spec = pltpu.PrefetchScalarGridSpec(
        num_scalar_prefetch=2, grid=(A, Rh // tr, N_CHIPS),
        in_specs=[pl.BlockSpec((None, None, tr, Cc), lambda a, r, sh, core, shard: (a * N_CHIPS + sh, core[0], r, 0)),
                  pl.BlockSpec((None, tr, Cc), lambda a, r, sh, core, shard: (a * N_CHIPS + sh, r, 0))],
        out_specs=[pl.BlockSpec((None, tr, Cc), lambda a, r, sh, core, shard: (a * N_CHIPS + sh, r, 0)),
                   pl.BlockSpec((None, tr, Cc), lambda a, r, sh, core, shard: (a, r, 0))],
    )
    return _pcall(
        body, name="grad_pair_sum", grid_spec=grid_spec,
        out_shape=[jax.ShapeDtypeStruct((A4, Rh, Cc), BF16), jax.ShapeDtypeStruct((A, Rh, Cc), F32)],
        compiler_params=_params("parallel", "parallel", "arbitrary"),
    )(core, shard, g5, r1)


def _swap_rider(grads):
    n = len(grads)

    def copies(ins, outs, send_sems, recv_sems):
        x, y, c, _chips = _where_am_i()
        return [_remote(ins[t].at[:, :, 1 - c], outs[t], send_sems, recv_sems, t, (x, y, 1 - c)) for t in range(n)]

    def start(ins, outs, send_sems, recv_sems):
        for cp in copies(ins, outs, send_sems, recv_sems):
            cp.start()

    def finish(ins, outs, send_sems, recv_sems):
        for cp in copies(ins, outs, send_sems, recv_sems):
            cp.wait()

    out_shapes = [jax.ShapeDtypeStruct(g.shape[:2] + g.shape[3:], g.dtype) for g in grads]
    return dict(name="swap", ins=list(grads), out_shapes=out_shapes, n=n, start=start, finish=finish)


def _reduce_chip(grads, r1, core, shard):
    qb, qf = [], []
    for g, r in zip(grads, r1):
        A, _, _, Rh, Cc = g.shape
        b, f = _pair_sum(g.reshape(A * N_CHIPS, 2, Rh, Cc), r.reshape(A * N_CHIPS, Rh, Cc), core, shard)
        qb.append(b.reshape(A, N_CHIPS, Rh, Cc))
        qf.append(f)
    return qb, qf


def _reduce_finish(qf, r2):
    return _as_shards(_join_sibling_halves(_chip_sums(qf, r2)))


def _chip_sums(qf, r2):
    halves = []
    for f, r in zip(qf, r2):
        A, Rh, Cc = f.shape
        terms = [f.reshape(A * Rh, Cc)] + [r[j].reshape(A * Rh, Cc) for j in range(3)]
        halves.append(_sum_rows("grad_chip_sum", terms, [F32])[0].reshape(A, Rh, Cc))
    return halves


def _as_shards(full):
    return [t.reshape(t.shape[0], 2 * t.shape[2], t.shape[3]) for t in full]


def _small_pair(v):
    return _sum_rows("small_pair_sum", [v, _swap_small(v)], [F32])[0]


def _small_chip_sum(pair, others):
    x, y = lax.axis_index("x"), lax.axis_index("y")
    s = 2 * x + y
    stack = jnp.concatenate([pair[None], others], axis=0)
    src = jnp.stack([s, s ^ 2, s ^ 1, s ^ 3])
    order = jnp.argsort(src)
    terms = [lax.dynamic_index_in_dim(stack, order[k], 0, keepdims=False) for k in range(N_CHIPS)]
    return _sum_rows("small_chip_sum", terms, [F32])[0]


def _block_diag(w):
    nb, bw, _ = w.shape
    per = LANES // bw
    w = w.reshape(nb // per, per, bw, bw)
    eye = jnp.eye(per, dtype=w.dtype)
    bd = jnp.einsum("tpij,pq->tpiqj", w, eye).reshape(nb // per, LANES, LANES)
    return bd.astype(BF16)


def _block_diag_grad(g, bw):
    nt = g.shape[0]
    per = LANES // bw
    g = g.reshape(nt, per, bw, per, bw)
    return jnp.stack([g[:, p, :, p, :] for p in range(per)], axis=1).reshape(nt * per, bw, bw)


def _split5(w):
    R, Cc = w.shape[-2:]
    return w.reshape(-1, 2, R // 2, Cc)


def kernel(x, w_in, conv_w, conv_b, w_rg, b_rg, w_ig, b_ig, lru_lambda, sinks, w_branch, w_out, ln1_g, ln1_b, w_ffn_in, w_ffn_out, ln2_g, ln2_b, loss_target, m_w_in, m_conv_w, m_conv_b, m_w_rg, m_b_rg, m_w_ig, m_b_ig, m_lru_lambda, m_sinks, m_w_branch, m_w_out, m_ln1_g, m_ln1_b, m_w_ffn_in, m_w_ffn_out, m_ln2_g, m_ln2_b, v_w_in, v_conv_w, v_conv_b, v_w_rg, v_b_rg, v_w_ig, v_b_ig, v_lru_lambda, v_sinks, v_w_branch, v_w_out, v_ln1_g, v_ln1_b, v_w_ffn_in, v_w_ffn_out, v_ln2_g, v_ln2_b):
    B, S, D = x.shape
    T = B * S
    L = w_in.shape[0]
    H = D // HEAD_DIM
    KVB = D // SWA_GROUP
    FH = w_ffn_out.shape[1] * N_CHIPS
    C = w_in.shape[2] * N_CHIPS
    alpha = (2.0 * L) ** 0.25
    off = {}
    pos = 0
    for nm, wd in (("lx", D), ("lg", D), ("qb", D), ("kb", KVB), ("vb", KVB), ("qc", D), ("kc", D), ("vc", D), ("gt", 3 * D)):
        off[nm] = pos
        pos += wd
    assert pos == C
    cx, cy, cc = lax.axis_index("x"), lax.axis_index("y"), lax.axis_index("c")
    shard = (2 * cx + cy).astype(jnp.int32)
    core_a = cc.astype(jnp.int32).reshape(1)
    shard_a = shard.reshape(1)

    def shard_views(l):
        return [_split5(w_in[l].astype(BF16)), _split5(w_branch[l].astype(BF16)), _split5(w_out[l].astype(BF16)),
                _split5(w_ffn_in[l].astype(BF16)), _split5(w_ffn_out[l].astype(BF16))]

    def as_weights(g):
        return dict(
            w_in=g[0].reshape(N_CHIPS, D, C // N_CHIPS),
            w_branch=g[1].reshape(3, D, D),
            w_out=g[2].reshape(D, D),
            w_ffn_in=g[3].reshape(N_CHIPS, D, 2 * FH // N_CHIPS),
            w_ffn_out=g[4].reshape(FH, D),
        )

    first_views = shard_views(0)
    w_in0 = _gather_weights(first_views[:1])
    full = [dict(w_in=w_in0[0].reshape(N_CHIPS, D, C // N_CHIPS))]
    cw_all = _gather_small(conv_w.reshape(L * CONV_WIDTH, D // N_CHIPS))
    conv_w_full = jnp.transpose(cw_all, (1, 0, 2)).reshape(L, CONV_WIDTH, D)

    def layer_params(l):
        return dict(conv_w=conv_w_full[l], conv_b=conv_b[l][None], w_rg_bd=_block_diag(w_rg[l]), b_rg=b_rg[l][None],
                    w_ig_bd=_block_diag(w_ig[l]), b_ig=b_ig[l][None], lam=lru_lambda[l][None])

    def sink_rows(l, hb):
        sk = sinks[l].reshape(H // hb, 1, hb)
        return jnp.pad(sk, ((0, 0), (0, 0), (0, LANES - hb)))

    hb_b = SWA_HB

    saved = []
    xin = x.reshape(T, D)
    for l in range(L):
        fw, lp = full[l], layer_params(l)
        nxt = shard_views(l + 1) if l + 1 < L else None
        own, ahead = {}, {}
        if l == 0:
            own = {"lru": (3,), "swa": (1, 2, 4)}
            ahead = {"proj": (0,), "dil": (1, 2, 3, 4)} if nxt is not None else {}
        elif nxt is not None:
            ahead = {"lru": (3,), "swa": (0,), "dil": (1, 2, 4)}
        landed_own, landed_next = {}, {}

        def carried(host):
            idx_own, idx_next = own.get(host, ()), ahead.get(host, ())
            views = [first_views[t] for t in idx_own] + [nxt[t] for t in idx_next]
            if not views:
                return None, lambda bufs: None

            def file(bufs):
                for t, buf in zip(idx_own, bufs[:len(idx_own)]):
                    landed_own[t] = buf
                for t, buf in zip(idx_next, bufs[len(idx_own):]):
                    landed_next[t] = buf
            return _gather_rider(views), file

        rider, file = carried("proj")
        proj_kw = dict(mode="nn", name="mm_proj", tm=512, n_outer=True)
        if rider is None:
            proj = _matmul(xin, fw["w_in"], **proj_kw)
        else:
            proj, bufs = _matmul(xin, fw["w_in"], rider=rider, **proj_kw)
            file(bufs)
        proj3 = proj.reshape(B, S, C)
        rider, file = carried("lru")
        res = _lru_fwd(proj3, lp, D=D, x_off=off["lx"], g_off=off["lg"], rider=rider)
        h3, ya3 = res[0], res[1]
        file(res[2:])
        skr = sink_rows(l, hb_b)
        rider, file = carried("swa")
        res = _swa_seq_fwd(proj3, skr, D=D, q_off=off["qb"], k_off=off["kb"], v_off=off["vb"], rider=rider)
        yb3, lse_b = res[0], res[1]
        file(res[2:])
        if l == 0:
            rest = sorted(landed_own)
            got = _gather_forward([landed_own[t] for t in rest], [first_views[t] for t in rest])
            fw = as_weights(w_in0 + got)
            full[0] = fw
        rider, file = carried("dil")
        res = _dil_fwd(proj3, D=D, q_off=off["qc"], k_off=off["kc"], v_off=off["vc"], rider=rider)
        yc3, lse_c = res[0], res[1]
        file(res[2:])
        ya, yb, yc = ya3.reshape(T, D), yb3.reshape(T, D), yc3.reshape(T, D)
        if nxt is not None:
            landed = [landed_next[t] for t in range(len(nxt))]
            res = _branch_fwd([ya, yb, yc], fw["w_branch"], proj, D=D, g_off=off["gt"], rider=_forward_rider(landed))
            branch, merged = res[0], res[1]
            full.append(as_weights(_place_own_shard(res[2:], nxt)))
        else:
            branch, merged = _branch_fwd([ya, yb, yc], fw["w_branch"], proj, D=D, g_off=off["gt"])
        z1, x1 = _matmul(merged, fw["w_out"], mode="nn", name="mm_out_ln", tn=1024, resid=xin, rs=alpha,
                         ln=(ln1_g[l][None], ln1_b[l][None]))
        ffn_h1, ffn_h3, f = _ffn_in_fwd(x1, fw["w_ffn_in"])
        z2, x2 = _matmul(f, fw["w_ffn_out"], mode="nn", name="mm_ffn_out_ln", tn=1024, tk=4096, resid=x1, rs=alpha,
                         ln=(ln2_g[l][None], ln2_b[l][None]))
        saved.append(dict(x=xin, proj=proj, h3=h3, ya=ya, yb=yb, lse_b=lse_b, yc=yc, lse_c=lse_c, branch=branch,
                          merged=merged, z1=z1, x1=x1, ffn_h1=ffn_h1, ffn_h3=ffn_h3, f=f, z2=z2, skr=skr))
        xin = x2

    dx, loss_rows = _loss_head(xin, loss_target.reshape(T, D))
    loss = lax.psum(jnp.sum(loss_rows), ("x", "y", "c"))

    big = {k: [None] * L for k in ("w_in", "w_branch", "w_out", "w_ffn_in", "w_ffn_out")}
    small = [None] * L

    def store_reduced(l, red):
        big["w_in"][l] = red[0].reshape(D, C // N_CHIPS)
        big["w_branch"][l] = red[1].reshape(3, D // N_CHIPS, D)
        big["w_out"][l] = red[2].reshape(D // N_CHIPS, D)
        big["w_ffn_in"][l] = red[3].reshape(D, 2 * FH // N_CHIPS)
        big["w_ffn_out"][l] = red[4].reshape(FH // N_CHIPS, D)

    above = None
    pending = None
    for l in reversed(range(L)):
        fw, lp, sv = full[l], layer_params(l), saved[l]
        dz2, dg2, db2 = _ln_bwd(dx, sv["z2"], ln2_g[l][None])
        g_ffn_out = _matmul(sv["f"], dz2, mode="tn", name="mm_dffn_out_w", tm=1408, tn=1024, tk=1024)
        dhh = _ffn_out_bwd(dz2, fw["w_ffn_out"], sv["ffn_h1"], sv["ffn_h3"])
        dx1 = _matmul(dhh, fw["w_ffn_in"], mode="nt", name="mm_dffn_in_x", tm=1024, tn=1024, resid=dz2, rs=alpha)
        g_ffn_in = _matmul(sv["x1"], dhh, mode="tn", name="mm_dffn_in_w", tm=1024, tk=1024, out_shards=N_CHIPS)
        dz1, dg1, db1 = _ln_bwd(dx1, sv["z1"], ln1_g[l][None])
        g_out = _matmul(sv["merged"], dz1, mode="tn", name="mm_dout_w", tm=1024, tn=1024, tk=1024)
        dbranch, dgates = _branch_bwd(dz1, fw["w_out"], sv["branch"], sv["proj"], D=D, g_off=off["gt"])
        ys = [sv["ya"], sv["yb"], sv["yc"]]
        dys, g_branch = [], []
        for n in range(3):
            dys.append(_matmul(dbranch, fw["w_branch"][n], mode="nt", name="mm_dbranch_x", tn=1024, tk=1024, a_pick=n))
            g_branch.append(_matmul(ys[n], dbranch, mode="tn", name="mm_dbranch_w", tm=1024, tn=1024, tk=1024, b_pick=n))
        proj3 = sv["proj"].reshape(B, S, C)
        r3 = lambda t: t.reshape(B, S, t.shape[-1])
        lru = _lru_bwd(proj3, sv["h3"], r3(dys[0]), lp, D=D, x_off=off["lx"], g_off=off["lg"],
                       rider=None if above is None else _swap_rider(above[1]))
        if above is not None:
            pending = (above[0],) + _reduce_chip(above[1], lru[9:], core_a, shard_a)
        dxr, dgate = lru[0], lru[1]
        dqb, dkb, dvb, dsk = _swa_seq_bwd(proj3, r3(sv["yb"]), sv["lse_b"], r3(dys[1]), sv["skr"], D=D, q_off=off["qb"],
                                      k_off=off["kb"], v_off=off["vb"])
        dil_kw = dict(D=D, q_off=off["qc"], k_off=off["kc"], v_off=off["vc"])
        if pending is None:
            acc = _dil_bwd(proj3, r3(sv["yc"]), sv["lse_c"], r3(dys[2]), **dil_kw)
        else:
            res = _dil_bwd(proj3, r3(sv["yc"]), sv["lse_c"], r3(dys[2]), rider=_exchange_rider(pending[1]), **dil_kw)
            acc = res[:3]
            halves = _chip_sums(pending[2], res[3:])
        f2 = lambda t: t.reshape(T, t.shape[-1]).astype(BF16)
        dproj = jnp.concatenate([f2(dxr), f2(dgate), f2(dqb), f2(dkb), f2(dvb), f2(acc[0]), f2(acc[1]), f2(acc[2])] + dgates, axis=1)
        dx_kw = dict(mode="nt", name="mm_dproj_x", tm=1024, tn=1024, resid=dz1, rs=alpha)
        if pending is None:
            dx = _matmul(dproj, fw["w_in"], **dx_kw)
        else:
            dx, got = _matmul(dproj, fw["w_in"], rider=_join_rider(halves), **dx_kw)
            store_reduced(pending[0], _as_shards(_place_own_half(got, halves)))
        g_in = _matmul(sv["x"], dproj, mode="tn", name="mm_dproj_w", tm=1024, tk=512, out_shards=N_CHIPS)

        g5 = [g_in.reshape(1, N_CHIPS, 2, D // 2, C // N_CHIPS),
              jnp.stack(g_branch).reshape(3, N_CHIPS, 2, D // N_CHIPS // 2, D),
              g_out.reshape(1, N_CHIPS, 2, D // N_CHIPS // 2, D),
              g_ffn_in.reshape(1, N_CHIPS, 2, D // 2, 2 * FH // N_CHIPS),
              g_ffn_out.reshape(1, N_CHIPS, 2, FH // N_CHIPS // 2, D)]
        above = (l, g5)

        dsinks = jnp.sum(dsk, axis=0)[:, 0, :hb_b].reshape(H)
        bw = w_rg.shape[-1]
        small[l] = [lru[2].reshape(-1), lru[3].reshape(-1), _block_diag_grad(lru[4], bw).reshape(-1), lru[5].reshape(-1),
                    _block_diag_grad(lru[6], bw).reshape(-1), lru[7].reshape(-1), lru[8].reshape(-1),
                    jnp.pad(dsinks, (0, LANES - H)), dg1.reshape(-1), db1.reshape(-1), dg2.reshape(-1), db2.reshape(-1)]

    qb, qf = _reduce_chip(above[1], _swap_sibling_halves(above[1]), core_a, shard_a)
    store_reduced(above[0], _reduce_finish(qf, _exchange_chips(qb)))

    sizes = [t.size for t in small[0]]
    flat = jnp.concatenate([t for l in range(L) for t in small[l]])
    n_flat = flat.size
    rows = -(-n_flat // (LANES * 256)) * 256
    flat = jnp.pad(flat, (0, rows * LANES - n_flat)).reshape(rows, LANES)
    pair = _small_pair(flat)

    order = ["w_in", "conv_w", "conv_b", "w_rg", "b_rg", "w_ig", "b_ig", "lru_lambda", "sinks", "w_branch", "w_out",
             "ln1_g", "ln1_b", "w_ffn_in", "w_ffn_out", "ln2_g", "ln2_b"]
    weights = dict(w_in=w_in, conv_w=conv_w, conv_b=conv_b, w_rg=w_rg, b_rg=b_rg, w_ig=w_ig, b_ig=b_ig, lru_lambda=lru_lambda,
                   sinks=sinks, w_branch=w_branch, w_out=w_out, ln1_g=ln1_g, ln1_b=ln1_b, w_ffn_in=w_ffn_in,
                   w_ffn_out=w_ffn_out, ln2_g=ln2_g, ln2_b=ln2_b)
    ms = dict(w_in=m_w_in, conv_w=m_conv_w, conv_b=m_conv_b, w_rg=m_w_rg, b_rg=m_b_rg, w_ig=m_w_ig, b_ig=m_b_ig,
              lru_lambda=m_lru_lambda, sinks=m_sinks, w_branch=m_w_branch, w_out=m_w_out, ln1_g=m_ln1_g, ln1_b=m_ln1_b,
              w_ffn_in=m_w_ffn_in, w_ffn_out=m_w_ffn_out, ln2_g=m_ln2_g, ln2_b=m_ln2_b)
    vs = dict(w_in=v_w_in, conv_w=v_conv_w, conv_b=v_conv_b, w_rg=v_w_rg, b_rg=v_b_rg, w_ig=v_w_ig, b_ig=v_b_ig,
              lru_lambda=v_lru_lambda, sinks=v_sinks, w_branch=v_w_branch, w_out=v_w_out, ln1_g=v_ln1_g, ln1_b=v_ln1_b,
              w_ffn_in=v_w_ffn_in, w_ffn_out=v_w_ffn_out, ln2_g=v_ln2_g, ln2_b=v_ln2_b)
    grads = dict(w_in=jnp.stack(big["w_in"]), w_branch=jnp.stack(big["w_branch"]), w_out=jnp.stack(big["w_out"]),
                 w_ffn_in=jnp.stack(big["w_ffn_in"]), w_ffn_out=jnp.stack(big["w_ffn_out"]))
    deltas, new_m, new_v = {}, {}, {}
    deltas["w_in"], new_m["w_in"], new_v["w_in"], others = _adamw(w_in, grads["w_in"], m_w_in, v_w_in,
                                                                  rider=_small_exchange_rider(pair))
    red_small = _small_chip_sum(pair, others).reshape(-1)
    per_layer = sum(sizes)
    names = ["conv_w", "conv_b", "w_rg", "b_rg", "w_ig", "b_ig", "lru_lambda", "sinks", "ln1_g", "ln1_b", "ln2_g", "ln2_b"]
    sg = {nm: [] for nm in names}
    for l in range(L):
        p = l * per_layer
        for nm, sz in zip(names, sizes):
            sg[nm].append(red_small[p:p + sz])
            p += sz
    grads.update(
        conv_w=lax.dynamic_slice_in_dim(jnp.stack(sg["conv_w"]).reshape(L, CONV_WIDTH, D), shard * (D // N_CHIPS), D // N_CHIPS, axis=2),
        conv_b=jnp.stack(sg["conv_b"]), w_rg=jnp.stack(sg["w_rg"]).reshape(w_rg.shape), b_rg=jnp.stack(sg["b_rg"]),
        w_ig=jnp.stack(sg["w_ig"]).reshape(w_ig.shape), b_ig=jnp.stack(sg["b_ig"]), lru_lambda=jnp.stack(sg["lru_lambda"]),
        sinks=jnp.stack(sg["sinks"])[:, :H], ln1_g=jnp.stack(sg["ln1_g"]), ln1_b=jnp.stack(sg["ln1_b"]),
        ln2_g=jnp.stack(sg["ln2_g"]), ln2_b=jnp.stack(sg["ln2_b"]),
    )

    for nm in order[1:]:
        deltas[nm], new_m[nm], new_v[nm] = _adamw(weights[nm], grads[nm], ms[nm], vs[nm])
    return (loss, dx.reshape(B, S, D), *[grads[nm] for nm in order], *[deltas[nm] for nm in order],
            *[new_m[nm] for nm in order], *[new_v[nm] for nm in order])
```

```python
import math

import jax
import jax.numpy as jnp
from jax import lax
from jax.experimental import pallas as pl
from jax.experimental.pallas import tpu as pltpu

HEAD_DIM = 64
WIN = 128
DILS = (1, 4, 16)
SWA_GROUP = 4
CONV_WIDTH = 4
LRU_C = 8.0
LN_EPS = 1e-5
NEG_INF = -1e30
N_CHIPS = 4
ADAM_LR, ADAM_B1, ADAM_B2, ADAM_EPS, ADAM_WD, ADAM_STEP = 0.001, 0.9, 0.999, 1e-08, 0.01, 10

LANES = 128
SUBLANES = 8
VMEM_LIMIT = 48 * 1024 * 1024

assert math.log2(HEAD_DIM) % 2 == 0

F32 = jnp.float32
BF16 = jnp.bfloat16
MESH = pl.DeviceIdType.MESH
ANY = pl.BlockSpec(memory_space=pl.ANY)


def _pcall(body, **kw):
    return pl.pallas_call(body, **kw)


def _pcall_comm(body, **kw):
    return pl.pallas_call(body, **kw)


def _params(*sem):
    return pltpu.CompilerParams(dimension_semantics=tuple(sem), vmem_limit_bytes=VMEM_LIMIT)


def _tile(dim, target):
    if dim <= target:
        return dim
    best = None
    for t in range(LANES, target + 1, LANES):
        if dim % t == 0:
            best = t
    assert best is not None, (dim, target)
    return best


def _sigmoid(x):
    return 1.0 / (1.0 + jnp.exp(-x))


def _dot(a, b, dims):
    return lax.dot_general(a, b, (dims, ((), ())), preferred_element_type=F32)


def _dot_nn(a, b):
    return _dot(a, b, ((1,), (0,)))


def _dot_nt(a, b):
    return _dot(a, b, ((1,), (1,)))


def _dot_tn(a, b):
    return _dot(a, b, ((0,), (0,)))


def _matmul(a, b, *, mode, name, out_dtype=F32, tm=512, tn=512, tk=2048, resid=None, rs=1.0, out_shards=0, n_outer=False,
            ln=None, a_pick=0, b_pick=0, rider=None):
    b_sh = b.ndim == 3
    a_st = a.ndim == 3
    if mode == "nn":
        M, K = a.shape[-2:]
        N = b.shape[-1] * (b.shape[0] if b_sh else 1)
    elif mode == "nt":
        M, K = a.shape[-2:]
        N = b.shape[-2]
    else:
        K, M = a.shape
        N = b.shape[-1]
    tm = _tile(M, tm)
    if mode == "nn" and b_sh:
        tn = b.shape[-1]
    elif out_shards:
        tn = N // out_shards
    else:
        tn = _tile(N, tn)
    if mode == "nt" and b_sh:
        tk = b.shape[-1]
    else:
        tk = _tile(K, tk)
    nk = K // tk
    grid = (N // tn, M // tm, nk) if n_outer else (M // tm, N // tn, nk)

    def spec(shape, f):
        return pl.BlockSpec(shape, (lambda g0, g1, k: f(g1, g0, k)) if n_outer else f)

    a_rows = spec((None, tm, tk), lambda i, j, k: (a_pick, i, k)) if a_st else spec((tm, tk), lambda i, j, k: (i, k))
    if mode == "nn":
        a_spec = a_rows
        b_spec = spec((None, tk, tn), lambda i, j, k: (j, k, 0)) if b_sh else spec((tk, tn), lambda i, j, k: (k, j))
        contract = _dot_nn
    elif mode == "nt":
        a_spec = a_rows
        b_spec = spec((None, tn, tk), lambda i, j, k: (k, j, 0)) if b_sh else spec((tn, tk), lambda i, j, k: (j, k))
        contract = _dot_nt
    else:
        a_spec = spec((tk, tm), lambda i, j, k: (k, i))
        b_spec = spec((None, tk, tn), lambda i, j, k: (b_pick, k, j)) if b_sh else spec((tk, tn), lambda i, j, k: (k, j))
        contract = _dot_tn
    if out_shards:
        out_shape = jax.ShapeDtypeStruct((out_shards, M, tn), out_dtype)
        o_spec = spec((None, tm, tn), lambda i, j, k: (j, i, 0))
    else:
        out_shape = jax.ShapeDtypeStruct((M, N), out_dtype)
        o_spec = spec((tm, tn), lambda i, j, k: (i, j))
    in_specs = [a_spec, b_spec]
    args = [a, b]
    if resid is not None:
        in_specs.append(spec((tm, tn), lambda i, j, k: (i, j)))
        args.append(resid)
    if ln is not None:
        assert tn == N and resid is not None and not out_shards
        in_specs += [spec((1, N), lambda i, j, k: (0, 0))] * 2
        args += list(ln)
        out_shape = [out_shape, out_shape]
        o_spec = [o_spec, o_spec]
    n_in = len(args)

    def body(*refs):
        a_ref, b_ref = refs[:2]
        r_ref = refs[2] if resid is not None else None
        o_ref = refs[n_in]
        part = contract(a_ref[...].astype(BF16), b_ref[...].astype(BF16))

        def finish(res):
            if resid is not None:
                res = res + rs * r_ref[...]
            o_ref[...] = res.astype(out_dtype)
            if ln is not None:
                g_ref, bb_ref, y_ref = refs[n_in - 2], refs[n_in - 1], refs[n_in + 1]
                zc = res - jnp.mean(res, axis=1, keepdims=True)
                var = jnp.mean(zc * zc, axis=1, keepdims=True)
                y_ref[...] = zc * lax.rsqrt(var + LN_EPS) * g_ref[...] + bb_ref[...]

        if nk == 1:
            finish(part)
            return
        acc_ref = refs[-1]
        k = pl.program_id(2)

        @pl.when(k == 0)
        def _():
            acc_ref[...] = part

        @pl.when(jnp.logical_and(k > 0, k < nk - 1))
        def _():
            acc_ref[...] += part

        @pl.when(k == nk - 1)
        def _():
            finish(acc_ref[...] + part)

    if rider is None:
        return _pcall(
            body, name=name, grid=grid, in_specs=in_specs, out_specs=o_spec, out_shape=out_shape,
            scratch_shapes=[pltpu.VMEM((tm, tn), F32)] if nk > 1 else [],
            compiler_params=_params("parallel", "parallel", "arbitrary"),
        )(*args)
    assert ln is None
    res = _call_with_rider(
        rider, body, name=name, grid=grid, in_specs=in_specs, out_specs=[o_spec], out_shape=[out_shape],
        scratch_shapes=[pltpu.VMEM((tm, tn), F32)] if nk > 1 else [], args=args)
    return res[0], list(res[1:])


def _shift_down(x, d, row):
    return jnp.where(row >= d, pltpu.roll(x, d, 0), 0.0)


def _shift_up(x, d, row, n):
    return jnp.where(row < n - d, pltpu.roll(x, n - d, 0), 0.0)


def _log1p(u):
    w = 1.0 + u
    return jnp.where(w == 1.0, u, jnp.log(w) * u / (w - 1.0))


def _gelu_parts(g):
    k = math.sqrt(2.0 / math.pi)
    c = 0.044715
    t = jnp.tanh(k * (g + c * g * g * g))
    val = 0.5 * g * (1.0 + t)
    der = 0.5 * (1.0 + t) + 0.5 * g * (1.0 - t * t) * k * (1.0 + 3.0 * c * g * g)
    return val, der


def _lru_gates(xr, cw_ref, cb_ref, wrg_ref, brg_ref, wig_ref, big_ref, lam_ref, row):
    xc = cw_ref[3:4, :] * xr + cb_ref[...]
    for d in range(1, CONV_WIDTH):
        xc = xc + cw_ref[3 - d:4 - d, :] * _shift_down(xr, d, row)
    xcb = xc.astype(BF16)
    r = _sigmoid(_dot_nn(xcb, wrg_ref[...]) + brg_ref[...])
    ig = _sigmoid(_dot_nn(xcb, wig_ref[...]) + big_ref[...])
    lam = lam_ref[...]
    sp = jnp.maximum(-lam, 0.0) + _log1p(jnp.exp(-jnp.abs(lam)))
    log_a = (-LRU_C) * r * sp
    a = jnp.exp(log_a)
    y2 = 2.0 * log_a
    one_m_a2 = jnp.where(y2 > -0.01, -(y2 + 0.5 * y2 * y2 + (1.0 / 6.0) * y2 * y2 * y2), 1.0 - jnp.exp(y2))
    mult = jnp.sqrt(one_m_a2)
    return xc, r, ig, sp, a, mult


def _scan_local(a, b, row, n, reverse):
    sub = row % SUBLANES
    d = 1
    while d < SUBLANES:
        if reverse:
            keep = sub < SUBLANES - d
            a_s = jnp.where(keep, pltpu.roll(a, n - d, 0), 1.0)
            b_s = jnp.where(keep, pltpu.roll(b, n - d, 0), 0.0)
        else:
            keep = sub >= d
            a_s = jnp.where(keep, pltpu.roll(a, d, 0), 1.0)
            b_s = jnp.where(keep, pltpu.roll(b, d, 0), 0.0)
        b = a * b_s + b
        a = a * a_s
        d *= 2
    return a, b


def _scan_carry(a_ref, b_ref, out_ref, n, reverse):
    ng = n // SUBLANES

    def step(gidx, carry):
        g = (ng - 1 - gidx) if reverse else gidx
        rows = pl.ds(pl.multiple_of(g * SUBLANES, SUBLANES), SUBLANES)
        h = a_ref[rows, :] * carry + b_ref[rows, :]
        out_ref[rows, :] = h
        return h[0:1, :] if reverse else h[SUBLANES - 1:SUBLANES, :]

    lax.fori_loop(0, ng, step, jnp.zeros((1, LANES), F32), unroll=8)


def _lru_specs(B, S, D, C, x_off, g_off):
    nct = D // LANES
    seq = lambda off: pl.BlockSpec((None, S, LANES), lambda ct, b: (b, 0, off // LANES + ct))
    row = lambda r: pl.BlockSpec((r, LANES), lambda ct, b: (0, ct))
    wbd = pl.BlockSpec((None, LANES, LANES), lambda ct, b: (ct, 0, 0))
    return nct, seq, row, wbd


def _lru_fwd(proj3, lp, *, D, x_off, g_off, rider=None):
    B, S, C = proj3.shape
    nct, seq, row, wbd = _lru_specs(B, S, D, C, x_off, g_off)

    def body(xr_ref, g_ref, cw_ref, cb_ref, wrg_ref, brg_ref, wig_ref, big_ref, lam_ref, h_ref, ya_ref, a_s, b_s):
        rowi = lax.broadcasted_iota(jnp.int32, (S, LANES), 0)
        xr = xr_ref[...]
        xc, r, ig, sp, a, mult = _lru_gates(xr, cw_ref, cb_ref, wrg_ref, brg_ref, wig_ref, big_ref, lam_ref, rowi)
        al, bl = _scan_local(a, mult * (ig * xc), rowi, S, False)
        a_s[...] = al
        b_s[...] = bl
        _scan_carry(a_s, b_s, h_ref, S, False)
        gel, _ = _gelu_parts(g_ref[...])
        ya_ref[...] = (h_ref[...] * gel).astype(BF16)

    out_seq = pl.BlockSpec((None, S, LANES), lambda ct, b: (b, 0, ct))
    return _call_with_rider(
        rider, body, name="lru_fwd", grid=(nct, B),
        in_specs=[seq(x_off), seq(g_off), row(CONV_WIDTH), row(1), wbd, row(1), wbd, row(1), row(1)],
        out_specs=[out_seq, out_seq],
        out_shape=[jax.ShapeDtypeStruct((B, S, D), F32), jax.ShapeDtypeStruct((B, S, D), BF16)],
        scratch_shapes=[pltpu.VMEM((S, LANES), F32), pltpu.VMEM((S, LANES), F32)],
        args=[proj3, proj3, lp["conv_w"], lp["conv_b"], lp["w_rg_bd"], lp["b_rg"], lp["w_ig_bd"], lp["b_ig"], lp["lam"]])


def _lru_bwd(proj3, h3, dya3, lp, *, D, x_off, g_off, rider=None):
    B, S, C = proj3.shape
    nct, seq, row, wbd = _lru_specs(B, S, D, C, x_off, g_off)

    def body(xr_ref, g_ref, h_ref, dy_ref, cw_ref, cb_ref, wrg_ref, brg_ref, wig_ref, big_ref, lam_ref,
             dxr_ref, dg_ref, dcw_ref, dcb_ref, dwrg_ref, dbrg_ref, dwig_ref, dbig_ref, dlam_ref, a_s, b_s, l_s):
        first = pl.program_id(1) == 0
        rowi = lax.broadcasted_iota(jnp.int32, (S, LANES), 0)
        xr = xr_ref[...]
        xc, r, ig, sp, a, mult = _lru_gates(xr, cw_ref, cb_ref, wrg_ref, brg_ref, wig_ref, big_ref, lam_ref, rowi)
        h = h_ref[...]
        dy = dy_ref[...]
        gel, dgel = _gelu_parts(g_ref[...])
        dg_ref[...] = (dy * h * dgel).astype(BF16)
        al, bl = _scan_local(_shift_up(a, 1, rowi, S), dy * gel, rowi, S, True)
        a_s[...] = al
        b_s[...] = bl
        _scan_carry(a_s, b_s, l_s, S, True)
        lamb = l_s[...]
        u = ig * xc
        da = lamb * _shift_down(h, 1, rowi)
        dlog_a = da * a - (lamb * u) * (a * a) / mult
        du = lamb * mult
        dpre_r = (dlog_a * ((-LRU_C) * sp)) * r * (1.0 - r)
        dpre_i = (du * xc) * ig * (1.0 - ig)
        dsp = jnp.sum(dlog_a * ((-LRU_C) * r), axis=0, keepdims=True)
        dlam = dsp * (-1.0 / (1.0 + jnp.exp(lam_ref[...])))
        dpr = dpre_r.astype(BF16)
        dpi = dpre_i.astype(BF16)
        dxc = du * ig + _dot_nt(dpr, wrg_ref[...]) + _dot_nt(dpi, wig_ref[...])
        xcb = xc.astype(BF16)
        dwrg = _dot_tn(xcb, dpr)
        dwig = _dot_tn(xcb, dpi)
        dxr = cw_ref[3:4, :] * dxc
        dcw = [jnp.sum(xr * dxc, axis=0, keepdims=True)]
        for d in range(1, CONV_WIDTH):
            dxr = dxr + cw_ref[3 - d:4 - d, :] * _shift_up(dxc, d, rowi, S)
            dcw.append(jnp.sum(_shift_down(xr, d, rowi) * dxc, axis=0, keepdims=True))
        dxr_ref[...] = dxr.astype(BF16)
        dcw_rows = jnp.concatenate(dcw[::-1], axis=0)
        sums = ((dcw_ref, dcw_rows), (dcb_ref, jnp.sum(dxc, axis=0, keepdims=True)), (dwrg_ref, dwrg),
                (dbrg_ref, jnp.sum(dpre_r, axis=0, keepdims=True)), (dwig_ref, dwig),
                (dbig_ref, jnp.sum(dpre_i, axis=0, keepdims=True)), (dlam_ref, dlam))

        @pl.when(first)
        def _():
            for ref, val in sums:
                ref[...] = val

        @pl.when(jnp.logical_not(first))
        def _():
            for ref, val in sums:
                ref[...] += val

    out_seq = pl.BlockSpec((None, S, LANES), lambda ct, b: (b, 0, ct))
    f = lambda shape: jax.ShapeDtypeStruct(shape, F32)
    nb = D // LANES
    return _call_with_rider(
        rider, body, name="lru_bwd", grid=(nct, B),
        in_specs=[seq(x_off), seq(g_off), out_seq, out_seq, row(CONV_WIDTH), row(1), wbd, row(1), wbd, row(1), row(1)],
        out_specs=[out_seq, out_seq, row(CONV_WIDTH), row(1), wbd, row(1), wbd, row(1), row(1)],
        out_shape=[jax.ShapeDtypeStruct((B, S, D), BF16), jax.ShapeDtypeStruct((B, S, D), BF16),
                   f((CONV_WIDTH, D)), f((1, D)), f((nb, LANES, LANES)), f((1, D)), f((nb, LANES, LANES)), f((1, D)), f((1, D))],
        scratch_shapes=[pltpu.VMEM((S, LANES), F32)] * 3, semantics=("parallel", "arbitrary"),
        args=[proj3, proj3, h3, dya3, lp["conv_w"], lp["conv_b"], lp["w_rg_bd"], lp["b_rg"], lp["w_ig_bd"], lp["b_ig"], lp["lam"]])


def _pair_stack(x, lo):
    z = jnp.zeros_like(x)
    return jnp.concatenate([jnp.where(lo, x, z), jnp.where(lo, z, x)], axis=0).astype(BF16)


def _pair_join(y2, lo):
    return jnp.where(lo, y2[:WIN], y2[WIN:])


def _pair_col(xb):
    return jnp.concatenate([xb[:, 0:1], xb[:, HEAD_DIM:HEAD_DIM + 1]], axis=0)


def _pair_bcast(col, lo):
    return jnp.where(lo, jnp.broadcast_to(col[:WIN], (WIN, LANES)), jnp.broadcast_to(col[WIN:], (WIN, LANES)))


def _dil_rows(it, d, S):
    if d == 1:
        cur = pl.multiple_of(it * WIN, WIN)
        prev = pl.multiple_of(jnp.maximum(it - 1, 0) * WIN, WIN)
        return pl.ds(cur, WIN), pl.ds(prev, WIN), it > 0
    r, i = it % d, it // d
    cur = i * (WIN * d) + r
    prev = jnp.maximum(i - 1, 0) * (WIN * d) + r
    return pl.ds(cur, WIN, stride=d), pl.ds(prev, WIN, stride=d), i > 0


def _dil_bias(two_blocks, stack=2):
    nk = 2 * WIN if two_blocks else WIN
    qi = lax.broadcasted_iota(jnp.int32, (stack * WIN, nk), 0) & (WIN - 1)
    kj = lax.broadcasted_iota(jnp.int32, (stack * WIN, nk), 1)
    if not two_blocks:
        return jnp.where(kj <= qi, 0.0, NEG_INF), None
    cur = jnp.logical_and(kj >= WIN, kj - WIN <= qi)
    prev = jnp.logical_and(kj < WIN, kj >= qi)
    return jnp.where(jnp.logical_or(cur, prev), 0.0, NEG_INF), jnp.where(cur, 0.0, NEG_INF)


def _dil_specs(B, S, D, C, offs):
    grid = (B, D // LANES)
    seq = lambda off: pl.BlockSpec((None, S, LANES), lambda b, p: (b, 0, off // LANES + p))
    return grid, [seq(o) for o in offs], seq(0)


def _call_with_rider(rider, body, *, name, grid, in_specs, out_specs, out_shape, scratch_shapes, args, semantics=None):
    if rider is None:
        return _pcall(body, name=name, grid=grid, in_specs=in_specs, out_specs=out_specs, out_shape=out_shape,
                      scratch_shapes=scratch_shapes, compiler_params=_params(*(semantics or ("parallel",) * len(grid))))(*args)
    n_in, n_out, n_sc = len(in_specs), len(out_specs), len(scratch_shapes)
    r_in, r_out = len(rider["ins"]), len(rider["out_shapes"])

    def wrapped(*refs):
        p = 0
        own_in = refs[p:p + n_in]; p += n_in
        rid_in = refs[p:p + r_in]; p += r_in
        own_out = refs[p:p + n_out]; p += n_out
        rid_out = refs[p:p + r_out]; p += r_out
        own_sc = refs[p:p + n_sc]; p += n_sc
        send_sems, recv_sems = refs[p:p + 2]
        ids = [pl.program_id(a) for a in range(len(grid))]
        first = ids[0] == 0
        last = ids[0] == grid[0] - 1
        for a in range(1, len(grid)):
            first = jnp.logical_and(first, ids[a] == 0)
            last = jnp.logical_and(last, ids[a] == grid[a] - 1)

        @pl.when(first)
        def _():
            rider["start"](rid_in, rid_out, send_sems, recv_sems)

        body(*own_in, *own_out, *own_sc)

        @pl.when(last)
        def _():
            rider["finish"](rid_in, rid_out, send_sems, recv_sems)

    aliases = {n_in + t: n_out + t for t in range(r_in)} if rider.get("in_place") else {}
    res = _pcall_comm(
        wrapped, name=name + "_" + rider["name"], grid=grid, in_specs=list(in_specs) + [ANY] * r_in,
        out_specs=list(out_specs) + [ANY] * r_out, out_shape=list(out_shape) + list(rider["out_shapes"]),
        scratch_shapes=list(scratch_shapes) + [pltpu.SemaphoreType.DMA((rider["n"],)), pltpu.SemaphoreType.DMA((rider["n"],))],
        input_output_aliases=aliases, compiler_params=_params(*(("arbitrary",) * len(grid))),
    )(*args, *rider["ins"])
    return res


def _dil_fwd(proj3, *, D, q_off, k_off, v_off, rider=None):
    B, S, C = proj3.shape
    n_it = S // WIN
    scale = HEAD_DIM ** -0.5
    grid, in_specs, out_spec = _dil_specs(B, S, D, C, (q_off, k_off, v_off))

    def body(q_ref, k_ref, v_ref, o_ref, l_ref):
        lo = lax.broadcasted_iota(jnp.int32, (WIN, LANES), 1) < HEAD_DIM
        for c, d in enumerate(DILS):
            two = S // d > WIN
            bias_all, bias_first = _dil_bias(two)

            def step(it, _, c=c, d=d, two=two, bias_all=bias_all, bias_first=bias_first):
                cur, prev, later = _dil_rows(it, d, S)
                q2 = _pair_stack(q_ref[cur, :] * scale, lo)
                if two:
                    k2 = jnp.concatenate([k_ref[prev, :], k_ref[cur, :]], axis=0).astype(BF16)
                    v2 = jnp.concatenate([v_ref[prev, :], v_ref[cur, :]], axis=0).astype(BF16)
                    bias = jnp.where(later, bias_all, bias_first)
                else:
                    k2, v2, bias = k_ref[cur, :].astype(BF16), v_ref[cur, :].astype(BF16), bias_all
                s2 = _dot_nt(q2, k2) + bias
                m2 = jnp.max(s2, axis=1, keepdims=True)
                p2 = jnp.exp(s2 - m2)
                den = jnp.sum(p2, axis=1, keepdims=True)
                oc = _pair_join(_dot_nn(p2.astype(BF16), v2) / den, lo)
                lc = _pair_bcast(m2 + jnp.log(den), lo)
                if c == 0:
                    o_ref[cur, :] = oc
                    l_ref[cur, :] = lc
                else:
                    l_old = l_ref[cur, :]
                    mx = jnp.maximum(l_old, lc)
                    e_old, e_new = jnp.exp(l_old - mx), jnp.exp(lc - mx)
                    tot = e_old + e_new
                    o_ref[cur, :] = (e_old * o_ref[cur, :] + e_new * oc) / tot
                    l_ref[cur, :] = mx + jnp.log(tot)
                return 0

            lax.fori_loop(0, n_it, step, 0, unroll=16)

    return _call_with_rider(
        rider, body, name="dil_fwd", grid=grid, in_specs=in_specs, out_specs=[out_spec, out_spec],
        out_shape=[jax.ShapeDtypeStruct((B, S, D), F32)] * 2, scratch_shapes=[], args=[proj3, proj3, proj3])


def _dil_bwd(proj3, o3, l3, do3, *, D, q_off, k_off, v_off, rider=None):
    B, S, C = proj3.shape
    n_it = S // WIN
    scale = HEAD_DIM ** -0.5
    grid, in_specs, out_spec = _dil_specs(B, S, D, C, (q_off, k_off, v_off))

    def body(q_ref, k_ref, v_ref, o_ref, l_ref, do_ref, dq_ref, dk_ref, dv_ref, dd_s, dq_s, dk_s, dv_s):
        lo = lax.broadcasted_iota(jnp.int32, (WIN, LANES), 1) < HEAD_DIM
        lo_s = lax.broadcasted_iota(jnp.int32, (S, LANES), 1) < HEAD_DIM
        prod = do_ref[...] * o_ref[...]
        d_lo = jnp.sum(jnp.where(lo_s, prod, 0.0), axis=1, keepdims=True)
        d_hi = jnp.sum(jnp.where(lo_s, 0.0, prod), axis=1, keepdims=True)
        dd_s[...] = jnp.where(lo_s, jnp.broadcast_to(d_lo, (S, LANES)), jnp.broadcast_to(d_hi, (S, LANES)))
        dq_s[...] = jnp.zeros_like(dq_s)
        dk_s[...] = jnp.zeros_like(dk_s)
        dv_s[...] = jnp.zeros_like(dv_s)
        for d in DILS:
            two = S // d > WIN
            bias_all, bias_first = _dil_bias(two)

            def step(it, _, d=d, two=two, bias_all=bias_all, bias_first=bias_first):
                cur, prev, later = _dil_rows(it, d, S)
                q2 = _pair_stack(q_ref[cur, :] * scale, lo)
                do2 = _pair_stack(do_ref[cur, :], lo)
                l2 = _pair_col(l_ref[cur, :])
                dd2 = _pair_col(dd_s[cur, :])
                if two:
                    k2 = jnp.concatenate([k_ref[prev, :], k_ref[cur, :]], axis=0).astype(BF16)
                    v2 = jnp.concatenate([v_ref[prev, :], v_ref[cur, :]], axis=0).astype(BF16)
                    bias = jnp.where(later, bias_all, bias_first)
                else:
                    k2, v2, bias = k_ref[cur, :].astype(BF16), v_ref[cur, :].astype(BF16), bias_all
                p2 = jnp.exp(_dot_nt(q2, k2) + bias - l2)
                ds2 = (p2 * (_dot_nt(do2, v2) - dd2)).astype(BF16)
                dq_s[cur, :] += _pair_join(_dot_nn(ds2, k2), lo) * scale
                dk2 = _dot_tn(ds2, q2)
                dv2 = _dot_tn(p2.astype(BF16), do2)
                if two:
                    dk_s[prev, :] += dk2[:WIN]
                    dv_s[prev, :] += dv2[:WIN]
                    dk_s[cur, :] += dk2[WIN:]
                    dv_s[cur, :] += dv2[WIN:]
                else:
                    dk_s[cur, :] += dk2
                    dv_s[cur, :] += dv2
                return 0

            lax.fori_loop(0, n_it, step, 0, unroll=16)
        dq_ref[...] = dq_s[...].astype(BF16)
        dk_ref[...] = dk_s[...].astype(BF16)
        dv_ref[...] = dv_s[...].astype(BF16)

    return _call_with_rider(
        rider, body, name="dil_bwd", grid=grid, in_specs=in_specs + [out_spec] * 3, out_specs=[out_spec] * 3,
        out_shape=[jax.ShapeDtypeStruct((B, S, D), BF16)] * 3, scratch_shapes=[pltpu.VMEM((S, LANES), F32)] * 4,
        args=[proj3, proj3, proj3, o3, l3, do3])


SWA_HB = 2 * SWA_GROUP


def _to_half(x, src, dst, lo):
    if src != dst:
        x = pltpu.roll(x, HEAD_DIM, 1)
    return jnp.where(lo if dst == 0 else jnp.logical_not(lo), x, 0.0)


def _swa_kv(g):
    return 2 * g // SWA_GROUP


SWA_STACKS = ((0, 1), (2, 3))


def _swa_stack(ref, gs, lo, dtype, rows=slice(None)):
    parts = []
    for g in gs:
        x = ref[rows, g * LANES:(g + 1) * LANES]
        parts += [_to_half(x, 0, _swa_kv(g), lo), _to_half(x, 1, _swa_kv(g), lo)]
    return jnp.concatenate(parts, axis=0).astype(dtype)


def _swa_unstack(y, gs, lo):
    out = []
    for t, g in enumerate(gs):
        even, odd = y[2 * t * WIN:(2 * t + 1) * WIN], y[(2 * t + 1) * WIN:(2 * t + 2) * WIN]
        out.append(_to_half(even, _swa_kv(g), 0, lo) + _to_half(odd, _swa_kv(g), 1, lo))
    return out


def _swa_cols(x, gs):
    cols = []
    for g in gs:
        cols += [jnp.broadcast_to(x[:, 2 * g:2 * g + 1], (WIN, 1)), jnp.broadcast_to(x[:, 2 * g + 1:2 * g + 2], (WIN, 1))]
    return jnp.concatenate(cols, axis=0)


SWA_UNROLL = 8


def _swa_seq_specs(B, S, D, q_off, k_off, v_off):
    qw = SWA_HB * HEAD_DIM
    assert q_off % qw == 0 and k_off % LANES == 0 and v_off % LANES == 0 and D % qw == 0
    seq = lambda width, off: pl.BlockSpec((None, S, width), lambda b, hh: (b, 0, off // width + hh))
    sink = pl.BlockSpec((None, 1, LANES), lambda b, hh: (hh, 0, 0))
    return (B, D // qw), seq, sink, qw


def _swa_rows(it):
    cur = pl.ds(pl.multiple_of(it * WIN, WIN), WIN)
    prev = pl.ds(pl.multiple_of(jnp.maximum(it - 1, 0) * WIN, WIN), WIN)
    return cur, prev, it > 0


def _swa_seq_fwd(proj3, sinks, *, D, q_off, k_off, v_off, rider=None):
    B, S, C = proj3.shape
    scale = HEAD_DIM ** -0.5
    grid, seq, sink, qw = _swa_seq_specs(B, S, D, q_off, k_off, v_off)
    nhb = D // qw

    def body(q_ref, k_ref, v_ref, sk_ref, o_ref, lse_ref):
        lo = lax.broadcasted_iota(jnp.int32, (WIN, LANES), 1) < HEAD_DIM
        lane = lax.broadcasted_iota(jnp.int32, (WIN, LANES), 1)
        sk = sk_ref[...]
        biases = [_dil_bias(True, 2 * len(gs)) for gs in SWA_STACKS]

        def step(it, _):
            cur, prev, later = _swa_rows(it)
            k2 = jnp.concatenate([k_ref[prev, :], k_ref[cur, :]], axis=0).astype(BF16)
            v2 = jnp.concatenate([v_ref[prev, :], v_ref[cur, :]], axis=0).astype(BF16)
            lse_acc = jnp.zeros((WIN, LANES), F32)
            for gs, (bias_all, bias_first) in zip(SWA_STACKS, biases):
                bias = jnp.where(later, bias_all, bias_first)
                qs = _swa_stack(q_ref, gs, lo, BF16, cur)
                sks = _swa_cols(sk, gs)
                s = _dot_nt(qs, k2) * scale + bias
                m = jnp.maximum(jnp.max(s, axis=1, keepdims=True), sks)
                p = jnp.exp(s - m)
                den = jnp.sum(p, axis=1, keepdims=True) + jnp.exp(sks - m)
                for g, grp in zip(gs, _swa_unstack(_dot_nn(p.astype(BF16), v2) / den, gs, lo)):
                    o_ref[cur, g * LANES:(g + 1) * LANES] = grp
                ls = m + jnp.log(den)
                for t, g in enumerate(gs):
                    lse_acc = jnp.where(lane == 2 * g, ls[2 * t * WIN:(2 * t + 1) * WIN], lse_acc)
                    lse_acc = jnp.where(lane == 2 * g + 1, ls[(2 * t + 1) * WIN:(2 * t + 2) * WIN], lse_acc)
            lse_ref[cur, :] = lse_acc
            return 0

        lax.fori_loop(0, S // WIN, step, 0, unroll=SWA_UNROLL)

    return _call_with_rider(
        rider, body, name="swa_fwd", grid=grid,
        in_specs=[seq(qw, q_off), seq(LANES, k_off), seq(LANES, v_off), sink],
        out_specs=[seq(qw, 0), seq(LANES, 0)],
        out_shape=[jax.ShapeDtypeStruct((B, S, D), F32), jax.ShapeDtypeStruct((B, S, nhb * LANES), F32)],
        scratch_shapes=[], args=[proj3, proj3, proj3, sinks])


def _swa_seq_bwd(proj3, o3, lse3, do3, sinks, *, D, q_off, k_off, v_off):
    B, S, C = proj3.shape
    scale = HEAD_DIM ** -0.5
    grid, seq, sink, qw = _swa_seq_specs(B, S, D, q_off, k_off, v_off)
    nhb = D // qw
    KV = D // SWA_GROUP

    def body(q_ref, k_ref, v_ref, o_ref, l_ref, do_ref, sk_ref, dq_ref, dk_ref, dv_ref, dsk_ref, dk_s, dv_s):
        lo = lax.broadcasted_iota(jnp.int32, (WIN, LANES), 1) < HEAD_DIM
        lane = lax.broadcasted_iota(jnp.int32, (1, LANES), 1)
        sk = sk_ref[...]
        biases = [_dil_bias(True, 2 * len(gs)) for gs in SWA_STACKS]
        dk_s[...] = jnp.zeros_like(dk_s)
        dv_s[...] = jnp.zeros_like(dv_s)

        dsk_ref[...] = jnp.zeros_like(dsk_ref)

        def step(it, _):
            cur, prev, later = _swa_rows(it)
            k2 = jnp.concatenate([k_ref[prev, :], k_ref[cur, :]], axis=0).astype(BF16)
            v2 = jnp.concatenate([v_ref[prev, :], v_ref[cur, :]], axis=0).astype(BF16)
            lse = l_ref[cur, :]
            dk2 = jnp.zeros((2 * WIN, LANES), F32)
            dv2 = jnp.zeros((2 * WIN, LANES), F32)
            dsk_acc = jnp.zeros((1, LANES), F32)
            for gs, (bias_all, bias_first) in zip(SWA_STACKS, biases):
                bias = jnp.where(later, bias_all, bias_first)
                qs = _swa_stack(q_ref, gs, lo, BF16, cur)
                dos = _swa_stack(do_ref, gs, lo, BF16, cur)
                dds = []
                for g in gs:
                    prod = do_ref[cur, g * LANES:(g + 1) * LANES] * o_ref[cur, g * LANES:(g + 1) * LANES]
                    dds += [jnp.sum(jnp.where(lo, prod, 0.0), axis=1, keepdims=True),
                            jnp.sum(jnp.where(lo, 0.0, prod), axis=1, keepdims=True)]
                dds = jnp.concatenate(dds, axis=0)
                ls = _swa_cols(lse, gs)
                ps = jnp.exp(_dot_nt(qs, k2) * scale + bias - ls)
                dss = (ps * (_dot_nt(dos, v2) - dds) * scale).astype(BF16)
                for g, grp in zip(gs, _swa_unstack(_dot_nn(dss, k2), gs, lo)):
                    dq_ref[cur, g * LANES:(g + 1) * LANES] = grp.astype(BF16)
                dk2 = dk2 + _dot_tn(dss, qs)
                dv2 = dv2 + _dot_tn(ps.astype(BF16), dos)
                dsks = jnp.exp(_swa_cols(sk, gs) - ls) * dds
                for t, g in enumerate(gs):
                    for u in range(2):
                        rows = slice((2 * t + u) * WIN, (2 * t + u + 1) * WIN)
                        dsk_acc = dsk_acc + jnp.where(lane == 2 * g + u, -jnp.sum(dsks[rows], axis=0, keepdims=True), 0.0)
            dk_s[prev, :] += dk2[:WIN]
            dv_s[prev, :] += dv2[:WIN]
            dk_s[cur, :] += dk2[WIN:]
            dv_s[cur, :] += dv2[WIN:]
            dsk_ref[...] += dsk_acc
            return 0

        lax.fori_loop(0, S // WIN, step, 0, unroll=SWA_UNROLL)
        dk_ref[...] = dk_s[...].astype(BF16)
        dv_ref[...] = dv_s[...].astype(BF16)

    return _pcall(
        body, name="swa_bwd", grid=grid,
        in_specs=[seq(qw, q_off), seq(LANES, k_off), seq(LANES, v_off), seq(qw, 0), seq(LANES, 0), seq(qw, 0), sink],
        out_specs=[seq(qw, 0), seq(LANES, 0), seq(LANES, 0), pl.BlockSpec((None, None, 1, LANES), lambda b, hh: (b, hh, 0, 0))],
        out_shape=[jax.ShapeDtypeStruct((B, S, D), BF16), jax.ShapeDtypeStruct((B, S, KV), BF16),
                   jax.ShapeDtypeStruct((B, S, KV), BF16), jax.ShapeDtypeStruct((B, nhb, 1, LANES), F32)],
        scratch_shapes=[pltpu.VMEM((S, LANES), F32), pltpu.VMEM((S, LANES), F32)],
        compiler_params=_params("parallel", "parallel"),
    )(proj3, proj3, proj3, o3, lse3, do3, sinks)


def _branch_fwd(ys, wb, proj, *, D, g_off, rider=None):
    T = proj.shape[0]
    tm, tn = _tile(T, 256), _tile(D, 512)
    n = len(ys)

    def body(*refs):
        y_refs, w_ref, g_refs, br_ref, mg_ref = refs[:n], refs[n], refs[n + 1:2 * n + 1], refs[2 * n + 1], refs[2 * n + 2]
        acc = None
        for k in range(n):
            br = _dot_nn(y_refs[k][...].astype(BF16), w_ref[k])
            br_ref[k] = br
            term = _sigmoid(g_refs[k][...]) * br
            acc = term if acc is None else acc + term
        mg_ref[...] = acc.astype(BF16)

    gate = lambda k: pl.BlockSpec((tm, tn), lambda i, j: (i, (g_off + k * D) // tn + j))
    return _call_with_rider(
        rider, body, name="branch_fwd", grid=(T // tm, D // tn),
        in_specs=[pl.BlockSpec((tm, D), lambda i, j: (i, 0))] * n + [pl.BlockSpec((n, D, tn), lambda i, j: (0, 0, j))]
        + [gate(k) for k in range(n)],
        out_specs=[pl.BlockSpec((n, tm, tn), lambda i, j: (0, i, j)), pl.BlockSpec((tm, tn), lambda i, j: (i, j))],
        out_shape=[jax.ShapeDtypeStruct((n, T, D), F32), jax.ShapeDtypeStruct((T, D), BF16)],
        scratch_shapes=[], args=[*ys, wb, *([proj] * n)])


def _branch_bwd(dmix, w_out, branch, proj, *, D, g_off):
    n, T, _ = branch.shape
    tm, tn = _tile(T, 512), _tile(D, 512)

    def body(dy_ref, w_ref, br_ref, *rest):
        g_refs, db_ref, dg_refs = rest[:n], rest[n], rest[n + 1:]
        dm = _dot_nt(dy_ref[...].astype(BF16), w_ref[...])
        for k in range(n):
            sg = _sigmoid(g_refs[k][...])
            db_ref[k] = (sg * dm).astype(BF16)
            dg_refs[k][...] = (dm * br_ref[k] * sg * (1.0 - sg)).astype(BF16)

    gate = lambda k: pl.BlockSpec((tm, tn), lambda i, j: (i, (g_off + k * D) // tn + j))
    blk = pl.BlockSpec((tm, tn), lambda i, j: (i, j))
    res = _pcall(
        body, name="branch_bwd", grid=(T // tm, D // tn),
        in_specs=[pl.BlockSpec((tm, D), lambda i, j: (i, 0)), pl.BlockSpec((tn, D), lambda i, j: (j, 0)),
                  pl.BlockSpec((n, tm, tn), lambda i, j: (0, i, j))] + [gate(k) for k in range(n)],
        out_specs=[pl.BlockSpec((n, tm, tn), lambda i, j: (0, i, j))] + [blk] * n,
        out_shape=[jax.ShapeDtypeStruct((n, T, D), BF16)] + [jax.ShapeDtypeStruct((T, D), BF16)] * n,
        compiler_params=_params("parallel", "parallel"),
    )(dmix, w_out, branch, *([proj] * n))
    return res[0], list(res[1:])


def _ln_bwd(dout, z, g):
    T, D = z.shape
    tm = _tile(T, 512)

    def body(do_ref, z_ref, g_ref, dz_ref, dg_ref, db_ref):
        z = z_ref[...]
        do = do_ref[...]
        mu = jnp.mean(z, axis=1, keepdims=True)
        zc = z - mu
        rstd = lax.rsqrt(jnp.mean(zc * zc, axis=1, keepdims=True) + LN_EPS)
        xhat = zc * rstd
        dxh = do * g_ref[...]
        dz_ref[...] = rstd * (dxh - jnp.mean(dxh, axis=1, keepdims=True) - xhat * jnp.mean(dxh * xhat, axis=1, keepdims=True))
        dg = jnp.sum(do * xhat, axis=0, keepdims=True)
        db = jnp.sum(do, axis=0, keepdims=True)
        first = pl.program_id(0) == 0

        @pl.when(first)
        def _():
            dg_ref[...] = dg
            db_ref[...] = db

        @pl.when(jnp.logical_not(first))
        def _():
            dg_ref[...] += dg
            db_ref[...] += db

    blk = pl.BlockSpec((tm, D), lambda i: (i, 0))
    vec = pl.BlockSpec((1, D), lambda i: (0, 0))
    return _pcall(
        body, name="ln_bwd", grid=(T // tm,), in_specs=[blk, blk, vec], out_specs=[blk, vec, vec],
        out_shape=[jax.ShapeDtypeStruct((T, D), F32), jax.ShapeDtypeStruct((1, D), F32), jax.ShapeDtypeStruct((1, D), F32)],
        compiler_params=_params("arbitrary"),
    )(dout, z, g)


def _ffn_in_fwd(x1, w_sh):
    T, D = x1.shape
    ns, _, nsh = w_sh.shape
    half = ns // 2
    Fh = half * nsh
    tm = _tile(T, 512)

    def body(x_ref, wa_ref, wb_ref, h1_ref, h3_ref, f_ref):
        xb = x_ref[...].astype(BF16)
        h1 = _dot_nn(xb, wa_ref[...])
        h3 = _dot_nn(xb, wb_ref[...])
        h1_ref[...] = h1
        h3_ref[...] = h3
        f_ref[...] = (h1 * _sigmoid(h1) * h3).astype(BF16)

    cols = pl.BlockSpec((tm, nsh), lambda j, i: (i, j))
    return _pcall(
        body, name="ffn_in_fwd", grid=(half, T // tm),
        in_specs=[pl.BlockSpec((tm, D), lambda j, i: (i, 0)), pl.BlockSpec((None, D, nsh), lambda j, i: (j, 0, 0)),
                  pl.BlockSpec((None, D, nsh), lambda j, i: (j + half, 0, 0))],
        out_specs=[cols, cols, cols],
        out_shape=[jax.ShapeDtypeStruct((T, Fh), F32), jax.ShapeDtypeStruct((T, Fh), F32), jax.ShapeDtypeStruct((T, Fh), BF16)],
        compiler_params=_params("parallel", "parallel"),
    )(x1, w_sh, w_sh)


def _ffn_out_bwd(dy, w_ffn_out, h1, h3):
    T, Fh = h1.shape
    D = w_ffn_out.shape[1]
    tm = _tile(T, 256)

    def body(dy_ref, w_ref, h1_ref, h3_ref, o_ref):
        d = _dot_nt(dy_ref[...].astype(BF16), w_ref[...])
        h1v = h1_ref[...]
        sg = _sigmoid(h1v)
        o_ref[:, :Fh] = (d * h3_ref[...] * sg * (1.0 + h1v * (1.0 - sg))).astype(BF16)
        o_ref[:, Fh:] = (d * h1v * sg).astype(BF16)

    blk = pl.BlockSpec((tm, Fh), lambda i: (i, 0))
    return _pcall(
        body, name="ffn_out_bwd", grid=(T // tm,),
        in_specs=[pl.BlockSpec((tm, D), lambda i: (i, 0)), pl.BlockSpec((Fh, D), lambda i: (0, 0)), blk, blk],
        out_specs=pl.BlockSpec((tm, 2 * Fh), lambda i: (i, 0)),
        out_shape=jax.ShapeDtypeStruct((T, 2 * Fh), BF16), compiler_params=_params("parallel"),
    )(dy, w_ffn_out, h1, h3)


def _loss_head(y, target):
    T, D = y.shape
    tm = _tile(T, 512)

    def body(y_ref, t_ref, dy_ref, l_ref):
        e = y_ref[...] - t_ref[...]
        dy_ref[...] = e * (1.0 / D)
        sq = e * e
        part = sq[:, 0:LANES]
        for c in range(1, D // LANES):
            part = part + sq[:, c * LANES:(c + 1) * LANES]
        part = jnp.sum(part, axis=0, keepdims=True) * (0.5 / D)
        first = pl.program_id(0) == 0

        @pl.when(first)
        def _():
            l_ref[...] = part

        @pl.when(jnp.logical_not(first))
        def _():
            l_ref[...] += part

    blk = pl.BlockSpec((tm, D), lambda i: (i, 0))
    return _pcall(
        body, name="loss_head", grid=(T // tm,), in_specs=[blk, blk],
        out_specs=[blk, pl.BlockSpec((1, LANES), lambda i: (0, 0))],
        out_shape=[jax.ShapeDtypeStruct((T, D), F32), jax.ShapeDtypeStruct((1, LANES), F32)],
        compiler_params=_params("arbitrary"),
    )(y, target)


def _as_rows(a):
    return a.reshape(-1, a.shape[-1])


def _adamw(w, g, m, v, rider=None):
    w2, g2, m2, v2 = (_as_rows(t) for t in (w, g, m, v))
    R, Cc = w2.shape
    cap = max(SUBLANES, min(512, (256 * 1024) // Cc))
    tm = R if (R <= cap or R % SUBLANES) else max(t for t in range(SUBLANES, cap + 1, SUBLANES) if R % t == 0)
    c1 = 1.0 - ADAM_B1 ** ADAM_STEP
    c2 = 1.0 - ADAM_B2 ** ADAM_STEP

    def body(w_ref, g_ref, m_ref, v_ref, d_ref, nm_ref, nv_ref):
        gg = g_ref[...]
        nm = ADAM_B1 * m_ref[...] + (1.0 - ADAM_B1) * gg
        nv = ADAM_B2 * v_ref[...] + (1.0 - ADAM_B2) * (gg * gg)
        d_ref[...] = (-ADAM_LR) * ((nm / c1) / (jnp.sqrt(nv / c2) + ADAM_EPS) + ADAM_WD * w_ref[...])
        nm_ref[...] = nm
        nv_ref[...] = nv

    blk = pl.BlockSpec((tm, Cc), lambda i: (i, 0))
    res = _call_with_rider(
        rider, body, name="adamw", grid=(R // tm,), in_specs=[blk] * 4, out_specs=[blk] * 3,
        out_shape=[jax.ShapeDtypeStruct((R, Cc), F32)] * 3, scratch_shapes=[], args=[w2, g2, m2, v2])
    return tuple(t.reshape(w.shape) for t in res[:3]) + tuple(res[3:])


def _where_am_i():
    x, y, c = lax.axis_index("x"), lax.axis_index("y"), lax.axis_index("c")
    chips = [(1 - x, y), (x, 1 - y), (1 - x, 1 - y)]
    return x, y, c, chips


def _remote(src, dst, send_sems, recv_sems, k, to):
    return pltpu.make_async_remote_copy(src_ref=src, dst_ref=dst, send_sem=send_sems.at[k], recv_sem=recv_sems.at[k],
                                        device_id=to, device_id_type=MESH)


def _comm_call(body, name, ins, out_shapes, n_remote, n_local):
    return _pcall_comm(
        body, name=name, in_specs=[ANY] * len(ins), out_specs=[ANY] * len(out_shapes), out_shape=out_shapes,
        scratch_shapes=[pltpu.SemaphoreType.DMA((n_remote,)), pltpu.SemaphoreType.DMA((n_remote,)),
                        pltpu.SemaphoreType.DMA((max(n_local, 1),))],
    )(*ins)


def _gather_weights(shards):
    n = len(shards)

    def body(*refs):
        ins, outs = refs[:n], refs[n:2 * n]
        send_sems, recv_sems, local_sems = refs[2 * n:]
        x, y, c, chips = _where_am_i()
        s = 2 * x + y
        sib = (x, y, 1 - c)
        first = []
        for t in range(n):
            for j, (cx, cy) in enumerate(chips):
                first.append(_remote(ins[t].at[:, c], outs[t].at[:, s, c], send_sems, recv_sems, 6 * t + j, (cx, cy, c)))
        for cp in first:
            cp.start()
        passed = []
        for j, (cx, cy) in enumerate(chips):
            sj = 2 * cx + cy
            for t in range(n):
                land = outs[t].at[:, sj, c]
                _remote(land, land, send_sems, recv_sems, 6 * t + j, (cx, cy, c)).wait_recv()
                fw = _remote(land, land, send_sems, recv_sems, 6 * t + 3 + j, sib)
                fw.start()
                passed.append(fw)
        for j, (cx, cy) in enumerate(chips):
            sj = 2 * cx + cy
            for t in range(n):
                land = outs[t].at[:, sj, 1 - c]
                _remote(land, land, send_sems, recv_sems, 6 * t + 3 + j, sib).wait_recv()
        for cp in first + passed:
            cp.wait_send()

    out_shapes = [jax.ShapeDtypeStruct((t.shape[0], N_CHIPS) + t.shape[1:], t.dtype) for t in shards]
    got = _comm_call(body, "gather_weights", shards, out_shapes, 6 * n, 0)
    s = 2 * lax.axis_index("x") + lax.axis_index("y")
    return [lax.dynamic_update_slice(g, t[:, None], (0, s, 0, 0, 0)) for g, t in zip(got, shards)]


def _gather_rider(shards):
    n = len(shards)

    def copies(ins, outs, send_sems, recv_sems):
        x, y, c, chips = _where_am_i()
        s = 2 * x + y
        return [_remote(ins[t].at[:, c], outs[t].at[:, s, c], send_sems, recv_sems, 3 * t + j, (cx, cy, c))
                for t in range(n) for j, (cx, cy) in enumerate(chips)]

    def start(ins, outs, send_sems, recv_sems):
        for cp in copies(ins, outs, send_sems, recv_sems):
            cp.start()

    def finish(ins, outs, send_sems, recv_sems):
        x, y, c, chips = _where_am_i()
        for t in range(n):
            for j, (cx, cy) in enumerate(chips):
                land = outs[t].at[:, 2 * cx + cy, c]
                _remote(land, land, send_sems, recv_sems, 3 * t + j, (cx, cy, c)).wait_recv()
        for cp in copies(ins, outs, send_sems, recv_sems):
            cp.wait_send()

    out_shapes = [jax.ShapeDtypeStruct((t.shape[0], N_CHIPS) + t.shape[1:], t.dtype) for t in shards]
    return dict(name="gather", ins=list(shards), out_shapes=out_shapes, n=3 * n, start=start, finish=finish)


def _gather_forward(landed, shards):
    n = len(landed)

    def body(*refs):
        outs = refs[n:2 * n]
        send_sems, recv_sems, _ = refs[2 * n:]
        x, y, c, chips = _where_am_i()
        sib = (x, y, 1 - c)
        cps = []
        for t in range(n):
            for j, (cx, cy) in enumerate(chips):
                land = outs[t].at[:, 2 * cx + cy, c]
                cps.append(_remote(land, land, send_sems, recv_sems, 3 * t + j, sib))
        for cp in cps:
            cp.start()
        for t in range(n):
            for j, (cx, cy) in enumerate(chips):
                land = outs[t].at[:, 2 * cx + cy, 1 - c]
                _remote(land, land, send_sems, recv_sems, 3 * t + j, sib).wait_recv()
        for cp in cps:
            cp.wait_send()

    got = _pcall_comm(
        body, name="gather_forward", in_specs=[ANY] * n, out_specs=[ANY] * n,
        out_shape=[jax.ShapeDtypeStruct(t.shape, t.dtype) for t in landed], input_output_aliases={t: t for t in range(n)},
        scratch_shapes=[pltpu.SemaphoreType.DMA((3 * n,)), pltpu.SemaphoreType.DMA((3 * n,)), pltpu.SemaphoreType.DMA((1,))],
    )(*landed)
    s = 2 * lax.axis_index("x") + lax.axis_index("y")
    return [lax.dynamic_update_slice(g, t[:, None], (0, s, 0, 0, 0)) for g, t in zip(got, shards)]


def _forward_rider(landed):
    n = len(landed)

    def copies(outs, send_sems, recv_sems):
        x, y, c, chips = _where_am_i()
        cps = []
        for t in range(n):
            for j, (cx, cy) in enumerate(chips):
                land = outs[t].at[:, 2 * cx + cy, c]
                cps.append(_remote(land, land, send_sems, recv_sems, 3 * t + j, (x, y, 1 - c)))
        return cps

    def start(ins, outs, send_sems, recv_sems):
        for cp in copies(outs, send_sems, recv_sems):
            cp.start()

    def finish(ins, outs, send_sems, recv_sems):
        x, y, c, chips = _where_am_i()
        for t in range(n):
            for j, (cx, cy) in enumerate(chips):
                land = outs[t].at[:, 2 * cx + cy, 1 - c]
                _remote(land, land, send_sems, recv_sems, 3 * t + j, (x, y, 1 - c)).wait_recv()
        for cp in copies(outs, send_sems, recv_sems):
            cp.wait_send()

    out_shapes = [jax.ShapeDtypeStruct(t.shape, t.dtype) for t in landed]
    return dict(name="forward", ins=list(landed), out_shapes=out_shapes, n=3 * n, start=start, finish=finish, in_place=True)


def _place_own_shard(got, shards):
    s = 2 * lax.axis_index("x") + lax.axis_index("y")
    return [lax.dynamic_update_slice(g, t[:, None], (0, s, 0, 0, 0)) for g, t in zip(got, shards)]


def _gather_small(v):
    def body(v_ref, out_ref, send_sems, recv_sems, local_sems):
        x, y, c, chips = _where_am_i()
        s = 2 * x + y
        mine = pltpu.make_async_copy(v_ref, out_ref.at[s], local_sems.at[0])
        mine.start()
        sends = [_remote(v_ref, out_ref.at[s], send_sems, recv_sems, j, (cx, cy, c)) for j, (cx, cy) in enumerate(chips)]
        for cp in sends:
            cp.start()
        for j, (cx, cy) in enumerate(chips):
            land = out_ref.at[2 * cx + cy]
            _remote(land, land, send_sems, recv_sems, j, (cx, cy, c)).wait_recv()
        for cp in sends:
            cp.wait_send()
        mine.wait()

    return _comm_call(body, "gather_small", [v], [jax.ShapeDtypeStruct((N_CHIPS,) + v.shape, v.dtype)], 3, 1)[0]


def _swap_sibling_halves(grads):
    n = len(grads)

    def body(*refs):
        ins, outs = refs[:n], refs[n:2 * n]
        send_sems, recv_sems, _ = refs[2 * n:]
        x, y, c, _chips = _where_am_i()
        sib = (x, y, 1 - c)
        cps = [_remote(ins[t].at[:, :, 1 - c], outs[t], send_sems, recv_sems, t, sib) for t in range(n)]
        for cp in cps:
            cp.start()
        for cp in cps:
            cp.wait()

    out_shapes = [jax.ShapeDtypeStruct(g.shape[:2] + g.shape[3:], g.dtype) for g in grads]
    return _comm_call(body, "grad_swap_halves", grads, out_shapes, n, 0)


def _exchange_chips(parts):
    n = len(parts)

    def body(*refs):
        ins, outs = refs[:n], refs[n:2 * n]
        send_sems, recv_sems, _ = refs[2 * n:]
        x, y, c, chips = _where_am_i()
        cps = []
        for t in range(n):
            for j, (cx, cy) in enumerate(chips):
                cps.append(_remote(ins[t].at[:, 2 * cx + cy], outs[t].at[j], send_sems, recv_sems, 3 * t + j, (cx, cy, c)))
        for cp in cps:
            cp.start()
        for cp in cps:
            cp.wait()

    out_shapes = [jax.ShapeDtypeStruct((3, p.shape[0]) + p.shape[2:], p.dtype) for p in parts]
    return _comm_call(body, "grad_exchange_chips", parts, out_shapes, 3 * n, 0)


def _exchange_rider(parts):
    n = len(parts)

    def copies(ins, outs, send_sems, recv_sems):
        x, y, c, chips = _where_am_i()
        return [_remote(ins[t].at[:, 2 * cx + cy], outs[t].at[j], send_sems, recv_sems, 3 * t + j, (cx, cy, c))
                for t in range(n) for j, (cx, cy) in enumerate(chips)]

    def start(ins, outs, send_sems, recv_sems):
        for cp in copies(ins, outs, send_sems, recv_sems):
            cp.start()

    def finish(ins, outs, send_sems, recv_sems):
        for cp in copies(ins, outs, send_sems, recv_sems):
            cp.wait()

    out_shapes = [jax.ShapeDtypeStruct((3, p.shape[0]) + p.shape[2:], p.dtype) for p in parts]
    return dict(name="exchange", ins=list(parts), out_shapes=out_shapes, n=3 * n, start=start, finish=finish)


def _join_sibling_halves(halves):
    n = len(halves)

    def body(*refs):
        ins, outs = refs[:n], refs[n:2 * n]
        send_sems, recv_sems, local_sems = refs[2 * n:]
        x, y, c, _chips = _where_am_i()
        sib = (x, y, 1 - c)
        cps = [_remote(ins[t], outs[t].at[:, c], send_sems, recv_sems, t, sib) for t in range(n)]
        for cp in cps:
            cp.start()
        for t in range(n):
            land = outs[t].at[:, 1 - c]
            _remote(land, land, send_sems, recv_sems, t, sib).wait_recv()
        for cp in cps:
            cp.wait_send()

    out_shapes = [jax.ShapeDtypeStruct((h.shape[0], 2) + h.shape[1:], h.dtype) for h in halves]
    got = _comm_call(body, "grad_join_halves", halves, out_shapes, n, 0)
    c = lax.axis_index("c")
    return [lax.dynamic_update_slice(g, h[:, None], (0, c, 0, 0)) for g, h in zip(got, halves)]


def _join_rider(halves):
    n = len(halves)

    def copies(ins, outs, send_sems, recv_sems):
        x, y, c, _chips = _where_am_i()
        return [_remote(ins[t], outs[t].at[:, c], send_sems, recv_sems, t, (x, y, 1 - c)) for t in range(n)]

    def start(ins, outs, send_sems, recv_sems):
        for cp in copies(ins, outs, send_sems, recv_sems):
            cp.start()

    def finish(ins, outs, send_sems, recv_sems):
        x, y, c, _chips = _where_am_i()
        for t in range(n):
            land = outs[t].at[:, 1 - c]
            _remote(land, land, send_sems, recv_sems, t, (x, y, 1 - c)).wait_recv()
        for cp in copies(ins, outs, send_sems, recv_sems):
            cp.wait_send()

    out_shapes = [jax.ShapeDtypeStruct((h.shape[0], 2) + h.shape[1:], h.dtype) for h in halves]
    return dict(name="join", ins=list(halves), out_shapes=out_shapes, n=n, start=start, finish=finish)


def _place_own_half(got, halves):
    c = lax.axis_index("c")
    return [lax.dynamic_update_slice(g, h[:, None], (0, c, 0, 0)) for g, h in zip(got, halves)]


def _small_exchange_rider(v):
    def copies(ins, outs, send_sems, recv_sems):
        x, y, c, chips = _where_am_i()
        return [_remote(ins[0], outs[0].at[j], send_sems, recv_sems, j, (cx, cy, c)) for j, (cx, cy) in enumerate(chips)]

    def start(ins, outs, send_sems, recv_sems):
        for cp in copies(ins, outs, send_sems, recv_sems):
            cp.start()

    def finish(ins, outs, send_sems, recv_sems):
        for cp in copies(ins, outs, send_sems, recv_sems):
            cp.wait()

    return dict(name="small_exchange", ins=[v], out_shapes=[jax.ShapeDtypeStruct((3,) + v.shape, v.dtype)], n=3,
                start=start, finish=finish)


def _swap_small(v):
    def body(v_ref, out_ref, send_sems, recv_sems, _):
        x, y, c, _chips = _where_am_i()
        cp = _remote(v_ref, out_ref, send_sems, recv_sems, 0, (x, y, 1 - c))
        cp.start()
        cp.wait()

    return _comm_call(body, "small_swap", [v], [jax.ShapeDtypeStruct(v.shape, v.dtype)], 1, 0)[0]


def _sum_rows(name, terms, out_dtypes):
    R, Cc = terms[0].shape
    tm = R if R <= 256 else max(t for t in range(16, 257, 16) if R % t == 0)
    n = len(terms)

    def body(*refs):
        acc = refs[0][...].astype(F32)
        for r in refs[1:n]:
            acc = acc + r[...].astype(F32)
        for o in refs[n:]:
            o[...] = acc.astype(o.dtype)

    blk = pl.BlockSpec((tm, Cc), lambda i: (i, 0))
    return _pcall(
        body, name=name, grid=(R // tm,), in_specs=[blk] * n, out_specs=[blk] * len(out_dtypes),
        out_shape=[jax.ShapeDtypeStruct((R, Cc), d) for d in out_dtypes], compiler_params=_params("parallel"),
    )(*terms)


def _pair_sum(g5, r1, core, shard):
    A4, _, Rh, Cc = g5.shape
    A = A4 // N_CHIPS
    tr = Rh if Rh <= 256 else max(t for t in range(16, 257, 16) if Rh % t == 0)

    def body(core_ref, shard_ref, g_ref, r_ref, qb_ref, qf_ref):
        q = g_ref[...] + r_ref[...]
        qb_ref[...] = q.astype(BF16)

        @pl.when(pl.program_id(2) == shard_ref[0])
        def _():
            qf_ref[...] = q

    grid_spec = pltpu.PrefetchScalarGridSpec(
        num_scalar_prefetch=2, grid=(A, Rh // tr, N_CHIPS),
        in_specs=[pl.BlockSpec((None, None, tr, Cc), lambda a, r, sh, core, shard: (a * N_CHIPS + sh, core[0], r, 0)),
                  pl.BlockSpec((None, tr, Cc), lambda a, r, sh, core, shard: (a * N_CHIPS + sh, r, 0))],
        out_specs=[pl.BlockSpec((None, tr, Cc), lambda a, r, sh, core, shard: (a * N_CHIPS + sh, r, 0)),
                   pl.BlockSpec((None, tr, Cc), lambda a, r, sh, core, shard: (a, r, 0))],
    )
    return _pcall(
        body, name="grad_pair_sum", grid_spec=grid_spec,
        out_shape=[jax.ShapeDtypeStruct((A4, Rh, Cc), BF16), jax.ShapeDtypeStruct((A, Rh, Cc), F32)],
        compiler_params=_params("parallel", "parallel", "arbitrary"),
    )(core, shard, g5, r1)


def _swap_rider(grads):
    n = len(grads)

    def copies(ins, outs, send_sems, recv_sems):
        x, y, c, _chips = _where_am_i()
        return [_remote(ins[t].at[:, :, 1 - c], outs[t], send_sems, recv_sems, t, (x, y, 1 - c)) for t in range(n)]

    def start(ins, outs, send_sems, recv_sems):
        for cp in copies(ins, outs, send_sems, recv_sems):
            cp.start()

    def finish(ins, outs, send_sems, recv_sems):
        for cp in copies(ins, outs, send_sems, recv_sems):
            cp.wait()

    out_shapes = [jax.ShapeDtypeStruct(g.shape[:2] + g.shape[3:], g.dtype) for g in grads]
    return dict(name="swap", ins=list(grads), out_shapes=out_shapes, n=n, start=start, finish=finish)


def _reduce_chip(grads, r1, core, shard):
    qb, qf = [], []
    for g, r in zip(grads, r1):
        A, _, _, Rh, Cc = g.shape
        b, f = _pair_sum(g.reshape(A * N_CHIPS, 2, Rh, Cc), r.reshape(A * N_CHIPS, Rh, Cc), core, shard)
        qb.append(b.reshape(A, N_CHIPS, Rh, Cc))
        qf.append(f)
    return qb, qf


def _reduce_finish(qf, r2):
    return _as_shards(_join_sibling_halves(_chip_sums(qf, r2)))


def _chip_sums(qf, r2):
    halves = []
    for f, r in zip(qf, r2):
        A, Rh, Cc = f.shape
        terms = [f.reshape(A * Rh, Cc)] + [r[j].reshape(A * Rh, Cc) for j in range(3)]
        halves.append(_sum_rows("grad_chip_sum", terms, [F32])[0].reshape(A, Rh, Cc))
    return halves


def _as_shards(full):
    return [t.reshape(t.shape[0], 2 * t.shape[2], t.shape[3]) for t in full]


def _small_pair(v):
    return _sum_rows("small_pair_sum", [v, _swap_small(v)], [F32])[0]


def _small_chip_sum(pair, others):
    x, y = lax.axis_index("x"), lax.axis_index("y")
    s = 2 * x + y
    stack = jnp.concatenate([pair[None], others], axis=0)
    src = jnp.stack([s, s ^ 2, s ^ 1, s ^ 3])
    order = jnp.argsort(src)
    terms = [lax.dynamic_index_in_dim(stack, order[k], 0, keepdims=False) for k in range(N_CHIPS)]
    return _sum_rows("small_chip_sum", terms, [F32])[0]


def _block_diag(w):
    nb, bw, _ = w.shape
    per = LANES // bw
    w = w.reshape(nb // per, per, bw, bw)
    eye = jnp.eye(per, dtype=w.dtype)
    bd = jnp.einsum("tpij,pq->tpiqj", w, eye).reshape(nb // per, LANES, LANES)
    return bd.astype(BF16)


def _block_diag_grad(g, bw):
    nt = g.shape[0]
    per = LANES // bw
    g = g.reshape(nt, per, bw, per, bw)
    return jnp.stack([g[:, p, :, p, :] for p in range(per)], axis=1).reshape(nt * per, bw, bw)


def _split5(w):
    R, Cc = w.shape[-2:]
    return w.reshape(-1, 2, R // 2, Cc)


def kernel(x, w_in, conv_w, conv_b, w_rg, b_rg, w_ig, b_ig, lru_lambda, sinks, w_branch, w_out, ln1_g, ln1_b, w_ffn_in, w_ffn_out, ln2_g, ln2_b, loss_target, m_w_in, m_conv_w, m_conv_b, m_w_rg, m_b_rg, m_w_ig, m_b_ig, m_lru_lambda, m_sinks, m_w_branch, m_w_out, m_ln1_g, m_ln1_b, m_w_ffn_in, m_w_ffn_out, m_ln2_g, m_ln2_b, v_w_in, v_conv_w, v_conv_b, v_w_rg, v_b_rg, v_w_ig, v_b_ig, v_lru_lambda, v_sinks, v_w_branch, v_w_out, v_ln1_g, v_ln1_b, v_w_ffn_in, v_w_ffn_out, v_ln2_g, v_ln2_b):
    B, S, D = x.shape
    T = B * S
    L = w_in.shape[0]
    H = D // HEAD_DIM
    KVB = D // SWA_GROUP
    FH = w_ffn_out.shape[1] * N_CHIPS
    C = w_in.shape[2] * N_CHIPS
    alpha = (2.0 * L) ** 0.25
    off = {}
    pos = 0
    for nm, wd in (("lx", D), ("lg", D), ("qb", D), ("kb", KVB), ("vb", KVB), ("qc", D), ("kc", D), ("vc", D), ("gt", 3 * D)):
        off[nm] = pos
        pos += wd
    assert pos == C
    cx, cy, cc = lax.axis_index("x"), lax.axis_index("y"), lax.axis_index("c")
    shard = (2 * cx + cy).astype(jnp.int32)
    core_a = cc.astype(jnp.int32).reshape(1)
    shard_a = shard.reshape(1)

    def shard_views(l):
        return [_split5(w_in[l].astype(BF16)), _split5(w_branch[l].astype(BF16)), _split5(w_out[l].astype(BF16)),
                _split5(w_ffn_in[l].astype(BF16)), _split5(w_ffn_out[l].astype(BF16))]

    def as_weights(g):
        return dict(
            w_in=g[0].reshape(N_CHIPS, D, C // N_CHIPS),
            w_branch=g[1].reshape(3, D, D),
            w_out=g[2].reshape(D, D),
            w_ffn_in=g[3].reshape(N_CHIPS, D, 2 * FH // N_CHIPS),
            w_ffn_out=g[4].reshape(FH, D),
        )

    first_views = shard_views(0)
    w_in0 = _gather_weights(first_views[:1])
    full = [dict(w_in=w_in0[0].reshape(N_CHIPS, D, C // N_CHIPS))]
    cw_all = _gather_small(conv_w.reshape(L * CONV_WIDTH, D // N_CHIPS))
    conv_w_full = jnp.transpose(cw_all, (1, 0, 2)).reshape(L, CONV_WIDTH, D)

    def layer_params(l):
        return dict(conv_w=conv_w_full[l], conv_b=conv_b[l][None], w_rg_bd=_block_diag(w_rg[l]), b_rg=b_rg[l][None],
                    w_ig_bd=_block_diag(w_ig[l]), b_ig=b_ig[l][None], lam=lru_lambda[l][None])

    def sink_rows(l, hb):
        sk = sinks[l].reshape(H // hb, 1, hb)
        return jnp.pad(sk, ((0, 0), (0, 0), (0, LANES - hb)))

    hb_b = SWA_HB

    saved = []
    xin = x.reshape(T, D)
    for l in range(L):
        fw, lp = full[l], layer_params(l)
        nxt = shard_views(l + 1) if l + 1 < L else None
        own, ahead = {}, {}
        if l == 0:
            own = {"lru": (3,), "swa": (1, 2, 4)}
            ahead = {"proj": (0,), "dil": (1, 2, 3, 4)} if nxt is not None else {}
        elif nxt is not None:
            ahead = {"lru": (3,), "swa": (0,), "dil": (1, 2, 4)}
        landed_own, landed_next = {}, {}

        def carried(host):
            idx_own, idx_next = own.get(host, ()), ahead.get(host, ())
            views = [first_views[t] for t in idx_own] + [nxt[t] for t in idx_next]
            if not views:
                return None, lambda bufs: None

            def file(bufs):
                for t, buf in zip(idx_own, bufs[:len(idx_own)]):
                    landed_own[t] = buf
                for t, buf in zip(idx_next, bufs[len(idx_own):]):
                    landed_next[t] = buf
            return _gather_rider(views), file

        rider, file = carried("proj")
        proj_kw = dict(mode="nn", name="mm_proj", tm=512, n_outer=True)
        if rider is None:
            proj = _matmul(xin, fw["w_in"], **proj_kw)
        else:
            proj, bufs = _matmul(xin, fw["w_in"], rider=rider, **proj_kw)
            file(bufs)
        proj3 = proj.reshape(B, S, C)
        rider, file = carried("lru")
        res = _lru_fwd(proj3, lp, D=D, x_off=off["lx"], g_off=off["lg"], rider=rider)
        h3, ya3 = res[0], res[1]
        file(res[2:])
        skr = sink_rows(l, hb_b)
        rider, file = carried("swa")
        res = _swa_seq_fwd(proj3, skr, D=D, q_off=off["qb"], k_off=off["kb"], v_off=off["vb"], rider=rider)
        yb3, lse_b = res[0], res[1]
        file(res[2:])
        if l == 0:
            rest = sorted(landed_own)
            got = _gather_forward([landed_own[t] for t in rest], [first_views[t] for t in rest])
            fw = as_weights(w_in0 + got)
            full[0] = fw
        rider, file = carried("dil")
        res = _dil_fwd(proj3, D=D, q_off=off["qc"], k_off=off["kc"], v_off=off["vc"], rider=rider)
        yc3, lse_c = res[0], res[1]
        file(res[2:])
        ya, yb, yc = ya3.reshape(T, D), yb3.reshape(T, D), yc3.reshape(T, D)
        if nxt is not None:
            landed = [landed_next[t] for t in range(len(nxt))]
            res = _branch_fwd([ya, yb, yc], fw["w_branch"], proj, D=D, g_off=off["gt"], rider=_forward_rider(landed))
            branch, merged = res[0], res[1]
            full.append(as_weights(_place_own_shard(res[2:], nxt)))
        else:
            branch, merged = _branch_fwd([ya, yb, yc], fw["w_branch"], proj, D=D, g_off=off["gt"])
        z1, x1 = _matmul(merged, fw["w_out"], mode="nn", name="mm_out_ln", tn=1024, resid=xin, rs=alpha,
                         ln=(ln1_g[l][None], ln1_b[l][None]))
        ffn_h1, ffn_h3, f = _ffn_in_fwd(x1, fw["w_ffn_in"])
        z2, x2 = _matmul(f, fw["w_ffn_out"], mode="nn", name="mm_ffn_out_ln", tn=1024, tk=4096, resid=x1, rs=alpha,
                         ln=(ln2_g[l][None], ln2_b[l][None]))
        saved.append(dict(x=xin, proj=proj, h3=h3, ya=ya, yb=yb, lse_b=lse_b, yc=yc, lse_c=lse_c, branch=branch,
                          merged=merged, z1=z1, x1=x1, ffn_h1=ffn_h1, ffn_h3=ffn_h3, f=f, z2=z2, skr=skr))
        xin = x2

    dx, loss_rows = _loss_head(xin, loss_target.reshape(T, D))
    loss = lax.psum(jnp.sum(loss_rows), ("x", "y", "c"))

    big = {k: [None] * L for k in ("w_in", "w_branch", "w_out", "w_ffn_in", "w_ffn_out")}
    small = [None] * L

    def store_reduced(l, red):
        big["w_in"][l] = red[0].reshape(D, C // N_CHIPS)
        big["w_branch"][l] = red[1].reshape(3, D // N_CHIPS, D)
        big["w_out"][l] = red[2].reshape(D // N_CHIPS, D)
        big["w_ffn_in"][l] = red[3].reshape(D, 2 * FH // N_CHIPS)
        big["w_ffn_out"][l] = red[4].reshape(FH // N_CHIPS, D)

    above = None
    pending = None
    for l in reversed(range(L)):
        fw, lp, sv = full[l], layer_params(l), saved[l]
        dz2, dg2, db2 = _ln_bwd(dx, sv["z2"], ln2_g[l][None])
        g_ffn_out = _matmul(sv["f"], dz2, mode="tn", name="mm_dffn_out_w", tm=1408, tn=1024, tk=1024)
        dhh = _ffn_out_bwd(dz2, fw["w_ffn_out"], sv["ffn_h1"], sv["ffn_h3"])
        dx1 = _matmul(dhh, fw["w_ffn_in"], mode="nt", name="mm_dffn_in_x", tm=1024, tn=1024, resid=dz2, rs=alpha)
        g_ffn_in = _matmul(sv["x1"], dhh, mode="tn", name="mm_dffn_in_w", tm=1024, tk=1024, out_shards=N_CHIPS)
        dz1, dg1, db1 = _ln_bwd(dx1, sv["z1"], ln1_g[l][None])
        g_out = _matmul(sv["merged"], dz1, mode="tn", name="mm_dout_w", tm=1024, tn=1024, tk=1024)
        dbranch, dgates = _branch_bwd(dz1, fw["w_out"], sv["branch"], sv["proj"], D=D, g_off=off["gt"])
        ys = [sv["ya"], sv["yb"], sv["yc"]]
        dys, g_branch = [], []
        for n in range(3):
            dys.append(_matmul(dbranch, fw["w_branch"][n], mode="nt", name="mm_dbranch_x", tn=1024, tk=1024, a_pick=n))
            g_branch.append(_matmul(ys[n], dbranch, mode="tn", name="mm_dbranch_w", tm=1024, tn=1024, tk=1024, b_pick=n))
        proj3 = sv["proj"].reshape(B, S, C)
        r3 = lambda t: t.reshape(B, S, t.shape[-1])
        lru = _lru_bwd(proj3, sv["h3"], r3(dys[0]), lp, D=D, x_off=off["lx"], g_off=off["lg"],
                       rider=None if above is None else _swap_rider(above[1]))
        if above is not None:
            pending = (above[0],) + _reduce_chip(above[1], lru[9:], core_a, shard_a)
        dxr, dgate = lru[0], lru[1]
        dqb, dkb, dvb, dsk = _swa_seq_bwd(proj3, r3(sv["yb"]), sv["lse_b"], r3(dys[1]), sv["skr"], D=D, q_off=off["qb"],
                                      k_off=off["kb"], v_off=off["vb"])
        dil_kw = dict(D=D, q_off=off["qc"], k_off=off["kc"], v_off=off["vc"])
        if pending is None:
            acc = _dil_bwd(proj3, r3(sv["yc"]), sv["lse_c"], r3(dys[2]), **dil_kw)
        else:
            res = _dil_bwd(proj3, r3(sv["yc"]), sv["lse_c"], r3(dys[2]), rider=_exchange_rider(pending[1]), **dil_kw)
            acc = res[:3]
            halves = _chip_sums(pending[2], res[3:])
        f2 = lambda t: t.reshape(T, t.shape[-1]).astype(BF16)
        dproj = jnp.concatenate([f2(dxr), f2(dgate), f2(dqb), f2(dkb), f2(dvb), f2(acc[0]), f2(acc[1]), f2(acc[2])] + dgates, axis=1)
        dx_kw = dict(mode="nt", name="mm_dproj_x", tm=1024, tn=1024, resid=dz1, rs=alpha)
        if pending is None:
            dx = _matmul(dproj, fw["w_in"], **dx_kw)
        else:
            dx, got = _matmul(dproj, fw["w_in"], rider=_join_rider(halves), **dx_kw)
            store_reduced(pending[0], _as_shards(_place_own_half(got, halves)))
        g_in = _matmul(sv["x"], dproj, mode="tn", name="mm_dproj_w", tm=512, tk=1024, out_shards=N_CHIPS)

        g5 = [g_in.reshape(1, N_CHIPS, 2, D // 2, C // N_CHIPS),
              jnp.stack(g_branch).reshape(3, N_CHIPS, 2, D // N_CHIPS // 2, D),
              g_out.reshape(1, N_CHIPS, 2, D // N_CHIPS // 2, D),
              g_ffn_in.reshape(1, N_CHIPS, 2, D // 2, 2 * FH // N_CHIPS),
              g_ffn_out.reshape(1, N_CHIPS, 2, FH // N_CHIPS // 2, D)]
        above = (l, g5)

        dsinks = jnp.sum(dsk, axis=0)[:, 0, :hb_b].reshape(H)
        bw = w_rg.shape[-1]
        small[l] = [lru[2].reshape(-1), lru[3].reshape(-1), _block_diag_grad(lru[4], bw).reshape(-1), lru[5].reshape(-1),
                    _block_diag_grad(lru[6], bw).reshape(-1), lru[7].reshape(-1), lru[8].reshape(-1),
                    jnp.pad(dsinks, (0, LANES - H)), dg1.reshape(-1), db1.reshape(-1), dg2.reshape(-1), db2.reshape(-1)]

    qb, qf = _reduce_chip(above[1], _swap_sibling_halves(above[1]), core_a, shard_a)
    store_reduced(above[0], _reduce_finish(qf, _exchange_chips(qb)))

    sizes = [t.size for t in small[0]]
    flat = jnp.concatenate([t for l in range(L) for t in small[l]])
    n_flat = flat.size
    rows = -(-n_flat // (LANES * 256)) * 256
    flat = jnp.pad(flat, (0, rows * LANES - n_flat)).reshape(rows, LANES)
    pair = _small_pair(flat)

    order = ["w_in", "conv_w", "conv_b", "w_rg", "b_rg", "w_ig", "b_ig", "lru_lambda", "sinks", "w_branch", "w_out",
             "ln1_g", "ln1_b", "w_ffn_in", "w_ffn_out", "ln2_g", "ln2_b"]
    weights = dict(w_in=w_in, conv_w=conv_w, conv_b=conv_b, w_rg=w_rg, b_rg=b_rg, w_ig=w_ig, b_ig=b_ig, lru_lambda=lru_lambda,
                   sinks=sinks, w_branch=w_branch, w_out=w_out, ln1_g=ln1_g, ln1_b=ln1_b, w_ffn_in=w_ffn_in,
                   w_ffn_out=w_ffn_out, ln2_g=ln2_g, ln2_b=ln2_b)
    ms = dict(w_in=m_w_in, conv_w=m_conv_w, conv_b=m_conv_b, w_rg=m_w_rg, b_rg=m_b_rg, w_ig=m_w_ig, b_ig=m_b_ig,
              lru_lambda=m_lru_lambda, sinks=m_sinks, w_branch=m_w_branch, w_out=m_w_out, ln1_g=m_ln1_g, ln1_b=m_ln1_b,
              w_ffn_in=m_w_ffn_in, w_ffn_out=m_w_ffn_out, ln2_g=m_ln2_g, ln2_b=m_ln2_b)
    vs = dict(w_in=v_w_in, conv_w=v_conv_w, conv_b=v_conv_b, w_rg=v_w_rg, b_rg=v_b_rg, w_ig=v_w_ig, b_ig=v_b_ig,
              lru_lambda=v_lru_lambda, sinks=v_sinks, w_branch=v_w_branch, w_out=v_w_out, ln1_g=v_ln1_g, ln1_b=v_ln1_b,
              w_ffn_in=v_w_ffn_in, w_ffn_out=v_w_ffn_out, ln2_g=v_ln2_g, ln2_b=v_ln2_b)
    grads = dict(w_in=jnp.stack(big["w_in"]), w_branch=jnp.stack(big["w_branch"]), w_out=jnp.stack(big["w_out"]),
                 w_ffn_in=jnp.stack(big["w_ffn_in"]), w_ffn_out=jnp.stack(big["w_ffn_out"]))
    deltas, new_m, new_v = {}, {}, {}
    half_rows = rows // 2
    others = []
    for nm, part in (("w_in", pair[:half_rows]), ("w_ffn_in", pair[half_rows:])):
        deltas[nm], new_m[nm], new_v[nm], got = _adamw(weights[nm], grads[nm], ms[nm], vs[nm],
                                                       rider=_small_exchange_rider(part))
        others.append(got)
    red_small = _small_chip_sum(pair, jnp.concatenate(others, axis=1)).reshape(-1)
    per_layer = sum(sizes)
    names = ["conv_w", "conv_b", "w_rg", "b_rg", "w_ig", "b_ig", "lru_lambda", "sinks", "ln1_g", "ln1_b", "ln2_g", "ln2_b"]
    sg = {nm: [] for nm in names}
    for l in range(L):
        p = l * per_layer
        for nm, sz in zip(names, sizes):
            sg[nm].append(red_small[p:p + sz])
            p += sz
    grads.update(
        conv_w=lax.dynamic_slice_in_dim(jnp.stack(sg["conv_w"]).reshape(L, CONV_WIDTH, D), shard * (D // N_CHIPS), D // N_CHIPS, axis=2),
        conv_b=jnp.stack(sg["conv_b"]), w_rg=jnp.stack(sg["w_rg"]).reshape(w_rg.shape), b_rg=jnp.stack(sg["b_rg"]),
        w_ig=jnp.stack(sg["w_ig"]).reshape(w_ig.shape), b_ig=jnp.stack(sg["b_ig"]), lru_lambda=jnp.stack(sg["lru_lambda"]),
        sinks=jnp.stack(sg["sinks"])[:, :H], ln1_g=jnp.stack(sg["ln1_g"]), ln1_b=jnp.stack(sg["ln1_b"]),
        ln2_g=jnp.stack(sg["ln2_g"]), ln2_b=jnp.stack(sg["ln2_b"]),
    )

    for nm in order:
        if nm not in deltas:
            deltas[nm], new_m[nm], new_v[nm] = _adamw(weights[nm], grads[nm], ms[nm], vs[nm])
    return (loss, dx.reshape(B, S, D), *[grads[nm] for nm in order], *[deltas[nm] for nm in order],
            *[new_m[nm] for nm in order], *[new_v[nm] for nm in order])
```

```python
import math

import jax
import jax.numpy as jnp
from jax import lax
from jax.experimental import pallas as pl
from jax.experimental.pallas import tpu as pltpu

HEAD_DIM = 64
WIN = 128
DILS = (1, 4, 16)
SWA_GROUP = 4
CONV_WIDTH = 4
LRU_C = 8.0
LN_EPS = 1e-5
NEG_INF = -1e30
N_CHIPS = 4
ADAM_LR, ADAM_B1, ADAM_B2, ADAM_EPS, ADAM_WD, ADAM_STEP = 0.001, 0.9, 0.999, 1e-08, 0.01, 10

LANES = 128
SUBLANES = 8
VMEM_LIMIT = 48 * 1024 * 1024

assert math.log2(HEAD_DIM) % 2 == 0

F32 = jnp.float32
BF16 = jnp.bfloat16
MESH = pl.DeviceIdType.MESH
ANY = pl.BlockSpec(memory_space=pl.ANY)


def _pcall(body, **kw):
    return pl.pallas_call(body, **kw)


def _pcall_comm(body, **kw):
    return pl.pallas_call(body, **kw)


def _params(*sem):
    return pltpu.CompilerParams(dimension_semantics=tuple(sem), vmem_limit_bytes=VMEM_LIMIT)


def _tile(dim, target):
    if dim <= target:
        return dim
    best = None
    for t in range(LANES, target + 1, LANES):
        if dim % t == 0:
            best = t
    assert best is not None, (dim, target)
    return best


def _sigmoid(x):
    return 1.0 / (1.0 + jnp.exp(-x))


def _dot(a, b, dims):
    return lax.dot_general(a, b, (dims, ((), ())), preferred_element_type=F32)


def _dot_nn(a, b):
    return _dot(a, b, ((1,), (0,)))


def _dot_nt(a, b):
    return _dot(a, b, ((1,), (1,)))


def _dot_tn(a, b):
    return _dot(a, b, ((0,), (0,)))


def _matmul(a, b, *, mode, name, out_dtype=F32, tm=512, tn=512, tk=2048, resid=None, rs=1.0, out_shards=0, n_outer=False,
            ln=None, a_pick=0, b_pick=0, rider=None):
    b_sh = b.ndim == 3
    a_st = a.ndim == 3
    if mode == "nn":
        M, K = a.shape[-2:]
        N = b.shape[-1] * (b.shape[0] if b_sh else 1)
    elif mode == "nt":
        M, K = a.shape[-2:]
        N = b.shape[-2]
    else:
        K, M = a.shape
        N = b.shape[-1]
    tm = _tile(M, tm)
    if mode == "nn" and b_sh:
        tn = b.shape[-1]
    elif out_shards:
        tn = N // out_shards
    else:
        tn = _tile(N, tn)
    if mode == "nt" and b_sh:
        tk = b.shape[-1]
    else:
        tk = _tile(K, tk)
    nk = K // tk
    grid = (N // tn, M // tm, nk) if n_outer else (M // tm, N // tn, nk)

    def spec(shape, f):
        return pl.BlockSpec(shape, (lambda g0, g1, k: f(g1, g0, k)) if n_outer else f)

    a_rows = spec((None, tm, tk), lambda i, j, k: (a_pick, i, k)) if a_st else spec((tm, tk), lambda i, j, k: (i, k))
    if mode == "nn":
        a_spec = a_rows
        b_spec = spec((None, tk, tn), lambda i, j, k: (j, k, 0)) if b_sh else spec((tk, tn), lambda i, j, k: (k, j))
        contract = _dot_nn
    elif mode == "nt":
        a_spec = a_rows
        b_spec = spec((None, tn, tk), lambda i, j, k: (k, j, 0)) if b_sh else spec((tn, tk), lambda i, j, k: (j, k))
        contract = _dot_nt
    else:
        a_spec = spec((tk, tm), lambda i, j, k: (k, i))
        b_spec = spec((None, tk, tn), lambda i, j, k: (b_pick, k, j)) if b_sh else spec((tk, tn), lambda i, j, k: (k, j))
        contract = _dot_tn
    if out_shards:
        out_shape = jax.ShapeDtypeStruct((out_shards, M, tn), out_dtype)
        o_spec = spec((None, tm, tn), lambda i, j, k: (j, i, 0))
    else:
        out_shape = jax.ShapeDtypeStruct((M, N), out_dtype)
        o_spec = spec((tm, tn), lambda i, j, k: (i, j))
    in_specs = [a_spec, b_spec]
    args = [a, b]
    if resid is not None:
        in_specs.append(spec((tm, tn), lambda i, j, k: (i, j)))
        args.append(resid)
    if ln is not None:
        assert tn == N and resid is not None and not out_shards
        in_specs += [spec((1, N), lambda i, j, k: (0, 0))] * 2
        args += list(ln)
        out_shape = [out_shape, out_shape]
        o_spec = [o_spec, o_spec]
    n_in = len(args)

    def body(*refs):
        a_ref, b_ref = refs[:2]
        r_ref = refs[2] if resid is not None else None
        o_ref = refs[n_in]
        part = contract(a_ref[...].astype(BF16), b_ref[...].astype(BF16))

        def finish(res):
            if resid is not None:
                res = res + rs * r_ref[...]
            o_ref[...] = res.astype(out_dtype)
            if ln is not None:
                g_ref, bb_ref, y_ref = refs[n_in - 2], refs[n_in - 1], refs[n_in + 1]
                zc = res - jnp.mean(res, axis=1, keepdims=True)
                var = jnp.mean(zc * zc, axis=1, keepdims=True)
                y_ref[...] = zc * lax.rsqrt(var + LN_EPS) * g_ref[...] + bb_ref[...]

        if nk == 1:
            finish(part)
            return
        acc_ref = refs[-1]
        k = pl.program_id(2)

        @pl.when(k == 0)
        def _():
            acc_ref[...] = part

        @pl.when(jnp.logical_and(k > 0, k < nk - 1))
        def _():
            acc_ref[...] += part

        @pl.when(k == nk - 1)
        def _():
            finish(acc_ref[...] + part)

    if rider is None:
        return _pcall(
            body, name=name, grid=grid, in_specs=in_specs, out_specs=o_spec, out_shape=out_shape,
            scratch_shapes=[pltpu.VMEM((tm, tn), F32)] if nk > 1 else [],
            compiler_params=_params("parallel", "parallel", "arbitrary"),
        )(*args)
    assert ln is None
    res = _call_with_rider(
        rider, body, name=name, grid=grid, in_specs=in_specs, out_specs=[o_spec], out_shape=[out_shape],
        scratch_shapes=[pltpu.VMEM((tm, tn), F32)] if nk > 1 else [], args=args)
    return res[0], list(res[1:])


def _shift_down(x, d, row):
    return jnp.where(row >= d, pltpu.roll(x, d, 0), 0.0)


def _shift_up(x, d, row, n):
    return jnp.where(row < n - d, pltpu.roll(x, n - d, 0), 0.0)


def _log1p(u):
    w = 1.0 + u
    return jnp.where(w == 1.0, u, jnp.log(w) * u / (w - 1.0))


def _gelu_parts(g):
    k = math.sqrt(2.0 / math.pi)
    c = 0.044715
    t = jnp.tanh(k * (g + c * g * g * g))
    val = 0.5 * g * (1.0 + t)
    der = 0.5 * (1.0 + t) + 0.5 * g * (1.0 - t * t) * k * (1.0 + 3.0 * c * g * g)
    return val, der


def _lru_gates(xr, cw_ref, cb_ref, wrg_ref, brg_ref, wig_ref, big_ref, lam_ref, row):
    xc = cw_ref[3:4, :] * xr + cb_ref[...]
    for d in range(1, CONV_WIDTH):
        xc = xc + cw_ref[3 - d:4 - d, :] * _shift_down(xr, d, row)
    xcb = xc.astype(BF16)
    r = _sigmoid(_dot_nn(xcb, wrg_ref[...]) + brg_ref[...])
    ig = _sigmoid(_dot_nn(xcb, wig_ref[...]) + big_ref[...])
    lam = lam_ref[...]
    sp = jnp.maximum(-lam, 0.0) + _log1p(jnp.exp(-jnp.abs(lam)))
    log_a = (-LRU_C) * r * sp
    a = jnp.exp(log_a)
    y2 = 2.0 * log_a
    one_m_a2 = jnp.where(y2 > -0.01, -(y2 + 0.5 * y2 * y2 + (1.0 / 6.0) * y2 * y2 * y2), 1.0 - jnp.exp(y2))
    mult = jnp.sqrt(one_m_a2)
    return xc, r, ig, sp, a, mult


def _scan_local(a, b, row, n, reverse):
    sub = row % SUBLANES
    d = 1
    while d < SUBLANES:
        if reverse:
            keep = sub < SUBLANES - d
            a_s = jnp.where(keep, pltpu.roll(a, n - d, 0), 1.0)
            b_s = jnp.where(keep, pltpu.roll(b, n - d, 0), 0.0)
        else:
            keep = sub >= d
            a_s = jnp.where(keep, pltpu.roll(a, d, 0), 1.0)
            b_s = jnp.where(keep, pltpu.roll(b, d, 0), 0.0)
        b = a * b_s + b
        a = a * a_s
        d *= 2
    return a, b


def _scan_carry(a_ref, b_ref, out_ref, n, reverse):
    ng = n // SUBLANES

    def step(gidx, carry):
        g = (ng - 1 - gidx) if reverse else gidx
        rows = pl.ds(pl.multiple_of(g * SUBLANES, SUBLANES), SUBLANES)
        h = a_ref[rows, :] * carry + b_ref[rows, :]
        out_ref[rows, :] = h
        return h[0:1, :] if reverse else h[SUBLANES - 1:SUBLANES, :]

    lax.fori_loop(0, ng, step, jnp.zeros((1, LANES), F32), unroll=8)


def _lru_specs(B, S, D, C, x_off, g_off):
    nct = D // LANES
    seq = lambda off: pl.BlockSpec((None, S, LANES), lambda ct, b: (b, 0, off // LANES + ct))
    row = lambda r: pl.BlockSpec((r, LANES), lambda ct, b: (0, ct))
    wbd = pl.BlockSpec((None, LANES, LANES), lambda ct, b: (ct, 0, 0))
    return nct, seq, row, wbd


def _lru_fwd(proj3, lp, *, D, x_off, g_off, rider=None):
    B, S, C = proj3.shape
    nct, seq, row, wbd = _lru_specs(B, S, D, C, x_off, g_off)

    def body(xr_ref, g_ref, cw_ref, cb_ref, wrg_ref, brg_ref, wig_ref, big_ref, lam_ref, h_ref, ya_ref, a_s, b_s):
        rowi = lax.broadcasted_iota(jnp.int32, (S, LANES), 0)
        xr = xr_ref[...]
        xc, r, ig, sp, a, mult = _lru_gates(xr, cw_ref, cb_ref, wrg_ref, brg_ref, wig_ref, big_ref, lam_ref, rowi)
        al, bl = _scan_local(a, mult * (ig * xc), rowi, S, False)
        a_s[...] = al
        b_s[...] = bl
        _scan_carry(a_s, b_s, h_ref, S, False)
        gel, _ = _gelu_parts(g_ref[...])
        ya_ref[...] = (h_ref[...] * gel).astype(BF16)

    out_seq = pl.BlockSpec((None, S, LANES), lambda ct, b: (b, 0, ct))
    return _call_with_rider(
        rider, body, name="lru_fwd", grid=(nct, B),
        in_specs=[seq(x_off), seq(g_off), row(CONV_WIDTH), row(1), wbd, row(1), wbd, row(1), row(1)],
        out_specs=[out_seq, out_seq],
        out_shape=[jax.ShapeDtypeStruct((B, S, D), F32), jax.ShapeDtypeStruct((B, S, D), BF16)],
        scratch_shapes=[pltpu.VMEM((S, LANES), F32), pltpu.VMEM((S, LANES), F32)],
        args=[proj3, proj3, lp["conv_w"], lp["conv_b"], lp["w_rg_bd"], lp["b_rg"], lp["w_ig_bd"], lp["b_ig"], lp["lam"]])


def _lru_bwd(proj3, h3, dya3, lp, *, D, x_off, g_off, rider=None):
    B, S, C = proj3.shape
    nct, seq, row, wbd = _lru_specs(B, S, D, C, x_off, g_off)

    def body(xr_ref, g_ref, h_ref, dy_ref, cw_ref, cb_ref, wrg_ref, brg_ref, wig_ref, big_ref, lam_ref,
             dxr_ref, dg_ref, dcw_ref, dcb_ref, dwrg_ref, dbrg_ref, dwig_ref, dbig_ref, dlam_ref, a_s, b_s, l_s):
        first = pl.program_id(1) == 0
        rowi = lax.broadcasted_iota(jnp.int32, (S, LANES), 0)
        xr = xr_ref[...]
        xc, r, ig, sp, a, mult = _lru_gates(xr, cw_ref, cb_ref, wrg_ref, brg_ref, wig_ref, big_ref, lam_ref, rowi)
        h = h_ref[...]
        dy = dy_ref[...]
        gel, dgel = _gelu_parts(g_ref[...])
        dg_ref[...] = (dy * h * dgel).astype(BF16)
        al, bl = _scan_local(_shift_up(a, 1, rowi, S), dy * gel, rowi, S, True)
        a_s[...] = al
        b_s[...] = bl
        _scan_carry(a_s, b_s, l_s, S, True)
        lamb = l_s[...]
        u = ig * xc
        da = lamb * _shift_down(h, 1, rowi)
        dlog_a = da * a - (lamb * u) * (a * a) / mult
        du = lamb * mult
        dpre_r = (dlog_a * ((-LRU_C) * sp)) * r * (1.0 - r)
        dpre_i = (du * xc) * ig * (1.0 - ig)
        dsp = jnp.sum(dlog_a * ((-LRU_C) * r), axis=0, keepdims=True)
        dlam = dsp * (-1.0 / (1.0 + jnp.exp(lam_ref[...])))
        dpr = dpre_r.astype(BF16)
        dpi = dpre_i.astype(BF16)
        dxc = du * ig + _dot_nt(dpr, wrg_ref[...]) + _dot_nt(dpi, wig_ref[...])
        xcb = xc.astype(BF16)
        dwrg = _dot_tn(xcb, dpr)
        dwig = _dot_tn(xcb, dpi)
        dxr = cw_ref[3:4, :] * dxc
        dcw = [jnp.sum(xr * dxc, axis=0, keepdims=True)]
        for d in range(1, CONV_WIDTH):
            dxr = dxr + cw_ref[3 - d:4 - d, :] * _shift_up(dxc, d, rowi, S)
            dcw.append(jnp.sum(_shift_down(xr, d, rowi) * dxc, axis=0, keepdims=True))
        dxr_ref[...] = dxr.astype(BF16)
        dcw_rows = jnp.concatenate(dcw[::-1], axis=0)
        sums = ((dcw_ref, dcw_rows), (dcb_ref, jnp.sum(dxc, axis=0, keepdims=True)), (dwrg_ref, dwrg),
                (dbrg_ref, jnp.sum(dpre_r, axis=0, keepdims=True)), (dwig_ref, dwig),
                (dbig_ref, jnp.sum(dpre_i, axis=0, keepdims=True)), (dlam_ref, dlam))

        @pl.when(first)
        def _():
            for ref, val in sums:
                ref[...] = val

        @pl.when(jnp.logical_not(first))
        def _():
            for ref, val in sums:
                ref[...] += val

    out_seq = pl.BlockSpec((None, S, LANES), lambda ct, b: (b, 0, ct))
    f = lambda shape: jax.ShapeDtypeStruct(shape, F32)
    nb = D // LANES
    return _call_with_rider(
        rider, body, name="lru_bwd", grid=(nct, B),
        in_specs=[seq(x_off), seq(g_off), out_seq, out_seq, row(CONV_WIDTH), row(1), wbd, row(1), wbd, row(1), row(1)],
        out_specs=[out_seq, out_seq, row(CONV_WIDTH), row(1), wbd, row(1), wbd, row(1), row(1)],
        out_shape=[jax.ShapeDtypeStruct((B, S, D), BF16), jax.ShapeDtypeStruct((B, S, D), BF16),
                   f((CONV_WIDTH, D)), f((1, D)), f((nb, LANES, LANES)), f((1, D)), f((nb, LANES, LANES)), f((1, D)), f((1, D))],
        scratch_shapes=[pltpu.VMEM((S, LANES), F32)] * 3, semantics=("parallel", "arbitrary"),
        args=[proj3, proj3, h3, dya3, lp["conv_w"], lp["conv_b"], lp["w_rg_bd"], lp["b_rg"], lp["w_ig_bd"], lp["b_ig"], lp["lam"]])


def _pair_stack(x, lo):
    z = jnp.zeros_like(x)
    return jnp.concatenate([jnp.where(lo, x, z), jnp.where(lo, z, x)], axis=0).astype(BF16)


def _pair_join(y2, lo):
    return jnp.where(lo, y2[:WIN], y2[WIN:])


def _pair_col(xb):
    return jnp.concatenate([xb[:, 0:1], xb[:, HEAD_DIM:HEAD_DIM + 1]], axis=0)


def _pair_bcast(col, lo):
    return jnp.where(lo, jnp.broadcast_to(col[:WIN], (WIN, LANES)), jnp.broadcast_to(col[WIN:], (WIN, LANES)))


def _dil_rows(it, d, S):
    if d == 1:
        cur = pl.multiple_of(it * WIN, WIN)
        prev = pl.multiple_of(jnp.maximum(it - 1, 0) * WIN, WIN)
        return pl.ds(cur, WIN), pl.ds(prev, WIN), it > 0
    r, i = it % d, it // d
    cur = i * (WIN * d) + r
    prev = jnp.maximum(i - 1, 0) * (WIN * d) + r
    return pl.ds(cur, WIN, stride=d), pl.ds(prev, WIN, stride=d), i > 0


def _dil_bias(two_blocks, stack=2):
    nk = 2 * WIN if two_blocks else WIN
    qi = lax.broadcasted_iota(jnp.int32, (stack * WIN, nk), 0) & (WIN - 1)
    kj = lax.broadcasted_iota(jnp.int32, (stack * WIN, nk), 1)
    if not two_blocks:
        return jnp.where(kj <= qi, 0.0, NEG_INF), None
    cur = jnp.logical_and(kj >= WIN, kj - WIN <= qi)
    prev = jnp.logical_and(kj < WIN, kj >= qi)
    return jnp.where(jnp.logical_or(cur, prev), 0.0, NEG_INF), jnp.where(cur, 0.0, NEG_INF)


def _dil_specs(B, S, D, C, offs):
    grid = (B, D // LANES)
    seq = lambda off: pl.BlockSpec((None, S, LANES), lambda b, p: (b, 0, off // LANES + p))
    return grid, [seq(o) for o in offs], seq(0)


def _call_with_rider(rider, body, *, name, grid, in_specs, out_specs, out_shape, scratch_shapes, args, semantics=None):
    if rider is None:
        return _pcall(body, name=name, grid=grid, in_specs=in_specs, out_specs=out_specs, out_shape=out_shape,
                      scratch_shapes=scratch_shapes, compiler_params=_params(*(semantics or ("parallel",) * len(grid))))(*args)
    n_in, n_out, n_sc = len(in_specs), len(out_specs), len(scratch_shapes)
    r_in, r_out = len(rider["ins"]), len(rider["out_shapes"])

    def wrapped(*refs):
        p = 0
        own_in = refs[p:p + n_in]; p += n_in
        rid_in = refs[p:p + r_in]; p += r_in
        own_out = refs[p:p + n_out]; p += n_out
        rid_out = refs[p:p + r_out]; p += r_out
        own_sc = refs[p:p + n_sc]; p += n_sc
        send_sems, recv_sems = refs[p:p + 2]
        ids = [pl.program_id(a) for a in range(len(grid))]
        first = ids[0] == 0
        last = ids[0] == grid[0] - 1
        for a in range(1, len(grid)):
            first = jnp.logical_and(first, ids[a] == 0)
            last = jnp.logical_and(last, ids[a] == grid[a] - 1)

        @pl.when(first)
        def _():
            rider["start"](rid_in, rid_out, send_sems, recv_sems)

        body(*own_in, *own_out, *own_sc)

        @pl.when(last)
        def _():
            rider["finish"](rid_in, rid_out, send_sems, recv_sems)

    aliases = {n_in + t: n_out + t for t in range(r_in)} if rider.get("in_place") else {}
    res = _pcall_comm(
        wrapped, name=name + "_" + rider["name"], grid=grid, in_specs=list(in_specs) + [ANY] * r_in,
        out_specs=list(out_specs) + [ANY] * r_out, out_shape=list(out_shape) + list(rider["out_shapes"]),
        scratch_shapes=list(scratch_shapes) + [pltpu.SemaphoreType.DMA((rider["n"],)), pltpu.SemaphoreType.DMA((rider["n"],))],
        input_output_aliases=aliases, compiler_params=_params(*(("arbitrary",) * len(grid))),
    )(*args, *rider["ins"])
    return res


def _dil_fwd(proj3, *, D, q_off, k_off, v_off, rider=None):
    B, S, C = proj3.shape
    n_it = S // WIN
    scale = HEAD_DIM ** -0.5
    grid, in_specs, out_spec = _dil_specs(B, S, D, C, (q_off, k_off, v_off))

    def body(q_ref, k_ref, v_ref, o_ref, l_ref):
        lo = lax.broadcasted_iota(jnp.int32, (WIN, LANES), 1) < HEAD_DIM
        for c, d in enumerate(DILS):
            two = S // d > WIN
            bias_all, bias_first = _dil_bias(two)

            def step(it, _, c=c, d=d, two=two, bias_all=bias_all, bias_first=bias_first):
                cur, prev, later = _dil_rows(it, d, S)
                q2 = _pair_stack(q_ref[cur, :] * scale, lo)
                if two:
                    k2 = jnp.concatenate([k_ref[prev, :], k_ref[cur, :]], axis=0).astype(BF16)
                    v2 = jnp.concatenate([v_ref[prev, :], v_ref[cur, :]], axis=0).astype(BF16)
                    bias = jnp.where(later, bias_all, bias_first)
                else:
                    k2, v2, bias = k_ref[cur, :].astype(BF16), v_ref[cur, :].astype(BF16), bias_all
                s2 = _dot_nt(q2, k2) + bias
                m2 = jnp.max(s2, axis=1, keepdims=True)
                p2 = jnp.exp(s2 - m2)
                den = jnp.sum(p2, axis=1, keepdims=True)
                oc = _pair_join(_dot_nn(p2.astype(BF16), v2) / den, lo)
                lc = _pair_bcast(m2 + jnp.log(den), lo)
                if c == 0:
                    o_ref[cur, :] = oc
                    l_ref[cur, :] = lc
                else:
                    l_old = l_ref[cur, :]
                    mx = jnp.maximum(l_old, lc)
                    e_old, e_new = jnp.exp(l_old - mx), jnp.exp(lc - mx)
                    tot = e_old + e_new
                    o_ref[cur, :] = (e_old * o_ref[cur, :] + e_new * oc) / tot
                    l_ref[cur, :] = mx + jnp.log(tot)
                return 0

            lax.fori_loop(0, n_it, step, 0, unroll=16)

    return _call_with_rider(
        rider, body, name="dil_fwd", grid=grid, in_specs=in_specs, out_specs=[out_spec, out_spec],
        out_shape=[jax.ShapeDtypeStruct((B, S, D), F32)] * 2, scratch_shapes=[], args=[proj3, proj3, proj3])


def _dil_bwd(proj3, o3, l3, do3, *, D, q_off, k_off, v_off, rider=None):
    B, S, C = proj3.shape
    n_it = S // WIN
    scale = HEAD_DIM ** -0.5
    grid, in_specs, out_spec = _dil_specs(B, S, D, C, (q_off, k_off, v_off))

    def body(q_ref, k_ref, v_ref, o_ref, l_ref, do_ref, dq_ref, dk_ref, dv_ref, dd_s, dq_s, dk_s, dv_s):
        lo = lax.broadcasted_iota(jnp.int32, (WIN, LANES), 1) < HEAD_DIM
        lo_s = lax.broadcasted_iota(jnp.int32, (S, LANES), 1) < HEAD_DIM
        prod = do_ref[...] * o_ref[...]
        d_lo = jnp.sum(jnp.where(lo_s, prod, 0.0), axis=1, keepdims=True)
        d_hi = jnp.sum(jnp.where(lo_s, 0.0, prod), axis=1, keepdims=True)
        dd_s[...] = jnp.where(lo_s, jnp.broadcast_to(d_lo, (S, LANES)), jnp.broadcast_to(d_hi, (S, LANES)))
        dq_s[...] = jnp.zeros_like(dq_s)
        dk_s[...] = jnp.zeros_like(dk_s)
        dv_s[...] = jnp.zeros_like(dv_s)
        for d in DILS:
            two = S // d > WIN
            bias_all, bias_first = _dil_bias(two)

            def step(it, _, d=d, two=two, bias_all=bias_all, bias_first=bias_first):
                cur, prev, later = _dil_rows(it, d, S)
                q2 = _pair_stack(q_ref[cur, :] * scale, lo)
                do2 = _pair_stack(do_ref[cur, :], lo)
                l2 = _pair_col(l_ref[cur, :])
                dd2 = _pair_col(dd_s[cur, :])
                if two:
                    k2 = jnp.concatenate([k_ref[prev, :], k_ref[cur, :]], axis=0).astype(BF16)
                    v2 = jnp.concatenate([v_ref[prev, :], v_ref[cur, :]], axis=0).astype(BF16)
                    bias = jnp.where(later, bias_all, bias_first)
                else:
                    k2, v2, bias = k_ref[cur, :].astype(BF16), v_ref[cur, :].astype(BF16), bias_all
                p2 = jnp.exp(_dot_nt(q2, k2) + bias - l2)
                ds2 = (p2 * (_dot_nt(do2, v2) - dd2)).astype(BF16)
                dq_s[cur, :] += _pair_join(_dot_nn(ds2, k2), lo) * scale
                dk2 = _dot_tn(ds2, q2)
                dv2 = _dot_tn(p2.astype(BF16), do2)
                if two:
                    dk_s[prev, :] += dk2[:WIN]
                    dv_s[prev, :] += dv2[:WIN]
                    dk_s[cur, :] += dk2[WIN:]
                    dv_s[cur, :] += dv2[WIN:]
                else:
                    dk_s[cur, :] += dk2
                    dv_s[cur, :] += dv2
                return 0

            lax.fori_loop(0, n_it, step, 0, unroll=16)
        dq_ref[...] = dq_s[...].astype(BF16)
        dk_ref[...] = dk_s[...].astype(BF16)
        dv_ref[...] = dv_s[...].astype(BF16)

    return _call_with_rider(
        rider, body, name="dil_bwd", grid=grid, in_specs=in_specs + [out_spec] * 3, out_specs=[out_spec] * 3,
        out_shape=[jax.ShapeDtypeStruct((B, S, D), BF16)] * 3, scratch_shapes=[pltpu.VMEM((S, LANES), F32)] * 4,
        args=[proj3, proj3, proj3, o3, l3, do3])


SWA_HB = 2 * SWA_GROUP


def _to_half(x, src, dst, lo):
    if src != dst:
        x = pltpu.roll(x, HEAD_DIM, 1)
    return jnp.where(lo if dst == 0 else jnp.logical_not(lo), x, 0.0)


def _swa_kv(g):
    return 2 * g // SWA_GROUP


SWA_STACKS = ((0, 1), (2, 3))


def _swa_stack(ref, gs, lo, dtype, rows=slice(None)):
    parts = []
    for g in gs:
        x = ref[rows, g * LANES:(g + 1) * LANES]
        parts += [_to_half(x, 0, _swa_kv(g), lo), _to_half(x, 1, _swa_kv(g), lo)]
    return jnp.concatenate(parts, axis=0).astype(dtype)


def _swa_unstack(y, gs, lo):
    out = []
    for t, g in enumerate(gs):
        even, odd = y[2 * t * WIN:(2 * t + 1) * WIN], y[(2 * t + 1) * WIN:(2 * t + 2) * WIN]
        out.append(_to_half(even, _swa_kv(g), 0, lo) + _to_half(odd, _swa_kv(g), 1, lo))
    return out


def _swa_cols(x, gs):
    cols = []
    for g in gs:
        cols += [jnp.broadcast_to(x[:, 2 * g:2 * g + 1], (WIN, 1)), jnp.broadcast_to(x[:, 2 * g + 1:2 * g + 2], (WIN, 1))]
    return jnp.concatenate(cols, axis=0)


SWA_UNROLL = 16


def _swa_seq_specs(B, S, D, q_off, k_off, v_off):
    qw = SWA_HB * HEAD_DIM
    assert q_off % qw == 0 and k_off % LANES == 0 and v_off % LANES == 0 and D % qw == 0
    seq = lambda width, off: pl.BlockSpec((None, S, width), lambda b, hh: (b, 0, off // width + hh))
    sink = pl.BlockSpec((None, 1, LANES), lambda b, hh: (hh, 0, 0))
    return (B, D // qw), seq, sink, qw


def _swa_rows(it):
    cur = pl.ds(pl.multiple_of(it * WIN, WIN), WIN)
    prev = pl.ds(pl.multiple_of(jnp.maximum(it - 1, 0) * WIN, WIN), WIN)
    return cur, prev, it > 0


def _swa_seq_fwd(proj3, sinks, *, D, q_off, k_off, v_off, rider=None):
    B, S, C = proj3.shape
    scale = HEAD_DIM ** -0.5
    grid, seq, sink, qw = _swa_seq_specs(B, S, D, q_off, k_off, v_off)
    nhb = D // qw

    def body(q_ref, k_ref, v_ref, sk_ref, o_ref, lse_ref):
        lo = lax.broadcasted_iota(jnp.int32, (WIN, LANES), 1) < HEAD_DIM
        lane = lax.broadcasted_iota(jnp.int32, (WIN, LANES), 1)
        sk = sk_ref[...]
        biases = [_dil_bias(True, 2 * len(gs)) for gs in SWA_STACKS]

        def step(it, _):
            cur, prev, later = _swa_rows(it)
            k2 = jnp.concatenate([k_ref[prev, :], k_ref[cur, :]], axis=0).astype(BF16)
            v2 = jnp.concatenate([v_ref[prev, :], v_ref[cur, :]], axis=0).astype(BF16)
            lse_acc = jnp.zeros((WIN, LANES), F32)
            for gs, (bias_all, bias_first) in zip(SWA_STACKS, biases):
                bias = jnp.where(later, bias_all, bias_first)
                qs = _swa_stack(q_ref, gs, lo, BF16, cur)
                sks = _swa_cols(sk, gs)
                s = _dot_nt(qs, k2) * scale + bias
                m = jnp.maximum(jnp.max(s, axis=1, keepdims=True), sks)
                p = jnp.exp(s - m)
                den = jnp.sum(p, axis=1, keepdims=True) + jnp.exp(sks - m)
                for g, grp in zip(gs, _swa_unstack(_dot_nn(p.astype(BF16), v2) / den, gs, lo)):
                    o_ref[cur, g * LANES:(g + 1) * LANES] = grp
                ls = m + jnp.log(den)
                for t, g in enumerate(gs):
                    lse_acc = jnp.where(lane == 2 * g, ls[2 * t * WIN:(2 * t + 1) * WIN], lse_acc)
                    lse_acc = jnp.where(lane == 2 * g + 1, ls[(2 * t + 1) * WIN:(2 * t + 2) * WIN], lse_acc)
            lse_ref[cur, :] = lse_acc
            return 0

        lax.fori_loop(0, S // WIN, step, 0, unroll=SWA_UNROLL)

    return _call_with_rider(
        rider, body, name="swa_fwd", grid=grid,
        in_specs=[seq(qw, q_off), seq(LANES, k_off), seq(LANES, v_off), sink],
        out_specs=[seq(qw, 0), seq(LANES, 0)],
        out_shape=[jax.ShapeDtypeStruct((B, S, D), F32), jax.ShapeDtypeStruct((B, S, nhb * LANES), F32)],
        scratch_shapes=[], args=[proj3, proj3, proj3, sinks])


def _swa_seq_bwd(proj3, o3, lse3, do3, sinks, *, D, q_off, k_off, v_off):
    B, S, C = proj3.shape
    scale = HEAD_DIM ** -0.5
    grid, seq, sink, qw = _swa_seq_specs(B, S, D, q_off, k_off, v_off)
    nhb = D // qw
    KV = D // SWA_GROUP

    def body(q_ref, k_ref, v_ref, o_ref, l_ref, do_ref, sk_ref, dq_ref, dk_ref, dv_ref, dsk_ref, dk_s, dv_s):
        lo = lax.broadcasted_iota(jnp.int32, (WIN, LANES), 1) < HEAD_DIM
        lane = lax.broadcasted_iota(jnp.int32, (1, LANES), 1)
        sk = sk_ref[...]
        biases = [_dil_bias(True, 2 * len(gs)) for gs in SWA_STACKS]
        dk_s[...] = jnp.zeros_like(dk_s)
        dv_s[...] = jnp.zeros_like(dv_s)

        dsk_ref[...] = jnp.zeros_like(dsk_ref)

        def step(it, _):
            cur, prev, later = _swa_rows(it)
            k2 = jnp.concatenate([k_ref[prev, :], k_ref[cur, :]], axis=0).astype(BF16)
            v2 = jnp.concatenate([v_ref[prev, :], v_ref[cur, :]], axis=0).astype(BF16)
            lse = l_ref[cur, :]
            dk2 = jnp.zeros((2 * WIN, LANES), F32)
            dv2 = jnp.zeros((2 * WIN, LANES), F32)
            dsk_acc = jnp.zeros((1, LANES), F32)
            for gs, (bias_all, bias_first) in zip(SWA_STACKS, biases):
                bias = jnp.where(later, bias_all, bias_first)
                qs = _swa_stack(q_ref, gs, lo, BF16, cur)
                dos = _swa_stack(do_ref, gs, lo, BF16, cur)
                dds = []
                for g in gs:
                    prod = do_ref[cur, g * LANES:(g + 1) * LANES] * o_ref[cur, g * LANES:(g + 1) * LANES]
                    dds += [jnp.sum(jnp.where(lo, prod, 0.0), axis=1, keepdims=True),
                            jnp.sum(jnp.where(lo, 0.0, prod), axis=1, keepdims=True)]
                dds = jnp.concatenate(dds, axis=0)
                ls = _swa_cols(lse, gs)
                ps = jnp.exp(_dot_nt(qs, k2) * scale + bias - ls)
                dss = (ps * (_dot_nt(dos, v2) - dds) * scale).astype(BF16)
                for g, grp in zip(gs, _swa_unstack(_dot_nn(dss, k2), gs, lo)):
                    dq_ref[cur, g * LANES:(g + 1) * LANES] = grp.astype(BF16)
                dk2 = dk2 + _dot_tn(dss, qs)
                dv2 = dv2 + _dot_tn(ps.astype(BF16), dos)
                dsks = jnp.exp(_swa_cols(sk, gs) - ls) * dds
                for t, g in enumerate(gs):
                    for u in range(2):
                        rows = slice((2 * t + u) * WIN, (2 * t + u + 1) * WIN)
                        dsk_acc = dsk_acc + jnp.where(lane == 2 * g + u, -jnp.sum(dsks[rows], axis=0, keepdims=True), 0.0)
            dk_s[prev, :] += dk2[:WIN]
            dv_s[prev, :] += dv2[:WIN]
            dk_s[cur, :] += dk2[WIN:]
            dv_s[cur, :] += dv2[WIN:]
            dsk_ref[...] += dsk_acc
            return 0

        lax.fori_loop(0, S // WIN, step, 0, unroll=SWA_UNROLL)
        dk_ref[...] = dk_s[...].astype(BF16)
        dv_ref[...] = dv_s[...].astype(BF16)

    return _pcall(
        body, name="swa_bwd", grid=grid,
        in_specs=[seq(qw, q_off), seq(LANES, k_off), seq(LANES, v_off), seq(qw, 0), seq(LANES, 0), seq(qw, 0), sink],
        out_specs=[seq(qw, 0), seq(LANES, 0), seq(LANES, 0), pl.BlockSpec((None, None, 1, LANES), lambda b, hh: (b, hh, 0, 0))],
        out_shape=[jax.ShapeDtypeStruct((B, S, D), BF16), jax.ShapeDtypeStruct((B, S, KV), BF16),
                   jax.ShapeDtypeStruct((B, S, KV), BF16), jax.ShapeDtypeStruct((B, nhb, 1, LANES), F32)],
        scratch_shapes=[pltpu.VMEM((S, LANES), F32), pltpu.VMEM((S, LANES), F32)],
        compiler_params=_params("parallel", "parallel"),
    )(proj3, proj3, proj3, o3, lse3, do3, sinks)


def _branch_fwd(ys, wb, proj, *, D, g_off, rider=None):
    T = proj.shape[0]
    tm, tn = _tile(T, 256), _tile(D, 512)
    n = len(ys)

    def body(*refs):
        y_refs, w_ref, g_refs, br_ref, mg_ref = refs[:n], refs[n], refs[n + 1:2 * n + 1], refs[2 * n + 1], refs[2 * n + 2]
        acc = None
        for k in range(n):
            br = _dot_nn(y_refs[k][...].astype(BF16), w_ref[k])
            br_ref[k] = br
            term = _sigmoid(g_refs[k][...]) * br
            acc = term if acc is None else acc + term
        mg_ref[...] = acc.astype(BF16)

    gate = lambda k: pl.BlockSpec((tm, tn), lambda i, j: (i, (g_off + k * D) // tn + j))
    return _call_with_rider(
        rider, body, name="branch_fwd", grid=(T // tm, D // tn),
        in_specs=[pl.BlockSpec((tm, D), lambda i, j: (i, 0))] * n + [pl.BlockSpec((n, D, tn), lambda i, j: (0, 0, j))]
        + [gate(k) for k in range(n)],
        out_specs=[pl.BlockSpec((n, tm, tn), lambda i, j: (0, i, j)), pl.BlockSpec((tm, tn), lambda i, j: (i, j))],
        out_shape=[jax.ShapeDtypeStruct((n, T, D), F32), jax.ShapeDtypeStruct((T, D), BF16)],
        scratch_shapes=[], args=[*ys, wb, *([proj] * n)])


def _branch_bwd(dmix, w_out, branch, proj, *, D, g_off):
    n, T, _ = branch.shape
    tm, tn = _tile(T, 512), _tile(D, 512)

    def body(dy_ref, w_ref, br_ref, *rest):
        g_refs, db_ref, dg_refs = rest[:n], rest[n], rest[n + 1:]
        dm = _dot_nt(dy_ref[...].astype(BF16), w_ref[...])
        for k in range(n):
            sg = _sigmoid(g_refs[k][...])
            db_ref[k] = (sg * dm).astype(BF16)
            dg_refs[k][...] = (dm * br_ref[k] * sg * (1.0 - sg)).astype(BF16)

    gate = lambda k: pl.BlockSpec((tm, tn), lambda i, j: (i, (g_off + k * D) // tn + j))
    blk = pl.BlockSpec((tm, tn), lambda i, j: (i, j))
    res = _pcall(
        body, name="branch_bwd", grid=(T // tm, D // tn),
        in_specs=[pl.BlockSpec((tm, D), lambda i, j: (i, 0)), pl.BlockSpec((tn, D), lambda i, j: (j, 0)),
                  pl.BlockSpec((n, tm, tn), lambda i, j: (0, i, j))] + [gate(k) for k in range(n)],
        out_specs=[pl.BlockSpec((n, tm, tn), lambda i, j: (0, i, j))] + [blk] * n,
        out_shape=[jax.ShapeDtypeStruct((n, T, D), BF16)] + [jax.ShapeDtypeStruct((T, D), BF16)] * n,
        compiler_params=_params("parallel", "parallel"),
    )(dmix, w_out, branch, *([proj] * n))
    return res[0], list(res[1:])


def _ln_bwd(dout, z, g):
    T, D = z.shape
    tm = _tile(T, 512)

    def body(do_ref, z_ref, g_ref, dz_ref, dg_ref, db_ref):
        z = z_ref[...]
        do = do_ref[...]
        mu = jnp.mean(z, axis=1, keepdims=True)
        zc = z - mu
        rstd = lax.rsqrt(jnp.mean(zc * zc, axis=1, keepdims=True) + LN_EPS)
        xhat = zc * rstd
        dxh = do * g_ref[...]
        dz_ref[...] = rstd * (dxh - jnp.mean(dxh, axis=1, keepdims=True) - xhat * jnp.mean(dxh * xhat, axis=1, keepdims=True))
        dg = jnp.sum(do * xhat, axis=0, keepdims=True)
        db = jnp.sum(do, axis=0, keepdims=True)
        first = pl.program_id(0) == 0

        @pl.when(first)
        def _():
            dg_ref[...] = dg
            db_ref[...] = db

        @pl.when(jnp.logical_not(first))
        def _():
            dg_ref[...] += dg
            db_ref[...] += db

    blk = pl.BlockSpec((tm, D), lambda i: (i, 0))
    vec = pl.BlockSpec((1, D), lambda i: (0, 0))
    return _pcall(
        body, name="ln_bwd", grid=(T // tm,), in_specs=[blk, blk, vec], out_specs=[blk, vec, vec],
        out_shape=[jax.ShapeDtypeStruct((T, D), F32), jax.ShapeDtypeStruct((1, D), F32), jax.ShapeDtypeStruct((1, D), F32)],
        compiler_params=_params("arbitrary"),
    )(dout, z, g)


def _ffn_in_fwd(x1, w_sh):
    T, D = x1.shape
    ns, _, nsh = w_sh.shape
    half = ns // 2
    Fh = half * nsh
    tm = _tile(T, 512)

    def body(x_ref, wa_ref, wb_ref, h1_ref, h3_ref, f_ref):
        xb = x_ref[...].astype(BF16)
        h1 = _dot_nn(xb, wa_ref[...])
        h3 = _dot_nn(xb, wb_ref[...])
        h1_ref[...] = h1
        h3_ref[...] = h3
        f_ref[...] = (h1 * _sigmoid(h1) * h3).astype(BF16)

    cols = pl.BlockSpec((tm, nsh), lambda j, i: (i, j))
    return _pcall(
        body, name="ffn_in_fwd", grid=(half, T // tm),
        in_specs=[pl.BlockSpec((tm, D), lambda j, i: (i, 0)), pl.BlockSpec((None, D, nsh), lambda j, i: (j, 0, 0)),
                  pl.BlockSpec((None, D, nsh), lambda j, i: (j + half, 0, 0))],
        out_specs=[cols, cols, cols],
        out_shape=[jax.ShapeDtypeStruct((T, Fh), F32), jax.ShapeDtypeStruct((T, Fh), F32), jax.ShapeDtypeStruct((T, Fh), BF16)],
        compiler_params=_params("parallel", "parallel"),
    )(x1, w_sh, w_sh)


def _ffn_out_bwd(dy, w_ffn_out, h1, h3):
    T, Fh = h1.shape
    D = w_ffn_out.shape[1]
    tm = _tile(T, 256)

    def body(dy_ref, w_ref, h1_ref, h3_ref, o_ref):
        d = _dot_nt(dy_ref[...].astype(BF16), w_ref[...])
        h1v = h1_ref[...]
        sg = _sigmoid(h1v)
        o_ref[:, :Fh] = (d * h3_ref[...] * sg * (1.0 + h1v * (1.0 - sg))).astype(BF16)
        o_ref[:, Fh:] = (d * h1v * sg).astype(BF16)

    blk = pl.BlockSpec((tm, Fh), lambda i: (i, 0))
    return _pcall(
        body, name="ffn_out_bwd", grid=(T // tm,),
        in_specs=[pl.BlockSpec((tm, D), lambda i: (i, 0)), pl.BlockSpec((Fh, D), lambda i: (0, 0)), blk, blk],
        out_specs=pl.BlockSpec((tm, 2 * Fh), lambda i: (i, 0)),
        out_shape=jax.ShapeDtypeStruct((T, 2 * Fh), BF16), compiler_params=_params("parallel"),
    )(dy, w_ffn_out, h1, h3)


def _loss_head(y, target):
    T, D = y.shape
    tm = _tile(T, 512)

    def body(y_ref, t_ref, dy_ref, l_ref):
        e = y_ref[...] - t_ref[...]
        dy_ref[...] = e * (1.0 / D)
        sq = e * e
        part = sq[:, 0:LANES]
        for c in range(1, D // LANES):
            part = part + sq[:, c * LANES:(c + 1) * LANES]
        part = jnp.sum(part, axis=0, keepdims=True) * (0.5 / D)
        first = pl.program_id(0) == 0

        @pl.when(first)
        def _():
            l_ref[...] = part

        @pl.when(jnp.logical_not(first))
        def _():
            l_ref[...] += part

    blk = pl.BlockSpec((tm, D), lambda i: (i, 0))
    return _pcall(
        body, name="loss_head", grid=(T // tm,), in_specs=[blk, blk],
        out_specs=[blk, pl.BlockSpec((1, LANES), lambda i: (0, 0))],
        out_shape=[jax.ShapeDtypeStruct((T, D), F32), jax.ShapeDtypeStruct((1, LANES), F32)],
        compiler_params=_params("arbitrary"),
    )(y, target)


def _as_rows(a):
    return a.reshape(-1, a.shape[-1])


def _adamw(w, g, m, v, rider=None):
    w2, g2, m2, v2 = (_as_rows(t) for t in (w, g, m, v))
    R, Cc = w2.shape
    cap = max(SUBLANES, min(512, (256 * 1024) // Cc))
    tm = R if (R <= cap or R % SUBLANES) else max(t for t in range(SUBLANES, cap + 1, SUBLANES) if R % t == 0)
    c1 = 1.0 - ADAM_B1 ** ADAM_STEP
    c2 = 1.0 - ADAM_B2 ** ADAM_STEP

    def body(w_ref, g_ref, m_ref, v_ref, d_ref, nm_ref, nv_ref):
        gg = g_ref[...]
        nm = ADAM_B1 * m_ref[...] + (1.0 - ADAM_B1) * gg
        nv = ADAM_B2 * v_ref[...] + (1.0 - ADAM_B2) * (gg * gg)
        d_ref[...] = (-ADAM_LR) * ((nm / c1) / (jnp.sqrt(nv / c2) + ADAM_EPS) + ADAM_WD * w_ref[...])
        nm_ref[...] = nm
        nv_ref[...] = nv

    blk = pl.BlockSpec((tm, Cc), lambda i: (i, 0))
    res = _call_with_rider(
        rider, body, name="adamw", grid=(R // tm,), in_specs=[blk] * 4, out_specs=[blk] * 3,
        out_shape=[jax.ShapeDtypeStruct((R, Cc), F32)] * 3, scratch_shapes=[], args=[w2, g2, m2, v2])
    return tuple(t.reshape(w.shape) for t in res[:3]) + tuple(res[3:])


def _where_am_i():
    x, y, c = lax.axis_index("x"), lax.axis_index("y"), lax.axis_index("c")
    chips = [(1 - x, y), (x, 1 - y), (1 - x, 1 - y)]
    return x, y, c, chips


def _remote(src, dst, send_sems, recv_sems, k, to):
    return pltpu.make_async_remote_copy(src_ref=src, dst_ref=dst, send_sem=send_sems.at[k], recv_sem=recv_sems.at[k],
                                        device_id=to, device_id_type=MESH)


def _comm_call(body, name, ins, out_shapes, n_remote, n_local):
    return _pcall_comm(
        body, name=name, in_specs=[ANY] * len(ins), out_specs=[ANY] * len(out_shapes), out_shape=out_shapes,
        scratch_shapes=[pltpu.SemaphoreType.DMA((n_remote,)), pltpu.SemaphoreType.DMA((n_remote,)),
                        pltpu.SemaphoreType.DMA((max(n_local, 1),))],
    )(*ins)


def _gather_weights(shards):
    n = len(shards)

    def body(*refs):
        ins, outs = refs[:n], refs[n:2 * n]
        send_sems, recv_sems, local_sems = refs[2 * n:]
        x, y, c, chips = _where_am_i()
        s = 2 * x + y
        sib = (x, y, 1 - c)
        first = []
        for t in range(n):
            for j, (cx, cy) in enumerate(chips):
                first.append(_remote(ins[t].at[:, c], outs[t].at[:, s, c], send_sems, recv_sems, 6 * t + j, (cx, cy, c)))
        for cp in first:
            cp.start()
        passed = []
        for j, (cx, cy) in enumerate(chips):
            sj = 2 * cx + cy
            for t in range(n):
                land = outs[t].at[:, sj, c]
                _remote(land, land, send_sems, recv_sems, 6 * t + j, (cx, cy, c)).wait_recv()
                fw = _remote(land, land, send_sems, recv_sems, 6 * t + 3 + j, sib)
                fw.start()
                passed.append(fw)
        for j, (cx, cy) in enumerate(chips):
            sj = 2 * cx + cy
            for t in range(n):
                land = outs[t].at[:, sj, 1 - c]
                _remote(land, land, send_sems, recv_sems, 6 * t + 3 + j, sib).wait_recv()
        for cp in first + passed:
            cp.wait_send()

    out_shapes = [jax.ShapeDtypeStruct((t.shape[0], N_CHIPS) + t.shape[1:], t.dtype) for t in shards]
    got = _comm_call(body, "gather_weights", shards, out_shapes, 6 * n, 0)
    s = 2 * lax.axis_index("x") + lax.axis_index("y")
    return [lax.dynamic_update_slice(g, t[:, None], (0, s, 0, 0, 0)) for g, t in zip(got, shards)]


def _gather_rider(shards):
    n = len(shards)

    def copies(ins, outs, send_sems, recv_sems):
        x, y, c, chips = _where_am_i()
        s = 2 * x + y
        return [_remote(ins[t].at[:, c], outs[t].at[:, s, c], send_sems, recv_sems, 3 * t + j, (cx, cy, c))
                for t in range(n) for j, (cx, cy) in enumerate(chips)]

    def start(ins, outs, send_sems, recv_sems):
        for cp in copies(ins, outs, send_sems, recv_sems):
            cp.start()

    def finish(ins, outs, send_sems, recv_sems):
        x, y, c, chips = _where_am_i()
        for t in range(n):
            for j, (cx, cy) in enumerate(chips):
                land = outs[t].at[:, 2 * cx + cy, c]
                _remote(land, land, send_sems, recv_sems, 3 * t + j, (cx, cy, c)).wait_recv()
        for cp in copies(ins, outs, send_sems, recv_sems):
            cp.wait_send()

    out_shapes = [jax.ShapeDtypeStruct((t.shape[0], N_CHIPS) + t.shape[1:], t.dtype) for t in shards]
    return dict(name="gather", ins=list(shards), out_shapes=out_shapes, n=3 * n, start=start, finish=finish)


def _gather_forward(landed, shards):
    n = len(landed)

    def body(*refs):
        outs = refs[n:2 * n]
        send_sems, recv_sems, _ = refs[2 * n:]
        x, y, c, chips = _where_am_i()
        sib = (x, y, 1 - c)
        cps = []
        for t in range(n):
            for j, (cx, cy) in enumerate(chips):
                land = outs[t].at[:, 2 * cx + cy, c]
                cps.append(_remote(land, land, send_sems, recv_sems, 3 * t + j, sib))
        for cp in cps:
            cp.start()
        for t in range(n):
            for j, (cx, cy) in enumerate(chips):
                land = outs[t].at[:, 2 * cx + cy, 1 - c]
                _remote(land, land, send_sems, recv_sems, 3 * t + j, sib).wait_recv()
        for cp in cps:
            cp.wait_send()

    got = _pcall_comm(
        body, name="gather_forward", in_specs=[ANY] * n, out_specs=[ANY] * n,
        out_shape=[jax.ShapeDtypeStruct(t.shape, t.dtype) for t in landed], input_output_aliases={t: t for t in range(n)},
        scratch_shapes=[pltpu.SemaphoreType.DMA((3 * n,)), pltpu.SemaphoreType.DMA((3 * n,)), pltpu.SemaphoreType.DMA((1,))],
    )(*landed)
    s = 2 * lax.axis_index("x") + lax.axis_index("y")
    return [lax.dynamic_update_slice(g, t[:, None], (0, s, 0, 0, 0)) for g, t in zip(got, shards)]


def _forward_rider(landed):
    n = len(landed)

    def copies(outs, send_sems, recv_sems):
        x, y, c, chips = _where_am_i()
        cps = []
        for t in range(n):
            for j, (cx, cy) in enumerate(chips):
                land = outs[t].at[:, 2 * cx + cy, c]
                cps.append(_remote(land, land, send_sems, recv_sems, 3 * t + j, (x, y, 1 - c)))
        return cps

    def start(ins, outs, send_sems, recv_sems):
        for cp in copies(outs, send_sems, recv_sems):
            cp.start()

    def finish(ins, outs, send_sems, recv_sems):
        x, y, c, chips = _where_am_i()
        for t in range(n):
            for j, (cx, cy) in enumerate(chips):
                land = outs[t].at[:, 2 * cx + cy, 1 - c]
                _remote(land, land, send_sems, recv_sems, 3 * t + j, (x, y, 1 - c)).wait_recv()
        for cp in copies(outs, send_sems, recv_sems):
            cp.wait_send()

    out_shapes = [jax.ShapeDtypeStruct(t.shape, t.dtype) for t in landed]
    return dict(name="forward", ins=list(landed), out_shapes=out_shapes, n=3 * n, start=start, finish=finish, in_place=True)


def _place_own_shard(got, shards):
    s = 2 * lax.axis_index("x") + lax.axis_index("y")
    return [lax.dynamic_update_slice(g, t[:, None], (0, s, 0, 0, 0)) for g, t in zip(got, shards)]


def _gather_small(v):
    def body(v_ref, out_ref, send_sems, recv_sems, local_sems):
        x, y, c, chips = _where_am_i()
        s = 2 * x + y
        mine = pltpu.make_async_copy(v_ref, out_ref.at[s], local_sems.at[0])
        mine.start()
        sends = [_remote(v_ref, out_ref.at[s], send_sems, recv_sems, j, (cx, cy, c)) for j, (cx, cy) in enumerate(chips)]
        for cp in sends:
            cp.start()
        for j, (cx, cy) in enumerate(chips):
            land = out_ref.at[2 * cx + cy]
            _remote(land, land, send_sems, recv_sems, j, (cx, cy, c)).wait_recv()
        for cp in sends:
            cp.wait_send()
        mine.wait()

    return _comm_call(body, "gather_small", [v], [jax.ShapeDtypeStruct((N_CHIPS,) + v.shape, v.dtype)], 3, 1)[0]


def _swap_sibling_halves(grads):
    n = len(grads)

    def body(*refs):
        ins, outs = refs[:n], refs[n:2 * n]
        send_sems, recv_sems, _ = refs[2 * n:]
        x, y, c, _chips = _where_am_i()
        sib = (x, y, 1 - c)
        cps = [_remote(ins[t].at[:, :, 1 - c], outs[t], send_sems, recv_sems, t, sib) for t in range(n)]
        for cp in cps:
            cp.start()
        for cp in cps:
            cp.wait()

    out_shapes = [jax.ShapeDtypeStruct(g.shape[:2] + g.shape[3:], g.dtype) for g in grads]
    return _comm_call(body, "grad_swap_halves", grads, out_shapes, n, 0)


def _exchange_chips(parts):
    n = len(parts)

    def body(*refs):
        ins, outs = refs[:n], refs[n:2 * n]
        send_sems, recv_sems, _ = refs[2 * n:]
        x, y, c, chips = _where_am_i()
        cps = []
        for t in range(n):
            for j, (cx, cy) in enumerate(chips):
                cps.append(_remote(ins[t].at[:, 2 * cx + cy], outs[t].at[j], send_sems, recv_sems, 3 * t + j, (cx, cy, c)))
        for cp in cps:
            cp.start()
        for cp in cps:
            cp.wait()

    out_shapes = [jax.ShapeDtypeStruct((3, p.shape[0]) + p.shape[2:], p.dtype) for p in parts]
    return _comm_call(body, "grad_exchange_chips", parts, out_shapes, 3 * n, 0)


def _exchange_rider(parts):
    n = len(parts)

    def copies(ins, outs, send_sems, recv_sems):
        x, y, c, chips = _where_am_i()
        return [_remote(ins[t].at[:, 2 * cx + cy], outs[t].at[j], send_sems, recv_sems, 3 * t + j, (cx, cy, c))
                for t in range(n) for j, (cx, cy) in enumerate(chips)]

    def start(ins, outs, send_sems, recv_sems):
        for cp in copies(ins, outs, send_sems, recv_sems):
            cp.start()

    def finish(ins, outs, send_sems, recv_sems):
        for cp in copies(ins, outs, send_sems, recv_sems):
            cp.wait()

    out_shapes = [jax.ShapeDtypeStruct((3, p.shape[0]) + p.shape[2:], p.dtype) for p in parts]
    return dict(name="exchange", ins=list(parts), out_shapes=out_shapes, n=3 * n, start=start, finish=finish)


def _join_sibling_halves(halves):
    n = len(halves)

    def body(*refs):
        ins, outs = refs[:n], refs[n:2 * n]
        send_sems, recv_sems, local_sems = refs[2 * n:]
        x, y, c, _chips = _where_am_i()
        sib = (x, y, 1 - c)
        cps = [_remote(ins[t], outs[t].at[:, c], send_sems, recv_sems, t, sib) for t in range(n)]
        for cp in cps:
            cp.start()
        for t in range(n):
            land = outs[t].at[:, 1 - c]
            _remote(land, land, send_sems, recv_sems, t, sib).wait_recv()
        for cp in cps:
            cp.wait_send()

    out_shapes = [jax.ShapeDtypeStruct((h.shape[0], 2) + h.shape[1:], h.dtype) for h in halves]
    got = _comm_call(body, "grad_join_halves", halves, out_shapes, n, 0)
    c = lax.axis_index("c")
    return [lax.dynamic_update_slice(g, h[:, None], (0, c, 0, 0)) for g, h in zip(got, halves)]


def _join_rider(halves):
    n = len(halves)

    def copies(ins, outs, send_sems, recv_sems):
        x, y, c, _chips = _where_am_i()
        return [_remote(ins[t], outs[t].at[:, c], send_sems, recv_sems, t, (x, y, 1 - c)) for t in range(n)]

    def start(ins, outs, send_sems, recv_sems):
        for cp in copies(ins, outs, send_sems, recv_sems):
            cp.start()

    def finish(ins, outs, send_sems, recv_sems):
        x, y, c, _chips = _where_am_i()
        for t in range(n):
            land = outs[t].at[:, 1 - c]
            _remote(land, land, send_sems, recv_sems, t, (x, y, 1 - c)).wait_recv()
        for cp in copies(ins, outs, send_sems, recv_sems):
            cp.wait_send()

    out_shapes = [jax.ShapeDtypeStruct((h.shape[0], 2) + h.shape[1:], h.dtype) for h in halves]
    return dict(name="join", ins=list(halves), out_shapes=out_shapes, n=n, start=start, finish=finish)


def _place_own_half(got, halves):
    c = lax.axis_index("c")
    return [lax.dynamic_update_slice(g, h[:, None], (0, c, 0, 0)) for g, h in zip(got, halves)]


def _small_exchange_rider(v):
    def copies(ins, outs, send_sems, recv_sems):
        x, y, c, chips = _where_am_i()
        return [_remote(ins[0], outs[0].at[j], send_sems, recv_sems, j, (cx, cy, c)) for j, (cx, cy) in enumerate(chips)]

    def start(ins, outs, send_sems, recv_sems):
        for cp in copies(ins, outs, send_sems, recv_sems):
            cp.start()

    def finish(ins, outs, send_sems, recv_sems):
        for cp in copies(ins, outs, send_sems, recv_sems):
            cp.wait()

    return dict(name="small_exchange", ins=[v], out_shapes=[jax.ShapeDtypeStruct((3,) + v.shape, v.dtype)], n=3,
                start=start, finish=finish)


def _swap_small(v):
    def body(v_ref, out_ref, send_sems, recv_sems, _):
        x, y, c, _chips = _where_am_i()
        cp = _remote(v_ref, out_ref, send_sems, recv_sems, 0, (x, y, 1 - c))
        cp.start()
        cp.wait()

    return _comm_call(body, "small_swap", [v], [jax.ShapeDtypeStruct(v.shape, v.dtype)], 1, 0)[0]


def _sum_rows(name, terms, out_dtypes):
    R, Cc = terms[0].shape
    tm = R if R <= 256 else max(t for t in range(16, 257, 16) if R % t == 0)
    n = len(terms)

    def body(*refs):
        acc = refs[0][...].astype(F32)
        for r in refs[1:n]:
            acc = acc + r[...].astype(F32)
        for o in refs[n:]:
            o[...] = acc.astype(o.dtype)

    blk = pl.BlockSpec((tm, Cc), lambda i: (i, 0))
    return _pcall(
        body, name=name, grid=(R // tm,), in_specs=[blk] * n, out_specs=[blk] * len(out_dtypes),
        out_shape=[jax.ShapeDtypeStruct((R, Cc), d) for d in out_dtypes], compiler_params=_params("parallel"),
    )(*terms)


def _pair_sum(g5, r1, core, shard):
    A4, _, Rh, Cc = g5.shape
    A = A4 // N_CHIPS
    tr = Rh if Rh <= 256 else max(t for t in range(16, 257, 16) if Rh % t == 0)

    def body(core_ref, shard_ref, g_ref, r_ref, qb_ref, qf_ref):
        q = g_ref[...] + r_ref[...]
        qb_ref[...] = q.astype(BF16)

        @pl.when(pl.program_id(2) == shard_ref[0])
        def _():
            qf_ref[...] = q

    grid_spec = pltpu.PrefetchScalarGridSpec(
        num_scalar_prefetch=2, grid=(A, Rh // tr, N_CHIPS),
        in_specs=[pl.BlockSpec((None, None, tr, Cc), lambda a, r, sh, core, shard: (a * N_CHIPS + sh, core[0], r, 0)),
                  pl.BlockSpec((None, tr, Cc), lambda a, r, sh, core, shard: (a * N_CHIPS + sh, r, 0))],
        out_specs=[pl.BlockSpec((None, tr, Cc), lambda a, r, sh, core, shard: (a * N_CHIPS + sh, r, 0)),
                   pl.BlockSpec((None, tr, Cc), lambda a, r, sh, core, shard: (a, r, 0))],
    )
    return _pcall(
        body, name="grad_pair_sum", grid_spec=grid_spec,
        out_shape=[jax.ShapeDtypeStruct((A4, Rh, Cc), BF16), jax.ShapeDtypeStruct((A, Rh, Cc), F32)],
        compiler_params=_params("parallel", "parallel", "arbitrary"),
    )(core, shard, g5, r1)


def _swap_rider(grads):
    n = len(grads)

    def copies(ins, outs, send_sems, recv_sems):
        x, y, c, _chips = _where_am_i()
        return [_remote(ins[t].at[:, :, 1 - c], outs[t], send_sems, recv_sems, t, (x, y, 1 - c)) for t in range(n)]

    def start(ins, outs, send_sems, recv_sems):
        for cp in copies(ins, outs, send_sems, recv_sems):
            cp.start()

    def finish(ins, outs, send_sems, recv_sems):
        for cp in copies(ins, outs, send_sems, recv_sems):
            cp.wait()

    out_shapes = [jax.ShapeDtypeStruct(g.shape[:2] + g.shape[3:], g.dtype) for g in grads]
    return dict(name="swap", ins=list(grads), out_shapes=out_shapes, n=n, start=start, finish=finish)


def _reduce_chip(grads, r1, core, shard):
    qb, qf = [], []
    for g, r in zip(grads, r1):
        A, _, _, Rh, Cc = g.shape
        b, f = _pair_sum(g.reshape(A * N_CHIPS, 2, Rh, Cc), r.reshape(A * N_CHIPS, Rh, Cc), core, shard)
        qb.append(b.reshape(A, N_CHIPS, Rh, Cc))
        qf.append(f)
    return qb, qf


def _reduce_finish(qf, r2):
    return _as_shards(_join_sibling_halves(_chip_sums(qf, r2)))


def _chip_sums(qf, r2):
    halves = []
    for f, r in zip(qf, r2):
        A, Rh, Cc = f.shape
        terms = [f.reshape(A * Rh, Cc)] + [r[j].reshape(A * Rh, Cc) for j in range(3)]
        halves.append(_sum_rows("grad_chip_sum", terms, [F32])[0].reshape(A, Rh, Cc))
    return halves


def _as_shards(full):
    return [t.reshape(t.shape[0], 2 * t.shape[2], t.shape[3]) for t in full]


def _small_pair(v):
    return _sum_rows("small_pair_sum", [v, _swap_small(v)], [F32])[0]


def _small_chip_sum(pair, others):
    x, y = lax.axis_index("x"), lax.axis_index("y")
    s = 2 * x + y
    stack = jnp.concatenate([pair[None], others], axis=0)
    src = jnp.stack([s, s ^ 2, s ^ 1, s ^ 3])
    order = jnp.argsort(src)
    terms = [lax.dynamic_index_in_dim(stack, order[k], 0, keepdims=False) for k in range(N_CHIPS)]
    return _sum_rows("small_chip_sum", terms, [F32])[0]


def _block_diag(w):
    nb, bw, _ = w.shape
    per = LANES // bw
    w = w.reshape(nb // per, per, bw, bw)
    eye = jnp.eye(per, dtype=w.dtype)
    bd = jnp.einsum("tpij,pq->tpiqj", w, eye).reshape(nb // per, LANES, LANES)
    return bd.astype(BF16)


def _block_diag_grad(g, bw):
    nt = g.shape[0]
    per = LANES // bw
    g = g.reshape(nt, per, bw, per, bw)
    return jnp.stack([g[:, p, :, p, :] for p in range(per)], axis=1).reshape(nt * per, bw, bw)


def _split5(w):
    R, Cc = w.shape[-2:]
    return w.reshape(-1, 2, R // 2, Cc)


def kernel(x, w_in, conv_w, conv_b, w_rg, b_rg, w_ig, b_ig, lru_lambda, sinks, w_branch, w_out, ln1_g, ln1_b, w_ffn_in, w_ffn_out, ln2_g, ln2_b, loss_target, m_w_in, m_conv_w, m_conv_b, m_w_rg, m_b_rg, m_w_ig, m_b_ig, m_lru_lambda, m_sinks, m_w_branch, m_w_out, m_ln1_g, m_ln1_b, m_w_ffn_in, m_w_ffn_out, m_ln2_g, m_ln2_b, v_w_in, v_conv_w, v_conv_b, v_w_rg, v_b_rg, v_w_ig, v_b_ig, v_lru_lambda, v_sinks, v_w_branch, v_w_out, v_ln1_g, v_ln1_b, v_w_ffn_in, v_w_ffn_out, v_ln2_g, v_ln2_b):
    B, S, D = x.shape
    T = B * S
    L = w_in.shape[0]
    H = D // HEAD_DIM
    KVB = D // SWA_GROUP
    FH = w_ffn_out.shape[1] * N_CHIPS
    C = w_in.shape[2] * N_CHIPS
    alpha = (2.0 * L) ** 0.25
    off = {}
    pos = 0
    for nm, wd in (("lx", D), ("lg", D), ("qb", D), ("kb", KVB), ("vb", KVB), ("qc", D), ("kc", D), ("vc", D), ("gt", 3 * D)):
        off[nm] = pos
        pos += wd
    assert pos == C
    cx, cy, cc = lax.axis_index("x"), lax.axis_index("y"), lax.axis_index("c")
    shard = (2 * cx + cy).astype(jnp.int32)
    core_a = cc.astype(jnp.int32).reshape(1)
    shard_a = shard.reshape(1)

    def shard_views(l):
        return [_split5(w_in[l].astype(BF16)), _split5(w_branch[l].astype(BF16)), _split5(w_out[l].astype(BF16)),
                _split5(w_ffn_in[l].astype(BF16)), _split5(w_ffn_out[l].astype(BF16))]

    def as_weights(g):
        return dict(
            w_in=g[0].reshape(N_CHIPS, D, C // N_CHIPS),
            w_branch=g[1].reshape(3, D, D),
            w_out=g[2].reshape(D, D),
            w_ffn_in=g[3].reshape(N_CHIPS, D, 2 * FH // N_CHIPS),
            w_ffn_out=g[4].reshape(FH, D),
        )

    first_views = shard_views(0)
    w_in0 = _gather_weights(first_views[:1])
    full = [dict(w_in=w_in0[0].reshape(N_CHIPS, D, C // N_CHIPS))]
    cw_all = _gather_small(conv_w.reshape(L * CONV_WIDTH, D // N_CHIPS))
    conv_w_full = jnp.transpose(cw_all, (1, 0, 2)).reshape(L, CONV_WIDTH, D)

    def layer_params(l):
        return dict(conv_w=conv_w_full[l], conv_b=conv_b[l][None], w_rg_bd=_block_diag(w_rg[l]), b_rg=b_rg[l][None],
                    w_ig_bd=_block_diag(w_ig[l]), b_ig=b_ig[l][None], lam=lru_lambda[l][None])

    def sink_rows(l, hb):
        sk = sinks[l].reshape(H // hb, 1, hb)
        return jnp.pad(sk, ((0, 0), (0, 0), (0, LANES - hb)))

    hb_b = SWA_HB

    saved = []
    xin = x.reshape(T, D)
    for l in range(L):
        fw, lp = full[l], layer_params(l)
        nxt = shard_views(l + 1) if l + 1 < L else None
        own, ahead = {}, {}
        if l == 0:
            own = {"lru": (3,), "swa": (1, 2, 4)}
            ahead = {"proj": (0,), "dil": (1, 2, 3, 4)} if nxt is not None else {}
        elif nxt is not None:
            ahead = {"lru": (3,), "swa": (0,), "dil": (1, 2, 4)}
        landed_own, landed_next = {}, {}

        def carried(host):
            idx_own, idx_next = own.get(host, ()), ahead.get(host, ())
            views = [first_views[t] for t in idx_own] + [nxt[t] for t in idx_next]
            if not views:
                return None, lambda bufs: None

            def file(bufs):
                for t, buf in zip(idx_own, bufs[:len(idx_own)]):
                    landed_own[t] = buf
                for t, buf in zip(idx_next, bufs[len(idx_own):]):
                    landed_next[t] = buf
            return _gather_rider(views), file

        rider, file = carried("proj")
        proj_kw = dict(mode="nn", name="mm_proj", tm=512, n_outer=True)
        if rider is None:
            proj = _matmul(xin, fw["w_in"], **proj_kw)
        else:
            proj, bufs = _matmul(xin, fw["w_in"], rider=rider, **proj_kw)
            file(bufs)
        proj3 = proj.reshape(B, S, C)
        rider, file = carried("lru")
        res = _lru_fwd(proj3, lp, D=D, x_off=off["lx"], g_off=off["lg"], rider=rider)
        h3, ya3 = res[0], res[1]
        file(res[2:])
        skr = sink_rows(l, hb_b)
        rider, file = carried("swa")
        res = _swa_seq_fwd(proj3, skr, D=D, q_off=off["qb"], k_off=off["kb"], v_off=off["vb"], rider=rider)
        yb3, lse_b = res[0], res[1]
        file(res[2:])
        if l == 0:
            rest = sorted(landed_own)
            got = _gather_forward([landed_own[t] for t in rest], [first_views[t] for t in rest])
            fw = as_weights(w_in0 + got)
            full[0] = fw
        rider, file = carried("dil")
        res = _dil_fwd(proj3, D=D, q_off=off["qc"], k_off=off["kc"], v_off=off["vc"], rider=rider)
        yc3, lse_c = res[0], res[1]
        file(res[2:])
        ya, yb, yc = ya3.reshape(T, D), yb3.reshape(T, D), yc3.reshape(T, D)
        if nxt is not None:
            landed = [landed_next[t] for t in range(len(nxt))]
            res = _branch_fwd([ya, yb, yc], fw["w_branch"], proj, D=D, g_off=off["gt"], rider=_forward_rider(landed))
            branch, merged = res[0], res[1]
            full.append(as_weights(_place_own_shard(res[2:], nxt)))
        else:
            branch, merged = _branch_fwd([ya, yb, yc], fw["w_branch"], proj, D=D, g_off=off["gt"])
        z1, x1 = _matmul(merged, fw["w_out"], mode="nn", name="mm_out_ln", tn=1024, resid=xin, rs=alpha,
                         ln=(ln1_g[l][None], ln1_b[l][None]))
        ffn_h1, ffn_h3, f = _ffn_in_fwd(x1, fw["w_ffn_in"])
        z2, x2 = _matmul(f, fw["w_ffn_out"], mode="nn", name="mm_ffn_out_ln", tn=1024, tk=4096, resid=x1, rs=alpha,
                         ln=(ln2_g[l][None], ln2_b[l][None]))
        saved.append(dict(x=xin, proj=proj, h3=h3, ya=ya, yb=yb, lse_b=lse_b, yc=yc, lse_c=lse_c, branch=branch,
                          merged=merged, z1=z1, x1=x1, ffn_h1=ffn_h1, ffn_h3=ffn_h3, f=f, z2=z2, skr=skr))
        xin = x2

    dx, loss_rows = _loss_head(xin, loss_target.reshape(T, D))
    loss = lax.psum(jnp.sum(loss_rows), ("x", "y", "c"))

    big = {k: [None] * L for k in ("w_in", "w_branch", "w_out", "w_ffn_in", "w_ffn_out")}
    small = [None] * L

    def store_reduced(l, red):
        big["w_in"][l] = red[0].reshape(D, C // N_CHIPS)
        big["w_branch"][l] = red[1].reshape(3, D // N_CHIPS, D)
        big["w_out"][l] = red[2].reshape(D // N_CHIPS, D)
        big["w_ffn_in"][l] = red[3].reshape(D, 2 * FH // N_CHIPS)
        big["w_ffn_out"][l] = red[4].reshape(FH // N_CHIPS, D)

    above = None
    pending = None
    for l in reversed(range(L)):
        fw, lp, sv = full[l], layer_params(l), saved[l]
        dz2, dg2, db2 = _ln_bwd(dx, sv["z2"], ln2_g[l][None])
        g_ffn_out = _matmul(sv["f"], dz2, mode="tn", name="mm_dffn_out_w", tm=1408, tn=1024, tk=1024)
        dhh = _ffn_out_bwd(dz2, fw["w_ffn_out"], sv["ffn_h1"], sv["ffn_h3"])
        dx1 = _matmul(dhh, fw["w_ffn_in"], mode="nt", name="mm_dffn_in_x", tm=1024, tn=1024, resid=dz2, rs=alpha)
        g_ffn_in = _matmul(sv["x1"], dhh, mode="tn", name="mm_dffn_in_w", tm=1024, tk=1024, out_shards=N_CHIPS)
        dz1, dg1, db1 = _ln_bwd(dx1, sv["z1"], ln1_g[l][None])
        g_out = _matmul(sv["merged"], dz1, mode="tn", name="mm_dout_w", tm=1024, tn=1024, tk=1024)
        dbranch, dgates = _branch_bwd(dz1, fw["w_out"], sv["branch"], sv["proj"], D=D, g_off=off["gt"])
        ys = [sv["ya"], sv["yb"], sv["yc"]]
        dys, g_branch = [], []
        for n in range(3):
            dys.append(_matmul(dbranch, fw["w_branch"][n], mode="nt", name="mm_dbranch_x", tn=1024, tk=1024, a_pick=n))
            g_branch.append(_matmul(ys[n], dbranch, mode="tn", name="mm_dbranch_w", tm=1024, tn=1024, tk=1024, b_pick=n))
        proj3 = sv["proj"].reshape(B, S, C)
        r3 = lambda t: t.reshape(B, S, t.shape[-1])
        lru = _lru_bwd(proj3, sv["h3"], r3(dys[0]), lp, D=D, x_off=off["lx"], g_off=off["lg"],
                       rider=None if above is None else _swap_rider(above[1]))
        if above is not None:
            pending = (above[0],) + _reduce_chip(above[1], lru[9:], core_a, shard_a)
        dxr, dgate = lru[0], lru[1]
        dqb, dkb, dvb, dsk = _swa_seq_bwd(proj3, r3(sv["yb"]), sv["lse_b"], r3(dys[1]), sv["skr"], D=D, q_off=off["qb"],
                                      k_off=off["kb"], v_off=off["vb"])
        dil_kw = dict(D=D, q_off=off["qc"], k_off=off["kc"], v_off=off["vc"])
        if pending is None:
            acc = _dil_bwd(proj3, r3(sv["yc"]), sv["lse_c"], r3(dys[2]), **dil_kw)
        else:
            res = _dil_bwd(proj3, r3(sv["yc"]), sv["lse_c"], r3(dys[2]), rider=_exchange_rider(pending[1]), **dil_kw)
            acc = res[:3]
            halves = _chip_sums(pending[2], res[3:])
        f2 = lambda t: t.reshape(T, t.shape[-1]).astype(BF16)
        dproj = jnp.concatenate([f2(dxr), f2(dgate), f2(dqb), f2(dkb), f2(dvb), f2(acc[0]), f2(acc[1]), f2(acc[2])] + dgates, axis=1)
        dx_kw = dict(mode="nt", name="mm_dproj_x", tm=1024, tn=1024, resid=dz1, rs=alpha)
        if pending is None:
            dx = _matmul(dproj, fw["w_in"], **dx_kw)
        else:
            dx, got = _matmul(dproj, fw["w_in"], rider=_join_rider(halves), **dx_kw)
            store_reduced(pending[0], _as_shards(_place_own_half(got, halves)))
        g_in = _matmul(sv["x"], dproj, mode="tn", name="mm_dproj_w", tm=512, tk=1024, out_shards=N_CHIPS)

        g5 = [g_in.reshape(1, N_CHIPS, 2, D // 2, C // N_CHIPS),
              jnp.stack(g_branch).reshape(3, N_CHIPS, 2, D // N_CHIPS // 2, D),
              g_out.reshape(1, N_CHIPS, 2, D // N_CHIPS // 2, D),
              g_ffn_in.reshape(1, N_CHIPS, 2, D // 2, 2 * FH // N_CHIPS),
              g_ffn_out.reshape(1, N_CHIPS, 2, FH // N_CHIPS // 2, D)]
        above = (l, g5)

        dsinks = jnp.sum(dsk, axis=0)[:, 0, :hb_b].reshape(H)
        bw = w_rg.shape[-1]
        small[l] = [lru[2].reshape(-1), lru[3].reshape(-1), _block_diag_grad(lru[4], bw).reshape(-1), lru[5].reshape(-1),
                    _block_diag_grad(lru[6], bw).reshape(-1), lru[7].reshape(-1), lru[8].reshape(-1),
                    jnp.pad(dsinks, (0, LANES - H)), dg1.reshape(-1), db1.reshape(-1), dg2.reshape(-1), db2.reshape(-1)]

    qb, qf = _reduce_chip(above[1], _swap_sibling_halves(above[1]), core_a, shard_a)
    store_reduced(above[0], _reduce_finish(qf, _exchange_chips(qb)))

    sizes = [t.size for t in small[0]]
    flat = jnp.concatenate([t for l in range(L) for t in small[l]])
    n_flat = flat.size
    rows = -(-n_flat // (LANES * 256)) * 256
    flat = jnp.pad(flat, (0, rows * LANES - n_flat)).reshape(rows, LANES)
    pair = _small_pair(flat)

    order = ["w_in", "conv_w", "conv_b", "w_rg", "b_rg", "w_ig", "b_ig", "lru_lambda", "sinks", "w_branch", "w_out",
             "ln1_g", "ln1_b", "w_ffn_in", "w_ffn_out", "ln2_g", "ln2_b"]
    weights = dict(w_in=w_in, conv_w=conv_w, conv_b=conv_b, w_rg=w_rg, b_rg=b_rg, w_ig=w_ig, b_ig=b_ig, lru_lambda=lru_lambda,
                   sinks=sinks, w_branch=w_branch, w_out=w_out, ln1_g=ln1_g, ln1_b=ln1_b, w_ffn_in=w_ffn_in,
                   w_ffn_out=w_ffn_out, ln2_g=ln2_g, ln2_b=ln2_b)
    ms = dict(w_in=m_w_in, conv_w=m_conv_w, conv_b=m_conv_b, w_rg=m_w_rg, b_rg=m_b_rg, w_ig=m_w_ig, b_ig=m_b_ig,
              lru_lambda=m_lru_lambda, sinks=m_sinks, w_branch=m_w_branch, w_out=m_w_out, ln1_g=m_ln1_g, ln1_b=m_ln1_b,
              w_ffn_in=m_w_ffn_in, w_ffn_out=m_w_ffn_out, ln2_g=m_ln2_g, ln2_b=m_ln2_b)
    vs = dict(w_in=v_w_in, conv_w=v_conv_w, conv_b=v_conv_b, w_rg=v_w_rg, b_rg=v_b_rg, w_ig=v_w_ig, b_ig=v_b_ig,
              lru_lambda=v_lru_lambda, sinks=v_sinks, w_branch=v_w_branch, w_out=v_w_out, ln1_g=v_ln1_g, ln1_b=v_ln1_b,
              w_ffn_in=v_w_ffn_in, w_ffn_out=v_w_ffn_out, ln2_g=v_ln2_g, ln2_b=v_ln2_b)
    grads = dict(w_in=jnp.stack(big["w_in"]), w_branch=jnp.stack(big["w_branch"]), w_out=jnp.stack(big["w_out"]),
                 w_ffn_in=jnp.stack(big["w_ffn_in"]), w_ffn_out=jnp.stack(big["w_ffn_out"]))
    deltas, new_m, new_v = {}, {}, {}
    deltas["w_in"], new_m["w_in"], new_v["w_in"], others = _adamw(w_in, grads["w_in"], m_w_in, v_w_in,
                                                                  rider=_small_exchange_rider(pair))
    red_small = _small_chip_sum(pair, others).reshape(-1)
    per_layer = sum(sizes)
    names = ["conv_w", "conv_b", "w_rg", "b_rg", "w_ig", "b_ig", "lru_lambda", "sinks", "ln1_g", "ln1_b", "ln2_g", "ln2_b"]
    sg = {nm: [] for nm in names}
    for l in range(L):
        p = l * per_layer
        for nm, sz in zip(names, sizes):
            sg[nm].append(red_small[p:p + sz])
            p += sz
    grads.update(
        conv_w=lax.dynamic_slice_in_dim(jnp.stack(sg["conv_w"]).reshape(L, CONV_WIDTH, D), shard * (D // N_CHIPS), D // N_CHIPS, axis=2),
        conv_b=jnp.stack(sg["conv_b"]), w_rg=jnp.stack(sg["w_rg"]).reshape(w_rg.shape), b_rg=jnp.stack(sg["b_rg"]),
        w_ig=jnp.stack(sg["w_ig"]).reshape(w_ig.shape), b_ig=jnp.stack(sg["b_ig"]), lru_lambda=jnp.stack(sg["lru_lambda"]),
        sinks=jnp.stack(sg["sinks"])[:, :H], ln1_g=jnp.stack(sg["ln1_g"]), ln1_b=jnp.stack(sg["ln1_b"]),
        ln2_g=jnp.stack(sg["ln2_g"]), ln2_b=jnp.stack(sg["ln2_b"]),
    )

    for nm in order:
        if nm not in deltas:
            deltas[nm], new_m[nm], new_v[nm] = _adamw(weights[nm], grads[nm], ms[nm], vs[nm])
    return (loss, dx.reshape(B, S, D), *[grads[nm] for nm in order], *[deltas[nm] for nm in order],
            *[new_m[nm] for nm in order], *[new_v[nm] for nm in order])
```

```python
import math

import jax
import jax.numpy as jnp
from jax import lax
from jax.experimental import pallas as pl
from jax.experimental.pallas import tpu as pltpu

HEAD_DIM = 64
WIN = 128
DILS = (1, 4, 16)
SWA_GROUP = 4
CONV_WIDTH = 4
LRU_C = 8.0
LN_EPS = 1e-5
NEG_INF = -1e30
N_CHIPS = 4
ADAM_LR, ADAM_B1, ADAM_B2, ADAM_EPS, ADAM_WD, ADAM_STEP = 0.001, 0.9, 0.999, 1e-08, 0.01, 10

LANES = 128
SUBLANES = 8
VMEM_LIMIT = 48 * 1024 * 1024

assert math.log2(HEAD_DIM) % 2 == 0

F32 = jnp.float32
BF16 = jnp.bfloat16
MESH = pl.DeviceIdType.MESH
ANY = pl.BlockSpec(memory_space=pl.ANY)


def _pcall(body, **kw):
    return pl.pallas_call(body, **kw)


def _pcall_comm(body, **kw):
    return pl.pallas_call(body, **kw)


def _params(*sem):
    return pltpu.CompilerParams(dimension_semantics=tuple(sem), vmem_limit_bytes=VMEM_LIMIT)


def _tile(dim, target):
    if dim <= target:
        return dim
    best = None
    for t in range(LANES, target + 1, LANES):
        if dim % t == 0:
            best = t
    assert best is not None, (dim, target)
    return best


def _sigmoid(x):
    return 1.0 / (1.0 + jnp.exp(-x))


def _dot(a, b, dims):
    return lax.dot_general(a, b, (dims, ((), ())), preferred_element_type=F32)


def _dot_nn(a, b):
    return _dot(a, b, ((1,), (0,)))


def _dot_nt(a, b):
    return _dot(a, b, ((1,), (1,)))


def _dot_tn(a, b):
    return _dot(a, b, ((0,), (0,)))


def _matmul(a, b, *, mode, name, out_dtype=F32, tm=512, tn=512, tk=2048, resid=None, rs=1.0, out_shards=0, n_outer=False,
            ln=None, a_pick=0, b_pick=0, rider=None):
    b_sh = b.ndim == 3
    a_st = a.ndim == 3
    if mode == "nn":
        M, K = a.shape[-2:]
        N = b.shape[-1] * (b.shape[0] if b_sh else 1)
    elif mode == "nt":
        M, K = a.shape[-2:]
        N = b.shape[-2]
    else:
        K, M = a.shape
        N = b.shape[-1]
    tm = _tile(M, tm)
    if mode == "nn" and b_sh:
        tn = b.shape[-1]
    elif out_shards:
        tn = N // out_shards
    else:
        tn = _tile(N, tn)
    if mode == "nt" and b_sh:
        tk = b.shape[-1]
    else:
        tk = _tile(K, tk)
    nk = K // tk
    grid = (N // tn, M // tm, nk) if n_outer else (M // tm, N // tn, nk)

    def spec(shape, f):
        return pl.BlockSpec(shape, (lambda g0, g1, k: f(g1, g0, k)) if n_outer else f)

    a_rows = spec((None, tm, tk), lambda i, j, k: (a_pick, i, k)) if a_st else spec((tm, tk), lambda i, j, k: (i, k))
    if mode == "nn":
        a_spec = a_rows
        b_spec = spec((None, tk, tn), lambda i, j, k: (j, k, 0)) if b_sh else spec((tk, tn), lambda i, j, k: (k, j))
        contract = _dot_nn
    elif mode == "nt":
        a_spec = a_rows
        b_spec = spec((None, tn, tk), lambda i, j, k: (k, j, 0)) if b_sh else spec((tn, tk), lambda i, j, k: (j, k))
        contract = _dot_nt
    else:
        a_spec = spec((tk, tm), lambda i, j, k: (k, i))
        b_spec = spec((None, tk, tn), lambda i, j, k: (b_pick, k, j)) if b_sh else spec((tk, tn), lambda i, j, k: (k, j))
        contract = _dot_tn
    if out_shards:
        out_shape = jax.ShapeDtypeStruct((out_shards, M, tn), out_dtype)
        o_spec = spec((None, tm, tn), lambda i, j, k: (j, i, 0))
    else:
        out_shape = jax.ShapeDtypeStruct((M, N), out_dtype)
        o_spec = spec((tm, tn), lambda i, j, k: (i, j))
    in_specs = [a_spec, b_spec]
    args = [a, b]
    if resid is not None:
        in_specs.append(spec((tm, tn), lambda i, j, k: (i, j)))
        args.append(resid)
    if ln is not None:
        assert tn == N and resid is not None and not out_shards
        in_specs += [spec((1, N), lambda i, j, k: (0, 0))] * 2
        args += list(ln)
        out_shape = [out_shape, out_shape]
        o_spec = [o_spec, o_spec]
    n_in = len(args)

    def body(*refs):
        a_ref, b_ref = refs[:2]
        r_ref = refs[2] if resid is not None else None
        o_ref = refs[n_in]
        part = contract(a_ref[...].astype(BF16), b_ref[...].astype(BF16))

        def finish(res):
            if resid is not None:
                res = res + rs * r_ref[...]
            o_ref[...] = res.astype(out_dtype)
            if ln is not None:
                g_ref, bb_ref, y_ref = refs[n_in - 2], refs[n_in - 1], refs[n_in + 1]
                zc = res - jnp.mean(res, axis=1, keepdims=True)
                var = jnp.mean(zc * zc, axis=1, keepdims=True)
                y_ref[...] = zc * lax.rsqrt(var + LN_EPS) * g_ref[...] + bb_ref[...]

        if nk == 1:
            finish(part)
            return
        acc_ref = refs[-1]
        k = pl.program_id(2)

        @pl.when(k == 0)
        def _():
            acc_ref[...] = part

        @pl.when(jnp.logical_and(k > 0, k < nk - 1))
        def _():
            acc_ref[...] += part

        @pl.when(k == nk - 1)
        def _():
            finish(acc_ref[...] + part)

    if rider is None:
        return _pcall(
            body, name=name, grid=grid, in_specs=in_specs, out_specs=o_spec, out_shape=out_shape,
            scratch_shapes=[pltpu.VMEM((tm, tn), F32)] if nk > 1 else [],
            compiler_params=_params("parallel", "parallel", "arbitrary"),
        )(*args)
    assert ln is None
    res = _call_with_rider(
        rider, body, name=name, grid=grid, in_specs=in_specs, out_specs=[o_spec], out_shape=[out_shape],
        scratch_shapes=[pltpu.VMEM((tm, tn), F32)] if nk > 1 else [], args=args)
    return res[0], list(res[1:])


def _shift_down(x, d, row):
    return jnp.where(row >= d, pltpu.roll(x, d, 0), 0.0)


def _shift_up(x, d, row, n):
    return jnp.where(row < n - d, pltpu.roll(x, n - d, 0), 0.0)


def _log1p(u):
    w = 1.0 + u
    return jnp.where(w == 1.0, u, jnp.log(w) * u / (w - 1.0))


def _gelu_parts(g):
    k = math.sqrt(2.0 / math.pi)
    c = 0.044715
    t = jnp.tanh(k * (g + c * g * g * g))
    val = 0.5 * g * (1.0 + t)
    der = 0.5 * (1.0 + t) + 0.5 * g * (1.0 - t * t) * k * (1.0 + 3.0 * c * g * g)
    return val, der


def _lru_gates(xr, cw_ref, cb_ref, wrg_ref, brg_ref, wig_ref, big_ref, lam_ref, row):
    xc = cw_ref[3:4, :] * xr + cb_ref[...]
    for d in range(1, CONV_WIDTH):
        xc = xc + cw_ref[3 - d:4 - d, :] * _shift_down(xr, d, row)
    xcb = xc.astype(BF16)
    r = _sigmoid(_dot_nn(xcb, wrg_ref[...]) + brg_ref[...])
    ig = _sigmoid(_dot_nn(xcb, wig_ref[...]) + big_ref[...])
    lam = lam_ref[...]
    sp = jnp.maximum(-lam, 0.0) + _log1p(jnp.exp(-jnp.abs(lam)))
    log_a = (-LRU_C) * r * sp
    a = jnp.exp(log_a)
    y2 = 2.0 * log_a
    one_m_a2 = jnp.where(y2 > -0.01, -(y2 + 0.5 * y2 * y2 + (1.0 / 6.0) * y2 * y2 * y2), 1.0 - jnp.exp(y2))
    mult = jnp.sqrt(one_m_a2)
    return xc, r, ig, sp, a, mult


def _scan_local(a, b, row, n, reverse):
    sub = row % SUBLANES
    d = 1
    while d < SUBLANES:
        if reverse:
            keep = sub < SUBLANES - d
            a_s = jnp.where(keep, pltpu.roll(a, n - d, 0), 1.0)
            b_s = jnp.where(keep, pltpu.roll(b, n - d, 0), 0.0)
        else:
            keep = sub >= d
            a_s = jnp.where(keep, pltpu.roll(a, d, 0), 1.0)
            b_s = jnp.where(keep, pltpu.roll(b, d, 0), 0.0)
        b = a * b_s + b
        a = a * a_s
        d *= 2
    return a, b


def _scan_carry(a_ref, b_ref, out_ref, n, reverse):
    ng = n // SUBLANES

    def step(gidx, carry):
        g = (ng - 1 - gidx) if reverse else gidx
        rows = pl.ds(pl.multiple_of(g * SUBLANES, SUBLANES), SUBLANES)
        h = a_ref[rows, :] * carry + b_ref[rows, :]
        out_ref[rows, :] = h
        return h[0:1, :] if reverse else h[SUBLANES - 1:SUBLANES, :]

    lax.fori_loop(0, ng, step, jnp.zeros((1, LANES), F32), unroll=8)


def _lru_specs(B, S, D, C, x_off, g_off):
    nct = D // LANES
    seq = lambda off: pl.BlockSpec((None, S, LANES), lambda ct, b: (b, 0, off // LANES + ct))
    row = lambda r: pl.BlockSpec((r, LANES), lambda ct, b: (0, ct))
    wbd = pl.BlockSpec((None, LANES, LANES), lambda ct, b: (ct, 0, 0))
    return nct, seq, row, wbd


def _lru_fwd(proj3, lp, *, D, x_off, g_off, rider=None):
    B, S, C = proj3.shape
    nct, seq, row, wbd = _lru_specs(B, S, D, C, x_off, g_off)

    def body(xr_ref, g_ref, cw_ref, cb_ref, wrg_ref, brg_ref, wig_ref, big_ref, lam_ref, h_ref, ya_ref, a_s, b_s):
        rowi = lax.broadcasted_iota(jnp.int32, (S, LANES), 0)
        xr = xr_ref[...]
        xc, r, ig, sp, a, mult = _lru_gates(xr, cw_ref, cb_ref, wrg_ref, brg_ref, wig_ref, big_ref, lam_ref, rowi)
        al, bl = _scan_local(a, mult * (ig * xc), rowi, S, False)
        a_s[...] = al
        b_s[...] = bl
        _scan_carry(a_s, b_s, h_ref, S, False)
        gel, _ = _gelu_parts(g_ref[...])
        ya_ref[...] = (h_ref[...] * gel).astype(BF16)

    out_seq = pl.BlockSpec((None, S, LANES), lambda ct, b: (b, 0, ct))
    return _call_with_rider(
        rider, body, name="lru_fwd", grid=(nct, B),
        in_specs=[seq(x_off), seq(g_off), row(CONV_WIDTH), row(1), wbd, row(1), wbd, row(1), row(1)],
        out_specs=[out_seq, out_seq],
        out_shape=[jax.ShapeDtypeStruct((B, S, D), F32), jax.ShapeDtypeStruct((B, S, D), BF16)],
        scratch_shapes=[pltpu.VMEM((S, LANES), F32), pltpu.VMEM((S, LANES), F32)],
        args=[proj3, proj3, lp["conv_w"], lp["conv_b"], lp["w_rg_bd"], lp["b_rg"], lp["w_ig_bd"], lp["b_ig"], lp["lam"]])


def _lru_bwd(proj3, h3, dya3, lp, *, D, x_off, g_off, rider=None):
    B, S, C = proj3.shape
    nct, seq, row, wbd = _lru_specs(B, S, D, C, x_off, g_off)

    def body(xr_ref, g_ref, h_ref, dy_ref, cw_ref, cb_ref, wrg_ref, brg_ref, wig_ref, big_ref, lam_ref,
             dxr_ref, dg_ref, dcw_ref, dcb_ref, dwrg_ref, dbrg_ref, dwig_ref, dbig_ref, dlam_ref, a_s, b_s, l_s):
        first = pl.program_id(1) == 0
        rowi = lax.broadcasted_iota(jnp.int32, (S, LANES), 0)
        xr = xr_ref[...]
        xc, r, ig, sp, a, mult = _lru_gates(xr, cw_ref, cb_ref, wrg_ref, brg_ref, wig_ref, big_ref, lam_ref, rowi)
        h = h_ref[...]
        dy = dy_ref[...]
        gel, dgel = _gelu_parts(g_ref[...])
        dg_ref[...] = (dy * h * dgel).astype(BF16)
        al, bl = _scan_local(_shift_up(a, 1, rowi, S), dy * gel, rowi, S, True)
        a_s[...] = al
        b_s[...] = bl
        _scan_carry(a_s, b_s, l_s, S, True)
        lamb = l_s[...]
        u = ig * xc
        da = lamb * _shift_down(h, 1, rowi)
        dlog_a = da * a - (lamb * u) * (a * a) / mult
        du = lamb * mult
        dpre_r = (dlog_a * ((-LRU_C) * sp)) * r * (1.0 - r)
        dpre_i = (du * xc) * ig * (1.0 - ig)
        dsp = jnp.sum(dlog_a * ((-LRU_C) * r), axis=0, keepdims=True)
        dlam = dsp * (-1.0 / (1.0 + jnp.exp(lam_ref[...])))
        dpr = dpre_r.astype(BF16)
        dpi = dpre_i.astype(BF16)
        dxc = du * ig + _dot_nt(dpr, wrg_ref[...]) + _dot_nt(dpi, wig_ref[...])
        xcb = xc.astype(BF16)
        dwrg = _dot_tn(xcb, dpr)
        dwig = _dot_tn(xcb, dpi)
        dxr = cw_ref[3:4, :] * dxc
        dcw = [jnp.sum(xr * dxc, axis=0, keepdims=True)]
        for d in range(1, CONV_WIDTH):
            dxr = dxr + cw_ref[3 - d:4 - d, :] * _shift_up(dxc, d, rowi, S)
            dcw.append(jnp.sum(_shift_down(xr, d, rowi) * dxc, axis=0, keepdims=True))
        dxr_ref[...] = dxr.astype(BF16)
        dcw_rows = jnp.concatenate(dcw[::-1], axis=0)
        sums = ((dcw_ref, dcw_rows), (dcb_ref, jnp.sum(dxc, axis=0, keepdims=True)), (dwrg_ref, dwrg),
                (dbrg_ref, jnp.sum(dpre_r, axis=0, keepdims=True)), (dwig_ref, dwig),
                (dbig_ref, jnp.sum(dpre_i, axis=0, keepdims=True)), (dlam_ref, dlam))

        @pl.when(first)
        def _():
            for ref, val in sums:
                ref[...] = val

        @pl.when(jnp.logical_not(first))
        def _():
            for ref, val in sums:
                ref[...] += val

    out_seq = pl.BlockSpec((None, S, LANES), lambda ct, b: (b, 0, ct))
    f = lambda shape: jax.ShapeDtypeStruct(shape, F32)
    nb = D // LANES
    return _call_with_rider(
        rider, body, name="lru_bwd", grid=(nct, B),
        in_specs=[seq(x_off), seq(g_off), out_seq, out_seq, row(CONV_WIDTH), row(1), wbd, row(1), wbd, row(1), row(1)],
        out_specs=[out_seq, out_seq, row(CONV_WIDTH), row(1), wbd, row(1), wbd, row(1), row(1)],
        out_shape=[jax.ShapeDtypeStruct((B, S, D), BF16), jax.ShapeDtypeStruct((B, S, D), BF16),
                   f((CONV_WIDTH, D)), f((1, D)), f((nb, LANES, LANES)), f((1, D)), f((nb, LANES, LANES)), f((1, D)), f((1, D))],
        scratch_shapes=[pltpu.VMEM((S, LANES), F32)] * 3, semantics=("parallel", "arbitrary"),
        args=[proj3, proj3, h3, dya3, lp["conv_w"], lp["conv_b"], lp["w_rg_bd"], lp["b_rg"], lp["w_ig_bd"], lp["b_ig"], lp["lam"]])


def _pair_stack(x, lo):
    z = jnp.zeros_like(x)
    return jnp.concatenate([jnp.where(lo, x, z), jnp.where(lo, z, x)], axis=0).astype(BF16)


def _pair_join(y2, lo):
    return jnp.where(lo, y2[:WIN], y2[WIN:])


def _pair_col(xb):
    return jnp.concatenate([xb[:, 0:1], xb[:, HEAD_DIM:HEAD_DIM + 1]], axis=0)


def _pair_bcast(col, lo):
    return jnp.where(lo, jnp.broadcast_to(col[:WIN], (WIN, LANES)), jnp.broadcast_to(col[WIN:], (WIN, LANES)))


def _dil_rows(it, d, S):
    if d == 1:
        cur = pl.multiple_of(it * WIN, WIN)
        prev = pl.multiple_of(jnp.maximum(it - 1, 0) * WIN, WIN)
        return pl.ds(cur, WIN), pl.ds(prev, WIN), it > 0
    r, i = it % d, it // d
    cur = i * (WIN * d) + r
    prev = jnp.maximum(i - 1, 0) * (WIN * d) + r
    return pl.ds(cur, WIN, stride=d), pl.ds(prev, WIN, stride=d), i > 0


def _dil_bias(two_blocks, stack=2):
    nk = 2 * WIN if two_blocks else WIN
    qi = lax.broadcasted_iota(jnp.int32, (stack * WIN, nk), 0) & (WIN - 1)
    kj = lax.broadcasted_iota(jnp.int32, (stack * WIN, nk), 1)
    if not two_blocks:
        return jnp.where(kj <= qi, 0.0, NEG_INF), None
    cur = jnp.logical_and(kj >= WIN, kj - WIN <= qi)
    prev = jnp.logical_and(kj < WIN, kj >= qi)
    return jnp.where(jnp.logical_or(cur, prev), 0.0, NEG_INF), jnp.where(cur, 0.0, NEG_INF)


def _dil_specs(B, S, D, C, offs):
    grid = (B, D // LANES)
    seq = lambda off: pl.BlockSpec((None, S, LANES), lambda b, p: (b, 0, off // LANES + p))
    return grid, [seq(o) for o in offs], seq(0)


def _call_with_rider(rider, body, *, name, grid, in_specs, out_specs, out_shape, scratch_shapes, args, semantics=None):
    if rider is None:
        return _pcall(body, name=name, grid=grid, in_specs=in_specs, out_specs=out_specs, out_shape=out_shape,
                      scratch_shapes=scratch_shapes, compiler_params=_params(*(semantics or ("parallel",) * len(grid))))(*args)
    n_in, n_out, n_sc = len(in_specs), len(out_specs), len(scratch_shapes)
    r_in, r_out = len(rider["ins"]), len(rider["out_shapes"])

    def wrapped(*refs):
        p = 0
        own_in = refs[p:p + n_in]; p += n_in
        rid_in = refs[p:p + r_in]; p += r_in
        own_out = refs[p:p + n_out]; p += n_out
        rid_out = refs[p:p + r_out]; p += r_out
        own_sc = refs[p:p + n_sc]; p += n_sc
        send_sems, recv_sems = refs[p:p + 2]
        ids = [pl.program_id(a) for a in range(len(grid))]
        first = ids[0] == 0
        last = ids[0] == grid[0] - 1
        for a in range(1, len(grid)):
            first = jnp.logical_and(first, ids[a] == 0)
            last = jnp.logical_and(last, ids[a] == grid[a] - 1)

        @pl.when(first)
        def _():
            rider["start"](rid_in, rid_out, send_sems, recv_sems)

        body(*own_in, *own_out, *own_sc)

        @pl.when(last)
        def _():
            rider["finish"](rid_in, rid_out, send_sems, recv_sems)

    aliases = {n_in + t: n_out + t for t in range(r_in)} if rider.get("in_place") else {}
    res = _pcall_comm(
        wrapped, name=name + "_" + rider["name"], grid=grid, in_specs=list(in_specs) + [ANY] * r_in,
        out_specs=list(out_specs) + [ANY] * r_out, out_shape=list(out_shape) + list(rider["out_shapes"]),
        scratch_shapes=list(scratch_shapes) + [pltpu.SemaphoreType.DMA((rider["n"],)), pltpu.SemaphoreType.DMA((rider["n"],))],
        input_output_aliases=aliases, compiler_params=_params(*(("arbitrary",) * len(grid))),
    )(*args, *rider["ins"])
    return res


def _dil_fwd(proj3, *, D, q_off, k_off, v_off, rider=None):
    B, S, C = proj3.shape
    n_it = S // WIN
    scale = HEAD_DIM ** -0.5
    grid, in_specs, out_spec = _dil_specs(B, S, D, C, (q_off, k_off, v_off))

    def body(q_ref, k_ref, v_ref, o_ref, l_ref):
        lo = lax.broadcasted_iota(jnp.int32, (WIN, LANES), 1) < HEAD_DIM
        for c, d in enumerate(DILS):
            two = S // d > WIN
            bias_all, bias_first = _dil_bias(two)

            def step(it, _, c=c, d=d, two=two, bias_all=bias_all, bias_first=bias_first):
                cur, prev, later = _dil_rows(it, d, S)
                q2 = _pair_stack(q_ref[cur, :] * scale, lo)
                if two:
                    k2 = jnp.concatenate([k_ref[prev, :], k_ref[cur, :]], axis=0).astype(BF16)
                    v2 = jnp.concatenate([v_ref[prev, :], v_ref[cur, :]], axis=0).astype(BF16)
                    bias = jnp.where(later, bias_all, bias_first)
                else:
                    k2, v2, bias = k_ref[cur, :].astype(BF16), v_ref[cur, :].astype(BF16), bias_all
                s2 = _dot_nt(q2, k2) + bias
                m2 = jnp.max(s2, axis=1, keepdims=True)
                p2 = jnp.exp(s2 - m2)
                den = jnp.sum(p2, axis=1, keepdims=True)
                oc = _pair_join(_dot_nn(p2.astype(BF16), v2) / den, lo)
                lc = _pair_bcast(m2 + jnp.log(den), lo)
                if c == 0:
                    o_ref[cur, :] = oc
                    l_ref[cur, :] = lc
                else:
                    l_old = l_ref[cur, :]
                    mx = jnp.maximum(l_old, lc)
                    e_old, e_new = jnp.exp(l_old - mx), jnp.exp(lc - mx)
                    tot = e_old + e_new
                    o_ref[cur, :] = (e_old * o_ref[cur, :] + e_new * oc) / tot
                    l_ref[cur, :] = mx + jnp.log(tot)
                return 0

            lax.fori_loop(0, n_it, step, 0, unroll=16)

    return _call_with_rider(
        rider, body, name="dil_fwd", grid=grid, in_specs=in_specs, out_specs=[out_spec, out_spec],
        out_shape=[jax.ShapeDtypeStruct((B, S, D), F32)] * 2, scratch_shapes=[], args=[proj3, proj3, proj3])


def _dil_bwd(proj3, o3, l3, do3, *, D, q_off, k_off, v_off, rider=None):
    B, S, C = proj3.shape
    n_it = S // WIN
    scale = HEAD_DIM ** -0.5
    grid, in_specs, out_spec = _dil_specs(B, S, D, C, (q_off, k_off, v_off))

    def body(q_ref, k_ref, v_ref, o_ref, l_ref, do_ref, dq_ref, dk_ref, dv_ref, dd_s, dq_s, dk_s, dv_s):
        lo = lax.broadcasted_iota(jnp.int32, (WIN, LANES), 1) < HEAD_DIM
        lo_s = lax.broadcasted_iota(jnp.int32, (S, LANES), 1) < HEAD_DIM
        prod = do_ref[...] * o_ref[...]
        d_lo = jnp.sum(jnp.where(lo_s, prod, 0.0), axis=1, keepdims=True)
        d_hi = jnp.sum(jnp.where(lo_s, 0.0, prod), axis=1, keepdims=True)
        dd_s[...] = jnp.where(lo_s, jnp.broadcast_to(d_lo, (S, LANES)), jnp.broadcast_to(d_hi, (S, LANES)))
        dq_s[...] = jnp.zeros_like(dq_s)
        dk_s[...] = jnp.zeros_like(dk_s)
        dv_s[...] = jnp.zeros_like(dv_s)
        for d in DILS:
            two = S // d > WIN
            bias_all, bias_first = _dil_bias(two)

            def step(it, _, d=d, two=two, bias_all=bias_all, bias_first=bias_first):
                cur, prev, later = _dil_rows(it, d, S)
                q2 = _pair_stack(q_ref[cur, :] * scale, lo)
                do2 = _pair_stack(do_ref[cur, :], lo)
                l2 = _pair_col(l_ref[cur, :])
                dd2 = _pair_col(dd_s[cur, :])
                if two:
                    k2 = jnp.concatenate([k_ref[prev, :], k_ref[cur, :]], axis=0).astype(BF16)
                    v2 = jnp.concatenate([v_ref[prev, :], v_ref[cur, :]], axis=0).astype(BF16)
                    bias = jnp.where(later, bias_all, bias_first)
                else:
                    k2, v2, bias = k_ref[cur, :].astype(BF16), v_ref[cur, :].astype(BF16), bias_all
                p2 = jnp.exp(_dot_nt(q2, k2) + bias - l2)
                ds2 = (p2 * (_dot_nt(do2, v2) - dd2)).astype(BF16)
                dq_s[cur, :] += _pair_join(_dot_nn(ds2, k2), lo) * scale
                dk2 = _dot_tn(ds2, q2)
                dv2 = _dot_tn(p2.astype(BF16), do2)
                if two:
                    dk_s[prev, :] += dk2[:WIN]
                    dv_s[prev, :] += dv2[:WIN]
                    dk_s[cur, :] += dk2[WIN:]
                    dv_s[cur, :] += dv2[WIN:]
                else:
                    dk_s[cur, :] += dk2
                    dv_s[cur, :] += dv2
                return 0

            lax.fori_loop(0, n_it, step, 0, unroll=16)
        dq_ref[...] = dq_s[...].astype(BF16)
        dk_ref[...] = dk_s[...].astype(BF16)
        dv_ref[...] = dv_s[...].astype(BF16)

    return _call_with_rider(
        rider, body, name="dil_bwd", grid=grid, in_specs=in_specs + [out_spec] * 3, out_specs=[out_spec] * 3,
        out_shape=[jax.ShapeDtypeStruct((B, S, D), BF16)] * 3, scratch_shapes=[pltpu.VMEM((S, LANES), F32)] * 4,
        args=[proj3, proj3, proj3, o3, l3, do3])


SWA_HB = 2 * SWA_GROUP


def _to_half(x, src, dst, lo):
    if src != dst:
        x = pltpu.roll(x, HEAD_DIM, 1)
    return jnp.where(lo if dst == 0 else jnp.logical_not(lo), x, 0.0)


def _swa_kv(g):
    return 2 * g // SWA_GROUP


SWA_STACKS = ((0, 1), (2, 3))


def _swa_stack(ref, gs, lo, dtype, rows=slice(None)):
    parts = []
    for g in gs:
        x = ref[rows, g * LANES:(g + 1) * LANES]
        parts += [_to_half(x, 0, _swa_kv(g), lo), _to_half(x, 1, _swa_kv(g), lo)]
    return jnp.concatenate(parts, axis=0).astype(dtype)


def _swa_unstack(y, gs, lo):
    out = []
    for t, g in enumerate(gs):
        even, odd = y[2 * t * WIN:(2 * t + 1) * WIN], y[(2 * t + 1) * WIN:(2 * t + 2) * WIN]
        out.append(_to_half(even, _swa_kv(g), 0, lo) + _to_half(odd, _swa_kv(g), 1, lo))
    return out


def _swa_cols(x, gs):
    cols = []
    for g in gs:
        cols += [jnp.broadcast_to(x[:, 2 * g:2 * g + 1], (WIN, 1)), jnp.broadcast_to(x[:, 2 * g + 1:2 * g + 2], (WIN, 1))]
    return jnp.concatenate(cols, axis=0)


SWA_UNROLL = 16
SWA_UNROLL_BWD = 8


def _swa_seq_specs(B, S, D, q_off, k_off, v_off):
    qw = SWA_HB * HEAD_DIM
    assert q_off % qw == 0 and k_off % LANES == 0 and v_off % LANES == 0 and D % qw == 0
    seq = lambda width, off: pl.BlockSpec((None, S, width), lambda b, hh: (b, 0, off // width + hh))
    sink = pl.BlockSpec((None, 1, LANES), lambda b, hh: (hh, 0, 0))
    return (B, D // qw), seq, sink, qw


def _swa_rows(it):
    cur = pl.ds(pl.multiple_of(it * WIN, WIN), WIN)
    prev = pl.ds(pl.multiple_of(jnp.maximum(it - 1, 0) * WIN, WIN), WIN)
    return cur, prev, it > 0


def _swa_seq_fwd(proj3, sinks, *, D, q_off, k_off, v_off, rider=None):
    B, S, C = proj3.shape
    scale = HEAD_DIM ** -0.5
    grid, seq, sink, qw = _swa_seq_specs(B, S, D, q_off, k_off, v_off)
    nhb = D // qw

    def body(q_ref, k_ref, v_ref, sk_ref, o_ref, lse_ref):
        lo = lax.broadcasted_iota(jnp.int32, (WIN, LANES), 1) < HEAD_DIM
        lane = lax.broadcasted_iota(jnp.int32, (WIN, LANES), 1)
        sk = sk_ref[...]
        biases = [_dil_bias(True, 2 * len(gs)) for gs in SWA_STACKS]

        def step(it, _):
            cur, prev, later = _swa_rows(it)
            k2 = jnp.concatenate([k_ref[prev, :], k_ref[cur, :]], axis=0).astype(BF16)
            v2 = jnp.concatenate([v_ref[prev, :], v_ref[cur, :]], axis=0).astype(BF16)
            lse_acc = jnp.zeros((WIN, LANES), F32)
            for gs, (bias_all, bias_first) in zip(SWA_STACKS, biases):
                bias = jnp.where(later, bias_all, bias_first)
                qs = _swa_stack(q_ref, gs, lo, BF16, cur)
                sks = _swa_cols(sk, gs)
                s = _dot_nt(qs, k2) * scale + bias
                m = jnp.maximum(jnp.max(s, axis=1, keepdims=True), sks)
                p = jnp.exp(s - m)
                den = jnp.sum(p, axis=1, keepdims=True) + jnp.exp(sks - m)
                for g, grp in zip(gs, _swa_unstack(_dot_nn(p.astype(BF16), v2) / den, gs, lo)):
                    o_ref[cur, g * LANES:(g + 1) * LANES] = grp
                ls = m + jnp.log(den)
                for t, g in enumerate(gs):
                    lse_acc = jnp.where(lane == 2 * g, ls[2 * t * WIN:(2 * t + 1) * WIN], lse_acc)
                    lse_acc = jnp.where(lane == 2 * g + 1, ls[(2 * t + 1) * WIN:(2 * t + 2) * WIN], lse_acc)
            lse_ref[cur, :] = lse_acc
            return 0

        lax.fori_loop(0, S // WIN, step, 0, unroll=SWA_UNROLL)

    return _call_with_rider(
        rider, body, name="swa_fwd", grid=grid,
        in_specs=[seq(qw, q_off), seq(LANES, k_off), seq(LANES, v_off), sink],
        out_specs=[seq(qw, 0), seq(LANES, 0)],
        out_shape=[jax.ShapeDtypeStruct((B, S, D), F32), jax.ShapeDtypeStruct((B, S, nhb * LANES), F32)],
        scratch_shapes=[], args=[proj3, proj3, proj3, sinks])


def _swa_seq_bwd(proj3, o3, lse3, do3, sinks, *, D, q_off, k_off, v_off):
    B, S, C = proj3.shape
    scale = HEAD_DIM ** -0.5
    grid, seq, sink, qw = _swa_seq_specs(B, S, D, q_off, k_off, v_off)
    nhb = D // qw
    KV = D // SWA_GROUP

    def body(q_ref, k_ref, v_ref, o_ref, l_ref, do_ref, sk_ref, dq_ref, dk_ref, dv_ref, dsk_ref, dk_s, dv_s):
        lo = lax.broadcasted_iota(jnp.int32, (WIN, LANES), 1) < HEAD_DIM
        lane = lax.broadcasted_iota(jnp.int32, (1, LANES), 1)
        sk = sk_ref[...]
        biases = [_dil_bias(True, 2 * len(gs)) for gs in SWA_STACKS]
        dk_s[...] = jnp.zeros_like(dk_s)
        dv_s[...] = jnp.zeros_like(dv_s)

        dsk_ref[...] = jnp.zeros_like(dsk_ref)

        def step(it, _):
            cur, prev, later = _swa_rows(it)
            k2 = jnp.concatenate([k_ref[prev, :], k_ref[cur, :]], axis=0).astype(BF16)
            v2 = jnp.concatenate([v_ref[prev, :], v_ref[cur, :]], axis=0).astype(BF16)
            lse = l_ref[cur, :]
            dk2 = jnp.zeros((2 * WIN, LANES), F32)
            dv2 = jnp.zeros((2 * WIN, LANES), F32)
            dsk_acc = jnp.zeros((1, LANES), F32)
            for gs, (bias_all, bias_first) in zip(SWA_STACKS, biases):
                bias = jnp.where(later, bias_all, bias_first)
                qs = _swa_stack(q_ref, gs, lo, BF16, cur)
                dos = _swa_stack(do_ref, gs, lo, BF16, cur)
                dds = []
                for g in gs:
                    prod = do_ref[cur, g * LANES:(g + 1) * LANES] * o_ref[cur, g * LANES:(g + 1) * LANES]
                    dds += [jnp.sum(jnp.where(lo, prod, 0.0), axis=1, keepdims=True),
                            jnp.sum(jnp.where(lo, 0.0, prod), axis=1, keepdims=True)]
                dds = jnp.concatenate(dds, axis=0)
                ls = _swa_cols(lse, gs)
                ps = jnp.exp(_dot_nt(qs, k2) * scale + bias - ls)
                dss = (ps * (_dot_nt(dos, v2) - dds) * scale).astype(BF16)
                for g, grp in zip(gs, _swa_unstack(_dot_nn(dss, k2), gs, lo)):
                    dq_ref[cur, g * LANES:(g + 1) * LANES] = grp.astype(BF16)
                dk2 = dk2 + _dot_tn(dss, qs)
                dv2 = dv2 + _dot_tn(ps.astype(BF16), dos)
                dsks = jnp.exp(_swa_cols(sk, gs) - ls) * dds
                for t, g in enumerate(gs):
                    for u in range(2):
                        rows = slice((2 * t + u) * WIN, (2 * t + u + 1) * WIN)
                        dsk_acc = dsk_acc + jnp.where(lane == 2 * g + u, -jnp.sum(dsks[rows], axis=0, keepdims=True), 0.0)
            dk_s[prev, :] += dk2[:WIN]
            dv_s[prev, :] += dv2[:WIN]
            dk_s[cur, :] += dk2[WIN:]
            dv_s[cur, :] += dv2[WIN:]
            dsk_ref[...] += dsk_acc
            return 0

        lax.fori_loop(0, S // WIN, step, 0, unroll=SWA_UNROLL_BWD)
        dk_ref[...] = dk_s[...].astype(BF16)
        dv_ref[...] = dv_s[...].astype(BF16)

    return _pcall(
        body, name="swa_bwd", grid=grid,
        in_specs=[seq(qw, q_off), seq(LANES, k_off), seq(LANES, v_off), seq(qw, 0), seq(LANES, 0), seq(qw, 0), sink],
        out_specs=[seq(qw, 0), seq(LANES, 0), seq(LANES, 0), pl.BlockSpec((None, None, 1, LANES), lambda b, hh: (b, hh, 0, 0))],
        out_shape=[jax.ShapeDtypeStruct((B, S, D), BF16), jax.ShapeDtypeStruct((B, S, KV), BF16),
                   jax.ShapeDtypeStruct((B, S, KV), BF16), jax.ShapeDtypeStruct((B, nhb, 1, LANES), F32)],
        scratch_shapes=[pltpu.VMEM((S, LANES), F32), pltpu.VMEM((S, LANES), F32)],
        compiler_params=_params("parallel", "parallel"),
    )(proj3, proj3, proj3, o3, lse3, do3, sinks)


def _branch_fwd(ys, wb, proj, *, D, g_off, rider=None):
    T = proj.shape[0]
    tm, tn = _tile(T, 256), _tile(D, 512)
    n = len(ys)

    def body(*refs):
        y_refs, w_ref, g_refs, br_ref, mg_ref = refs[:n], refs[n], refs[n + 1:2 * n + 1], refs[2 * n + 1], refs[2 * n + 2]
        acc = None
        for k in range(n):
            br = _dot_nn(y_refs[k][...].astype(BF16), w_ref[k])
            br_ref[k] = br
            term = _sigmoid(g_refs[k][...]) * br
            acc = term if acc is None else acc + term
        mg_ref[...] = acc.astype(BF16)

    gate = lambda k: pl.BlockSpec((tm, tn), lambda i, j: (i, (g_off + k * D) // tn + j))
    return _call_with_rider(
        rider, body, name="branch_fwd", grid=(T // tm, D // tn),
        in_specs=[pl.BlockSpec((tm, D), lambda i, j: (i, 0))] * n + [pl.BlockSpec((n, D, tn), lambda i, j: (0, 0, j))]
        + [gate(k) for k in range(n)],
        out_specs=[pl.BlockSpec((n, tm, tn), lambda i, j: (0, i, j)), pl.BlockSpec((tm, tn), lambda i, j: (i, j))],
        out_shape=[jax.ShapeDtypeStruct((n, T, D), F32), jax.ShapeDtypeStruct((T, D), BF16)],
        scratch_shapes=[], args=[*ys, wb, *([proj] * n)])


def _branch_bwd(dmix, w_out, branch, proj, *, D, g_off):
    n, T, _ = branch.shape
    tm, tn = _tile(T, 512), _tile(D, 512)

    def body(dy_ref, w_ref, br_ref, *rest):
        g_refs, db_ref, dg_refs = rest[:n], rest[n], rest[n + 1:]
        dm = _dot_nt(dy_ref[...].astype(BF16), w_ref[...])
        for k in range(n):
            sg = _sigmoid(g_refs[k][...])
            db_ref[k] = (sg * dm).astype(BF16)
            dg_refs[k][...] = (dm * br_ref[k] * sg * (1.0 - sg)).astype(BF16)

    gate = lambda k: pl.BlockSpec((tm, tn), lambda i, j: (i, (g_off + k * D) // tn + j))
    blk = pl.BlockSpec((tm, tn), lambda i, j: (i, j))
    res = _pcall(
        body, name="branch_bwd", grid=(T // tm, D // tn),
        in_specs=[pl.BlockSpec((tm, D), lambda i, j: (i, 0)), pl.BlockSpec((tn, D), lambda i, j: (j, 0)),
                  pl.BlockSpec((n, tm, tn), lambda i, j: (0, i, j))] + [gate(k) for k in range(n)],
        out_specs=[pl.BlockSpec((n, tm, tn), lambda i, j: (0, i, j))] + [blk] * n,
        out_shape=[jax.ShapeDtypeStruct((n, T, D), BF16)] + [jax.ShapeDtypeStruct((T, D), BF16)] * n,
        compiler_params=_params("parallel", "parallel"),
    )(dmix, w_out, branch, *([proj] * n))
    return res[0], list(res[1:])


def _ln_bwd(dout, z, g):
    T, D = z.shape
    tm = _tile(T, 512)

    def body(do_ref, z_ref, g_ref, dz_ref, dg_ref, db_ref):
        z = z_ref[...]
        do = do_ref[...]
        mu = jnp.mean(z, axis=1, keepdims=True)
        zc = z - mu
        rstd = lax.rsqrt(jnp.mean(zc * zc, axis=1, keepdims=True) + LN_EPS)
        xhat = zc * rstd
        dxh = do * g_ref[...]
        dz_ref[...] = rstd * (dxh - jnp.mean(dxh, axis=1, keepdims=True) - xhat * jnp.mean(dxh * xhat, axis=1, keepdims=True))
        dg = jnp.sum(do * xhat, axis=0, keepdims=True)
        db = jnp.sum(do, axis=0, keepdims=True)
        first = pl.program_id(0) == 0

        @pl.when(first)
        def _():
            dg_ref[...] = dg
            db_ref[...] = db

        @pl.when(jnp.logical_not(first))
        def _():
            dg_ref[...] += dg
            db_ref[...] += db

    blk = pl.BlockSpec((tm, D), lambda i: (i, 0))
    vec = pl.BlockSpec((1, D), lambda i: (0, 0))
    return _pcall(
        body, name="ln_bwd", grid=(T // tm,), in_specs=[blk, blk, vec], out_specs=[blk, vec, vec],
        out_shape=[jax.ShapeDtypeStruct((T, D), F32), jax.ShapeDtypeStruct((1, D), F32), jax.ShapeDtypeStruct((1, D), F32)],
        compiler_params=_params("arbitrary"),
    )(dout, z, g)


def _ffn_in_fwd(x1, w_sh):
    T, D = x1.shape
    ns, _, nsh = w_sh.shape
    half = ns // 2
    Fh = half * nsh
    tm = _tile(T, 512)

    def body(x_ref, wa_ref, wb_ref, h1_ref, h3_ref, f_ref):
        xb = x_ref[...].astype(BF16)
        h1 = _dot_nn(xb, wa_ref[...])
        h3 = _dot_nn(xb, wb_ref[...])
        h1_ref[...] = h1
        h3_ref[...] = h3
        f_ref[...] = (h1 * _sigmoid(h1) * h3).astype(BF16)

    cols = pl.BlockSpec((tm, nsh), lambda j, i: (i, j))
    return _pcall(
        body, name="ffn_in_fwd", grid=(half, T // tm),
        in_specs=[pl.BlockSpec((tm, D), lambda j, i: (i, 0)), pl.BlockSpec((None, D, nsh), lambda j, i: (j, 0, 0)),
                  pl.BlockSpec((None, D, nsh), lambda j, i: (j + half, 0, 0))],
        out_specs=[cols, cols, cols],
        out_shape=[jax.ShapeDtypeStruct((T, Fh), F32), jax.ShapeDtypeStruct((T, Fh), F32), jax.ShapeDtypeStruct((T, Fh), BF16)],
        compiler_params=_params("parallel", "parallel"),
    )(x1, w_sh, w_sh)


def _ffn_out_bwd(dy, w_ffn_out, h1, h3):
    T, Fh = h1.shape
    D = w_ffn_out.shape[1]
    tm = _tile(T, 256)

    def body(dy_ref, w_ref, h1_ref, h3_ref, o_ref):
        d = _dot_nt(dy_ref[...].astype(BF16), w_ref[...])
        h1v = h1_ref[...]
        sg = _sigmoid(h1v)
        o_ref[:, :Fh] = (d * h3_ref[...] * sg * (1.0 + h1v * (1.0 - sg))).astype(BF16)
        o_ref[:, Fh:] = (d * h1v * sg).astype(BF16)

    blk = pl.BlockSpec((tm, Fh), lambda i: (i, 0))
    return _pcall(
        body, name="ffn_out_bwd", grid=(T // tm,),
        in_specs=[pl.BlockSpec((tm, D), lambda i: (i, 0)), pl.BlockSpec((Fh, D), lambda i: (0, 0)), blk, blk],
        out_specs=pl.BlockSpec((tm, 2 * Fh), lambda i: (i, 0)),
        out_shape=jax.ShapeDtypeStruct((T, 2 * Fh), BF16), compiler_params=_params("parallel"),
    )(dy, w_ffn_out, h1, h3)


def _loss_head(y, target):
    T, D = y.shape
    tm = _tile(T, 512)

    def body(y_ref, t_ref, dy_ref, l_ref):
        e = y_ref[...] - t_ref[...]
        dy_ref[...] = e * (1.0 / D)
        sq = e * e
        part = sq[:, 0:LANES]
        for c in range(1, D // LANES):
            part = part + sq[:, c * LANES:(c + 1) * LANES]
        part = jnp.sum(part, axis=0, keepdims=True) * (0.5 / D)
        first = pl.program_id(0) == 0

        @pl.when(first)
        def _():
            l_ref[...] = part

        @pl.when(jnp.logical_not(first))
        def _():
            l_ref[...] += part

    blk = pl.BlockSpec((tm, D), lambda i: (i, 0))
    return _pcall(
        body, name="loss_head", grid=(T // tm,), in_specs=[blk, blk],
        out_specs=[blk, pl.BlockSpec((1, LANES), lambda i: (0, 0))],
        out_shape=[jax.ShapeDtypeStruct((T, D), F32), jax.ShapeDtypeStruct((1, LANES), F32)],
        compiler_params=_params("arbitrary"),
    )(y, target)


def _as_rows(a):
    return a.reshape(-1, a.shape[-1])


def _adamw(w, g, m, v, rider=None):
    w2, g2, m2, v2 = (_as_rows(t) for t in (w, g, m, v))
    R, Cc = w2.shape
    cap = max(SUBLANES, min(512, (256 * 1024) // Cc))
    tm = R if (R <= cap or R % SUBLANES) else max(t for t in range(SUBLANES, cap + 1, SUBLANES) if R % t == 0)
    c1 = 1.0 - ADAM_B1 ** ADAM_STEP
    c2 = 1.0 - ADAM_B2 ** ADAM_STEP

    def body(w_ref, g_ref, m_ref, v_ref, d_ref, nm_ref, nv_ref):
        gg = g_ref[...]
        nm = ADAM_B1 * m_ref[...] + (1.0 - ADAM_B1) * gg
        nv = ADAM_B2 * v_ref[...] + (1.0 - ADAM_B2) * (gg * gg)
        d_ref[...] = (-ADAM_LR) * ((nm / c1) / (jnp.sqrt(nv / c2) + ADAM_EPS) + ADAM_WD * w_ref[...])
        nm_ref[...] = nm
        nv_ref[...] = nv

    blk = pl.BlockSpec((tm, Cc), lambda i: (i, 0))
    res = _call_with_rider(
        rider, body, name="adamw", grid=(R // tm,), in_specs=[blk] * 4, out_specs=[blk] * 3,
        out_shape=[jax.ShapeDtypeStruct((R, Cc), F32)] * 3, scratch_shapes=[], args=[w2, g2, m2, v2])
    return tuple(t.reshape(w.shape) for t in res[:3]) + tuple(res[3:])


def _where_am_i():
    x, y, c = lax.axis_index("x"), lax.axis_index("y"), lax.axis_index("c")
    chips = [(1 - x, y), (x, 1 - y), (1 - x, 1 - y)]
    return x, y, c, chips


def _remote(src, dst, send_sems, recv_sems, k, to):
    return pltpu.make_async_remote_copy(src_ref=src, dst_ref=dst, send_sem=send_sems.at[k], recv_sem=recv_sems.at[k],
                                        device_id=to, device_id_type=MESH)


def _comm_call(body, name, ins, out_shapes, n_remote, n_local):
    return _pcall_comm(
        body, name=name, in_specs=[ANY] * len(ins), out_specs=[ANY] * len(out_shapes), out_shape=out_shapes,
        scratch_shapes=[pltpu.SemaphoreType.DMA((n_remote,)), pltpu.SemaphoreType.DMA((n_remote,)),
                        pltpu.SemaphoreType.DMA((max(n_local, 1),))],
    )(*ins)


def _gather_weights(shards):
    n = len(shards)

    def body(*refs):
        ins, outs = refs[:n], refs[n:2 * n]
        send_sems, recv_sems, local_sems = refs[2 * n:]
        x, y, c, chips = _where_am_i()
        s = 2 * x + y
        sib = (x, y, 1 - c)
        first = []
        for t in range(n):
            for j, (cx, cy) in enumerate(chips):
                first.append(_remote(ins[t].at[:, c], outs[t].at[:, s, c], send_sems, recv_sems, 6 * t + j, (cx, cy, c)))
        for cp in first:
            cp.start()
        passed = []
        for j, (cx, cy) in enumerate(chips):
            sj = 2 * cx + cy
            for t in range(n):
                land = outs[t].at[:, sj, c]
                _remote(land, land, send_sems, recv_sems, 6 * t + j, (cx, cy, c)).wait_recv()
                fw = _remote(land, land, send_sems, recv_sems, 6 * t + 3 + j, sib)
                fw.start()
                passed.append(fw)
        for j, (cx, cy) in enumerate(chips):
            sj = 2 * cx + cy
            for t in range(n):
                land = outs[t].at[:, sj, 1 - c]
                _remote(land, land, send_sems, recv_sems, 6 * t + 3 + j, sib).wait_recv()
        for cp in first + passed:
            cp.wait_send()

    out_shapes = [jax.ShapeDtypeStruct((t.shape[0], N_CHIPS) + t.shape[1:], t.dtype) for t in shards]
    got = _comm_call(body, "gather_weights", shards, out_shapes, 6 * n, 0)
    s = 2 * lax.axis_index("x") + lax.axis_index("y")
    return [lax.dynamic_update_slice(g, t[:, None], (0, s, 0, 0, 0)) for g, t in zip(got, shards)]


def _gather_rider(shards):
    n = len(shards)

    def copies(ins, outs, send_sems, recv_sems):
        x, y, c, chips = _where_am_i()
        s = 2 * x + y
        return [_remote(ins[t].at[:, c], outs[t].at[:, s, c], send_sems, recv_sems, 3 * t + j, (cx, cy, c))
                for t in range(n) for j, (cx, cy) in enumerate(chips)]

    def start(ins, outs, send_sems, recv_sems):
        for cp in copies(ins, outs, send_sems, recv_sems):
            cp.start()

    def finish(ins, outs, send_sems, recv_sems):
        x, y, c, chips = _where_am_i()
        for t in range(n):
            for j, (cx, cy) in enumerate(chips):
                land = outs[t].at[:, 2 * cx + cy, c]
                _remote(land, land, send_sems, recv_sems, 3 * t + j, (cx, cy, c)).wait_recv()
        for cp in copies(ins, outs, send_sems, recv_sems):
            cp.wait_send()

    out_shapes = [jax.ShapeDtypeStruct((t.shape[0], N_CHIPS) + t.shape[1:], t.dtype) for t in shards]
    return dict(name="gather", ins=list(shards), out_shapes=out_shapes, n=3 * n, start=start, finish=finish)


def _gather_forward(landed, shards):
    n = len(landed)

    def body(*refs):
        outs = refs[n:2 * n]
        send_sems, recv_sems, _ = refs[2 * n:]
        x, y, c, chips = _where_am_i()
        sib = (x, y, 1 - c)
        cps = []
        for t in range(n):
            for j, (cx, cy) in enumerate(chips):
                land = outs[t].at[:, 2 * cx + cy, c]
                cps.append(_remote(land, land, send_sems, recv_sems, 3 * t + j, sib))
        for cp in cps:
            cp.start()
        for t in range(n):
            for j, (cx, cy) in enumerate(chips):
                land = outs[t].at[:, 2 * cx + cy, 1 - c]
                _remote(land, land, send_sems, recv_sems, 3 * t + j, sib).wait_recv()
        for cp in cps:
            cp.wait_send()

    got = _pcall_comm(
        body, name="gather_forward", in_specs=[ANY] * n, out_specs=[ANY] * n,
        out_shape=[jax.ShapeDtypeStruct(t.shape, t.dtype) for t in landed], input_output_aliases={t: t for t in range(n)},
        scratch_shapes=[pltpu.SemaphoreType.DMA((3 * n,)), pltpu.SemaphoreType.DMA((3 * n,)), pltpu.SemaphoreType.DMA((1,))],
    )(*landed)
    s = 2 * lax.axis_index("x") + lax.axis_index("y")
    return [lax.dynamic_update_slice(g, t[:, None], (0, s, 0, 0, 0)) for g, t in zip(got, shards)]


def _forward_rider(landed):
    n = len(landed)

    def copies(outs, send_sems, recv_sems):
        x, y, c, chips = _where_am_i()
        cps = []
        for t in range(n):
            for j, (cx, cy) in enumerate(chips):
                land = outs[t].at[:, 2 * cx + cy, c]
                cps.append(_remote(land, land, send_sems, recv_sems, 3 * t + j, (x, y, 1 - c)))
        return cps

    def start(ins, outs, send_sems, recv_sems):
        for cp in copies(outs, send_sems, recv_sems):
            cp.start()

    def finish(ins, outs, send_sems, recv_sems):
        x, y, c, chips = _where_am_i()
        for t in range(n):
            for j, (cx, cy) in enumerate(chips):
                land = outs[t].at[:, 2 * cx + cy, 1 - c]
                _remote(land, land, send_sems, recv_sems, 3 * t + j, (x, y, 1 - c)).wait_recv()
        for cp in copies(outs, send_sems, recv_sems):
            cp.wait_send()

    out_shapes = [jax.ShapeDtypeStruct(t.shape, t.dtype) for t in landed]
    return dict(name="forward", ins=list(landed), out_shapes=out_shapes, n=3 * n, start=start, finish=finish, in_place=True)


def _place_own_shard(got, shards):
    s = 2 * lax.axis_index("x") + lax.axis_index("y")
    return [lax.dynamic_update_slice(g, t[:, None], (0, s, 0, 0, 0)) for g, t in zip(got, shards)]


def _gather_small(v):
    def body(v_ref, out_ref, send_sems, recv_sems, local_sems):
        x, y, c, chips = _where_am_i()
        s = 2 * x + y
        mine = pltpu.make_async_copy(v_ref, out_ref.at[s], local_sems.at[0])
        mine.start()
        sends = [_remote(v_ref, out_ref.at[s], send_sems, recv_sems, j, (cx, cy, c)) for j, (cx, cy) in enumerate(chips)]
        for cp in sends:
            cp.start()
        for j, (cx, cy) in enumerate(chips):
            land = out_ref.at[2 * cx + cy]
            _remote(land, land, send_sems, recv_sems, j, (cx, cy, c)).wait_recv()
        for cp in sends:
            cp.wait_send()
        mine.wait()

    return _comm_call(body, "gather_small", [v], [jax.ShapeDtypeStruct((N_CHIPS,) + v.shape, v.dtype)], 3, 1)[0]


def _swap_sibling_halves(grads):
    n = len(grads)

    def body(*refs):
        ins, outs = refs[:n], refs[n:2 * n]
        send_sems, recv_sems, _ = refs[2 * n:]
        x, y, c, _chips = _where_am_i()
        sib = (x, y, 1 - c)
        cps = [_remote(ins[t].at[:, :, 1 - c], outs[t], send_sems, recv_sems, t, sib) for t in range(n)]
        for cp in cps:
            cp.start()
        for cp in cps:
            cp.wait()

    out_shapes = [jax.ShapeDtypeStruct(g.shape[:2] + g.shape[3:], g.dtype) for g in grads]
    return _comm_call(body, "grad_swap_halves", grads, out_shapes, n, 0)


def _exchange_chips(parts):
    n = len(parts)

    def body(*refs):
        ins, outs = refs[:n], refs[n:2 * n]
        send_sems, recv_sems, _ = refs[2 * n:]
        x, y, c, chips = _where_am_i()
        cps = []
        for t in range(n):
            for j, (cx, cy) in enumerate(chips):
                cps.append(_remote(ins[t].at[:, 2 * cx + cy], outs[t].at[j], send_sems, recv_sems, 3 * t + j, (cx, cy, c)))
        for cp in cps:
            cp.start()
        for cp in cps:
            cp.wait()

    out_shapes = [jax.ShapeDtypeStruct((3, p.shape[0]) + p.shape[2:], p.dtype) for p in parts]
    return _comm_call(body, "grad_exchange_chips", parts, out_shapes, 3 * n, 0)


def _exchange_rider(parts):
    n = len(parts)

    def copies(ins, outs, send_sems, recv_sems):
        x, y, c, chips = _where_am_i()
        return [_remote(ins[t].at[:, 2 * cx + cy], outs[t].at[j], send_sems, recv_sems, 3 * t + j, (cx, cy, c))
                for t in range(n) for j, (cx, cy) in enumerate(chips)]

    def start(ins, outs, send_sems, recv_sems):
        for cp in copies(ins, outs, send_sems, recv_sems):
            cp.start()

    def finish(ins, outs, send_sems, recv_sems):
        for cp in copies(ins, outs, send_sems, recv_sems):
            cp.wait()

    out_shapes = [jax.ShapeDtypeStruct((3, p.shape[0]) + p.shape[2:], p.dtype) for p in parts]
    return dict(name="exchange", ins=list(parts), out_shapes=out_shapes, n=3 * n, start=start, finish=finish)


def _join_sibling_halves(halves):
    n = len(halves)

    def body(*refs):
        ins, outs = refs[:n], refs[n:2 * n]
        send_sems, recv_sems, local_sems = refs[2 * n:]
        x, y, c, _chips = _where_am_i()
        sib = (x, y, 1 - c)
        cps = [_remote(ins[t], outs[t].at[:, c], send_sems, recv_sems, t, sib) for t in range(n)]
        for cp in cps:
            cp.start()
        for t in range(n):
            land = outs[t].at[:, 1 - c]
            _remote(land, land, send_sems, recv_sems, t, sib).wait_recv()
        for cp in cps:
            cp.wait_send()

    out_shapes = [jax.ShapeDtypeStruct((h.shape[0], 2) + h.shape[1:], h.dtype) for h in halves]
    got = _comm_call(body, "grad_join_halves", halves, out_shapes, n, 0)
    c = lax.axis_index("c")
    return [lax.dynamic_update_slice(g, h[:, None], (0, c, 0, 0)) for g, h in zip(got, halves)]


def _join_rider(halves):
    n = len(halves)

    def copies(ins, outs, send_sems, recv_sems):
        x, y, c, _chips = _where_am_i()
        return [_remote(ins[t], outs[t].at[:, c], send_sems, recv_sems, t, (x, y, 1 - c)) for t in range(n)]

    def start(ins, outs, send_sems, recv_sems):
        for cp in copies(ins, outs, send_sems, recv_sems):
            cp.start()

    def finish(ins, outs, send_sems, recv_sems):
        x, y, c, _chips = _where_am_i()
        for t in range(n):
            land = outs[t].at[:, 1 - c]
            _remote(land, land, send_sems, recv_sems, t, (x, y, 1 - c)).wait_recv()
        for cp in copies(ins, outs, send_sems, recv_sems):
            cp.wait_send()

    out_shapes = [jax.ShapeDtypeStruct((h.shape[0], 2) + h.shape[1:], h.dtype) for h in halves]
    return dict(name="join", ins=list(halves), out_shapes=out_shapes, n=n, start=start, finish=finish)


def _place_own_half(got, halves):
    c = lax.axis_index("c")
    return [lax.dynamic_update_slice(g, h[:, None], (0, c, 0, 0)) for g, h in zip(got, halves)]


def _small_exchange_rider(v):
    def copies(ins, outs, send_sems, recv_sems):
        x, y, c, chips = _where_am_i()
        return [_remote(ins[0], outs[0].at[j], send_sems, recv_sems, j, (cx, cy, c)) for j, (cx, cy) in enumerate(chips)]

    def start(ins, outs, send_sems, recv_sems):
        for cp in copies(ins, outs, send_sems, recv_sems):
            cp.start()

    def finish(ins, outs, send_sems, recv_sems):
        for cp in copies(ins, outs, send_sems, recv_sems):
            cp.wait()

    return dict(name="small_exchange", ins=[v], out_shapes=[jax.ShapeDtypeStruct((3,) + v.shape, v.dtype)], n=3,
                start=start, finish=finish)


def _swap_small(v):
    def body(v_ref, out_ref, send_sems, recv_sems, _):
        x, y, c, _chips = _where_am_i()
        cp = _remote(v_ref, out_ref, send_sems, recv_sems, 0, (x, y, 1 - c))
        cp.start()
        cp.wait()

    return _comm_call(body, "small_swap", [v], [jax.ShapeDtypeStruct(v.shape, v.dtype)], 1, 0)[0]


def _sum_rows(name, terms, out_dtypes):
    R, Cc = terms[0].shape
    tm = R if R <= 256 else max(t for t in range(16, 257, 16) if R % t == 0)
    n = len(terms)

    def body(*refs):
        acc = refs[0][...].astype(F32)
        for r in refs[1:n]:
            acc = acc + r[...].astype(F32)
        for o in refs[n:]:
            o[...] = acc.astype(o.dtype)

    blk = pl.BlockSpec((tm, Cc), lambda i: (i, 0))
    return _pcall(
        body, name=name, grid=(R // tm,), in_specs=[blk] * n, out_specs=[blk] * len(out_dtypes),
        out_shape=[jax.ShapeDtypeStruct((R, Cc), d) for d in out_dtypes], compiler_params=_params("parallel"),
    )(*terms)


def _pair_sum(g5, r1, core, shard):
    A4, _, Rh, Cc = g5.shape
    A = A4 // N_CHIPS
    tr = Rh if Rh <= 256 else max(t for t in range(16, 257, 16) if Rh % t == 0)

    def body(core_ref, shard_ref, g_ref, r_ref, qb_ref, qf_ref):
        q = g_ref[...] + r_ref[...]
        qb_ref[...] = q.astype(BF16)

        @pl.when(pl.program_id(2) == shard_ref[0])
        def _():
            qf_ref[...] = q

    grid_spec = pltpu.PrefetchScalarGridSpec(
        num_scalar_prefetch=2, grid=(A, Rh // tr, N_CHIPS),
        in_specs=[pl.BlockSpec((None, None, tr, Cc), lambda a, r, sh, core, shard: (a * N_CHIPS + sh, core[0], r, 0)),
                  pl.BlockSpec((None, tr, Cc), lambda a, r, sh, core, shard: (a * N_CHIPS + sh, r, 0))],
        out_specs=[pl.BlockSpec((None, tr, Cc), lambda a, r, sh, core, shard: (a * N_CHIPS + sh, r, 0)),
                   pl.BlockSpec((None, tr, Cc), lambda a, r, sh, core, shard: (a, r, 0))],
    )
    return _pcall(
        body, name="grad_pair_sum", grid_spec=grid_spec,
        out_shape=[jax.ShapeDtypeStruct((A4, Rh, Cc), BF16), jax.ShapeDtypeStruct((A, Rh, Cc), F32)],
        compiler_params=_params("parallel", "parallel", "arbitrary"),
    )(core, shard, g5, r1)


def _swap_rider(grads):
    n = len(grads)

    def copies(ins, outs, send_sems, recv_sems):
        x, y, c, _chips = _where_am_i()
        return [_remote(ins[t].at[:, :, 1 - c], outs[t], send_sems, recv_sems, t, (x, y, 1 - c)) for t in range(n)]

    def start(ins, outs, send_sems, recv_sems):
        for cp in copies(ins, outs, send_sems, recv_sems):
            cp.start()

    def finish(ins, outs, send_sems, recv_sems):
        for cp in copies(ins, outs, send_sems, recv_sems):
            cp.wait()

    out_shapes = [jax.ShapeDtypeStruct(g.shape[:2] + g.shape[3:], g.dtype) for g in grads]
    return dict(name="swap", ins=list(grads), out_shapes=out_shapes, n=n, start=start, finish=finish)


def _reduce_chip(grads, r1, core, shard):
    qb, qf = [], []
    for g, r in zip(grads, r1):
        A, _, _, Rh, Cc = g.shape
        b, f = _pair_sum(g.reshape(A * N_CHIPS, 2, Rh, Cc), r.reshape(A * N_CHIPS, Rh, Cc), core, shard)
        qb.append(b.reshape(A, N_CHIPS, Rh, Cc))
        qf.append(f)
    return qb, qf


def _reduce_finish(qf, r2):
    return _as_shards(_join_sibling_halves(_chip_sums(qf, r2)))


def _chip_sums(qf, r2):
    halves = []
    for f, r in zip(qf, r2):
        A, Rh, Cc = f.shape
        terms = [f.reshape(A * Rh, Cc)] + [r[j].reshape(A * Rh, Cc) for j in range(3)]
        halves.append(_sum_rows("grad_chip_sum", terms, [F32])[0].reshape(A, Rh, Cc))
    return halves


def _as_shards(full):
    return [t.reshape(t.shape[0], 2 * t.shape[2], t.shape[3]) for t in full]


def _small_pair(v):
    return _sum_rows("small_pair_sum", [v, _swap_small(v)], [F32])[0]


def _small_chip_sum(pair, others):
    x, y = lax.axis_index("x"), lax.axis_index("y")
    s = 2 * x + y
    stack = jnp.concatenate([pair[None], others], axis=0)
    src = jnp.stack([s, s ^ 2, s ^ 1, s ^ 3])
    order = jnp.argsort(src)
    terms = [lax.dynamic_index_in_dim(stack, order[k], 0, keepdims=False) for k in range(N_CHIPS)]
    return _sum_rows("small_chip_sum", terms, [F32])[0]


def _block_diag(w):
    nb, bw, _ = w.shape
    per = LANES // bw
    w = w.reshape(nb // per, per, bw, bw)
    eye = jnp.eye(per, dtype=w.dtype)
    bd = jnp.einsum("tpij,pq->tpiqj", w, eye).reshape(nb // per, LANES, LANES)
    return bd.astype(BF16)


def _block_diag_grad(g, bw):
    nt = g.shape[0]
    per = LANES // bw
    g = g.reshape(nt, per, bw, per, bw)
    return jnp.stack([g[:, p, :, p, :] for p in range(per)], axis=1).reshape(nt * per, bw, bw)


def _split5(w):
    R, Cc = w.shape[-2:]
    return w.reshape(-1, 2, R // 2, Cc)


def kernel(x, w_in, conv_w, conv_b, w_rg, b_rg, w_ig, b_ig, lru_lambda, sinks, w_branch, w_out, ln1_g, ln1_b, w_ffn_in, w_ffn_out, ln2_g, ln2_b, loss_target, m_w_in, m_conv_w, m_conv_b, m_w_rg, m_b_rg, m_w_ig, m_b_ig, m_lru_lambda, m_sinks, m_w_branch, m_w_out, m_ln1_g, m_ln1_b, m_w_ffn_in, m_w_ffn_out, m_ln2_g, m_ln2_b, v_w_in, v_conv_w, v_conv_b, v_w_rg, v_b_rg, v_w_ig, v_b_ig, v_lru_lambda, v_sinks, v_w_branch, v_w_out, v_ln1_g, v_ln1_b, v_w_ffn_in, v_w_ffn_out, v_ln2_g, v_ln2_b):
    B, S, D = x.shape
    T = B * S
    L = w_in.shape[0]
    H = D // HEAD_DIM
    KVB = D // SWA_GROUP
    FH = w_ffn_out.shape[1] * N_CHIPS
    C = w_in.shape[2] * N_CHIPS
    alpha = (2.0 * L) ** 0.25
    off = {}
    pos = 0
    for nm, wd in (("lx", D), ("lg", D), ("qb", D), ("kb", KVB), ("vb", KVB), ("qc", D), ("kc", D), ("vc", D), ("gt", 3 * D)):
        off[nm] = pos
        pos += wd
    assert pos == C
    cx, cy, cc = lax.axis_index("x"), lax.axis_index("y"), lax.axis_index("c")
    shard = (2 * cx + cy).astype(jnp.int32)
    core_a = cc.astype(jnp.int32).reshape(1)
    shard_a = shard.reshape(1)

    def shard_views(l):
        return [_split5(w_in[l].astype(BF16)), _split5(w_branch[l].astype(BF16)), _split5(w_out[l].astype(BF16)),
                _split5(w_ffn_in[l].astype(BF16)), _split5(w_ffn_out[l].astype(BF16))]

    def as_weights(g):
        return dict(
            w_in=g[0].reshape(N_CHIPS, D, C // N_CHIPS),
            w_branch=g[1].reshape(3, D, D),
            w_out=g[2].reshape(D, D),
            w_ffn_in=g[3].reshape(N_CHIPS, D, 2 * FH // N_CHIPS),
            w_ffn_out=g[4].reshape(FH, D),
        )

    first_views = shard_views(0)
    w_in0 = _gather_weights(first_views[:1])
    full = [dict(w_in=w_in0[0].reshape(N_CHIPS, D, C // N_CHIPS))]
    cw_all = _gather_small(conv_w.reshape(L * CONV_WIDTH, D // N_CHIPS))
    conv_w_full = jnp.transpose(cw_all, (1, 0, 2)).reshape(L, CONV_WIDTH, D)

    def layer_params(l):
        return dict(conv_w=conv_w_full[l], conv_b=conv_b[l][None], w_rg_bd=_block_diag(w_rg[l]), b_rg=b_rg[l][None],
                    w_ig_bd=_block_diag(w_ig[l]), b_ig=b_ig[l][None], lam=lru_lambda[l][None])

    def sink_rows(l, hb):
        sk = sinks[l].reshape(H // hb, 1, hb)
        return jnp.pad(sk, ((0, 0), (0, 0), (0, LANES - hb)))

    hb_b = SWA_HB

    saved = []
    xin = x.reshape(T, D)
    for l in range(L):
        fw, lp = full[l], layer_params(l)
        nxt = shard_views(l + 1) if l + 1 < L else None
        own, ahead = {}, {}
        if l == 0:
            own = {"lru": (3,), "swa": (1, 2, 4)}
            ahead = {"proj": (0,), "dil": (1, 2, 3, 4)} if nxt is not None else {}
        elif nxt is not None:
            ahead = {"lru": (3,), "swa": (0,), "dil": (1, 2, 4)}
        landed_own, landed_next = {}, {}

        def carried(host):
            idx_own, idx_next = own.get(host, ()), ahead.get(host, ())
            views = [first_views[t] for t in idx_own] + [nxt[t] for t in idx_next]
            if not views:
                return None, lambda bufs: None

            def file(bufs):
                for t, buf in zip(idx_own, bufs[:len(idx_own)]):
                    landed_own[t] = buf
                for t, buf in zip(idx_next, bufs[len(idx_own):]):
                    landed_next[t] = buf
            return _gather_rider(views), file

        rider, file = carried("proj")
        proj_kw = dict(mode="nn", name="mm_proj", tm=512, n_outer=True)
        if rider is None:
            proj = _matmul(xin, fw["w_in"], **proj_kw)
        else:
            proj, bufs = _matmul(xin, fw["w_in"], rider=rider, **proj_kw)
            file(bufs)
        proj3 = proj.reshape(B, S, C)
        rider, file = carried("lru")
        res = _lru_fwd(proj3, lp, D=D, x_off=off["lx"], g_off=off["lg"], rider=rider)
        h3, ya3 = res[0], res[1]
        file(res[2:])
        skr = sink_rows(l, hb_b)
        rider, file = carried("swa")
        res = _swa_seq_fwd(proj3, skr, D=D, q_off=off["qb"], k_off=off["kb"], v_off=off["vb"], rider=rider)
        yb3, lse_b = res[0], res[1]
        file(res[2:])
        if l == 0:
            rest = sorted(landed_own)
            got = _gather_forward([landed_own[t] for t in rest], [first_views[t] for t in rest])
            fw = as_weights(w_in0 + got)
            full[0] = fw
        rider, file = carried("dil")
        res = _dil_fwd(proj3, D=D, q_off=off["qc"], k_off=off["kc"], v_off=off["vc"], rider=rider)
        yc3, lse_c = res[0], res[1]
        file(res[2:])
        ya, yb, yc = ya3.reshape(T, D), yb3.reshape(T, D), yc3.reshape(T, D)
        if nxt is not None:
            landed = [landed_next[t] for t in range(len(nxt))]
            res = _branch_fwd([ya, yb, yc], fw["w_branch"], proj, D=D, g_off=off["gt"], rider=_forward_rider(landed))
            branch, merged = res[0], res[1]
            full.append(as_weights(_place_own_shard(res[2:], nxt)))
        else:
            branch, merged = _branch_fwd([ya, yb, yc], fw["w_branch"], proj, D=D, g_off=off["gt"])
        z1, x1 = _matmul(merged, fw["w_out"], mode="nn", name="mm_out_ln", tn=1024, resid=xin, rs=alpha,
                         ln=(ln1_g[l][None], ln1_b[l][None]))
        ffn_h1, ffn_h3, f = _ffn_in_fwd(x1, fw["w_ffn_in"])
        z2, x2 = _matmul(f, fw["w_ffn_out"], mode="nn", name="mm_ffn_out_ln", tn=1024, tk=4096, resid=x1, rs=alpha,
                         ln=(ln2_g[l][None], ln2_b[l][None]))
        saved.append(dict(x=xin, proj=proj, h3=h3, ya=ya, yb=yb, lse_b=lse_b, yc=yc, lse_c=lse_c, branch=branch,
                          merged=merged, z1=z1, x1=x1, ffn_h1=ffn_h1, ffn_h3=ffn_h3, f=f, z2=z2, skr=skr))
        xin = x2

    dx, loss_rows = _loss_head(xin, loss_target.reshape(T, D))
    loss = lax.psum(jnp.sum(loss_rows), ("x", "y", "c"))

    big = {k: [None] * L for k in ("w_in", "w_branch", "w_out", "w_ffn_in", "w_ffn_out")}
    small = [None] * L

    def store_reduced(l, red):
        big["w_in"][l] = red[0].reshape(D, C // N_CHIPS)
        big["w_branch"][l] = red[1].reshape(3, D // N_CHIPS, D)
        big["w_out"][l] = red[2].reshape(D // N_CHIPS, D)
        big["w_ffn_in"][l] = red[3].reshape(D, 2 * FH // N_CHIPS)
        big["w_ffn_out"][l] = red[4].reshape(FH // N_CHIPS, D)

    above = None
    pending = None
    for l in reversed(range(L)):
        fw, lp, sv = full[l], layer_params(l), saved[l]
        dz2, dg2, db2 = _ln_bwd(dx, sv["z2"], ln2_g[l][None])
        g_ffn_out = _matmul(sv["f"], dz2, mode="tn", name="mm_dffn_out_w", tm=1408, tn=1024, tk=1024)
        dhh = _ffn_out_bwd(dz2, fw["w_ffn_out"], sv["ffn_h1"], sv["ffn_h3"])
        dx1 = _matmul(dhh, fw["w_ffn_in"], mode="nt", name="mm_dffn_in_x", tm=1024, tn=1024, resid=dz2, rs=alpha)
        g_ffn_in = _matmul(sv["x1"], dhh, mode="tn", name="mm_dffn_in_w", tm=1024, tk=1024, out_shards=N_CHIPS)
        dz1, dg1, db1 = _ln_bwd(dx1, sv["z1"], ln1_g[l][None])
        g_out = _matmul(sv["merged"], dz1, mode="tn", name="mm_dout_w", tm=1024, tn=1024, tk=1024)
        dbranch, dgates = _branch_bwd(dz1, fw["w_out"], sv["branch"], sv["proj"], D=D, g_off=off["gt"])
        ys = [sv["ya"], sv["yb"], sv["yc"]]
        dys, g_branch = [], []
        for n in range(3):
            dys.append(_matmul(dbranch, fw["w_branch"][n], mode="nt", name="mm_dbranch_x", tn=1024, tk=1024, a_pick=n))
            g_branch.append(_matmul(ys[n], dbranch, mode="tn", name="mm_dbranch_w", tm=1024, tn=1024, tk=1024, b_pick=n))
        proj3 = sv["proj"].reshape(B, S, C)
        r3 = lambda t: t.reshape(B, S, t.shape[-1])
        lru = _lru_bwd(proj3, sv["h3"], r3(dys[0]), lp, D=D, x_off=off["lx"], g_off=off["lg"],
                       rider=None if above is None else _swap_rider(above[1]))
        if above is not None:
            pending = (above[0],) + _reduce_chip(above[1], lru[9:], core_a, shard_a)
        dxr, dgate = lru[0], lru[1]
        dqb, dkb, dvb, dsk = _swa_seq_bwd(proj3, r3(sv["yb"]), sv["lse_b"], r3(dys[1]), sv["skr"], D=D, q_off=off["qb"],
                                      k_off=off["kb"], v_off=off["vb"])
        dil_kw = dict(D=D, q_off=off["qc"], k_off=off["kc"], v_off=off["vc"])
        if pending is None:
            acc = _dil_bwd(proj3, r3(sv["yc"]), sv["lse_c"], r3(dys[2]), **dil_kw)
        else:
            res = _dil_bwd(proj3, r3(sv["yc"]), sv["lse_c"], r3(dys[2]), rider=_exchange_rider(pending[1]), **dil_kw)
            acc = res[:3]
            halves = _chip_sums(pending[2], res[3:])
        f2 = lambda t: t.reshape(T, t.shape[-1]).astype(BF16)
        dproj = jnp.concatenate([f2(dxr), f2(dgate), f2(dqb), f2(dkb), f2(dvb), f2(acc[0]), f2(acc[1]), f2(acc[2])] + dgates, axis=1)
        dx_kw = dict(mode="nt", name="mm_dproj_x", tm=1024, tn=1024, resid=dz1, rs=alpha)
        if pending is None:
            dx = _matmul(dproj, fw["w_in"], **dx_kw)
        else:
            dx, got = _matmul(dproj, fw["w_in"], rider=_join_rider(halves), **dx_kw)
            store_reduced(pending[0], _as_shards(_place_own_half(got, halves)))
        g_in = _matmul(sv["x"], dproj, mode="tn", name="mm_dproj_w", tm=512, tk=1024, out_shards=N_CHIPS)

        g5 = [g_in.reshape(1, N_CHIPS, 2, D // 2, C // N_CHIPS),
              jnp.stack(g_branch).reshape(3, N_CHIPS, 2, D // N_CHIPS // 2, D),
              g_out.reshape(1, N_CHIPS, 2, D // N_CHIPS // 2, D),
              g_ffn_in.reshape(1, N_CHIPS, 2, D // 2, 2 * FH // N_CHIPS),
              g_ffn_out.reshape(1, N_CHIPS, 2, FH // N_CHIPS // 2, D)]
        above = (l, g5)

        dsinks = jnp.sum(dsk, axis=0)[:, 0, :hb_b].reshape(H)
        bw = w_rg.shape[-1]
        small[l] = [lru[2].reshape(-1), lru[3].reshape(-1), _block_diag_grad(lru[4], bw).reshape(-1), lru[5].reshape(-1),
                    _block_diag_grad(lru[6], bw).reshape(-1), lru[7].reshape(-1), lru[8].reshape(-1),
                    jnp.pad(dsinks, (0, LANES - H)), dg1.reshape(-1), db1.reshape(-1), dg2.reshape(-1), db2.reshape(-1)]

    qb, qf = _reduce_chip(above[1], _swap_sibling_halves(above[1]), core_a, shard_a)
    store_reduced(above[0], _reduce_finish(qf, _exchange_chips(qb)))

    sizes = [t.size for t in small[0]]
    flat = jnp.concatenate([t for l in range(L) for t in small[l]])
    n_flat = flat.size
    rows = -(-n_flat // (LANES * 256)) * 256
    flat = jnp.pad(flat, (0, rows * LANES - n_flat)).reshape(rows, LANES)
    pair = _small_pair(flat)

    order = ["w_in", "conv_w", "conv_b", "w_rg", "b_rg", "w_ig", "b_ig", "lru_lambda", "sinks", "w_branch", "w_out",
             "ln1_g", "ln1_b", "w_ffn_in", "w_ffn_out", "ln2_g", "ln2_b"]
    weights = dict(w_in=w_in, conv_w=conv_w, conv_b=conv_b, w_rg=w_rg, b_rg=b_rg, w_ig=w_ig, b_ig=b_ig, lru_lambda=lru_lambda,
                   sinks=sinks, w_branch=w_branch, w_out=w_out, ln1_g=ln1_g, ln1_b=ln1_b, w_ffn_in=w_ffn_in,
                   w_ffn_out=w_ffn_out, ln2_g=ln2_g, ln2_b=ln2_b)
    ms = dict(w_in=m_w_in, conv_w=m_conv_w, conv_b=m_conv_b, w_rg=m_w_rg, b_rg=m_b_rg, w_ig=m_w_ig, b_ig=m_b_ig,
              lru_lambda=m_lru_lambda, sinks=m_sinks, w_branch=m_w_branch, w_out=m_w_out, ln1_g=m_ln1_g, ln1_b=m_ln1_b,
              w_ffn_in=m_w_ffn_in, w_ffn_out=m_w_ffn_out, ln2_g=m_ln2_g, ln2_b=m_ln2_b)
    vs = dict(w_in=v_w_in, conv_w=v_conv_w, conv_b=v_conv_b, w_rg=v_w_rg, b_rg=v_b_rg, w_ig=v_w_ig, b_ig=v_b_ig,
              lru_lambda=v_lru_lambda, sinks=v_sinks, w_branch=v_w_branch, w_out=v_w_out, ln1_g=v_ln1_g, ln1_b=v_ln1_b,
              w_ffn_in=v_w_ffn_in, w_ffn_out=v_w_ffn_out, ln2_g=v_ln2_g, ln2_b=v_ln2_b)
    grads = dict(w_in=jnp.stack(big["w_in"]), w_branch=jnp.stack(big["w_branch"]), w_out=jnp.stack(big["w_out"]),
                 w_ffn_in=jnp.stack(big["w_ffn_in"]), w_ffn_out=jnp.stack(big["w_ffn_out"]))
    deltas, new_m, new_v = {}, {}, {}
    deltas["w_in"], new_m["w_in"], new_v["w_in"], others = _adamw(w_in, grads["w_in"], m_w_in, v_w_in,
                                                                  rider=_small_exchange_rider(pair))
    red_small = _small_chip_sum(pair, others).reshape(-1)
    per_layer = sum(sizes)
    names = ["conv_w", "conv_b", "w_rg", "b_rg", "w_ig", "b_ig", "lru_lambda", "sinks", "ln1_g", "ln1_b", "ln2_g", "ln2_b"]
    sg = {nm: [] for nm in names}
    for l in range(L):
        p = l * per_layer
        for nm, sz in zip(names, sizes):
            sg[nm].append(red_small[p:p + sz])
            p += sz
    grads.update(
        conv_w=lax.dynamic_slice_in_dim(jnp.stack(sg["conv_w"]).reshape(L, CONV_WIDTH, D), shard * (D // N_CHIPS), D // N_CHIPS, axis=2),
        conv_b=jnp.stack(sg["conv_b"]), w_rg=jnp.stack(sg["w_rg"]).reshape(w_rg.shape), b_rg=jnp.stack(sg["b_rg"]),
        w_ig=jnp.stack(sg["w_ig"]).reshape(w_ig.shape), b_ig=jnp.stack(sg["b_ig"]), lru_lambda=jnp.stack(sg["lru_lambda"]),
        sinks=jnp.stack(sg["sinks"])[:, :H], ln1_g=jnp.stack(sg["ln1_g"]), ln1_b=jnp.stack(sg["ln1_b"]),
        ln2_g=jnp.stack(sg["ln2_g"]), ln2_b=jnp.stack(sg["ln2_b"]),
    )

    for nm in order:
        if nm not in deltas:
            deltas[nm], new_m[nm], new_v[nm] = _adamw(weights[nm], grads[nm], ms[nm], vs[nm])
    return (loss, dx.reshape(B, S, D), *[grads[nm] for nm in order], *[deltas[nm] for nm in order],
            *[new_m[nm] for nm in order], *[new_v[nm] for nm in order])
```

```python
import math

import jax
import jax.numpy as jnp
from jax import lax
from jax.experimental import pallas as pl
from jax.experimental.pallas import tpu as pltpu

HEAD_DIM = 64
WIN = 128
DILS = (1, 4, 16)
SWA_GROUP = 4
CONV_WIDTH = 4
LRU_C = 8.0
LN_EPS = 1e-5
NEG_INF = -1e30
N_CHIPS = 4
ADAM_LR, ADAM_B1, ADAM_B2, ADAM_EPS, ADAM_WD, ADAM_STEP = 0.001, 0.9, 0.999, 1e-08, 0.01, 10

LANES = 128
SUBLANES = 8
VMEM_LIMIT = 48 * 1024 * 1024

assert math.log2(HEAD_DIM) % 2 == 0

F32 = jnp.float32
BF16 = jnp.bfloat16
MESH = pl.DeviceIdType.MESH
ANY = pl.BlockSpec(memory_space=pl.ANY)


def _pcall(body, **kw):
    return pl.pallas_call(body, **kw)


def _pcall_comm(body, **kw):
    return pl.pallas_call(body, **kw)


def _params(*sem):
    return pltpu.CompilerParams(dimension_semantics=tuple(sem), vmem_limit_bytes=VMEM_LIMIT)


def _tile(dim, target):
    if dim <= target:
        return dim
    best = None
    for t in range(LANES, target + 1, LANES):
        if dim % t == 0:
            best = t
    assert best is not None, (dim, target)
    return best


def _sigmoid(x):
    return 1.0 / (1.0 + jnp.exp(-x))


def _dot(a, b, dims):
    return lax.dot_general(a, b, (dims, ((), ())), preferred_element_type=F32)


def _dot_nn(a, b):
    return _dot(a, b, ((1,), (0,)))


def _dot_nt(a, b):
    return _dot(a, b, ((1,), (1,)))


def _dot_tn(a, b):
    return _dot(a, b, ((0,), (0,)))


def _matmul(a, b, *, mode, name, out_dtype=F32, tm=512, tn=512, tk=2048, resid=None, rs=1.0, out_shards=0, n_outer=False,
            ln=None, a_pick=0, b_pick=0, rider=None):
    b_sh = b.ndim == 3
    a_st = a.ndim == 3
    if mode == "nn":
        M, K = a.shape[-2:]
        N = b.shape[-1] * (b.shape[0] if b_sh else 1)
    elif mode == "nt":
        M, K = a.shape[-2:]
        N = b.shape[-2]
    else:
        K, M = a.shape
        N = b.shape[-1]
    tm = _tile(M, tm)
    if mode == "nn" and b_sh:
        tn = b.shape[-1]
    elif out_shards:
        tn = N // out_shards
    else:
        tn = _tile(N, tn)
    if mode == "nt" and b_sh:
        tk = b.shape[-1]
    else:
        tk = _tile(K, tk)
    nk = K // tk
    grid = (N // tn, M // tm, nk) if n_outer else (M // tm, N // tn, nk)

    def spec(shape, f):
        return pl.BlockSpec(shape, (lambda g0, g1, k: f(g1, g0, k)) if n_outer else f)

    a_rows = spec((None, tm, tk), lambda i, j, k: (a_pick, i, k)) if a_st else spec((tm, tk), lambda i, j, k: (i, k))
    if mode == "nn":
        a_spec = a_rows
        b_spec = spec((None, tk, tn), lambda i, j, k: (j, k, 0)) if b_sh else spec((tk, tn), lambda i, j, k: (k, j))
        contract = _dot_nn
    elif mode == "nt":
        a_spec = a_rows
        b_spec = spec((None, tn, tk), lambda i, j, k: (k, j, 0)) if b_sh else spec((tn, tk), lambda i, j, k: (j, k))
        contract = _dot_nt
    else:
        a_spec = spec((tk, tm), lambda i, j, k: (k, i))
        b_spec = spec((None, tk, tn), lambda i, j, k: (b_pick, k, j)) if b_sh else spec((tk, tn), lambda i, j, k: (k, j))
        contract = _dot_tn
    if out_shards:
        out_shape = jax.ShapeDtypeStruct((out_shards, M, tn), out_dtype)
        o_spec = spec((None, tm, tn), lambda i, j, k: (j, i, 0))
    else:
        out_shape = jax.ShapeDtypeStruct((M, N), out_dtype)
        o_spec = spec((tm, tn), lambda i, j, k: (i, j))
    in_specs = [a_spec, b_spec]
    args = [a, b]
    if resid is not None:
        in_specs.append(spec((tm, tn), lambda i, j, k: (i, j)))
        args.append(resid)
    if ln is not None:
        assert tn == N and resid is not None and not out_shards
        in_specs += [spec((1, N), lambda i, j, k: (0, 0))] * 2
        args += list(ln)
        out_shape = [out_shape, out_shape, jax.ShapeDtypeStruct((M, N), BF16)]
        o_spec = [o_spec, o_spec, o_spec]
    n_in = len(args)

    def body(*refs):
        a_ref, b_ref = refs[:2]
        r_ref = refs[2] if resid is not None else None
        o_ref = refs[n_in]
        part = contract(a_ref[...].astype(BF16), b_ref[...].astype(BF16))

        def finish(res):
            if resid is not None:
                res = res + rs * r_ref[...]
            o_ref[...] = res.astype(out_dtype)
            if ln is not None:
                g_ref, bb_ref, y_ref = refs[n_in - 2], refs[n_in - 1], refs[n_in + 1]
                zc = res - jnp.mean(res, axis=1, keepdims=True)
                var = jnp.mean(zc * zc, axis=1, keepdims=True)
                y = zc * lax.rsqrt(var + LN_EPS) * g_ref[...] + bb_ref[...]
                y_ref[...] = y
                refs[n_in + 2][...] = y.astype(BF16)

        if nk == 1:
            finish(part)
            return
        acc_ref = refs[-1]
        k = pl.program_id(2)

        @pl.when(k == 0)
        def _():
            acc_ref[...] = part

        @pl.when(jnp.logical_and(k > 0, k < nk - 1))
        def _():
            acc_ref[...] += part

        @pl.when(k == nk - 1)
        def _():
            finish(acc_ref[...] + part)

    if rider is None:
        return _pcall(
            body, name=name, grid=grid, in_specs=in_specs, out_specs=o_spec, out_shape=out_shape,
            scratch_shapes=[pltpu.VMEM((tm, tn), F32)] if nk > 1 else [],
            compiler_params=_params("parallel", "parallel", "arbitrary"),
        )(*args)
    assert ln is None
    res = _call_with_rider(
        rider, body, name=name, grid=grid, in_specs=in_specs, out_specs=[o_spec], out_shape=[out_shape],
        scratch_shapes=[pltpu.VMEM((tm, tn), F32)] if nk > 1 else [], args=args)
    return res[0], list(res[1:])


def _shift_down(x, d, row):
    return jnp.where(row >= d, pltpu.roll(x, d, 0), 0.0)


def _shift_up(x, d, row, n):
    return jnp.where(row < n - d, pltpu.roll(x, n - d, 0), 0.0)


def _log1p(u):
    w = 1.0 + u
    return jnp.where(w == 1.0, u, jnp.log(w) * u / (w - 1.0))


def _gelu_parts(g):
    k = math.sqrt(2.0 / math.pi)
    c = 0.044715
    t = jnp.tanh(k * (g + c * g * g * g))
    val = 0.5 * g * (1.0 + t)
    der = 0.5 * (1.0 + t) + 0.5 * g * (1.0 - t * t) * k * (1.0 + 3.0 * c * g * g)
    return val, der


def _lru_gates(xr, cw_ref, cb_ref, wrg_ref, brg_ref, wig_ref, big_ref, lam_ref, row):
    xc = cw_ref[3:4, :] * xr + cb_ref[...]
    for d in range(1, CONV_WIDTH):
        xc = xc + cw_ref[3 - d:4 - d, :] * _shift_down(xr, d, row)
    xcb = xc.astype(BF16)
    r = _sigmoid(_dot_nn(xcb, wrg_ref[...]) + brg_ref[...])
    ig = _sigmoid(_dot_nn(xcb, wig_ref[...]) + big_ref[...])
    lam = lam_ref[...]
    sp = jnp.maximum(-lam, 0.0) + _log1p(jnp.exp(-jnp.abs(lam)))
    log_a = (-LRU_C) * r * sp
    a = jnp.exp(log_a)
    y2 = 2.0 * log_a
    one_m_a2 = jnp.where(y2 > -0.01, -(y2 + 0.5 * y2 * y2 + (1.0 / 6.0) * y2 * y2 * y2), 1.0 - jnp.exp(y2))
    mult = jnp.sqrt(one_m_a2)
    return xc, r, ig, sp, a, mult


def _scan_local(a, b, row, n, reverse):
    sub = row % SUBLANES
    d = 1
    while d < SUBLANES:
        if reverse:
            keep = sub < SUBLANES - d
            a_s = jnp.where(keep, pltpu.roll(a, n - d, 0), 1.0)
            b_s = jnp.where(keep, pltpu.roll(b, n - d, 0), 0.0)
        else:
            keep = sub >= d
            a_s = jnp.where(keep, pltpu.roll(a, d, 0), 1.0)
            b_s = jnp.where(keep, pltpu.roll(b, d, 0), 0.0)
        b = a * b_s + b
        a = a * a_s
        d *= 2
    return a, b


def _scan_carry(a_ref, b_ref, out_ref, n, reverse):
    ng = n // SUBLANES

    def step(gidx, carry):
        g = (ng - 1 - gidx) if reverse else gidx
        rows = pl.ds(pl.multiple_of(g * SUBLANES, SUBLANES), SUBLANES)
        h = a_ref[rows, :] * carry + b_ref[rows, :]
        out_ref[rows, :] = h
        return h[0:1, :] if reverse else h[SUBLANES - 1:SUBLANES, :]

    lax.fori_loop(0, ng, step, jnp.zeros((1, LANES), F32), unroll=8)


def _lru_specs(B, S, D, C, x_off, g_off):
    nct = D // LANES
    seq = lambda off: pl.BlockSpec((None, S, LANES), lambda ct, b: (b, 0, off // LANES + ct))
    row = lambda r: pl.BlockSpec((r, LANES), lambda ct, b: (0, ct))
    wbd = pl.BlockSpec((None, LANES, LANES), lambda ct, b: (ct, 0, 0))
    return nct, seq, row, wbd


def _lru_fwd(proj3, lp, *, D, x_off, g_off, rider=None):
    B, S, C = proj3.shape
    nct, seq, row, wbd = _lru_specs(B, S, D, C, x_off, g_off)

    def body(xr_ref, g_ref, cw_ref, cb_ref, wrg_ref, brg_ref, wig_ref, big_ref, lam_ref, h_ref, ya_ref, a_s, b_s):
        rowi = lax.broadcasted_iota(jnp.int32, (S, LANES), 0)
        xr = xr_ref[...]
        xc, r, ig, sp, a, mult = _lru_gates(xr, cw_ref, cb_ref, wrg_ref, brg_ref, wig_ref, big_ref, lam_ref, rowi)
        al, bl = _scan_local(a, mult * (ig * xc), rowi, S, False)
        a_s[...] = al
        b_s[...] = bl
        _scan_carry(a_s, b_s, h_ref, S, False)
        gel, _ = _gelu_parts(g_ref[...])
        ya_ref[...] = (h_ref[...] * gel).astype(BF16)

    out_seq = pl.BlockSpec((None, S, LANES), lambda ct, b: (b, 0, ct))
    return _call_with_rider(
        rider, body, name="lru_fwd", grid=(nct, B),
        in_specs=[seq(x_off), seq(g_off), row(CONV_WIDTH), row(1), wbd, row(1), wbd, row(1), row(1)],
        out_specs=[out_seq, out_seq],
        out_shape=[jax.ShapeDtypeStruct((B, S, D), F32), jax.ShapeDtypeStruct((B, S, D), BF16)],
        scratch_shapes=[pltpu.VMEM((S, LANES), F32), pltpu.VMEM((S, LANES), F32)],
        args=[proj3, proj3, lp["conv_w"], lp["conv_b"], lp["w_rg_bd"], lp["b_rg"], lp["w_ig_bd"], lp["b_ig"], lp["lam"]])


def _lru_bwd(proj3, h3, dya3, lp, *, D, x_off, g_off, rider=None):
    B, S, C = proj3.shape
    nct, seq, row, wbd = _lru_specs(B, S, D, C, x_off, g_off)

    def body(xr_ref, g_ref, h_ref, dy_ref, cw_ref, cb_ref, wrg_ref, brg_ref, wig_ref, big_ref, lam_ref,
             dxr_ref, dg_ref, dcw_ref, dcb_ref, dwrg_ref, dbrg_ref, dwig_ref, dbig_ref, dlam_ref, a_s, b_s, l_s):
        first = pl.program_id(1) == 0
        rowi = lax.broadcasted_iota(jnp.int32, (S, LANES), 0)
        xr = xr_ref[...]
        xc, r, ig, sp, a, mult = _lru_gates(xr, cw_ref, cb_ref, wrg_ref, brg_ref, wig_ref, big_ref, lam_ref, rowi)
        h = h_ref[...]
        dy = dy_ref[...]
        gel, dgel = _gelu_parts(g_ref[...])
        dg_ref[...] = (dy * h * dgel).astype(BF16)
        al, bl = _scan_local(_shift_up(a, 1, rowi, S), dy * gel, rowi, S, True)
        a_s[...] = al
        b_s[...] = bl
        _scan_carry(a_s, b_s, l_s, S, True)
        lamb = l_s[...]
        u = ig * xc
        da = lamb * _shift_down(h, 1, rowi)
        dlog_a = da * a - (lamb * u) * (a * a) / mult
        du = lamb * mult
        dpre_r = (dlog_a * ((-LRU_C) * sp)) * r * (1.0 - r)
        dpre_i = (du * xc) * ig * (1.0 - ig)
        dsp = jnp.sum(dlog_a * ((-LRU_C) * r), axis=0, keepdims=True)
        dlam = dsp * (-1.0 / (1.0 + jnp.exp(lam_ref[...])))
        dpr = dpre_r.astype(BF16)
        dpi = dpre_i.astype(BF16)
        dxc = du * ig + _dot_nt(dpr, wrg_ref[...]) + _dot_nt(dpi, wig_ref[...])
        xcb = xc.astype(BF16)
        dwrg = _dot_tn(xcb, dpr)
        dwig = _dot_tn(xcb, dpi)
        dxr = cw_ref[3:4, :] * dxc
        dcw = [jnp.sum(xr * dxc, axis=0, keepdims=True)]
        for d in range(1, CONV_WIDTH):
            dxr = dxr + cw_ref[3 - d:4 - d, :] * _shift_up(dxc, d, rowi, S)
            dcw.append(jnp.sum(_shift_down(xr, d, rowi) * dxc, axis=0, keepdims=True))
        dxr_ref[...] = dxr.astype(BF16)
        dcw_rows = jnp.concatenate(dcw[::-1], axis=0)
        sums = ((dcw_ref, dcw_rows), (dcb_ref, jnp.sum(dxc, axis=0, keepdims=True)), (dwrg_ref, dwrg),
                (dbrg_ref, jnp.sum(dpre_r, axis=0, keepdims=True)), (dwig_ref, dwig),
                (dbig_ref, jnp.sum(dpre_i, axis=0, keepdims=True)), (dlam_ref, dlam))

        @pl.when(first)
        def _():
            for ref, val in sums:
                ref[...] = val

        @pl.when(jnp.logical_not(first))
        def _():
            for ref, val in sums:
                ref[...] += val

    out_seq = pl.BlockSpec((None, S, LANES), lambda ct, b: (b, 0, ct))
    f = lambda shape: jax.ShapeDtypeStruct(shape, F32)
    nb = D // LANES
    return _call_with_rider(
        rider, body, name="lru_bwd", grid=(nct, B),
        in_specs=[seq(x_off), seq(g_off), out_seq, out_seq, row(CONV_WIDTH), row(1), wbd, row(1), wbd, row(1), row(1)],
        out_specs=[out_seq, out_seq, row(CONV_WIDTH), row(1), wbd, row(1), wbd, row(1), row(1)],
        out_shape=[jax.ShapeDtypeStruct((B, S, D), BF16), jax.ShapeDtypeStruct((B, S, D), BF16),
                   f((CONV_WIDTH, D)), f((1, D)), f((nb, LANES, LANES)), f((1, D)), f((nb, LANES, LANES)), f((1, D)), f((1, D))],
        scratch_shapes=[pltpu.VMEM((S, LANES), F32)] * 3, semantics=("parallel", "arbitrary"),
        args=[proj3, proj3, h3, dya3, lp["conv_w"], lp["conv_b"], lp["w_rg_bd"], lp["b_rg"], lp["w_ig_bd"], lp["b_ig"], lp["lam"]])


def _pair_stack(x, lo):
    z = jnp.zeros_like(x)
    return jnp.concatenate([jnp.where(lo, x, z), jnp.where(lo, z, x)], axis=0).astype(BF16)


def _pair_join(y2, lo):
    return jnp.where(lo, y2[:WIN], y2[WIN:])


def _pair_col(xb):
    return jnp.concatenate([xb[:, 0:1], xb[:, HEAD_DIM:HEAD_DIM + 1]], axis=0)


def _pair_bcast(col, lo):
    return jnp.where(lo, jnp.broadcast_to(col[:WIN], (WIN, LANES)), jnp.broadcast_to(col[WIN:], (WIN, LANES)))


def _dil_rows(it, d, S):
    if d == 1:
        cur = pl.multiple_of(it * WIN, WIN)
        prev = pl.multiple_of(jnp.maximum(it - 1, 0) * WIN, WIN)
        return pl.ds(cur, WIN), pl.ds(prev, WIN), it > 0
    r, i = it % d, it // d
    cur = i * (WIN * d) + r
    prev = jnp.maximum(i - 1, 0) * (WIN * d) + r
    return pl.ds(cur, WIN, stride=d), pl.ds(prev, WIN, stride=d), i > 0


def _dil_bias(two_blocks, stack=2):
    nk = 2 * WIN if two_blocks else WIN
    qi = lax.broadcasted_iota(jnp.int32, (stack * WIN, nk), 0) & (WIN - 1)
    kj = lax.broadcasted_iota(jnp.int32, (stack * WIN, nk), 1)
    if not two_blocks:
        return jnp.where(kj <= qi, 0.0, NEG_INF), None
    cur = jnp.logical_and(kj >= WIN, kj - WIN <= qi)
    prev = jnp.logical_and(kj < WIN, kj >= qi)
    return jnp.where(jnp.logical_or(cur, prev), 0.0, NEG_INF), jnp.where(cur, 0.0, NEG_INF)


def _dil_specs(B, S, D, C, offs):
    grid = (B, D // LANES)
    seq = lambda off: pl.BlockSpec((None, S, LANES), lambda b, p: (b, 0, off // LANES + p))
    return grid, [seq(o) for o in offs], seq(0)


def _call_with_rider(rider, body, *, name, grid, in_specs, out_specs, out_shape, scratch_shapes, args, semantics=None):
    if rider is None:
        return _pcall(body, name=name, grid=grid, in_specs=in_specs, out_specs=out_specs, out_shape=out_shape,
                      scratch_shapes=scratch_shapes, compiler_params=_params(*(semantics or ("parallel",) * len(grid))))(*args)
    n_in, n_out, n_sc = len(in_specs), len(out_specs), len(scratch_shapes)
    r_in, r_out = len(rider["ins"]), len(rider["out_shapes"])

    def wrapped(*refs):
        p = 0
        own_in = refs[p:p + n_in]; p += n_in
        rid_in = refs[p:p + r_in]; p += r_in
        own_out = refs[p:p + n_out]; p += n_out
        rid_out = refs[p:p + r_out]; p += r_out
        own_sc = refs[p:p + n_sc]; p += n_sc
        send_sems, recv_sems = refs[p:p + 2]
        ids = [pl.program_id(a) for a in range(len(grid))]
        first = ids[0] == 0
        last = ids[0] == grid[0] - 1
        for a in range(1, len(grid)):
            first = jnp.logical_and(first, ids[a] == 0)
            last = jnp.logical_and(last, ids[a] == grid[a] - 1)

        @pl.when(first)
        def _():
            rider["start"](rid_in, rid_out, send_sems, recv_sems)

        body(*own_in, *own_out, *own_sc)

        @pl.when(last)
        def _():
            rider["finish"](rid_in, rid_out, send_sems, recv_sems)

    aliases = {n_in + t: n_out + t for t in range(r_in)} if rider.get("in_place") else {}
    res = _pcall_comm(
        wrapped, name=name + "_" + rider["name"], grid=grid, in_specs=list(in_specs) + [ANY] * r_in,
        out_specs=list(out_specs) + [ANY] * r_out, out_shape=list(out_shape) + list(rider["out_shapes"]),
        scratch_shapes=list(scratch_shapes) + [pltpu.SemaphoreType.DMA((rider["n"],)), pltpu.SemaphoreType.DMA((rider["n"],))],
        input_output_aliases=aliases, compiler_params=_params(*(("arbitrary",) * len(grid))),
    )(*args, *rider["ins"])
    return res


def _dil_fwd(proj3, *, D, q_off, k_off, v_off, rider=None):
    B, S, C = proj3.shape
    n_it = S // WIN
    scale = HEAD_DIM ** -0.5
    grid, in_specs, out_spec = _dil_specs(B, S, D, C, (q_off, k_off, v_off))

    def body(q_ref, k_ref, v_ref, o_ref, l_ref):
        lo = lax.broadcasted_iota(jnp.int32, (WIN, LANES), 1) < HEAD_DIM
        for c, d in enumerate(DILS):
            two = S // d > WIN
            bias_all, bias_first = _dil_bias(two)

            def step(it, _, c=c, d=d, two=two, bias_all=bias_all, bias_first=bias_first):
                cur, prev, later = _dil_rows(it, d, S)
                q2 = _pair_stack(q_ref[cur, :] * scale, lo)
                if two:
                    k2 = jnp.concatenate([k_ref[prev, :], k_ref[cur, :]], axis=0).astype(BF16)
                    v2 = jnp.concatenate([v_ref[prev, :], v_ref[cur, :]], axis=0).astype(BF16)
                    bias = jnp.where(later, bias_all, bias_first)
                else:
                    k2, v2, bias = k_ref[cur, :].astype(BF16), v_ref[cur, :].astype(BF16), bias_all
                s2 = _dot_nt(q2, k2) + bias
                m2 = jnp.max(s2, axis=1, keepdims=True)
                p2 = jnp.exp(s2 - m2)
                den = jnp.sum(p2, axis=1, keepdims=True)
                oc = _pair_join(_dot_nn(p2.astype(BF16), v2) / den, lo)
                lc = _pair_bcast(m2 + jnp.log(den), lo)
                if c == 0:
                    o_ref[cur, :] = oc
                    l_ref[cur, :] = lc
                else:
                    l_old = l_ref[cur, :]
                    mx = jnp.maximum(l_old, lc)
                    e_old, e_new = jnp.exp(l_old - mx), jnp.exp(lc - mx)
                    tot = e_old + e_new
                    o_ref[cur, :] = (e_old * o_ref[cur, :] + e_new * oc) / tot
                    l_ref[cur, :] = mx + jnp.log(tot)
                return 0

            lax.fori_loop(0, n_it, step, 0, unroll=16)

    return _call_with_rider(
        rider, body, name="dil_fwd", grid=grid, in_specs=in_specs, out_specs=[out_spec, out_spec],
        out_shape=[jax.ShapeDtypeStruct((B, S, D), F32)] * 2, scratch_shapes=[], args=[proj3, proj3, proj3])


def _dil_bwd(proj3, o3, l3, do3, *, D, q_off, k_off, v_off, rider=None):
    B, S, C = proj3.shape
    n_it = S // WIN
    scale = HEAD_DIM ** -0.5
    grid, in_specs, out_spec = _dil_specs(B, S, D, C, (q_off, k_off, v_off))

    def body(q_ref, k_ref, v_ref, o_ref, l_ref, do_ref, dq_ref, dk_ref, dv_ref, dd_s, dq_s, dk_s, dv_s):
        lo = lax.broadcasted_iota(jnp.int32, (WIN, LANES), 1) < HEAD_DIM
        lo_s = lax.broadcasted_iota(jnp.int32, (S, LANES), 1) < HEAD_DIM
        prod = do_ref[...] * o_ref[...]
        d_lo = jnp.sum(jnp.where(lo_s, prod, 0.0), axis=1, keepdims=True)
        d_hi = jnp.sum(jnp.where(lo_s, 0.0, prod), axis=1, keepdims=True)
        dd_s[...] = jnp.where(lo_s, jnp.broadcast_to(d_lo, (S, LANES)), jnp.broadcast_to(d_hi, (S, LANES)))
        dq_s[...] = jnp.zeros_like(dq_s)
        dk_s[...] = jnp.zeros_like(dk_s)
        dv_s[...] = jnp.zeros_like(dv_s)
        for d in DILS:
            two = S // d > WIN
            bias_all, bias_first = _dil_bias(two)

            def step(it, _, d=d, two=two, bias_all=bias_all, bias_first=bias_first):
                cur, prev, later = _dil_rows(it, d, S)
                q2 = _pair_stack(q_ref[cur, :] * scale, lo)
                do2 = _pair_stack(do_ref[cur, :], lo)
                l2 = _pair_col(l_ref[cur, :])
                dd2 = _pair_col(dd_s[cur, :])
                if two:
                    k2 = jnp.concatenate([k_ref[prev, :], k_ref[cur, :]], axis=0).astype(BF16)
                    v2 = jnp.concatenate([v_ref[prev, :], v_ref[cur, :]], axis=0).astype(BF16)
                    bias = jnp.where(later, bias_all, bias_first)
                else:
                    k2, v2, bias = k_ref[cur, :].astype(BF16), v_ref[cur, :].astype(BF16), bias_all
                p2 = jnp.exp(_dot_nt(q2, k2) + bias - l2)
                ds2 = (p2 * (_dot_nt(do2, v2) - dd2)).astype(BF16)
                dq_s[cur, :] += _pair_join(_dot_nn(ds2, k2), lo) * scale
                dk2 = _dot_tn(ds2, q2)
                dv2 = _dot_tn(p2.astype(BF16), do2)
                if two:
                    dk_s[prev, :] += dk2[:WIN]
                    dv_s[prev, :] += dv2[:WIN]
                    dk_s[cur, :] += dk2[WIN:]
                    dv_s[cur, :] += dv2[WIN:]
                else:
                    dk_s[cur, :] += dk2
                    dv_s[cur, :] += dv2
                return 0

            lax.fori_loop(0, n_it, step, 0, unroll=16)
        dq_ref[...] = dq_s[...].astype(BF16)
        dk_ref[...] = dk_s[...].astype(BF16)
        dv_ref[...] = dv_s[...].astype(BF16)

    return _call_with_rider(
        rider, body, name="dil_bwd", grid=grid, in_specs=in_specs + [out_spec] * 3, out_specs=[out_spec] * 3,
        out_shape=[jax.ShapeDtypeStruct((B, S, D), BF16)] * 3, scratch_shapes=[pltpu.VMEM((S, LANES), F32)] * 4,
        args=[proj3, proj3, proj3, o3, l3, do3])


SWA_HB = 2 * SWA_GROUP


def _to_half(x, src, dst, lo):
    if src != dst:
        x = pltpu.roll(x, HEAD_DIM, 1)
    return jnp.where(lo if dst == 0 else jnp.logical_not(lo), x, 0.0)


def _swa_kv(g):
    return 2 * g // SWA_GROUP


SWA_STACKS = ((0, 1), (2, 3))


def _swa_stack(ref, gs, lo, dtype, rows=slice(None)):
    parts = []
    for g in gs:
        x = ref[rows, g * LANES:(g + 1) * LANES]
        parts += [_to_half(x, 0, _swa_kv(g), lo), _to_half(x, 1, _swa_kv(g), lo)]
    return jnp.concatenate(parts, axis=0).astype(dtype)


def _swa_unstack(y, gs, lo):
    out = []
    for t, g in enumerate(gs):
        even, odd = y[2 * t * WIN:(2 * t + 1) * WIN], y[(2 * t + 1) * WIN:(2 * t + 2) * WIN]
        out.append(_to_half(even, _swa_kv(g), 0, lo) + _to_half(odd, _swa_kv(g), 1, lo))
    return out


def _swa_cols(x, gs):
    cols = []
    for g in gs:
        cols += [jnp.broadcast_to(x[:, 2 * g:2 * g + 1], (WIN, 1)), jnp.broadcast_to(x[:, 2 * g + 1:2 * g + 2], (WIN, 1))]
    return jnp.concatenate(cols, axis=0)


SWA_UNROLL = 16
SWA_UNROLL_BWD = 8


def _swa_seq_specs(B, S, D, q_off, k_off, v_off):
    qw = SWA_HB * HEAD_DIM
    assert q_off % qw == 0 and k_off % LANES == 0 and v_off % LANES == 0 and D % qw == 0
    seq = lambda width, off: pl.BlockSpec((None, S, width), lambda b, hh: (b, 0, off // width + hh))
    sink = pl.BlockSpec((None, 1, LANES), lambda b, hh: (hh, 0, 0))
    return (B, D // qw), seq, sink, qw


def _swa_rows(it):
    cur = pl.ds(pl.multiple_of(it * WIN, WIN), WIN)
    prev = pl.ds(pl.multiple_of(jnp.maximum(it - 1, 0) * WIN, WIN), WIN)
    return cur, prev, it > 0


def _swa_seq_fwd(proj3, sinks, *, D, q_off, k_off, v_off, rider=None):
    B, S, C = proj3.shape
    scale = HEAD_DIM ** -0.5
    grid, seq, sink, qw = _swa_seq_specs(B, S, D, q_off, k_off, v_off)
    nhb = D // qw

    def body(q_ref, k_ref, v_ref, sk_ref, o_ref, lse_ref):
        lo = lax.broadcasted_iota(jnp.int32, (WIN, LANES), 1) < HEAD_DIM
        lane = lax.broadcasted_iota(jnp.int32, (WIN, LANES), 1)
        sk = sk_ref[...]
        biases = [_dil_bias(True, 2 * len(gs)) for gs in SWA_STACKS]

        def step(it, _):
            cur, prev, later = _swa_rows(it)
            k2 = jnp.concatenate([k_ref[prev, :], k_ref[cur, :]], axis=0).astype(BF16)
            v2 = jnp.concatenate([v_ref[prev, :], v_ref[cur, :]], axis=0).astype(BF16)
            lse_acc = jnp.zeros((WIN, LANES), F32)
            for gs, (bias_all, bias_first) in zip(SWA_STACKS, biases):
                bias = jnp.where(later, bias_all, bias_first)
                qs = _swa_stack(q_ref, gs, lo, BF16, cur)
                sks = _swa_cols(sk, gs)
                s = _dot_nt(qs, k2) * scale + bias
                m = jnp.maximum(jnp.max(s, axis=1, keepdims=True), sks)
                p = jnp.exp(s - m)
                den = jnp.sum(p, axis=1, keepdims=True) + jnp.exp(sks - m)
                for g, grp in zip(gs, _swa_unstack(_dot_nn(p.astype(BF16), v2) / den, gs, lo)):
                    o_ref[cur, g * LANES:(g + 1) * LANES] = grp
                ls = m + jnp.log(den)
                for t, g in enumerate(gs):
                    lse_acc = jnp.where(lane == 2 * g, ls[2 * t * WIN:(2 * t + 1) * WIN], lse_acc)
                    lse_acc = jnp.where(lane == 2 * g + 1, ls[(2 * t + 1) * WIN:(2 * t + 2) * WIN], lse_acc)
            lse_ref[cur, :] = lse_acc
            return 0

        lax.fori_loop(0, S // WIN, step, 0, unroll=SWA_UNROLL)

    return _call_with_rider(
        rider, body, name="swa_fwd", grid=grid,
        in_specs=[seq(qw, q_off), seq(LANES, k_off), seq(LANES, v_off), sink],
        out_specs=[seq(qw, 0), seq(LANES, 0)],
        out_shape=[jax.ShapeDtypeStruct((B, S, D), F32), jax.ShapeDtypeStruct((B, S, nhb * LANES), F32)],
        scratch_shapes=[], args=[proj3, proj3, proj3, sinks])


def _swa_seq_bwd(proj3, o3, lse3, do3, sinks, *, D, q_off, k_off, v_off):
    B, S, C = proj3.shape
    scale = HEAD_DIM ** -0.5
    grid, seq, sink, qw = _swa_seq_specs(B, S, D, q_off, k_off, v_off)
    nhb = D // qw
    KV = D // SWA_GROUP

    def body(q_ref, k_ref, v_ref, o_ref, l_ref, do_ref, sk_ref, dq_ref, dk_ref, dv_ref, dsk_ref, dk_s, dv_s):
        lo = lax.broadcasted_iota(jnp.int32, (WIN, LANES), 1) < HEAD_DIM
        lane = lax.broadcasted_iota(jnp.int32, (1, LANES), 1)
        sk = sk_ref[...]
        biases = [_dil_bias(True, 2 * len(gs)) for gs in SWA_STACKS]
        dk_s[...] = jnp.zeros_like(dk_s)
        dv_s[...] = jnp.zeros_like(dv_s)

        dsk_ref[...] = jnp.zeros_like(dsk_ref)

        def step(it, _):
            cur, prev, later = _swa_rows(it)
            k2 = jnp.concatenate([k_ref[prev, :], k_ref[cur, :]], axis=0).astype(BF16)
            v2 = jnp.concatenate([v_ref[prev, :], v_ref[cur, :]], axis=0).astype(BF16)
            lse = l_ref[cur, :]
            dk2 = jnp.zeros((2 * WIN, LANES), F32)
            dv2 = jnp.zeros((2 * WIN, LANES), F32)
            dsk_acc = jnp.zeros((1, LANES), F32)
            for gs, (bias_all, bias_first) in zip(SWA_STACKS, biases):
                bias = jnp.where(later, bias_all, bias_first)
                qs = _swa_stack(q_ref, gs, lo, BF16, cur)
                dos = _swa_stack(do_ref, gs, lo, BF16, cur)
                dds = []
                for g in gs:
                    prod = do_ref[cur, g * LANES:(g + 1) * LANES] * o_ref[cur, g * LANES:(g + 1) * LANES]
                    dds += [jnp.sum(jnp.where(lo, prod, 0.0), axis=1, keepdims=True),
                            jnp.sum(jnp.where(lo, 0.0, prod), axis=1, keepdims=True)]
                dds = jnp.concatenate(dds, axis=0)
                ls = _swa_cols(lse, gs)
                ps = jnp.exp(_dot_nt(qs, k2) * scale + bias - ls)
                dss = (ps * (_dot_nt(dos, v2) - dds) * scale).astype(BF16)
                for g, grp in zip(gs, _swa_unstack(_dot_nn(dss, k2), gs, lo)):
                    dq_ref[cur, g * LANES:(g + 1) * LANES] = grp.astype(BF16)
                dk2 = dk2 + _dot_tn(dss, qs)
                dv2 = dv2 + _dot_tn(ps.astype(BF16), dos)
                dsks = jnp.exp(_swa_cols(sk, gs) - ls) * dds
                for t, g in enumerate(gs):
                    for u in range(2):
                        rows = slice((2 * t + u) * WIN, (2 * t + u + 1) * WIN)
                        dsk_acc = dsk_acc + jnp.where(lane == 2 * g + u, -jnp.sum(dsks[rows], axis=0, keepdims=True), 0.0)
            dk_s[prev, :] += dk2[:WIN]
            dv_s[prev, :] += dv2[:WIN]
            dk_s[cur, :] += dk2[WIN:]
            dv_s[cur, :] += dv2[WIN:]
            dsk_ref[...] += dsk_acc
            return 0

        lax.fori_loop(0, S // WIN, step, 0, unroll=SWA_UNROLL_BWD)
        dk_ref[...] = dk_s[...].astype(BF16)
        dv_ref[...] = dv_s[...].astype(BF16)

    return _pcall(
        body, name="swa_bwd", grid=grid,
        in_specs=[seq(qw, q_off), seq(LANES, k_off), seq(LANES, v_off), seq(qw, 0), seq(LANES, 0), seq(qw, 0), sink],
        out_specs=[seq(qw, 0), seq(LANES, 0), seq(LANES, 0), pl.BlockSpec((None, None, 1, LANES), lambda b, hh: (b, hh, 0, 0))],
        out_shape=[jax.ShapeDtypeStruct((B, S, D), BF16), jax.ShapeDtypeStruct((B, S, KV), BF16),
                   jax.ShapeDtypeStruct((B, S, KV), BF16), jax.ShapeDtypeStruct((B, nhb, 1, LANES), F32)],
        scratch_shapes=[pltpu.VMEM((S, LANES), F32), pltpu.VMEM((S, LANES), F32)],
        compiler_params=_params("parallel", "parallel"),
    )(proj3, proj3, proj3, o3, lse3, do3, sinks)


def _branch_fwd(ys, wb, proj, *, D, g_off, rider=None):
    T = proj.shape[0]
    tm, tn = _tile(T, 256), _tile(D, 512)
    n = len(ys)

    def body(*refs):
        y_refs, w_ref, g_refs, br_ref, mg_ref = refs[:n], refs[n], refs[n + 1:2 * n + 1], refs[2 * n + 1], refs[2 * n + 2]
        acc = None
        for k in range(n):
            br = _dot_nn(y_refs[k][...].astype(BF16), w_ref[k])
            br_ref[k] = br
            term = _sigmoid(g_refs[k][...]) * br
            acc = term if acc is None else acc + term
        mg_ref[...] = acc.astype(BF16)

    gate = lambda k: pl.BlockSpec((tm, tn), lambda i, j: (i, (g_off + k * D) // tn + j))
    return _call_with_rider(
        rider, body, name="branch_fwd", grid=(T // tm, D // tn),
        in_specs=[pl.BlockSpec((tm, D), lambda i, j: (i, 0))] * n + [pl.BlockSpec((n, D, tn), lambda i, j: (0, 0, j))]
        + [gate(k) for k in range(n)],
        out_specs=[pl.BlockSpec((n, tm, tn), lambda i, j: (0, i, j)), pl.BlockSpec((tm, tn), lambda i, j: (i, j))],
        out_shape=[jax.ShapeDtypeStruct((n, T, D), F32), jax.ShapeDtypeStruct((T, D), BF16)],
        scratch_shapes=[], args=[*ys, wb, *([proj] * n)])


def _branch_bwd(dmix, w_out, branch, proj, *, D, g_off):
    n, T, _ = branch.shape
    tm, tn = _tile(T, 512), _tile(D, 512)

    def body(dy_ref, w_ref, br_ref, *rest):
        g_refs, db_ref, dg_refs = rest[:n], rest[n], rest[n + 1:]
        dm = _dot_nt(dy_ref[...].astype(BF16), w_ref[...])
        for k in range(n):
            sg = _sigmoid(g_refs[k][...])
            db_ref[k] = (sg * dm).astype(BF16)
            dg_refs[k][...] = (dm * br_ref[k] * sg * (1.0 - sg)).astype(BF16)

    gate = lambda k: pl.BlockSpec((tm, tn), lambda i, j: (i, (g_off + k * D) // tn + j))
    blk = pl.BlockSpec((tm, tn), lambda i, j: (i, j))
    res = _pcall(
        body, name="branch_bwd", grid=(T // tm, D // tn),
        in_specs=[pl.BlockSpec((tm, D), lambda i, j: (i, 0)), pl.BlockSpec((tn, D), lambda i, j: (j, 0)),
                  pl.BlockSpec((n, tm, tn), lambda i, j: (0, i, j))] + [gate(k) for k in range(n)],
        out_specs=[pl.BlockSpec((n, tm, tn), lambda i, j: (0, i, j))] + [blk] * n,
        out_shape=[jax.ShapeDtypeStruct((n, T, D), BF16)] + [jax.ShapeDtypeStruct((T, D), BF16)] * n,
        compiler_params=_params("parallel", "parallel"),
    )(dmix, w_out, branch, *([proj] * n))
    return res[0], list(res[1:])


def _ln_bwd(dout, z, g):
    T, D = z.shape
    tm = _tile(T, 512)

    def body(do_ref, z_ref, g_ref, dz_ref, dg_ref, db_ref):
        z = z_ref[...]
        do = do_ref[...]
        mu = jnp.mean(z, axis=1, keepdims=True)
        zc = z - mu
        rstd = lax.rsqrt(jnp.mean(zc * zc, axis=1, keepdims=True) + LN_EPS)
        xhat = zc * rstd
        dxh = do * g_ref[...]
        dz_ref[...] = rstd * (dxh - jnp.mean(dxh, axis=1, keepdims=True) - xhat * jnp.mean(dxh * xhat, axis=1, keepdims=True))
        dg = jnp.sum(do * xhat, axis=0, keepdims=True)
        db = jnp.sum(do, axis=0, keepdims=True)
        first = pl.program_id(0) == 0

        @pl.when(first)
        def _():
            dg_ref[...] = dg
            db_ref[...] = db

        @pl.when(jnp.logical_not(first))
        def _():
            dg_ref[...] += dg
            db_ref[...] += db

    blk = pl.BlockSpec((tm, D), lambda i: (i, 0))
    vec = pl.BlockSpec((1, D), lambda i: (0, 0))
    return _pcall(
        body, name="ln_bwd", grid=(T // tm,), in_specs=[blk, blk, vec], out_specs=[blk, vec, vec],
        out_shape=[jax.ShapeDtypeStruct((T, D), F32), jax.ShapeDtypeStruct((1, D), F32), jax.ShapeDtypeStruct((1, D), F32)],
        compiler_params=_params("arbitrary"),
    )(dout, z, g)


def _ffn_in_fwd(x1, w_sh):
    T, D = x1.shape
    ns, _, nsh = w_sh.shape
    half = ns // 2
    Fh = half * nsh
    tm = _tile(T, 512)

    def body(x_ref, wa_ref, wb_ref, h1_ref, h3_ref, f_ref):
        xb = x_ref[...].astype(BF16)
        h1 = _dot_nn(xb, wa_ref[...])
        h3 = _dot_nn(xb, wb_ref[...])
        h1_ref[...] = h1
        h3_ref[...] = h3
        f_ref[...] = (h1 * _sigmoid(h1) * h3).astype(BF16)

    cols = pl.BlockSpec((tm, nsh), lambda j, i: (i, j))
    return _pcall(
        body, name="ffn_in_fwd", grid=(half, T // tm),
        in_specs=[pl.BlockSpec((tm, D), lambda j, i: (i, 0)), pl.BlockSpec((None, D, nsh), lambda j, i: (j, 0, 0)),
                  pl.BlockSpec((None, D, nsh), lambda j, i: (j + half, 0, 0))],
        out_specs=[cols, cols, cols],
        out_shape=[jax.ShapeDtypeStruct((T, Fh), F32), jax.ShapeDtypeStruct((T, Fh), F32), jax.ShapeDtypeStruct((T, Fh), BF16)],
        compiler_params=_params("parallel", "parallel"),
    )(x1, w_sh, w_sh)


def _ffn_out_bwd(dy, w_ffn_out, h1, h3):
    T, Fh = h1.shape
    D = w_ffn_out.shape[1]
    tm = _tile(T, 256)

    def body(dy_ref, w_ref, h1_ref, h3_ref, o_ref):
        d = _dot_nt(dy_ref[...].astype(BF16), w_ref[...])
        h1v = h1_ref[...]
        sg = _sigmoid(h1v)
        o_ref[:, :Fh] = (d * h3_ref[...] * sg * (1.0 + h1v * (1.0 - sg))).astype(BF16)
        o_ref[:, Fh:] = (d * h1v * sg).astype(BF16)

    blk = pl.BlockSpec((tm, Fh), lambda i: (i, 0))
    return _pcall(
        body, name="ffn_out_bwd", grid=(T // tm,),
        in_specs=[pl.BlockSpec((tm, D), lambda i: (i, 0)), pl.BlockSpec((Fh, D), lambda i: (0, 0)), blk, blk],
        out_specs=pl.BlockSpec((tm, 2 * Fh), lambda i: (i, 0)),
        out_shape=jax.ShapeDtypeStruct((T, 2 * Fh), BF16), compiler_params=_params("parallel"),
    )(dy, w_ffn_out, h1, h3)


def _loss_head(y, target):
    T, D = y.shape
    tm = _tile(T, 512)

    def body(y_ref, t_ref, dy_ref, l_ref):
        e = y_ref[...] - t_ref[...]
        dy_ref[...] = e * (1.0 / D)
        sq = e * e
        part = sq[:, 0:LANES]
        for c in range(1, D // LANES):
            part = part + sq[:, c * LANES:(c + 1) * LANES]
        part = jnp.sum(part, axis=0, keepdims=True) * (0.5 / D)
        first = pl.program_id(0) == 0

        @pl.when(first)
        def _():
            l_ref[...] = part

        @pl.when(jnp.logical_not(first))
        def _():
            l_ref[...] += part

    blk = pl.BlockSpec((tm, D), lambda i: (i, 0))
    return _pcall(
        body, name="loss_head", grid=(T // tm,), in_specs=[blk, blk],
        out_specs=[blk, pl.BlockSpec((1, LANES), lambda i: (0, 0))],
        out_shape=[jax.ShapeDtypeStruct((T, D), F32), jax.ShapeDtypeStruct((1, LANES), F32)],
        compiler_params=_params("arbitrary"),
    )(y, target)


def _as_rows(a):
    return a.reshape(-1, a.shape[-1])


def _adamw(w, g, m, v, rider=None):
    w2, g2, m2, v2 = (_as_rows(t) for t in (w, g, m, v))
    R, Cc = w2.shape
    cap = max(SUBLANES, min(512, (256 * 1024) // Cc))
    tm = R if (R <= cap or R % SUBLANES) else max(t for t in range(SUBLANES, cap + 1, SUBLANES) if R % t == 0)
    c1 = 1.0 - ADAM_B1 ** ADAM_STEP
    c2 = 1.0 - ADAM_B2 ** ADAM_STEP

    def body(w_ref, g_ref, m_ref, v_ref, d_ref, nm_ref, nv_ref):
        gg = g_ref[...]
        nm = ADAM_B1 * m_ref[...] + (1.0 - ADAM_B1) * gg
        nv = ADAM_B2 * v_ref[...] + (1.0 - ADAM_B2) * (gg * gg)
        d_ref[...] = (-ADAM_LR) * ((nm / c1) / (jnp.sqrt(nv / c2) + ADAM_EPS) + ADAM_WD * w_ref[...])
        nm_ref[...] = nm
        nv_ref[...] = nv

    blk = pl.BlockSpec((tm, Cc), lambda i: (i, 0))
    res = _call_with_rider(
        rider, body, name="adamw", grid=(R // tm,), in_specs=[blk] * 4, out_specs=[blk] * 3,
        out_shape=[jax.ShapeDtypeStruct((R, Cc), F32)] * 3, scratch_shapes=[], args=[w2, g2, m2, v2])
    return tuple(t.reshape(w.shape) for t in res[:3]) + tuple(res[3:])


def _where_am_i():
    x, y, c = lax.axis_index("x"), lax.axis_index("y"), lax.axis_index("c")
    chips = [(1 - x, y), (x, 1 - y), (1 - x, 1 - y)]
    return x, y, c, chips


def _remote(src, dst, send_sems, recv_sems, k, to):
    return pltpu.make_async_remote_copy(src_ref=src, dst_ref=dst, send_sem=send_sems.at[k], recv_sem=recv_sems.at[k],
                                        device_id=to, device_id_type=MESH)


def _comm_call(body, name, ins, out_shapes, n_remote, n_local):
    return _pcall_comm(
        body, name=name, in_specs=[ANY] * len(ins), out_specs=[ANY] * len(out_shapes), out_shape=out_shapes,
        scratch_shapes=[pltpu.SemaphoreType.DMA((n_remote,)), pltpu.SemaphoreType.DMA((n_remote,)),
                        pltpu.SemaphoreType.DMA((max(n_local, 1),))],
    )(*ins)


def _gather_weights(shards):
    n = len(shards)

    def body(*refs):
        ins, outs = refs[:n], refs[n:2 * n]
        send_sems, recv_sems, local_sems = refs[2 * n:]
        x, y, c, chips = _where_am_i()
        s = 2 * x + y
        sib = (x, y, 1 - c)
        first = []
        for t in range(n):
            for j, (cx, cy) in enumerate(chips):
                first.append(_remote(ins[t].at[:, c], outs[t].at[:, s, c], send_sems, recv_sems, 6 * t + j, (cx, cy, c)))
        for cp in first:
            cp.start()
        passed = []
        for j, (cx, cy) in enumerate(chips):
            sj = 2 * cx + cy
            for t in range(n):
                land = outs[t].at[:, sj, c]
                _remote(land, land, send_sems, recv_sems, 6 * t + j, (cx, cy, c)).wait_recv()
                fw = _remote(land, land, send_sems, recv_sems, 6 * t + 3 + j, sib)
                fw.start()
                passed.append(fw)
        for j, (cx, cy) in enumerate(chips):
            sj = 2 * cx + cy
            for t in range(n):
                land = outs[t].at[:, sj, 1 - c]
                _remote(land, land, send_sems, recv_sems, 6 * t + 3 + j, sib).wait_recv()
        for cp in first + passed:
            cp.wait_send()

    out_shapes = [jax.ShapeDtypeStruct((t.shape[0], N_CHIPS) + t.shape[1:], t.dtype) for t in shards]
    got = _comm_call(body, "gather_weights", shards, out_shapes, 6 * n, 0)
    s = 2 * lax.axis_index("x") + lax.axis_index("y")
    return [lax.dynamic_update_slice(g, t[:, None], (0, s, 0, 0, 0)) for g, t in zip(got, shards)]


def _gather_rider(shards):
    n = len(shards)

    def copies(ins, outs, send_sems, recv_sems):
        x, y, c, chips = _where_am_i()
        s = 2 * x + y
        return [_remote(ins[t].at[:, c], outs[t].at[:, s, c], send_sems, recv_sems, 3 * t + j, (cx, cy, c))
                for t in range(n) for j, (cx, cy) in enumerate(chips)]

    def start(ins, outs, send_sems, recv_sems):
        for cp in copies(ins, outs, send_sems, recv_sems):
            cp.start()

    def finish(ins, outs, send_sems, recv_sems):
        x, y, c, chips = _where_am_i()
        for t in range(n):
            for j, (cx, cy) in enumerate(chips):
                land = outs[t].at[:, 2 * cx + cy, c]
                _remote(land, land, send_sems, recv_sems, 3 * t + j, (cx, cy, c)).wait_recv()
        for cp in copies(ins, outs, send_sems, recv_sems):
            cp.wait_send()

    out_shapes = [jax.ShapeDtypeStruct((t.shape[0], N_CHIPS) + t.shape[1:], t.dtype) for t in shards]
    return dict(name="gather", ins=list(shards), out_shapes=out_shapes, n=3 * n, start=start, finish=finish)


def _gather_forward(landed, shards):
    n = len(landed)

    def body(*refs):
        outs = refs[n:2 * n]
        send_sems, recv_sems, _ = refs[2 * n:]
        x, y, c, chips = _where_am_i()
        sib = (x, y, 1 - c)
        cps = []
        for t in range(n):
            for j, (cx, cy) in enumerate(chips):
                land = outs[t].at[:, 2 * cx + cy, c]
                cps.append(_remote(land, land, send_sems, recv_sems, 3 * t + j, sib))
        for cp in cps:
            cp.start()
        for t in range(n):
            for j, (cx, cy) in enumerate(chips):
                land = outs[t].at[:, 2 * cx + cy, 1 - c]
                _remote(land, land, send_sems, recv_sems, 3 * t + j, sib).wait_recv()
        for cp in cps:
            cp.wait_send()

    got = _pcall_comm(
        body, name="gather_forward", in_specs=[ANY] * n, out_specs=[ANY] * n,
        out_shape=[jax.ShapeDtypeStruct(t.shape, t.dtype) for t in landed], input_output_aliases={t: t for t in range(n)},
        scratch_shapes=[pltpu.SemaphoreType.DMA((3 * n,)), pltpu.SemaphoreType.DMA((3 * n,)), pltpu.SemaphoreType.DMA((1,))],
    )(*landed)
    s = 2 * lax.axis_index("x") + lax.axis_index("y")
    return [lax.dynamic_update_slice(g, t[:, None], (0, s, 0, 0, 0)) for g, t in zip(got, shards)]


def _forward_rider(landed):
    n = len(landed)

    def copies(outs, send_sems, recv_sems):
        x, y, c, chips = _where_am_i()
        cps = []
        for t in range(n):
            for j, (cx, cy) in enumerate(chips):
                land = outs[t].at[:, 2 * cx + cy, c]
                cps.append(_remote(land, land, send_sems, recv_sems, 3 * t + j, (x, y, 1 - c)))
        return cps

    def start(ins, outs, send_sems, recv_sems):
        for cp in copies(outs, send_sems, recv_sems):
            cp.start()

    def finish(ins, outs, send_sems, recv_sems):
        x, y, c, chips = _where_am_i()
        for t in range(n):
            for j, (cx, cy) in enumerate(chips):
                land = outs[t].at[:, 2 * cx + cy, 1 - c]
                _remote(land, land, send_sems, recv_sems, 3 * t + j, (x, y, 1 - c)).wait_recv()
        for cp in copies(outs, send_sems, recv_sems):
            cp.wait_send()

    out_shapes = [jax.ShapeDtypeStruct(t.shape, t.dtype) for t in landed]
    return dict(name="forward", ins=list(landed), out_shapes=out_shapes, n=3 * n, start=start, finish=finish, in_place=True)


def _place_own_shard(got, shards):
    s = 2 * lax.axis_index("x") + lax.axis_index("y")
    return [lax.dynamic_update_slice(g, t[:, None], (0, s, 0, 0, 0)) for g, t in zip(got, shards)]


def _gather_small(v):
    def body(v_ref, out_ref, send_sems, recv_sems, local_sems):
        x, y, c, chips = _where_am_i()
        s = 2 * x + y
        mine = pltpu.make_async_copy(v_ref, out_ref.at[s], local_sems.at[0])
        mine.start()
        sends = [_remote(v_ref, out_ref.at[s], send_sems, recv_sems, j, (cx, cy, c)) for j, (cx, cy) in enumerate(chips)]
        for cp in sends:
            cp.start()
        for j, (cx, cy) in enumerate(chips):
            land = out_ref.at[2 * cx + cy]
            _remote(land, land, send_sems, recv_sems, j, (cx, cy, c)).wait_recv()
        for cp in sends:
            cp.wait_send()
        mine.wait()

    return _comm_call(body, "gather_small", [v], [jax.ShapeDtypeStruct((N_CHIPS,) + v.shape, v.dtype)], 3, 1)[0]


def _swap_sibling_halves(grads):
    n = len(grads)

    def body(*refs):
        ins, outs = refs[:n], refs[n:2 * n]
        send_sems, recv_sems, _ = refs[2 * n:]
        x, y, c, _chips = _where_am_i()
        sib = (x, y, 1 - c)
        cps = [_remote(ins[t].at[:, :, 1 - c], outs[t], send_sems, recv_sems, t, sib) for t in range(n)]
        for cp in cps:
            cp.start()
        for cp in cps:
            cp.wait()

    out_shapes = [jax.ShapeDtypeStruct(g.shape[:2] + g.shape[3:], g.dtype) for g in grads]
    return _comm_call(body, "grad_swap_halves", grads, out_shapes, n, 0)


def _exchange_chips(parts):
    n = len(parts)

    def body(*refs):
        ins, outs = refs[:n], refs[n:2 * n]
        send_sems, recv_sems, _ = refs[2 * n:]
        x, y, c, chips = _where_am_i()
        cps = []
        for t in range(n):
            for j, (cx, cy) in enumerate(chips):
                cps.append(_remote(ins[t].at[:, 2 * cx + cy], outs[t].at[j], send_sems, recv_sems, 3 * t + j, (cx, cy, c)))
        for cp in cps:
            cp.start()
        for cp in cps:
            cp.wait()

    out_shapes = [jax.ShapeDtypeStruct((3, p.shape[0]) + p.shape[2:], p.dtype) for p in parts]
    return _comm_call(body, "grad_exchange_chips", parts, out_shapes, 3 * n, 0)


def _exchange_rider(parts):
    n = len(parts)

    def copies(ins, outs, send_sems, recv_sems):
        x, y, c, chips = _where_am_i()
        return [_remote(ins[t].at[:, 2 * cx + cy], outs[t].at[j], send_sems, recv_sems, 3 * t + j, (cx, cy, c))
                for t in range(n) for j, (cx, cy) in enumerate(chips)]

    def start(ins, outs, send_sems, recv_sems):
        for cp in copies(ins, outs, send_sems, recv_sems):
            cp.start()

    def finish(ins, outs, send_sems, recv_sems):
        for cp in copies(ins, outs, send_sems, recv_sems):
            cp.wait()

    out_shapes = [jax.ShapeDtypeStruct((3, p.shape[0]) + p.shape[2:], p.dtype) for p in parts]
    return dict(name="exchange", ins=list(parts), out_shapes=out_shapes, n=3 * n, start=start, finish=finish)


def _join_sibling_halves(halves):
    n = len(halves)

    def body(*refs):
        ins, outs = refs[:n], refs[n:2 * n]
        send_sems, recv_sems, local_sems = refs[2 * n:]
        x, y, c, _chips = _where_am_i()
        sib = (x, y, 1 - c)
        cps = [_remote(ins[t], outs[t].at[:, c], send_sems, recv_sems, t, sib) for t in range(n)]
        for cp in cps:
            cp.start()
        for t in range(n):
            land = outs[t].at[:, 1 - c]
            _remote(land, land, send_sems, recv_sems, t, sib).wait_recv()
        for cp in cps:
            cp.wait_send()

    out_shapes = [jax.ShapeDtypeStruct((h.shape[0], 2) + h.shape[1:], h.dtype) for h in halves]
    got = _comm_call(body, "grad_join_halves", halves, out_shapes, n, 0)
    c = lax.axis_index("c")
    return [lax.dynamic_update_slice(g, h[:, None], (0, c, 0, 0)) for g, h in zip(got, halves)]


def _join_rider(halves):
    n = len(halves)

    def copies(ins, outs, send_sems, recv_sems):
        x, y, c, _chips = _where_am_i()
        return [_remote(ins[t], outs[t].at[:, c], send_sems, recv_sems, t, (x, y, 1 - c)) for t in range(n)]

    def start(ins, outs, send_sems, recv_sems):
        for cp in copies(ins, outs, send_sems, recv_sems):
            cp.start()

    def finish(ins, outs, send_sems, recv_sems):
        x, y, c, _chips = _where_am_i()
        for t in range(n):
            land = outs[t].at[:, 1 - c]
            _remote(land, land, send_sems, recv_sems, t, (x, y, 1 - c)).wait_recv()
        for cp in copies(ins, outs, send_sems, recv_sems):
            cp.wait_send()

    out_shapes = [jax.ShapeDtypeStruct((h.shape[0], 2) + h.shape[1:], h.dtype) for h in halves]
    return dict(name="join", ins=list(halves), out_shapes=out_shapes, n=n, start=start, finish=finish)


def _place_own_half(got, halves):
    c = lax.axis_index("c")
    return [lax.dynamic_update_slice(g, h[:, None], (0, c, 0, 0)) for g, h in zip(got, halves)]


def _small_exchange_rider(v):
    def copies(ins, outs, send_sems, recv_sems):
        x, y, c, chips = _where_am_i()
        return [_remote(ins[0], outs[0].at[j], send_sems, recv_sems, j, (cx, cy, c)) for j, (cx, cy) in enumerate(chips)]

    def start(ins, outs, send_sems, recv_sems):
        for cp in copies(ins, outs, send_sems, recv_sems):
            cp.start()

    def finish(ins, outs, send_sems, recv_sems):
        for cp in copies(ins, outs, send_sems, recv_sems):
            cp.wait()

    return dict(name="small_exchange", ins=[v], out_shapes=[jax.ShapeDtypeStruct((3,) + v.shape, v.dtype)], n=3,
                start=start, finish=finish)


def _swap_small(v):
    def body(v_ref, out_ref, send_sems, recv_sems, _):
        x, y, c, _chips = _where_am_i()
        cp = _remote(v_ref, out_ref, send_sems, recv_sems, 0, (x, y, 1 - c))
        cp.start()
        cp.wait()

    return _comm_call(body, "small_swap", [v], [jax.ShapeDtypeStruct(v.shape, v.dtype)], 1, 0)[0]


def _sum_rows(name, terms, out_dtypes):
    R, Cc = terms[0].shape
    tm = R if R <= 256 else max(t for t in range(16, 257, 16) if R % t == 0)
    n = len(terms)

    def body(*refs):
        acc = refs[0][...].astype(F32)
        for r in refs[1:n]:
            acc = acc + r[...].astype(F32)
        for o in refs[n:]:
            o[...] = acc.astype(o.dtype)

    blk = pl.BlockSpec((tm, Cc), lambda i: (i, 0))
    return _pcall(
        body, name=name, grid=(R // tm,), in_specs=[blk] * n, out_specs=[blk] * len(out_dtypes),
        out_shape=[jax.ShapeDtypeStruct((R, Cc), d) for d in out_dtypes], compiler_params=_params("parallel"),
    )(*terms)


def _pair_sum(g5, r1, core, shard):
    A4, _, Rh, Cc = g5.shape
    A = A4 // N_CHIPS
    tr = Rh if Rh <= 256 else max(t for t in range(16, 257, 16) if Rh % t == 0)

    def body(core_ref, shard_ref, g_ref, r_ref, qb_ref, qf_ref):
        q = g_ref[...] + r_ref[...]
        qb_ref[...] = q.astype(BF16)

        @pl.when(pl.program_id(2) == shard_ref[0])
        def _():
            qf_ref[...] = q

    grid_spec = pltpu.PrefetchScalarGridSpec(
        num_scalar_prefetch=2, grid=(A, Rh // tr, N_CHIPS),
        in_specs=[pl.BlockSpec((None, None, tr, Cc), lambda a, r, sh, core, shard: (a * N_CHIPS + sh, core[0], r, 0)),
                  pl.BlockSpec((None, tr, Cc), lambda a, r, sh, core, shard: (a * N_CHIPS + sh, r, 0))],
        out_specs=[pl.BlockSpec((None, tr, Cc), lambda a, r, sh, core, shard: (a * N_CHIPS + sh, r, 0)),
                   pl.BlockSpec((None, tr, Cc), lambda a, r, sh, core, shard: (a, r, 0))],
    )
    return _pcall(
        body, name="grad_pair_sum", grid_spec=grid_spec,
        out_shape=[jax.ShapeDtypeStruct((A4, Rh, Cc), BF16), jax.ShapeDtypeStruct((A, Rh, Cc), F32)],
        compiler_params=_params("parallel", "parallel", "arbitrary"),
    )(core, shard, g5, r1)


def _swap_rider(grads):
    n = len(grads)

    def copies(ins, outs, send_sems, recv_sems):
        x, y, c, _chips = _where_am_i()
        return [_remote(ins[t].at[:, :, 1 - c], outs[t], send_sems, recv_sems, t, (x, y, 1 - c)) for t in range(n)]

    def start(ins, outs, send_sems, recv_sems):
        for cp in copies(ins, outs, send_sems, recv_sems):
            cp.start()

    def finish(ins, outs, send_sems, recv_sems):
        for cp in copies(ins, outs, send_sems, recv_sems):
            cp.wait()

    out_shapes = [jax.ShapeDtypeStruct(g.shape[:2] + g.shape[3:], g.dtype) for g in grads]
    return dict(name="swap", ins=list(grads), out_shapes=out_shapes, n=n, start=start, finish=finish)


def _reduce_chip(grads, r1, core, shard):
    qb, qf = [], []
    for g, r in zip(grads, r1):
        A, _, _, Rh, Cc = g.shape
        b, f = _pair_sum(g.reshape(A * N_CHIPS, 2, Rh, Cc), r.reshape(A * N_CHIPS, Rh, Cc), core, shard)
        qb.append(b.reshape(A, N_CHIPS, Rh, Cc))
        qf.append(f)
    return qb, qf


def _reduce_finish(qf, r2):
    return _as_shards(_join_sibling_halves(_chip_sums(qf, r2)))


def _chip_sums(qf, r2):
    halves = []
    for f, r in zip(qf, r2):
        A, Rh, Cc = f.shape
        terms = [f.reshape(A * Rh, Cc)] + [r[j].reshape(A * Rh, Cc) for j in range(3)]
        halves.append(_sum_rows("grad_chip_sum", terms, [F32])[0].reshape(A, Rh, Cc))
    return halves


def _as_shards(full):
    return [t.reshape(t.shape[0], 2 * t.shape[2], t.shape[3]) for t in full]


def _small_pair(v):
    return _sum_rows("small_pair_sum", [v, _swap_small(v)], [F32])[0]


def _small_chip_sum(pair, others):
    x, y = lax.axis_index("x"), lax.axis_index("y")
    s = 2 * x + y
    stack = jnp.concatenate([pair[None], others], axis=0)
    src = jnp.stack([s, s ^ 2, s ^ 1, s ^ 3])
    order = jnp.argsort(src)
    terms = [lax.dynamic_index_in_dim(stack, order[k], 0, keepdims=False) for k in range(N_CHIPS)]
    return _sum_rows("small_chip_sum", terms, [F32])[0]


def _block_diag(w):
    nb, bw, _ = w.shape
    per = LANES // bw
    w = w.reshape(nb // per, per, bw, bw)
    eye = jnp.eye(per, dtype=w.dtype)
    bd = jnp.einsum("tpij,pq->tpiqj", w, eye).reshape(nb // per, LANES, LANES)
    return bd.astype(BF16)


def _block_diag_grad(g, bw):
    nt = g.shape[0]
    per = LANES // bw
    g = g.reshape(nt, per, bw, per, bw)
    return jnp.stack([g[:, p, :, p, :] for p in range(per)], axis=1).reshape(nt * per, bw, bw)


def _split5(w):
    R, Cc = w.shape[-2:]
    return w.reshape(-1, 2, R // 2, Cc)


def kernel(x, w_in, conv_w, conv_b, w_rg, b_rg, w_ig, b_ig, lru_lambda, sinks, w_branch, w_out, ln1_g, ln1_b, w_ffn_in, w_ffn_out, ln2_g, ln2_b, loss_target, m_w_in, m_conv_w, m_conv_b, m_w_rg, m_b_rg, m_w_ig, m_b_ig, m_lru_lambda, m_sinks, m_w_branch, m_w_out, m_ln1_g, m_ln1_b, m_w_ffn_in, m_w_ffn_out, m_ln2_g, m_ln2_b, v_w_in, v_conv_w, v_conv_b, v_w_rg, v_b_rg, v_w_ig, v_b_ig, v_lru_lambda, v_sinks, v_w_branch, v_w_out, v_ln1_g, v_ln1_b, v_w_ffn_in, v_w_ffn_out, v_ln2_g, v_ln2_b):
    B, S, D = x.shape
    T = B * S
    L = w_in.shape[0]
    H = D // HEAD_DIM
    KVB = D // SWA_GROUP
    FH = w_ffn_out.shape[1] * N_CHIPS
    C = w_in.shape[2] * N_CHIPS
    alpha = (2.0 * L) ** 0.25
    off = {}
    pos = 0
    for nm, wd in (("lx", D), ("lg", D), ("qb", D), ("kb", KVB), ("vb", KVB), ("qc", D), ("kc", D), ("vc", D), ("gt", 3 * D)):
        off[nm] = pos
        pos += wd
    assert pos == C
    cx, cy, cc = lax.axis_index("x"), lax.axis_index("y"), lax.axis_index("c")
    shard = (2 * cx + cy).astype(jnp.int32)
    core_a = cc.astype(jnp.int32).reshape(1)
    shard_a = shard.reshape(1)

    def shard_views(l):
        return [_split5(w_in[l].astype(BF16)), _split5(w_branch[l].astype(BF16)), _split5(w_out[l].astype(BF16)),
                _split5(w_ffn_in[l].astype(BF16)), _split5(w_ffn_out[l].astype(BF16))]

    def as_weights(g):
        return dict(
            w_in=g[0].reshape(N_CHIPS, D, C // N_CHIPS),
            w_branch=g[1].reshape(3, D, D),
            w_out=g[2].reshape(D, D),
            w_ffn_in=g[3].reshape(N_CHIPS, D, 2 * FH // N_CHIPS),
            w_ffn_out=g[4].reshape(FH, D),
        )

    first_views = shard_views(0)
    w_in0 = _gather_weights(first_views[:1])
    full = [dict(w_in=w_in0[0].reshape(N_CHIPS, D, C // N_CHIPS))]
    cw_all = _gather_small(conv_w.reshape(L * CONV_WIDTH, D // N_CHIPS))
    conv_w_full = jnp.transpose(cw_all, (1, 0, 2)).reshape(L, CONV_WIDTH, D)

    def layer_params(l):
        return dict(conv_w=conv_w_full[l], conv_b=conv_b[l][None], w_rg_bd=_block_diag(w_rg[l]), b_rg=b_rg[l][None],
                    w_ig_bd=_block_diag(w_ig[l]), b_ig=b_ig[l][None], lam=lru_lambda[l][None])

    def sink_rows(l, hb):
        sk = sinks[l].reshape(H // hb, 1, hb)
        return jnp.pad(sk, ((0, 0), (0, 0), (0, LANES - hb)))

    hb_b = SWA_HB

    saved = []
    xin = x.reshape(T, D)
    xin_b = xin.astype(BF16)
    for l in range(L):
        fw, lp = full[l], layer_params(l)
        nxt = shard_views(l + 1) if l + 1 < L else None
        own, ahead = {}, {}
        if l == 0:
            own = {"lru": (3,), "swa": (1, 2, 4)}
            ahead = {"proj": (0,), "dil": (1, 2, 3, 4)} if nxt is not None else {}
        elif nxt is not None:
            ahead = {"lru": (3,), "swa": (0,), "dil": (1, 2, 4)}
        landed_own, landed_next = {}, {}

        def carried(host):
            idx_own, idx_next = own.get(host, ()), ahead.get(host, ())
            views = [first_views[t] for t in idx_own] + [nxt[t] for t in idx_next]
            if not views:
                return None, lambda bufs: None

            def file(bufs):
                for t, buf in zip(idx_own, bufs[:len(idx_own)]):
                    landed_own[t] = buf
                for t, buf in zip(idx_next, bufs[len(idx_own):]):
                    landed_next[t] = buf
            return _gather_rider(views), file

        rider, file = carried("proj")
        proj_kw = dict(mode="nn", name="mm_proj", tm=512, n_outer=True)
        if rider is None:
            proj = _matmul(xin_b, fw["w_in"], **proj_kw)
        else:
            proj, bufs = _matmul(xin_b, fw["w_in"], rider=rider, **proj_kw)
            file(bufs)
        proj3 = proj.reshape(B, S, C)
        rider, file = carried("lru")
        res = _lru_fwd(proj3, lp, D=D, x_off=off["lx"], g_off=off["lg"], rider=rider)
        h3, ya3 = res[0], res[1]
        file(res[2:])
        skr = sink_rows(l, hb_b)
        rider, file = carried("swa")
        res = _swa_seq_fwd(proj3, skr, D=D, q_off=off["qb"], k_off=off["kb"], v_off=off["vb"], rider=rider)
        yb3, lse_b = res[0], res[1]
        file(res[2:])
        if l == 0:
            rest = sorted(landed_own)
            got = _gather_forward([landed_own[t] for t in rest], [first_views[t] for t in rest])
            fw = as_weights(w_in0 + got)
            full[0] = fw
        rider, file = carried("dil")
        res = _dil_fwd(proj3, D=D, q_off=off["qc"], k_off=off["kc"], v_off=off["vc"], rider=rider)
        yc3, lse_c = res[0], res[1]
        file(res[2:])
        ya, yb, yc = ya3.reshape(T, D), yb3.reshape(T, D), yc3.reshape(T, D)
        if nxt is not None:
            landed = [landed_next[t] for t in range(len(nxt))]
            res = _branch_fwd([ya, yb, yc], fw["w_branch"], proj, D=D, g_off=off["gt"], rider=_forward_rider(landed))
            branch, merged = res[0], res[1]
            full.append(as_weights(_place_own_shard(res[2:], nxt)))
        else:
            branch, merged = _branch_fwd([ya, yb, yc], fw["w_branch"], proj, D=D, g_off=off["gt"])
        z1, x1, x1_b = _matmul(merged, fw["w_out"], mode="nn", name="mm_out_ln", tn=1024, resid=xin, rs=alpha,
                               ln=(ln1_g[l][None], ln1_b[l][None]))
        ffn_h1, ffn_h3, f = _ffn_in_fwd(x1_b, fw["w_ffn_in"])
        z2, x2, x2_b = _matmul(f, fw["w_ffn_out"], mode="nn", name="mm_ffn_out_ln", tn=1024, tk=4096, resid=x1, rs=alpha,
                               ln=(ln2_g[l][None], ln2_b[l][None]))
        saved.append(dict(x=xin_b, proj=proj, h3=h3, ya=ya, yb=yb, lse_b=lse_b, yc=yc, lse_c=lse_c, branch=branch,
                          merged=merged, z1=z1, x1=x1_b, ffn_h1=ffn_h1, ffn_h3=ffn_h3, f=f, z2=z2, skr=skr))
        xin, xin_b = x2, x2_b

    dx, loss_rows = _loss_head(xin, loss_target.reshape(T, D))
    loss = lax.psum(jnp.sum(loss_rows), ("x", "y", "c"))

    big = {k: [None] * L for k in ("w_in", "w_branch", "w_out", "w_ffn_in", "w_ffn_out")}
    small = [None] * L

    def store_reduced(l, red):
        big["w_in"][l] = red[0].reshape(D, C // N_CHIPS)
        big["w_branch"][l] = red[1].reshape(3, D // N_CHIPS, D)
        big["w_out"][l] = red[2].reshape(D // N_CHIPS, D)
        big["w_ffn_in"][l] = red[3].reshape(D, 2 * FH // N_CHIPS)
        big["w_ffn_out"][l] = red[4].reshape(FH // N_CHIPS, D)

    above = None
    pending = None
    for l in reversed(range(L)):
        fw, lp, sv = full[l], layer_params(l), saved[l]
        dz2, dg2, db2 = _ln_bwd(dx, sv["z2"], ln2_g[l][None])
        g_ffn_out = _matmul(sv["f"], dz2, mode="tn", name="mm_dffn_out_w", tm=1408, tn=1024, tk=1024)
        dhh = _ffn_out_bwd(dz2, fw["w_ffn_out"], sv["ffn_h1"], sv["ffn_h3"])
        dx1 = _matmul(dhh, fw["w_ffn_in"], mode="nt", name="mm_dffn_in_x", tm=1024, tn=1024, resid=dz2, rs=alpha)
        g_ffn_in = _matmul(sv["x1"], dhh, mode="tn", name="mm_dffn_in_w", tm=1024, tk=1024, out_shards=N_CHIPS)
        dz1, dg1, db1 = _ln_bwd(dx1, sv["z1"], ln1_g[l][None])
        g_out = _matmul(sv["merged"], dz1, mode="tn", name="mm_dout_w", tm=1024, tn=1024, tk=1024)
        dbranch, dgates = _branch_bwd(dz1, fw["w_out"], sv["branch"], sv["proj"], D=D, g_off=off["gt"])
        ys = [sv["ya"], sv["yb"], sv["yc"]]
        dys, g_branch = [], []
        for n in range(3):
            dys.append(_matmul(dbranch, fw["w_branch"][n], mode="nt", name="mm_dbranch_x", tn=1024, tk=1024, a_pick=n))
            g_branch.append(_matmul(ys[n], dbranch, mode="tn", name="mm_dbranch_w", tm=1024, tn=1024, tk=1024, b_pick=n))
        proj3 = sv["proj"].reshape(B, S, C)
        r3 = lambda t: t.reshape(B, S, t.shape[-1])
        lru = _lru_bwd(proj3, sv["h3"], r3(dys[0]), lp, D=D, x_off=off["lx"], g_off=off["lg"],
                       rider=None if above is None else _swap_rider(above[1]))
        if above is not None:
            pending = (above[0],) + _reduce_chip(above[1], lru[9:], core_a, shard_a)
        dxr, dgate = lru[0], lru[1]
        dqb, dkb, dvb, dsk = _swa_seq_bwd(proj3, r3(sv["yb"]), sv["lse_b"], r3(dys[1]), sv["skr"], D=D, q_off=off["qb"],
                                      k_off=off["kb"], v_off=off["vb"])
        dil_kw = dict(D=D, q_off=off["qc"], k_off=off["kc"], v_off=off["vc"])
        if pending is None:
            acc = _dil_bwd(proj3, r3(sv["yc"]), sv["lse_c"], r3(dys[2]), **dil_kw)
        else:
            res = _dil_bwd(proj3, r3(sv["yc"]), sv["lse_c"], r3(dys[2]), rider=_exchange_rider(pending[1]), **dil_kw)
            acc = res[:3]
            halves = _chip_sums(pending[2], res[3:])
        f2 = lambda t: t.reshape(T, t.shape[-1]).astype(BF16)
        dproj = jnp.concatenate([f2(dxr), f2(dgate), f2(dqb), f2(dkb), f2(dvb), f2(acc[0]), f2(acc[1]), f2(acc[2])] + dgates, axis=1)
        dx_kw = dict(mode="nt", name="mm_dproj_x", tm=1024, tn=1024, resid=dz1, rs=alpha)
        if pending is None:
            dx = _matmul(dproj, fw["w_in"], **dx_kw)
        else:
            dx, got = _matmul(dproj, fw["w_in"], rider=_join_rider(halves), **dx_kw)
            store_reduced(pending[0], _as_shards(_place_own_half(got, halves)))
        g_in = _matmul(sv["x"], dproj, mode="tn", name="mm_dproj_w", tm=512, tk=1024, out_shards=N_CHIPS)

        g5 = [g_in.reshape(1, N_CHIPS, 2, D // 2, C // N_CHIPS),
              jnp.stack(g_branch).reshape(3, N_CHIPS, 2, D // N_CHIPS // 2, D),
              g_out.reshape(1, N_CHIPS, 2, D // N_CHIPS // 2, D),
              g_ffn_in.reshape(1, N_CHIPS, 2, D // 2, 2 * FH // N_CHIPS),
              g_ffn_out.reshape(1, N_CHIPS, 2, FH // N_CHIPS // 2, D)]
        above = (l, g5)

        dsinks = jnp.sum(dsk, axis=0)[:, 0, :hb_b].reshape(H)
        bw = w_rg.shape[-1]
        small[l] = [lru[2].reshape(-1), lru[3].reshape(-1), _block_diag_grad(lru[4], bw).reshape(-1), lru[5].reshape(-1),
                    _block_diag_grad(lru[6], bw).reshape(-1), lru[7].reshape(-1), lru[8].reshape(-1),
                    jnp.pad(dsinks, (0, LANES - H)), dg1.reshape(-1), db1.reshape(-1), dg2.reshape(-1), db2.reshape(-1)]

    qb, qf = _reduce_chip(above[1], _swap_sibling_halves(above[1]), core_a, shard_a)
    store_reduced(above[0], _reduce_finish(qf, _exchange_chips(qb)))

    sizes = [t.size for t in small[0]]
    flat = jnp.concatenate([t for l in range(L) for t in small[l]])
    n_flat = flat.size
    rows = -(-n_flat // (LANES * 256)) * 256
    flat = jnp.pad(flat, (0, rows * LANES - n_flat)).reshape(rows, LANES)
    pair = _small_pair(flat)

    order = ["w_in", "conv_w", "conv_b", "w_rg", "b_rg", "w_ig", "b_ig", "lru_lambda", "sinks", "w_branch", "w_out",
             "ln1_g", "ln1_b", "w_ffn_in", "w_ffn_out", "ln2_g", "ln2_b"]
    weights = dict(w_in=w_in, conv_w=conv_w, conv_b=conv_b, w_rg=w_rg, b_rg=b_rg, w_ig=w_ig, b_ig=b_ig, lru_lambda=lru_lambda,
                   sinks=sinks, w_branch=w_branch, w_out=w_out, ln1_g=ln1_g, ln1_b=ln1_b, w_ffn_in=w_ffn_in,
                   w_ffn_out=w_ffn_out, ln2_g=ln2_g, ln2_b=ln2_b)
    ms = dict(w_in=m_w_in, conv_w=m_conv_w, conv_b=m_conv_b, w_rg=m_w_rg, b_rg=m_b_rg, w_ig=m_w_ig, b_ig=m_b_ig,
              lru_lambda=m_lru_lambda, sinks=m_sinks, w_branch=m_w_branch, w_out=m_w_out, ln1_g=m_ln1_g, ln1_b=m_ln1_b,
              w_ffn_in=m_w_ffn_in, w_ffn_out=m_w_ffn_out, ln2_g=m_ln2_g, ln2_b=m_ln2_b)
    vs = dict(w_in=v_w_in, conv_w=v_conv_w, conv_b=v_conv_b, w_rg=v_w_rg, b_rg=v_b_rg, w_ig=v_w_ig, b_ig=v_b_ig,
              lru_lambda=v_lru_lambda, sinks=v_sinks, w_branch=v_w_branch, w_out=v_w_out, ln1_g=v_ln1_g, ln1_b=v_ln1_b,
              w_ffn_in=v_w_ffn_in, w_ffn_out=v_w_ffn_out, ln2_g=v_ln2_g, ln2_b=v_ln2_b)
    grads = dict(w_in=jnp.stack(big["w_in"]), w_branch=jnp.stack(big["w_branch"]), w_out=jnp.stack(big["w_out"]),
                 w_ffn_in=jnp.stack(big["w_ffn_in"]), w_ffn_out=jnp.stack(big["w_ffn_out"]))
    deltas, new_m, new_v = {}, {}, {}
    deltas["w_in"], new_m["w_in"], new_v["w_in"], others = _adamw(w_in, grads["w_in"], m_w_in, v_w_in,
                                                                  rider=_small_exchange_rider(pair))
    red_small = _small_chip_sum(pair, others).reshape(-1)
    per_layer = sum(sizes)
    names = ["conv_w", "conv_b", "w_rg", "b_rg", "w_ig", "b_ig", "lru_lambda", "sinks", "ln1_g", "ln1_b", "ln2_g", "ln2_b"]
    sg = {nm: [] for nm in names}
    for l in range(L):
        p = l * per_layer
        for nm, sz in zip(names, sizes):
            sg[nm].append(red_small[p:p + sz])
            p += sz
    grads.update(
        conv_w=lax.dynamic_slice_in_dim(jnp.stack(sg["conv_w"]).reshape(L, CONV_WIDTH, D), shard * (D // N_CHIPS), D // N_CHIPS, axis=2),
        conv_b=jnp.stack(sg["conv_b"]), w_rg=jnp.stack(sg["w_rg"]).reshape(w_rg.shape), b_rg=jnp.stack(sg["b_rg"]),
        w_ig=jnp.stack(sg["w_ig"]).reshape(w_ig.shape), b_ig=jnp.stack(sg["b_ig"]), lru_lambda=jnp.stack(sg["lru_lambda"]),
        sinks=jnp.stack(sg["sinks"])[:, :H], ln1_g=jnp.stack(sg["ln1_g"]), ln1_b=jnp.stack(sg["ln1_b"]),
        ln2_g=jnp.stack(sg["ln2_g"]), ln2_b=jnp.stack(sg["ln2_b"]),
    )

    for nm in order:
        if nm not in deltas:
            deltas[nm], new_m[nm], new_v[nm] = _adamw(weights[nm], grads[nm], ms[nm], vs[nm])
    return (loss, dx.reshape(B, S, D), *[grads[nm] for nm in order], *[deltas[nm] for nm in order],
            *[new_m[nm] for nm in order], *[new_v[nm] for nm in order])
```
